```python
import jax, jax.numpy as jnp
from jax import lax
import numpy as np

D_MODEL = 1024
BATCH = 8
SEQ = 16384
DEPTH = 1

N_META = 16
CHUNK = 64
EPS = 1e-6
M_HEADS = 4
M_DV = D_MODEL // M_HEADS
M_DQK = M_DV // 2
M_QK = M_HEADS * M_DQK
M_V = M_HEADS * M_DV
CONV_W = 4
F_BIAS = 3.0
G_HEADS = 4
G_DV = D_MODEL // G_HEADS
G_DK = G_DV // 2
G_QK = G_HEADS * G_DK
G_V = G_HEADS * G_DV
G_RANK = 16
G_TAU = 16.0
D_FF = ((8 * D_MODEL + 3 * 256 - 1) // (3 * 256)) * 256
PROJ_WIDTHS = (M_QK, M_QK, M_V, M_HEADS, M_HEADS, M_V, G_QK, G_QK, G_V, G_RANK, G_V, D_MODEL, D_MODEL)
N_PROJ = sum(PROJ_WIDTHS)

kernel_name = 'hybrid_mlstm_gla_block'


def rmsnorm(x, g):
    xf = x.astype(jnp.float32)
    y = xf * lax.rsqrt(jnp.mean(xf * xf, axis=-1, keepdims=True) + EPS)
    return (y * g.astype(jnp.float32)).astype(x.dtype)


def head_rmsnorm(h, g):
    y = h * lax.rsqrt(jnp.mean(h * h, axis=-1, keepdims=True) + EPS)
    return y * g.astype(jnp.float32)


def split_cols(p):
    idx = np.cumsum(np.array(PROJ_WIDTHS))[:-1]
    return jnp.split(p, idx, axis=-1)


def to_chunks(t, n_heads):
    b, tp = t.shape[:2]
    t = t.reshape(b, tp // CHUNK, CHUNK, n_heads, -1)
    return jnp.transpose(t, (0, 3, 1, 2, 4)).astype(jnp.float32)


def from_chunks(t):
    b, h, nc, l, d = t.shape
    return jnp.transpose(t, (0, 2, 3, 1, 4)).reshape(b, nc * l, h, d)


def causal_depthwise_conv(x, w, bias):
    k = w.shape[0]
    y = lax.conv_general_dilated(x, w[:, None, :].astype(x.dtype), window_strides=(1,),
                                 padding=[(k - 1, 0)], dimension_numbers=('NWC', 'WIO', 'NWC'),
                                 feature_group_count=x.shape[-1])
    return y + bias.astype(x.dtype)


def mlstm_chunkwise(q, k, v, logi, logf):
    L = q.shape[3]
    b = jnp.cumsum(logf, axis=-1)
    g = b[..., -1]
    causal = jnp.tril(jnp.ones((L, L), dtype=bool))
    dmat = jnp.where(causal, b[..., :, None] - b[..., None, :] + logi[..., None, :], -jnp.inf)
    wlog = g[..., None] - b + logi

    def step(carry, inp):
        C, n, m = carry
        kc, vc, wc, gc = inp
        m_new = jnp.maximum(gc + m, jnp.max(wc, axis=-1))
        a = jnp.exp(gc + m - m_new)
        w = jnp.exp(wc - m_new[..., None])
        C_new = a[..., None, None] * C + jnp.einsum('bhl,bhlv,bhlk->bhvk', w, vc, kc)
        n_new = a[..., None] * n + jnp.einsum('bhl,bhlk->bhk', w, kc)
        return (C_new, n_new, m_new), (C, n, m)

    bsz, nh, _, _, dqk = q.shape
    dv = v.shape[-1]
    init = (jnp.zeros((bsz, nh, dv, dqk), jnp.float32), jnp.zeros((bsz, nh, dqk), jnp.float32),
            jnp.zeros((bsz, nh), jnp.float32))
    xs = (jnp.moveaxis(k, 2, 0), jnp.moveaxis(v, 2, 0), jnp.moveaxis(wlog, 2, 0), jnp.moveaxis(g, 2, 0))
    _, (Cs, ns, ms) = lax.scan(step, init, xs)
    Cs = jnp.moveaxis(Cs, 0, 2)
    ns = jnp.moveaxis(ns, 0, 2)
    ms = jnp.moveaxis(ms, 0, 2)

    inter_log = b + ms[..., None]
    m_row = jnp.maximum(inter_log, jnp.max(dmat, axis=-1))
    sim = jnp.einsum('bhcjd,bhcsd->bhcjs', q, k)
    wts = jnp.exp(dmat - m_row[..., None]) * sim
    a_inter = jnp.exp(inter_log - m_row)
    num = (a_inter[..., None] * jnp.einsum('bhcvd,bhcjd->bhcjv', Cs, q)
           + jnp.einsum('bhcjs,bhcsv->bhcjv', wts, v))
    den = a_inter * jnp.einsum('bhcd,bhcjd->bhcj', ns, q) + jnp.sum(wts, axis=-1)
    return num / jnp.maximum(jnp.abs(den), jnp.exp(-m_row))[..., None]


def gla_chunked(q, k, v, loga):
    L = q.shape[3]
    bc = jnp.cumsum(loga, axis=3)
    btot = bc[..., -1, :]
    q_dec = q * jnp.exp(bc)
    k_inv = k * jnp.exp(-bc)
    k_end = k * jnp.exp(btot[..., None, :] - bc)
    causal = jnp.tril(jnp.ones((L, L), dtype=bool))
    att = jnp.where(causal, jnp.einsum('bhcjd,bhcsd->bhcjs', q_dec, k_inv), 0.0)
    intra = jnp.einsum('bhcjs,bhcsv->bhcjv', att, v)

    def step(S, inp):
        ke, vc, bt = inp
        S_new = jnp.exp(bt)[..., None] * S + jnp.einsum('bhlk,bhlv->bhkv', ke, vc)
        return S_new, S

    bsz, nh, _, _, dk = q.shape
    dv = v.shape[-1]
    S0 = jnp.zeros((bsz, nh, dk, dv), jnp.float32)
    _, S_prev = lax.scan(step, S0, (jnp.moveaxis(k_end, 2, 0), jnp.moveaxis(v, 2, 0), jnp.moveaxis(btot, 2, 0)))
    S_prev = jnp.moveaxis(S_prev, 0, 2)
    return intra + jnp.einsum('bhcjk,bhckv->bhcjv', q_dec, S_prev)


def _fwd_setup_inputs(seed: int = 0) -> dict:
    key = jax.random.key(seed)
    ks = jax.random.split(key, 24)
    f32 = jnp.float32
    nrm = lambda k, shape, s: jax.random.normal(k, shape, f32) * s
    m_gate_b = jnp.stack([nrm(ks[6], (DEPTH, M_HEADS), 0.01),
                          F_BIAS + nrm(ks[7], (DEPTH, M_HEADS), 0.1)], axis=1)
    return {
        'x': nrm(ks[0], (BATCH, SEQ, D_MODEL), 1.0),
        'meta_tokens': nrm(ks[1], (N_META, D_MODEL), 1.0),
        'norm1_g': 1.0 + nrm(ks[2], (DEPTH, D_MODEL), 0.02),
        'w_in': nrm(ks[3], (DEPTH, D_MODEL, N_PROJ), D_MODEL ** -0.5),
        'conv_w': nrm(ks[4], (DEPTH, CONV_W, 2 * M_QK), CONV_W ** -0.5),
        'conv_b': nrm(ks[5], (DEPTH, 2 * M_QK), 0.01),
        'm_gate_b': m_gate_b,
        'g_a2': nrm(ks[8], (DEPTH, G_RANK, G_QK), G_RANK ** -0.5),
        'g_a2_b': nrm(ks[9], (DEPTH, G_QK), 0.01),
        'm_head_g': 1.0 + nrm(ks[10], (DEPTH, M_HEADS, M_DV), 0.02),
        'g_head_g': 1.0 + nrm(ks[11], (DEPTH, G_HEADS, G_DV), 0.02),
        'w_branch_m': nrm(ks[12], (DEPTH, M_V, D_MODEL), M_V ** -0.5),
        'w_branch_g': nrm(ks[13], (DEPTH, G_V, D_MODEL), G_V ** -0.5),
        'w_out': nrm(ks[14], (DEPTH, D_MODEL, D_MODEL), D_MODEL ** -0.5),
        'norm2_g': 1.0 + nrm(ks[15], (DEPTH, D_MODEL), 0.02),
        'w_ff_gate': nrm(ks[16], (DEPTH, D_MODEL, D_FF), D_MODEL ** -0.5),
        'w_ff_up': nrm(ks[17], (DEPTH, D_MODEL, D_FF), D_MODEL ** -0.5),
        'w_ff_down': nrm(ks[18], (DEPTH, D_FF, D_MODEL), D_FF ** -0.5),
        'final_g': 1.0 + nrm(ks[19], (D_MODEL,), 0.02),
    }


def _fwd_reference(x, meta_tokens, norm1_g, w_in, conv_w, conv_b, m_gate_b, g_a2, g_a2_b, m_head_g, g_head_g,
              w_branch_m, w_branch_g, w_out, norm2_g, w_ff_gate, w_ff_up, w_ff_down, final_g):
    f32 = jnp.float32
    bsz, _, d = x.shape
    dt = x.dtype
    n_pad = CHUNK - N_META
    meta = jnp.broadcast_to(meta_tokens.astype(dt)[None], (bsz, N_META, d))
    h = jnp.concatenate([jnp.zeros((bsz, n_pad, d), dt), meta, x], axis=1)
    tp = h.shape[1]
    valid = (jnp.arange(tp) >= n_pad)[None, :, None]

    for l in range(DEPTH):
        xn = rmsnorm(h, norm1_g[l])
        proj = jnp.where(valid, xn @ w_in[l].astype(dt), 0.0).astype(dt)
        (mq, mk, mv, mi, mf, mo, gq, gk, gv, ga, gr, gate_m, gate_g) = split_cols(proj)

        mqk = jax.nn.silu(causal_depthwise_conv(jnp.concatenate([mq, mk], axis=-1), conv_w[l], conv_b[l]))
        mq, mk = jnp.split(mqk, 2, axis=-1)
        logi = jnp.where(valid, mi.astype(f32) + m_gate_b[l, 0], -jnp.inf)
        logf = jnp.where(valid, jax.nn.log_sigmoid(mf.astype(f32) + m_gate_b[l, 1]), 0.0)
        hm = mlstm_chunkwise(to_chunks(mq, M_HEADS) * (M_DQK ** -0.5), to_chunks(mk, M_HEADS),
                             to_chunks(mv, M_HEADS), to_chunks(logi, M_HEADS)[..., 0],
                             to_chunks(logf, M_HEADS)[..., 0])
        hm = head_rmsnorm(from_chunks(hm), m_head_g[l]) * jax.nn.sigmoid(mo.astype(f32)).reshape(bsz, tp, M_HEADS, M_DV)
        y_m = hm.reshape(bsz, tp, M_V).astype(dt)

        za = ga @ g_a2[l].astype(dt) + g_a2_b[l].astype(dt)
        loga = jnp.where(valid, jax.nn.log_sigmoid(za.astype(f32)) / G_TAU, 0.0)
        hg = gla_chunked(to_chunks(gq, G_HEADS) * (G_DK ** -0.5), to_chunks(gk, G_HEADS),
                         to_chunks(gv, G_HEADS), to_chunks(loga, G_HEADS))
        hg = head_rmsnorm(from_chunks(hg), g_head_g[l]) * jax.nn.silu(gr.astype(f32)).reshape(bsz, tp, G_HEADS, G_DV)
        y_g = hg.reshape(bsz, tp, G_V).astype(dt)

        merged = (jax.nn.sigmoid(gate_m) * (y_m @ w_branch_m[l].astype(dt))
                  + jax.nn.sigmoid(gate_g) * (y_g @ w_branch_g[l].astype(dt)))
        h = h + merged @ w_out[l].astype(dt)

        hn = rmsnorm(h, norm2_g[l])
        ff = jax.nn.silu(hn @ w_ff_gate[l].astype(dt)) * (hn @ w_ff_up[l].astype(dt))
        h = h + ff @ w_ff_down[l].astype(dt)

    out = rmsnorm(h, final_g)
    return out[:, CHUNK:, :]


import jax as _jax
import jax.numpy as _jnp

TWIN_FORMAT = 'train_step'
FWD_PARAMS = ['x', 'meta_tokens', 'norm1_g', 'w_in', 'conv_w', 'conv_b', 'm_gate_b', 'g_a2', 'g_a2_b', 'm_head_g', 'g_head_g', 'w_branch_m', 'w_branch_g', 'w_out', 'norm2_g', 'w_ff_gate', 'w_ff_up', 'w_ff_down', 'final_g']
TWIN_WEIGHTS = ['meta_tokens', 'norm1_g', 'w_in', 'conv_w', 'conv_b', 'm_gate_b', 'g_a2', 'g_a2_b', 'm_head_g', 'g_head_g', 'w_branch_m', 'w_branch_g', 'w_out', 'norm2_g', 'w_ff_gate', 'w_ff_up', 'w_ff_down', 'final_g']
TWIN_DIFF_INPUT = 'x'
TWIN_INPUTS = ['x', 'meta_tokens', 'norm1_g', 'w_in', 'conv_w', 'conv_b', 'm_gate_b', 'g_a2', 'g_a2_b', 'm_head_g', 'g_head_g', 'w_branch_m', 'w_branch_g', 'w_out', 'norm2_g', 'w_ff_gate', 'w_ff_up', 'w_ff_down', 'final_g', 'loss_target', 'm_meta_tokens', 'm_norm1_g', 'm_w_in', 'm_conv_w', 'm_conv_b', 'm_m_gate_b', 'm_g_a2', 'm_g_a2_b', 'm_m_head_g', 'm_g_head_g', 'm_w_branch_m', 'm_w_branch_g', 'm_w_out', 'm_norm2_g', 'm_w_ff_gate', 'm_w_ff_up', 'm_w_ff_down', 'm_final_g', 'v_meta_tokens', 'v_norm1_g', 'v_w_in', 'v_conv_w', 'v_conv_b', 'v_m_gate_b', 'v_g_a2', 'v_g_a2_b', 'v_m_head_g', 'v_g_head_g', 'v_w_branch_m', 'v_w_branch_g', 'v_w_out', 'v_norm2_g', 'v_w_ff_gate', 'v_w_ff_up', 'v_w_ff_down', 'v_final_g']
TWIN_OUTPUTS = ['loss', 'grad_x', 'grad_meta_tokens', 'grad_norm1_g', 'grad_w_in', 'grad_conv_w', 'grad_conv_b', 'grad_m_gate_b', 'grad_g_a2', 'grad_g_a2_b', 'grad_m_head_g', 'grad_g_head_g', 'grad_w_branch_m', 'grad_w_branch_g', 'grad_w_out', 'grad_norm2_g', 'grad_w_ff_gate', 'grad_w_ff_up', 'grad_w_ff_down', 'grad_final_g', 'delta_meta_tokens', 'delta_norm1_g', 'delta_w_in', 'delta_conv_w', 'delta_conv_b', 'delta_m_gate_b', 'delta_g_a2', 'delta_g_a2_b', 'delta_m_head_g', 'delta_g_head_g', 'delta_w_branch_m', 'delta_w_branch_g', 'delta_w_out', 'delta_norm2_g', 'delta_w_ff_gate', 'delta_w_ff_up', 'delta_w_ff_down', 'delta_final_g', 'new_m_meta_tokens', 'new_m_norm1_g', 'new_m_w_in', 'new_m_conv_w', 'new_m_conv_b', 'new_m_m_gate_b', 'new_m_g_a2', 'new_m_g_a2_b', 'new_m_m_head_g', 'new_m_g_head_g', 'new_m_w_branch_m', 'new_m_w_branch_g', 'new_m_w_out', 'new_m_norm2_g', 'new_m_w_ff_gate', 'new_m_w_ff_up', 'new_m_w_ff_down', 'new_m_final_g', 'new_v_meta_tokens', 'new_v_norm1_g', 'new_v_w_in', 'new_v_conv_w', 'new_v_conv_b', 'new_v_m_gate_b', 'new_v_g_a2', 'new_v_g_a2_b', 'new_v_m_head_g', 'new_v_g_head_g', 'new_v_w_branch_m', 'new_v_w_branch_g', 'new_v_w_out', 'new_v_norm2_g', 'new_v_w_ff_gate', 'new_v_w_ff_up', 'new_v_w_ff_down', 'new_v_final_g']
TWIN_LEAF_KINDS = {'loss': 'loss', 'grad_x': 'grad_x', 'grad_meta_tokens': 'grad_w', 'grad_norm1_g': 'grad_w', 'grad_w_in': 'grad_w', 'grad_conv_w': 'grad_w', 'grad_conv_b': 'grad_w', 'grad_m_gate_b': 'grad_w', 'grad_g_a2': 'grad_w', 'grad_g_a2_b': 'grad_w', 'grad_m_head_g': 'grad_w', 'grad_g_head_g': 'grad_w', 'grad_w_branch_m': 'grad_w', 'grad_w_branch_g': 'grad_w', 'grad_w_out': 'grad_w', 'grad_norm2_g': 'grad_w', 'grad_w_ff_gate': 'grad_w', 'grad_w_ff_up': 'grad_w', 'grad_w_ff_down': 'grad_w', 'grad_final_g': 'grad_w', 'delta_meta_tokens': 'delta_w', 'delta_norm1_g': 'delta_w', 'delta_w_in': 'delta_w', 'delta_conv_w': 'delta_w', 'delta_conv_b': 'delta_w', 'delta_m_gate_b': 'delta_w', 'delta_g_a2': 'delta_w', 'delta_g_a2_b': 'delta_w', 'delta_m_head_g': 'delta_w', 'delta_g_head_g': 'delta_w', 'delta_w_branch_m': 'delta_w', 'delta_w_branch_g': 'delta_w', 'delta_w_out': 'delta_w', 'delta_norm2_g': 'delta_w', 'delta_w_ff_gate': 'delta_w', 'delta_w_ff_up': 'delta_w', 'delta_w_ff_down': 'delta_w', 'delta_final_g': 'delta_w', 'new_m_meta_tokens': 'new_m', 'new_m_norm1_g': 'new_m', 'new_m_w_in': 'new_m', 'new_m_conv_w': 'new_m', 'new_m_conv_b': 'new_m', 'new_m_m_gate_b': 'new_m', 'new_m_g_a2': 'new_m', 'new_m_g_a2_b': 'new_m', 'new_m_m_head_g': 'new_m', 'new_m_g_head_g': 'new_m', 'new_m_w_branch_m': 'new_m', 'new_m_w_branch_g': 'new_m', 'new_m_w_out': 'new_m', 'new_m_norm2_g': 'new_m', 'new_m_w_ff_gate': 'new_m', 'new_m_w_ff_up': 'new_m', 'new_m_w_ff_down': 'new_m', 'new_m_final_g': 'new_m', 'new_v_meta_tokens': 'new_v', 'new_v_norm1_g': 'new_v', 'new_v_w_in': 'new_v', 'new_v_conv_w': 'new_v', 'new_v_conv_b': 'new_v', 'new_v_m_gate_b': 'new_v', 'new_v_g_a2': 'new_v', 'new_v_g_a2_b': 'new_v', 'new_v_m_head_g': 'new_v', 'new_v_g_head_g': 'new_v', 'new_v_w_branch_m': 'new_v', 'new_v_w_branch_g': 'new_v', 'new_v_w_out': 'new_v', 'new_v_norm2_g': 'new_v', 'new_v_w_ff_gate': 'new_v', 'new_v_w_ff_up': 'new_v', 'new_v_w_ff_down': 'new_v', 'new_v_final_g': 'new_v'}


def _forward(args):
    return _fwd_reference(*[args[k] for k in FWD_PARAMS])


def _output_shape():
    def fwd():
        inp = _fwd_setup_inputs(0)
        return _fwd_reference(*[inp[k] for k in FWD_PARAMS])
    out = _jax.eval_shape(fwd)
    return out.shape, out.dtype

N_MICROBATCH = 1
ADAM_LR = 0.001
ADAM_B1 = 0.9
ADAM_B2 = 0.999
ADAM_EPS = 1e-08
ADAM_WD = 0.01
ADAM_STEP = 10
PER_EXAMPLE_BATCH_AXIS = {'x': 0, 'loss_target': 0}
SHARED_INPUTS = []
_WEIGHT_DTYPES = {'meta_tokens': _jnp.float32, 'norm1_g': _jnp.float32, 'w_in': _jnp.float32, 'conv_w': _jnp.float32, 'conv_b': _jnp.float32, 'm_gate_b': _jnp.float32, 'g_a2': _jnp.float32, 'g_a2_b': _jnp.float32, 'm_head_g': _jnp.float32, 'g_head_g': _jnp.float32, 'w_branch_m': _jnp.float32, 'w_branch_g': _jnp.float32, 'w_out': _jnp.float32, 'norm2_g': _jnp.float32, 'w_ff_gate': _jnp.float32, 'w_ff_up': _jnp.float32, 'w_ff_down': _jnp.float32, 'final_g': _jnp.float32}
MOMENT_SCALE = {'meta_tokens': 9.936508e-03, 'norm1_g': 3.355007e-01, 'w_in': 1.084151e-01, 'conv_w': 1.140406e-01, 'conv_b': 1.027269e-01, 'm_gate_b': 1.264881e+00, 'g_a2': 1.960548e-02, 'g_a2_b': 7.557260e-02, 'm_head_g': 1.118965e-01, 'g_head_g': 1.343668e-01, 'w_branch_m': 1.063800e-01, 'w_branch_g': 1.169691e-01, 'w_out': 1.584804e-01, 'norm2_g': 2.542604e-01, 'w_ff_gate': 1.056372e-01, 'w_ff_up': 1.027701e-01, 'w_ff_down': 1.698009e-01, 'final_g': 1.277248e+02}


def _to_microbatches(a, axis):
    t = _jnp.moveaxis(a, axis, 0)
    t = t.reshape((N_MICROBATCH, t.shape[0] // N_MICROBATCH) + t.shape[1:])
    return _jnp.moveaxis(t, 1, axis + 1)


def setup_inputs(seed: int = 0) -> dict:
    inp = _fwd_setup_inputs(seed)
    key = _jax.random.fold_in(_jax.random.key(seed), 7919)
    shape, _ = _output_shape()
    out = dict(inp)
    out["loss_target"] = _jax.random.normal(_jax.random.fold_in(key, 0), shape, _jnp.float32)
    for i, name in enumerate(TWIN_WEIGHTS):
        w = inp[name].astype(_jnp.float32)
        if MOMENT_SCALE is None:
            s = _jnp.sqrt(_jnp.mean(_jnp.square(w)) + 1e-30)
        else:
            s = MOMENT_SCALE[name]
        km, kv = _jax.random.split(_jax.random.fold_in(key, i + 1))
        out[name] = w
        out["m_" + name] = s * _jax.random.normal(km, w.shape, _jnp.float32)
        out["v_" + name] = (s * s) * _jax.random.uniform(kv, w.shape, _jnp.float32, 0.5, 1.5)
    if N_MICROBATCH > 1:
        for name, axis in PER_EXAMPLE_BATCH_AXIS.items():
            out[name] = _to_microbatches(out[name], axis)
    return {'x': out['x'], 'meta_tokens': out['meta_tokens'], 'norm1_g': out['norm1_g'], 'w_in': out['w_in'], 'conv_w': out['conv_w'], 'conv_b': out['conv_b'], 'm_gate_b': out['m_gate_b'], 'g_a2': out['g_a2'], 'g_a2_b': out['g_a2_b'], 'm_head_g': out['m_head_g'], 'g_head_g': out['g_head_g'], 'w_branch_m': out['w_branch_m'], 'w_branch_g': out['w_branch_g'], 'w_out': out['w_out'], 'norm2_g': out['norm2_g'], 'w_ff_gate': out['w_ff_gate'], 'w_ff_up': out['w_ff_up'], 'w_ff_down': out['w_ff_down'], 'final_g': out['final_g'], 'loss_target': out['loss_target'], 'm_meta_tokens': out['m_meta_tokens'], 'm_norm1_g': out['m_norm1_g'], 'm_w_in': out['m_w_in'], 'm_conv_w': out['m_conv_w'], 'm_conv_b': out['m_conv_b'], 'm_m_gate_b': out['m_m_gate_b'], 'm_g_a2': out['m_g_a2'], 'm_g_a2_b': out['m_g_a2_b'], 'm_m_head_g': out['m_m_head_g'], 'm_g_head_g': out['m_g_head_g'], 'm_w_branch_m': out['m_w_branch_m'], 'm_w_branch_g': out['m_w_branch_g'], 'm_w_out': out['m_w_out'], 'm_norm2_g': out['m_norm2_g'], 'm_w_ff_gate': out['m_w_ff_gate'], 'm_w_ff_up': out['m_w_ff_up'], 'm_w_ff_down': out['m_w_ff_down'], 'm_final_g': out['m_final_g'], 'v_meta_tokens': out['v_meta_tokens'], 'v_norm1_g': out['v_norm1_g'], 'v_w_in': out['v_w_in'], 'v_conv_w': out['v_conv_w'], 'v_conv_b': out['v_conv_b'], 'v_m_gate_b': out['v_m_gate_b'], 'v_g_a2': out['v_g_a2'], 'v_g_a2_b': out['v_g_a2_b'], 'v_m_head_g': out['v_m_head_g'], 'v_g_head_g': out['v_g_head_g'], 'v_w_branch_m': out['v_w_branch_m'], 'v_w_branch_g': out['v_w_branch_g'], 'v_w_out': out['v_w_out'], 'v_norm2_g': out['v_norm2_g'], 'v_w_ff_gate': out['v_w_ff_gate'], 'v_w_ff_up': out['v_w_ff_up'], 'v_w_ff_down': out['v_w_ff_down'], 'v_final_g': out['v_final_g']}


def _loss(weights, diff, rest, loss_target):
    with _jax.named_scope("forward"):
        args = {**rest, TWIN_DIFF_INPUT: diff, **{k: w.astype(_WEIGHT_DTYPES[k]) for k, w in weights.items()}}
        y = _forward(args)
    with _jax.named_scope("loss_head"):
        err = _jnp.square(y.astype(_jnp.float32) - loss_target)
        return 0.5 * _jnp.sum(_jnp.mean(err, axis=-1)) if err.ndim else 0.5 * err


def _adamw(w, g, m, v):
    m = ADAM_B1 * m + (1.0 - ADAM_B1) * g
    v = ADAM_B2 * v + (1.0 - ADAM_B2) * _jnp.square(g)
    m_hat = m / (1.0 - ADAM_B1 ** ADAM_STEP)
    v_hat = v / (1.0 - ADAM_B2 ** ADAM_STEP)
    delta = -ADAM_LR * (m_hat / (_jnp.sqrt(v_hat) + ADAM_EPS) + ADAM_WD * w)
    return delta, m, v


def reference(x, meta_tokens, norm1_g, w_in, conv_w, conv_b, m_gate_b, g_a2, g_a2_b, m_head_g, g_head_g, w_branch_m, w_branch_g, w_out, norm2_g, w_ff_gate, w_ff_up, w_ff_down, final_g, loss_target, m_meta_tokens, m_norm1_g, m_w_in, m_conv_w, m_conv_b, m_m_gate_b, m_g_a2, m_g_a2_b, m_m_head_g, m_g_head_g, m_w_branch_m, m_w_branch_g, m_w_out, m_norm2_g, m_w_ff_gate, m_w_ff_up, m_w_ff_down, m_final_g, v_meta_tokens, v_norm1_g, v_w_in, v_conv_w, v_conv_b, v_m_gate_b, v_g_a2, v_g_a2_b, v_m_head_g, v_g_head_g, v_w_branch_m, v_w_branch_g, v_w_out, v_norm2_g, v_w_ff_gate, v_w_ff_up, v_w_ff_down, v_final_g):
    given = dict(x=x, meta_tokens=meta_tokens, norm1_g=norm1_g, w_in=w_in, conv_w=conv_w, conv_b=conv_b, m_gate_b=m_gate_b, g_a2=g_a2, g_a2_b=g_a2_b, m_head_g=m_head_g, g_head_g=g_head_g, w_branch_m=w_branch_m, w_branch_g=w_branch_g, w_out=w_out, norm2_g=norm2_g, w_ff_gate=w_ff_gate, w_ff_up=w_ff_up, w_ff_down=w_ff_down, final_g=final_g, loss_target=loss_target, m_meta_tokens=m_meta_tokens, m_norm1_g=m_norm1_g, m_w_in=m_w_in, m_conv_w=m_conv_w, m_conv_b=m_conv_b, m_m_gate_b=m_m_gate_b, m_g_a2=m_g_a2, m_g_a2_b=m_g_a2_b, m_m_head_g=m_m_head_g, m_g_head_g=m_g_head_g, m_w_branch_m=m_w_branch_m, m_w_branch_g=m_w_branch_g, m_w_out=m_w_out, m_norm2_g=m_norm2_g, m_w_ff_gate=m_w_ff_gate, m_w_ff_up=m_w_ff_up, m_w_ff_down=m_w_ff_down, m_final_g=m_final_g, v_meta_tokens=v_meta_tokens, v_norm1_g=v_norm1_g, v_w_in=v_w_in, v_conv_w=v_conv_w, v_conv_b=v_conv_b, v_m_gate_b=v_m_gate_b, v_g_a2=v_g_a2, v_g_a2_b=v_g_a2_b, v_m_head_g=v_m_head_g, v_g_head_g=v_g_head_g, v_w_branch_m=v_w_branch_m, v_w_branch_g=v_w_branch_g, v_w_out=v_w_out, v_norm2_g=v_norm2_g, v_w_ff_gate=v_w_ff_gate, v_w_ff_up=v_w_ff_up, v_w_ff_down=v_w_ff_down, v_final_g=v_final_g)
    weights = {n: given[n] for n in TWIN_WEIGHTS}
    shared = {n: given[n] for n in SHARED_INPUTS}
    per_example = {n: given[n] for n in ['x']}
    grad_fn = _jax.value_and_grad(_loss, argnums=(0, 1))

    def one_microbatch(ex, loss_target):
        ex = dict(ex)
        diff = ex.pop(TWIN_DIFF_INPUT)
        return grad_fn(weights, diff, {**shared, **ex}, loss_target)

    if N_MICROBATCH == 1:
        loss, (grad_w, grad_x) = one_microbatch(per_example, given["loss_target"])
    else:
        def body(carry, xs):
            loss_sum, grad_sum = carry
            l_k, (gw_k, gx_k) = one_microbatch(xs[0], xs[1])
            with _jax.named_scope("update"):
                return (loss_sum + l_k, _jax.tree.map(_jnp.add, grad_sum, gw_k)), gx_k

        init = (_jnp.zeros((), _jnp.float32), _jax.tree.map(_jnp.zeros_like, weights))
        (loss, grad_w), grad_x = _jax.lax.scan(body, init, (per_example, given["loss_target"]))
    with _jax.named_scope("update"):
        delta_w, new_m, new_v = {}, {}, {}
        for n in TWIN_WEIGHTS:
            delta_w[n], new_m[n], new_v[n] = _adamw(weights[n], grad_w[n], given["m_" + n], given["v_" + n])
    return (loss, grad_x, *[grad_w[n] for n in TWIN_WEIGHTS], *[delta_w[n] for n in TWIN_WEIGHTS],
            *[new_m[n] for n in TWIN_WEIGHTS], *[new_v[n] for n in TWIN_WEIGHTS])
```

```python
import functools

import jax
import jax.numpy as jnp
from jax import lax
from jax.experimental import pallas as pl
from jax.experimental.pallas import tpu as pltpu

F32 = jnp.float32
MXU = jnp.bfloat16
WIRE = jnp.bfloat16

D = 1024
NH = 4
DV = 256
DQK = 128
L = 64
NMETA = 16
PADR = 512
CH0 = PADR // L - 1
NPADROWS = PADR - NMETA
RANK = 16
DFF = 2816
EPS = 1e-6
TAU = 16.0
QSCALE = DQK ** -0.5
NEG = -1e30
NDEV = 8

MV0, MO0, QK0, GQ0, GK0, GV0, GR0, GM0, GG0, SM0 = 0, 1024, 2048, 3072, 3584, 4096, 5120, 6144, 7168, 8192
NP = 8320
NPROJ = 8216

ADAM_LR, ADAM_B1, ADAM_B2, ADAM_EPS, ADAM_WD, ADAM_STEP = 0.001, 0.9, 0.999, 1e-08, 0.01, 10

VMEM_LIMIT = 56 * 1024 * 1024
TM = 512


def _cp(sem):
    return pltpu.CompilerParams(dimension_semantics=sem, vmem_limit_bytes=VMEM_LIMIT)


def _sigmoid(x):
    return 1.0 / (1.0 + jnp.exp(-x))


def _log_sigmoid(x):
    return jnp.minimum(x, 0.0) - jnp.log1p(jnp.exp(-jnp.abs(x)))


def _dot(a, b, ca, cb):
    return lax.dot_general(a.astype(MXU), b.astype(MXU), (((ca,), (cb,)), ((), ())), preferred_element_type=F32)


def _dot_exact(a, b):
    return lax.dot_general(a, b, (((1,), (0,)), ((), ())), precision=lax.Precision.HIGHEST,
                           preferred_element_type=F32)


def _rb(tm, w, cb):
    return pl.BlockSpec((tm, w), lambda i: (i, cb))


def _const(shape):
    nd = len(shape)
    return pl.BlockSpec(shape, lambda i: (0,) * nd)


def _pick(n, target):
    if n <= target:
        return n
    best = None
    for t in range(128, target + 1, 128):
        if n % t == 0:
            best = t
    assert best is not None, (n, target)
    return best


def matmul(a, b, mode, name, add=None, out_dtype=F32, tm=512, tn=1664, tk=1024):
    if mode == "nn":
        (M, K), (K2, N) = a.shape, b.shape
    elif mode == "nt":
        (M, K), (N, K2) = a.shape, b.shape
    else:
        (K, M), (K2, N) = a.shape, b.shape
    assert K == K2, (a.shape, b.shape, mode)
    tm, tn, tk = _pick(M, tm), _pick(N, tn), _pick(K, tk)
    nk = K // tk
    ca, cb = {"nn": (1, 0), "nt": (1, 1), "tn": (0, 0)}[mode]
    a_spec = {"nn": pl.BlockSpec((tm, tk), lambda i, j, k: (i, k)),
              "nt": pl.BlockSpec((tm, tk), lambda i, j, k: (i, k)),
              "tn": pl.BlockSpec((tk, tm), lambda i, j, k: (k, i))}[mode]
    b_spec = {"nn": pl.BlockSpec((tk, tn), lambda i, j, k: (k, j)),
              "nt": pl.BlockSpec((tn, tk), lambda i, j, k: (j, k)),
              "tn": pl.BlockSpec((tk, tn), lambda i, j, k: (k, j))}[mode]
    o_spec = pl.BlockSpec((tm, tn), lambda i, j, k: (i, j))
    has_add = add is not None

    def body(*refs):
        if has_add:
            a_ref, b_ref, add_ref, o_ref, acc_ref = refs
        else:
            a_ref, b_ref, o_ref, acc_ref = refs
            add_ref = None
        k = pl.program_id(2)
        part = _dot(a_ref[...], b_ref[...], ca, cb)

        @pl.when(k == 0)
        def _():
            acc_ref[...] = part

        @pl.when(k > 0)
        def _():
            acc_ref[...] += part

        @pl.when(k == nk - 1)
        def _():
            res = acc_ref[...]
            if has_add:
                res = res + add_ref[...]
            o_ref[...] = res.astype(o_ref.dtype)

    in_specs = [a_spec, b_spec] + ([o_spec] if has_add else [])
    args = (a, b) + ((add,) if has_add else ())
    return pl.pallas_call(
        body, name=name, grid=(M // tm, N // tn, nk),
        in_specs=in_specs, out_specs=o_spec,
        out_shape=jax.ShapeDtypeStruct((M, N), out_dtype),
        scratch_shapes=[pltpu.VMEM((tm, tn), F32)],
        compiler_params=_cp(("parallel", "parallel", "arbitrary")),
    )(*args)


def rms_fwd(x, g, name):
    R = x.shape[0]

    def body(x_ref, g_ref, y_ref):
        xv = x_ref[...]
        r = lax.rsqrt(jnp.mean(xv * xv, axis=-1, keepdims=True) + EPS)
        y_ref[...] = (xv * r * g_ref[...]).astype(y_ref.dtype)

    return pl.pallas_call(
        body, name=name, grid=(R // TM,),
        in_specs=[_rb(TM, D, 0), _const((1, D))], out_specs=_rb(TM, D, 0),
        out_shape=jax.ShapeDtypeStruct((R, D), MXU), compiler_params=_cp(("parallel",)),
    )(x, g)


def rms_bwd(dy, x, g, dres, name):
    R = x.shape[0]

    def body(dy_ref, x_ref, g_ref, dres_ref, dx_ref, dg_ref):
        i = pl.program_id(0)
        xv, dyv = x_ref[...], dy_ref[...]
        r = lax.rsqrt(jnp.mean(xv * xv, axis=-1, keepdims=True) + EPS)
        dyg = dyv * g_ref[...]
        dx_ref[...] = dres_ref[...] + r * dyg - xv * (r * r * r * jnp.mean(dyg * xv, axis=-1, keepdims=True))
        part = jnp.sum(dyv * xv * r, axis=0, keepdims=True)

        @pl.when(i == 0)
        def _():
            dg_ref[...] = part

        @pl.when(i > 0)
        def _():
            dg_ref[...] += part

    return pl.pallas_call(
        body, name=name, grid=(R // TM,),
        in_specs=[_rb(TM, D, 0), _rb(TM, D, 0), _const((1, D)), _rb(TM, D, 0)],
        out_specs=[_rb(TM, D, 0), _const((1, D))],
        out_shape=[jax.ShapeDtypeStruct((R, D), F32), jax.ShapeDtypeStruct((1, D), F32)],
        compiler_params=_cp(("arbitrary",)),
    )(dy, x, g, dres)


def _shift_down(cur, prev8, s):
    tm = cur.shape[0]
    rolled = pltpu.roll(cur, s, 0)
    rows8 = lax.broadcasted_iota(jnp.int32, (8, cur.shape[1]), 0)
    head = jnp.where(rows8 < s, pltpu.roll(prev8, s, 0), rolled[0:8])
    return jnp.concatenate([head, rolled[8:tm]], axis=0)


def _shift_up(cur, next8, s):
    tm = cur.shape[0]
    rolled = pltpu.roll(cur, tm - s, 0)
    rows8 = lax.broadcasted_iota(jnp.int32, (8, cur.shape[1]), 0)
    tail = jnp.where(rows8 >= 8 - s, pltpu.roll(next8, 8 - s, 0), rolled[tm - 8:tm])
    return jnp.concatenate([rolled[0:tm - 8], tail], axis=0)


def prep_fwd(proj, conv_w, conv_b, gb_row):
    R = proj.shape[0]
    t8 = TM // 8

    def body(x_ref, halo_ref, sm_ref, w_ref, b_ref, gb_ref, c_ref, qk_ref, gl_ref):
        i = pl.program_id(0)
        x = x_ref[...]
        halo = halo_ref[...]
        w = w_ref[...]
        c = x * w[3:4, :] + b_ref[...]
        for s in (1, 2, 3):
            c = c + _shift_down(x, halo, s) * w[3 - s:4 - s, :]
        c_ref[...] = c
        qk_ref[...] = c * _sigmoid(c)
        z = sm_ref[...] + gb_ref[...]
        lane = lax.broadcasted_iota(jnp.int32, z.shape, 1)
        row = lax.broadcasted_iota(jnp.int32, z.shape, 0) + i * TM
        valid = row >= NPADROWS
        logi = jnp.where(valid, z, NEG)
        logf = jnp.where(valid, _log_sigmoid(z), 0.0)
        gl_ref[...] = jnp.where(lane < 4, logi, jnp.where(lane < 8, logf, 0.0))

    return pl.pallas_call(
        body, name="prep_fwd", grid=(R // TM,),
        in_specs=[_rb(TM, 1024, QK0 // 1024),
                  pl.BlockSpec((8, 1024), lambda i: (jnp.maximum(i * t8 - 1, 0), QK0 // 1024)),
                  _rb(TM, 128, SM0 // 128), _const((4, 1024)), _const((1, 1024)), _const((1, 128))],
        out_specs=[_rb(TM, 1024, 0), _rb(TM, 1024, 0), _rb(TM, 128, 0)],
        out_shape=[jax.ShapeDtypeStruct((R, 1024), F32), jax.ShapeDtypeStruct((R, 1024), F32),
                   jax.ShapeDtypeStruct((R, 128), F32)],
        compiler_params=_cp(("parallel",)),
    )(proj, proj, proj, conv_w, conv_b, gb_row)


def merge_fwd(bm, bg, proj):
    R = bm.shape[0]

    def body(bm_ref, bg_ref, gm_ref, gg_ref, o_ref):
        o_ref[...] = (_sigmoid(gm_ref[...]) * bm_ref[...] + _sigmoid(gg_ref[...]) * bg_ref[...]).astype(o_ref.dtype)

    return pl.pallas_call(
        body, name="merge_fwd", grid=(R // TM,),
        in_specs=[_rb(TM, D, 0), _rb(TM, D, 0), _rb(TM, D, GM0 // D), _rb(TM, D, GG0 // D)],
        out_specs=_rb(TM, D, 0), out_shape=jax.ShapeDtypeStruct((R, D), MXU),
        compiler_params=_cp(("parallel",)),
    )(bm, bg, proj, proj)


def merge_bwd(dmerged, bm, bg, proj):
    R = bm.shape[0]

    def body(dm_ref, bm_ref, bg_ref, gm_ref, gg_ref, dbm_ref, dbg_ref, dp_ref):
        dm = dm_ref[...]
        sm, sg = _sigmoid(gm_ref[...]), _sigmoid(gg_ref[...])
        dbm_ref[...] = (dm * sm).astype(dbm_ref.dtype)
        dbg_ref[...] = (dm * sg).astype(dbg_ref.dtype)
        dp_ref[:, 0:D] = (dm * bm_ref[...] * sm * (1.0 - sm)).astype(dp_ref.dtype)
        dp_ref[:, D:2 * D] = (dm * bg_ref[...] * sg * (1.0 - sg)).astype(dp_ref.dtype)

    return pl.pallas_call(
        body, name="merge_bwd", grid=(R // TM,),
        in_specs=[_rb(TM, D, 0), _rb(TM, D, 0), _rb(TM, D, 0), _rb(TM, D, GM0 // D), _rb(TM, D, GG0 // D)],
        out_specs=[_rb(TM, D, 0), _rb(TM, D, 0), _rb(TM, 2 * D, GM0 // (2 * D))],
        out_shape=[jax.ShapeDtypeStruct((R, D), MXU), jax.ShapeDtypeStruct((R, D), MXU),
                   jax.ShapeDtypeStruct((R, NP), MXU)],
        compiler_params=_cp(("parallel",)),
    )(dmerged, bm, bg, proj, proj)


def swiglu_fwd(au):
    R = au.shape[0]
    tw = DFF // 2

    def body(a_ref, u_ref, o_ref):
        a = a_ref[...]
        o_ref[...] = (a * _sigmoid(a) * u_ref[...]).astype(o_ref.dtype)

    return pl.pallas_call(
        body, name="swiglu_fwd", grid=(R // TM, 2),
        in_specs=[pl.BlockSpec((TM, tw), lambda i, j: (i, j)), pl.BlockSpec((TM, tw), lambda i, j: (i, j + 2))],
        out_specs=pl.BlockSpec((TM, tw), lambda i, j: (i, j)),
        out_shape=jax.ShapeDtypeStruct((R, DFF), MXU), compiler_params=_cp(("parallel", "parallel")),
    )(au, au)


def swiglu_bwd(au, dff):
    R = au.shape[0]
    tw = DFF // 2

    def body(a_ref, u_ref, d_ref, da_ref, du_ref):
        a, d = a_ref[...], d_ref[...]
        s = _sigmoid(a)
        da_ref[...] = (d * u_ref[...] * s * (1.0 + a * (1.0 - s))).astype(da_ref.dtype)
        du_ref[...] = (d * a * s).astype(du_ref.dtype)

    return pl.pallas_call(
        body, name="swiglu_bwd", grid=(R // TM, 2),
        in_specs=[pl.BlockSpec((TM, tw), lambda i, j: (i, j)), pl.BlockSpec((TM, tw), lambda i, j: (i, j + 2)),
                  pl.BlockSpec((TM, tw), lambda i, j: (i, j))],
        out_specs=[pl.BlockSpec((TM, tw), lambda i, j: (i, j)), pl.BlockSpec((TM, tw), lambda i, j: (i, j))],
        out_shape=[jax.ShapeDtypeStruct((R, DFF), MXU), jax.ShapeDtypeStruct((R, DFF), MXU)],
        compiler_params=_cp(("parallel", "parallel")),
    )(au, au, dff)


def final_loss(h2, gf, target):
    R = h2.shape[0]
    assert PADR == TM

    def body(h_ref, g_ref, t_ref, dh_ref, loss_ref, dg_ref):
        i = pl.program_id(0)
        hv = h_ref[...]
        r = lax.rsqrt(jnp.mean(hv * hv, axis=-1, keepdims=True) + EPS)
        g = g_ref[...]
        live = (i >= 1).astype(F32)
        e = (hv * r * g - t_ref[...]) * live
        dy = e * (1.0 / D)
        dyg = dy * g
        dh_ref[...] = r * dyg - hv * (r * r * r * jnp.mean(dyg * hv, axis=-1, keepdims=True))
        lpart = jnp.zeros((1, 128), F32) + 0.5 * jnp.sum(jnp.sum(e * e, axis=1, keepdims=True), axis=0, keepdims=True) * (1.0 / D)
        gpart = jnp.sum(dy * hv * r, axis=0, keepdims=True)

        @pl.when(i == 0)
        def _():
            loss_ref[...] = lpart
            dg_ref[...] = gpart

        @pl.when(i > 0)
        def _():
            loss_ref[...] += lpart
            dg_ref[...] += gpart

    return pl.pallas_call(
        body, name="final_loss", grid=(R // TM,),
        in_specs=[_rb(TM, D, 0), _const((1, D)), pl.BlockSpec((TM, D), lambda i: (jnp.maximum(i - 1, 0), 0))],
        out_specs=[_rb(TM, D, 0), _const((1, 128)), _const((1, D))],
        out_shape=[jax.ShapeDtypeStruct((R, D), F32), jax.ShapeDtypeStruct((1, 128), F32),
                   jax.ShapeDtypeStruct((1, D), F32)],
        compiler_params=_cp(("arbitrary",)),
    )(h2, gf, target)


def conv_bwd(dc, proj, conv_w, dproj):
    R = dc.shape[0]
    t8 = TM // 8
    nt = R // TM

    def body(dc_ref, nxt_ref, x_ref, prv_ref, w_ref, dp_in, dp_ref, dw_ref):
        del dp_in
        i = pl.program_id(0)
        dcv = dc_ref[...]
        nxt = nxt_ref[...] * (i < nt - 1).astype(F32)
        x = x_ref[...]
        prv = prv_ref[...]
        w = w_ref[...]
        dx = dcv * w[3:4, :]
        rows = [None] * 4
        rows[3] = jnp.sum(dcv * x, axis=0, keepdims=True)
        for s in (1, 2, 3):
            dx = dx + _shift_up(dcv, nxt, s) * w[3 - s:4 - s, :]
            rows[3 - s] = jnp.sum(dcv * _shift_down(x, prv, s), axis=0, keepdims=True)
        dp_ref[...] = dx.astype(dp_ref.dtype)
        part = jnp.concatenate(rows + [jnp.sum(dcv, axis=0, keepdims=True), jnp.zeros((3, 1024), F32)], axis=0)

        @pl.when(i == 0)
        def _():
            dw_ref[...] = part

        @pl.when(i > 0)
        def _():
            dw_ref[...] += part

    return pl.pallas_call(
        body, name="conv_bwd", grid=(nt,),
        in_specs=[_rb(TM, 1024, 0),
                  pl.BlockSpec((8, 1024), lambda i: (jnp.minimum((i + 1) * t8, nt * t8 - 1), 0)),
                  _rb(TM, 1024, QK0 // 1024),
                  pl.BlockSpec((8, 1024), lambda i: (jnp.maximum(i * t8 - 1, 0), QK0 // 1024)),
                  _const((4, 1024)), pl.BlockSpec(memory_space=pl.ANY)],
        out_specs=[_rb(TM, 1024, QK0 // 1024), _const((8, 1024))],
        out_shape=[jax.ShapeDtypeStruct((R, NP), MXU), jax.ShapeDtypeStruct((8, 1024), F32)],
        input_output_aliases={5: 0},
        compiler_params=_cp(("arbitrary",)),
    )(dc, dc, proj, proj, conv_w, dproj)


def small_bwd(dgl, dga, proj, gb_row, dproj):
    R = dgl.shape[0]

    def body(dgl_ref, dga_ref, sm_ref, gb_ref, dp_in, dp_ref, dgb_ref):
        del dp_in
        i = pl.program_id(0)
        z = sm_ref[...] + gb_ref[...]
        lane = lax.broadcasted_iota(jnp.int32, z.shape, 1)
        row = lax.broadcasted_iota(jnp.int32, z.shape, 0) + i * TM
        valid = row >= NPADROWS
        dgl_v = dgl_ref[...]
        dgate = jnp.where(valid, jnp.where(lane < 4, dgl_v, dgl_v * _sigmoid(-z)), 0.0)
        ds = jnp.where(lane < 8, dgate, dga_ref[...])
        dp_ref[...] = ds.astype(dp_ref.dtype)
        part = jnp.sum(jnp.where(lane < 8, dgate, 0.0), axis=0, keepdims=True)

        @pl.when(i == 0)
        def _():
            dgb_ref[...] = part

        @pl.when(i > 0)
        def _():
            dgb_ref[...] += part

    return pl.pallas_call(
        body, name="small_bwd", grid=(R // TM,),
        in_specs=[_rb(TM, 128, 0), _rb(TM, 128, 0), _rb(TM, 128, SM0 // 128), _const((1, 128)),
                  pl.BlockSpec(memory_space=pl.ANY)],
        out_specs=[_rb(TM, 128, SM0 // 128), _const((1, 128))],
        out_shape=[jax.ShapeDtypeStruct((R, NP), MXU), jax.ShapeDtypeStruct((1, 128), F32)],
        input_output_aliases={4: 0},
        compiler_params=_cp(("arbitrary",)),
    )(dgl, dga, proj, gb_row, dproj)


def _masks():
    r = lax.broadcasted_iota(jnp.int32, (L, L), 0)
    c = lax.broadcasted_iota(jnp.int32, (L, L), 1)
    return r >= c, r == c, r


def _to_row(col, eye):
    return jnp.sum(jnp.where(eye, col, 0.0), axis=0, keepdims=True)


def _to_col(row, eye):
    return jnp.sum(jnp.where(eye, row, 0.0), axis=1, keepdims=True)


def _mlstm_chunk(q, k, logi_c, logf_c, m, n):
    tril, eye, _ = _masks()
    logi_r, logf_r = _to_row(logi_c, eye), _to_row(logf_c, eye)
    b_c = jnp.sum(jnp.where(tril, logf_r, 0.0), axis=1, keepdims=True)
    b_r = _to_row(b_c, eye)
    g = jnp.sum(logf_c, axis=0, keepdims=True)
    dmat = jnp.where(tril, b_c - b_r + logi_r, NEG)
    mrow = jnp.maximum(b_c + m, jnp.max(dmat, axis=1, keepdims=True))
    dm = jnp.exp(dmat - mrow)
    s = _dot(q, k, 1, 1)
    w = dm * s
    a_in = jnp.exp(b_c + m - mrow)
    qn = jnp.sum(q * n, axis=1, keepdims=True)
    den = a_in * qn + jnp.sum(w, axis=1, keepdims=True)
    floor = jnp.exp(-mrow)
    nrm = jnp.maximum(jnp.abs(den), floor)
    wlog_c = g - b_c + logi_c
    m_new = jnp.maximum(g + m, jnp.max(wlog_c, axis=0, keepdims=True))
    a_st = jnp.exp(g + m - m_new)
    w_c = jnp.exp(wlog_c - m_new)
    return dict(b_c=b_c, g=g, dm=dm, s=s, w=w, a_in=a_in, qn=qn, den=den, floor=floor, nrm=nrm,
                m_new=m_new, a_st=a_st, w_c=w_c, tril=tril, eye=eye)


def mlstm_fwd(qk, proj, gl, head_g):
    R = qk.shape[0]
    NC = R // L

    def body(qk_ref, v_ref, gl_ref, mo_ref, hg_ref, hm_ref, ym_ref, cs_ref, nm_ref, c_s, nm_s):
        c = pl.program_id(0)

        @pl.when(c <= CH0)
        def _():
            c_s[...] = jnp.zeros_like(c_s)
            nm_s[...] = jnp.zeros_like(nm_s)

        @pl.when(c < CH0)
        def _():
            hm_ref[...] = jnp.zeros_like(hm_ref)
            ym_ref[...] = jnp.zeros_like(ym_ref)
            cs_ref[...] = jnp.zeros_like(cs_ref)
            nm_ref[...] = jnp.zeros_like(nm_ref)

        @pl.when(c >= CH0)
        def _():
            glv = gl_ref[...]
            for h in range(NH):
                q = qk_ref[:, h * DQK:(h + 1) * DQK] * QSCALE
                k = qk_ref[:, 512 + h * DQK:512 + (h + 1) * DQK]
                v = v_ref[:, h * DV:(h + 1) * DV]
                C = c_s[h]
                n = nm_s[h, 0:1, :]
                m = nm_s[h, 1:2, 0:1]
                f = _mlstm_chunk(q, k, glv[:, h:h + 1], glv[:, 4 + h:5 + h], m, n)
                num = f["a_in"] * _dot(q, C, 1, 1) + _dot(f["w"], v, 1, 0)
                hh = num / f["nrm"]
                cs_ref[0, h] = C
                nm_ref[0, h] = nm_s[h]
                c_s[h] = f["a_st"] * C + _dot(f["w_c"] * v, k, 0, 0)
                n_new = f["a_st"] * n + jnp.sum(f["w_c"] * k, axis=0, keepdims=True)
                rowi = lax.broadcasted_iota(jnp.int32, (8, DQK), 0)
                nm_s[h] = jnp.where(rowi == 0, n_new, jnp.where(rowi == 1, f["m_new"], 0.0))
                rm = lax.rsqrt(jnp.mean(hh * hh, axis=-1, keepdims=True) + EPS)
                sl = slice(h * DV, (h + 1) * DV)
                hm_ref[:, sl] = hh
                ym_ref[:, sl] = (hh * rm * hg_ref[:, sl] * _sigmoid(mo_ref[:, sl])).astype(ym_ref.dtype)

    return pl.pallas_call(
        body, name="mlstm_fwd", grid=(NC,),
        in_specs=[_rb(L, 1024, 0), _rb(L, 1024, MV0 // 1024), _rb(L, 128, 0), _rb(L, 1024, MO0 // 1024),
                  _const((1, 1024))],
        out_specs=[_rb(L, 1024, 0), _rb(L, 1024, 0),
                   pl.BlockSpec((1, NH, DV, DQK), lambda c: (c, 0, 0, 0)),
                   pl.BlockSpec((1, NH, 8, DQK), lambda c: (c, 0, 0, 0))],
        out_shape=[jax.ShapeDtypeStruct((R, 1024), F32), jax.ShapeDtypeStruct((R, 1024), MXU),
                   jax.ShapeDtypeStruct((NC, NH, DV, DQK), F32), jax.ShapeDtypeStruct((NC, NH, 8, DQK), F32)],
        scratch_shapes=[pltpu.VMEM((NH, DV, DQK), F32), pltpu.VMEM((NH, 8, DQK), F32)],
        compiler_params=_cp(("arbitrary",)),
    )(qk, proj, gl, proj, head_g)


def mlstm_bwd(dym, hm, qk, cpre, proj, gl, head_g, cs, nm, dproj):
    R = qk.shape[0]
    NC = R // L
    rev = lambda c: NC - 1 - c

    def body(dym_ref, hm_ref, qk_ref, cp_ref, v_ref, gl_ref, mo_ref, hg_ref, cs_ref, nm_ref, dp_in,
             dp_ref, dc_ref, dgl_ref, dhg_ref, dc_s, dn_s):
        del dp_in
        step = pl.program_id(0)
        c = NC - 1 - step

        @pl.when(step == 0)
        def _():
            dc_s[...] = jnp.zeros_like(dc_s)
            dn_s[...] = jnp.zeros_like(dn_s)
            dhg_ref[...] = jnp.zeros_like(dhg_ref)

        @pl.when(c < CH0)
        def _():
            dp_ref[...] = jnp.zeros_like(dp_ref)
            dc_ref[...] = jnp.zeros_like(dc_ref)
            dgl_ref[...] = jnp.zeros_like(dgl_ref)

        @pl.when(c >= CH0)
        def _():
            glv = gl_ref[...]
            lane = lax.broadcasted_iota(jnp.int32, (L, 128), 1)
            dgl = jnp.zeros((L, 128), F32)
            for h in range(NH):
                sl = slice(h * DV, (h + 1) * DV)
                sq = slice(h * DQK, (h + 1) * DQK)
                sk = slice(512 + h * DQK, 512 + (h + 1) * DQK)
                hh = hm_ref[:, sl]
                gain = hg_ref[:, sl]
                rm = lax.rsqrt(jnp.mean(hh * hh, axis=-1, keepdims=True) + EPS)
                sg = _sigmoid(mo_ref[:, sl])
                dyv = dym_ref[:, sl]
                dno = dyv * sg
                dp_ref[:, 1024 + h * DV:1024 + (h + 1) * DV] = (dyv * hh * rm * gain * sg * (1.0 - sg)).astype(dp_ref.dtype)
                dhg_ref[:, sl] += jnp.sum(dno * hh * rm, axis=0, keepdims=True)
                dnog = dno * gain
                dh = rm * dnog - hh * (rm * rm * rm * jnp.mean(dnog * hh, axis=-1, keepdims=True))
                q = qk_ref[:, sq] * QSCALE
                k = qk_ref[:, sk]
                v = v_ref[:, sl]
                C = cs_ref[0, h]
                n = nm_ref[0, h, 0:1, :]
                m = nm_ref[0, h, 1:2, 0:1]
                f = _mlstm_chunk(q, k, glv[:, h:h + 1], glv[:, 4 + h:5 + h], m, n)
                eye = f["eye"]
                a_in, nrm, den, w = f["a_in"], f["nrm"], f["den"], f["w"]
                dnum = dh / nrm
                dnrm = -jnp.sum(dh * hh, axis=1, keepdims=True) / nrm
                dden = jnp.where(jnp.abs(den) >= f["floor"], dnrm * jnp.sign(den), 0.0)
                dw = _dot(dnum, v, 1, 1) + dden
                dv = _dot(w, dnum, 0, 0)
                ds = dw * f["dm"]
                e = dw * w
                qc = _dot(q, C, 1, 1)
                dq = _dot(ds, k, 1, 0) + a_in * _dot(dnum, C, 1, 0) + (a_in * dden) * n
                dk = _dot(ds, q, 0, 0)
                dC_in = _dot(a_in * dnum, q, 0, 0)
                dn_in = jnp.sum((a_in * dden) * q, axis=0, keepdims=True)
                da_in = jnp.sum(dnum * qc, axis=1, keepdims=True) + dden * f["qn"]
                col_e = _to_col(jnp.sum(e, axis=0, keepdims=True), eye)
                db = jnp.sum(e, axis=1, keepdims=True) + da_in * a_in - col_e
                dlogi = col_e
                dCp = dc_s[h]
                dnp = dn_s[h, 0:1, :]
                a_st, w_c = f["a_st"], f["w_c"]
                da_st = (jnp.sum(jnp.sum(dCp * C, axis=1, keepdims=True), axis=0, keepdims=True)
                         + jnp.sum(dnp * n, axis=1, keepdims=True))
                vdc = _dot(v, dCp, 1, 0)
                dw_c = jnp.sum((vdc + dnp) * k, axis=1, keepdims=True)
                dv = dv + w_c * _dot(k, dCp, 1, 1)
                dk = dk + w_c * (vdc + dnp)
                fw = dw_c * w_c
                dg = jnp.sum(fw, axis=0, keepdims=True) + da_st * a_st
                db = db - fw
                dlogi = dlogi + fw
                rowc = lax.broadcasted_iota(jnp.int32, (L, 1), 0)
                db = db + jnp.where(rowc == L - 1, dg, 0.0)
                triu = lax.broadcasted_iota(jnp.int32, (L, L), 1) >= lax.broadcasted_iota(jnp.int32, (L, L), 0)
                dlogf = jnp.sum(jnp.where(triu, _to_row(db, eye), 0.0), axis=1, keepdims=True)
                dc_s[h] = a_st * dCp + dC_in
                dn_new = a_st * dnp + dn_in
                dn_s[h] = jnp.zeros((8, DQK), F32) + dn_new
                cq, ck = cp_ref[:, sq], cp_ref[:, sk]
                s_q, s_k = _sigmoid(cq), _sigmoid(ck)
                dc_ref[:, sq] = dq * QSCALE * s_q * (1.0 + cq * (1.0 - s_q))
                dc_ref[:, sk] = dk * s_k * (1.0 + ck * (1.0 - s_k))
                dp_ref[:, sl] = dv.astype(dp_ref.dtype)
                dgl = jnp.where(lane == h, dlogi, jnp.where(lane == 4 + h, dlogf, dgl))
            dgl_ref[...] = dgl

    return pl.pallas_call(
        body, name="mlstm_bwd", grid=(NC,),
        in_specs=[pl.BlockSpec((L, 1024), lambda c: (rev(c), 0)), pl.BlockSpec((L, 1024), lambda c: (rev(c), 0)),
                  pl.BlockSpec((L, 1024), lambda c: (rev(c), 0)), pl.BlockSpec((L, 1024), lambda c: (rev(c), 0)),
                  pl.BlockSpec((L, 1024), lambda c: (rev(c), MV0 // 1024)),
                  pl.BlockSpec((L, 128), lambda c: (rev(c), 0)),
                  pl.BlockSpec((L, 1024), lambda c: (rev(c), MO0 // 1024)), _const((1, 1024)),
                  pl.BlockSpec((1, NH, DV, DQK), lambda c: (rev(c), 0, 0, 0)),
                  pl.BlockSpec((1, NH, 8, DQK), lambda c: (rev(c), 0, 0, 0)),
                  pl.BlockSpec(memory_space=pl.ANY)],
        out_specs=[pl.BlockSpec((L, 2048), lambda c: (rev(c), 0)), pl.BlockSpec((L, 1024), lambda c: (rev(c), 0)),
                   pl.BlockSpec((L, 128), lambda c: (rev(c), 0)), _const((1, 1024))],
        out_shape=[jax.ShapeDtypeStruct((R, NP), MXU), jax.ShapeDtypeStruct((R, 1024), F32),
                   jax.ShapeDtypeStruct((R, 128), F32), jax.ShapeDtypeStruct((1, 1024), F32)],
        scratch_shapes=[pltpu.VMEM((NH, DV, DQK), F32), pltpu.VMEM((NH, 8, DQK), F32)],
        input_output_aliases={10: 0},
        compiler_params=_cp(("arbitrary",)),
    )(dym, hm, qk, cpre, proj, gl, proj, head_g, cs, nm, dproj)


def _gla_logs(sm, a2p, b2, valid):
    za = _dot(sm, a2p, 1, 0) + b2
    return za, jnp.where(valid, _log_sigmoid(za) * (1.0 / TAU), 0.0)


def _valid_rows(c, width):
    row = lax.broadcasted_iota(jnp.int32, (L, width), 0) + c * L
    return row >= NPADROWS


def _gla_chunk(q, k, la):
    tril, _, _ = _masks()
    bc = _dot_exact(tril.astype(F32), la)
    btot = jnp.sum(la, axis=0, keepdims=True)
    ebc = jnp.exp(bc)
    qd = q * ebc
    ki = k * jnp.exp(-bc)
    ke = k * jnp.exp(btot - bc)
    att = jnp.where(tril, _dot(qd, ki, 1, 1), 0.0)
    return dict(tril=tril, bc=bc, btot=btot, ebc=ebc, qd=qd, ki=ki, ke=ke, att=att)


def _col128(row):
    r = lax.broadcasted_iota(jnp.int32, (DQK, DQK), 0)
    c = lax.broadcasted_iota(jnp.int32, (DQK, DQK), 1)
    return jnp.sum(jnp.where(r == c, row, 0.0), axis=1, keepdims=True)


def gla_fwd(proj, a2p, b2, head_g):
    R = proj.shape[0]
    NC = R // L

    def body(q_ref, k_ref, v_ref, gr_ref, sm_ref, a2_ref, b2_ref, hg_ref, hgl_ref, yg_ref, ss_ref, s_s):
        c = pl.program_id(0)

        @pl.when(c <= CH0)
        def _():
            s_s[...] = jnp.zeros_like(s_s)

        @pl.when(c < CH0)
        def _():
            hgl_ref[...] = jnp.zeros_like(hgl_ref)
            yg_ref[...] = jnp.zeros_like(yg_ref)
            ss_ref[...] = jnp.zeros_like(ss_ref)

        @pl.when(c >= CH0)
        def _():
            _, loga = _gla_logs(sm_ref[...], a2_ref[...], b2_ref[...], _valid_rows(c, 512))
            for h in range(NH):
                sq = slice(h * DQK, (h + 1) * DQK)
                sl = slice(h * DV, (h + 1) * DV)
                q = q_ref[:, sq] * QSCALE
                k = k_ref[:, sq]
                v = v_ref[:, sl]
                S = s_s[h]
                f = _gla_chunk(q, k, loga[:, sq])
                o = _dot(f["att"], v, 1, 0) + _dot(f["qd"], S, 1, 0)
                ss_ref[0, h] = S
                s_s[h] = _col128(jnp.exp(f["btot"])) * S + _dot(f["ke"], v, 0, 0)
                rg = lax.rsqrt(jnp.mean(o * o, axis=-1, keepdims=True) + EPS)
                gr = gr_ref[:, sl]
                hgl_ref[:, sl] = o
                yg_ref[:, sl] = (o * rg * hg_ref[:, sl] * gr * _sigmoid(gr)).astype(yg_ref.dtype)

    return pl.pallas_call(
        body, name="gla_fwd", grid=(NC,),
        in_specs=[_rb(L, 512, GQ0 // 512), _rb(L, 512, GK0 // 512), _rb(L, 1024, GV0 // 1024),
                  _rb(L, 1024, GR0 // 1024), _rb(L, 128, SM0 // 128), _const((128, 512)), _const((1, 512)),
                  _const((1, 1024))],
        out_specs=[_rb(L, 1024, 0), _rb(L, 1024, 0), pl.BlockSpec((1, NH, DQK, DV), lambda c: (c, 0, 0, 0))],
        out_shape=[jax.ShapeDtypeStruct((R, 1024), F32), jax.ShapeDtypeStruct((R, 1024), MXU),
                   jax.ShapeDtypeStruct((NC, NH, DQK, DV), F32)],
        scratch_shapes=[pltpu.VMEM((NH, DQK, DV), F32)],
        compiler_params=_cp(("arbitrary",)),
    )(proj, proj, proj, proj, proj, a2p, b2, head_g)


def gla_bwd(dyg, hgl, proj, a2p, b2, head_g, ss, dproj):
    R = proj.shape[0]
    NC = R // L
    rev = lambda c: NC - 1 - c

    def body(dy_ref, ho_ref, q_ref, k_ref, v_ref, gr_ref, sm_ref, a2_ref, b2_ref, hg_ref, ss_ref, dp_in,
             dp_ref, dga_ref, da2_ref, db2_ref, dhg_ref, ds_s):
        del dp_in
        step = pl.program_id(0)
        c = NC - 1 - step

        @pl.when(step == 0)
        def _():
            ds_s[...] = jnp.zeros_like(ds_s)
            da2_ref[...] = jnp.zeros_like(da2_ref)
            db2_ref[...] = jnp.zeros_like(db2_ref)
            dhg_ref[...] = jnp.zeros_like(dhg_ref)

        @pl.when(c < CH0)
        def _():
            dp_ref[...] = jnp.zeros_like(dp_ref)
            dga_ref[...] = jnp.zeros_like(dga_ref)

        @pl.when(c >= CH0)
        def _():
            valid = _valid_rows(c, 512)
            sm = sm_ref[...]
            za, loga = _gla_logs(sm, a2_ref[...], b2_ref[...], valid)
            dloga = []
            for h in range(NH):
                sq = slice(h * DQK, (h + 1) * DQK)
                sl = slice(h * DV, (h + 1) * DV)
                o = ho_ref[:, sl]
                gain = hg_ref[:, sl]
                rg = lax.rsqrt(jnp.mean(o * o, axis=-1, keepdims=True) + EPS)
                gr = gr_ref[:, sl]
                sg = _sigmoid(gr)
                dyv = dy_ref[:, sl]
                dno = dyv * gr * sg
                dp_ref[:, 2048 + h * DV:2048 + (h + 1) * DV] = (
                    dyv * o * rg * gain * sg * (1.0 + gr * (1.0 - sg))).astype(dp_ref.dtype)
                dhg_ref[:, sl] += jnp.sum(dno * o * rg, axis=0, keepdims=True)
                dnog = dno * gain
                do = rg * dnog - o * (rg * rg * rg * jnp.mean(dnog * o, axis=-1, keepdims=True))
                q = q_ref[:, sq] * QSCALE
                k = k_ref[:, sq]
                v = v_ref[:, sl]
                S = ss_ref[0, h]
                f = _gla_chunk(q, k, loga[:, sq])
                tril, qd, ki, ke = f["tril"], f["qd"], f["ki"], f["ke"]
                dSp = ds_s[h]
                datt = jnp.where(tril, _dot(do, v, 1, 1), 0.0)
                dqd = _dot(do, S, 1, 1) + _dot(datt, ki, 1, 0)
                dki = _dot(datt, qd, 0, 0)
                dv = _dot(f["att"], do, 0, 0) + _dot(ke, dSp, 1, 0)
                dke = _dot(v, dSp, 1, 1)
                ebt = jnp.exp(f["btot"])
                dbtot = jnp.sum(dke * ke, axis=0, keepdims=True) + ebt * _to_row128(jnp.sum(dSp * S, axis=1, keepdims=True))
                ds_s[h] = _dot(qd, do, 0, 0) + _col128(ebt) * dSp
                dq = dqd * f["ebc"]
                dk = dki * jnp.exp(-f["bc"]) + dke * jnp.exp(f["btot"] - f["bc"])
                dbc = dqd * qd - dki * ki - dke * ke
                rowc = lax.broadcasted_iota(jnp.int32, (L, DQK), 0)
                dbc = dbc + jnp.where(rowc == L - 1, dbtot, 0.0)
                triu = lax.broadcasted_iota(jnp.int32, (L, L), 1) >= lax.broadcasted_iota(jnp.int32, (L, L), 0)
                dloga.append(_dot_exact(triu.astype(F32), dbc))
                dp_ref[:, sq] = (dq * QSCALE).astype(dp_ref.dtype)
                dp_ref[:, 512 + h * DQK:512 + (h + 1) * DQK] = dk.astype(dp_ref.dtype)
                dp_ref[:, 1024 + h * DV:1024 + (h + 1) * DV] = dv.astype(dp_ref.dtype)
            dza = jnp.where(valid, jnp.concatenate(dloga, axis=1) * (1.0 / TAU) * _sigmoid(-za), 0.0)
            dga_ref[...] = _dot(dza, a2_ref[...], 1, 1)
            da2_ref[...] += _dot(sm, dza, 0, 0)
            db2_ref[...] += jnp.sum(dza, axis=0, keepdims=True)

    return pl.pallas_call(
        body, name="gla_bwd", grid=(NC,),
        in_specs=[pl.BlockSpec((L, 1024), lambda c: (rev(c), 0)), pl.BlockSpec((L, 1024), lambda c: (rev(c), 0)),
                  pl.BlockSpec((L, 512), lambda c: (rev(c), GQ0 // 512)),
                  pl.BlockSpec((L, 512), lambda c: (rev(c), GK0 // 512)),
                  pl.BlockSpec((L, 1024), lambda c: (rev(c), GV0 // 1024)),
                  pl.BlockSpec((L, 1024), lambda c: (rev(c), GR0 // 1024)),
                  pl.BlockSpec((L, 128), lambda c: (rev(c), SM0 // 128)),
                  _const((128, 512)), _const((1, 512)), _const((1, 1024)),
                  pl.BlockSpec((1, NH, DQK, DV), lambda c: (rev(c), 0, 0, 0)),
                  pl.BlockSpec(memory_space=pl.ANY)],
        out_specs=[pl.BlockSpec((L, 3072), lambda c: (rev(c), GQ0 // 3072)),
                   pl.BlockSpec((L, 128), lambda c: (rev(c), 0)),
                   _const((128, 512)), _const((1, 512)), _const((1, 1024))],
        out_shape=[jax.ShapeDtypeStruct((R, NP), MXU), jax.ShapeDtypeStruct((R, 128), F32),
                   jax.ShapeDtypeStruct((128, 512), F32), jax.ShapeDtypeStruct((1, 512), F32),
                   jax.ShapeDtypeStruct((1, 1024), F32)],
        scratch_shapes=[pltpu.VMEM((NH, DQK, DV), F32)],
        input_output_aliases={11: 0},
        compiler_params=_cp(("arbitrary",)),
    )(dyg, hgl, proj, proj, proj, proj, proj, a2p, b2, head_g, ss, dproj)


def _to_row128(col):
    r = lax.broadcasted_iota(jnp.int32, (DQK, DQK), 0)
    c = lax.broadcasted_iota(jnp.int32, (DQK, DQK), 1)
    return jnp.sum(jnp.where(r == c, col, 0.0), axis=0, keepdims=True)


def local_step(x, target, meta, norm1_g, wp, conv_w, conv_b, m_gate_b, g_a2, g_a2_b, m_head_g, g_head_g,
               w_bm, w_bg, w_out, norm2_g, w_gu, w_down, final_g):
    seq = x.shape[0]
    assert seq % TM == 0
    h0 = jnp.concatenate([jnp.zeros((NPADROWS, D), F32), meta, x], axis=0)
    gb_row = jnp.zeros((1, 128), F32).at[0, 0:8].set(m_gate_b.reshape(8))
    a2p = jnp.zeros((128, 512), F32).at[8:8 + RANK].set(g_a2)
    mhg = m_head_g.reshape(1, 1024)
    ghg = g_head_g.reshape(1, 1024)

    xn = rms_fwd(h0, norm1_g, "rms1_fwd")
    proj = matmul(xn, wp, "nn", "proj_fwd")
    cpre, qk, gl = prep_fwd(proj, conv_w, conv_b, gb_row)
    hm, ym, cs, nm = mlstm_fwd(qk, proj, gl, mhg)
    hgl, yg, ss = gla_fwd(proj, a2p, g_a2_b, ghg)
    bm = matmul(ym, w_bm, "nn", "branch_m_fwd")
    bg = matmul(yg, w_bg, "nn", "branch_g_fwd")
    merged = merge_fwd(bm, bg, proj)
    h1 = matmul(merged, w_out, "nn", "out_fwd", add=h0)
    hn = rms_fwd(h1, norm2_g, "rms2_fwd")
    au = matmul(hn, w_gu, "nn", "ff_in_fwd", tn=1408)
    ff = swiglu_fwd(au)
    h2 = matmul(ff, w_down, "nn", "ff_down_fwd", add=h1, tk=1408)
    dh2, loss, d_final_g = final_loss(h2, final_g.reshape(1, D), target)

    d_w_down = matmul(ff, dh2, "tn", "ff_down_wgrad", tm=1408, tk=512)
    dff = matmul(dh2, w_down, "nt", "ff_down_dgrad", tn=1408)
    da, du = swiglu_bwd(au, dff)
    dau = jnp.concatenate([da, du], axis=1)
    d_w_gu = matmul(hn, dau, "tn", "ff_in_wgrad", tn=1408, tk=512)
    dhn = matmul(dau, w_gu, "nt", "ff_in_dgrad", tk=1408)
    dh1, d_norm2_g = rms_bwd(dhn, h1, norm2_g, dh2, "rms2_bwd")

    d_w_out = matmul(merged, dh1, "tn", "out_wgrad", tk=512)
    dmerged = matmul(dh1, w_out, "nt", "out_dgrad")
    dbm, dbg, dproj = merge_bwd(dmerged, bm, bg, proj)
    d_w_bm = matmul(ym, dbm, "tn", "branch_m_wgrad", tk=512)
    d_w_bg = matmul(yg, dbg, "tn", "branch_g_wgrad", tk=512)
    dym = matmul(dbm, w_bm, "nt", "branch_m_dgrad")
    dyg = matmul(dbg, w_bg, "nt", "branch_g_dgrad")
    dproj, dc, dgl, d_mhg = mlstm_bwd(dym, hm, qk, cpre, proj, gl, mhg, cs, nm, dproj)
    dproj, dga, d_a2p, d_a2b, d_ghg = gla_bwd(dyg, hgl, proj, a2p, g_a2_b, ghg, ss, dproj)
    dproj, d_conv = conv_bwd(dc, proj, conv_w, dproj)
    dproj, d_gb = small_bwd(dgl, dga, proj, gb_row, dproj)
    d_wp = matmul(xn, dproj, "tn", "proj_wgrad", tk=512)
    dxn = matmul(dproj, wp, "nt", "proj_dgrad", tk=1664)
    dh0, d_norm1_g = rms_bwd(dxn, h0, norm1_g, dh1, "rms1_bwd")

    grads = dict(
        norm1_g=d_norm1_g, wp=d_wp, conv_w=d_conv[0:4], conv_b=d_conv[4:5], m_gate_b=d_gb[0, 0:8].reshape(1, 2, 4),
        g_a2=d_a2p[8:8 + RANK], g_a2_b=d_a2b, m_head_g=d_mhg.reshape(NH, DV), g_head_g=d_ghg.reshape(NH, DV),
        w_branch_m=d_w_bm, w_branch_g=d_w_bg, w_out=d_w_out, norm2_g=d_norm2_g, w_gu=d_w_gu, w_ff_down=d_w_down,
        final_g=d_final_g)
    return loss, dh0, grads


def regroup_cols(w):
    pad = jnp.zeros(w.shape[:-1] + (NP - NPROJ,), w.dtype)
    return jnp.concatenate([w[..., 1024:2048], w[..., 2056:3080], w[..., 0:1024], w[..., 3080:5128],
                            w[..., 5144:8216], w[..., 2048:2056], w[..., 5128:5144], pad], axis=-1)


def ungroup_cols(g):
    return jnp.concatenate([g[..., QK0:QK0 + 1024], g[..., MV0:MV0 + 1024], g[..., SM0:SM0 + 8],
                            g[..., MO0:MO0 + 1024], g[..., GQ0:GQ0 + 2048], g[..., SM0 + 8:SM0 + 24],
                            g[..., GR0:GR0 + 3072]], axis=-1)


_MESHID = pl.DeviceIdType.MESH
_RELS = [(0, 0, 1), (1, 0, 0), (0, 1, 0), (1, 1, 0), (1, 0, 1), (0, 1, 1), (1, 1, 1)]


def _flip(v, bit):
    return 1 - v if bit else v


def all_gather(arrs, name):
    n = len(arrs)

    def body(*refs):
        ins, outs = refs[:n], refs[n:2 * n]
        send_sems, recv_sems, local_sems = refs[2 * n:]
        x, y, c = lax.axis_index("x"), lax.axis_index("y"), lax.axis_index("c")
        me, sibling = (x, y, c), (x, y, 1 - c)
        chips = [(1 - x, y), (x, 1 - y), (1 - x, 1 - y)]

        def slot(p):
            return 4 * p[0] + 2 * p[1] + p[2]

        def copy(a, k, block, to, src=None):
            dst = outs[a].at[slot(block)]
            return pltpu.make_async_remote_copy(
                src_ref=dst if src is None else src, dst_ref=dst,
                send_sem=send_sems.at[a, k], recv_sem=recv_sems.at[a, k],
                device_id=to, device_id_type=_MESHID)

        mine = [pltpu.make_async_copy(ins[a], outs[a].at[slot(me)], local_sems.at[a]) for a in range(n)]
        for cp in mine:
            cp.start()
        first = []
        for a in range(n):
            first.append(copy(a, 0, me, sibling, src=ins[a]))
            first += [copy(a, 1 + j, me, (*chip, c), src=ins[a]) for j, chip in enumerate(chips)]
        for cp in first:
            cp.start()
        passed = []
        for j, chip in enumerate(chips):
            for a in range(n):
                copy(a, 1 + j, (*chip, c), me).wait_recv()
                fwd = copy(a, 4 + j, (*chip, c), sibling)
                fwd.start()
                passed.append(fwd)
        for a in range(n):
            copy(a, 0, sibling, me).wait_recv()
            for j, chip in enumerate(chips):
                copy(a, 4 + j, (*chip, 1 - c), me).wait_recv()
        for cp in first + passed:
            cp.wait_send()
        for cp in mine:
            cp.wait()

    anyspec = pl.BlockSpec(memory_space=pl.ANY)
    return pl.pallas_call(
        body, name=name,
        in_specs=[anyspec] * n, out_specs=[anyspec] * n,
        out_shape=[jax.ShapeDtypeStruct((NDEV,) + a.shape, a.dtype) for a in arrs],
        scratch_shapes=[pltpu.SemaphoreType.DMA((n, 7)), pltpu.SemaphoreType.DMA((n, 7)),
                        pltpu.SemaphoreType.DMA((n,))],
    )(*arrs)


def exchange(blocks, rep, name):
    def body(b_ref, r_ref, ob_ref, or_ref, send_sems, recv_sems, local_sems):
        x, y, c = lax.axis_index("x"), lax.axis_index("y"), lax.axis_index("c")
        me = 4 * x + 2 * y + c
        loc = [pltpu.make_async_copy(b_ref.at[me], ob_ref.at[me], local_sems.at[0]),
               pltpu.make_async_copy(r_ref, or_ref.at[me], local_sems.at[1])]
        for cp in loc:
            cp.start()
        sends = []
        for k, (fx, fy, fc) in enumerate(_RELS):
            peer = (_flip(x, fx), _flip(y, fy), _flip(c, fc))
            pid = 4 * peer[0] + 2 * peer[1] + peer[2]
            sends.append(pltpu.make_async_remote_copy(
                src_ref=b_ref.at[pid], dst_ref=ob_ref.at[me], send_sem=send_sems.at[0, k],
                recv_sem=recv_sems.at[0, k], device_id=peer, device_id_type=_MESHID))
            sends.append(pltpu.make_async_remote_copy(
                src_ref=r_ref, dst_ref=or_ref.at[me], send_sem=send_sems.at[1, k],
                recv_sem=recv_sems.at[1, k], device_id=peer, device_id_type=_MESHID))
        for cp in sends:
            cp.start()
        for k, (fx, fy, fc) in enumerate(_RELS):
            peer = (_flip(x, fx), _flip(y, fy), _flip(c, fc))
            pid = 4 * peer[0] + 2 * peer[1] + peer[2]
            pltpu.make_async_remote_copy(
                src_ref=b_ref.at[pid], dst_ref=ob_ref.at[pid], send_sem=send_sems.at[0, k],
                recv_sem=recv_sems.at[0, k], device_id=peer, device_id_type=_MESHID).wait_recv()
            pltpu.make_async_remote_copy(
                src_ref=r_ref, dst_ref=or_ref.at[pid], send_sem=send_sems.at[1, k],
                recv_sem=recv_sems.at[1, k], device_id=peer, device_id_type=_MESHID).wait_recv()
        for cp in sends:
            cp.wait_send()
        for cp in loc:
            cp.wait()

    anyspec = pl.BlockSpec(memory_space=pl.ANY)
    return pl.pallas_call(
        body, name=name,
        in_specs=[anyspec, anyspec], out_specs=[anyspec, anyspec],
        out_shape=[jax.ShapeDtypeStruct(blocks.shape, blocks.dtype),
                   jax.ShapeDtypeStruct((NDEV,) + rep.shape, rep.dtype)],
        scratch_shapes=[pltpu.SemaphoreType.DMA((2, 7)), pltpu.SemaphoreType.DMA((2, 7)),
                        pltpu.SemaphoreType.DMA((2,))],
    )(blocks, rep)


def adamw(parts, w, m, v, name, tr):
    rows = w.shape[0]
    c1 = 1.0 - ADAM_B1 ** ADAM_STEP
    c2 = 1.0 - ADAM_B2 ** ADAM_STEP

    def body(p_ref, w_ref, m_ref, v_ref, g_ref, d_ref, nm_ref, nv_ref):
        g = p_ref[0].astype(F32)
        for j in range(1, NDEV):
            g = g + p_ref[j].astype(F32)
        mn = ADAM_B1 * m_ref[...] + (1.0 - ADAM_B1) * g
        vn = ADAM_B2 * v_ref[...] + (1.0 - ADAM_B2) * (g * g)
        g_ref[...] = g
        nm_ref[...] = mn
        nv_ref[...] = vn
        d_ref[...] = -ADAM_LR * ((mn / c1) / (jnp.sqrt(vn / c2) + ADAM_EPS) + ADAM_WD * w_ref[...])

    spec = _rb(tr, 1024, 0)
    return pl.pallas_call(
        body, name=name, grid=(rows // tr,),
        in_specs=[pl.BlockSpec((NDEV, tr, 1024), lambda i: (0, i, 0)), spec, spec, spec],
        out_specs=[spec] * 4, out_shape=[jax.ShapeDtypeStruct((rows, 1024), F32)] * 4,
        compiler_params=_cp(("parallel",)),
    )(parts, w, m, v)


SHARDED = [("meta_tokens", 1, (16, 1024)), ("w_in", 1, (1024, NPROJ)), ("conv_w", 1, (4, 1024)),
           ("g_a2", 1, (16, 512)), ("m_head_g", 1, (4, 256)), ("g_head_g", 1, (4, 256)),
           ("w_branch_m", 0, (1024, 1024)), ("w_branch_g", 0, (1024, 1024)), ("w_out", 0, (1024, 1024)),
           ("w_ff_gate", 1, (1024, DFF)), ("w_ff_up", 1, (1024, DFF)), ("w_ff_down", 0, (DFF, 1024))]
BIG = ("w_in", "w_branch_m", "w_branch_g", "w_out", "w_ff_gate", "w_ff_up", "w_ff_down")
REPL = [("norm1_g", (1, 1024)), ("conv_b", (1, 1024)), ("m_gate_b", (1, 2, 4)), ("g_a2_b", (1, 512)),
        ("norm2_g", (1, 1024)), ("final_g", (1024,))]
ROWS_S = 2480
ROWS_R = 8
REPL_SIZE = 1024 + 1024 + 8 + 512 + 1024 + 1024


def _shard_shape(ax, full):
    s = list(full)
    s[ax] //= NDEV
    return tuple(s)


def pack_rows(vecs, rows):
    flat = jnp.concatenate([v.reshape(-1) for v in vecs])
    return jnp.pad(flat, (0, rows * 1024 - flat.shape[0])).reshape(rows, 1024)


def unpack_rows(packed, shapes):
    flat = packed.reshape(-1)
    out, off = [], 0
    for s in shapes:
        n = 1
        for d in s:
            n *= d
        out.append(flat[off:off + n].reshape(s))
        off += n
    return out


def gathered_full(g8, names, dtype_rows):
    flat = g8.reshape(NDEV, -1)
    out, off = {}, 0
    for name, ax, full in SHARDED:
        if name not in names:
            continue
        ss = _shard_shape(ax, full)
        n = ss[0] * ss[1]
        blk = flat[:, off:off + n].reshape((NDEV,) + ss)
        off += n
        if ax == 0:
            out[name] = blk.reshape(full)
        else:
            out[name] = jnp.transpose(blk, (1, 0, 2)).reshape(full)
    return out


def split_blocks(full_grads):
    per_dev = []
    for j in range(NDEV):
        vecs = []
        for name, ax, full in SHARDED:
            g = full_grads[name]
            n = full[ax] // NDEV
            vecs.append(lax.slice_in_dim(g, j * n, (j + 1) * n, axis=ax))
        per_dev.append(pack_rows(vecs, ROWS_S))
    return jnp.stack(per_dev)


def kernel(x, meta_tokens, norm1_g, w_in, conv_w, conv_b, m_gate_b, g_a2, g_a2_b, m_head_g, g_head_g, w_branch_m, w_branch_g, w_out, norm2_g, w_ff_gate, w_ff_up, w_ff_down, final_g, loss_target, m_meta_tokens, m_norm1_g, m_w_in, m_conv_w, m_conv_b, m_m_gate_b, m_g_a2, m_g_a2_b, m_m_head_g, m_g_head_g, m_w_branch_m, m_w_branch_g, m_w_out, m_norm2_g, m_w_ff_gate, m_w_ff_up, m_w_ff_down, m_final_g, v_meta_tokens, v_norm1_g, v_w_in, v_conv_w, v_conv_b, v_m_gate_b, v_g_a2, v_g_a2_b, v_m_head_g, v_g_head_g, v_w_branch_m, v_w_branch_g, v_w_out, v_norm2_g, v_w_ff_gate, v_w_ff_up, v_w_ff_down, v_final_g):
    w_sh = dict(meta_tokens=meta_tokens, w_in=w_in[0], conv_w=conv_w[0], g_a2=g_a2[0], m_head_g=m_head_g[0],
                g_head_g=g_head_g[0], w_branch_m=w_branch_m[0], w_branch_g=w_branch_g[0], w_out=w_out[0],
                w_ff_gate=w_ff_gate[0], w_ff_up=w_ff_up[0], w_ff_down=w_ff_down[0])
    m_sh = dict(meta_tokens=m_meta_tokens, w_in=m_w_in[0], conv_w=m_conv_w[0], g_a2=m_g_a2[0],
                m_head_g=m_m_head_g[0], g_head_g=m_g_head_g[0], w_branch_m=m_w_branch_m[0],
                w_branch_g=m_w_branch_g[0], w_out=m_w_out[0], w_ff_gate=m_w_ff_gate[0], w_ff_up=m_w_ff_up[0],
                w_ff_down=m_w_ff_down[0])
    v_sh = dict(meta_tokens=v_meta_tokens, w_in=v_w_in[0], conv_w=v_conv_w[0], g_a2=v_g_a2[0],
                m_head_g=v_m_head_g[0], g_head_g=v_g_head_g[0], w_branch_m=v_w_branch_m[0],
                w_branch_g=v_w_branch_g[0], w_out=v_w_out[0], w_ff_gate=v_w_ff_gate[0], w_ff_up=v_w_ff_up[0],
                w_ff_down=v_w_ff_down[0])
    w_rep = dict(norm1_g=norm1_g, conv_b=conv_b, m_gate_b=m_gate_b, g_a2_b=g_a2_b, norm2_g=norm2_g, final_g=final_g)
    m_rep = dict(norm1_g=m_norm1_g, conv_b=m_conv_b, m_gate_b=m_m_gate_b, g_a2_b=m_g_a2_b, norm2_g=m_norm2_g,
                 final_g=m_final_g)
    v_rep = dict(norm1_g=v_norm1_g, conv_b=v_conv_b, m_gate_b=v_m_gate_b, g_a2_b=v_g_a2_b, norm2_g=v_norm2_g,
                 final_g=v_final_g)
    names_s = [n for n, _, _ in SHARDED]
    small = [n for n in names_s if n not in BIG]

    big_rows = pack_rows([w_sh[n].astype(MXU) for n in names_s if n in BIG], ROWS_S)
    small_rows = pack_rows([w_sh[n] for n in small], ROWS_R)
    big8, small8 = all_gather([big_rows, small_rows], "param_all_gather")
    full = gathered_full(big8, BIG, None)
    full.update(gathered_full(small8, small, None))
    wp = regroup_cols(full["w_in"])
    w_gu = jnp.concatenate([full["w_ff_gate"], full["w_ff_up"]], axis=1)

    loss, dh0, g = local_step(
        x[0], loss_target[0], full["meta_tokens"], norm1_g, wp, full["conv_w"], conv_b, m_gate_b[0],
        full["g_a2"], g_a2_b, full["m_head_g"], full["g_head_g"], full["w_branch_m"], full["w_branch_g"],
        full["w_out"], norm2_g, w_gu, full["w_ff_down"], final_g)

    full_grads = dict(meta_tokens=dh0[NPADROWS:PADR], w_in=ungroup_cols(g["wp"]), conv_w=g["conv_w"], g_a2=g["g_a2"],
                      m_head_g=g["m_head_g"], g_head_g=g["g_head_g"], w_branch_m=g["w_branch_m"],
                      w_branch_g=g["w_branch_g"], w_out=g["w_out"], w_ff_gate=g["w_gu"][:, :DFF],
                      w_ff_up=g["w_gu"][:, DFF:], w_ff_down=g["w_ff_down"])
    blocks = split_blocks(full_grads).astype(WIRE)
    rep = pack_rows([g[n] for n, _ in REPL] + [loss[0, 0:1]], ROWS_R)
    got_blocks, got_rep = exchange(blocks, rep, "grad_exchange")

    shapes_s = [_shard_shape(ax, fullshape) for _, ax, fullshape in SHARDED]
    outs_s = adamw(got_blocks, pack_rows([w_sh[n] for n in names_s], ROWS_S),
                   pack_rows([m_sh[n] for n in names_s], ROWS_S), pack_rows([v_sh[n] for n in names_s], ROWS_S),
                   "adamw_sharded", ROWS_S // 5)
    shapes_r = [s for _, s in REPL]
    one = jnp.ones((1,), F32)
    outs_r = adamw(got_rep, pack_rows([w_rep[n] for n, _ in REPL] + [one], ROWS_R),
                   pack_rows([m_rep[n] for n, _ in REPL] + [one], ROWS_R),
                   pack_rows([v_rep[n] for n, _ in REPL] + [one], ROWS_R), "adamw_replicated", ROWS_R)

    def lead(name, arr):
        return arr if name == "meta_tokens" else arr[None]

    result = {}
    for kind, packed_s, packed_r in zip(("grad", "delta", "new_m", "new_v"), outs_s, outs_r):
        for name, arr in zip(names_s, unpack_rows(packed_s, shapes_s)):
            result[kind, name] = lead(name, arr)
        for name, arr in zip([n for n, _ in REPL], unpack_rows(packed_r, shapes_r)):
            result[kind, name] = arr
    loss_total = outs_r[0].reshape(-1)[REPL_SIZE]
    order = ["meta_tokens", "norm1_g", "w_in", "conv_w", "conv_b", "m_gate_b", "g_a2", "g_a2_b", "m_head_g", "g_head_g",
             "w_branch_m", "w_branch_g", "w_out", "norm2_g", "w_ff_gate", "w_ff_up", "w_ff_down", "final_g"]
    grad_x = dh0[PADR:][None]
    return (loss_total, grad_x, *[result[kind, n] for kind in ("grad", "delta", "new_m", "new_v") for n in order])
```

```python
import functools

import jax
import jax.numpy as jnp
from jax import lax
from jax.experimental import pallas as pl
from jax.experimental.pallas import tpu as pltpu

F32 = jnp.float32
MXU = jnp.bfloat16
WIRE = jnp.bfloat16

D = 1024
NH = 4
DV = 256
DQK = 128
L = 64
NMETA = 16
PADR = 512
CH0 = PADR // L - 1
NPADROWS = PADR - NMETA
RANK = 16
DFF = 2816
EPS = 1e-6
TAU = 16.0
QSCALE = DQK ** -0.5
NEG = -1e30
NDEV = 8

MV0, MO0, QK0, GQ0, GK0, GV0, GR0, GM0, GG0, SM0 = 0, 1024, 2048, 3072, 3584, 4096, 5120, 6144, 7168, 8192
NP = 8320
NPROJ = 8216

ADAM_LR, ADAM_B1, ADAM_B2, ADAM_EPS, ADAM_WD, ADAM_STEP = 0.001, 0.9, 0.999, 1e-08, 0.01, 10

VMEM_LIMIT = 56 * 1024 * 1024
TM = 512


def _cp(sem):
    return pltpu.CompilerParams(dimension_semantics=sem, vmem_limit_bytes=VMEM_LIMIT)


def _sigmoid(x):
    return 1.0 / (1.0 + jnp.exp(-x))


def _log_sigmoid(x):
    return jnp.minimum(x, 0.0) - jnp.log1p(jnp.exp(-jnp.abs(x)))


def _dot(a, b, ca, cb):
    return lax.dot_general(a.astype(MXU), b.astype(MXU), (((ca,), (cb,)), ((), ())), preferred_element_type=F32)


def _dot_exact(a, b):
    return lax.dot_general(a, b, (((1,), (0,)), ((), ())), precision=lax.Precision.HIGHEST,
                           preferred_element_type=F32)


def _rb(tm, w, cb):
    return pl.BlockSpec((tm, w), lambda i: (i, cb))


def _const(shape):
    nd = len(shape)
    return pl.BlockSpec(shape, lambda i: (0,) * nd)


def _pick(n, target):
    if n <= target:
        return n
    best = None
    for t in range(128, target + 1, 128):
        if n % t == 0:
            best = t
    assert best is not None, (n, target)
    return best


def matmul(a, b, mode, name, add=None, out_dtype=F32, tm=512, tn=1664, tk=1024):
    if mode == "nn":
        (M, K), (K2, N) = a.shape, b.shape
    elif mode == "nt":
        (M, K), (N, K2) = a.shape, b.shape
    else:
        (K, M), (K2, N) = a.shape, b.shape
    assert K == K2, (a.shape, b.shape, mode)
    tm, tn, tk = _pick(M, tm), _pick(N, tn), _pick(K, tk)
    nk = K // tk
    assert nk == 1 or out_dtype == F32
    ca, cb = {"nn": (1, 0), "nt": (1, 1), "tn": (0, 0)}[mode]
    a_spec = {"nn": pl.BlockSpec((tm, tk), lambda j, i, k: (i, k)),
              "nt": pl.BlockSpec((tm, tk), lambda j, i, k: (i, k)),
              "tn": pl.BlockSpec((tk, tm), lambda j, i, k: (k, i))}[mode]
    b_spec = {"nn": pl.BlockSpec((tk, tn), lambda j, i, k: (k, j)),
              "nt": pl.BlockSpec((tn, tk), lambda j, i, k: (j, k)),
              "tn": pl.BlockSpec((tk, tn), lambda j, i, k: (k, j))}[mode]
    o_spec = pl.BlockSpec((tm, tn), lambda j, i, k: (i, j))
    has_add = add is not None

    def body(*refs):
        if has_add:
            a_ref, b_ref, add_ref, o_ref = refs
        else:
            a_ref, b_ref, o_ref = refs
            add_ref = None
        part = _dot(a_ref[...], b_ref[...], ca, cb)
        if nk == 1:
            if has_add:
                part = part + add_ref[...]
            o_ref[...] = part.astype(o_ref.dtype)
            return
        k = pl.program_id(2)

        @pl.when(k == 0)
        def _():
            o_ref[...] = part + add_ref[...] if has_add else part

        @pl.when(k > 0)
        def _():
            o_ref[...] += part

    in_specs = [a_spec, b_spec] + ([o_spec] if has_add else [])
    args = (a, b) + ((add,) if has_add else ())
    return pl.pallas_call(
        body, name=name, grid=(N // tn, M // tm, nk),
        in_specs=in_specs, out_specs=o_spec,
        out_shape=jax.ShapeDtypeStruct((M, N), out_dtype),
        compiler_params=_cp(("parallel", "parallel", "arbitrary")),
    )(*args)


def rms_fwd(x, g, name):
    R = x.shape[0]

    def body(x_ref, g_ref, y_ref):
        xv = x_ref[...]
        r = lax.rsqrt(jnp.mean(xv * xv, axis=-1, keepdims=True) + EPS)
        y_ref[...] = (xv * r * g_ref[...]).astype(y_ref.dtype)

    return pl.pallas_call(
        body, name=name, grid=(R // TM,),
        in_specs=[_rb(TM, D, 0), _const((1, D))], out_specs=_rb(TM, D, 0),
        out_shape=jax.ShapeDtypeStruct((R, D), MXU), compiler_params=_cp(("parallel",)),
    )(x, g)


def rms_bwd(dy, x, g, dres, name):
    R = x.shape[0]

    def body(dy_ref, x_ref, g_ref, dres_ref, dx_ref, dg_ref):
        i = pl.program_id(0)
        xv, dyv = x_ref[...], dy_ref[...]
        r = lax.rsqrt(jnp.mean(xv * xv, axis=-1, keepdims=True) + EPS)
        dyg = dyv * g_ref[...]
        dx_ref[...] = dres_ref[...] + r * dyg - xv * (r * r * r * jnp.mean(dyg * xv, axis=-1, keepdims=True))
        part = jnp.sum(dyv * xv * r, axis=0, keepdims=True)

        @pl.when(i == 0)
        def _():
            dg_ref[...] = part

        @pl.when(i > 0)
        def _():
            dg_ref[...] += part

    return pl.pallas_call(
        body, name=name, grid=(R // TM,),
        in_specs=[_rb(TM, D, 0), _rb(TM, D, 0), _const((1, D)), _rb(TM, D, 0)],
        out_specs=[_rb(TM, D, 0), _const((1, D))],
        out_shape=[jax.ShapeDtypeStruct((R, D), F32), jax.ShapeDtypeStruct((1, D), F32)],
        compiler_params=_cp(("arbitrary",)),
    )(dy, x, g, dres)


def _shift_down(cur, prev8, s):
    tm = cur.shape[0]
    rolled = pltpu.roll(cur, s, 0)
    rows8 = lax.broadcasted_iota(jnp.int32, (8, cur.shape[1]), 0)
    head = jnp.where(rows8 < s, pltpu.roll(prev8, s, 0), rolled[0:8])
    return jnp.concatenate([head, rolled[8:tm]], axis=0)


def _shift_up(cur, next8, s):
    tm = cur.shape[0]
    rolled = pltpu.roll(cur, tm - s, 0)
    rows8 = lax.broadcasted_iota(jnp.int32, (8, cur.shape[1]), 0)
    tail = jnp.where(rows8 >= 8 - s, pltpu.roll(next8, 8 - s, 0), rolled[tm - 8:tm])
    return jnp.concatenate([rolled[0:tm - 8], tail], axis=0)


def prep_fwd(proj, conv_w, conv_b, gb_row):
    R = proj.shape[0]
    t8 = TM // 8

    def body(x_ref, halo_ref, sm_ref, w_ref, b_ref, gb_ref, c_ref, qk_ref, gl_ref):
        i = pl.program_id(0)
        x = x_ref[...]
        halo = halo_ref[...]
        w = w_ref[...]
        c = x * w[3:4, :] + b_ref[...]
        for s in (1, 2, 3):
            c = c + _shift_down(x, halo, s) * w[3 - s:4 - s, :]
        c_ref[...] = c
        qk_ref[...] = c * _sigmoid(c)
        z = sm_ref[...] + gb_ref[...]
        lane = lax.broadcasted_iota(jnp.int32, z.shape, 1)
        row = lax.broadcasted_iota(jnp.int32, z.shape, 0) + i * TM
        valid = row >= NPADROWS
        logi = jnp.where(valid, z, NEG)
        logf = jnp.where(valid, _log_sigmoid(z), 0.0)
        gl_ref[...] = jnp.where(lane < 4, logi, jnp.where(lane < 8, logf, 0.0))

    return pl.pallas_call(
        body, name="prep_fwd", grid=(R // TM,),
        in_specs=[_rb(TM, 1024, QK0 // 1024),
                  pl.BlockSpec((8, 1024), lambda i: (jnp.maximum(i * t8 - 1, 0), QK0 // 1024)),
                  _rb(TM, 128, SM0 // 128), _const((4, 1024)), _const((1, 1024)), _const((1, 128))],
        out_specs=[_rb(TM, 1024, 0), _rb(TM, 1024, 0), _rb(TM, 128, 0)],
        out_shape=[jax.ShapeDtypeStruct((R, 1024), F32), jax.ShapeDtypeStruct((R, 1024), F32),
                   jax.ShapeDtypeStruct((R, 128), F32)],
        compiler_params=_cp(("parallel",)),
    )(proj, proj, proj, conv_w, conv_b, gb_row)


def merge_fwd(bm, bg, proj):
    R = bm.shape[0]

    def body(bm_ref, bg_ref, gm_ref, gg_ref, o_ref):
        o_ref[...] = (_sigmoid(gm_ref[...]) * bm_ref[...] + _sigmoid(gg_ref[...]) * bg_ref[...]).astype(o_ref.dtype)

    return pl.pallas_call(
        body, name="merge_fwd", grid=(R // TM,),
        in_specs=[_rb(TM, D, 0), _rb(TM, D, 0), _rb(TM, D, GM0 // D), _rb(TM, D, GG0 // D)],
        out_specs=_rb(TM, D, 0), out_shape=jax.ShapeDtypeStruct((R, D), MXU),
        compiler_params=_cp(("parallel",)),
    )(bm, bg, proj, proj)


def merge_bwd(dmerged, bm, bg, proj):
    R = bm.shape[0]

    def body(dm_ref, bm_ref, bg_ref, gm_ref, gg_ref, dbm_ref, dbg_ref, dp_ref):
        dm = dm_ref[...]
        sm, sg = _sigmoid(gm_ref[...]), _sigmoid(gg_ref[...])
        dbm_ref[...] = (dm * sm).astype(dbm_ref.dtype)
        dbg_ref[...] = (dm * sg).astype(dbg_ref.dtype)
        dp_ref[:, 0:D] = (dm * bm_ref[...] * sm * (1.0 - sm)).astype(dp_ref.dtype)
        dp_ref[:, D:2 * D] = (dm * bg_ref[...] * sg * (1.0 - sg)).astype(dp_ref.dtype)

    return pl.pallas_call(
        body, name="merge_bwd", grid=(R // TM,),
        in_specs=[_rb(TM, D, 0), _rb(TM, D, 0), _rb(TM, D, 0), _rb(TM, D, GM0 // D), _rb(TM, D, GG0 // D)],
        out_specs=[_rb(TM, D, 0), _rb(TM, D, 0), _rb(TM, 2 * D, GM0 // (2 * D))],
        out_shape=[jax.ShapeDtypeStruct((R, D), MXU), jax.ShapeDtypeStruct((R, D), MXU),
                   jax.ShapeDtypeStruct((R, NP), MXU)],
        compiler_params=_cp(("parallel",)),
    )(dmerged, bm, bg, proj, proj)


def swiglu_fwd(au):
    R = au.shape[0]
    tw = DFF // 2

    def body(a_ref, u_ref, o_ref):
        a = a_ref[...]
        o_ref[...] = (a * _sigmoid(a) * u_ref[...]).astype(o_ref.dtype)

    return pl.pallas_call(
        body, name="swiglu_fwd", grid=(R // TM, 2),
        in_specs=[pl.BlockSpec((TM, tw), lambda i, j: (i, j)), pl.BlockSpec((TM, tw), lambda i, j: (i, j + 2))],
        out_specs=pl.BlockSpec((TM, tw), lambda i, j: (i, j)),
        out_shape=jax.ShapeDtypeStruct((R, DFF), MXU), compiler_params=_cp(("parallel", "parallel")),
    )(au, au)


def swiglu_bwd(au, dff):
    R = au.shape[0]
    tw = DFF // 2

    def body(a_ref, u_ref, d_ref, dau_ref):
        a, d = a_ref[...], d_ref[...]
        s = _sigmoid(a)
        dau_ref[:, 0:DFF] = (d * u_ref[...] * s * (1.0 + a * (1.0 - s))).astype(dau_ref.dtype)
        dau_ref[:, DFF:2 * DFF] = (d * a * s).astype(dau_ref.dtype)

    tr = 256
    return pl.pallas_call(
        body, name="swiglu_bwd", grid=(R // tr,),
        in_specs=[_rb(tr, DFF, 0), _rb(tr, DFF, 1), _rb(tr, DFF, 0)],
        out_specs=_rb(tr, 2 * DFF, 0),
        out_shape=jax.ShapeDtypeStruct((R, 2 * DFF), MXU),
        compiler_params=_cp(("parallel",)),
    )(au, au, dff)


def final_loss(h2, gf, target):
    R = h2.shape[0]
    assert PADR == TM

    def body(h_ref, g_ref, t_ref, dh_ref, loss_ref, dg_ref):
        i = pl.program_id(0)
        hv = h_ref[...]
        r = lax.rsqrt(jnp.mean(hv * hv, axis=-1, keepdims=True) + EPS)
        g = g_ref[...]
        live = (i >= 1).astype(F32)
        e = (hv * r * g - t_ref[...]) * live
        dy = e * (1.0 / D)
        dyg = dy * g
        dh_ref[...] = r * dyg - hv * (r * r * r * jnp.mean(dyg * hv, axis=-1, keepdims=True))
        lpart = jnp.zeros((1, 128), F32) + 0.5 * jnp.sum(jnp.sum(e * e, axis=1, keepdims=True), axis=0, keepdims=True) * (1.0 / D)
        gpart = jnp.sum(dy * hv * r, axis=0, keepdims=True)

        @pl.when(i == 0)
        def _():
            loss_ref[...] = lpart
            dg_ref[...] = gpart

        @pl.when(i > 0)
        def _():
            loss_ref[...] += lpart
            dg_ref[...] += gpart

    return pl.pallas_call(
        body, name="final_loss", grid=(R // TM,),
        in_specs=[_rb(TM, D, 0), _const((1, D)), pl.BlockSpec((TM, D), lambda i: (jnp.maximum(i - 1, 0), 0))],
        out_specs=[_rb(TM, D, 0), _const((1, 128)), _const((1, D))],
        out_shape=[jax.ShapeDtypeStruct((R, D), F32), jax.ShapeDtypeStruct((1, 128), F32),
                   jax.ShapeDtypeStruct((1, D), F32)],
        compiler_params=_cp(("arbitrary",)),
    )(h2, gf, target)


def conv_bwd(dc, proj, conv_w, dproj):
    R = dc.shape[0]
    t8 = TM // 8
    nt = R // TM

    def body(dc_ref, nxt_ref, x_ref, prv_ref, w_ref, dp_in, dp_ref, dw_ref):
        del dp_in
        i = pl.program_id(0)
        dcv = dc_ref[...]
        nxt = nxt_ref[...] * (i < nt - 1).astype(F32)
        x = x_ref[...]
        prv = prv_ref[...]
        w = w_ref[...]
        dx = dcv * w[3:4, :]
        rows = [None] * 4
        rows[3] = jnp.sum(dcv * x, axis=0, keepdims=True)
        for s in (1, 2, 3):
            dx = dx + _shift_up(dcv, nxt, s) * w[3 - s:4 - s, :]
            rows[3 - s] = jnp.sum(dcv * _shift_down(x, prv, s), axis=0, keepdims=True)
        dp_ref[...] = dx.astype(dp_ref.dtype)
        part = jnp.concatenate(rows + [jnp.sum(dcv, axis=0, keepdims=True), jnp.zeros((3, 1024), F32)], axis=0)

        @pl.when(i == 0)
        def _():
            dw_ref[...] = part

        @pl.when(i > 0)
        def _():
            dw_ref[...] += part

    return pl.pallas_call(
        body, name="conv_bwd", grid=(nt,),
        in_specs=[_rb(TM, 1024, 0),
                  pl.BlockSpec((8, 1024), lambda i: (jnp.minimum((i + 1) * t8, nt * t8 - 1), 0)),
                  _rb(TM, 1024, QK0 // 1024),
                  pl.BlockSpec((8, 1024), lambda i: (jnp.maximum(i * t8 - 1, 0), QK0 // 1024)),
                  _const((4, 1024)), pl.BlockSpec(memory_space=pl.ANY)],
        out_specs=[_rb(TM, 1024, QK0 // 1024), _const((8, 1024))],
        out_shape=[jax.ShapeDtypeStruct((R, NP), MXU), jax.ShapeDtypeStruct((8, 1024), F32)],
        input_output_aliases={5: 0},
        compiler_params=_cp(("arbitrary",)),
    )(dc, dc, proj, proj, conv_w, dproj)


def small_bwd(dgl, dga, proj, gb_row, dproj):
    R = dgl.shape[0]

    def body(dgl_ref, dga_ref, sm_ref, gb_ref, dp_in, dp_ref, dgb_ref):
        del dp_in
        i = pl.program_id(0)
        z = sm_ref[...] + gb_ref[...]
        lane = lax.broadcasted_iota(jnp.int32, z.shape, 1)
        row = lax.broadcasted_iota(jnp.int32, z.shape, 0) + i * TM
        valid = row >= NPADROWS
        dgl_v = dgl_ref[...]
        dgate = jnp.where(valid, jnp.where(lane < 4, dgl_v, dgl_v * _sigmoid(-z)), 0.0)
        ds = jnp.where(lane < 8, dgate, dga_ref[...])
        dp_ref[...] = ds.astype(dp_ref.dtype)
        part = jnp.sum(jnp.where(lane < 8, dgate, 0.0), axis=0, keepdims=True)

        @pl.when(i == 0)
        def _():
            dgb_ref[...] = part

        @pl.when(i > 0)
        def _():
            dgb_ref[...] += part

    return pl.pallas_call(
        body, name="small_bwd", grid=(R // TM,),
        in_specs=[_rb(TM, 128, 0), _rb(TM, 128, 0), _rb(TM, 128, SM0 // 128), _const((1, 128)),
                  pl.BlockSpec(memory_space=pl.ANY)],
        out_specs=[_rb(TM, 128, SM0 // 128), _const((1, 128))],
        out_shape=[jax.ShapeDtypeStruct((R, NP), MXU), jax.ShapeDtypeStruct((1, 128), F32)],
        input_output_aliases={4: 0},
        compiler_params=_cp(("arbitrary",)),
    )(dgl, dga, proj, gb_row, dproj)


def _masks():
    r = lax.broadcasted_iota(jnp.int32, (L, L), 0)
    c = lax.broadcasted_iota(jnp.int32, (L, L), 1)
    return r >= c, r == c, r


def _to_row(col, eye):
    return jnp.sum(jnp.where(eye, col, 0.0), axis=0, keepdims=True)


def _to_col(row, eye):
    return jnp.sum(jnp.where(eye, row, 0.0), axis=1, keepdims=True)


def _mlstm_chunk(q, k, logi_c, logf_c, m, n):
    tril, eye, _ = _masks()
    logi_r, logf_r = _to_row(logi_c, eye), _to_row(logf_c, eye)
    b_c = jnp.sum(jnp.where(tril, logf_r, 0.0), axis=1, keepdims=True)
    b_r = _to_row(b_c, eye)
    g = jnp.sum(logf_c, axis=0, keepdims=True)
    dmat = jnp.where(tril, b_c - b_r + logi_r, NEG)
    mrow = jnp.maximum(b_c + m, jnp.max(dmat, axis=1, keepdims=True))
    dm = jnp.exp(dmat - mrow)
    s = _dot(q, k, 1, 1)
    w = dm * s
    a_in = jnp.exp(b_c + m - mrow)
    qn = jnp.sum(q * n, axis=1, keepdims=True)
    den = a_in * qn + jnp.sum(w, axis=1, keepdims=True)
    floor = jnp.exp(-mrow)
    nrm = jnp.maximum(jnp.abs(den), floor)
    wlog_c = g - b_c + logi_c
    m_new = jnp.maximum(g + m, jnp.max(wlog_c, axis=0, keepdims=True))
    a_st = jnp.exp(g + m - m_new)
    w_c = jnp.exp(wlog_c - m_new)
    return dict(b_c=b_c, g=g, dm=dm, s=s, w=w, a_in=a_in, qn=qn, den=den, floor=floor, nrm=nrm,
                m_new=m_new, a_st=a_st, w_c=w_c, tril=tril, eye=eye)


def mlstm_fwd(qk, proj, gl, head_g):
    R = qk.shape[0]
    NC = R // L

    def body(qk_ref, v_ref, gl_ref, mo_ref, hg_ref, hm_ref, ym_ref, cs_ref, nm_ref, c_s, nm_s):
        c = pl.program_id(0)

        @pl.when(c <= CH0)
        def _():
            c_s[...] = jnp.zeros_like(c_s)
            nm_s[...] = jnp.zeros_like(nm_s)

        @pl.when(c < CH0)
        def _():
            hm_ref[...] = jnp.zeros_like(hm_ref)
            ym_ref[...] = jnp.zeros_like(ym_ref)
            cs_ref[...] = jnp.zeros_like(cs_ref)
            nm_ref[...] = jnp.zeros_like(nm_ref)

        @pl.when(c >= CH0)
        def _():
            glv = gl_ref[...]
            for h in range(NH):
                q = qk_ref[:, h * DQK:(h + 1) * DQK] * QSCALE
                k = qk_ref[:, 512 + h * DQK:512 + (h + 1) * DQK]
                v = v_ref[:, h * DV:(h + 1) * DV]
                C = c_s[h]
                n = nm_s[h, 0:1, :]
                m = nm_s[h, 1:2, 0:1]
                f = _mlstm_chunk(q, k, glv[:, h:h + 1], glv[:, 4 + h:5 + h], m, n)
                num = f["a_in"] * _dot(q, C, 1, 1) + _dot(f["w"], v, 1, 0)
                hh = num / f["nrm"]
                cs_ref[0, h] = C
                nm_ref[0, h] = nm_s[h]
                c_s[h] = f["a_st"] * C + _dot(f["w_c"] * v, k, 0, 0)
                n_new = f["a_st"] * n + jnp.sum(f["w_c"] * k, axis=0, keepdims=True)
                rowi = lax.broadcasted_iota(jnp.int32, (8, DQK), 0)
                nm_s[h] = jnp.where(rowi == 0, n_new, jnp.where(rowi == 1, f["m_new"], 0.0))
                rm = lax.rsqrt(jnp.mean(hh * hh, axis=-1, keepdims=True) + EPS)
                sl = slice(h * DV, (h + 1) * DV)
                hm_ref[:, sl] = hh
                ym_ref[:, sl] = (hh * rm * hg_ref[:, sl] * _sigmoid(mo_ref[:, sl])).astype(ym_ref.dtype)

    return pl.pallas_call(
        body, name="mlstm_fwd", grid=(NC,),
        in_specs=[_rb(L, 1024, 0), _rb(L, 1024, MV0 // 1024), _rb(L, 128, 0), _rb(L, 1024, MO0 // 1024),
                  _const((1, 1024))],
        out_specs=[_rb(L, 1024, 0), _rb(L, 1024, 0),
                   pl.BlockSpec((1, NH, DV, DQK), lambda c: (c, 0, 0, 0)),
                   pl.BlockSpec((1, NH, 8, DQK), lambda c: (c, 0, 0, 0))],
        out_shape=[jax.ShapeDtypeStruct((R, 1024), F32), jax.ShapeDtypeStruct((R, 1024), MXU),
                   jax.ShapeDtypeStruct((NC, NH, DV, DQK), F32), jax.ShapeDtypeStruct((NC, NH, 8, DQK), F32)],
        scratch_shapes=[pltpu.VMEM((NH, DV, DQK), F32), pltpu.VMEM((NH, 8, DQK), F32)],
        compiler_params=_cp(("arbitrary",)),
    )(qk, proj, gl, proj, head_g)


def mlstm_bwd(dym, hm, qk, cpre, proj, gl, head_g, cs, nm, dproj):
    R = qk.shape[0]
    NC = R // L
    rev = lambda c: NC - 1 - c

    def body(dym_ref, hm_ref, qk_ref, cp_ref, v_ref, gl_ref, mo_ref, hg_ref, cs_ref, nm_ref, dp_in,
             dp_ref, dc_ref, dgl_ref, dhg_ref, dc_s, dn_s):
        del dp_in
        step = pl.program_id(0)
        c = NC - 1 - step

        @pl.when(step == 0)
        def _():
            dc_s[...] = jnp.zeros_like(dc_s)
            dn_s[...] = jnp.zeros_like(dn_s)
            dhg_ref[...] = jnp.zeros_like(dhg_ref)

        @pl.when(c < CH0)
        def _():
            dp_ref[...] = jnp.zeros_like(dp_ref)
            dc_ref[...] = jnp.zeros_like(dc_ref)
            dgl_ref[...] = jnp.zeros_like(dgl_ref)

        @pl.when(c >= CH0)
        def _():
            glv = gl_ref[...]
            lane = lax.broadcasted_iota(jnp.int32, (L, 128), 1)
            dgl = jnp.zeros((L, 128), F32)
            for h in range(NH):
                sl = slice(h * DV, (h + 1) * DV)
                sq = slice(h * DQK, (h + 1) * DQK)
                sk = slice(512 + h * DQK, 512 + (h + 1) * DQK)
                hh = hm_ref[:, sl]
                gain = hg_ref[:, sl]
                rm = lax.rsqrt(jnp.mean(hh * hh, axis=-1, keepdims=True) + EPS)
                sg = _sigmoid(mo_ref[:, sl])
                dyv = dym_ref[:, sl]
                dno = dyv * sg
                dp_ref[:, 1024 + h * DV:1024 + (h + 1) * DV] = (dyv * hh * rm * gain * sg * (1.0 - sg)).astype(dp_ref.dtype)
                dhg_ref[:, sl] += jnp.sum(dno * hh * rm, axis=0, keepdims=True)
                dnog = dno * gain
                dh = rm * dnog - hh * (rm * rm * rm * jnp.mean(dnog * hh, axis=-1, keepdims=True))
                q = qk_ref[:, sq] * QSCALE
                k = qk_ref[:, sk]
                v = v_ref[:, sl]
                C = cs_ref[0, h]
                n = nm_ref[0, h, 0:1, :]
                m = nm_ref[0, h, 1:2, 0:1]
                f = _mlstm_chunk(q, k, glv[:, h:h + 1], glv[:, 4 + h:5 + h], m, n)
                eye = f["eye"]
                a_in, nrm, den, w = f["a_in"], f["nrm"], f["den"], f["w"]
                dnum = dh / nrm
                dnrm = -jnp.sum(dh * hh, axis=1, keepdims=True) / nrm
                dden = jnp.where(jnp.abs(den) >= f["floor"], dnrm * jnp.sign(den), 0.0)
                dw = _dot(dnum, v, 1, 1) + dden
                dv = _dot(w, dnum, 0, 0)
                ds = dw * f["dm"]
                e = dw * w
                qc = _dot(q, C, 1, 1)
                dq = _dot(ds, k, 1, 0) + a_in * _dot(dnum, C, 1, 0) + (a_in * dden) * n
                dk = _dot(ds, q, 0, 0)
                dC_in = _dot(a_in * dnum, q, 0, 0)
                dn_in = jnp.sum((a_in * dden) * q, axis=0, keepdims=True)
                da_in = jnp.sum(dnum * qc, axis=1, keepdims=True) + dden * f["qn"]
                col_e = _to_col(jnp.sum(e, axis=0, keepdims=True), eye)
                db = jnp.sum(e, axis=1, keepdims=True) + da_in * a_in - col_e
                dlogi = col_e
                dCp = dc_s[h]
                dnp = dn_s[h, 0:1, :]
                a_st, w_c = f["a_st"], f["w_c"]
                da_st = (jnp.sum(jnp.sum(dCp * C, axis=1, keepdims=True), axis=0, keepdims=True)
                         + jnp.sum(dnp * n, axis=1, keepdims=True))
                vdc = _dot(v, dCp, 1, 0)
                dw_c = jnp.sum((vdc + dnp) * k, axis=1, keepdims=True)
                dv = dv + w_c * _dot(k, dCp, 1, 1)
                dk = dk + w_c * (vdc + dnp)
                fw = dw_c * w_c
                dg = jnp.sum(fw, axis=0, keepdims=True) + da_st * a_st
                db = db - fw
                dlogi = dlogi + fw
                rowc = lax.broadcasted_iota(jnp.int32, (L, 1), 0)
                db = db + jnp.where(rowc == L - 1, dg, 0.0)
                triu = lax.broadcasted_iota(jnp.int32, (L, L), 1) >= lax.broadcasted_iota(jnp.int32, (L, L), 0)
                dlogf = jnp.sum(jnp.where(triu, _to_row(db, eye), 0.0), axis=1, keepdims=True)
                dc_s[h] = a_st * dCp + dC_in
                dn_new = a_st * dnp + dn_in
                dn_s[h] = jnp.zeros((8, DQK), F32) + dn_new
                cq, ck = cp_ref[:, sq], cp_ref[:, sk]
                s_q, s_k = _sigmoid(cq), _sigmoid(ck)
                dc_ref[:, sq] = dq * QSCALE * s_q * (1.0 + cq * (1.0 - s_q))
                dc_ref[:, sk] = dk * s_k * (1.0 + ck * (1.0 - s_k))
                dp_ref[:, sl] = dv.astype(dp_ref.dtype)
                dgl = jnp.where(lane == h, dlogi, jnp.where(lane == 4 + h, dlogf, dgl))
            dgl_ref[...] = dgl

    return pl.pallas_call(
        body, name="mlstm_bwd", grid=(NC,),
        in_specs=[pl.BlockSpec((L, 1024), lambda c: (rev(c), 0)), pl.BlockSpec((L, 1024), lambda c: (rev(c), 0)),
                  pl.BlockSpec((L, 1024), lambda c: (rev(c), 0)), pl.BlockSpec((L, 1024), lambda c: (rev(c), 0)),
                  pl.BlockSpec((L, 1024), lambda c: (rev(c), MV0 // 1024)),
                  pl.BlockSpec((L, 128), lambda c: (rev(c), 0)),
                  pl.BlockSpec((L, 1024), lambda c: (rev(c), MO0 // 1024)), _const((1, 1024)),
                  pl.BlockSpec((1, NH, DV, DQK), lambda c: (rev(c), 0, 0, 0)),
                  pl.BlockSpec((1, NH, 8, DQK), lambda c: (rev(c), 0, 0, 0)),
                  pl.BlockSpec(memory_space=pl.ANY)],
        out_specs=[pl.BlockSpec((L, 2048), lambda c: (rev(c), 0)), pl.BlockSpec((L, 1024), lambda c: (rev(c), 0)),
                   pl.BlockSpec((L, 128), lambda c: (rev(c), 0)), _const((1, 1024))],
        out_shape=[jax.ShapeDtypeStruct((R, NP), MXU), jax.ShapeDtypeStruct((R, 1024), F32),
                   jax.ShapeDtypeStruct((R, 128), F32), jax.ShapeDtypeStruct((1, 1024), F32)],
        scratch_shapes=[pltpu.VMEM((NH, DV, DQK), F32), pltpu.VMEM((NH, 8, DQK), F32)],
        input_output_aliases={10: 0},
        compiler_params=_cp(("arbitrary",)),
    )(dym, hm, qk, cpre, proj, gl, proj, head_g, cs, nm, dproj)


def _gla_logs(sm, a2p, b2, valid):
    za = _dot(sm, a2p, 1, 0) + b2
    return za, jnp.where(valid, _log_sigmoid(za) * (1.0 / TAU), 0.0)


def _valid_rows(c, width):
    row = lax.broadcasted_iota(jnp.int32, (L, width), 0) + c * L
    return row >= NPADROWS


def _gla_chunk(q, k, la):
    tril, _, _ = _masks()
    bc = _dot_exact(tril.astype(F32), la)
    btot = jnp.sum(la, axis=0, keepdims=True)
    ebc = jnp.exp(bc)
    qd = q * ebc
    ki = k * jnp.exp(-bc)
    ke = k * jnp.exp(btot - bc)
    att = jnp.where(tril, _dot(qd, ki, 1, 1), 0.0)
    return dict(tril=tril, bc=bc, btot=btot, ebc=ebc, qd=qd, ki=ki, ke=ke, att=att)


def _col128(row):
    r = lax.broadcasted_iota(jnp.int32, (DQK, DQK), 0)
    c = lax.broadcasted_iota(jnp.int32, (DQK, DQK), 1)
    return jnp.sum(jnp.where(r == c, row, 0.0), axis=1, keepdims=True)


def gla_fwd(proj, a2p, b2, head_g):
    R = proj.shape[0]
    NC = R // L

    def body(q_ref, k_ref, v_ref, gr_ref, sm_ref, a2_ref, b2_ref, hg_ref, hgl_ref, yg_ref, ss_ref, s_s):
        c = pl.program_id(0)

        @pl.when(c <= CH0)
        def _():
            s_s[...] = jnp.zeros_like(s_s)

        @pl.when(c < CH0)
        def _():
            hgl_ref[...] = jnp.zeros_like(hgl_ref)
            yg_ref[...] = jnp.zeros_like(yg_ref)
            ss_ref[...] = jnp.zeros_like(ss_ref)

        @pl.when(c >= CH0)
        def _():
            _, loga = _gla_logs(sm_ref[...], a2_ref[...], b2_ref[...], _valid_rows(c, 512))
            for h in range(NH):
                sq = slice(h * DQK, (h + 1) * DQK)
                sl = slice(h * DV, (h + 1) * DV)
                q = q_ref[:, sq] * QSCALE
                k = k_ref[:, sq]
                v = v_ref[:, sl]
                S = s_s[h]
                f = _gla_chunk(q, k, loga[:, sq])
                o = _dot(f["att"], v, 1, 0) + _dot(f["qd"], S, 1, 0)
                ss_ref[0, h] = S
                s_s[h] = _col128(jnp.exp(f["btot"])) * S + _dot(f["ke"], v, 0, 0)
                rg = lax.rsqrt(jnp.mean(o * o, axis=-1, keepdims=True) + EPS)
                gr = gr_ref[:, sl]
                hgl_ref[:, sl] = o
                yg_ref[:, sl] = (o * rg * hg_ref[:, sl] * gr * _sigmoid(gr)).astype(yg_ref.dtype)

    return pl.pallas_call(
        body, name="gla_fwd", grid=(NC,),
        in_specs=[_rb(L, 512, GQ0 // 512), _rb(L, 512, GK0 // 512), _rb(L, 1024, GV0 // 1024),
                  _rb(L, 1024, GR0 // 1024), _rb(L, 128, SM0 // 128), _const((128, 512)), _const((1, 512)),
                  _const((1, 1024))],
        out_specs=[_rb(L, 1024, 0), _rb(L, 1024, 0), pl.BlockSpec((1, NH, DQK, DV), lambda c: (c, 0, 0, 0))],
        out_shape=[jax.ShapeDtypeStruct((R, 1024), F32), jax.ShapeDtypeStruct((R, 1024), MXU),
                   jax.ShapeDtypeStruct((NC, NH, DQK, DV), F32)],
        scratch_shapes=[pltpu.VMEM((NH, DQK, DV), F32)],
        compiler_params=_cp(("arbitrary",)),
    )(proj, proj, proj, proj, proj, a2p, b2, head_g)


def gla_bwd(dyg, hgl, proj, a2p, b2, head_g, ss, dproj):
    R = proj.shape[0]
    NC = R // L
    rev = lambda c: NC - 1 - c

    def body(dy_ref, ho_ref, q_ref, k_ref, v_ref, gr_ref, sm_ref, a2_ref, b2_ref, hg_ref, ss_ref, dp_in,
             dp_ref, dga_ref, da2_ref, db2_ref, dhg_ref, ds_s):
        del dp_in
        step = pl.program_id(0)
        c = NC - 1 - step

        @pl.when(step == 0)
        def _():
            ds_s[...] = jnp.zeros_like(ds_s)
            da2_ref[...] = jnp.zeros_like(da2_ref)
            db2_ref[...] = jnp.zeros_like(db2_ref)
            dhg_ref[...] = jnp.zeros_like(dhg_ref)

        @pl.when(c < CH0)
        def _():
            dp_ref[...] = jnp.zeros_like(dp_ref)
            dga_ref[...] = jnp.zeros_like(dga_ref)

        @pl.when(c >= CH0)
        def _():
            valid = _valid_rows(c, 512)
            sm = sm_ref[...]
            za, loga = _gla_logs(sm, a2_ref[...], b2_ref[...], valid)
            dloga = []
            for h in range(NH):
                sq = slice(h * DQK, (h + 1) * DQK)
                sl = slice(h * DV, (h + 1) * DV)
                o = ho_ref[:, sl]
                gain = hg_ref[:, sl]
                rg = lax.rsqrt(jnp.mean(o * o, axis=-1, keepdims=True) + EPS)
                gr = gr_ref[:, sl]
                sg = _sigmoid(gr)
                dyv = dy_ref[:, sl]
                dno = dyv * gr * sg
                dp_ref[:, 2048 + h * DV:2048 + (h + 1) * DV] = (
                    dyv * o * rg * gain * sg * (1.0 + gr * (1.0 - sg))).astype(dp_ref.dtype)
                dhg_ref[:, sl] += jnp.sum(dno * o * rg, axis=0, keepdims=True)
                dnog = dno * gain
                do = rg * dnog - o * (rg * rg * rg * jnp.mean(dnog * o, axis=-1, keepdims=True))
                q = q_ref[:, sq] * QSCALE
                k = k_ref[:, sq]
                v = v_ref[:, sl]
                S = ss_ref[0, h]
                f = _gla_chunk(q, k, loga[:, sq])
                tril, qd, ki, ke = f["tril"], f["qd"], f["ki"], f["ke"]
                dSp = ds_s[h]
                datt = jnp.where(tril, _dot(do, v, 1, 1), 0.0)
                dqd = _dot(do, S, 1, 1) + _dot(datt, ki, 1, 0)
                dki = _dot(datt, qd, 0, 0)
                dv = _dot(f["att"], do, 0, 0) + _dot(ke, dSp, 1, 0)
                dke = _dot(v, dSp, 1, 1)
                ebt = jnp.exp(f["btot"])
                dbtot = jnp.sum(dke * ke, axis=0, keepdims=True) + ebt * _to_row128(jnp.sum(dSp * S, axis=1, keepdims=True))
                ds_s[h] = _dot(qd, do, 0, 0) + _col128(ebt) * dSp
                dq = dqd * f["ebc"]
                dk = dki * jnp.exp(-f["bc"]) + dke * jnp.exp(f["btot"] - f["bc"])
                dbc = dqd * qd - dki * ki - dke * ke
                rowc = lax.broadcasted_iota(jnp.int32, (L, DQK), 0)
                dbc = dbc + jnp.where(rowc == L - 1, dbtot, 0.0)
                triu = lax.broadcasted_iota(jnp.int32, (L, L), 1) >= lax.broadcasted_iota(jnp.int32, (L, L), 0)
                dloga.append(_dot_exact(triu.astype(F32), dbc))
                dp_ref[:, sq] = (dq * QSCALE).astype(dp_ref.dtype)
                dp_ref[:, 512 + h * DQK:512 + (h + 1) * DQK] = dk.astype(dp_ref.dtype)
                dp_ref[:, 1024 + h * DV:1024 + (h + 1) * DV] = dv.astype(dp_ref.dtype)
            dza = jnp.where(valid, jnp.concatenate(dloga, axis=1) * (1.0 / TAU) * _sigmoid(-za), 0.0)
            dga_ref[...] = _dot(dza, a2_ref[...], 1, 1)
            da2_ref[...] += _dot(sm, dza, 0, 0)
            db2_ref[...] += jnp.sum(dza, axis=0, keepdims=True)

    return pl.pallas_call(
        body, name="gla_bwd", grid=(NC,),
        in_specs=[pl.BlockSpec((L, 1024), lambda c: (rev(c), 0)), pl.BlockSpec((L, 1024), lambda c: (rev(c), 0)),
                  pl.BlockSpec((L, 512), lambda c: (rev(c), GQ0 // 512)),
                  pl.BlockSpec((L, 512), lambda c: (rev(c), GK0 // 512)),
                  pl.BlockSpec((L, 1024), lambda c: (rev(c), GV0 // 1024)),
                  pl.BlockSpec((L, 1024), lambda c: (rev(c), GR0 // 1024)),
                  pl.BlockSpec((L, 128), lambda c: (rev(c), SM0 // 128)),
                  _const((128, 512)), _const((1, 512)), _const((1, 1024)),
                  pl.BlockSpec((1, NH, DQK, DV), lambda c: (rev(c), 0, 0, 0)),
                  pl.BlockSpec(memory_space=pl.ANY)],
        out_specs=[pl.BlockSpec((L, 3072), lambda c: (rev(c), GQ0 // 3072)),
                   pl.BlockSpec((L, 128), lambda c: (rev(c), 0)),
                   _const((128, 512)), _const((1, 512)), _const((1, 1024))],
        out_shape=[jax.ShapeDtypeStruct((R, NP), MXU), jax.ShapeDtypeStruct((R, 128), F32),
                   jax.ShapeDtypeStruct((128, 512), F32), jax.ShapeDtypeStruct((1, 512), F32),
                   jax.ShapeDtypeStruct((1, 1024), F32)],
        scratch_shapes=[pltpu.VMEM((NH, DQK, DV), F32)],
        input_output_aliases={11: 0},
        compiler_params=_cp(("arbitrary",)),
    )(dyg, hgl, proj, proj, proj, proj, proj, a2p, b2, head_g, ss, dproj)


def _to_row128(col):
    r = lax.broadcasted_iota(jnp.int32, (DQK, DQK), 0)
    c = lax.broadcasted_iota(jnp.int32, (DQK, DQK), 1)
    return jnp.sum(jnp.where(r == c, col, 0.0), axis=0, keepdims=True)


def local_step(x, target, meta, norm1_g, wp, conv_w, conv_b, m_gate_b, g_a2, g_a2_b, m_head_g, g_head_g,
               w_bm, w_bg, w_out, norm2_g, w_gu, w_down, final_g):
    seq = x.shape[0]
    assert seq % TM == 0
    h0 = jnp.concatenate([jnp.zeros((NPADROWS, D), F32), meta, x], axis=0)
    gb_row = jnp.zeros((1, 128), F32).at[0, 0:8].set(m_gate_b.reshape(8))
    a2p = jnp.zeros((128, 512), F32).at[8:8 + RANK].set(g_a2)
    mhg = m_head_g.reshape(1, 1024)
    ghg = g_head_g.reshape(1, 1024)

    xn = rms_fwd(h0, norm1_g, "rms1_fwd")
    proj = matmul(xn, wp, "nn", "proj_fwd", tm=1536)
    cpre, qk, gl = prep_fwd(proj, conv_w, conv_b, gb_row)
    hm, ym, cs, nm = mlstm_fwd(qk, proj, gl, mhg)
    hgl, yg, ss = gla_fwd(proj, a2p, g_a2_b, ghg)
    bm = matmul(ym, w_bm, "nn", "branch_m_fwd")
    bg = matmul(yg, w_bg, "nn", "branch_g_fwd")
    merged = merge_fwd(bm, bg, proj)
    h1 = matmul(merged, w_out, "nn", "out_fwd", add=h0)
    hn = rms_fwd(h1, norm2_g, "rms2_fwd")
    au = matmul(hn, w_gu, "nn", "ff_in_fwd", tm=1536, tn=1408)
    ff = swiglu_fwd(au)
    h2 = matmul(ff, w_down, "nn", "ff_down_fwd", add=h1, tm=768, tk=2816)
    dh2, loss, d_final_g = final_loss(h2, final_g.reshape(1, D), target)

    d_w_down = matmul(ff, dh2, "tn", "ff_down_wgrad", tm=1408, tk=1536)
    dff = matmul(dh2, w_down, "nt", "ff_down_dgrad", tm=1536, tn=1408)
    dau = swiglu_bwd(au, dff)
    d_w_gu = matmul(hn, dau, "tn", "ff_in_wgrad", tm=1024, tn=1408, tk=1536)
    dhn = matmul(dau, w_gu, "nt", "ff_in_dgrad", tm=1536, tk=1408)
    dh1, d_norm2_g = rms_bwd(dhn, h1, norm2_g, dh2, "rms2_bwd")

    d_w_out = matmul(merged, dh1, "tn", "out_wgrad", tm=1024, tk=1536)
    dmerged = matmul(dh1, w_out, "nt", "out_dgrad")
    dbm, dbg, dproj = merge_bwd(dmerged, bm, bg, proj)
    d_w_bm = matmul(ym, dbm, "tn", "branch_m_wgrad", tm=1024, tk=1536)
    d_w_bg = matmul(yg, dbg, "tn", "branch_g_wgrad", tm=1024, tk=1536)
    dym = matmul(dbm, w_bm, "nt", "branch_m_dgrad")
    dyg = matmul(dbg, w_bg, "nt", "branch_g_dgrad")
    dproj, dc, dgl, d_mhg = mlstm_bwd(dym, hm, qk, cpre, proj, gl, mhg, cs, nm, dproj)
    dproj, dga, d_a2p, d_a2b, d_ghg = gla_bwd(dyg, hgl, proj, a2p, g_a2_b, ghg, ss, dproj)
    dproj, d_conv = conv_bwd(dc, proj, conv_w, dproj)
    dproj, d_gb = small_bwd(dgl, dga, proj, gb_row, dproj)
    d_wp = matmul(xn, dproj, "tn", "proj_wgrad", tm=1024, tn=1664, tk=1536)
    dxn = matmul(dproj, wp, "nt", "proj_dgrad", tm=1536, tk=1664)
    dh0, d_norm1_g = rms_bwd(dxn, h0, norm1_g, dh1, "rms1_bwd")

    grads = dict(
        norm1_g=d_norm1_g, wp=d_wp, conv_w=d_conv[0:4], conv_b=d_conv[4:5], m_gate_b=d_gb[0, 0:8].reshape(1, 2, 4),
        g_a2=d_a2p[8:8 + RANK], g_a2_b=d_a2b, m_head_g=d_mhg.reshape(NH, DV), g_head_g=d_ghg.reshape(NH, DV),
        w_branch_m=d_w_bm, w_branch_g=d_w_bg, w_out=d_w_out, norm2_g=d_norm2_g, w_gu=d_w_gu, w_ff_down=d_w_down,
        final_g=d_final_g)
    return loss, dh0, grads


_SEGS = [(0, 1024, QK0), (1024, 2048, MV0), (2048, 2056, SM0), (2056, 3080, MO0), (3080, 5128, GQ0),
         (5128, 5144, SM0 + 8), (5144, 8216, GR0)]
SHARD_W = NPROJ // NDEV


def regroup_cols(w8):
    parts = []
    for lo, hi, _ in sorted(_SEGS, key=lambda s: s[2]):
        while lo < hi:
            j = lo // SHARD_W
            end = min(hi, (j + 1) * SHARD_W)
            parts.append(w8[j, :, lo - j * SHARD_W:end - j * SHARD_W])
            lo = end
    parts.append(jnp.zeros((w8.shape[1], NP - NPROJ), w8.dtype))
    return jnp.concatenate(parts, axis=1)


def ungroup_cols(g):
    blocks = []
    for j in range(NDEV):
        lo, hi = j * SHARD_W, (j + 1) * SHARD_W
        parts = []
        for s_lo, s_hi, s_at in _SEGS:
            a, b = max(lo, s_lo), min(hi, s_hi)
            if a < b:
                parts.append(g[:, s_at + a - s_lo:s_at + b - s_lo])
        blocks.append(jnp.concatenate(parts, axis=1))
    return jnp.stack(blocks)


def col_blocks(g):
    r, c8 = g.shape
    return jnp.transpose(g.reshape(r, NDEV, c8 // NDEV), (1, 0, 2))


def from_col_blocks(g8):
    n, r, c = g8.shape
    return jnp.transpose(g8, (1, 0, 2)).reshape(r, n * c)


_MESHID = pl.DeviceIdType.MESH
_RELS = [(0, 0, 1), (1, 0, 0), (0, 1, 0), (1, 1, 0), (1, 0, 1), (0, 1, 1), (1, 1, 1)]


def _flip(v, bit):
    return 1 - v if bit else v


def all_gather(arrs, name):
    n = len(arrs)

    def body(*refs):
        ins, outs = refs[:n], refs[n:2 * n]
        send_sems, recv_sems, local_sems = refs[2 * n:]
        x, y, c = lax.axis_index("x"), lax.axis_index("y"), lax.axis_index("c")
        me, sibling = (x, y, c), (x, y, 1 - c)
        chips = [(1 - x, y), (x, 1 - y), (1 - x, 1 - y)]

        def slot(p):
            return 4 * p[0] + 2 * p[1] + p[2]

        def copy(a, k, block, to, src=None):
            dst = outs[a].at[slot(block)]
            return pltpu.make_async_remote_copy(
                src_ref=dst if src is None else src, dst_ref=dst,
                send_sem=send_sems.at[a, k], recv_sem=recv_sems.at[a, k],
                device_id=to, device_id_type=_MESHID)

        mine = [pltpu.make_async_copy(ins[a], outs[a].at[slot(me)], local_sems.at[a]) for a in range(n)]
        for cp in mine:
            cp.start()
        first = []
        for a in range(n):
            first.append(copy(a, 0, me, sibling, src=ins[a]))
            first += [copy(a, 1 + j, me, (*chip, c), src=ins[a]) for j, chip in enumerate(chips)]
        for cp in first:
            cp.start()
        passed = []
        for j, chip in enumerate(chips):
            for a in range(n):
                copy(a, 1 + j, (*chip, c), me).wait_recv()
                fwd = copy(a, 4 + j, (*chip, c), sibling)
                fwd.start()
                passed.append(fwd)
        for a in range(n):
            copy(a, 0, sibling, me).wait_recv()
            for j, chip in enumerate(chips):
                copy(a, 4 + j, (*chip, 1 - c), me).wait_recv()
        for cp in first + passed:
            cp.wait_send()
        for cp in mine:
            cp.wait()

    anyspec = pl.BlockSpec(memory_space=pl.ANY)
    return pl.pallas_call(
        body, name=name,
        in_specs=[anyspec] * n, out_specs=[anyspec] * n,
        out_shape=[jax.ShapeDtypeStruct((NDEV,) + a.shape, a.dtype) for a in arrs],
        scratch_shapes=[pltpu.SemaphoreType.DMA((n, 7)), pltpu.SemaphoreType.DMA((n, 7)),
                        pltpu.SemaphoreType.DMA((n,))],
    )(*arrs)


def exchange(blocks, rep, name):
    n = len(blocks)

    def body(*refs):
        b_refs, r_ref = refs[:n], refs[n]
        ob_refs, or_ref = refs[n + 1:2 * n + 1], refs[2 * n + 1]
        send_sems, recv_sems, local_sems = refs[2 * n + 2:]
        x, y, c = lax.axis_index("x"), lax.axis_index("y"), lax.axis_index("c")
        me = 4 * x + 2 * y + c

        def pairs(src_slot, dst_slot):
            return [(b_refs[a].at[src_slot], ob_refs[a].at[dst_slot]) for a in range(n)] + [(r_ref, or_ref.at[dst_slot])]

        loc = [pltpu.make_async_copy(s, d, local_sems.at[a]) for a, (s, d) in enumerate(pairs(me, me))]
        for cp in loc:
            cp.start()
        sends = []
        for k, (fx, fy, fc) in enumerate(_RELS):
            peer = (_flip(x, fx), _flip(y, fy), _flip(c, fc))
            pid = 4 * peer[0] + 2 * peer[1] + peer[2]
            for a, (s, d) in enumerate(pairs(pid, me)):
                sends.append(pltpu.make_async_remote_copy(
                    src_ref=s, dst_ref=d, send_sem=send_sems.at[a, k], recv_sem=recv_sems.at[a, k],
                    device_id=peer, device_id_type=_MESHID))
        for cp in sends:
            cp.start()
        for k, (fx, fy, fc) in enumerate(_RELS):
            peer = (_flip(x, fx), _flip(y, fy), _flip(c, fc))
            pid = 4 * peer[0] + 2 * peer[1] + peer[2]
            for a, (s, d) in enumerate(pairs(pid, pid)):
                pltpu.make_async_remote_copy(
                    src_ref=s, dst_ref=d, send_sem=send_sems.at[a, k], recv_sem=recv_sems.at[a, k],
                    device_id=peer, device_id_type=_MESHID).wait_recv()
        for cp in sends:
            cp.wait_send()
        for cp in loc:
            cp.wait()

    anyspec = pl.BlockSpec(memory_space=pl.ANY)
    return pl.pallas_call(
        body, name=name,
        in_specs=[anyspec] * (n + 1), out_specs=[anyspec] * (n + 1),
        out_shape=[jax.ShapeDtypeStruct(b.shape, b.dtype) for b in blocks]
        + [jax.ShapeDtypeStruct((NDEV,) + rep.shape, rep.dtype)],
        scratch_shapes=[pltpu.SemaphoreType.DMA((n + 1, 7)), pltpu.SemaphoreType.DMA((n + 1, 7)),
                        pltpu.SemaphoreType.DMA((n + 1,))],
    )(*blocks, rep)


def adamw(parts, w, m, v, name, tr):
    npart, r, c = parts.shape
    c1 = 1.0 - ADAM_B1 ** ADAM_STEP
    c2 = 1.0 - ADAM_B2 ** ADAM_STEP

    def body(p_ref, w_ref, m_ref, v_ref, g_ref, d_ref, nm_ref, nv_ref):
        g = p_ref[0].astype(F32)
        for j in range(1, npart):
            g = g + p_ref[j].astype(F32)
        mn = ADAM_B1 * m_ref[...] + (1.0 - ADAM_B1) * g
        vn = ADAM_B2 * v_ref[...] + (1.0 - ADAM_B2) * (g * g)
        g_ref[...] = g
        nm_ref[...] = mn
        nv_ref[...] = vn
        d_ref[...] = -ADAM_LR * ((mn / c1) / (jnp.sqrt(vn / c2) + ADAM_EPS) + ADAM_WD * w_ref[...])

    spec = _rb(tr, c, 0)
    return pl.pallas_call(
        body, name=name, grid=(r // tr,),
        in_specs=[pl.BlockSpec((npart, tr, c), lambda i: (0, i, 0)), spec, spec, spec],
        out_specs=[spec] * 4, out_shape=[jax.ShapeDtypeStruct((r, c), F32)] * 4,
        compiler_params=_cp(("parallel",)),
    )(parts, w, m, v)


def sum_parts(parts, name, tc):
    npart, r, c = parts.shape

    def body(p_ref, o_ref):
        g = p_ref[0].astype(F32)
        for j in range(1, npart):
            g = g + p_ref[j].astype(F32)
        o_ref[...] = g

    return pl.pallas_call(
        body, name=name, grid=(c // tc,),
        in_specs=[pl.BlockSpec((npart, r, tc), lambda i: (0, 0, i))],
        out_specs=pl.BlockSpec((r, tc), lambda i: (0, i)),
        out_shape=jax.ShapeDtypeStruct((r, c), F32),
        compiler_params=_cp(("parallel",)),
    )(parts)


TINY = [("meta_tokens", (16, 1024)), ("conv_w", (4, 1024)), ("g_a2", (16, 512)), ("m_head_g", (4, 256)),
        ("g_head_g", (4, 256))]
REPL = [("norm1_g", (1, 1024)), ("conv_b", (1, 1024)), ("m_gate_b", (1, 2, 4)), ("g_a2_b", (1, 512)),
        ("norm2_g", (1, 1024)), ("final_g", (1024,))]
TINY_SIZE = 16 * 1024 + 4 * 1024 + 16 * 512 + 2 * 4 * 256
REPL_SIZE = 1024 + 1024 + 8 + 512 + 1024 + 1024
ROWS_GATHER = 8
ROWS_REP = 40
ROWS_OWN = 16


def pack_rows(vecs, rows):
    flat = jnp.concatenate([v.reshape(-1) for v in vecs])
    return jnp.pad(flat, (0, rows * 1024 - flat.shape[0])).reshape(rows, 1024)


def unpack_rows(packed, shapes):
    flat = packed.reshape(-1)
    out, off = [], 0
    for s in shapes:
        n = 1
        for d in s:
            n *= d
        out.append(flat[off:off + n].reshape(s))
        off += n
    return out


def kernel(x, meta_tokens, norm1_g, w_in, conv_w, conv_b, m_gate_b, g_a2, g_a2_b, m_head_g, g_head_g, w_branch_m, w_branch_g, w_out, norm2_g, w_ff_gate, w_ff_up, w_ff_down, final_g, loss_target, m_meta_tokens, m_norm1_g, m_w_in, m_conv_w, m_conv_b, m_m_gate_b, m_g_a2, m_g_a2_b, m_m_head_g, m_g_head_g, m_w_branch_m, m_w_branch_g, m_w_out, m_norm2_g, m_w_ff_gate, m_w_ff_up, m_w_ff_down, m_final_g, v_meta_tokens, v_norm1_g, v_w_in, v_conv_w, v_conv_b, v_m_gate_b, v_g_a2, v_g_a2_b, v_m_head_g, v_g_head_g, v_w_branch_m, v_w_branch_g, v_w_out, v_norm2_g, v_w_ff_gate, v_w_ff_up, v_w_ff_down, v_final_g):
    w_sh = dict(meta_tokens=meta_tokens, w_in=w_in[0], conv_w=conv_w[0], g_a2=g_a2[0], m_head_g=m_head_g[0],
                g_head_g=g_head_g[0], w_branch_m=w_branch_m[0], w_branch_g=w_branch_g[0], w_out=w_out[0],
                w_ff_gate=w_ff_gate[0], w_ff_up=w_ff_up[0], w_ff_down=w_ff_down[0])
    m_sh = dict(meta_tokens=m_meta_tokens, w_in=m_w_in[0], conv_w=m_conv_w[0], g_a2=m_g_a2[0],
                m_head_g=m_m_head_g[0], g_head_g=m_g_head_g[0], w_branch_m=m_w_branch_m[0],
                w_branch_g=m_w_branch_g[0], w_out=m_w_out[0], w_ff_gate=m_w_ff_gate[0], w_ff_up=m_w_ff_up[0],
                w_ff_down=m_w_ff_down[0])
    v_sh = dict(meta_tokens=v_meta_tokens, w_in=v_w_in[0], conv_w=v_conv_w[0], g_a2=v_g_a2[0],
                m_head_g=v_m_head_g[0], g_head_g=v_g_head_g[0], w_branch_m=v_w_branch_m[0],
                w_branch_g=v_w_branch_g[0], w_out=v_w_out[0], w_ff_gate=v_w_ff_gate[0], w_ff_up=v_w_ff_up[0],
                w_ff_down=v_w_ff_down[0])
    w_rep = dict(norm1_g=norm1_g, conv_b=conv_b, m_gate_b=m_gate_b, g_a2_b=g_a2_b, norm2_g=norm2_g, final_g=final_g)
    m_rep = dict(norm1_g=m_norm1_g, conv_b=m_conv_b, m_gate_b=m_m_gate_b, g_a2_b=m_g_a2_b, norm2_g=m_norm2_g,
                 final_g=m_final_g)
    v_rep = dict(norm1_g=v_norm1_g, conv_b=v_conv_b, m_gate_b=v_m_gate_b, g_a2_b=v_g_a2_b, norm2_g=v_norm2_g,
                 final_g=v_final_g)
    dev = 4 * lax.axis_index("x") + 2 * lax.axis_index("y") + lax.axis_index("c")
    tiny_names = [n for n, _ in TINY]
    repl_names = [n for n, _ in REPL]
    tiny_shard_shapes = [(s[0], s[1] // NDEV) for _, s in TINY]

    send = [w_sh["w_in"].astype(MXU), w_sh["w_branch_m"].astype(MXU), w_sh["w_branch_g"].astype(MXU),
            w_sh["w_out"].astype(MXU), w_sh["w_ff_gate"].astype(MXU), w_sh["w_ff_up"].astype(MXU),
            w_sh["w_ff_down"].astype(MXU), pack_rows([w_sh[n] for n in tiny_names], ROWS_GATHER)]
    in8, bm8, bg8, out8, ffg8, ffu8, ffd8, tiny8 = all_gather(send, "param_all_gather")
    wp = regroup_cols(in8)
    w_gu = jnp.concatenate([from_col_blocks(ffg8), from_col_blocks(ffu8)], axis=1)
    tiny_full = {}
    for j in range(NDEV):
        for name, blk in zip(tiny_names, unpack_rows(tiny8[j], tiny_shard_shapes)):
            tiny_full.setdefault(name, []).append(blk)
    tiny_full = {n: jnp.concatenate(v, axis=1) for n, v in tiny_full.items()}

    loss, dh0, g = local_step(
        x[0], loss_target[0], tiny_full["meta_tokens"], norm1_g, wp, tiny_full["conv_w"], conv_b, m_gate_b[0],
        tiny_full["g_a2"], g_a2_b, tiny_full["m_head_g"], tiny_full["g_head_g"], bm8.reshape(D, D),
        bg8.reshape(D, D), out8.reshape(D, D), norm2_g, w_gu, ffd8.reshape(DFF, D), final_g)

    g["meta_tokens"] = dh0[NPADROWS:PADR]
    blocks = [ungroup_cols(g["wp"]).astype(WIRE), g["w_branch_m"].reshape(NDEV, D // NDEV, D).astype(WIRE),
              g["w_branch_g"].reshape(NDEV, D // NDEV, D).astype(WIRE), g["w_out"].reshape(NDEV, D // NDEV, D).astype(WIRE),
              col_blocks(g["w_gu"][:, :DFF]).astype(WIRE), col_blocks(g["w_gu"][:, DFF:]).astype(WIRE),
              g["w_ff_down"].reshape(NDEV, DFF // NDEV, D).astype(WIRE)]
    rep = pack_rows([g[n] for n in tiny_names + repl_names] + [loss[0, 0:1]], ROWS_REP)
    *got, got_rep = exchange(blocks, rep, "grad_exchange")

    result = {}

    def update(name, parts, tr):
        outs = adamw(parts, w_sh[name], m_sh[name], v_sh[name], "adamw_" + name, tr)
        for kind, arr in zip(("grad", "delta", "new_m", "new_v"), outs):
            result[kind, name] = arr[None]

    update("w_in", got[0], 128)
    update("w_branch_m", got[1], 128)
    update("w_branch_g", got[2], 128)
    update("w_out", got[3], 128)
    update("w_ff_gate", got[4], 256)
    update("w_ff_up", got[5], 256)
    update("w_ff_down", got[6], DFF // NDEV)

    rep_sum = sum_parts(got_rep, "sum_small", 1024)
    rep_g = unpack_rows(rep_sum, [s for _, s in TINY] + [s for _, s in REPL] + [(1,)])
    own_g = [lax.dynamic_slice_in_dim(gf, dev * ss[1], ss[1], axis=1) for gf, ss in zip(rep_g, tiny_shard_shapes)]
    own_g += rep_g[len(TINY):len(TINY) + len(REPL)]
    w_all = {**w_sh, **w_rep}
    m_all = {**m_sh, **m_rep}
    v_all = {**v_sh, **v_rep}
    names = tiny_names + repl_names
    outs = adamw(pack_rows(own_g, ROWS_OWN)[None], pack_rows([w_all[n] for n in names], ROWS_OWN),
                 pack_rows([m_all[n] for n in names], ROWS_OWN), pack_rows([v_all[n] for n in names], ROWS_OWN),
                 "adamw_small", ROWS_OWN)
    shapes = tiny_shard_shapes + [s for _, s in REPL]
    for kind, packed in zip(("grad", "delta", "new_m", "new_v"), outs):
        for name, arr in zip(names, unpack_rows(packed, shapes)):
            result[kind, name] = arr[None] if name in tiny_names and name != "meta_tokens" else arr
    loss_total = rep_g[-1][0]
    order = ["meta_tokens", "norm1_g", "w_in", "conv_w", "conv_b", "m_gate_b", "g_a2", "g_a2_b", "m_head_g", "g_head_g",
             "w_branch_m", "w_branch_g", "w_out", "norm2_g", "w_ff_gate", "w_ff_up", "w_ff_down", "final_g"]
    grad_x = dh0[PADR:][None]
    return (loss_total, grad_x, *[result[kind, n] for kind in ("grad", "delta", "new_m", "new_v") for n in order])
```

```python
import functools

import jax
import jax.numpy as jnp
from jax import lax
from jax.experimental import pallas as pl
from jax.experimental.pallas import tpu as pltpu

F32 = jnp.float32
MXU = jnp.bfloat16
WIRE = jnp.bfloat16

D = 1024
NH = 4
DV = 256
DQK = 128
L = 64
NMETA = 16
PADR = 512
CH0 = PADR // L - 1
NPADROWS = PADR - NMETA
RANK = 16
DFF = 2816
EPS = 1e-6
TAU = 16.0
QSCALE = DQK ** -0.5
NEG = -1e30
NDEV = 8

MV0, MO0, GQ0, GK0, GV0, GR0, QK0, GM0, GG0, SM0 = 0, 1024, 2048, 2560, 3072, 4096, 5120, 6144, 7168, 8192
NP = 8320
NPROJ = 8216

ADAM_LR, ADAM_B1, ADAM_B2, ADAM_EPS, ADAM_WD, ADAM_STEP = 0.001, 0.9, 0.999, 1e-08, 0.01, 10

VMEM_LIMIT = 56 * 1024 * 1024
TM = 512


def _cp(sem):
    return pltpu.CompilerParams(dimension_semantics=sem, vmem_limit_bytes=VMEM_LIMIT)


def _sigmoid(x):
    return 1.0 / (1.0 + jnp.exp(-x))


def _log_sigmoid(x):
    return jnp.minimum(x, 0.0) - jnp.log1p(jnp.exp(-jnp.abs(x)))


def _dot(a, b, ca, cb):
    return lax.dot_general(a.astype(MXU), b.astype(MXU), (((ca,), (cb,)), ((), ())), preferred_element_type=F32)


def _dot_exact(a, b):
    return lax.dot_general(a, b, (((1,), (0,)), ((), ())), precision=lax.Precision.HIGHEST,
                           preferred_element_type=F32)


def _rb(tm, w, cb):
    return pl.BlockSpec((tm, w), lambda i: (i, cb))


def _const(shape):
    nd = len(shape)
    return pl.BlockSpec(shape, lambda i: (0,) * nd)


def _pick(n, target):
    if n <= target:
        return n
    best = None
    for t in range(128, target + 1, 128):
        if n % t == 0:
            best = t
    assert best is not None, (n, target)
    return best


def matmul(a, b, mode, name, add=None, out_dtype=F32, tm=512, tn=1664, tk=1024):
    if mode == "nn":
        (M, K), (K2, N) = a.shape, b.shape
    elif mode == "nt":
        (M, K), (N, K2) = a.shape, b.shape
    else:
        (K, M), (K2, N) = a.shape, b.shape
    assert K == K2, (a.shape, b.shape, mode)
    tm, tn, tk = _pick(M, tm), _pick(N, tn), _pick(K, tk)
    nk = K // tk
    assert nk == 1 or out_dtype == F32
    ca, cb = {"nn": (1, 0), "nt": (1, 1), "tn": (0, 0)}[mode]
    a_spec = {"nn": pl.BlockSpec((tm, tk), lambda j, i, k: (i, k)),
              "nt": pl.BlockSpec((tm, tk), lambda j, i, k: (i, k)),
              "tn": pl.BlockSpec((tk, tm), lambda j, i, k: (k, i))}[mode]
    b_spec = {"nn": pl.BlockSpec((tk, tn), lambda j, i, k: (k, j)),
              "nt": pl.BlockSpec((tn, tk), lambda j, i, k: (j, k)),
              "tn": pl.BlockSpec((tk, tn), lambda j, i, k: (k, j))}[mode]
    o_spec = pl.BlockSpec((tm, tn), lambda j, i, k: (i, j))
    has_add = add is not None

    def body(*refs):
        if has_add:
            a_ref, b_ref, add_ref, o_ref = refs
        else:
            a_ref, b_ref, o_ref = refs
            add_ref = None
        part = _dot(a_ref[...], b_ref[...], ca, cb)
        if nk == 1:
            if has_add:
                part = part + add_ref[...]
            o_ref[...] = part.astype(o_ref.dtype)
            return
        k = pl.program_id(2)

        @pl.when(k == 0)
        def _():
            o_ref[...] = part + add_ref[...] if has_add else part

        @pl.when(k > 0)
        def _():
            o_ref[...] += part

    in_specs = [a_spec, b_spec] + ([o_spec] if has_add else [])
    args = (a, b) + ((add,) if has_add else ())
    return pl.pallas_call(
        body, name=name, grid=(N // tn, M // tm, nk),
        in_specs=in_specs, out_specs=o_spec,
        out_shape=jax.ShapeDtypeStruct((M, N), out_dtype),
        compiler_params=_cp(("parallel", "parallel", "arbitrary")),
    )(*args)


def rms_fwd(x, g, name):
    R = x.shape[0]

    def body(x_ref, g_ref, y_ref):
        xv = x_ref[...]
        r = lax.rsqrt(jnp.mean(xv * xv, axis=-1, keepdims=True) + EPS)
        y_ref[...] = (xv * r * g_ref[...]).astype(y_ref.dtype)

    return pl.pallas_call(
        body, name=name, grid=(R // TM,),
        in_specs=[_rb(TM, D, 0), _const((1, D))], out_specs=_rb(TM, D, 0),
        out_shape=jax.ShapeDtypeStruct((R, D), MXU), compiler_params=_cp(("parallel",)),
    )(x, g)


def rms_bwd(dy, x, g, dres, name):
    R = x.shape[0]

    def body(dy_ref, x_ref, g_ref, dres_ref, dx_ref, dg_ref):
        i = pl.program_id(0)
        xv, dyv = x_ref[...], dy_ref[...]
        r = lax.rsqrt(jnp.mean(xv * xv, axis=-1, keepdims=True) + EPS)
        dyg = dyv * g_ref[...]
        dx_ref[...] = dres_ref[...] + r * dyg - xv * (r * r * r * jnp.mean(dyg * xv, axis=-1, keepdims=True))
        part = jnp.sum(dyv * xv * r, axis=0, keepdims=True)

        @pl.when(i == 0)
        def _():
            dg_ref[...] = part

        @pl.when(i > 0)
        def _():
            dg_ref[...] += part

    return pl.pallas_call(
        body, name=name, grid=(R // TM,),
        in_specs=[_rb(TM, D, 0), _rb(TM, D, 0), _const((1, D)), _rb(TM, D, 0)],
        out_specs=[_rb(TM, D, 0), _const((1, D))],
        out_shape=[jax.ShapeDtypeStruct((R, D), F32), jax.ShapeDtypeStruct((1, D), F32)],
        compiler_params=_cp(("arbitrary",)),
    )(dy, x, g, dres)


def _shift_down(cur, prev8, s):
    tm = cur.shape[0]
    rolled = pltpu.roll(cur, s, 0)
    rows8 = lax.broadcasted_iota(jnp.int32, (8, cur.shape[1]), 0)
    head = jnp.where(rows8 < s, pltpu.roll(prev8, s, 0), rolled[0:8])
    return jnp.concatenate([head, rolled[8:tm]], axis=0)


def _shift_up(cur, next8, s):
    tm = cur.shape[0]
    rolled = pltpu.roll(cur, tm - s, 0)
    rows8 = lax.broadcasted_iota(jnp.int32, (8, cur.shape[1]), 0)
    tail = jnp.where(rows8 >= 8 - s, pltpu.roll(next8, 8 - s, 0), rolled[tm - 8:tm])
    return jnp.concatenate([rolled[0:tm - 8], tail], axis=0)


def prep_fwd(proj, conv_w, conv_b, gb_row):
    R = proj.shape[0]
    t8 = TM // 8

    def body(x_ref, halo_ref, sm_ref, w_ref, b_ref, gb_ref, c_ref, qk_ref, gl_ref):
        i = pl.program_id(0)
        x = x_ref[...]
        halo = halo_ref[...]
        w = w_ref[...]
        c = x * w[3:4, :] + b_ref[...]
        for s in (1, 2, 3):
            c = c + _shift_down(x, halo, s) * w[3 - s:4 - s, :]
        c_ref[...] = c
        qk_ref[...] = c * _sigmoid(c)
        z = sm_ref[...] + gb_ref[...]
        lane = lax.broadcasted_iota(jnp.int32, z.shape, 1)
        row = lax.broadcasted_iota(jnp.int32, z.shape, 0) + i * TM
        valid = row >= NPADROWS
        logi = jnp.where(valid, z, NEG)
        logf = jnp.where(valid, _log_sigmoid(z), 0.0)
        gl_ref[...] = jnp.where(lane < 4, logi, jnp.where(lane < 8, logf, 0.0))

    return pl.pallas_call(
        body, name="prep_fwd", grid=(R // TM,),
        in_specs=[_rb(TM, 1024, QK0 // 1024),
                  pl.BlockSpec((8, 1024), lambda i: (jnp.maximum(i * t8 - 1, 0), QK0 // 1024)),
                  _rb(TM, 128, SM0 // 128), _const((4, 1024)), _const((1, 1024)), _const((1, 128))],
        out_specs=[_rb(TM, 1024, 0), _rb(TM, 1024, 0), _rb(TM, 128, 0)],
        out_shape=[jax.ShapeDtypeStruct((R, 1024), F32), jax.ShapeDtypeStruct((R, 1024), F32),
                   jax.ShapeDtypeStruct((R, 128), F32)],
        compiler_params=_cp(("parallel",)),
    )(proj, proj, proj, conv_w, conv_b, gb_row)


def merge_fwd(bm, bg, proj):
    R = bm.shape[0]

    def body(bm_ref, bg_ref, gm_ref, gg_ref, o_ref):
        o_ref[...] = (_sigmoid(gm_ref[...]) * bm_ref[...] + _sigmoid(gg_ref[...]) * bg_ref[...]).astype(o_ref.dtype)

    return pl.pallas_call(
        body, name="merge_fwd", grid=(R // TM,),
        in_specs=[_rb(TM, D, 0), _rb(TM, D, 0), _rb(TM, D, GM0 // D), _rb(TM, D, GG0 // D)],
        out_specs=_rb(TM, D, 0), out_shape=jax.ShapeDtypeStruct((R, D), MXU),
        compiler_params=_cp(("parallel",)),
    )(bm, bg, proj, proj)


def merge_bwd(dmerged, bm, bg, proj):
    R = bm.shape[0]

    def body(dm_ref, bm_ref, bg_ref, gm_ref, gg_ref, dbm_ref, dbg_ref, dp_ref):
        dm = dm_ref[...]
        sm, sg = _sigmoid(gm_ref[...]), _sigmoid(gg_ref[...])
        dbm_ref[...] = (dm * sm).astype(dbm_ref.dtype)
        dbg_ref[...] = (dm * sg).astype(dbg_ref.dtype)
        dp_ref[:, 0:D] = (dm * bm_ref[...] * sm * (1.0 - sm)).astype(dp_ref.dtype)
        dp_ref[:, D:2 * D] = (dm * bg_ref[...] * sg * (1.0 - sg)).astype(dp_ref.dtype)

    return pl.pallas_call(
        body, name="merge_bwd", grid=(R // TM,),
        in_specs=[_rb(TM, D, 0), _rb(TM, D, 0), _rb(TM, D, 0), _rb(TM, D, GM0 // D), _rb(TM, D, GG0 // D)],
        out_specs=[_rb(TM, D, 0), _rb(TM, D, 0), _rb(TM, 2 * D, GM0 // (2 * D))],
        out_shape=[jax.ShapeDtypeStruct((R, D), MXU), jax.ShapeDtypeStruct((R, D), MXU),
                   jax.ShapeDtypeStruct((R, NP), MXU)],
        compiler_params=_cp(("parallel",)),
    )(dmerged, bm, bg, proj, proj)


def swiglu_fwd(au):
    R = au.shape[0]
    tw = DFF // 2

    def body(a_ref, u_ref, o_ref):
        a = a_ref[...].astype(F32)
        o_ref[...] = (a * _sigmoid(a) * u_ref[...].astype(F32)).astype(o_ref.dtype)

    return pl.pallas_call(
        body, name="swiglu_fwd", grid=(R // TM, 2),
        in_specs=[pl.BlockSpec((TM, tw), lambda i, j: (i, j)), pl.BlockSpec((TM, tw), lambda i, j: (i, j + 2))],
        out_specs=pl.BlockSpec((TM, tw), lambda i, j: (i, j)),
        out_shape=jax.ShapeDtypeStruct((R, DFF), MXU), compiler_params=_cp(("parallel", "parallel")),
    )(au, au)


def swiglu_bwd(au, dff):
    R = au.shape[0]
    tw = DFF // 2

    def body(a_ref, u_ref, d_ref, dau_ref):
        a, d = a_ref[...].astype(F32), d_ref[...]
        s = _sigmoid(a)
        dau_ref[:, 0:DFF] = (d * u_ref[...].astype(F32) * s * (1.0 + a * (1.0 - s))).astype(dau_ref.dtype)
        dau_ref[:, DFF:2 * DFF] = (d * a * s).astype(dau_ref.dtype)

    tr = 256
    return pl.pallas_call(
        body, name="swiglu_bwd", grid=(R // tr,),
        in_specs=[_rb(tr, DFF, 0), _rb(tr, DFF, 1), _rb(tr, DFF, 0)],
        out_specs=_rb(tr, 2 * DFF, 0),
        out_shape=jax.ShapeDtypeStruct((R, 2 * DFF), MXU),
        compiler_params=_cp(("parallel",)),
    )(au, au, dff)


def final_loss(h2, gf, target):
    R = h2.shape[0]
    assert PADR == TM

    def body(h_ref, g_ref, t_ref, dh_ref, loss_ref, dg_ref):
        i = pl.program_id(0)
        hv = h_ref[...]
        r = lax.rsqrt(jnp.mean(hv * hv, axis=-1, keepdims=True) + EPS)
        g = g_ref[...]
        live = (i >= 1).astype(F32)
        e = (hv * r * g - t_ref[...]) * live
        dy = e * (1.0 / D)
        dyg = dy * g
        dh_ref[...] = r * dyg - hv * (r * r * r * jnp.mean(dyg * hv, axis=-1, keepdims=True))
        lpart = jnp.zeros((1, 128), F32) + 0.5 * jnp.sum(jnp.sum(e * e, axis=1, keepdims=True), axis=0, keepdims=True) * (1.0 / D)
        gpart = jnp.sum(dy * hv * r, axis=0, keepdims=True)

        @pl.when(i == 0)
        def _():
            loss_ref[...] = lpart
            dg_ref[...] = gpart

        @pl.when(i > 0)
        def _():
            loss_ref[...] += lpart
            dg_ref[...] += gpart

    return pl.pallas_call(
        body, name="final_loss", grid=(R // TM,),
        in_specs=[_rb(TM, D, 0), _const((1, D)), pl.BlockSpec((TM, D), lambda i: (jnp.maximum(i - 1, 0), 0))],
        out_specs=[_rb(TM, D, 0), _const((1, 128)), _const((1, D))],
        out_shape=[jax.ShapeDtypeStruct((R, D), F32), jax.ShapeDtypeStruct((1, 128), F32),
                   jax.ShapeDtypeStruct((1, D), F32)],
        compiler_params=_cp(("arbitrary",)),
    )(h2, gf, target)


def conv_bwd(dc, proj, conv_w, dproj):
    R = dc.shape[0]
    t8 = TM // 8
    nt = R // TM

    def body(dc_ref, nxt_ref, x_ref, prv_ref, w_ref, dp_in, dp_ref, dw_ref):
        del dp_in
        i = pl.program_id(0)
        dcv = dc_ref[...]
        nxt = nxt_ref[...] * (i < nt - 1).astype(F32)
        x = x_ref[...]
        prv = prv_ref[...]
        w = w_ref[...]
        dx = dcv * w[3:4, :]
        rows = [None] * 4
        rows[3] = jnp.sum(dcv * x, axis=0, keepdims=True)
        for s in (1, 2, 3):
            dx = dx + _shift_up(dcv, nxt, s) * w[3 - s:4 - s, :]
            rows[3 - s] = jnp.sum(dcv * _shift_down(x, prv, s), axis=0, keepdims=True)
        dp_ref[...] = dx.astype(dp_ref.dtype)
        part = jnp.concatenate(rows + [jnp.sum(dcv, axis=0, keepdims=True), jnp.zeros((3, 1024), F32)], axis=0)

        @pl.when(i == 0)
        def _():
            dw_ref[...] = part

        @pl.when(i > 0)
        def _():
            dw_ref[...] += part

    return pl.pallas_call(
        body, name="conv_bwd", grid=(nt,),
        in_specs=[_rb(TM, 1024, 0),
                  pl.BlockSpec((8, 1024), lambda i: (jnp.minimum((i + 1) * t8, nt * t8 - 1), 0)),
                  _rb(TM, 1024, QK0 // 1024),
                  pl.BlockSpec((8, 1024), lambda i: (jnp.maximum(i * t8 - 1, 0), QK0 // 1024)),
                  _const((4, 1024)), pl.BlockSpec(memory_space=pl.ANY)],
        out_specs=[_rb(TM, 1024, QK0 // 1024), _const((8, 1024))],
        out_shape=[jax.ShapeDtypeStruct((R, NP), MXU), jax.ShapeDtypeStruct((8, 1024), F32)],
        input_output_aliases={5: 0},
        compiler_params=_cp(("arbitrary",)),
    )(dc, dc, proj, proj, conv_w, dproj)


def small_bwd(dgl, dga, proj, gb_row, dproj):
    R = dgl.shape[0]

    def body(dgl_ref, dga_ref, sm_ref, gb_ref, dp_in, dp_ref, dgb_ref):
        del dp_in
        i = pl.program_id(0)
        z = sm_ref[...] + gb_ref[...]
        lane = lax.broadcasted_iota(jnp.int32, z.shape, 1)
        row = lax.broadcasted_iota(jnp.int32, z.shape, 0) + i * TM
        valid = row >= NPADROWS
        dgl_v = dgl_ref[...]
        dgate = jnp.where(valid, jnp.where(lane < 4, dgl_v, dgl_v * _sigmoid(-z)), 0.0)
        ds = jnp.where(lane < 8, dgate, dga_ref[...])
        dp_ref[...] = ds.astype(dp_ref.dtype)
        part = jnp.sum(jnp.where(lane < 8, dgate, 0.0), axis=0, keepdims=True)

        @pl.when(i == 0)
        def _():
            dgb_ref[...] = part

        @pl.when(i > 0)
        def _():
            dgb_ref[...] += part

    return pl.pallas_call(
        body, name="small_bwd", grid=(R // TM,),
        in_specs=[_rb(TM, 128, 0), _rb(TM, 128, 0), _rb(TM, 128, SM0 // 128), _const((1, 128)),
                  pl.BlockSpec(memory_space=pl.ANY)],
        out_specs=[_rb(TM, 128, SM0 // 128), _const((1, 128))],
        out_shape=[jax.ShapeDtypeStruct((R, NP), MXU), jax.ShapeDtypeStruct((1, 128), F32)],
        input_output_aliases={4: 0},
        compiler_params=_cp(("arbitrary",)),
    )(dgl, dga, proj, gb_row, dproj)


def _masks():
    r = lax.broadcasted_iota(jnp.int32, (L, L), 0)
    c = lax.broadcasted_iota(jnp.int32, (L, L), 1)
    return r >= c, r == c, r


def _to_row(col, eye):
    return jnp.sum(jnp.where(eye, col, 0.0), axis=0, keepdims=True)


def _to_col(row, eye):
    return jnp.sum(jnp.where(eye, row, 0.0), axis=1, keepdims=True)


def _mlstm_chunk(q, k, logi_c, logf_c, m, n):
    tril, eye, _ = _masks()
    logi_r, logf_r = _to_row(logi_c, eye), _to_row(logf_c, eye)
    b_c = jnp.sum(jnp.where(tril, logf_r, 0.0), axis=1, keepdims=True)
    b_r = _to_row(b_c, eye)
    g = jnp.sum(logf_c, axis=0, keepdims=True)
    dmat = jnp.where(tril, b_c - b_r + logi_r, NEG)
    mrow = jnp.maximum(b_c + m, jnp.max(dmat, axis=1, keepdims=True))
    dm = jnp.exp(dmat - mrow)
    s = _dot(q, k, 1, 1)
    w = dm * s
    a_in = jnp.exp(b_c + m - mrow)
    qn = jnp.sum(q * n, axis=1, keepdims=True)
    den = a_in * qn + jnp.sum(w, axis=1, keepdims=True)
    floor = jnp.exp(-mrow)
    nrm = jnp.maximum(jnp.abs(den), floor)
    wlog_c = g - b_c + logi_c
    m_new = jnp.maximum(g + m, jnp.max(wlog_c, axis=0, keepdims=True))
    a_st = jnp.exp(g + m - m_new)
    w_c = jnp.exp(wlog_c - m_new)
    return dict(b_c=b_c, g=g, dm=dm, s=s, w=w, a_in=a_in, qn=qn, den=den, floor=floor, nrm=nrm,
                m_new=m_new, a_st=a_st, w_c=w_c, tril=tril, eye=eye)


def _mlstm_head_fwd(h, glv, qk_ref, v_ref, mo_ref, hg_ref, hm_ref, ym_ref, cs_ref, nm_ref, c_s, nm_s):
    q = qk_ref[:, h * DQK:(h + 1) * DQK] * QSCALE
    k = qk_ref[:, 512 + h * DQK:512 + (h + 1) * DQK]
    v = v_ref[:, h * DV:(h + 1) * DV]
    C = c_s[h]
    n = nm_s[h, 0:1, :]
    m = nm_s[h, 1:2, 0:1]
    f = _mlstm_chunk(q, k, glv[:, h:h + 1], glv[:, 4 + h:5 + h], m, n)
    num = f["a_in"] * _dot(q, C, 1, 1) + _dot(f["w"], v, 1, 0)
    hh = num / f["nrm"]
    cs_ref[0, h] = C
    nm_ref[0, h] = nm_s[h]
    c_s[h] = f["a_st"] * C + _dot(f["w_c"] * v, k, 0, 0)
    n_new = f["a_st"] * n + jnp.sum(f["w_c"] * k, axis=0, keepdims=True)
    rowi = lax.broadcasted_iota(jnp.int32, (8, DQK), 0)
    nm_s[h] = jnp.where(rowi == 0, n_new, jnp.where(rowi == 1, f["m_new"], 0.0))
    rm = lax.rsqrt(jnp.mean(hh * hh, axis=-1, keepdims=True) + EPS)
    sl = slice(h * DV, (h + 1) * DV)
    hm_ref[:, sl] = hh
    ym_ref[:, sl] = (hh * rm * hg_ref[:, sl] * _sigmoid(mo_ref[:, sl])).astype(ym_ref.dtype)


def _mlstm_bwd_parts(dym_ref, hm_ref, qk_ref, cp_ref, v_ref, gl_ref, mo_ref, hg_ref, cs_ref, nm_ref,
                     dp_ref, dc_ref, dgl_ref, dhg_ref, dc_s, dn_s):
        def init():
            dc_s[...] = jnp.zeros_like(dc_s)
            dn_s[...] = jnp.zeros_like(dn_s)
            dhg_ref[...] = jnp.zeros_like(dhg_ref)

        def zero():
            dp_ref[...] = jnp.zeros_like(dp_ref)
            dc_ref[...] = jnp.zeros_like(dc_ref)
            dgl_ref[...] = jnp.zeros_like(dgl_ref)

        def compute():
            glv = gl_ref[...]
            lane = lax.broadcasted_iota(jnp.int32, (L, 128), 1)
            dgl = jnp.zeros((L, 128), F32)
            for h in range(NH):
                sl = slice(h * DV, (h + 1) * DV)
                sq = slice(h * DQK, (h + 1) * DQK)
                sk = slice(512 + h * DQK, 512 + (h + 1) * DQK)
                hh = hm_ref[:, sl]
                gain = hg_ref[:, sl]
                rm = lax.rsqrt(jnp.mean(hh * hh, axis=-1, keepdims=True) + EPS)
                sg = _sigmoid(mo_ref[:, sl])
                dyv = dym_ref[:, sl]
                dno = dyv * sg
                dp_ref[:, 1024 + h * DV:1024 + (h + 1) * DV] = (dyv * hh * rm * gain * sg * (1.0 - sg)).astype(dp_ref.dtype)
                dhg_ref[:, sl] += jnp.sum(dno * hh * rm, axis=0, keepdims=True)
                dnog = dno * gain
                dh = rm * dnog - hh * (rm * rm * rm * jnp.mean(dnog * hh, axis=-1, keepdims=True))
                q = qk_ref[:, sq] * QSCALE
                k = qk_ref[:, sk]
                v = v_ref[:, sl]
                C = cs_ref[0, h]
                n = nm_ref[0, h, 0:1, :]
                m = nm_ref[0, h, 1:2, 0:1]
                f = _mlstm_chunk(q, k, glv[:, h:h + 1], glv[:, 4 + h:5 + h], m, n)
                eye = f["eye"]
                a_in, nrm, den, w = f["a_in"], f["nrm"], f["den"], f["w"]
                dnum = dh / nrm
                dnrm = -jnp.sum(dh * hh, axis=1, keepdims=True) / nrm
                dden = jnp.where(jnp.abs(den) >= f["floor"], dnrm * jnp.sign(den), 0.0)
                dw = _dot(dnum, v, 1, 1) + dden
                dv = _dot(w, dnum, 0, 0)
                ds = dw * f["dm"]
                e = dw * w
                qc = _dot(q, C, 1, 1)
                dq = _dot(ds, k, 1, 0) + a_in * _dot(dnum, C, 1, 0) + (a_in * dden) * n
                dk = _dot(ds, q, 0, 0)
                dC_in = _dot(a_in * dnum, q, 0, 0)
                dn_in = jnp.sum((a_in * dden) * q, axis=0, keepdims=True)
                da_in = jnp.sum(dnum * qc, axis=1, keepdims=True) + dden * f["qn"]
                col_e = _to_col(jnp.sum(e, axis=0, keepdims=True), eye)
                db = jnp.sum(e, axis=1, keepdims=True) + da_in * a_in - col_e
                dlogi = col_e
                dCp = dc_s[h]
                dnp = dn_s[h, 0:1, :]
                a_st, w_c = f["a_st"], f["w_c"]
                da_st = (jnp.sum(jnp.sum(dCp * C, axis=1, keepdims=True), axis=0, keepdims=True)
                         + jnp.sum(dnp * n, axis=1, keepdims=True))
                vdc = _dot(v, dCp, 1, 0)
                dw_c = jnp.sum((vdc + dnp) * k, axis=1, keepdims=True)
                dv = dv + w_c * _dot(k, dCp, 1, 1)
                dk = dk + w_c * (vdc + dnp)
                fw = dw_c * w_c
                dg = jnp.sum(fw, axis=0, keepdims=True) + da_st * a_st
                db = db - fw
                dlogi = dlogi + fw
                rowc = lax.broadcasted_iota(jnp.int32, (L, 1), 0)
                db = db + jnp.where(rowc == L - 1, dg, 0.0)
                triu = lax.broadcasted_iota(jnp.int32, (L, L), 1) >= lax.broadcasted_iota(jnp.int32, (L, L), 0)
                dlogf = jnp.sum(jnp.where(triu, _to_row(db, eye), 0.0), axis=1, keepdims=True)
                dc_s[h] = a_st * dCp + dC_in
                dn_new = a_st * dnp + dn_in
                dn_s[h] = jnp.zeros((8, DQK), F32) + dn_new
                cq, ck = cp_ref[:, sq], cp_ref[:, sk]
                s_q, s_k = _sigmoid(cq), _sigmoid(ck)
                dc_ref[:, sq] = dq * QSCALE * s_q * (1.0 + cq * (1.0 - s_q))
                dc_ref[:, sk] = dk * s_k * (1.0 + ck * (1.0 - s_k))
                dp_ref[:, sl] = dv.astype(dp_ref.dtype)
                dgl = jnp.where(lane == h, dlogi, jnp.where(lane == 4 + h, dlogf, dgl))
            dgl_ref[...] = dgl

        return init, zero, compute


def _gla_logs(sm, a2p, b2, valid):
    za = _dot(sm, a2p, 1, 0) + b2
    return za, jnp.where(valid, _log_sigmoid(za) * (1.0 / TAU), 0.0)


def _valid_rows(c, width):
    row = lax.broadcasted_iota(jnp.int32, (L, width), 0) + c * L
    return row >= NPADROWS


def _gla_chunk(q, k, la):
    tril, _, _ = _masks()
    bc = _dot_exact(tril.astype(F32), la)
    btot = jnp.sum(la, axis=0, keepdims=True)
    ebc = jnp.exp(bc)
    qd = q * ebc
    ki = k * jnp.exp(-bc)
    ke = k * jnp.exp(btot - bc)
    att = jnp.where(tril, _dot(qd, ki, 1, 1), 0.0)
    return dict(tril=tril, bc=bc, btot=btot, ebc=ebc, qd=qd, ki=ki, ke=ke, att=att)


def _col128(row):
    r = lax.broadcasted_iota(jnp.int32, (DQK, DQK), 0)
    c = lax.broadcasted_iota(jnp.int32, (DQK, DQK), 1)
    return jnp.sum(jnp.where(r == c, row, 0.0), axis=1, keepdims=True)


def _gla_fwd_parts(c, q_ref, k_ref, v_ref, gr_ref, sm_ref, a2_ref, b2_ref, hg_ref, hgl_ref, yg_ref, ss_ref, s_s):
        def init():
            s_s[...] = jnp.zeros_like(s_s)

        def zero():
            hgl_ref[...] = jnp.zeros_like(hgl_ref)
            yg_ref[...] = jnp.zeros_like(yg_ref)
            ss_ref[...] = jnp.zeros_like(ss_ref)

        def compute():
            _, loga = _gla_logs(sm_ref[...], a2_ref[...], b2_ref[...], _valid_rows(c, 512))
            for h in range(NH):
                sq = slice(h * DQK, (h + 1) * DQK)
                sl = slice(h * DV, (h + 1) * DV)
                q = q_ref[:, sq] * QSCALE
                k = k_ref[:, sq]
                v = v_ref[:, sl]
                S = s_s[h]
                f = _gla_chunk(q, k, loga[:, sq])
                o = _dot(f["att"], v, 1, 0) + _dot(f["qd"], S, 1, 0)
                ss_ref[0, h] = S
                s_s[h] = _col128(jnp.exp(f["btot"])) * S + _dot(f["ke"], v, 0, 0)
                rg = lax.rsqrt(jnp.mean(o * o, axis=-1, keepdims=True) + EPS)
                gr = gr_ref[:, sl]
                hgl_ref[:, sl] = o
                yg_ref[:, sl] = (o * rg * hg_ref[:, sl] * gr * _sigmoid(gr)).astype(yg_ref.dtype)

        return init, zero, compute


def _gla_bwd_parts(c, dy_ref, ho_ref, q_ref, k_ref, v_ref, gr_ref, sm_ref, a2_ref, b2_ref, hg_ref, ss_ref,
                   dp_ref, dga_ref, da2_ref, db2_ref, dhg_ref, ds_s):
        def init():
            ds_s[...] = jnp.zeros_like(ds_s)
            da2_ref[...] = jnp.zeros_like(da2_ref)
            db2_ref[...] = jnp.zeros_like(db2_ref)
            dhg_ref[...] = jnp.zeros_like(dhg_ref)

        def zero():
            dp_ref[...] = jnp.zeros_like(dp_ref)
            dga_ref[...] = jnp.zeros_like(dga_ref)

        def compute():
            valid = _valid_rows(c, 512)
            sm = sm_ref[...]
            za, loga = _gla_logs(sm, a2_ref[...], b2_ref[...], valid)
            dloga = []
            for h in range(NH):
                sq = slice(h * DQK, (h + 1) * DQK)
                sl = slice(h * DV, (h + 1) * DV)
                o = ho_ref[:, sl]
                gain = hg_ref[:, sl]
                rg = lax.rsqrt(jnp.mean(o * o, axis=-1, keepdims=True) + EPS)
                gr = gr_ref[:, sl]
                sg = _sigmoid(gr)
                dyv = dy_ref[:, sl]
                dno = dyv * gr * sg
                dp_ref[:, 2048 + h * DV:2048 + (h + 1) * DV] = (
                    dyv * o * rg * gain * sg * (1.0 + gr * (1.0 - sg))).astype(dp_ref.dtype)
                dhg_ref[:, sl] += jnp.sum(dno * o * rg, axis=0, keepdims=True)
                dnog = dno * gain
                do = rg * dnog - o * (rg * rg * rg * jnp.mean(dnog * o, axis=-1, keepdims=True))
                q = q_ref[:, sq] * QSCALE
                k = k_ref[:, sq]
                v = v_ref[:, sl]
                S = ss_ref[0, h]
                f = _gla_chunk(q, k, loga[:, sq])
                tril, qd, ki, ke = f["tril"], f["qd"], f["ki"], f["ke"]
                dSp = ds_s[h]
                datt = jnp.where(tril, _dot(do, v, 1, 1), 0.0)
                dqd = _dot(do, S, 1, 1) + _dot(datt, ki, 1, 0)
                dki = _dot(datt, qd, 0, 0)
                dv = _dot(f["att"], do, 0, 0) + _dot(ke, dSp, 1, 0)
                dke = _dot(v, dSp, 1, 1)
                ebt = jnp.exp(f["btot"])
                dbtot = jnp.sum(dke * ke, axis=0, keepdims=True) + ebt * _to_row128(jnp.sum(dSp * S, axis=1, keepdims=True))
                ds_s[h] = _dot(qd, do, 0, 0) + _col128(ebt) * dSp
                dq = dqd * f["ebc"]
                dk = dki * jnp.exp(-f["bc"]) + dke * jnp.exp(f["btot"] - f["bc"])
                dbc = dqd * qd - dki * ki - dke * ke
                rowc = lax.broadcasted_iota(jnp.int32, (L, DQK), 0)
                dbc = dbc + jnp.where(rowc == L - 1, dbtot, 0.0)
                triu = lax.broadcasted_iota(jnp.int32, (L, L), 1) >= lax.broadcasted_iota(jnp.int32, (L, L), 0)
                dloga.append(_dot_exact(triu.astype(F32), dbc))
                dp_ref[:, sq] = (dq * QSCALE).astype(dp_ref.dtype)
                dp_ref[:, 512 + h * DQK:512 + (h + 1) * DQK] = dk.astype(dp_ref.dtype)
                dp_ref[:, 1024 + h * DV:1024 + (h + 1) * DV] = dv.astype(dp_ref.dtype)
            dza = jnp.where(valid, jnp.concatenate(dloga, axis=1) * (1.0 / TAU) * _sigmoid(-za), 0.0)
            dga_ref[...] = _dot(dza, a2_ref[...], 1, 1)
            da2_ref[...] += _dot(sm, dza, 0, 0)
            db2_ref[...] += jnp.sum(dza, axis=0, keepdims=True)

        return init, zero, compute


def mix_fwd(qk, proj, gl, m_head_g, a2p, b2, g_head_g):
    R = qk.shape[0]
    NC = R // L

    def body(qk_ref, mv_ref, gl_ref, mo_ref, mhg_ref, gq_ref, gk_ref, gv_ref, gr_ref, sm_ref, a2_ref, b2_ref, ghg_ref,
             hm_ref, ym_ref, cs_ref, nm_ref, hgl_ref, yg_ref, ss_ref, c_s, nm_s, s_s):
        c = pl.program_id(0)
        g_init, g_zero, g_compute = _gla_fwd_parts(c, gq_ref, gk_ref, gv_ref, gr_ref, sm_ref, a2_ref, b2_ref, ghg_ref,
                                                   hgl_ref, yg_ref, ss_ref, s_s)

        @pl.when(c <= CH0)
        def _():
            c_s[...] = jnp.zeros_like(c_s)
            nm_s[...] = jnp.zeros_like(nm_s)
            g_init()

        @pl.when(c < CH0)
        def _():
            hm_ref[...] = jnp.zeros_like(hm_ref)
            ym_ref[...] = jnp.zeros_like(ym_ref)
            cs_ref[...] = jnp.zeros_like(cs_ref)
            nm_ref[...] = jnp.zeros_like(nm_ref)
            g_zero()

        @pl.when(c >= CH0)
        def _():
            glv = gl_ref[...]
            for h in range(NH):
                _mlstm_head_fwd(h, glv, qk_ref, mv_ref, mo_ref, mhg_ref, hm_ref, ym_ref, cs_ref, nm_ref, c_s, nm_s)
            g_compute()

    st_m = pl.BlockSpec((1, NH, DV, DQK), lambda c: (c, 0, 0, 0))
    st_n = pl.BlockSpec((1, NH, 8, DQK), lambda c: (c, 0, 0, 0))
    st_g = pl.BlockSpec((1, NH, DQK, DV), lambda c: (c, 0, 0, 0))
    return pl.pallas_call(
        body, name="mix_fwd", grid=(NC,),
        in_specs=[_rb(L, 1024, 0), _rb(L, 1024, MV0 // 1024), _rb(L, 128, 0), _rb(L, 1024, MO0 // 1024),
                  _const((1, 1024)),
                  _rb(L, 512, GQ0 // 512), _rb(L, 512, GK0 // 512), _rb(L, 1024, GV0 // 1024),
                  _rb(L, 1024, GR0 // 1024), _rb(L, 128, SM0 // 128), _const((128, 512)), _const((1, 512)),
                  _const((1, 1024))],
        out_specs=[_rb(L, 1024, 0), _rb(L, 1024, 0), st_m, st_n, _rb(L, 1024, 0), _rb(L, 1024, 0), st_g],
        out_shape=[jax.ShapeDtypeStruct((R, 1024), F32), jax.ShapeDtypeStruct((R, 1024), MXU),
                   jax.ShapeDtypeStruct((NC, NH, DV, DQK), F32), jax.ShapeDtypeStruct((NC, NH, 8, DQK), F32),
                   jax.ShapeDtypeStruct((R, 1024), F32), jax.ShapeDtypeStruct((R, 1024), MXU),
                   jax.ShapeDtypeStruct((NC, NH, DQK, DV), F32)],
        scratch_shapes=[pltpu.VMEM((NH, DV, DQK), F32), pltpu.VMEM((NH, 8, DQK), F32),
                        pltpu.VMEM((NH, DQK, DV), F32)],
        compiler_params=_cp(("arbitrary",)),
    )(qk, proj, gl, proj, m_head_g, proj, proj, proj, proj, proj, a2p, b2, g_head_g)


def mix_bwd(dym, hm, qk, cpre, proj, gl, m_head_g, cs, nm, dyg, hgl, a2p, b2, g_head_g, ss, dproj):
    R = qk.shape[0]
    NC = R // L
    rev = lambda c: NC - 1 - c
    GW = GR0 + 1024 - GQ0

    def body(dym_ref, hm_ref, qk_ref, cp_ref, mv_ref, gl_ref, mo_ref, mhg_ref, cs_ref, nm_ref,
             dyg_ref, ho_ref, gq_ref, gk_ref, gv_ref, gr_ref, sm_ref, a2_ref, b2_ref, ghg_ref, ss_ref, dp_in,
             dp_ref, dc_ref, dgl_ref, dmhg_ref, dga_ref, da2_ref, db2_ref, dghg_ref, dc_s, dn_s, ds_s):
        del dp_in
        step = pl.program_id(0)
        c = NC - 1 - step
        m_init, m_zero, m_compute = _mlstm_bwd_parts(
            dym_ref, hm_ref, qk_ref, cp_ref, mv_ref, gl_ref, mo_ref, mhg_ref, cs_ref, nm_ref,
            dp_ref.at[:, 0:GQ0], dc_ref, dgl_ref, dmhg_ref, dc_s, dn_s)
        g_init, g_zero, g_compute = _gla_bwd_parts(
            c, dyg_ref, ho_ref, gq_ref, gk_ref, gv_ref, gr_ref, sm_ref, a2_ref, b2_ref, ghg_ref, ss_ref,
            dp_ref.at[:, GQ0:GQ0 + GW], dga_ref, da2_ref, db2_ref, dghg_ref, ds_s)

        @pl.when(step == 0)
        def _():
            m_init()
            g_init()

        @pl.when(c < CH0)
        def _():
            m_zero()
            g_zero()

        @pl.when(c >= CH0)
        def _():
            m_compute()
            g_compute()

    def rows(w, cb):
        return pl.BlockSpec((L, w), lambda c: (rev(c), cb))

    return pl.pallas_call(
        body, name="mix_bwd", grid=(NC,),
        in_specs=[rows(1024, 0), rows(1024, 0), rows(1024, 0), rows(1024, 0), rows(1024, MV0 // 1024), rows(128, 0),
                  rows(1024, MO0 // 1024), _const((1, 1024)),
                  pl.BlockSpec((1, NH, DV, DQK), lambda c: (rev(c), 0, 0, 0)),
                  pl.BlockSpec((1, NH, 8, DQK), lambda c: (rev(c), 0, 0, 0)),
                  rows(1024, 0), rows(1024, 0), rows(512, GQ0 // 512), rows(512, GK0 // 512),
                  rows(1024, GV0 // 1024), rows(1024, GR0 // 1024), rows(128, SM0 // 128),
                  _const((128, 512)), _const((1, 512)), _const((1, 1024)),
                  pl.BlockSpec((1, NH, DQK, DV), lambda c: (rev(c), 0, 0, 0)),
                  pl.BlockSpec(memory_space=pl.ANY)],
        out_specs=[rows(GQ0 + GW, 0), rows(1024, 0), rows(128, 0), _const((1, 1024)),
                   rows(128, 0), _const((128, 512)), _const((1, 512)), _const((1, 1024))],
        out_shape=[jax.ShapeDtypeStruct((R, NP), MXU), jax.ShapeDtypeStruct((R, 1024), F32),
                   jax.ShapeDtypeStruct((R, 128), F32), jax.ShapeDtypeStruct((1, 1024), F32),
                   jax.ShapeDtypeStruct((R, 128), F32), jax.ShapeDtypeStruct((128, 512), F32),
                   jax.ShapeDtypeStruct((1, 512), F32), jax.ShapeDtypeStruct((1, 1024), F32)],
        scratch_shapes=[pltpu.VMEM((NH, DV, DQK), F32), pltpu.VMEM((NH, 8, DQK), F32),
                        pltpu.VMEM((NH, DQK, DV), F32)],
        input_output_aliases={21: 0},
        compiler_params=_cp(("arbitrary",)),
    )(dym, hm, qk, cpre, proj, gl, proj, m_head_g, cs, nm, dyg, hgl, proj, proj, proj, proj, proj, a2p, b2,
      g_head_g, ss, dproj)


def _to_row128(col):
    r = lax.broadcasted_iota(jnp.int32, (DQK, DQK), 0)
    c = lax.broadcasted_iota(jnp.int32, (DQK, DQK), 1)
    return jnp.sum(jnp.where(r == c, col, 0.0), axis=0, keepdims=True)


def local_step(x, target, meta, norm1_g, wp, conv_w, conv_b, m_gate_b, g_a2, g_a2_b, m_head_g, g_head_g,
               w_bm, w_bg, w_out, norm2_g, w_gu, w_down, final_g):
    seq = x.shape[0]
    assert seq % TM == 0
    h0 = jnp.concatenate([jnp.zeros((NPADROWS, D), F32), meta, x], axis=0)
    gb_row = jnp.zeros((1, 128), F32).at[0, 0:8].set(m_gate_b.reshape(8))
    a2p = jnp.zeros((128, 512), F32).at[8:8 + RANK].set(g_a2)
    mhg = m_head_g.reshape(1, 1024)
    ghg = g_head_g.reshape(1, 1024)

    xn = rms_fwd(h0, norm1_g, "rms1_fwd")
    proj = matmul(xn, wp, "nn", "proj_fwd", tm=1536)
    cpre, qk, gl = prep_fwd(proj, conv_w, conv_b, gb_row)
    hm, ym, cs, nm, hgl, yg, ss = mix_fwd(qk, proj, gl, mhg, a2p, g_a2_b, ghg)
    bm = matmul(ym, w_bm, "nn", "branch_m_fwd")
    bg = matmul(yg, w_bg, "nn", "branch_g_fwd")
    merged = merge_fwd(bm, bg, proj)
    h1 = matmul(merged, w_out, "nn", "out_fwd", add=h0)
    hn = rms_fwd(h1, norm2_g, "rms2_fwd")
    au = matmul(hn, w_gu, "nn", "ff_in_fwd", out_dtype=MXU, tm=1536, tn=1408)
    ff = swiglu_fwd(au)
    h2 = matmul(ff, w_down, "nn", "ff_down_fwd", add=h1, tm=768, tk=2816)
    dh2, loss, d_final_g = final_loss(h2, final_g.reshape(1, D), target)

    d_w_down = matmul(ff, dh2, "tn", "ff_down_wgrad", tm=1408, tk=1536)
    dff = matmul(dh2, w_down, "nt", "ff_down_dgrad", tm=1536, tn=1408)
    dau = swiglu_bwd(au, dff)
    d_w_gu = matmul(hn, dau, "tn", "ff_in_wgrad", tm=1024, tn=1408, tk=1536)
    dhn = matmul(dau, w_gu, "nt", "ff_in_dgrad", tm=1536, tk=1408)
    dh1, d_norm2_g = rms_bwd(dhn, h1, norm2_g, dh2, "rms2_bwd")

    d_w_out = matmul(merged, dh1, "tn", "out_wgrad", tm=1024, tk=1536)
    dmerged = matmul(dh1, w_out, "nt", "out_dgrad")
    dbm, dbg, dproj = merge_bwd(dmerged, bm, bg, proj)
    d_w_bm = matmul(ym, dbm, "tn", "branch_m_wgrad", tm=1024, tk=1536)
    d_w_bg = matmul(yg, dbg, "tn", "branch_g_wgrad", tm=1024, tk=1536)
    dym = matmul(dbm, w_bm, "nt", "branch_m_dgrad")
    dyg = matmul(dbg, w_bg, "nt", "branch_g_dgrad")
    dproj, dc, dgl, d_mhg, dga, d_a2p, d_a2b, d_ghg = mix_bwd(
        dym, hm, qk, cpre, proj, gl, mhg, cs, nm, dyg, hgl, a2p, g_a2_b, ghg, ss, dproj)
    dproj, d_conv = conv_bwd(dc, proj, conv_w, dproj)
    dproj, d_gb = small_bwd(dgl, dga, proj, gb_row, dproj)
    d_wp = matmul(xn, dproj, "tn", "proj_wgrad", tm=1024, tn=1664, tk=1536)
    dxn = matmul(dproj, wp, "nt", "proj_dgrad", tm=1536, tk=1664)
    dh0, d_norm1_g = rms_bwd(dxn, h0, norm1_g, dh1, "rms1_bwd")

    grads = dict(
        norm1_g=d_norm1_g, wp=d_wp, conv_w=d_conv[0:4], conv_b=d_conv[4:5], m_gate_b=d_gb[0, 0:8].reshape(1, 2, 4),
        g_a2=d_a2p[8:8 + RANK], g_a2_b=d_a2b, m_head_g=d_mhg.reshape(NH, DV), g_head_g=d_ghg.reshape(NH, DV),
        w_branch_m=d_w_bm, w_branch_g=d_w_bg, w_out=d_w_out, norm2_g=d_norm2_g, w_gu=d_w_gu, w_ff_down=d_w_down,
        final_g=d_final_g)
    return loss, dh0, grads


_SEGS = [(0, 1024, QK0), (1024, 2048, MV0), (2048, 2056, SM0), (2056, 3080, MO0), (3080, 5128, GQ0),
         (5128, 5144, SM0 + 8), (5144, 6168, GR0), (6168, 8216, GM0)]
SHARD_W = NPROJ // NDEV


def regroup_cols(w8):
    parts = []
    for lo, hi, _ in sorted(_SEGS, key=lambda s: s[2]):
        while lo < hi:
            j = lo // SHARD_W
            end = min(hi, (j + 1) * SHARD_W)
            parts.append(w8[j, :, lo - j * SHARD_W:end - j * SHARD_W])
            lo = end
    parts.append(jnp.zeros((w8.shape[1], NP - NPROJ), w8.dtype))
    return jnp.concatenate(parts, axis=1)


def ungroup_cols(g):
    blocks = []
    for j in range(NDEV):
        lo, hi = j * SHARD_W, (j + 1) * SHARD_W
        parts = []
        for s_lo, s_hi, s_at in _SEGS:
            a, b = max(lo, s_lo), min(hi, s_hi)
            if a < b:
                parts.append(g[:, s_at + a - s_lo:s_at + b - s_lo])
        blocks.append(jnp.concatenate(parts, axis=1))
    return jnp.stack(blocks)


def col_blocks(g):
    r, c8 = g.shape
    return jnp.transpose(g.reshape(r, NDEV, c8 // NDEV), (1, 0, 2))


def from_col_blocks(g8):
    n, r, c = g8.shape
    return jnp.transpose(g8, (1, 0, 2)).reshape(r, n * c)


_MESHID = pl.DeviceIdType.MESH
_RELS = [(0, 0, 1), (1, 0, 0), (0, 1, 0), (1, 1, 0), (1, 0, 1), (0, 1, 1), (1, 1, 1)]


def _flip(v, bit):
    return 1 - v if bit else v


def all_gather(arrs, name):
    n = len(arrs)

    def body(*refs):
        ins, outs = refs[:n], refs[n:2 * n]
        send_sems, recv_sems, local_sems = refs[2 * n:]
        x, y, c = lax.axis_index("x"), lax.axis_index("y"), lax.axis_index("c")
        me, sibling = (x, y, c), (x, y, 1 - c)
        chips = [(1 - x, y), (x, 1 - y), (1 - x, 1 - y)]

        def slot(p):
            return 4 * p[0] + 2 * p[1] + p[2]

        def copy(a, k, block, to, src=None):
            dst = outs[a].at[slot(block)]
            return pltpu.make_async_remote_copy(
                src_ref=dst if src is None else src, dst_ref=dst,
                send_sem=send_sems.at[a, k], recv_sem=recv_sems.at[a, k],
                device_id=to, device_id_type=_MESHID)

        mine = [pltpu.make_async_copy(ins[a], outs[a].at[slot(me)], local_sems.at[a]) for a in range(n)]
        for cp in mine:
            cp.start()
        first = []
        for a in range(n):
            first.append(copy(a, 0, me, sibling, src=ins[a]))
            first += [copy(a, 1 + j, me, (*chip, c), src=ins[a]) for j, chip in enumerate(chips)]
        for cp in first:
            cp.start()
        passed = []
        for j, chip in enumerate(chips):
            for a in range(n):
                copy(a, 1 + j, (*chip, c), me).wait_recv()
                fwd = copy(a, 4 + j, (*chip, c), sibling)
                fwd.start()
                passed.append(fwd)
        for a in range(n):
            copy(a, 0, sibling, me).wait_recv()
            for j, chip in enumerate(chips):
                copy(a, 4 + j, (*chip, 1 - c), me).wait_recv()
        for cp in first + passed:
            cp.wait_send()
        for cp in mine:
            cp.wait()

    anyspec = pl.BlockSpec(memory_space=pl.ANY)
    return pl.pallas_call(
        body, name=name,
        in_specs=[anyspec] * n, out_specs=[anyspec] * n,
        out_shape=[jax.ShapeDtypeStruct((NDEV,) + a.shape, a.dtype) for a in arrs],
        scratch_shapes=[pltpu.SemaphoreType.DMA((n, 7)), pltpu.SemaphoreType.DMA((n, 7)),
                        pltpu.SemaphoreType.DMA((n,))],
    )(*arrs)


def exchange(blocks, rep, name):
    n = len(blocks)

    def body(*refs):
        b_refs, r_ref = refs[:n], refs[n]
        ob_refs, or_ref = refs[n + 1:2 * n + 1], refs[2 * n + 1]
        send_sems, recv_sems, local_sems = refs[2 * n + 2:]
        x, y, c = lax.axis_index("x"), lax.axis_index("y"), lax.axis_index("c")
        me = 4 * x + 2 * y + c

        def pairs(src_slot, dst_slot):
            return [(b_refs[a].at[src_slot], ob_refs[a].at[dst_slot]) for a in range(n)] + [(r_ref, or_ref.at[dst_slot])]

        loc = [pltpu.make_async_copy(s, d, local_sems.at[a]) for a, (s, d) in enumerate(pairs(me, me))]
        for cp in loc:
            cp.start()
        sends = []
        for k, (fx, fy, fc) in enumerate(_RELS):
            peer = (_flip(x, fx), _flip(y, fy), _flip(c, fc))
            pid = 4 * peer[0] + 2 * peer[1] + peer[2]
            for a, (s, d) in enumerate(pairs(pid, me)):
                sends.append(pltpu.make_async_remote_copy(
                    src_ref=s, dst_ref=d, send_sem=send_sems.at[a, k], recv_sem=recv_sems.at[a, k],
                    device_id=peer, device_id_type=_MESHID))
        for cp in sends:
            cp.start()
        for k, (fx, fy, fc) in enumerate(_RELS):
            peer = (_flip(x, fx), _flip(y, fy), _flip(c, fc))
            pid = 4 * peer[0] + 2 * peer[1] + peer[2]
            for a, (s, d) in enumerate(pairs(pid, pid)):
                pltpu.make_async_remote_copy(
                    src_ref=s, dst_ref=d, send_sem=send_sems.at[a, k], recv_sem=recv_sems.at[a, k],
                    device_id=peer, device_id_type=_MESHID).wait_recv()
        for cp in sends:
            cp.wait_send()
        for cp in loc:
            cp.wait()

    anyspec = pl.BlockSpec(memory_space=pl.ANY)
    return pl.pallas_call(
        body, name=name,
        in_specs=[anyspec] * (n + 1), out_specs=[anyspec] * (n + 1),
        out_shape=[jax.ShapeDtypeStruct(b.shape, b.dtype) for b in blocks]
        + [jax.ShapeDtypeStruct((NDEV,) + rep.shape, rep.dtype)],
        scratch_shapes=[pltpu.SemaphoreType.DMA((n + 1, 7)), pltpu.SemaphoreType.DMA((n + 1, 7)),
                        pltpu.SemaphoreType.DMA((n + 1,))],
    )(*blocks, rep)


def adamw(parts, w, m, v, name, tr):
    npart, r, c = parts.shape
    c1 = 1.0 - ADAM_B1 ** ADAM_STEP
    c2 = 1.0 - ADAM_B2 ** ADAM_STEP

    def body(p_ref, w_ref, m_ref, v_ref, g_ref, d_ref, nm_ref, nv_ref):
        g = p_ref[0].astype(F32)
        for j in range(1, npart):
            g = g + p_ref[j].astype(F32)
        mn = ADAM_B1 * m_ref[...] + (1.0 - ADAM_B1) * g
        vn = ADAM_B2 * v_ref[...] + (1.0 - ADAM_B2) * (g * g)
        g_ref[...] = g
        nm_ref[...] = mn
        nv_ref[...] = vn
        d_ref[...] = -ADAM_LR * ((mn / c1) / (jnp.sqrt(vn / c2) + ADAM_EPS) + ADAM_WD * w_ref[...])

    spec = _rb(tr, c, 0)
    return pl.pallas_call(
        body, name=name, grid=(r // tr,),
        in_specs=[pl.BlockSpec((npart, tr, c), lambda i: (0, i, 0)), spec, spec, spec],
        out_specs=[spec] * 4, out_shape=[jax.ShapeDtypeStruct((r, c), F32)] * 4,
        compiler_params=_cp(("parallel",)),
    )(parts, w, m, v)


def sum_parts(parts, name, tc):
    npart, r, c = parts.shape

    def body(p_ref, o_ref):
        g = p_ref[0].astype(F32)
        for j in range(1, npart):
            g = g + p_ref[j].astype(F32)
        o_ref[...] = g

    return pl.pallas_call(
        body, name=name, grid=(c // tc,),
        in_specs=[pl.BlockSpec((npart, r, tc), lambda i: (0, 0, i))],
        out_specs=pl.BlockSpec((r, tc), lambda i: (0, i)),
        out_shape=jax.ShapeDtypeStruct((r, c), F32),
        compiler_params=_cp(("parallel",)),
    )(parts)


TINY = [("meta_tokens", (16, 1024)), ("conv_w", (4, 1024)), ("g_a2", (16, 512)), ("m_head_g", (4, 256)),
        ("g_head_g", (4, 256))]
REPL = [("norm1_g", (1, 1024)), ("conv_b", (1, 1024)), ("m_gate_b", (1, 2, 4)), ("g_a2_b", (1, 512)),
        ("norm2_g", (1, 1024)), ("final_g", (1024,))]
TINY_SIZE = 16 * 1024 + 4 * 1024 + 16 * 512 + 2 * 4 * 256
REPL_SIZE = 1024 + 1024 + 8 + 512 + 1024 + 1024
ROWS_GATHER = 8
ROWS_REP = 40
ROWS_OWN = 16


def pack_rows(vecs, rows):
    flat = jnp.concatenate([v.reshape(-1) for v in vecs])
    return jnp.pad(flat, (0, rows * 1024 - flat.shape[0])).reshape(rows, 1024)


def unpack_rows(packed, shapes):
    flat = packed.reshape(-1)
    out, off = [], 0
    for s in shapes:
        n = 1
        for d in s:
            n *= d
        out.append(flat[off:off + n].reshape(s))
        off += n
    return out


def kernel(x, meta_tokens, norm1_g, w_in, conv_w, conv_b, m_gate_b, g_a2, g_a2_b, m_head_g, g_head_g, w_branch_m, w_branch_g, w_out, norm2_g, w_ff_gate, w_ff_up, w_ff_down, final_g, loss_target, m_meta_tokens, m_norm1_g, m_w_in, m_conv_w, m_conv_b, m_m_gate_b, m_g_a2, m_g_a2_b, m_m_head_g, m_g_head_g, m_w_branch_m, m_w_branch_g, m_w_out, m_norm2_g, m_w_ff_gate, m_w_ff_up, m_w_ff_down, m_final_g, v_meta_tokens, v_norm1_g, v_w_in, v_conv_w, v_conv_b, v_m_gate_b, v_g_a2, v_g_a2_b, v_m_head_g, v_g_head_g, v_w_branch_m, v_w_branch_g, v_w_out, v_norm2_g, v_w_ff_gate, v_w_ff_up, v_w_ff_down, v_final_g):
    w_sh = dict(meta_tokens=meta_tokens, w_in=w_in[0], conv_w=conv_w[0], g_a2=g_a2[0], m_head_g=m_head_g[0],
                g_head_g=g_head_g[0], w_branch_m=w_branch_m[0], w_branch_g=w_branch_g[0], w_out=w_out[0],
                w_ff_gate=w_ff_gate[0], w_ff_up=w_ff_up[0], w_ff_down=w_ff_down[0])
    m_sh = dict(meta_tokens=m_meta_tokens, w_in=m_w_in[0], conv_w=m_conv_w[0], g_a2=m_g_a2[0],
                m_head_g=m_m_head_g[0], g_head_g=m_g_head_g[0], w_branch_m=m_w_branch_m[0],
                w_branch_g=m_w_branch_g[0], w_out=m_w_out[0], w_ff_gate=m_w_ff_gate[0], w_ff_up=m_w_ff_up[0],
                w_ff_down=m_w_ff_down[0])
    v_sh = dict(meta_tokens=v_meta_tokens, w_in=v_w_in[0], conv_w=v_conv_w[0], g_a2=v_g_a2[0],
                m_head_g=v_m_head_g[0], g_head_g=v_g_head_g[0], w_branch_m=v_w_branch_m[0],
                w_branch_g=v_w_branch_g[0], w_out=v_w_out[0], w_ff_gate=v_w_ff_gate[0], w_ff_up=v_w_ff_up[0],
                w_ff_down=v_w_ff_down[0])
    w_rep = dict(norm1_g=norm1_g, conv_b=conv_b, m_gate_b=m_gate_b, g_a2_b=g_a2_b, norm2_g=norm2_g, final_g=final_g)
    m_rep = dict(norm1_g=m_norm1_g, conv_b=m_conv_b, m_gate_b=m_m_gate_b, g_a2_b=m_g_a2_b, norm2_g=m_norm2_g,
                 final_g=m_final_g)
    v_rep = dict(norm1_g=v_norm1_g, conv_b=v_conv_b, m_gate_b=v_m_gate_b, g_a2_b=v_g_a2_b, norm2_g=v_norm2_g,
                 final_g=v_final_g)
    dev = 4 * lax.axis_index("x") + 2 * lax.axis_index("y") + lax.axis_index("c")
    tiny_names = [n for n, _ in TINY]
    repl_names = [n for n, _ in REPL]
    tiny_shard_shapes = [(s[0], s[1] // NDEV) for _, s in TINY]

    send = [w_sh["w_in"].astype(MXU), w_sh["w_branch_m"].astype(MXU), w_sh["w_branch_g"].astype(MXU),
            w_sh["w_out"].astype(MXU), w_sh["w_ff_gate"].astype(MXU), w_sh["w_ff_up"].astype(MXU),
            w_sh["w_ff_down"].astype(MXU), pack_rows([w_sh[n] for n in tiny_names], ROWS_GATHER)]
    in8, bm8, bg8, out8, ffg8, ffu8, ffd8, tiny8 = all_gather(send, "param_all_gather")
    wp = regroup_cols(in8)
    w_gu = jnp.concatenate([from_col_blocks(ffg8), from_col_blocks(ffu8)], axis=1)
    tiny_full = {}
    for j in range(NDEV):
        for name, blk in zip(tiny_names, unpack_rows(tiny8[j], tiny_shard_shapes)):
            tiny_full.setdefault(name, []).append(blk)
    tiny_full = {n: jnp.concatenate(v, axis=1) for n, v in tiny_full.items()}

    loss, dh0, g = local_step(
        x[0], loss_target[0], tiny_full["meta_tokens"], norm1_g, wp, tiny_full["conv_w"], conv_b, m_gate_b[0],
        tiny_full["g_a2"], g_a2_b, tiny_full["m_head_g"], tiny_full["g_head_g"], bm8.reshape(D, D),
        bg8.reshape(D, D), out8.reshape(D, D), norm2_g, w_gu, ffd8.reshape(DFF, D), final_g)

    g["meta_tokens"] = dh0[NPADROWS:PADR]
    blocks = [ungroup_cols(g["wp"]).astype(WIRE), g["w_branch_m"].reshape(NDEV, D // NDEV, D).astype(WIRE),
              g["w_branch_g"].reshape(NDEV, D // NDEV, D).astype(WIRE), g["w_out"].reshape(NDEV, D // NDEV, D).astype(WIRE),
              col_blocks(g["w_gu"][:, :DFF]).astype(WIRE), col_blocks(g["w_gu"][:, DFF:]).astype(WIRE),
              g["w_ff_down"].reshape(NDEV, DFF // NDEV, D).astype(WIRE)]
    rep = pack_rows([g[n] for n in tiny_names + repl_names] + [loss[0, 0:1]], ROWS_REP)
    *got, got_rep = exchange(blocks, rep, "grad_exchange")

    result = {}

    def update(name, parts, tr):
        outs = adamw(parts, w_sh[name], m_sh[name], v_sh[name], "adamw_" + name, tr)
        for kind, arr in zip(("grad", "delta", "new_m", "new_v"), outs):
            result[kind, name] = arr[None]

    update("w_in", got[0], 128)
    update("w_branch_m", got[1], 128)
    update("w_branch_g", got[2], 128)
    update("w_out", got[3], 128)
    update("w_ff_gate", got[4], 256)
    update("w_ff_up", got[5], 256)
    update("w_ff_down", got[6], DFF // NDEV)

    rep_sum = sum_parts(got_rep, "sum_small", 1024)
    rep_g = unpack_rows(rep_sum, [s for _, s in TINY] + [s for _, s in REPL] + [(1,)])
    own_g = [lax.dynamic_slice_in_dim(gf, dev * ss[1], ss[1], axis=1) for gf, ss in zip(rep_g, tiny_shard_shapes)]
    own_g += rep_g[len(TINY):len(TINY) + len(REPL)]
    w_all = {**w_sh, **w_rep}
    m_all = {**m_sh, **m_rep}
    v_all = {**v_sh, **v_rep}
    names = tiny_names + repl_names
    outs = adamw(pack_rows(own_g, ROWS_OWN)[None], pack_rows([w_all[n] for n in names], ROWS_OWN),
                 pack_rows([m_all[n] for n in names], ROWS_OWN), pack_rows([v_all[n] for n in names], ROWS_OWN),
                 "adamw_small", ROWS_OWN)
    shapes = tiny_shard_shapes + [s for _, s in REPL]
    for kind, packed in zip(("grad", "delta", "new_m", "new_v"), outs):
        for name, arr in zip(names, unpack_rows(packed, shapes)):
            result[kind, name] = arr[None] if name in tiny_names and name != "meta_tokens" else arr
    loss_total = rep_g[-1][0]
    order = ["meta_tokens", "norm1_g", "w_in", "conv_w", "conv_b", "m_gate_b", "g_a2", "g_a2_b", "m_head_g", "g_head_g",
             "w_branch_m", "w_branch_g", "w_out", "norm2_g", "w_ff_gate", "w_ff_up", "w_ff_down", "final_g"]
    grad_x = dh0[PADR:][None]
    return (loss_total, grad_x, *[result[kind, n] for kind in ("grad", "delta", "new_m", "new_v") for n in order])
```

```python
import functools

import jax
import jax.numpy as jnp
from jax import lax
from jax.experimental import pallas as pl
from jax.experimental.pallas import tpu as pltpu

F32 = jnp.float32
MXU = jnp.bfloat16
WIRE = jnp.bfloat16

D = 1024
NH = 4
DV = 256
DQK = 128
L = 64
NMETA = 16
PADR = 512
CH0 = PADR // L - 1
NPADROWS = PADR - NMETA
RANK = 16
DFF = 2816
EPS = 1e-6
TAU = 16.0
QSCALE = DQK ** -0.5
NEG = -1e30
NDEV = 8

MV0, MO0, GQ0, GK0, GV0, GR0, QK0, GM0, GG0, SM0 = 0, 1024, 2048, 2560, 3072, 4096, 5120, 6144, 7168, 8192
NP = 8320
NPROJ = 8216

ADAM_LR, ADAM_B1, ADAM_B2, ADAM_EPS, ADAM_WD, ADAM_STEP = 0.001, 0.9, 0.999, 1e-08, 0.01, 10

VMEM_LIMIT = 56 * 1024 * 1024
TM = 512


def _cp(sem):
    return pltpu.CompilerParams(dimension_semantics=sem, vmem_limit_bytes=VMEM_LIMIT)


def _sigmoid(x):
    return 1.0 / (1.0 + jnp.exp(-x))


def _log_sigmoid(x):
    return jnp.minimum(x, 0.0) - jnp.log1p(jnp.exp(-jnp.abs(x)))


def _dot(a, b, ca, cb):
    return lax.dot_general(a.astype(MXU), b.astype(MXU), (((ca,), (cb,)), ((), ())), preferred_element_type=F32)


def _dot_exact(a, b):
    return lax.dot_general(a, b, (((1,), (0,)), ((), ())), precision=lax.Precision.HIGHEST,
                           preferred_element_type=F32)


def _rb(tm, w, cb):
    return pl.BlockSpec((tm, w), lambda i: (i, cb))


def _const(shape):
    nd = len(shape)
    return pl.BlockSpec(shape, lambda i: (0,) * nd)


def _pick(n, target):
    if n <= target:
        return n
    best = None
    for t in range(128, target + 1, 128):
        if n % t == 0:
            best = t
    assert best is not None, (n, target)
    return best


def matmul(a, b, mode, name, add=None, out_dtype=F32, tm=512, tn=1664, tk=1024, order=None):
    if mode == "nn":
        (M, K), (K2, N) = a.shape, b.shape
    elif mode == "nt":
        (M, K), (N, K2) = a.shape, b.shape
    else:
        (K, M), (K2, N) = a.shape, b.shape
    assert K == K2, (a.shape, b.shape, mode)
    tm, tn, tk = _pick(M, tm), _pick(N, tn), _pick(K, tk)
    nk = K // tk
    assert nk == 1 or out_dtype == F32
    ca, cb = {"nn": (1, 0), "nt": (1, 1), "tn": (0, 0)}[mode]
    a_spec = {"nn": pl.BlockSpec((tm, tk), lambda j, i, k: (i, k)),
              "nt": pl.BlockSpec((tm, tk), lambda j, i, k: (i, k)),
              "tn": pl.BlockSpec((tk, tm), lambda j, i, k: (k, i))}[mode]
    b_spec = {"nn": pl.BlockSpec((tk, tn), lambda j, i, k: (k, j)),
              "nt": pl.BlockSpec((tn, tk), lambda j, i, k: (j, k)),
              "tn": pl.BlockSpec((tk, tn), lambda j, i, k: (k, j))}[mode]
    o_spec = pl.BlockSpec((tm, tn), lambda j, i, k: (i, j))
    has_add = add is not None

    def body(*refs):
        if order is not None:
            refs = refs[:-2] + refs[-1:]
        if has_add:
            a_ref, b_ref, add_ref, o_ref = refs
        else:
            a_ref, b_ref, o_ref = refs
            add_ref = None
        part = _dot(a_ref[...], b_ref[...], ca, cb)
        if nk == 1:
            if has_add:
                part = part + add_ref[...]
            o_ref[...] = part.astype(o_ref.dtype)
            return
        k = pl.program_id(2)

        @pl.when(k == 0)
        def _():
            o_ref[...] = part + add_ref[...] if has_add else part

        @pl.when(k > 0)
        def _():
            o_ref[...] += part

    in_specs = [a_spec, b_spec] + ([o_spec] if has_add else [])
    args = (a, b) + ((add,) if has_add else ())
    if order is not None:
        in_specs.append(pl.BlockSpec(order.shape, lambda j, i, k: (0, 0)))
        args += (order,)
    return pl.pallas_call(
        body, name=name, grid=(N // tn, M // tm, nk),
        in_specs=in_specs, out_specs=o_spec,
        out_shape=jax.ShapeDtypeStruct((M, N), out_dtype),
        compiler_params=_cp(("parallel", "parallel", "arbitrary")),
    )(*args)


def rms_fwd(x, g, name):
    R = x.shape[0]

    def body(x_ref, g_ref, y_ref):
        xv = x_ref[...]
        r = lax.rsqrt(jnp.mean(xv * xv, axis=-1, keepdims=True) + EPS)
        y_ref[...] = (xv * r * g_ref[...]).astype(y_ref.dtype)

    return pl.pallas_call(
        body, name=name, grid=(R // TM,),
        in_specs=[_rb(TM, D, 0), _const((1, D))], out_specs=_rb(TM, D, 0),
        out_shape=jax.ShapeDtypeStruct((R, D), MXU), compiler_params=_cp(("parallel",)),
    )(x, g)


def rms_bwd(dy, x, g, dres, name):
    R = x.shape[0]

    def body(dy_ref, x_ref, g_ref, dres_ref, dx_ref, dg_ref):
        i = pl.program_id(0)
        xv, dyv = x_ref[...], dy_ref[...]
        r = lax.rsqrt(jnp.mean(xv * xv, axis=-1, keepdims=True) + EPS)
        dyg = dyv * g_ref[...]
        dx_ref[...] = dres_ref[...] + r * dyg - xv * (r * r * r * jnp.mean(dyg * xv, axis=-1, keepdims=True))
        part = jnp.sum(dyv * xv * r, axis=0, keepdims=True)

        @pl.when(i == 0)
        def _():
            dg_ref[...] = part

        @pl.when(i > 0)
        def _():
            dg_ref[...] += part

    return pl.pallas_call(
        body, name=name, grid=(R // TM,),
        in_specs=[_rb(TM, D, 0), _rb(TM, D, 0), _const((1, D)), _rb(TM, D, 0)],
        out_specs=[_rb(TM, D, 0), _const((1, D))],
        out_shape=[jax.ShapeDtypeStruct((R, D), F32), jax.ShapeDtypeStruct((1, D), F32)],
        compiler_params=_cp(("arbitrary",)),
    )(dy, x, g, dres)


def _shift_down(cur, prev8, s):
    tm = cur.shape[0]
    rolled = pltpu.roll(cur, s, 0)
    rows8 = lax.broadcasted_iota(jnp.int32, (8, cur.shape[1]), 0)
    head = jnp.where(rows8 < s, pltpu.roll(prev8, s, 0), rolled[0:8])
    return jnp.concatenate([head, rolled[8:tm]], axis=0)


def _shift_up(cur, next8, s):
    tm = cur.shape[0]
    rolled = pltpu.roll(cur, tm - s, 0)
    rows8 = lax.broadcasted_iota(jnp.int32, (8, cur.shape[1]), 0)
    tail = jnp.where(rows8 >= 8 - s, pltpu.roll(next8, 8 - s, 0), rolled[tm - 8:tm])
    return jnp.concatenate([rolled[0:tm - 8], tail], axis=0)


def prep_fwd(proj, conv_w, conv_b, gb_row):
    R = proj.shape[0]
    t8 = TM // 8

    def body(x_ref, halo_ref, sm_ref, w_ref, b_ref, gb_ref, c_ref, qk_ref, gl_ref):
        i = pl.program_id(0)
        x = x_ref[...]
        halo = halo_ref[...]
        w = w_ref[...]
        c = x * w[3:4, :] + b_ref[...]
        for s in (1, 2, 3):
            c = c + _shift_down(x, halo, s) * w[3 - s:4 - s, :]
        c_ref[...] = c
        qk_ref[...] = c * _sigmoid(c)
        z = sm_ref[...] + gb_ref[...]
        lane = lax.broadcasted_iota(jnp.int32, z.shape, 1)
        row = lax.broadcasted_iota(jnp.int32, z.shape, 0) + i * TM
        valid = row >= NPADROWS
        logi = jnp.where(valid, z, NEG)
        logf = jnp.where(valid, _log_sigmoid(z), 0.0)
        gl_ref[...] = jnp.where(lane < 4, logi, jnp.where(lane < 8, logf, 0.0))

    return pl.pallas_call(
        body, name="prep_fwd", grid=(R // TM,),
        in_specs=[_rb(TM, 1024, QK0 // 1024),
                  pl.BlockSpec((8, 1024), lambda i: (jnp.maximum(i * t8 - 1, 0), QK0 // 1024)),
                  _rb(TM, 128, SM0 // 128), _const((4, 1024)), _const((1, 1024)), _const((1, 128))],
        out_specs=[_rb(TM, 1024, 0), _rb(TM, 1024, 0), _rb(TM, 128, 0)],
        out_shape=[jax.ShapeDtypeStruct((R, 1024), F32), jax.ShapeDtypeStruct((R, 1024), F32),
                   jax.ShapeDtypeStruct((R, 128), F32)],
        compiler_params=_cp(("parallel",)),
    )(proj, proj, proj, conv_w, conv_b, gb_row)


def merge_fwd(bm, bg, proj):
    R = bm.shape[0]

    def body(bm_ref, bg_ref, gm_ref, gg_ref, o_ref):
        o_ref[...] = (_sigmoid(gm_ref[...]) * bm_ref[...] + _sigmoid(gg_ref[...]) * bg_ref[...]).astype(o_ref.dtype)

    return pl.pallas_call(
        body, name="merge_fwd", grid=(R // TM,),
        in_specs=[_rb(TM, D, 0), _rb(TM, D, 0), _rb(TM, D, GM0 // D), _rb(TM, D, GG0 // D)],
        out_specs=_rb(TM, D, 0), out_shape=jax.ShapeDtypeStruct((R, D), MXU),
        compiler_params=_cp(("parallel",)),
    )(bm, bg, proj, proj)


def merge_bwd(dmerged, bm, bg, proj):
    R = bm.shape[0]

    def body(dm_ref, bm_ref, bg_ref, gm_ref, gg_ref, dbm_ref, dbg_ref, dp_ref):
        dm = dm_ref[...]
        sm, sg = _sigmoid(gm_ref[...]), _sigmoid(gg_ref[...])
        dbm_ref[...] = (dm * sm).astype(dbm_ref.dtype)
        dbg_ref[...] = (dm * sg).astype(dbg_ref.dtype)
        dp_ref[:, 0:D] = (dm * bm_ref[...] * sm * (1.0 - sm)).astype(dp_ref.dtype)
        dp_ref[:, D:2 * D] = (dm * bg_ref[...] * sg * (1.0 - sg)).astype(dp_ref.dtype)

    return pl.pallas_call(
        body, name="merge_bwd", grid=(R // TM,),
        in_specs=[_rb(TM, D, 0), _rb(TM, D, 0), _rb(TM, D, 0), _rb(TM, D, GM0 // D), _rb(TM, D, GG0 // D)],
        out_specs=[_rb(TM, D, 0), _rb(TM, D, 0), _rb(TM, 2 * D, GM0 // (2 * D))],
        out_shape=[jax.ShapeDtypeStruct((R, D), MXU), jax.ShapeDtypeStruct((R, D), MXU),
                   jax.ShapeDtypeStruct((R, NP), MXU)],
        compiler_params=_cp(("parallel",)),
    )(dmerged, bm, bg, proj, proj)


def swiglu_fwd(au):
    R = au.shape[0]
    tw = DFF // 2

    def body(a_ref, u_ref, o_ref):
        a = a_ref[...].astype(F32)
        o_ref[...] = (a * _sigmoid(a) * u_ref[...].astype(F32)).astype(o_ref.dtype)

    return pl.pallas_call(
        body, name="swiglu_fwd", grid=(R // TM, 2),
        in_specs=[pl.BlockSpec((TM, tw), lambda i, j: (i, j)), pl.BlockSpec((TM, tw), lambda i, j: (i, j + 2))],
        out_specs=pl.BlockSpec((TM, tw), lambda i, j: (i, j)),
        out_shape=jax.ShapeDtypeStruct((R, DFF), MXU), compiler_params=_cp(("parallel", "parallel")),
    )(au, au)


def swiglu_bwd(au, dff):
    R = au.shape[0]
    tw = DFF // 2

    def body(a_ref, u_ref, d_ref, dau_ref):
        a, d = a_ref[...].astype(F32), d_ref[...]
        s = _sigmoid(a)
        dau_ref[:, 0:DFF] = (d * u_ref[...].astype(F32) * s * (1.0 + a * (1.0 - s))).astype(dau_ref.dtype)
        dau_ref[:, DFF:2 * DFF] = (d * a * s).astype(dau_ref.dtype)

    tr = 256
    return pl.pallas_call(
        body, name="swiglu_bwd", grid=(R // tr,),
        in_specs=[_rb(tr, DFF, 0), _rb(tr, DFF, 1), _rb(tr, DFF, 0)],
        out_specs=_rb(tr, 2 * DFF, 0),
        out_shape=jax.ShapeDtypeStruct((R, 2 * DFF), MXU),
        compiler_params=_cp(("parallel",)),
    )(au, au, dff)


def final_loss(h2, gf, target):
    R = h2.shape[0]
    assert PADR == TM

    def body(h_ref, g_ref, t_ref, dh_ref, loss_ref, dg_ref):
        i = pl.program_id(0)
        hv = h_ref[...]
        r = lax.rsqrt(jnp.mean(hv * hv, axis=-1, keepdims=True) + EPS)
        g = g_ref[...]
        live = (i >= 1).astype(F32)
        e = (hv * r * g - t_ref[...]) * live
        dy = e * (1.0 / D)
        dyg = dy * g
        dh_ref[...] = r * dyg - hv * (r * r * r * jnp.mean(dyg * hv, axis=-1, keepdims=True))
        lpart = jnp.zeros((1, 128), F32) + 0.5 * jnp.sum(jnp.sum(e * e, axis=1, keepdims=True), axis=0, keepdims=True) * (1.0 / D)
        gpart = jnp.sum(dy * hv * r, axis=0, keepdims=True)

        @pl.when(i == 0)
        def _():
            loss_ref[...] = lpart
            dg_ref[...] = gpart

        @pl.when(i > 0)
        def _():
            loss_ref[...] += lpart
            dg_ref[...] += gpart

    return pl.pallas_call(
        body, name="final_loss", grid=(R // TM,),
        in_specs=[_rb(TM, D, 0), _const((1, D)), pl.BlockSpec((TM, D), lambda i: (jnp.maximum(i - 1, 0), 0))],
        out_specs=[_rb(TM, D, 0), _const((1, 128)), _const((1, D))],
        out_shape=[jax.ShapeDtypeStruct((R, D), F32), jax.ShapeDtypeStruct((1, 128), F32),
                   jax.ShapeDtypeStruct((1, D), F32)],
        compiler_params=_cp(("arbitrary",)),
    )(h2, gf, target)


def conv_bwd(dc, proj, conv_w, dproj):
    R = dc.shape[0]
    t8 = TM // 8
    nt = R // TM

    def body(dc_ref, nxt_ref, x_ref, prv_ref, w_ref, dp_in, dp_ref, dw_ref):
        del dp_in
        i = pl.program_id(0)
        dcv = dc_ref[...]
        nxt = nxt_ref[...] * (i < nt - 1).astype(F32)
        x = x_ref[...]
        prv = prv_ref[...]
        w = w_ref[...]
        dx = dcv * w[3:4, :]
        rows = [None] * 4
        rows[3] = jnp.sum(dcv * x, axis=0, keepdims=True)
        for s in (1, 2, 3):
            dx = dx + _shift_up(dcv, nxt, s) * w[3 - s:4 - s, :]
            rows[3 - s] = jnp.sum(dcv * _shift_down(x, prv, s), axis=0, keepdims=True)
        dp_ref[...] = dx.astype(dp_ref.dtype)
        part = jnp.concatenate(rows + [jnp.sum(dcv, axis=0, keepdims=True), jnp.zeros((3, 1024), F32)], axis=0)

        @pl.when(i == 0)
        def _():
            dw_ref[...] = part

        @pl.when(i > 0)
        def _():
            dw_ref[...] += part

    return pl.pallas_call(
        body, name="conv_bwd", grid=(nt,),
        in_specs=[_rb(TM, 1024, 0),
                  pl.BlockSpec((8, 1024), lambda i: (jnp.minimum((i + 1) * t8, nt * t8 - 1), 0)),
                  _rb(TM, 1024, QK0 // 1024),
                  pl.BlockSpec((8, 1024), lambda i: (jnp.maximum(i * t8 - 1, 0), QK0 // 1024)),
                  _const((4, 1024)), pl.BlockSpec(memory_space=pl.ANY)],
        out_specs=[_rb(TM, 1024, QK0 // 1024), _const((8, 1024))],
        out_shape=[jax.ShapeDtypeStruct((R, NP), MXU), jax.ShapeDtypeStruct((8, 1024), F32)],
        input_output_aliases={5: 0},
        compiler_params=_cp(("arbitrary",)),
    )(dc, dc, proj, proj, conv_w, dproj)


def small_bwd(dgl, dga, proj, gb_row, dproj):
    R = dgl.shape[0]

    def body(dgl_ref, dga_ref, sm_ref, gb_ref, dp_in, dp_ref, dgb_ref):
        del dp_in
        i = pl.program_id(0)
        z = sm_ref[...] + gb_ref[...]
        lane = lax.broadcasted_iota(jnp.int32, z.shape, 1)
        row = lax.broadcasted_iota(jnp.int32, z.shape, 0) + i * TM
        valid = row >= NPADROWS
        dgl_v = dgl_ref[...]
        dgate = jnp.where(valid, jnp.where(lane < 4, dgl_v, dgl_v * _sigmoid(-z)), 0.0)
        ds = jnp.where(lane < 8, dgate, dga_ref[...])
        dp_ref[...] = ds.astype(dp_ref.dtype)
        part = jnp.sum(jnp.where(lane < 8, dgate, 0.0), axis=0, keepdims=True)

        @pl.when(i == 0)
        def _():
            dgb_ref[...] = part

        @pl.when(i > 0)
        def _():
            dgb_ref[...] += part

    return pl.pallas_call(
        body, name="small_bwd", grid=(R // TM,),
        in_specs=[_rb(TM, 128, 0), _rb(TM, 128, 0), _rb(TM, 128, SM0 // 128), _const((1, 128)),
                  pl.BlockSpec(memory_space=pl.ANY)],
        out_specs=[_rb(TM, 128, SM0 // 128), _const((1, 128))],
        out_shape=[jax.ShapeDtypeStruct((R, NP), MXU), jax.ShapeDtypeStruct((1, 128), F32)],
        input_output_aliases={4: 0},
        compiler_params=_cp(("arbitrary",)),
    )(dgl, dga, proj, gb_row, dproj)


def _masks():
    r = lax.broadcasted_iota(jnp.int32, (L, L), 0)
    c = lax.broadcasted_iota(jnp.int32, (L, L), 1)
    return r >= c, r == c, r


def _to_row(col, eye):
    return jnp.sum(jnp.where(eye, col, 0.0), axis=0, keepdims=True)


def _to_col(row, eye):
    return jnp.sum(jnp.where(eye, row, 0.0), axis=1, keepdims=True)


def _mlstm_chunk(q, k, logi_c, logf_c, m, n):
    tril, eye, _ = _masks()
    logi_r, logf_r = _to_row(logi_c, eye), _to_row(logf_c, eye)
    b_c = jnp.sum(jnp.where(tril, logf_r, 0.0), axis=1, keepdims=True)
    b_r = _to_row(b_c, eye)
    g = jnp.sum(logf_c, axis=0, keepdims=True)
    dmat = jnp.where(tril, b_c - b_r + logi_r, NEG)
    mrow = jnp.maximum(b_c + m, jnp.max(dmat, axis=1, keepdims=True))
    dm = jnp.exp(dmat - mrow)
    s = _dot(q, k, 1, 1)
    w = dm * s
    a_in = jnp.exp(b_c + m - mrow)
    qn = jnp.sum(q * n, axis=1, keepdims=True)
    den = a_in * qn + jnp.sum(w, axis=1, keepdims=True)
    floor = jnp.exp(-mrow)
    nrm = jnp.maximum(jnp.abs(den), floor)
    wlog_c = g - b_c + logi_c
    m_new = jnp.maximum(g + m, jnp.max(wlog_c, axis=0, keepdims=True))
    a_st = jnp.exp(g + m - m_new)
    w_c = jnp.exp(wlog_c - m_new)
    return dict(b_c=b_c, g=g, dm=dm, s=s, w=w, a_in=a_in, qn=qn, den=den, floor=floor, nrm=nrm,
                m_new=m_new, a_st=a_st, w_c=w_c, tril=tril, eye=eye)


def _mlstm_head_fwd(h, glv, qk_ref, v_ref, mo_ref, hg_ref, hm_ref, ym_ref, cs_ref, nm_ref, c_s, nm_s):
    q = qk_ref[:, h * DQK:(h + 1) * DQK] * QSCALE
    k = qk_ref[:, 512 + h * DQK:512 + (h + 1) * DQK]
    v = v_ref[:, h * DV:(h + 1) * DV]
    C = c_s[h]
    n = nm_s[h, 0:1, :]
    m = nm_s[h, 1:2, 0:1]
    f = _mlstm_chunk(q, k, glv[:, h:h + 1], glv[:, 4 + h:5 + h], m, n)
    num = f["a_in"] * _dot(q, C, 1, 1) + _dot(f["w"], v, 1, 0)
    hh = num / f["nrm"]
    cs_ref[0, h] = C
    nm_ref[0, h] = nm_s[h]
    c_s[h] = f["a_st"] * C + _dot(f["w_c"] * v, k, 0, 0)
    n_new = f["a_st"] * n + jnp.sum(f["w_c"] * k, axis=0, keepdims=True)
    rowi = lax.broadcasted_iota(jnp.int32, (8, DQK), 0)
    nm_s[h] = jnp.where(rowi == 0, n_new, jnp.where(rowi == 1, f["m_new"], 0.0))
    rm = lax.rsqrt(jnp.mean(hh * hh, axis=-1, keepdims=True) + EPS)
    sl = slice(h * DV, (h + 1) * DV)
    hm_ref[:, sl] = hh
    ym_ref[:, sl] = (hh * rm * hg_ref[:, sl] * _sigmoid(mo_ref[:, sl])).astype(ym_ref.dtype)


def _mlstm_bwd_parts(dym_ref, hm_ref, qk_ref, cp_ref, v_ref, gl_ref, mo_ref, hg_ref, cs_ref, nm_ref,
                     dp_ref, dc_ref, dgl_ref, dhg_ref, dc_s, dn_s):
        def init():
            dc_s[...] = jnp.zeros_like(dc_s)
            dn_s[...] = jnp.zeros_like(dn_s)
            dhg_ref[...] = jnp.zeros_like(dhg_ref)

        def zero():
            dp_ref[...] = jnp.zeros_like(dp_ref)
            dc_ref[...] = jnp.zeros_like(dc_ref)
            dgl_ref[...] = jnp.zeros_like(dgl_ref)

        def compute():
            glv = gl_ref[...]
            lane = lax.broadcasted_iota(jnp.int32, (L, 128), 1)
            dgl = jnp.zeros((L, 128), F32)
            for h in range(NH):
                sl = slice(h * DV, (h + 1) * DV)
                sq = slice(h * DQK, (h + 1) * DQK)
                sk = slice(512 + h * DQK, 512 + (h + 1) * DQK)
                hh = hm_ref[:, sl]
                gain = hg_ref[:, sl]
                rm = lax.rsqrt(jnp.mean(hh * hh, axis=-1, keepdims=True) + EPS)
                sg = _sigmoid(mo_ref[:, sl])
                dyv = dym_ref[:, sl]
                dno = dyv * sg
                dp_ref[:, 1024 + h * DV:1024 + (h + 1) * DV] = (dyv * hh * rm * gain * sg * (1.0 - sg)).astype(dp_ref.dtype)
                dhg_ref[:, sl] += jnp.sum(dno * hh * rm, axis=0, keepdims=True)
                dnog = dno * gain
                dh = rm * dnog - hh * (rm * rm * rm * jnp.mean(dnog * hh, axis=-1, keepdims=True))
                q = qk_ref[:, sq] * QSCALE
                k = qk_ref[:, sk]
                v = v_ref[:, sl]
                C = cs_ref[0, h]
                n = nm_ref[0, h, 0:1, :]
                m = nm_ref[0, h, 1:2, 0:1]
                f = _mlstm_chunk(q, k, glv[:, h:h + 1], glv[:, 4 + h:5 + h], m, n)
                eye = f["eye"]
                a_in, nrm, den, w = f["a_in"], f["nrm"], f["den"], f["w"]
                dnum = dh / nrm
                dnrm = -jnp.sum(dh * hh, axis=1, keepdims=True) / nrm
                dden = jnp.where(jnp.abs(den) >= f["floor"], dnrm * jnp.sign(den), 0.0)
                dw = _dot(dnum, v, 1, 1) + dden
                dv = _dot(w, dnum, 0, 0)
                ds = dw * f["dm"]
                e = dw * w
                qc = _dot(q, C, 1, 1)
                dq = _dot(ds, k, 1, 0) + a_in * _dot(dnum, C, 1, 0) + (a_in * dden) * n
                dk = _dot(ds, q, 0, 0)
                dC_in = _dot(a_in * dnum, q, 0, 0)
                dn_in = jnp.sum((a_in * dden) * q, axis=0, keepdims=True)
                da_in = jnp.sum(dnum * qc, axis=1, keepdims=True) + dden * f["qn"]
                col_e = _to_col(jnp.sum(e, axis=0, keepdims=True), eye)
                db = jnp.sum(e, axis=1, keepdims=True) + da_in * a_in - col_e
                dlogi = col_e
                dCp = dc_s[h]
                dnp = dn_s[h, 0:1, :]
                a_st, w_c = f["a_st"], f["w_c"]
                da_st = (jnp.sum(jnp.sum(dCp * C, axis=1, keepdims=True), axis=0, keepdims=True)
                         + jnp.sum(dnp * n, axis=1, keepdims=True))
                vdc = _dot(v, dCp, 1, 0)
                dw_c = jnp.sum((vdc + dnp) * k, axis=1, keepdims=True)
                dv = dv + w_c * _dot(k, dCp, 1, 1)
                dk = dk + w_c * (vdc + dnp)
                fw = dw_c * w_c
                dg = jnp.sum(fw, axis=0, keepdims=True) + da_st * a_st
                db = db - fw
                dlogi = dlogi + fw
                rowc = lax.broadcasted_iota(jnp.int32, (L, 1), 0)
                db = db + jnp.where(rowc == L - 1, dg, 0.0)
                triu = lax.broadcasted_iota(jnp.int32, (L, L), 1) >= lax.broadcasted_iota(jnp.int32, (L, L), 0)
                dlogf = jnp.sum(jnp.where(triu, _to_row(db, eye), 0.0), axis=1, keepdims=True)
                dc_s[h] = a_st * dCp + dC_in
                dn_new = a_st * dnp + dn_in
                dn_s[h] = jnp.zeros((8, DQK), F32) + dn_new
                cq, ck = cp_ref[:, sq], cp_ref[:, sk]
                s_q, s_k = _sigmoid(cq), _sigmoid(ck)
                dc_ref[:, sq] = dq * QSCALE * s_q * (1.0 + cq * (1.0 - s_q))
                dc_ref[:, sk] = dk * s_k * (1.0 + ck * (1.0 - s_k))
                dp_ref[:, sl] = dv.astype(dp_ref.dtype)
                dgl = jnp.where(lane == h, dlogi, jnp.where(lane == 4 + h, dlogf, dgl))
            dgl_ref[...] = dgl

        return init, zero, compute


def _gla_logs(sm, a2p, b2, valid):
    za = _dot(sm, a2p, 1, 0) + b2
    return za, jnp.where(valid, _log_sigmoid(za) * (1.0 / TAU), 0.0)


def _valid_rows(c, width):
    row = lax.broadcasted_iota(jnp.int32, (L, width), 0) + c * L
    return row >= NPADROWS


def _gla_chunk(q, k, la):
    tril, _, _ = _masks()
    bc = _dot_exact(tril.astype(F32), la)
    btot = jnp.sum(la, axis=0, keepdims=True)
    ebc = jnp.exp(bc)
    qd = q * ebc
    ki = k * jnp.exp(-bc)
    ke = k * jnp.exp(btot - bc)
    att = jnp.where(tril, _dot(qd, ki, 1, 1), 0.0)
    return dict(tril=tril, bc=bc, btot=btot, ebc=ebc, qd=qd, ki=ki, ke=ke, att=att)


def _col128(row):
    r = lax.broadcasted_iota(jnp.int32, (DQK, DQK), 0)
    c = lax.broadcasted_iota(jnp.int32, (DQK, DQK), 1)
    return jnp.sum(jnp.where(r == c, row, 0.0), axis=1, keepdims=True)


def _gla_fwd_parts(c, q_ref, k_ref, v_ref, gr_ref, sm_ref, a2_ref, b2_ref, hg_ref, hgl_ref, yg_ref, ss_ref, s_s):
        def init():
            s_s[...] = jnp.zeros_like(s_s)

        def zero():
            hgl_ref[...] = jnp.zeros_like(hgl_ref)
            yg_ref[...] = jnp.zeros_like(yg_ref)
            ss_ref[...] = jnp.zeros_like(ss_ref)

        def compute():
            _, loga = _gla_logs(sm_ref[...], a2_ref[...], b2_ref[...], _valid_rows(c, 512))
            for h in range(NH):
                sq = slice(h * DQK, (h + 1) * DQK)
                sl = slice(h * DV, (h + 1) * DV)
                q = q_ref[:, sq] * QSCALE
                k = k_ref[:, sq]
                v = v_ref[:, sl]
                S = s_s[h]
                f = _gla_chunk(q, k, loga[:, sq])
                o = _dot(f["att"], v, 1, 0) + _dot(f["qd"], S, 1, 0)
                ss_ref[0, h] = S
                s_s[h] = _col128(jnp.exp(f["btot"])) * S + _dot(f["ke"], v, 0, 0)
                rg = lax.rsqrt(jnp.mean(o * o, axis=-1, keepdims=True) + EPS)
                gr = gr_ref[:, sl]
                hgl_ref[:, sl] = o
                yg_ref[:, sl] = (o * rg * hg_ref[:, sl] * gr * _sigmoid(gr)).astype(yg_ref.dtype)

        return init, zero, compute


def _gla_bwd_parts(c, dy_ref, ho_ref, q_ref, k_ref, v_ref, gr_ref, sm_ref, a2_ref, b2_ref, hg_ref, ss_ref,
                   dp_ref, dga_ref, da2_ref, db2_ref, dhg_ref, ds_s):
        def init():
            ds_s[...] = jnp.zeros_like(ds_s)
            da2_ref[...] = jnp.zeros_like(da2_ref)
            db2_ref[...] = jnp.zeros_like(db2_ref)
            dhg_ref[...] = jnp.zeros_like(dhg_ref)

        def zero():
            dp_ref[...] = jnp.zeros_like(dp_ref)
            dga_ref[...] = jnp.zeros_like(dga_ref)

        def compute():
            valid = _valid_rows(c, 512)
            sm = sm_ref[...]
            za, loga = _gla_logs(sm, a2_ref[...], b2_ref[...], valid)
            dloga = []
            for h in range(NH):
                sq = slice(h * DQK, (h + 1) * DQK)
                sl = slice(h * DV, (h + 1) * DV)
                o = ho_ref[:, sl]
                gain = hg_ref[:, sl]
                rg = lax.rsqrt(jnp.mean(o * o, axis=-1, keepdims=True) + EPS)
                gr = gr_ref[:, sl]
                sg = _sigmoid(gr)
                dyv = dy_ref[:, sl]
                dno = dyv * gr * sg
                dp_ref[:, 2048 + h * DV:2048 + (h + 1) * DV] = (
                    dyv * o * rg * gain * sg * (1.0 + gr * (1.0 - sg))).astype(dp_ref.dtype)
                dhg_ref[:, sl] += jnp.sum(dno * o * rg, axis=0, keepdims=True)
                dnog = dno * gain
                do = rg * dnog - o * (rg * rg * rg * jnp.mean(dnog * o, axis=-1, keepdims=True))
                q = q_ref[:, sq] * QSCALE
                k = k_ref[:, sq]
                v = v_ref[:, sl]
                S = ss_ref[0, h]
                f = _gla_chunk(q, k, loga[:, sq])
                tril, qd, ki, ke = f["tril"], f["qd"], f["ki"], f["ke"]
                dSp = ds_s[h]
                datt = jnp.where(tril, _dot(do, v, 1, 1), 0.0)
                dqd = _dot(do, S, 1, 1) + _dot(datt, ki, 1, 0)
                dki = _dot(datt, qd, 0, 0)
                dv = _dot(f["att"], do, 0, 0) + _dot(ke, dSp, 1, 0)
                dke = _dot(v, dSp, 1, 1)
                ebt = jnp.exp(f["btot"])
                dbtot = jnp.sum(dke * ke, axis=0, keepdims=True) + ebt * _to_row128(jnp.sum(dSp * S, axis=1, keepdims=True))
                ds_s[h] = _dot(qd, do, 0, 0) + _col128(ebt) * dSp
                dq = dqd * f["ebc"]
                dk = dki * jnp.exp(-f["bc"]) + dke * jnp.exp(f["btot"] - f["bc"])
                dbc = dqd * qd - dki * ki - dke * ke
                rowc = lax.broadcasted_iota(jnp.int32, (L, DQK), 0)
                dbc = dbc + jnp.where(rowc == L - 1, dbtot, 0.0)
                triu = lax.broadcasted_iota(jnp.int32, (L, L), 1) >= lax.broadcasted_iota(jnp.int32, (L, L), 0)
                dloga.append(_dot_exact(triu.astype(F32), dbc))
                dp_ref[:, sq] = (dq * QSCALE).astype(dp_ref.dtype)
                dp_ref[:, 512 + h * DQK:512 + (h + 1) * DQK] = dk.astype(dp_ref.dtype)
                dp_ref[:, 1024 + h * DV:1024 + (h + 1) * DV] = dv.astype(dp_ref.dtype)
            dza = jnp.where(valid, jnp.concatenate(dloga, axis=1) * (1.0 / TAU) * _sigmoid(-za), 0.0)
            dga_ref[...] = _dot(dza, a2_ref[...], 1, 1)
            da2_ref[...] += _dot(sm, dza, 0, 0)
            db2_ref[...] += jnp.sum(dza, axis=0, keepdims=True)

        return init, zero, compute


def mix_fwd(qk, proj, gl, m_head_g, a2p, b2, g_head_g):
    R = qk.shape[0]
    NC = R // L

    def body(qk_ref, mv_ref, gl_ref, mo_ref, mhg_ref, gq_ref, gk_ref, gv_ref, gr_ref, sm_ref, a2_ref, b2_ref, ghg_ref,
             hm_ref, ym_ref, cs_ref, nm_ref, hgl_ref, yg_ref, ss_ref, c_s, nm_s, s_s):
        c = pl.program_id(0)
        g_init, g_zero, g_compute = _gla_fwd_parts(c, gq_ref, gk_ref, gv_ref, gr_ref, sm_ref, a2_ref, b2_ref, ghg_ref,
                                                   hgl_ref, yg_ref, ss_ref, s_s)

        @pl.when(c <= CH0)
        def _():
            c_s[...] = jnp.zeros_like(c_s)
            nm_s[...] = jnp.zeros_like(nm_s)
            g_init()

        @pl.when(c < CH0)
        def _():
            hm_ref[...] = jnp.zeros_like(hm_ref)
            ym_ref[...] = jnp.zeros_like(ym_ref)
            cs_ref[...] = jnp.zeros_like(cs_ref)
            nm_ref[...] = jnp.zeros_like(nm_ref)
            g_zero()

        @pl.when(c >= CH0)
        def _():
            glv = gl_ref[...]
            for h in range(NH):
                _mlstm_head_fwd(h, glv, qk_ref, mv_ref, mo_ref, mhg_ref, hm_ref, ym_ref, cs_ref, nm_ref, c_s, nm_s)
            g_compute()

    st_m = pl.BlockSpec((1, NH, DV, DQK), lambda c: (c, 0, 0, 0))
    st_n = pl.BlockSpec((1, NH, 8, DQK), lambda c: (c, 0, 0, 0))
    st_g = pl.BlockSpec((1, NH, DQK, DV), lambda c: (c, 0, 0, 0))
    return pl.pallas_call(
        body, name="mix_fwd", grid=(NC,),
        in_specs=[_rb(L, 1024, 0), _rb(L, 1024, MV0 // 1024), _rb(L, 128, 0), _rb(L, 1024, MO0 // 1024),
                  _const((1, 1024)),
                  _rb(L, 512, GQ0 // 512), _rb(L, 512, GK0 // 512), _rb(L, 1024, GV0 // 1024),
                  _rb(L, 1024, GR0 // 1024), _rb(L, 128, SM0 // 128), _const((128, 512)), _const((1, 512)),
                  _const((1, 1024))],
        out_specs=[_rb(L, 1024, 0), _rb(L, 1024, 0), st_m, st_n, _rb(L, 1024, 0), _rb(L, 1024, 0), st_g],
        out_shape=[jax.ShapeDtypeStruct((R, 1024), F32), jax.ShapeDtypeStruct((R, 1024), MXU),
                   jax.ShapeDtypeStruct((NC, NH, DV, DQK), F32), jax.ShapeDtypeStruct((NC, NH, 8, DQK), F32),
                   jax.ShapeDtypeStruct((R, 1024), F32), jax.ShapeDtypeStruct((R, 1024), MXU),
                   jax.ShapeDtypeStruct((NC, NH, DQK, DV), F32)],
        scratch_shapes=[pltpu.VMEM((NH, DV, DQK), F32), pltpu.VMEM((NH, 8, DQK), F32),
                        pltpu.VMEM((NH, DQK, DV), F32)],
        compiler_params=_cp(("arbitrary",)),
    )(qk, proj, gl, proj, m_head_g, proj, proj, proj, proj, proj, a2p, b2, g_head_g)


def mix_bwd(dym, hm, qk, cpre, proj, gl, m_head_g, cs, nm, dyg, hgl, a2p, b2, g_head_g, ss, dproj):
    R = qk.shape[0]
    NC = R // L
    rev = lambda c: NC - 1 - c
    GW = GR0 + 1024 - GQ0

    def body(dym_ref, hm_ref, qk_ref, cp_ref, mv_ref, gl_ref, mo_ref, mhg_ref, cs_ref, nm_ref,
             dyg_ref, ho_ref, gq_ref, gk_ref, gv_ref, gr_ref, sm_ref, a2_ref, b2_ref, ghg_ref, ss_ref, dp_in,
             dp_ref, dc_ref, dgl_ref, dmhg_ref, dga_ref, da2_ref, db2_ref, dghg_ref, dc_s, dn_s, ds_s):
        del dp_in
        step = pl.program_id(0)
        c = NC - 1 - step
        m_init, m_zero, m_compute = _mlstm_bwd_parts(
            dym_ref, hm_ref, qk_ref, cp_ref, mv_ref, gl_ref, mo_ref, mhg_ref, cs_ref, nm_ref,
            dp_ref.at[:, 0:GQ0], dc_ref, dgl_ref, dmhg_ref, dc_s, dn_s)
        g_init, g_zero, g_compute = _gla_bwd_parts(
            c, dyg_ref, ho_ref, gq_ref, gk_ref, gv_ref, gr_ref, sm_ref, a2_ref, b2_ref, ghg_ref, ss_ref,
            dp_ref.at[:, GQ0:GQ0 + GW], dga_ref, da2_ref, db2_ref, dghg_ref, ds_s)

        @pl.when(step == 0)
        def _():
            m_init()
            g_init()

        @pl.when(c < CH0)
        def _():
            m_zero()
            g_zero()

        @pl.when(c >= CH0)
        def _():
            m_compute()
            g_compute()

    def rows(w, cb):
        return pl.BlockSpec((L, w), lambda c: (rev(c), cb))

    return pl.pallas_call(
        body, name="mix_bwd", grid=(NC,),
        in_specs=[rows(1024, 0), rows(1024, 0), rows(1024, 0), rows(1024, 0), rows(1024, MV0 // 1024), rows(128, 0),
                  rows(1024, MO0 // 1024), _const((1, 1024)),
                  pl.BlockSpec((1, NH, DV, DQK), lambda c: (rev(c), 0, 0, 0)),
                  pl.BlockSpec((1, NH, 8, DQK), lambda c: (rev(c), 0, 0, 0)),
                  rows(1024, 0), rows(1024, 0), rows(512, GQ0 // 512), rows(512, GK0 // 512),
                  rows(1024, GV0 // 1024), rows(1024, GR0 // 1024), rows(128, SM0 // 128),
                  _const((128, 512)), _const((1, 512)), _const((1, 1024)),
                  pl.BlockSpec((1, NH, DQK, DV), lambda c: (rev(c), 0, 0, 0)),
                  pl.BlockSpec(memory_space=pl.ANY)],
        out_specs=[rows(GQ0 + GW, 0), rows(1024, 0), rows(128, 0), _const((1, 1024)),
                   rows(128, 0), _const((128, 512)), _const((1, 512)), _const((1, 1024))],
        out_shape=[jax.ShapeDtypeStruct((R, NP), MXU), jax.ShapeDtypeStruct((R, 1024), F32),
                   jax.ShapeDtypeStruct((R, 128), F32), jax.ShapeDtypeStruct((1, 1024), F32),
                   jax.ShapeDtypeStruct((R, 128), F32), jax.ShapeDtypeStruct((128, 512), F32),
                   jax.ShapeDtypeStruct((1, 512), F32), jax.ShapeDtypeStruct((1, 1024), F32)],
        scratch_shapes=[pltpu.VMEM((NH, DV, DQK), F32), pltpu.VMEM((NH, 8, DQK), F32),
                        pltpu.VMEM((NH, DQK, DV), F32)],
        input_output_aliases={21: 0},
        compiler_params=_cp(("arbitrary",)),
    )(dym, hm, qk, cpre, proj, gl, proj, m_head_g, cs, nm, dyg, hgl, proj, proj, proj, proj, proj, a2p, b2,
      g_head_g, ss, dproj)


def _to_row128(col):
    r = lax.broadcasted_iota(jnp.int32, (DQK, DQK), 0)
    c = lax.broadcasted_iota(jnp.int32, (DQK, DQK), 1)
    return jnp.sum(jnp.where(r == c, col, 0.0), axis=0, keepdims=True)


def local_step(x, target, meta, norm1_g, wp, conv_w, conv_b, m_gate_b, g_a2, g_a2_b, m_head_g, g_head_g,
               norm2_g, final_g, late_weights, send_early, send_wp, first_order=None):
    seq = x.shape[0]
    assert seq % TM == 0
    h0 = jnp.concatenate([jnp.zeros((NPADROWS, D), F32), meta, x], axis=0)
    gb_row = jnp.zeros((1, 128), F32).at[0, 0:8].set(m_gate_b.reshape(8))
    a2p = jnp.zeros((128, 512), F32).at[8:8 + RANK].set(g_a2)
    mhg = m_head_g.reshape(1, 1024)
    ghg = g_head_g.reshape(1, 1024)

    xn = rms_fwd(h0, norm1_g, "rms1_fwd")
    proj = matmul(xn, wp, "nn", "proj_fwd", tm=1536, order=first_order)
    cpre, qk, gl = prep_fwd(proj, conv_w, conv_b, gb_row)
    hm, ym, cs, nm, hgl, yg, ss = mix_fwd(qk, proj, gl, mhg, a2p, g_a2_b, ghg)
    w_bm, w_bg, w_out, w_gu, w_down = late_weights(ym)
    bm = matmul(ym, w_bm, "nn", "branch_m_fwd")
    bg = matmul(yg, w_bg, "nn", "branch_g_fwd")
    merged = merge_fwd(bm, bg, proj)
    h1 = matmul(merged, w_out, "nn", "out_fwd", add=h0)
    hn = rms_fwd(h1, norm2_g, "rms2_fwd")
    au = matmul(hn, w_gu, "nn", "ff_in_fwd", out_dtype=MXU, tm=1536, tn=1408)
    ff = swiglu_fwd(au)
    h2 = matmul(ff, w_down, "nn", "ff_down_fwd", add=h1, tm=768, tk=2816)
    dh2, loss, d_final_g = final_loss(h2, final_g.reshape(1, D), target)

    d_w_down = matmul(ff, dh2, "tn", "ff_down_wgrad", tm=1408, tk=1536)
    dff = matmul(dh2, w_down, "nt", "ff_down_dgrad", tm=1536, tn=1408)
    dau = swiglu_bwd(au, dff)
    d_w_gu = matmul(hn, dau, "tn", "ff_in_wgrad", tm=1024, tn=1408, tk=1536)
    dhn = matmul(dau, w_gu, "nt", "ff_in_dgrad", tm=1536, tk=1408)
    dh1, d_norm2_g = rms_bwd(dhn, h1, norm2_g, dh2, "rms2_bwd")

    d_w_out = matmul(merged, dh1, "tn", "out_wgrad", tm=1024, tk=1536)
    dmerged = matmul(dh1, w_out, "nt", "out_dgrad")
    dbm, dbg, dproj = merge_bwd(dmerged, bm, bg, proj)
    d_w_bm = matmul(ym, dbm, "tn", "branch_m_wgrad", tm=1024, tk=1536)
    d_w_bg = matmul(yg, dbg, "tn", "branch_g_wgrad", tm=1024, tk=1536)
    token = send_early(dict(w_branch_m=d_w_bm, w_branch_g=d_w_bg, w_out=d_w_out, w_gu=d_w_gu, w_ff_down=d_w_down))
    dym = matmul(dbm, w_bm, "nt", "branch_m_dgrad", order=token)
    dyg = matmul(dbg, w_bg, "nt", "branch_g_dgrad")
    dproj, dc, dgl, d_mhg, dga, d_a2p, d_a2b, d_ghg = mix_bwd(
        dym, hm, qk, cpre, proj, gl, mhg, cs, nm, dyg, hgl, a2p, g_a2_b, ghg, ss, dproj)
    dproj, d_conv = conv_bwd(dc, proj, conv_w, dproj)
    dproj, d_gb = small_bwd(dgl, dga, proj, gb_row, dproj)
    d_wp = matmul(xn, dproj, "tn", "proj_wgrad", tm=1024, tn=1664, tk=1536)
    token = send_wp(d_wp)
    dxn = matmul(dproj, wp, "nt", "proj_dgrad", tm=1536, tk=1664, order=token)
    dh0, d_norm1_g = rms_bwd(dxn, h0, norm1_g, dh1, "rms1_bwd")

    grads = dict(
        norm1_g=d_norm1_g, conv_w=d_conv[0:4], conv_b=d_conv[4:5], m_gate_b=d_gb[0, 0:8].reshape(1, 2, 4),
        g_a2=d_a2p[8:8 + RANK], g_a2_b=d_a2b, m_head_g=d_mhg.reshape(NH, DV), g_head_g=d_ghg.reshape(NH, DV),
        norm2_g=d_norm2_g, final_g=d_final_g)
    return loss, dh0, grads


_SEGS = [(0, 1024, QK0), (1024, 2048, MV0), (2048, 2056, SM0), (2056, 3080, MO0), (3080, 5128, GQ0),
         (5128, 5144, SM0 + 8), (5144, 6168, GR0), (6168, 8216, GM0)]
SHARD_W = NPROJ // NDEV


def regroup_cols(w8):
    parts = []
    for lo, hi, _ in sorted(_SEGS, key=lambda s: s[2]):
        while lo < hi:
            j = lo // SHARD_W
            end = min(hi, (j + 1) * SHARD_W)
            parts.append(w8[j, :, lo - j * SHARD_W:end - j * SHARD_W])
            lo = end
    parts.append(jnp.zeros((w8.shape[1], NP - NPROJ), w8.dtype))
    return jnp.concatenate(parts, axis=1)


def ungroup_cols(g):
    blocks = []
    for j in range(NDEV):
        lo, hi = j * SHARD_W, (j + 1) * SHARD_W
        parts = []
        for s_lo, s_hi, s_at in _SEGS:
            a, b = max(lo, s_lo), min(hi, s_hi)
            if a < b:
                parts.append(g[:, s_at + a - s_lo:s_at + b - s_lo])
        blocks.append(jnp.concatenate(parts, axis=1))
    return jnp.stack(blocks)


def col_blocks(g):
    r, c8 = g.shape
    return jnp.transpose(g.reshape(r, NDEV, c8 // NDEV), (1, 0, 2))


def from_col_blocks(g8):
    n, r, c = g8.shape
    return jnp.transpose(g8, (1, 0, 2)).reshape(r, n * c)


_MESHID = pl.DeviceIdType.MESH
_RELS = [(0, 0, 1), (1, 0, 0), (0, 1, 0), (1, 1, 0), (1, 0, 1), (0, 1, 1), (1, 1, 1)]


def _flip(v, bit):
    return 1 - v if bit else v


def all_gather(arrs, name):
    n = len(arrs)

    def body(*refs):
        ins, outs = refs[:n], refs[n:2 * n]
        send_sems, recv_sems, local_sems = refs[2 * n:]
        x, y, c = lax.axis_index("x"), lax.axis_index("y"), lax.axis_index("c")
        me, sibling = (x, y, c), (x, y, 1 - c)
        chips = [(1 - x, y), (x, 1 - y), (1 - x, 1 - y)]

        def slot(p):
            return 4 * p[0] + 2 * p[1] + p[2]

        def copy(a, k, block, to, src=None):
            dst = outs[a].at[slot(block)]
            return pltpu.make_async_remote_copy(
                src_ref=dst if src is None else src, dst_ref=dst,
                send_sem=send_sems.at[a, k], recv_sem=recv_sems.at[a, k],
                device_id=to, device_id_type=_MESHID)

        mine = [pltpu.make_async_copy(ins[a], outs[a].at[slot(me)], local_sems.at[a]) for a in range(n)]
        for cp in mine:
            cp.start()
        first = []
        for a in range(n):
            first.append(copy(a, 0, me, sibling, src=ins[a]))
            first += [copy(a, 1 + j, me, (*chip, c), src=ins[a]) for j, chip in enumerate(chips)]
        for cp in first:
            cp.start()
        passed = []
        for j, chip in enumerate(chips):
            for a in range(n):
                copy(a, 1 + j, (*chip, c), me).wait_recv()
                fwd = copy(a, 4 + j, (*chip, c), sibling)
                fwd.start()
                passed.append(fwd)
        for a in range(n):
            copy(a, 0, sibling, me).wait_recv()
            for j, chip in enumerate(chips):
                copy(a, 4 + j, (*chip, 1 - c), me).wait_recv()
        for cp in first + passed:
            cp.wait_send()
        for cp in mine:
            cp.wait()

    anyspec = pl.BlockSpec(memory_space=pl.ANY)
    return pl.pallas_call(
        body, name=name,
        in_specs=[anyspec] * n, out_specs=[anyspec] * n,
        out_shape=[jax.ShapeDtypeStruct((NDEV,) + a.shape, a.dtype) for a in arrs],
        scratch_shapes=[pltpu.SemaphoreType.DMA((n, 7)), pltpu.SemaphoreType.DMA((n, 7)),
                        pltpu.SemaphoreType.DMA((n,))],
    )(*arrs)


def exchange(blocks, rep, name):
    n = len(blocks)

    def body(*refs):
        b_refs, r_ref = refs[:n], refs[n]
        ob_refs, or_ref = refs[n + 1:2 * n + 1], refs[2 * n + 1]
        send_sems, recv_sems, local_sems = refs[2 * n + 2:]
        x, y, c = lax.axis_index("x"), lax.axis_index("y"), lax.axis_index("c")
        me = 4 * x + 2 * y + c

        def pairs(src_slot, dst_slot):
            return [(b_refs[a].at[src_slot], ob_refs[a].at[dst_slot]) for a in range(n)] + [(r_ref, or_ref.at[dst_slot])]

        loc = [pltpu.make_async_copy(s, d, local_sems.at[a]) for a, (s, d) in enumerate(pairs(me, me))]
        for cp in loc:
            cp.start()
        sends = []
        for k, (fx, fy, fc) in enumerate(_RELS):
            peer = (_flip(x, fx), _flip(y, fy), _flip(c, fc))
            pid = 4 * peer[0] + 2 * peer[1] + peer[2]
            for a, (s, d) in enumerate(pairs(pid, me)):
                sends.append(pltpu.make_async_remote_copy(
                    src_ref=s, dst_ref=d, send_sem=send_sems.at[a, k], recv_sem=recv_sems.at[a, k],
                    device_id=peer, device_id_type=_MESHID))
        for cp in sends:
            cp.start()
        for k, (fx, fy, fc) in enumerate(_RELS):
            peer = (_flip(x, fx), _flip(y, fy), _flip(c, fc))
            pid = 4 * peer[0] + 2 * peer[1] + peer[2]
            for a, (s, d) in enumerate(pairs(pid, pid)):
                pltpu.make_async_remote_copy(
                    src_ref=s, dst_ref=d, send_sem=send_sems.at[a, k], recv_sem=recv_sems.at[a, k],
                    device_id=peer, device_id_type=_MESHID).wait_recv()
        for cp in sends:
            cp.wait_send()
        for cp in loc:
            cp.wait()

    anyspec = pl.BlockSpec(memory_space=pl.ANY)
    return pl.pallas_call(
        body, name=name,
        in_specs=[anyspec] * (n + 1), out_specs=[anyspec] * (n + 1),
        out_shape=[jax.ShapeDtypeStruct(b.shape, b.dtype) for b in blocks]
        + [jax.ShapeDtypeStruct((NDEV,) + rep.shape, rep.dtype)],
        scratch_shapes=[pltpu.SemaphoreType.DMA((n + 1, 7)), pltpu.SemaphoreType.DMA((n + 1, 7)),
                        pltpu.SemaphoreType.DMA((n + 1,))],
    )(*blocks, rep)


_HBM = pl.BlockSpec(memory_space=pltpu.HBM)
_SEM = pl.BlockSpec(memory_space=pltpu.SEMAPHORE)
_EFFECT = pltpu.SideEffectType.DATAFLOW_SIDE_EFFECTING


def _peer_ids():
    x, y, c = lax.axis_index("x"), lax.axis_index("y"), lax.axis_index("c")
    peers = []
    for fx, fy, fc in _RELS:
        p = (_flip(x, fx), _flip(y, fy), _flip(c, fc))
        peers.append((p, 4 * p[0] + 2 * p[1] + p[2]))
    return 4 * x + 2 * y + c, peers


def _split_copy(src, land, a, k, peer, src_slot, dst_slot, send_sems, recv_sems):
    return pltpu.make_async_remote_copy(
        src_ref=src if src_slot is None else src.at[src_slot], dst_ref=land.at[dst_slot],
        send_sem=send_sems.at[7 * a + k], recv_sem=recv_sems.at[7 * a + k], device_id=peer, device_id_type=_MESHID)


def send_start(srcs, per_peer, order, name):
    n = len(srcs)
    lands = [lax.empty((NDEV,) + (s.shape[1:] if per_peer else s.shape), s.dtype) for s in srcs]

    def body(*refs):
        src_refs, land_refs = refs[1:1 + n], refs[1 + n:1 + 2 * n]
        send_sems, recv_sems = refs[1 + 2 * n], refs[2 + 2 * n]
        token, local_sem = refs[3 + 4 * n], refs[4 + 4 * n]
        me, peers = _peer_ids()
        for a in range(n):
            own = pltpu.make_async_copy(src_refs[a].at[me] if per_peer else src_refs[a], land_refs[a].at[me], local_sem)
            own.start()
            own.wait()
        for a in range(n):
            for k, (peer, pid) in enumerate(peers):
                _split_copy(src_refs[a], land_refs[a], a, k, peer, pid if per_peer else None, me,
                            send_sems, recv_sems).start()
        token[...] = jnp.zeros_like(token)

    outs = pl.pallas_call(
        body, name=name,
        in_specs=[pl.BlockSpec(memory_space=pl.ANY)] + [_HBM] * (2 * n),
        out_shape=(pltpu.SemaphoreType.DMA((7 * n,)), pltpu.SemaphoreType.DMA((7 * n,)),
                   *[pltpu.HBM(s.shape, s.dtype) for s in srcs], *[pltpu.HBM(l.shape, l.dtype) for l in lands],
                   jax.ShapeDtypeStruct((8, 128), F32)),
        out_specs=(_SEM, _SEM, *[_HBM] * (2 * n), pl.BlockSpec(memory_space=pltpu.VMEM)),
        input_output_aliases={1 + i: 2 + i for i in range(2 * n)},
        scratch_shapes=[pltpu.SemaphoreType.DMA],
        compiler_params=pltpu.CompilerParams(has_side_effects=_EFFECT),
    )(order, *[pltpu.with_memory_space_constraint(s, pltpu.HBM) for s in srcs],
      *[pltpu.with_memory_space_constraint(l, pltpu.HBM) for l in lands])
    return (n, per_peer, outs[0], outs[1], outs[2:2 + n], outs[2 + n:2 + 2 * n]), outs[2 + 2 * n]


def send_wait(handle, after, name):
    n, per_peer, send_sems, recv_sems, src_thru, land_thru = handle

    def body(*refs):
        src_refs, land_refs = refs[:n], refs[n:2 * n]
        s_sems, r_sems = refs[2 * n], refs[2 * n + 1]
        _, peers = _peer_ids()
        for a in range(n):
            for k, (peer, pid) in enumerate(peers):
                cp = _split_copy(src_refs[a], land_refs[a], a, k, peer, pid if per_peer else None, pid, s_sems, r_sems)
                cp.wait_send()
                cp.wait_recv()

    outs = pl.pallas_call(
        body, name=name,
        in_specs=[_HBM] * (2 * n) + [_SEM, _SEM, pl.BlockSpec(memory_space=pl.ANY)],
        out_shape=tuple(pltpu.HBM(t.shape, t.dtype) for t in (*src_thru, *land_thru)),
        out_specs=tuple([_HBM] * (2 * n)),
        input_output_aliases={i: i for i in range(2 * n)},
        compiler_params=pltpu.CompilerParams(has_side_effects=_EFFECT),
    )(*src_thru, *land_thru, send_sems, recv_sems, after)
    return list(outs[n:2 * n])


def adamw(parts, w, m, v, name, tr):
    npart, r, c = parts.shape
    c1 = 1.0 - ADAM_B1 ** ADAM_STEP
    c2 = 1.0 - ADAM_B2 ** ADAM_STEP

    def body(p_ref, w_ref, m_ref, v_ref, g_ref, d_ref, nm_ref, nv_ref):
        g = p_ref[0].astype(F32)
        for j in range(1, npart):
            g = g + p_ref[j].astype(F32)
        mn = ADAM_B1 * m_ref[...] + (1.0 - ADAM_B1) * g
        vn = ADAM_B2 * v_ref[...] + (1.0 - ADAM_B2) * (g * g)
        g_ref[...] = g
        nm_ref[...] = mn
        nv_ref[...] = vn
        d_ref[...] = -ADAM_LR * ((mn / c1) / (jnp.sqrt(vn / c2) + ADAM_EPS) + ADAM_WD * w_ref[...])

    spec = _rb(tr, c, 0)
    return pl.pallas_call(
        body, name=name, grid=(r // tr,),
        in_specs=[pl.BlockSpec((npart, tr, c), lambda i: (0, i, 0)), spec, spec, spec],
        out_specs=[spec] * 4, out_shape=[jax.ShapeDtypeStruct((r, c), F32)] * 4,
        compiler_params=_cp(("parallel",)),
    )(parts, w, m, v)


def sum_parts(parts, name, tc):
    npart, r, c = parts.shape

    def body(p_ref, o_ref):
        g = p_ref[0].astype(F32)
        for j in range(1, npart):
            g = g + p_ref[j].astype(F32)
        o_ref[...] = g

    return pl.pallas_call(
        body, name=name, grid=(c // tc,),
        in_specs=[pl.BlockSpec((npart, r, tc), lambda i: (0, 0, i))],
        out_specs=pl.BlockSpec((r, tc), lambda i: (0, i)),
        out_shape=jax.ShapeDtypeStruct((r, c), F32),
        compiler_params=_cp(("parallel",)),
    )(parts)


TINY = [("meta_tokens", (16, 1024)), ("conv_w", (4, 1024)), ("g_a2", (16, 512)), ("m_head_g", (4, 256)),
        ("g_head_g", (4, 256))]
REPL = [("norm1_g", (1, 1024)), ("conv_b", (1, 1024)), ("m_gate_b", (1, 2, 4)), ("g_a2_b", (1, 512)),
        ("norm2_g", (1, 1024)), ("final_g", (1024,))]
TINY_SIZE = 16 * 1024 + 4 * 1024 + 16 * 512 + 2 * 4 * 256
REPL_SIZE = 1024 + 1024 + 8 + 512 + 1024 + 1024
ROWS_GATHER = 8
ROWS_REP = 40
ROWS_OWN = 16


def pack_rows(vecs, rows):
    flat = jnp.concatenate([v.reshape(-1) for v in vecs])
    return jnp.pad(flat, (0, rows * 1024 - flat.shape[0])).reshape(rows, 1024)


def unpack_rows(packed, shapes):
    flat = packed.reshape(-1)
    out, off = [], 0
    for s in shapes:
        n = 1
        for d in s:
            n *= d
        out.append(flat[off:off + n].reshape(s))
        off += n
    return out


def kernel(x, meta_tokens, norm1_g, w_in, conv_w, conv_b, m_gate_b, g_a2, g_a2_b, m_head_g, g_head_g, w_branch_m, w_branch_g, w_out, norm2_g, w_ff_gate, w_ff_up, w_ff_down, final_g, loss_target, m_meta_tokens, m_norm1_g, m_w_in, m_conv_w, m_conv_b, m_m_gate_b, m_g_a2, m_g_a2_b, m_m_head_g, m_g_head_g, m_w_branch_m, m_w_branch_g, m_w_out, m_norm2_g, m_w_ff_gate, m_w_ff_up, m_w_ff_down, m_final_g, v_meta_tokens, v_norm1_g, v_w_in, v_conv_w, v_conv_b, v_m_gate_b, v_g_a2, v_g_a2_b, v_m_head_g, v_g_head_g, v_w_branch_m, v_w_branch_g, v_w_out, v_norm2_g, v_w_ff_gate, v_w_ff_up, v_w_ff_down, v_final_g):
    w_sh = dict(meta_tokens=meta_tokens, w_in=w_in[0], conv_w=conv_w[0], g_a2=g_a2[0], m_head_g=m_head_g[0],
                g_head_g=g_head_g[0], w_branch_m=w_branch_m[0], w_branch_g=w_branch_g[0], w_out=w_out[0],
                w_ff_gate=w_ff_gate[0], w_ff_up=w_ff_up[0], w_ff_down=w_ff_down[0])
    m_sh = dict(meta_tokens=m_meta_tokens, w_in=m_w_in[0], conv_w=m_conv_w[0], g_a2=m_g_a2[0],
                m_head_g=m_m_head_g[0], g_head_g=m_g_head_g[0], w_branch_m=m_w_branch_m[0],
                w_branch_g=m_w_branch_g[0], w_out=m_w_out[0], w_ff_gate=m_w_ff_gate[0], w_ff_up=m_w_ff_up[0],
                w_ff_down=m_w_ff_down[0])
    v_sh = dict(meta_tokens=v_meta_tokens, w_in=v_w_in[0], conv_w=v_conv_w[0], g_a2=v_g_a2[0],
                m_head_g=v_m_head_g[0], g_head_g=v_g_head_g[0], w_branch_m=v_w_branch_m[0],
                w_branch_g=v_w_branch_g[0], w_out=v_w_out[0], w_ff_gate=v_w_ff_gate[0], w_ff_up=v_w_ff_up[0],
                w_ff_down=v_w_ff_down[0])
    w_rep = dict(norm1_g=norm1_g, conv_b=conv_b, m_gate_b=m_gate_b, g_a2_b=g_a2_b, norm2_g=norm2_g, final_g=final_g)
    m_rep = dict(norm1_g=m_norm1_g, conv_b=m_conv_b, m_gate_b=m_m_gate_b, g_a2_b=m_g_a2_b, norm2_g=m_norm2_g,
                 final_g=m_final_g)
    v_rep = dict(norm1_g=v_norm1_g, conv_b=v_conv_b, m_gate_b=v_m_gate_b, g_a2_b=v_g_a2_b, norm2_g=v_norm2_g,
                 final_g=v_final_g)
    dev = 4 * lax.axis_index("x") + 2 * lax.axis_index("y") + lax.axis_index("c")
    tiny_names = [n for n, _ in TINY]
    repl_names = [n for n, _ in REPL]
    tiny_shard_shapes = [(s[0], s[1] // NDEV) for _, s in TINY]

    in8, tiny8 = all_gather([w_sh["w_in"].astype(MXU), pack_rows([w_sh[n] for n in tiny_names], ROWS_GATHER)],
                            "param_all_gather")
    late_names = ["w_branch_m", "w_branch_g", "w_out", "w_ff_gate", "w_ff_up", "w_ff_down"]
    late, first_order = send_start([w_sh[n].astype(MXU) for n in late_names], False, tiny8, "late_weights_start")
    wp = regroup_cols(in8)
    handles = {}

    def late_weights(after):
        bm8, bg8, out8, ffg8, ffu8, ffd8 = send_wait(late, after, "late_weights_wait")
        w_gu = jnp.concatenate([from_col_blocks(ffg8), from_col_blocks(ffu8)], axis=1)
        return bm8.reshape(D, D), bg8.reshape(D, D), out8.reshape(D, D), w_gu, ffd8.reshape(DFF, D)

    def send_early(g):
        blocks = [g["w_branch_m"].reshape(NDEV, D // NDEV, D).astype(WIRE),
                  g["w_branch_g"].reshape(NDEV, D // NDEV, D).astype(WIRE),
                  g["w_out"].reshape(NDEV, D // NDEV, D).astype(WIRE),
                  col_blocks(g["w_gu"][:, :DFF]).astype(WIRE), col_blocks(g["w_gu"][:, DFF:]).astype(WIRE),
                  g["w_ff_down"].reshape(NDEV, DFF // NDEV, D).astype(WIRE)]
        handles["early"], token = send_start(blocks, True, blocks[0], "early_grads_start")
        return token

    def send_wp(d_wp):
        blocks = [ungroup_cols(d_wp).astype(WIRE)]
        handles["wp"], token = send_start(blocks, True, blocks[0], "proj_grads_start")
        return token

    tiny_full = {}
    for j in range(NDEV):
        for name, blk in zip(tiny_names, unpack_rows(tiny8[j], tiny_shard_shapes)):
            tiny_full.setdefault(name, []).append(blk)
    tiny_full = {n: jnp.concatenate(v, axis=1) for n, v in tiny_full.items()}

    loss, dh0, g = local_step(
        x[0], loss_target[0], tiny_full["meta_tokens"], norm1_g, wp, tiny_full["conv_w"], conv_b, m_gate_b[0],
        tiny_full["g_a2"], g_a2_b, tiny_full["m_head_g"], tiny_full["g_head_g"], norm2_g, final_g,
        late_weights, send_early, send_wp, first_order)

    g["meta_tokens"] = dh0[NPADROWS:PADR]
    rep = pack_rows([g[n] for n in tiny_names + repl_names] + [loss[0, 0:1]], ROWS_REP)
    (got_rep,) = exchange([], rep, "small_grad_exchange")
    got_early = send_wait(handles["early"], got_rep, "early_grads_wait")
    (got_wp,) = send_wait(handles["wp"], got_rep, "proj_grads_wait")

    result = {}

    def update(name, parts, tr):
        outs = adamw(parts, w_sh[name], m_sh[name], v_sh[name], "adamw_" + name, tr)
        for kind, arr in zip(("grad", "delta", "new_m", "new_v"), outs):
            result[kind, name] = arr[None]

    update("w_in", got_wp, 128)
    update("w_branch_m", got_early[0], 128)
    update("w_branch_g", got_early[1], 128)
    update("w_out", got_early[2], 128)
    update("w_ff_gate", got_early[3], 256)
    update("w_ff_up", got_early[4], 256)
    update("w_ff_down", got_early[5], DFF // NDEV)

    rep_sum = sum_parts(got_rep, "sum_small", 1024)
    rep_g = unpack_rows(rep_sum, [s for _, s in TINY] + [s for _, s in REPL] + [(1,)])
    own_g = [lax.dynamic_slice_in_dim(gf, dev * ss[1], ss[1], axis=1) for gf, ss in zip(rep_g, tiny_shard_shapes)]
    own_g += rep_g[len(TINY):len(TINY) + len(REPL)]
    w_all = {**w_sh, **w_rep}
    m_all = {**m_sh, **m_rep}
    v_all = {**v_sh, **v_rep}
    names = tiny_names + repl_names
    outs = adamw(pack_rows(own_g, ROWS_OWN)[None], pack_rows([w_all[n] for n in names], ROWS_OWN),
                 pack_rows([m_all[n] for n in names], ROWS_OWN), pack_rows([v_all[n] for n in names], ROWS_OWN),
                 "adamw_small", ROWS_OWN)
    shapes = tiny_shard_shapes + [s for _, s in REPL]
    for kind, packed in zip(("grad", "delta", "new_m", "new_v"), outs):
        for name, arr in zip(names, unpack_rows(packed, shapes)):
            result[kind, name] = arr[None] if name in tiny_names and name != "meta_tokens" else arr
    loss_total = rep_g[-1][0]
    order = ["meta_tokens", "norm1_g", "w_in", "conv_w", "conv_b", "m_gate_b", "g_a2", "g_a2_b", "m_head_g", "g_head_g",
             "w_branch_m", "w_branch_g", "w_out", "norm2_g", "w_ff_gate", "w_ff_up", "w_ff_down", "final_g"]
    grad_x = dh0[PADR:][None]
    return (loss_total, grad_x, *[result[kind, n] for kind in ("grad", "delta", "new_m", "new_v") for n in order])
```

```python
import functools

import jax
import jax.numpy as jnp
from jax import lax
from jax.experimental import pallas as pl
from jax.experimental.pallas import tpu as pltpu

F32 = jnp.float32
MXU = jnp.bfloat16
WIRE = jnp.bfloat16

D = 1024
NH = 4
DV = 256
DQK = 128
L = 64
NMETA = 16
PADR = 512
CH0 = PADR // L - 1
NPADROWS = PADR - NMETA
RANK = 16
DFF = 2816
EPS = 1e-6
TAU = 16.0
QSCALE = DQK ** -0.5
NEG = -1e30
NDEV = 8

MV0, MO0, GQ0, GK0, GV0, GR0, QK0, GM0, GG0, SM0 = 0, 1024, 2048, 2560, 3072, 4096, 5120, 6144, 7168, 8192
NP = 8320
NPROJ = 8216

ADAM_LR, ADAM_B1, ADAM_B2, ADAM_EPS, ADAM_WD, ADAM_STEP = 0.001, 0.9, 0.999, 1e-08, 0.01, 10

VMEM_LIMIT = 56 * 1024 * 1024
TM = 512


def _cp(sem):
    return pltpu.CompilerParams(dimension_semantics=sem, vmem_limit_bytes=VMEM_LIMIT)


def _sigmoid(x):
    return 1.0 / (1.0 + jnp.exp(-x))


def _log_sigmoid(x):
    return jnp.minimum(x, 0.0) - jnp.log1p(jnp.exp(-jnp.abs(x)))


def _dot(a, b, ca, cb):
    return lax.dot_general(a.astype(MXU), b.astype(MXU), (((ca,), (cb,)), ((), ())), preferred_element_type=F32)


def _dot_exact(a, b):
    return lax.dot_general(a, b, (((1,), (0,)), ((), ())), precision=lax.Precision.HIGHEST,
                           preferred_element_type=F32)


def _rb(tm, w, cb):
    return pl.BlockSpec((tm, w), lambda i: (i, cb))


def _const(shape):
    nd = len(shape)
    return pl.BlockSpec(shape, lambda i: (0,) * nd)


def _pick(n, target):
    if n <= target:
        return n
    best = None
    for t in range(128, target + 1, 128):
        if n % t == 0:
            best = t
    assert best is not None, (n, target)
    return best


def matmul(a, b, mode, name, add=None, out_dtype=F32, tm=512, tn=1664, tk=1024, order=None):
    if mode == "nn":
        (M, K), (K2, N) = a.shape, b.shape
    elif mode == "nt":
        (M, K), (N, K2) = a.shape, b.shape
    else:
        (K, M), (K2, N) = a.shape, b.shape
    assert K == K2, (a.shape, b.shape, mode)
    tm, tn, tk = _pick(M, tm), _pick(N, tn), _pick(K, tk)
    nk = K // tk
    assert nk == 1 or out_dtype == F32
    ca, cb = {"nn": (1, 0), "nt": (1, 1), "tn": (0, 0)}[mode]
    a_spec = {"nn": pl.BlockSpec((tm, tk), lambda j, i, k: (i, k)),
              "nt": pl.BlockSpec((tm, tk), lambda j, i, k: (i, k)),
              "tn": pl.BlockSpec((tk, tm), lambda j, i, k: (k, i))}[mode]
    b_spec = {"nn": pl.BlockSpec((tk, tn), lambda j, i, k: (k, j)),
              "nt": pl.BlockSpec((tn, tk), lambda j, i, k: (j, k)),
              "tn": pl.BlockSpec((tk, tn), lambda j, i, k: (k, j))}[mode]
    o_spec = pl.BlockSpec((tm, tn), lambda j, i, k: (i, j))
    has_add = add is not None

    def body(*refs):
        if order is not None:
            refs = refs[:-2] + refs[-1:]
        if has_add:
            a_ref, b_ref, add_ref, o_ref = refs
        else:
            a_ref, b_ref, o_ref = refs
            add_ref = None
        part = _dot(a_ref[...], b_ref[...], ca, cb)
        if nk == 1:
            if has_add:
                part = part + add_ref[...]
            o_ref[...] = part.astype(o_ref.dtype)
            return
        k = pl.program_id(2)

        @pl.when(k == 0)
        def _():
            o_ref[...] = part + add_ref[...] if has_add else part

        @pl.when(k > 0)
        def _():
            o_ref[...] += part

    in_specs = [a_spec, b_spec] + ([o_spec] if has_add else [])
    args = (a, b) + ((add,) if has_add else ())
    if order is not None:
        in_specs.append(pl.BlockSpec(order.shape, lambda j, i, k: (0, 0)))
        args += (order,)
    return pl.pallas_call(
        body, name=name, grid=(N // tn, M // tm, nk),
        in_specs=in_specs, out_specs=o_spec,
        out_shape=jax.ShapeDtypeStruct((M, N), out_dtype),
        compiler_params=_cp(("parallel", "parallel", "arbitrary")),
    )(*args)


def rms_fwd(x, g, name):
    R = x.shape[0]

    def body(x_ref, g_ref, y_ref):
        xv = x_ref[...]
        r = lax.rsqrt(jnp.mean(xv * xv, axis=-1, keepdims=True) + EPS)
        y_ref[...] = (xv * r * g_ref[...]).astype(y_ref.dtype)

    return pl.pallas_call(
        body, name=name, grid=(R // TM,),
        in_specs=[_rb(TM, D, 0), _const((1, D))], out_specs=_rb(TM, D, 0),
        out_shape=jax.ShapeDtypeStruct((R, D), MXU), compiler_params=_cp(("parallel",)),
    )(x, g)


def rms_bwd(dy, x, g, dres, name):
    R = x.shape[0]

    def body(dy_ref, x_ref, g_ref, dres_ref, dx_ref, dg_ref):
        i = pl.program_id(0)
        xv, dyv = x_ref[...], dy_ref[...]
        r = lax.rsqrt(jnp.mean(xv * xv, axis=-1, keepdims=True) + EPS)
        dyg = dyv * g_ref[...]
        dx_ref[...] = dres_ref[...] + r * dyg - xv * (r * r * r * jnp.mean(dyg * xv, axis=-1, keepdims=True))
        part = jnp.sum(dyv * xv * r, axis=0, keepdims=True)

        @pl.when(i == 0)
        def _():
            dg_ref[...] = part

        @pl.when(i > 0)
        def _():
            dg_ref[...] += part

    return pl.pallas_call(
        body, name=name, grid=(R // TM,),
        in_specs=[_rb(TM, D, 0), _rb(TM, D, 0), _const((1, D)), _rb(TM, D, 0)],
        out_specs=[_rb(TM, D, 0), _const((1, D))],
        out_shape=[jax.ShapeDtypeStruct((R, D), F32), jax.ShapeDtypeStruct((1, D), F32)],
        compiler_params=_cp(("arbitrary",)),
    )(dy, x, g, dres)


def rms_bwd_input(dy, x, g, dres, name):
    R = x.shape[0]
    assert PADR == TM

    def body(dy_ref, x_ref, g_ref, dres_ref, dx_ref, dmeta_ref, dg_ref):
        i = pl.program_id(0)
        xv, dyv = x_ref[...], dy_ref[...]
        r = lax.rsqrt(jnp.mean(xv * xv, axis=-1, keepdims=True) + EPS)
        dyg = dyv * g_ref[...]
        dx = dres_ref[...] + r * dyg - xv * (r * r * r * jnp.mean(dyg * xv, axis=-1, keepdims=True))
        dx_ref[...] = dx
        part = jnp.sum(dyv * xv * r, axis=0, keepdims=True)

        @pl.when(i == 0)
        def _():
            dg_ref[...] = part
            dmeta_ref[...] = dx[NPADROWS:PADR]

        @pl.when(i > 0)
        def _():
            dg_ref[...] += part

    return pl.pallas_call(
        body, name=name, grid=(R // TM,),
        in_specs=[_rb(TM, D, 0), _rb(TM, D, 0), _const((1, D)), _rb(TM, D, 0)],
        out_specs=[pl.BlockSpec((TM, D), lambda i: (jnp.maximum(i - 1, 0), 0)), _const((NMETA, D)), _const((1, D))],
        out_shape=[jax.ShapeDtypeStruct((R - PADR, D), F32), jax.ShapeDtypeStruct((NMETA, D), F32),
                   jax.ShapeDtypeStruct((1, D), F32)],
        compiler_params=_cp(("arbitrary",)),
    )(dy, x, g, dres)


def _shift_down(cur, prev8, s):
    tm = cur.shape[0]
    rolled = pltpu.roll(cur, s, 0)
    rows8 = lax.broadcasted_iota(jnp.int32, (8, cur.shape[1]), 0)
    head = jnp.where(rows8 < s, pltpu.roll(prev8, s, 0), rolled[0:8])
    return jnp.concatenate([head, rolled[8:tm]], axis=0)


def _shift_up(cur, next8, s):
    tm = cur.shape[0]
    rolled = pltpu.roll(cur, tm - s, 0)
    rows8 = lax.broadcasted_iota(jnp.int32, (8, cur.shape[1]), 0)
    tail = jnp.where(rows8 >= 8 - s, pltpu.roll(next8, 8 - s, 0), rolled[tm - 8:tm])
    return jnp.concatenate([rolled[0:tm - 8], tail], axis=0)


def prep_fwd(proj, conv_w, conv_b, gb_row):
    R = proj.shape[0]
    t8 = TM // 8

    def body(x_ref, halo_ref, sm_ref, w_ref, b_ref, gb_ref, c_ref, qk_ref, gl_ref):
        i = pl.program_id(0)
        x = x_ref[...]
        halo = halo_ref[...]
        w = w_ref[...]
        c = x * w[3:4, :] + b_ref[...]
        for s in (1, 2, 3):
            c = c + _shift_down(x, halo, s) * w[3 - s:4 - s, :]
        c_ref[...] = c
        qk_ref[...] = c * _sigmoid(c)
        z = sm_ref[...] + gb_ref[...]
        lane = lax.broadcasted_iota(jnp.int32, z.shape, 1)
        row = lax.broadcasted_iota(jnp.int32, z.shape, 0) + i * TM
        valid = row >= NPADROWS
        logi = jnp.where(valid, z, NEG)
        logf = jnp.where(valid, _log_sigmoid(z), 0.0)
        gl_ref[...] = jnp.where(lane < 4, logi, jnp.where(lane < 8, logf, 0.0))

    return pl.pallas_call(
        body, name="prep_fwd", grid=(R // TM,),
        in_specs=[_rb(TM, 1024, QK0 // 1024),
                  pl.BlockSpec((8, 1024), lambda i: (jnp.maximum(i * t8 - 1, 0), QK0 // 1024)),
                  _rb(TM, 128, SM0 // 128), _const((4, 1024)), _const((1, 1024)), _const((1, 128))],
        out_specs=[_rb(TM, 1024, 0), _rb(TM, 1024, 0), _rb(TM, 128, 0)],
        out_shape=[jax.ShapeDtypeStruct((R, 1024), F32), jax.ShapeDtypeStruct((R, 1024), F32),
                   jax.ShapeDtypeStruct((R, 128), F32)],
        compiler_params=_cp(("parallel",)),
    )(proj, proj, proj, conv_w, conv_b, gb_row)


def merge_fwd(bm, bg, proj):
    R = bm.shape[0]

    def body(bm_ref, bg_ref, gm_ref, gg_ref, o_ref):
        o_ref[...] = (_sigmoid(gm_ref[...]) * bm_ref[...] + _sigmoid(gg_ref[...]) * bg_ref[...]).astype(o_ref.dtype)

    return pl.pallas_call(
        body, name="merge_fwd", grid=(R // TM,),
        in_specs=[_rb(TM, D, 0), _rb(TM, D, 0), _rb(TM, D, GM0 // D), _rb(TM, D, GG0 // D)],
        out_specs=_rb(TM, D, 0), out_shape=jax.ShapeDtypeStruct((R, D), MXU),
        compiler_params=_cp(("parallel",)),
    )(bm, bg, proj, proj)


def merge_bwd(dmerged, bm, bg, proj):
    R = bm.shape[0]

    def body(dm_ref, bm_ref, bg_ref, gm_ref, gg_ref, dbm_ref, dbg_ref, dp_ref):
        dm = dm_ref[...]
        sm, sg = _sigmoid(gm_ref[...]), _sigmoid(gg_ref[...])
        dbm_ref[...] = (dm * sm).astype(dbm_ref.dtype)
        dbg_ref[...] = (dm * sg).astype(dbg_ref.dtype)
        dp_ref[:, 0:D] = (dm * bm_ref[...] * sm * (1.0 - sm)).astype(dp_ref.dtype)
        dp_ref[:, D:2 * D] = (dm * bg_ref[...] * sg * (1.0 - sg)).astype(dp_ref.dtype)

    return pl.pallas_call(
        body, name="merge_bwd", grid=(R // TM,),
        in_specs=[_rb(TM, D, 0), _rb(TM, D, 0), _rb(TM, D, 0), _rb(TM, D, GM0 // D), _rb(TM, D, GG0 // D)],
        out_specs=[_rb(TM, D, 0), _rb(TM, D, 0), _rb(TM, 2 * D, GM0 // (2 * D))],
        out_shape=[jax.ShapeDtypeStruct((R, D), MXU), jax.ShapeDtypeStruct((R, D), MXU),
                   jax.ShapeDtypeStruct((R, NP), MXU)],
        compiler_params=_cp(("parallel",)),
    )(dmerged, bm, bg, proj, proj)


def swiglu_fwd(au):
    R = au.shape[0]
    tw = DFF // 2

    def body(a_ref, u_ref, o_ref):
        a = a_ref[...].astype(F32)
        o_ref[...] = (a * _sigmoid(a) * u_ref[...].astype(F32)).astype(o_ref.dtype)

    return pl.pallas_call(
        body, name="swiglu_fwd", grid=(R // TM, 2),
        in_specs=[pl.BlockSpec((TM, tw), lambda i, j: (i, j)), pl.BlockSpec((TM, tw), lambda i, j: (i, j + 2))],
        out_specs=pl.BlockSpec((TM, tw), lambda i, j: (i, j)),
        out_shape=jax.ShapeDtypeStruct((R, DFF), MXU), compiler_params=_cp(("parallel", "parallel")),
    )(au, au)


def swiglu_bwd(au, dff):
    R = au.shape[0]
    tw = DFF // 2

    def body(a_ref, u_ref, d_ref, dau_ref):
        a, d = a_ref[...].astype(F32), d_ref[...]
        s = _sigmoid(a)
        dau_ref[:, 0:DFF] = (d * u_ref[...].astype(F32) * s * (1.0 + a * (1.0 - s))).astype(dau_ref.dtype)
        dau_ref[:, DFF:2 * DFF] = (d * a * s).astype(dau_ref.dtype)

    tr = 256
    return pl.pallas_call(
        body, name="swiglu_bwd", grid=(R // tr,),
        in_specs=[_rb(tr, DFF, 0), _rb(tr, DFF, 1), _rb(tr, DFF, 0)],
        out_specs=_rb(tr, 2 * DFF, 0),
        out_shape=jax.ShapeDtypeStruct((R, 2 * DFF), MXU),
        compiler_params=_cp(("parallel",)),
    )(au, au, dff)


def final_loss(h2, gf, target):
    R = h2.shape[0]
    assert PADR == TM

    def body(h_ref, g_ref, t_ref, dh_ref, loss_ref, dg_ref):
        i = pl.program_id(0)
        hv = h_ref[...]
        r = lax.rsqrt(jnp.mean(hv * hv, axis=-1, keepdims=True) + EPS)
        g = g_ref[...]
        live = (i >= 1).astype(F32)
        e = (hv * r * g - t_ref[...]) * live
        dy = e * (1.0 / D)
        dyg = dy * g
        dh_ref[...] = r * dyg - hv * (r * r * r * jnp.mean(dyg * hv, axis=-1, keepdims=True))
        lpart = jnp.zeros((1, 128), F32) + 0.5 * jnp.sum(jnp.sum(e * e, axis=1, keepdims=True), axis=0, keepdims=True) * (1.0 / D)
        gpart = jnp.sum(dy * hv * r, axis=0, keepdims=True)

        @pl.when(i == 0)
        def _():
            loss_ref[...] = lpart
            dg_ref[...] = gpart

        @pl.when(i > 0)
        def _():
            loss_ref[...] += lpart
            dg_ref[...] += gpart

    return pl.pallas_call(
        body, name="final_loss", grid=(R // TM,),
        in_specs=[_rb(TM, D, 0), _const((1, D)), pl.BlockSpec((TM, D), lambda i: (jnp.maximum(i - 1, 0), 0))],
        out_specs=[_rb(TM, D, 0), _const((1, 128)), _const((1, D))],
        out_shape=[jax.ShapeDtypeStruct((R, D), F32), jax.ShapeDtypeStruct((1, 128), F32),
                   jax.ShapeDtypeStruct((1, D), F32)],
        compiler_params=_cp(("arbitrary",)),
    )(h2, gf, target)


def conv_bwd(dc, proj, conv_w, dproj):
    R = dc.shape[0]
    t8 = TM // 8
    nt = R // TM

    def body(dc_ref, nxt_ref, x_ref, prv_ref, w_ref, dp_in, dp_ref, dw_ref):
        del dp_in
        i = pl.program_id(0)
        dcv = dc_ref[...]
        nxt = nxt_ref[...] * (i < nt - 1).astype(F32)
        x = x_ref[...]
        prv = prv_ref[...]
        w = w_ref[...]
        dx = dcv * w[3:4, :]
        rows = [None] * 4
        rows[3] = jnp.sum(dcv * x, axis=0, keepdims=True)
        for s in (1, 2, 3):
            dx = dx + _shift_up(dcv, nxt, s) * w[3 - s:4 - s, :]
            rows[3 - s] = jnp.sum(dcv * _shift_down(x, prv, s), axis=0, keepdims=True)
        dp_ref[...] = dx.astype(dp_ref.dtype)
        part = jnp.concatenate(rows + [jnp.sum(dcv, axis=0, keepdims=True), jnp.zeros((3, 1024), F32)], axis=0)

        @pl.when(i == 0)
        def _():
            dw_ref[...] = part

        @pl.when(i > 0)
        def _():
            dw_ref[...] += part

    return pl.pallas_call(
        body, name="conv_bwd", grid=(nt,),
        in_specs=[_rb(TM, 1024, 0),
                  pl.BlockSpec((8, 1024), lambda i: (jnp.minimum((i + 1) * t8, nt * t8 - 1), 0)),
                  _rb(TM, 1024, QK0 // 1024),
                  pl.BlockSpec((8, 1024), lambda i: (jnp.maximum(i * t8 - 1, 0), QK0 // 1024)),
                  _const((4, 1024)), pl.BlockSpec(memory_space=pl.ANY)],
        out_specs=[_rb(TM, 1024, QK0 // 1024), _const((8, 1024))],
        out_shape=[jax.ShapeDtypeStruct((R, NP), MXU), jax.ShapeDtypeStruct((8, 1024), F32)],
        input_output_aliases={5: 0},
        compiler_params=_cp(("arbitrary",)),
    )(dc, dc, proj, proj, conv_w, dproj)


def small_bwd(dgl, dga, proj, gb_row, dproj):
    R = dgl.shape[0]

    def body(dgl_ref, dga_ref, sm_ref, gb_ref, dp_in, dp_ref, dgb_ref):
        del dp_in
        i = pl.program_id(0)
        z = sm_ref[...] + gb_ref[...]
        lane = lax.broadcasted_iota(jnp.int32, z.shape, 1)
        row = lax.broadcasted_iota(jnp.int32, z.shape, 0) + i * TM
        valid = row >= NPADROWS
        dgl_v = dgl_ref[...]
        dgate = jnp.where(valid, jnp.where(lane < 4, dgl_v, dgl_v * _sigmoid(-z)), 0.0)
        ds = jnp.where(lane < 8, dgate, dga_ref[...])
        dp_ref[...] = ds.astype(dp_ref.dtype)
        part = jnp.sum(jnp.where(lane < 8, dgate, 0.0), axis=0, keepdims=True)

        @pl.when(i == 0)
        def _():
            dgb_ref[...] = part

        @pl.when(i > 0)
        def _():
            dgb_ref[...] += part

    return pl.pallas_call(
        body, name="small_bwd", grid=(R // TM,),
        in_specs=[_rb(TM, 128, 0), _rb(TM, 128, 0), _rb(TM, 128, SM0 // 128), _const((1, 128)),
                  pl.BlockSpec(memory_space=pl.ANY)],
        out_specs=[_rb(TM, 128, SM0 // 128), _const((1, 128))],
        out_shape=[jax.ShapeDtypeStruct((R, NP), MXU), jax.ShapeDtypeStruct((1, 128), F32)],
        input_output_aliases={4: 0},
        compiler_params=_cp(("arbitrary",)),
    )(dgl, dga, proj, gb_row, dproj)


def _masks():
    r = lax.broadcasted_iota(jnp.int32, (L, L), 0)
    c = lax.broadcasted_iota(jnp.int32, (L, L), 1)
    return r >= c, r == c, r


def _to_row(col, eye):
    return jnp.sum(jnp.where(eye, col, 0.0), axis=0, keepdims=True)


def _to_col(row, eye):
    return jnp.sum(jnp.where(eye, row, 0.0), axis=1, keepdims=True)


def _mlstm_chunk(q, k, logi_c, logf_c, m, n):
    tril, eye, _ = _masks()
    logi_r, logf_r = _to_row(logi_c, eye), _to_row(logf_c, eye)
    b_c = jnp.sum(jnp.where(tril, logf_r, 0.0), axis=1, keepdims=True)
    b_r = _to_row(b_c, eye)
    g = jnp.sum(logf_c, axis=0, keepdims=True)
    dmat = jnp.where(tril, b_c - b_r + logi_r, NEG)
    mrow = jnp.maximum(b_c + m, jnp.max(dmat, axis=1, keepdims=True))
    dm = jnp.exp(dmat - mrow)
    s = _dot(q, k, 1, 1)
    w = dm * s
    a_in = jnp.exp(b_c + m - mrow)
    qn = jnp.sum(q * n, axis=1, keepdims=True)
    den = a_in * qn + jnp.sum(w, axis=1, keepdims=True)
    floor = jnp.exp(-mrow)
    nrm = jnp.maximum(jnp.abs(den), floor)
    wlog_c = g - b_c + logi_c
    m_new = jnp.maximum(g + m, jnp.max(wlog_c, axis=0, keepdims=True))
    a_st = jnp.exp(g + m - m_new)
    w_c = jnp.exp(wlog_c - m_new)
    return dict(b_c=b_c, g=g, dm=dm, s=s, w=w, a_in=a_in, qn=qn, den=den, floor=floor, nrm=nrm,
                m_new=m_new, a_st=a_st, w_c=w_c, tril=tril, eye=eye)


def _mlstm_head_fwd(h, glv, qk_ref, v_ref, mo_ref, hg_ref, hm_ref, ym_ref, cs_ref, nm_ref, c_s, nm_s):
    q = qk_ref[:, h * DQK:(h + 1) * DQK] * QSCALE
    k = qk_ref[:, 512 + h * DQK:512 + (h + 1) * DQK]
    v = v_ref[:, h * DV:(h + 1) * DV]
    C = c_s[h]
    n = nm_s[h, 0:1, :]
    m = nm_s[h, 1:2, 0:1]
    f = _mlstm_chunk(q, k, glv[:, h:h + 1], glv[:, 4 + h:5 + h], m, n)
    num = f["a_in"] * _dot(q, C, 1, 1) + _dot(f["w"], v, 1, 0)
    hh = num / f["nrm"]
    cs_ref[0, h] = C
    nm_ref[0, h] = nm_s[h]
    c_s[h] = f["a_st"] * C + _dot(f["w_c"] * v, k, 0, 0)
    n_new = f["a_st"] * n + jnp.sum(f["w_c"] * k, axis=0, keepdims=True)
    rowi = lax.broadcasted_iota(jnp.int32, (8, DQK), 0)
    nm_s[h] = jnp.where(rowi == 0, n_new, jnp.where(rowi == 1, f["m_new"], 0.0))
    rm = lax.rsqrt(jnp.mean(hh * hh, axis=-1, keepdims=True) + EPS)
    sl = slice(h * DV, (h + 1) * DV)
    hm_ref[:, sl] = hh
    ym_ref[:, sl] = (hh * rm * hg_ref[:, sl] * _sigmoid(mo_ref[:, sl])).astype(ym_ref.dtype)


def _mlstm_bwd_parts(dym_ref, hm_ref, qk_ref, cp_ref, v_ref, gl_ref, mo_ref, hg_ref, cs_ref, nm_ref,
                     dp_ref, dc_ref, dgl_ref, dhg_ref, dc_s, dn_s):
        def init():
            dc_s[...] = jnp.zeros_like(dc_s)
            dn_s[...] = jnp.zeros_like(dn_s)
            dhg_ref[...] = jnp.zeros_like(dhg_ref)

        def zero():
            dp_ref[...] = jnp.zeros_like(dp_ref)
            dc_ref[...] = jnp.zeros_like(dc_ref)
            dgl_ref[...] = jnp.zeros_like(dgl_ref)

        def compute():
            glv = gl_ref[...]
            lane = lax.broadcasted_iota(jnp.int32, (L, 128), 1)
            dgl = jnp.zeros((L, 128), F32)
            for h in range(NH):
                sl = slice(h * DV, (h + 1) * DV)
                sq = slice(h * DQK, (h + 1) * DQK)
                sk = slice(512 + h * DQK, 512 + (h + 1) * DQK)
                hh = hm_ref[:, sl]
                gain = hg_ref[:, sl]
                rm = lax.rsqrt(jnp.mean(hh * hh, axis=-1, keepdims=True) + EPS)
                sg = _sigmoid(mo_ref[:, sl])
                dyv = dym_ref[:, sl]
                dno = dyv * sg
                dp_ref[:, 1024 + h * DV:1024 + (h + 1) * DV] = (dyv * hh * rm * gain * sg * (1.0 - sg)).astype(dp_ref.dtype)
                dhg_ref[:, sl] += jnp.sum(dno * hh * rm, axis=0, keepdims=True)
                dnog = dno * gain
                dh = rm * dnog - hh * (rm * rm * rm * jnp.mean(dnog * hh, axis=-1, keepdims=True))
                q = qk_ref[:, sq] * QSCALE
                k = qk_ref[:, sk]
                v = v_ref[:, sl]
                C = cs_ref[0, h]
                n = nm_ref[0, h, 0:1, :]
                m = nm_ref[0, h, 1:2, 0:1]
                f = _mlstm_chunk(q, k, glv[:, h:h + 1], glv[:, 4 + h:5 + h], m, n)
                eye = f["eye"]
                a_in, nrm, den, w = f["a_in"], f["nrm"], f["den"], f["w"]
                dnum = dh / nrm
                dnrm = -jnp.sum(dh * hh, axis=1, keepdims=True) / nrm
                dden = jnp.where(jnp.abs(den) >= f["floor"], dnrm * jnp.sign(den), 0.0)
                dw = _dot(dnum, v, 1, 1) + dden
                dv = _dot(w, dnum, 0, 0)
                ds = dw * f["dm"]
                e = dw * w
                qc = _dot(q, C, 1, 1)
                dq = _dot(ds, k, 1, 0) + a_in * _dot(dnum, C, 1, 0) + (a_in * dden) * n
                dk = _dot(ds, q, 0, 0)
                dC_in = _dot(a_in * dnum, q, 0, 0)
                dn_in = jnp.sum((a_in * dden) * q, axis=0, keepdims=True)
                da_in = jnp.sum(dnum * qc, axis=1, keepdims=True) + dden * f["qn"]
                col_e = _to_col(jnp.sum(e, axis=0, keepdims=True), eye)
                db = jnp.sum(e, axis=1, keepdims=True) + da_in * a_in - col_e
                dlogi = col_e
                dCp = dc_s[h]
                dnp = dn_s[h, 0:1, :]
                a_st, w_c = f["a_st"], f["w_c"]
                da_st = (jnp.sum(jnp.sum(dCp * C, axis=1, keepdims=True), axis=0, keepdims=True)
                         + jnp.sum(dnp * n, axis=1, keepdims=True))
                vdc = _dot(v, dCp, 1, 0)
                dw_c = jnp.sum((vdc + dnp) * k, axis=1, keepdims=True)
                dv = dv + w_c * _dot(k, dCp, 1, 1)
                dk = dk + w_c * (vdc + dnp)
                fw = dw_c * w_c
                dg = jnp.sum(fw, axis=0, keepdims=True) + da_st * a_st
                db = db - fw
                dlogi = dlogi + fw
                rowc = lax.broadcasted_iota(jnp.int32, (L, 1), 0)
                db = db + jnp.where(rowc == L - 1, dg, 0.0)
                triu = lax.broadcasted_iota(jnp.int32, (L, L), 1) >= lax.broadcasted_iota(jnp.int32, (L, L), 0)
                dlogf = jnp.sum(jnp.where(triu, _to_row(db, eye), 0.0), axis=1, keepdims=True)
                dc_s[h] = a_st * dCp + dC_in
                dn_new = a_st * dnp + dn_in
                dn_s[h] = jnp.zeros((8, DQK), F32) + dn_new
                cq, ck = cp_ref[:, sq], cp_ref[:, sk]
                s_q, s_k = _sigmoid(cq), _sigmoid(ck)
                dc_ref[:, sq] = dq * QSCALE * s_q * (1.0 + cq * (1.0 - s_q))
                dc_ref[:, sk] = dk * s_k * (1.0 + ck * (1.0 - s_k))
                dp_ref[:, sl] = dv.astype(dp_ref.dtype)
                dgl = jnp.where(lane == h, dlogi, jnp.where(lane == 4 + h, dlogf, dgl))
            dgl_ref[...] = dgl

        return init, zero, compute


def _gla_logs(sm, a2p, b2, valid):
    za = _dot(sm, a2p, 1, 0) + b2
    return za, jnp.where(valid, _log_sigmoid(za) * (1.0 / TAU), 0.0)


def _valid_rows(c, width):
    row = lax.broadcasted_iota(jnp.int32, (L, width), 0) + c * L
    return row >= NPADROWS


def _gla_chunk(q, k, la):
    tril, _, _ = _masks()
    bc = _dot_exact(tril.astype(F32), la)
    btot = jnp.sum(la, axis=0, keepdims=True)
    ebc = jnp.exp(bc)
    qd = q * ebc
    ki = k * jnp.exp(-bc)
    ke = k * jnp.exp(btot - bc)
    att = jnp.where(tril, _dot(qd, ki, 1, 1), 0.0)
    return dict(tril=tril, bc=bc, btot=btot, ebc=ebc, qd=qd, ki=ki, ke=ke, att=att)


def _col128(row):
    r = lax.broadcasted_iota(jnp.int32, (DQK, DQK), 0)
    c = lax.broadcasted_iota(jnp.int32, (DQK, DQK), 1)
    return jnp.sum(jnp.where(r == c, row, 0.0), axis=1, keepdims=True)


def _gla_fwd_parts(c, q_ref, k_ref, v_ref, gr_ref, sm_ref, a2_ref, b2_ref, hg_ref, hgl_ref, yg_ref, ss_ref, s_s):
        def init():
            s_s[...] = jnp.zeros_like(s_s)

        def zero():
            hgl_ref[...] = jnp.zeros_like(hgl_ref)
            yg_ref[...] = jnp.zeros_like(yg_ref)
            ss_ref[...] = jnp.zeros_like(ss_ref)

        def compute():
            _, loga = _gla_logs(sm_ref[...], a2_ref[...], b2_ref[...], _valid_rows(c, 512))
            for h in range(NH):
                sq = slice(h * DQK, (h + 1) * DQK)
                sl = slice(h * DV, (h + 1) * DV)
                q = q_ref[:, sq] * QSCALE
                k = k_ref[:, sq]
                v = v_ref[:, sl]
                S = s_s[h]
                f = _gla_chunk(q, k, loga[:, sq])
                o = _dot(f["att"], v, 1, 0) + _dot(f["qd"], S, 1, 0)
                ss_ref[0, h] = S
                s_s[h] = _col128(jnp.exp(f["btot"])) * S + _dot(f["ke"], v, 0, 0)
                rg = lax.rsqrt(jnp.mean(o * o, axis=-1, keepdims=True) + EPS)
                gr = gr_ref[:, sl]
                hgl_ref[:, sl] = o
                yg_ref[:, sl] = (o * rg * hg_ref[:, sl] * gr * _sigmoid(gr)).astype(yg_ref.dtype)

        return init, zero, compute


def _gla_bwd_parts(c, dy_ref, ho_ref, q_ref, k_ref, v_ref, gr_ref, sm_ref, a2_ref, b2_ref, hg_ref, ss_ref,
                   dp_ref, dga_ref, da2_ref, db2_ref, dhg_ref, ds_s):
        def init():
            ds_s[...] = jnp.zeros_like(ds_s)
            da2_ref[...] = jnp.zeros_like(da2_ref)
            db2_ref[...] = jnp.zeros_like(db2_ref)
            dhg_ref[...] = jnp.zeros_like(dhg_ref)

        def zero():
            dp_ref[...] = jnp.zeros_like(dp_ref)
            dga_ref[...] = jnp.zeros_like(dga_ref)

        def compute():
            valid = _valid_rows(c, 512)
            sm = sm_ref[...]
            za, loga = _gla_logs(sm, a2_ref[...], b2_ref[...], valid)
            dloga = []
            for h in range(NH):
                sq = slice(h * DQK, (h + 1) * DQK)
                sl = slice(h * DV, (h + 1) * DV)
                o = ho_ref[:, sl]
                gain = hg_ref[:, sl]
                rg = lax.rsqrt(jnp.mean(o * o, axis=-1, keepdims=True) + EPS)
                gr = gr_ref[:, sl]
                sg = _sigmoid(gr)
                dyv = dy_ref[:, sl]
                dno = dyv * gr * sg
                dp_ref[:, 2048 + h * DV:2048 + (h + 1) * DV] = (
                    dyv * o * rg * gain * sg * (1.0 + gr * (1.0 - sg))).astype(dp_ref.dtype)
                dhg_ref[:, sl] += jnp.sum(dno * o * rg, axis=0, keepdims=True)
                dnog = dno * gain
                do = rg * dnog - o * (rg * rg * rg * jnp.mean(dnog * o, axis=-1, keepdims=True))
                q = q_ref[:, sq] * QSCALE
                k = k_ref[:, sq]
                v = v_ref[:, sl]
                S = ss_ref[0, h]
                f = _gla_chunk(q, k, loga[:, sq])
                tril, qd, ki, ke = f["tril"], f["qd"], f["ki"], f["ke"]
                dSp = ds_s[h]
                datt = jnp.where(tril, _dot(do, v, 1, 1), 0.0)
                dqd = _dot(do, S, 1, 1) + _dot(datt, ki, 1, 0)
                dki = _dot(datt, qd, 0, 0)
                dv = _dot(f["att"], do, 0, 0) + _dot(ke, dSp, 1, 0)
                dke = _dot(v, dSp, 1, 1)
                ebt = jnp.exp(f["btot"])
                dbtot = jnp.sum(dke * ke, axis=0, keepdims=True) + ebt * _to_row128(jnp.sum(dSp * S, axis=1, keepdims=True))
                ds_s[h] = _dot(qd, do, 0, 0) + _col128(ebt) * dSp
                dq = dqd * f["ebc"]
                dk = dki * jnp.exp(-f["bc"]) + dke * jnp.exp(f["btot"] - f["bc"])
                dbc = dqd * qd - dki * ki - dke * ke
                rowc = lax.broadcasted_iota(jnp.int32, (L, DQK), 0)
                dbc = dbc + jnp.where(rowc == L - 1, dbtot, 0.0)
                triu = lax.broadcasted_iota(jnp.int32, (L, L), 1) >= lax.broadcasted_iota(jnp.int32, (L, L), 0)
                dloga.append(_dot_exact(triu.astype(F32), dbc))
                dp_ref[:, sq] = (dq * QSCALE).astype(dp_ref.dtype)
                dp_ref[:, 512 + h * DQK:512 + (h + 1) * DQK] = dk.astype(dp_ref.dtype)
                dp_ref[:, 1024 + h * DV:1024 + (h + 1) * DV] = dv.astype(dp_ref.dtype)
            dza = jnp.where(valid, jnp.concatenate(dloga, axis=1) * (1.0 / TAU) * _sigmoid(-za), 0.0)
            dga_ref[...] = _dot(dza, a2_ref[...], 1, 1)
            da2_ref[...] += _dot(sm, dza, 0, 0)
            db2_ref[...] += jnp.sum(dza, axis=0, keepdims=True)

        return init, zero, compute


def mix_fwd(qk, proj, gl, m_head_g, a2p, b2, g_head_g):
    R = qk.shape[0]
    NC = R // L

    def body(qk_ref, mv_ref, gl_ref, mo_ref, mhg_ref, gq_ref, gk_ref, gv_ref, gr_ref, sm_ref, a2_ref, b2_ref, ghg_ref,
             hm_ref, ym_ref, cs_ref, nm_ref, hgl_ref, yg_ref, ss_ref, c_s, nm_s, s_s):
        c = pl.program_id(0)
        g_init, g_zero, g_compute = _gla_fwd_parts(c, gq_ref, gk_ref, gv_ref, gr_ref, sm_ref, a2_ref, b2_ref, ghg_ref,
                                                   hgl_ref, yg_ref, ss_ref, s_s)

        @pl.when(c <= CH0)
        def _():
            c_s[...] = jnp.zeros_like(c_s)
            nm_s[...] = jnp.zeros_like(nm_s)
            g_init()

        @pl.when(c < CH0)
        def _():
            hm_ref[...] = jnp.zeros_like(hm_ref)
            ym_ref[...] = jnp.zeros_like(ym_ref)
            cs_ref[...] = jnp.zeros_like(cs_ref)
            nm_ref[...] = jnp.zeros_like(nm_ref)
            g_zero()

        @pl.when(c >= CH0)
        def _():
            glv = gl_ref[...]
            for h in range(NH):
                _mlstm_head_fwd(h, glv, qk_ref, mv_ref, mo_ref, mhg_ref, hm_ref, ym_ref, cs_ref, nm_ref, c_s, nm_s)
            g_compute()

    st_m = pl.BlockSpec((1, NH, DV, DQK), lambda c: (c, 0, 0, 0))
    st_n = pl.BlockSpec((1, NH, 8, DQK), lambda c: (c, 0, 0, 0))
    st_g = pl.BlockSpec((1, NH, DQK, DV), lambda c: (c, 0, 0, 0))
    return pl.pallas_call(
        body, name="mix_fwd", grid=(NC,),
        in_specs=[_rb(L, 1024, 0), _rb(L, 1024, MV0 // 1024), _rb(L, 128, 0), _rb(L, 1024, MO0 // 1024),
                  _const((1, 1024)),
                  _rb(L, 512, GQ0 // 512), _rb(L, 512, GK0 // 512), _rb(L, 1024, GV0 // 1024),
                  _rb(L, 1024, GR0 // 1024), _rb(L, 128, SM0 // 128), _const((128, 512)), _const((1, 512)),
                  _const((1, 1024))],
        out_specs=[_rb(L, 1024, 0), _rb(L, 1024, 0), st_m, st_n, _rb(L, 1024, 0), _rb(L, 1024, 0), st_g],
        out_shape=[jax.ShapeDtypeStruct((R, 1024), F32), jax.ShapeDtypeStruct((R, 1024), MXU),
                   jax.ShapeDtypeStruct((NC, NH, DV, DQK), F32), jax.ShapeDtypeStruct((NC, NH, 8, DQK), F32),
                   jax.ShapeDtypeStruct((R, 1024), F32), jax.ShapeDtypeStruct((R, 1024), MXU),
                   jax.ShapeDtypeStruct((NC, NH, DQK, DV), F32)],
        scratch_shapes=[pltpu.VMEM((NH, DV, DQK), F32), pltpu.VMEM((NH, 8, DQK), F32),
                        pltpu.VMEM((NH, DQK, DV), F32)],
        compiler_params=_cp(("arbitrary",)),
    )(qk, proj, gl, proj, m_head_g, proj, proj, proj, proj, proj, a2p, b2, g_head_g)


def mix_bwd(dym, hm, qk, cpre, proj, gl, m_head_g, cs, nm, dyg, hgl, a2p, b2, g_head_g, ss, dproj):
    R = qk.shape[0]
    NC = R // L
    rev = lambda c: NC - 1 - c
    GW = GR0 + 1024 - GQ0

    def body(dym_ref, hm_ref, qk_ref, cp_ref, mv_ref, gl_ref, mo_ref, mhg_ref, cs_ref, nm_ref,
             dyg_ref, ho_ref, gq_ref, gk_ref, gv_ref, gr_ref, sm_ref, a2_ref, b2_ref, ghg_ref, ss_ref, dp_in,
             dp_ref, dc_ref, dgl_ref, dmhg_ref, dga_ref, da2_ref, db2_ref, dghg_ref, dc_s, dn_s, ds_s):
        del dp_in
        step = pl.program_id(0)
        c = NC - 1 - step
        m_init, m_zero, m_compute = _mlstm_bwd_parts(
            dym_ref, hm_ref, qk_ref, cp_ref, mv_ref, gl_ref, mo_ref, mhg_ref, cs_ref, nm_ref,
            dp_ref.at[:, 0:GQ0], dc_ref, dgl_ref, dmhg_ref, dc_s, dn_s)
        g_init, g_zero, g_compute = _gla_bwd_parts(
            c, dyg_ref, ho_ref, gq_ref, gk_ref, gv_ref, gr_ref, sm_ref, a2_ref, b2_ref, ghg_ref, ss_ref,
            dp_ref.at[:, GQ0:GQ0 + GW], dga_ref, da2_ref, db2_ref, dghg_ref, ds_s)

        @pl.when(step == 0)
        def _():
            m_init()
            g_init()

        @pl.when(c < CH0)
        def _():
            m_zero()
            g_zero()

        @pl.when(c >= CH0)
        def _():
            m_compute()
            g_compute()

    def rows(w, cb):
        return pl.BlockSpec((L, w), lambda c: (rev(c), cb))

    return pl.pallas_call(
        body, name="mix_bwd", grid=(NC,),
        in_specs=[rows(1024, 0), rows(1024, 0), rows(1024, 0), rows(1024, 0), rows(1024, MV0 // 1024), rows(128, 0),
                  rows(1024, MO0 // 1024), _const((1, 1024)),
                  pl.BlockSpec((1, NH, DV, DQK), lambda c: (rev(c), 0, 0, 0)),
                  pl.BlockSpec((1, NH, 8, DQK), lambda c: (rev(c), 0, 0, 0)),
                  rows(1024, 0), rows(1024, 0), rows(512, GQ0 // 512), rows(512, GK0 // 512),
                  rows(1024, GV0 // 1024), rows(1024, GR0 // 1024), rows(128, SM0 // 128),
                  _const((128, 512)), _const((1, 512)), _const((1, 1024)),
                  pl.BlockSpec((1, NH, DQK, DV), lambda c: (rev(c), 0, 0, 0)),
                  pl.BlockSpec(memory_space=pl.ANY)],
        out_specs=[rows(GQ0 + GW, 0), rows(1024, 0), rows(128, 0), _const((1, 1024)),
                   rows(128, 0), _const((128, 512)), _const((1, 512)), _const((1, 1024))],
        out_shape=[jax.ShapeDtypeStruct((R, NP), MXU), jax.ShapeDtypeStruct((R, 1024), F32),
                   jax.ShapeDtypeStruct((R, 128), F32), jax.ShapeDtypeStruct((1, 1024), F32),
                   jax.ShapeDtypeStruct((R, 128), F32), jax.ShapeDtypeStruct((128, 512), F32),
                   jax.ShapeDtypeStruct((1, 512), F32), jax.ShapeDtypeStruct((1, 1024), F32)],
        scratch_shapes=[pltpu.VMEM((NH, DV, DQK), F32), pltpu.VMEM((NH, 8, DQK), F32),
                        pltpu.VMEM((NH, DQK, DV), F32)],
        input_output_aliases={21: 0},
        compiler_params=_cp(("arbitrary",)),
    )(dym, hm, qk, cpre, proj, gl, proj, m_head_g, cs, nm, dyg, hgl, proj, proj, proj, proj, proj, a2p, b2,
      g_head_g, ss, dproj)


def _to_row128(col):
    r = lax.broadcasted_iota(jnp.int32, (DQK, DQK), 0)
    c = lax.broadcasted_iota(jnp.int32, (DQK, DQK), 1)
    return jnp.sum(jnp.where(r == c, col, 0.0), axis=0, keepdims=True)


def local_step(x, target, meta, norm1_g, wp, conv_w, conv_b, m_gate_b, g_a2, g_a2_b, m_head_g, g_head_g,
               norm2_g, final_g, late_weights, send_early, send_wp, first_order=None):
    seq = x.shape[0]
    assert seq % TM == 0
    h0 = jnp.concatenate([jnp.zeros((NPADROWS, D), F32), meta, x], axis=0)
    gb_row = jnp.zeros((1, 128), F32).at[0, 0:8].set(m_gate_b.reshape(8))
    a2p = jnp.zeros((128, 512), F32).at[8:8 + RANK].set(g_a2)
    mhg = m_head_g.reshape(1, 1024)
    ghg = g_head_g.reshape(1, 1024)

    xn = rms_fwd(h0, norm1_g, "rms1_fwd")
    proj = matmul(xn, wp, "nn", "proj_fwd", tm=1536, order=first_order)
    cpre, qk, gl = prep_fwd(proj, conv_w, conv_b, gb_row)
    hm, ym, cs, nm, hgl, yg, ss = mix_fwd(qk, proj, gl, mhg, a2p, g_a2_b, ghg)
    w_bm, w_bg, w_out, w_gu, w_down = late_weights(ym)
    bm = matmul(ym, w_bm, "nn", "branch_m_fwd")
    bg = matmul(yg, w_bg, "nn", "branch_g_fwd")
    merged = merge_fwd(bm, bg, proj)
    h1 = matmul(merged, w_out, "nn", "out_fwd", add=h0)
    hn = rms_fwd(h1, norm2_g, "rms2_fwd")
    au = matmul(hn, w_gu, "nn", "ff_in_fwd", out_dtype=MXU, tm=1536, tn=1408)
    ff = swiglu_fwd(au)
    h2 = matmul(ff, w_down, "nn", "ff_down_fwd", add=h1, tm=768, tk=2816)
    dh2, loss, d_final_g = final_loss(h2, final_g.reshape(1, D), target)

    d_w_down = matmul(ff, dh2, "tn", "ff_down_wgrad", tm=1408, tk=1536)
    dff = matmul(dh2, w_down, "nt", "ff_down_dgrad", tm=1536, tn=1408)
    dau = swiglu_bwd(au, dff)
    d_w_gu = matmul(hn, dau, "tn", "ff_in_wgrad", tm=1024, tn=1408, tk=1536)
    dhn = matmul(dau, w_gu, "nt", "ff_in_dgrad", tm=1536, tk=1408)
    dh1, d_norm2_g = rms_bwd(dhn, h1, norm2_g, dh2, "rms2_bwd")

    d_w_out = matmul(merged, dh1, "tn", "out_wgrad", tm=1024, tk=1536)
    dmerged = matmul(dh1, w_out, "nt", "out_dgrad")
    dbm, dbg, dproj = merge_bwd(dmerged, bm, bg, proj)
    d_w_bm = matmul(ym, dbm, "tn", "branch_m_wgrad", tm=1024, tk=1536)
    d_w_bg = matmul(yg, dbg, "tn", "branch_g_wgrad", tm=1024, tk=1536)
    token = send_early(dict(w_branch_m=d_w_bm, w_branch_g=d_w_bg, w_out=d_w_out, w_gu=d_w_gu, w_ff_down=d_w_down))
    dym = matmul(dbm, w_bm, "nt", "branch_m_dgrad", order=token)
    dyg = matmul(dbg, w_bg, "nt", "branch_g_dgrad")
    dproj, dc, dgl, d_mhg, dga, d_a2p, d_a2b, d_ghg = mix_bwd(
        dym, hm, qk, cpre, proj, gl, mhg, cs, nm, dyg, hgl, a2p, g_a2_b, ghg, ss, dproj)
    dproj, d_conv = conv_bwd(dc, proj, conv_w, dproj)
    dproj, d_gb = small_bwd(dgl, dga, proj, gb_row, dproj)
    d_wp = matmul(xn, dproj, "tn", "proj_wgrad", tm=1024, tn=1664, tk=1536)
    token = send_wp(d_wp)
    dxn = matmul(dproj, wp, "nt", "proj_dgrad", tm=1536, tk=1664, order=token)
    grad_x, d_meta, d_norm1_g = rms_bwd_input(dxn, h0, norm1_g, dh1, "rms1_bwd")

    grads = dict(
        meta_tokens=d_meta, norm1_g=d_norm1_g, conv_w=d_conv[0:4], conv_b=d_conv[4:5], m_gate_b=d_gb[0, 0:8].reshape(1, 2, 4),
        g_a2=d_a2p[8:8 + RANK], g_a2_b=d_a2b, m_head_g=d_mhg.reshape(NH, DV), g_head_g=d_ghg.reshape(NH, DV),
        norm2_g=d_norm2_g, final_g=d_final_g)
    return loss, grad_x, grads


_SEGS = [(0, 1024, QK0), (1024, 2048, MV0), (2048, 2056, SM0), (2056, 3080, MO0), (3080, 5128, GQ0),
         (5128, 5144, SM0 + 8), (5144, 6168, GR0), (6168, 8216, GM0)]
SHARD_W = NPROJ // NDEV


def regroup_cols(w8):
    parts = []
    for lo, hi, _ in sorted(_SEGS, key=lambda s: s[2]):
        while lo < hi:
            j = lo // SHARD_W
            end = min(hi, (j + 1) * SHARD_W)
            parts.append(w8[j, :, lo - j * SHARD_W:end - j * SHARD_W])
            lo = end
    parts.append(jnp.zeros((w8.shape[1], NP - NPROJ), w8.dtype))
    return jnp.concatenate(parts, axis=1)


def ungroup_cols(g):
    blocks = []
    for j in range(NDEV):
        lo, hi = j * SHARD_W, (j + 1) * SHARD_W
        parts = []
        for s_lo, s_hi, s_at in _SEGS:
            a, b = max(lo, s_lo), min(hi, s_hi)
            if a < b:
                parts.append(g[:, s_at + a - s_lo:s_at + b - s_lo])
        blocks.append(jnp.concatenate(parts, axis=1))
    return jnp.stack(blocks)


def col_blocks(g):
    r, c8 = g.shape
    return jnp.transpose(g.reshape(r, NDEV, c8 // NDEV), (1, 0, 2))


def from_col_blocks(g8):
    n, r, c = g8.shape
    return jnp.transpose(g8, (1, 0, 2)).reshape(r, n * c)


_MESHID = pl.DeviceIdType.MESH
_RELS = [(0, 0, 1), (1, 0, 0), (0, 1, 0), (1, 1, 0), (1, 0, 1), (0, 1, 1), (1, 1, 1)]


def _flip(v, bit):
    return 1 - v if bit else v


def all_gather(arrs, name):
    n = len(arrs)

    def body(*refs):
        ins, outs = refs[:n], refs[n:2 * n]
        send_sems, recv_sems, local_sems = refs[2 * n:]
        x, y, c = lax.axis_index("x"), lax.axis_index("y"), lax.axis_index("c")
        me, sibling = (x, y, c), (x, y, 1 - c)
        chips = [(1 - x, y), (x, 1 - y), (1 - x, 1 - y)]

        def slot(p):
            return 4 * p[0] + 2 * p[1] + p[2]

        def copy(a, k, block, to, src=None):
            dst = outs[a].at[slot(block)]
            return pltpu.make_async_remote_copy(
                src_ref=dst if src is None else src, dst_ref=dst,
                send_sem=send_sems.at[a, k], recv_sem=recv_sems.at[a, k],
                device_id=to, device_id_type=_MESHID)

        mine = [pltpu.make_async_copy(ins[a], outs[a].at[slot(me)], local_sems.at[a]) for a in range(n)]
        for cp in mine:
            cp.start()
        first = []
        for a in range(n):
            first.append(copy(a, 0, me, sibling, src=ins[a]))
            first += [copy(a, 1 + j, me, (*chip, c), src=ins[a]) for j, chip in enumerate(chips)]
        for cp in first:
            cp.start()
        passed = []
        for j, chip in enumerate(chips):
            for a in range(n):
                copy(a, 1 + j, (*chip, c), me).wait_recv()
                fwd = copy(a, 4 + j, (*chip, c), sibling)
                fwd.start()
                passed.append(fwd)
        for a in range(n):
            copy(a, 0, sibling, me).wait_recv()
            for j, chip in enumerate(chips):
                copy(a, 4 + j, (*chip, 1 - c), me).wait_recv()
        for cp in first + passed:
            cp.wait_send()
        for cp in mine:
            cp.wait()

    anyspec = pl.BlockSpec(memory_space=pl.ANY)
    return pl.pallas_call(
        body, name=name,
        in_specs=[anyspec] * n, out_specs=[anyspec] * n,
        out_shape=[jax.ShapeDtypeStruct((NDEV,) + a.shape, a.dtype) for a in arrs],
        scratch_shapes=[pltpu.SemaphoreType.DMA((n, 7)), pltpu.SemaphoreType.DMA((n, 7)),
                        pltpu.SemaphoreType.DMA((n,))],
    )(*arrs)


def exchange(blocks, rep, name):
    n = len(blocks)

    def body(*refs):
        b_refs, r_ref = refs[:n], refs[n]
        ob_refs, or_ref = refs[n + 1:2 * n + 1], refs[2 * n + 1]
        send_sems, recv_sems, local_sems = refs[2 * n + 2:]
        x, y, c = lax.axis_index("x"), lax.axis_index("y"), lax.axis_index("c")
        me = 4 * x + 2 * y + c

        def pairs(src_slot, dst_slot):
            return [(b_refs[a].at[src_slot], ob_refs[a].at[dst_slot]) for a in range(n)] + [(r_ref, or_ref.at[dst_slot])]

        loc = [pltpu.make_async_copy(s, d, local_sems.at[a]) for a, (s, d) in enumerate(pairs(me, me))]
        for cp in loc:
            cp.start()
        sends = []
        for k, (fx, fy, fc) in enumerate(_RELS):
            peer = (_flip(x, fx), _flip(y, fy), _flip(c, fc))
            pid = 4 * peer[0] + 2 * peer[1] + peer[2]
            for a, (s, d) in enumerate(pairs(pid, me)):
                sends.append(pltpu.make_async_remote_copy(
                    src_ref=s, dst_ref=d, send_sem=send_sems.at[a, k], recv_sem=recv_sems.at[a, k],
                    device_id=peer, device_id_type=_MESHID))
        for cp in sends:
            cp.start()
        for k, (fx, fy, fc) in enumerate(_RELS):
            peer = (_flip(x, fx), _flip(y, fy), _flip(c, fc))
            pid = 4 * peer[0] + 2 * peer[1] + peer[2]
            for a, (s, d) in enumerate(pairs(pid, pid)):
                pltpu.make_async_remote_copy(
                    src_ref=s, dst_ref=d, send_sem=send_sems.at[a, k], recv_sem=recv_sems.at[a, k],
                    device_id=peer, device_id_type=_MESHID).wait_recv()
        for cp in sends:
            cp.wait_send()
        for cp in loc:
            cp.wait()

    anyspec = pl.BlockSpec(memory_space=pl.ANY)
    return pl.pallas_call(
        body, name=name,
        in_specs=[anyspec] * (n + 1), out_specs=[anyspec] * (n + 1),
        out_shape=[jax.ShapeDtypeStruct(b.shape, b.dtype) for b in blocks]
        + [jax.ShapeDtypeStruct((NDEV,) + rep.shape, rep.dtype)],
        scratch_shapes=[pltpu.SemaphoreType.DMA((n + 1, 7)), pltpu.SemaphoreType.DMA((n + 1, 7)),
                        pltpu.SemaphoreType.DMA((n + 1,))],
    )(*blocks, rep)


_HBM = pl.BlockSpec(memory_space=pltpu.HBM)
_SEM = pl.BlockSpec(memory_space=pltpu.SEMAPHORE)
_EFFECT = pltpu.SideEffectType.DATAFLOW_SIDE_EFFECTING


def _peer_ids():
    x, y, c = lax.axis_index("x"), lax.axis_index("y"), lax.axis_index("c")
    peers = []
    for fx, fy, fc in _RELS:
        p = (_flip(x, fx), _flip(y, fy), _flip(c, fc))
        peers.append((p, 4 * p[0] + 2 * p[1] + p[2]))
    return 4 * x + 2 * y + c, peers


def _split_copy(src, land, a, k, peer, src_slot, dst_slot, send_sems, recv_sems):
    return pltpu.make_async_remote_copy(
        src_ref=src if src_slot is None else src.at[src_slot], dst_ref=land.at[dst_slot],
        send_sem=send_sems.at[7 * a + k], recv_sem=recv_sems.at[7 * a + k], device_id=peer, device_id_type=_MESHID)


def _own_copy(src, land, a, n, me, per_peer, send_sems):
    return pltpu.make_async_copy(src.at[me] if per_peer else src, land.at[me], send_sems.at[7 * n + a])


def send_start(srcs, per_peer, order, name):
    n = len(srcs)
    lands = [lax.empty((NDEV,) + (s.shape[1:] if per_peer else s.shape), s.dtype) for s in srcs]

    def body(*refs):
        src_refs, land_refs = refs[1:1 + n], refs[1 + n:1 + 2 * n]
        send_sems, recv_sems = refs[1 + 2 * n], refs[2 + 2 * n]
        token = refs[3 + 4 * n]
        me, peers = _peer_ids()
        for a in range(n):
            _own_copy(src_refs[a], land_refs[a], a, n, me, per_peer, send_sems).start()
        for a in range(n):
            for k, (peer, pid) in enumerate(peers):
                _split_copy(src_refs[a], land_refs[a], a, k, peer, pid if per_peer else None, me,
                            send_sems, recv_sems).start()
        token[...] = jnp.zeros_like(token)

    outs = pl.pallas_call(
        body, name=name,
        in_specs=[pl.BlockSpec(memory_space=pl.ANY)] + [_HBM] * (2 * n),
        out_shape=(pltpu.SemaphoreType.DMA((8 * n,)), pltpu.SemaphoreType.DMA((7 * n,)),
                   *[pltpu.HBM(s.shape, s.dtype) for s in srcs], *[pltpu.HBM(l.shape, l.dtype) for l in lands],
                   jax.ShapeDtypeStruct((8, 128), F32)),
        out_specs=(_SEM, _SEM, *[_HBM] * (2 * n), pl.BlockSpec(memory_space=pltpu.VMEM)),
        input_output_aliases={1 + i: 2 + i for i in range(2 * n)},
        compiler_params=pltpu.CompilerParams(has_side_effects=_EFFECT),
    )(order, *[pltpu.with_memory_space_constraint(s, pltpu.HBM) for s in srcs],
      *[pltpu.with_memory_space_constraint(l, pltpu.HBM) for l in lands])
    return (n, per_peer, outs[0], outs[1], outs[2:2 + n], outs[2 + n:2 + 2 * n]), outs[2 + 2 * n]


def send_wait(handle, after, name):
    n, per_peer, send_sems, recv_sems, src_thru, land_thru = handle

    def body(*refs):
        src_refs, land_refs = refs[:n], refs[n:2 * n]
        s_sems, r_sems = refs[2 * n], refs[2 * n + 1]
        me, peers = _peer_ids()
        for a in range(n):
            _own_copy(src_refs[a], land_refs[a], a, n, me, per_peer, s_sems).wait()
            for k, (peer, pid) in enumerate(peers):
                cp = _split_copy(src_refs[a], land_refs[a], a, k, peer, pid if per_peer else None, pid, s_sems, r_sems)
                cp.wait_send()
                cp.wait_recv()

    outs = pl.pallas_call(
        body, name=name,
        in_specs=[_HBM] * (2 * n) + [_SEM, _SEM, pl.BlockSpec(memory_space=pl.ANY)],
        out_shape=tuple(pltpu.HBM(t.shape, t.dtype) for t in (*src_thru, *land_thru)),
        out_specs=tuple([_HBM] * (2 * n)),
        input_output_aliases={i: i for i in range(2 * n)},
        compiler_params=pltpu.CompilerParams(has_side_effects=_EFFECT),
    )(*src_thru, *land_thru, send_sems, recv_sems, after)
    return list(outs[n:2 * n])


def adamw(parts, w, m, v, name, tr):
    npart, r, c = parts.shape
    c1 = 1.0 - ADAM_B1 ** ADAM_STEP
    c2 = 1.0 - ADAM_B2 ** ADAM_STEP

    def body(p_ref, w_ref, m_ref, v_ref, g_ref, d_ref, nm_ref, nv_ref):
        g = p_ref[0].astype(F32)
        for j in range(1, npart):
            g = g + p_ref[j].astype(F32)
        mn = ADAM_B1 * m_ref[...] + (1.0 - ADAM_B1) * g
        vn = ADAM_B2 * v_ref[...] + (1.0 - ADAM_B2) * (g * g)
        g_ref[...] = g
        nm_ref[...] = mn
        nv_ref[...] = vn
        d_ref[...] = -ADAM_LR * ((mn / c1) / (jnp.sqrt(vn / c2) + ADAM_EPS) + ADAM_WD * w_ref[...])

    spec = _rb(tr, c, 0)
    return pl.pallas_call(
        body, name=name, grid=(r // tr,),
        in_specs=[pl.BlockSpec((npart, tr, c), lambda i: (0, i, 0)), spec, spec, spec],
        out_specs=[spec] * 4, out_shape=[jax.ShapeDtypeStruct((r, c), F32)] * 4,
        compiler_params=_cp(("parallel",)),
    )(parts, w, m, v)


def sum_parts(parts, name, tc):
    npart, r, c = parts.shape

    def body(p_ref, o_ref):
        g = p_ref[0].astype(F32)
        for j in range(1, npart):
            g = g + p_ref[j].astype(F32)
        o_ref[...] = g

    return pl.pallas_call(
        body, name=name, grid=(c // tc,),
        in_specs=[pl.BlockSpec((npart, r, tc), lambda i: (0, 0, i))],
        out_specs=pl.BlockSpec((r, tc), lambda i: (0, i)),
        out_shape=jax.ShapeDtypeStruct((r, c), F32),
        compiler_params=_cp(("parallel",)),
    )(parts)


TINY = [("meta_tokens", (16, 1024)), ("conv_w", (4, 1024)), ("g_a2", (16, 512)), ("m_head_g", (4, 256)),
        ("g_head_g", (4, 256))]
REPL = [("norm1_g", (1, 1024)), ("conv_b", (1, 1024)), ("m_gate_b", (1, 2, 4)), ("g_a2_b", (1, 512)),
        ("norm2_g", (1, 1024)), ("final_g", (1024,))]
TINY_SIZE = 16 * 1024 + 4 * 1024 + 16 * 512 + 2 * 4 * 256
REPL_SIZE = 1024 + 1024 + 8 + 512 + 1024 + 1024
ROWS_GATHER = 8
ROWS_REP = 40
ROWS_OWN = 16


def pack_rows(vecs, rows):
    flat = jnp.concatenate([v.reshape(-1) for v in vecs])
    return jnp.pad(flat, (0, rows * 1024 - flat.shape[0])).reshape(rows, 1024)


def unpack_rows(packed, shapes):
    flat = packed.reshape(-1)
    out, off = [], 0
    for s in shapes:
        n = 1
        for d in s:
            n *= d
        out.append(flat[off:off + n].reshape(s))
        off += n
    return out


def kernel(x, meta_tokens, norm1_g, w_in, conv_w, conv_b, m_gate_b, g_a2, g_a2_b, m_head_g, g_head_g, w_branch_m, w_branch_g, w_out, norm2_g, w_ff_gate, w_ff_up, w_ff_down, final_g, loss_target, m_meta_tokens, m_norm1_g, m_w_in, m_conv_w, m_conv_b, m_m_gate_b, m_g_a2, m_g_a2_b, m_m_head_g, m_g_head_g, m_w_branch_m, m_w_branch_g, m_w_out, m_norm2_g, m_w_ff_gate, m_w_ff_up, m_w_ff_down, m_final_g, v_meta_tokens, v_norm1_g, v_w_in, v_conv_w, v_conv_b, v_m_gate_b, v_g_a2, v_g_a2_b, v_m_head_g, v_g_head_g, v_w_branch_m, v_w_branch_g, v_w_out, v_norm2_g, v_w_ff_gate, v_w_ff_up, v_w_ff_down, v_final_g):
    w_sh = dict(meta_tokens=meta_tokens, w_in=w_in[0], conv_w=conv_w[0], g_a2=g_a2[0], m_head_g=m_head_g[0],
                g_head_g=g_head_g[0], w_branch_m=w_branch_m[0], w_branch_g=w_branch_g[0], w_out=w_out[0],
                w_ff_gate=w_ff_gate[0], w_ff_up=w_ff_up[0], w_ff_down=w_ff_down[0])
    m_sh = dict(meta_tokens=m_meta_tokens, w_in=m_w_in[0], conv_w=m_conv_w[0], g_a2=m_g_a2[0],
                m_head_g=m_m_head_g[0], g_head_g=m_g_head_g[0], w_branch_m=m_w_branch_m[0],
                w_branch_g=m_w_branch_g[0], w_out=m_w_out[0], w_ff_gate=m_w_ff_gate[0], w_ff_up=m_w_ff_up[0],
                w_ff_down=m_w_ff_down[0])
    v_sh = dict(meta_tokens=v_meta_tokens, w_in=v_w_in[0], conv_w=v_conv_w[0], g_a2=v_g_a2[0],
                m_head_g=v_m_head_g[0], g_head_g=v_g_head_g[0], w_branch_m=v_w_branch_m[0],
                w_branch_g=v_w_branch_g[0], w_out=v_w_out[0], w_ff_gate=v_w_ff_gate[0], w_ff_up=v_w_ff_up[0],
                w_ff_down=v_w_ff_down[0])
    w_rep = dict(norm1_g=norm1_g, conv_b=conv_b, m_gate_b=m_gate_b, g_a2_b=g_a2_b, norm2_g=norm2_g, final_g=final_g)
    m_rep = dict(norm1_g=m_norm1_g, conv_b=m_conv_b, m_gate_b=m_m_gate_b, g_a2_b=m_g_a2_b, norm2_g=m_norm2_g,
                 final_g=m_final_g)
    v_rep = dict(norm1_g=v_norm1_g, conv_b=v_conv_b, m_gate_b=v_m_gate_b, g_a2_b=v_g_a2_b, norm2_g=v_norm2_g,
                 final_g=v_final_g)
    dev = 4 * lax.axis_index("x") + 2 * lax.axis_index("y") + lax.axis_index("c")
    tiny_names = [n for n, _ in TINY]
    repl_names = [n for n, _ in REPL]
    tiny_shard_shapes = [(s[0], s[1] // NDEV) for _, s in TINY]

    in8, tiny8 = all_gather([w_sh["w_in"].astype(MXU), pack_rows([w_sh[n] for n in tiny_names], ROWS_GATHER)],
                            "param_all_gather")
    late_names = ["w_branch_m", "w_branch_g", "w_out", "w_ff_gate", "w_ff_up", "w_ff_down"]
    late, first_order = send_start([w_sh[n].astype(MXU) for n in late_names], False, tiny8, "late_weights_start")
    wp = regroup_cols(in8)
    handles = {}

    def late_weights(after):
        bm8, bg8, out8, ffg8, ffu8, ffd8 = send_wait(late, after, "late_weights_wait")
        w_gu = jnp.concatenate([from_col_blocks(ffg8), from_col_blocks(ffu8)], axis=1)
        return bm8.reshape(D, D), bg8.reshape(D, D), out8.reshape(D, D), w_gu, ffd8.reshape(DFF, D)

    def send_early(g):
        blocks = [g["w_branch_m"].reshape(NDEV, D // NDEV, D).astype(WIRE),
                  g["w_branch_g"].reshape(NDEV, D // NDEV, D).astype(WIRE),
                  g["w_out"].reshape(NDEV, D // NDEV, D).astype(WIRE),
                  col_blocks(g["w_gu"][:, :DFF]).astype(WIRE), col_blocks(g["w_gu"][:, DFF:]).astype(WIRE),
                  g["w_ff_down"].reshape(NDEV, DFF // NDEV, D).astype(WIRE)]
        handles["early"], token = send_start(blocks, True, blocks[0], "early_grads_start")
        return token

    def send_wp(d_wp):
        blocks = [ungroup_cols(d_wp).astype(WIRE)]
        handles["wp"], token = send_start(blocks, True, blocks[0], "proj_grads_start")
        return token

    tiny_full = {}
    for j in range(NDEV):
        for name, blk in zip(tiny_names, unpack_rows(tiny8[j], tiny_shard_shapes)):
            tiny_full.setdefault(name, []).append(blk)
    tiny_full = {n: jnp.concatenate(v, axis=1) for n, v in tiny_full.items()}

    loss, grad_x, g = local_step(
        x[0], loss_target[0], tiny_full["meta_tokens"], norm1_g, wp, tiny_full["conv_w"], conv_b, m_gate_b[0],
        tiny_full["g_a2"], g_a2_b, tiny_full["m_head_g"], tiny_full["g_head_g"], norm2_g, final_g,
        late_weights, send_early, send_wp, first_order)

    rep = pack_rows([g[n] for n in tiny_names + repl_names] + [loss[0, 0:1]], ROWS_REP)
    (got_rep,) = exchange([], rep, "small_grad_exchange")
    got_early = send_wait(handles["early"], got_rep, "early_grads_wait")
    (got_wp,) = send_wait(handles["wp"], got_rep, "proj_grads_wait")

    result = {}

    def update(name, parts, tr):
        outs = adamw(parts, w_sh[name], m_sh[name], v_sh[name], "adamw_" + name, tr)
        for kind, arr in zip(("grad", "delta", "new_m", "new_v"), outs):
            result[kind, name] = arr[None]

    update("w_in", got_wp, 128)
    update("w_branch_m", got_early[0], 128)
    update("w_branch_g", got_early[1], 128)
    update("w_out", got_early[2], 128)
    update("w_ff_gate", got_early[3], 256)
    update("w_ff_up", got_early[4], 256)
    update("w_ff_down", got_early[5], DFF // NDEV)

    rep_sum = sum_parts(got_rep, "sum_small", 1024)
    rep_g = unpack_rows(rep_sum, [s for _, s in TINY] + [s for _, s in REPL] + [(1,)])
    own_g = [lax.dynamic_slice_in_dim(gf, dev * ss[1], ss[1], axis=1) for gf, ss in zip(rep_g, tiny_shard_shapes)]
    own_g += rep_g[len(TINY):len(TINY) + len(REPL)]
    w_all = {**w_sh, **w_rep}
    m_all = {**m_sh, **m_rep}
    v_all = {**v_sh, **v_rep}
    names = tiny_names + repl_names
    outs = adamw(pack_rows(own_g, ROWS_OWN)[None], pack_rows([w_all[n] for n in names], ROWS_OWN),
                 pack_rows([m_all[n] for n in names], ROWS_OWN), pack_rows([v_all[n] for n in names], ROWS_OWN),
                 "adamw_small", ROWS_OWN)
    shapes = tiny_shard_shapes + [s for _, s in REPL]
    for kind, packed in zip(("grad", "delta", "new_m", "new_v"), outs):
        for name, arr in zip(names, unpack_rows(packed, shapes)):
            result[kind, name] = arr[None] if name in tiny_names and name != "meta_tokens" else arr
    loss_total = rep_g[-1][0]
    order = ["meta_tokens", "norm1_g", "w_in", "conv_w", "conv_b", "m_gate_b", "g_a2", "g_a2_b", "m_head_g", "g_head_g",
             "w_branch_m", "w_branch_g", "w_out", "norm2_g", "w_ff_gate", "w_ff_up", "w_ff_down", "final_g"]
    return (loss_total, grad_x[None], *[result[kind, n] for kind in ("grad", "delta", "new_m", "new_v") for n in order])
```

```python
import functools

import jax
import jax.numpy as jnp
from jax import lax
from jax.experimental import pallas as pl
from jax.experimental.pallas import tpu as pltpu

F32 = jnp.float32
MXU = jnp.bfloat16
WIRE = jnp.bfloat16

D = 1024
NH = 4
DV = 256
DQK = 128
L = 64
NMETA = 16
PADR = 512
LM = 128
CH0 = PADR // LM - 1
NPADROWS = PADR - NMETA
RANK = 16
DFF = 2816
EPS = 1e-6
TAU = 16.0
QSCALE = DQK ** -0.5
NEG = -1e30
NDEV = 8

MV0, MO0, GQ0, GK0, GV0, GR0, QK0, GM0, GG0, SM0 = 0, 1024, 2048, 2560, 3072, 4096, 5120, 6144, 7168, 8192
NP = 8320
NPROJ = 8216

ADAM_LR, ADAM_B1, ADAM_B2, ADAM_EPS, ADAM_WD, ADAM_STEP = 0.001, 0.9, 0.999, 1e-08, 0.01, 10

VMEM_LIMIT = 56 * 1024 * 1024
TM = 512


def _cp(sem):
    return pltpu.CompilerParams(dimension_semantics=sem, vmem_limit_bytes=VMEM_LIMIT)


def _sigmoid(x):
    return 1.0 / (1.0 + jnp.exp(-x))


def _log_sigmoid(x):
    return jnp.minimum(x, 0.0) - jnp.log1p(jnp.exp(-jnp.abs(x)))


def _dot(a, b, ca, cb):
    return lax.dot_general(a.astype(MXU), b.astype(MXU), (((ca,), (cb,)), ((), ())), preferred_element_type=F32)


def _dot_exact(a, b):
    return lax.dot_general(a, b, (((1,), (0,)), ((), ())), precision=lax.Precision.HIGHEST,
                           preferred_element_type=F32)


def _rb(tm, w, cb):
    return pl.BlockSpec((tm, w), lambda i: (i, cb))


def _const(shape):
    nd = len(shape)
    return pl.BlockSpec(shape, lambda i: (0,) * nd)


def _pick(n, target):
    if n <= target:
        return n
    best = None
    for t in range(128, target + 1, 128):
        if n % t == 0:
            best = t
    assert best is not None, (n, target)
    return best


def matmul(a, b, mode, name, add=None, out_dtype=F32, tm=512, tn=1664, tk=1024, order=None):
    if mode == "nn":
        (M, K), (K2, N) = a.shape, b.shape
    elif mode == "nt":
        (M, K), (N, K2) = a.shape, b.shape
    else:
        (K, M), (K2, N) = a.shape, b.shape
    assert K == K2, (a.shape, b.shape, mode)
    tm, tn, tk = _pick(M, tm), _pick(N, tn), _pick(K, tk)
    nk = K // tk
    assert nk == 1 or out_dtype == F32
    ca, cb = {"nn": (1, 0), "nt": (1, 1), "tn": (0, 0)}[mode]
    a_spec = {"nn": pl.BlockSpec((tm, tk), lambda j, i, k: (i, k)),
              "nt": pl.BlockSpec((tm, tk), lambda j, i, k: (i, k)),
              "tn": pl.BlockSpec((tk, tm), lambda j, i, k: (k, i))}[mode]
    b_spec = {"nn": pl.BlockSpec((tk, tn), lambda j, i, k: (k, j)),
              "nt": pl.BlockSpec((tn, tk), lambda j, i, k: (j, k)),
              "tn": pl.BlockSpec((tk, tn), lambda j, i, k: (k, j))}[mode]
    o_spec = pl.BlockSpec((tm, tn), lambda j, i, k: (i, j))
    has_add = add is not None

    def body(*refs):
        if order is not None:
            refs = refs[:-2] + refs[-1:]
        if has_add:
            a_ref, b_ref, add_ref, o_ref = refs
        else:
            a_ref, b_ref, o_ref = refs
            add_ref = None
        part = _dot(a_ref[...], b_ref[...], ca, cb)
        if nk == 1:
            if has_add:
                part = part + add_ref[...]
            o_ref[...] = part.astype(o_ref.dtype)
            return
        k = pl.program_id(2)

        @pl.when(k == 0)
        def _():
            o_ref[...] = part + add_ref[...] if has_add else part

        @pl.when(k > 0)
        def _():
            o_ref[...] += part

    in_specs = [a_spec, b_spec] + ([o_spec] if has_add else [])
    args = (a, b) + ((add,) if has_add else ())
    if order is not None:
        in_specs.append(pl.BlockSpec(order.shape, lambda j, i, k: (0, 0)))
        args += (order,)
    return pl.pallas_call(
        body, name=name, grid=(N // tn, M // tm, nk),
        in_specs=in_specs, out_specs=o_spec,
        out_shape=jax.ShapeDtypeStruct((M, N), out_dtype),
        compiler_params=_cp(("parallel", "parallel", "arbitrary")),
    )(*args)


def rms_fwd(x, g, name):
    R = x.shape[0]

    def body(x_ref, g_ref, y_ref):
        xv = x_ref[...]
        r = lax.rsqrt(jnp.mean(xv * xv, axis=-1, keepdims=True) + EPS)
        y_ref[...] = (xv * r * g_ref[...]).astype(y_ref.dtype)

    return pl.pallas_call(
        body, name=name, grid=(R // TM,),
        in_specs=[_rb(TM, D, 0), _const((1, D))], out_specs=_rb(TM, D, 0),
        out_shape=jax.ShapeDtypeStruct((R, D), MXU), compiler_params=_cp(("parallel",)),
    )(x, g)


def rms_bwd(dy, x, g, dres, name):
    R = x.shape[0]

    def body(dy_ref, x_ref, g_ref, dres_ref, dx_ref, dg_ref):
        i = pl.program_id(0)
        xv, dyv = x_ref[...], dy_ref[...]
        r = lax.rsqrt(jnp.mean(xv * xv, axis=-1, keepdims=True) + EPS)
        dyg = dyv * g_ref[...]
        dx_ref[...] = dres_ref[...] + r * dyg - xv * (r * r * r * jnp.mean(dyg * xv, axis=-1, keepdims=True))
        part = jnp.sum(dyv * xv * r, axis=0, keepdims=True)

        @pl.when(i == 0)
        def _():
            dg_ref[...] = part

        @pl.when(i > 0)
        def _():
            dg_ref[...] += part

    return pl.pallas_call(
        body, name=name, grid=(R // TM,),
        in_specs=[_rb(TM, D, 0), _rb(TM, D, 0), _const((1, D)), _rb(TM, D, 0)],
        out_specs=[_rb(TM, D, 0), _const((1, D))],
        out_shape=[jax.ShapeDtypeStruct((R, D), F32), jax.ShapeDtypeStruct((1, D), F32)],
        compiler_params=_cp(("arbitrary",)),
    )(dy, x, g, dres)


def rms_bwd_input(dy, x, g, dres, name):
    R = x.shape[0]
    assert PADR == TM

    def body(dy_ref, x_ref, g_ref, dres_ref, dx_ref, dmeta_ref, dg_ref):
        i = pl.program_id(0)
        xv, dyv = x_ref[...], dy_ref[...]
        r = lax.rsqrt(jnp.mean(xv * xv, axis=-1, keepdims=True) + EPS)
        dyg = dyv * g_ref[...]
        dx = dres_ref[...] + r * dyg - xv * (r * r * r * jnp.mean(dyg * xv, axis=-1, keepdims=True))
        dx_ref[...] = dx
        part = jnp.sum(dyv * xv * r, axis=0, keepdims=True)

        @pl.when(i == 0)
        def _():
            dg_ref[...] = part
            dmeta_ref[...] = dx[NPADROWS:PADR]

        @pl.when(i > 0)
        def _():
            dg_ref[...] += part

    return pl.pallas_call(
        body, name=name, grid=(R // TM,),
        in_specs=[_rb(TM, D, 0), _rb(TM, D, 0), _const((1, D)), _rb(TM, D, 0)],
        out_specs=[pl.BlockSpec((TM, D), lambda i: (jnp.maximum(i - 1, 0), 0)), _const((NMETA, D)), _const((1, D))],
        out_shape=[jax.ShapeDtypeStruct((R - PADR, D), F32), jax.ShapeDtypeStruct((NMETA, D), F32),
                   jax.ShapeDtypeStruct((1, D), F32)],
        compiler_params=_cp(("arbitrary",)),
    )(dy, x, g, dres)


def _shift_down(cur, prev8, s):
    tm = cur.shape[0]
    rolled = pltpu.roll(cur, s, 0)
    rows8 = lax.broadcasted_iota(jnp.int32, (8, cur.shape[1]), 0)
    head = jnp.where(rows8 < s, pltpu.roll(prev8, s, 0), rolled[0:8])
    return jnp.concatenate([head, rolled[8:tm]], axis=0)


def _shift_up(cur, next8, s):
    tm = cur.shape[0]
    rolled = pltpu.roll(cur, tm - s, 0)
    rows8 = lax.broadcasted_iota(jnp.int32, (8, cur.shape[1]), 0)
    tail = jnp.where(rows8 >= 8 - s, pltpu.roll(next8, 8 - s, 0), rolled[tm - 8:tm])
    return jnp.concatenate([rolled[0:tm - 8], tail], axis=0)


def prep_fwd(proj, conv_w, conv_b, gb_row):
    R = proj.shape[0]
    t8 = TM // 8

    def body(x_ref, halo_ref, sm_ref, w_ref, b_ref, gb_ref, c_ref, qk_ref, gl_ref):
        i = pl.program_id(0)
        x = x_ref[...]
        halo = halo_ref[...]
        w = w_ref[...]
        c = x * w[3:4, :] + b_ref[...]
        for s in (1, 2, 3):
            c = c + _shift_down(x, halo, s) * w[3 - s:4 - s, :]
        c_ref[...] = c
        qk_ref[...] = c * _sigmoid(c)
        z = sm_ref[...] + gb_ref[...]
        lane = lax.broadcasted_iota(jnp.int32, z.shape, 1)
        row = lax.broadcasted_iota(jnp.int32, z.shape, 0) + i * TM
        valid = row >= NPADROWS
        logi = jnp.where(valid, z, NEG)
        logf = jnp.where(valid, _log_sigmoid(z), 0.0)
        gl_ref[...] = jnp.where(lane < 4, logi, jnp.where(lane < 8, logf, 0.0))

    return pl.pallas_call(
        body, name="prep_fwd", grid=(R // TM,),
        in_specs=[_rb(TM, 1024, QK0 // 1024),
                  pl.BlockSpec((8, 1024), lambda i: (jnp.maximum(i * t8 - 1, 0), QK0 // 1024)),
                  _rb(TM, 128, SM0 // 128), _const((4, 1024)), _const((1, 1024)), _const((1, 128))],
        out_specs=[_rb(TM, 1024, 0), _rb(TM, 1024, 0), _rb(TM, 128, 0)],
        out_shape=[jax.ShapeDtypeStruct((R, 1024), F32), jax.ShapeDtypeStruct((R, 1024), F32),
                   jax.ShapeDtypeStruct((R, 128), F32)],
        compiler_params=_cp(("parallel",)),
    )(proj, proj, proj, conv_w, conv_b, gb_row)


def merge_fwd(bm, bg, proj):
    R = bm.shape[0]

    def body(bm_ref, bg_ref, gm_ref, gg_ref, o_ref):
        o_ref[...] = (_sigmoid(gm_ref[...]) * bm_ref[...] + _sigmoid(gg_ref[...]) * bg_ref[...]).astype(o_ref.dtype)

    return pl.pallas_call(
        body, name="merge_fwd", grid=(R // TM,),
        in_specs=[_rb(TM, D, 0), _rb(TM, D, 0), _rb(TM, D, GM0 // D), _rb(TM, D, GG0 // D)],
        out_specs=_rb(TM, D, 0), out_shape=jax.ShapeDtypeStruct((R, D), MXU),
        compiler_params=_cp(("parallel",)),
    )(bm, bg, proj, proj)


def merge_bwd(dmerged, bm, bg, proj):
    R = bm.shape[0]

    def body(dm_ref, bm_ref, bg_ref, gm_ref, gg_ref, dbm_ref, dbg_ref, dp_ref):
        dm = dm_ref[...]
        sm, sg = _sigmoid(gm_ref[...]), _sigmoid(gg_ref[...])
        dbm_ref[...] = (dm * sm).astype(dbm_ref.dtype)
        dbg_ref[...] = (dm * sg).astype(dbg_ref.dtype)
        dp_ref[:, 0:D] = (dm * bm_ref[...] * sm * (1.0 - sm)).astype(dp_ref.dtype)
        dp_ref[:, D:2 * D] = (dm * bg_ref[...] * sg * (1.0 - sg)).astype(dp_ref.dtype)

    return pl.pallas_call(
        body, name="merge_bwd", grid=(R // TM,),
        in_specs=[_rb(TM, D, 0), _rb(TM, D, 0), _rb(TM, D, 0), _rb(TM, D, GM0 // D), _rb(TM, D, GG0 // D)],
        out_specs=[_rb(TM, D, 0), _rb(TM, D, 0), _rb(TM, 2 * D, GM0 // (2 * D))],
        out_shape=[jax.ShapeDtypeStruct((R, D), MXU), jax.ShapeDtypeStruct((R, D), MXU),
                   jax.ShapeDtypeStruct((R, NP), MXU)],
        compiler_params=_cp(("parallel",)),
    )(dmerged, bm, bg, proj, proj)


def swiglu_fwd(au):
    R = au.shape[0]
    tw = DFF // 2

    def body(a_ref, u_ref, o_ref):
        a = a_ref[...].astype(F32)
        o_ref[...] = (a * _sigmoid(a) * u_ref[...].astype(F32)).astype(o_ref.dtype)

    return pl.pallas_call(
        body, name="swiglu_fwd", grid=(R // TM, 2),
        in_specs=[pl.BlockSpec((TM, tw), lambda i, j: (i, j)), pl.BlockSpec((TM, tw), lambda i, j: (i, j + 2))],
        out_specs=pl.BlockSpec((TM, tw), lambda i, j: (i, j)),
        out_shape=jax.ShapeDtypeStruct((R, DFF), MXU), compiler_params=_cp(("parallel", "parallel")),
    )(au, au)


def swiglu_bwd(au, dff):
    R = au.shape[0]
    tw = DFF // 2

    def body(a_ref, u_ref, d_ref, dau_ref):
        a, d = a_ref[...].astype(F32), d_ref[...]
        s = _sigmoid(a)
        dau_ref[:, 0:DFF] = (d * u_ref[...].astype(F32) * s * (1.0 + a * (1.0 - s))).astype(dau_ref.dtype)
        dau_ref[:, DFF:2 * DFF] = (d * a * s).astype(dau_ref.dtype)

    tr = 256
    return pl.pallas_call(
        body, name="swiglu_bwd", grid=(R // tr,),
        in_specs=[_rb(tr, DFF, 0), _rb(tr, DFF, 1), _rb(tr, DFF, 0)],
        out_specs=_rb(tr, 2 * DFF, 0),
        out_shape=jax.ShapeDtypeStruct((R, 2 * DFF), MXU),
        compiler_params=_cp(("parallel",)),
    )(au, au, dff)


def final_loss(h2, gf, target):
    R = h2.shape[0]
    assert PADR == TM

    def body(h_ref, g_ref, t_ref, dh_ref, loss_ref, dg_ref):
        i = pl.program_id(0)
        hv = h_ref[...]
        r = lax.rsqrt(jnp.mean(hv * hv, axis=-1, keepdims=True) + EPS)
        g = g_ref[...]
        live = (i >= 1).astype(F32)
        e = (hv * r * g - t_ref[...]) * live
        dy = e * (1.0 / D)
        dyg = dy * g
        dh_ref[...] = r * dyg - hv * (r * r * r * jnp.mean(dyg * hv, axis=-1, keepdims=True))
        lpart = jnp.zeros((1, 128), F32) + 0.5 * jnp.sum(jnp.sum(e * e, axis=1, keepdims=True), axis=0, keepdims=True) * (1.0 / D)
        gpart = jnp.sum(dy * hv * r, axis=0, keepdims=True)

        @pl.when(i == 0)
        def _():
            loss_ref[...] = lpart
            dg_ref[...] = gpart

        @pl.when(i > 0)
        def _():
            loss_ref[...] += lpart
            dg_ref[...] += gpart

    return pl.pallas_call(
        body, name="final_loss", grid=(R // TM,),
        in_specs=[_rb(TM, D, 0), _const((1, D)), pl.BlockSpec((TM, D), lambda i: (jnp.maximum(i - 1, 0), 0))],
        out_specs=[_rb(TM, D, 0), _const((1, 128)), _const((1, D))],
        out_shape=[jax.ShapeDtypeStruct((R, D), F32), jax.ShapeDtypeStruct((1, 128), F32),
                   jax.ShapeDtypeStruct((1, D), F32)],
        compiler_params=_cp(("arbitrary",)),
    )(h2, gf, target)


def conv_bwd(dc, proj, conv_w, dproj):
    R = dc.shape[0]
    t8 = TM // 8
    nt = R // TM

    def body(dc_ref, nxt_ref, x_ref, prv_ref, w_ref, dp_in, dp_ref, dw_ref):
        del dp_in
        i = pl.program_id(0)
        dcv = dc_ref[...]
        nxt = nxt_ref[...] * (i < nt - 1).astype(F32)
        x = x_ref[...]
        prv = prv_ref[...]
        w = w_ref[...]
        dx = dcv * w[3:4, :]
        rows = [None] * 4
        rows[3] = jnp.sum(dcv * x, axis=0, keepdims=True)
        for s in (1, 2, 3):
            dx = dx + _shift_up(dcv, nxt, s) * w[3 - s:4 - s, :]
            rows[3 - s] = jnp.sum(dcv * _shift_down(x, prv, s), axis=0, keepdims=True)
        dp_ref[...] = dx.astype(dp_ref.dtype)
        part = jnp.concatenate(rows + [jnp.sum(dcv, axis=0, keepdims=True), jnp.zeros((3, 1024), F32)], axis=0)

        @pl.when(i == 0)
        def _():
            dw_ref[...] = part

        @pl.when(i > 0)
        def _():
            dw_ref[...] += part

    return pl.pallas_call(
        body, name="conv_bwd", grid=(nt,),
        in_specs=[_rb(TM, 1024, 0),
                  pl.BlockSpec((8, 1024), lambda i: (jnp.minimum((i + 1) * t8, nt * t8 - 1), 0)),
                  _rb(TM, 1024, QK0 // 1024),
                  pl.BlockSpec((8, 1024), lambda i: (jnp.maximum(i * t8 - 1, 0), QK0 // 1024)),
                  _const((4, 1024)), pl.BlockSpec(memory_space=pl.ANY)],
        out_specs=[_rb(TM, 1024, QK0 // 1024), _const((8, 1024))],
        out_shape=[jax.ShapeDtypeStruct((R, NP), MXU), jax.ShapeDtypeStruct((8, 1024), F32)],
        input_output_aliases={5: 0},
        compiler_params=_cp(("arbitrary",)),
    )(dc, dc, proj, proj, conv_w, dproj)


def small_bwd(dgl, dga, proj, gb_row, dproj):
    R = dgl.shape[0]

    def body(dgl_ref, dga_ref, sm_ref, gb_ref, dp_in, dp_ref, dgb_ref):
        del dp_in
        i = pl.program_id(0)
        z = sm_ref[...] + gb_ref[...]
        lane = lax.broadcasted_iota(jnp.int32, z.shape, 1)
        row = lax.broadcasted_iota(jnp.int32, z.shape, 0) + i * TM
        valid = row >= NPADROWS
        dgl_v = dgl_ref[...]
        dgate = jnp.where(valid, jnp.where(lane < 4, dgl_v, dgl_v * _sigmoid(-z)), 0.0)
        ds = jnp.where(lane < 8, dgate, dga_ref[...])
        dp_ref[...] = ds.astype(dp_ref.dtype)
        part = jnp.sum(jnp.where(lane < 8, dgate, 0.0), axis=0, keepdims=True)

        @pl.when(i == 0)
        def _():
            dgb_ref[...] = part

        @pl.when(i > 0)
        def _():
            dgb_ref[...] += part

    return pl.pallas_call(
        body, name="small_bwd", grid=(R // TM,),
        in_specs=[_rb(TM, 128, 0), _rb(TM, 128, 0), _rb(TM, 128, SM0 // 128), _const((1, 128)),
                  pl.BlockSpec(memory_space=pl.ANY)],
        out_specs=[_rb(TM, 128, SM0 // 128), _const((1, 128))],
        out_shape=[jax.ShapeDtypeStruct((R, NP), MXU), jax.ShapeDtypeStruct((1, 128), F32)],
        input_output_aliases={4: 0},
        compiler_params=_cp(("arbitrary",)),
    )(dgl, dga, proj, gb_row, dproj)


def _masks(n=L):
    r = lax.broadcasted_iota(jnp.int32, (n, n), 0)
    c = lax.broadcasted_iota(jnp.int32, (n, n), 1)
    return r >= c, r == c, r


def _to_row(col, eye):
    return jnp.sum(jnp.where(eye, col, 0.0), axis=0, keepdims=True)


def _to_col(row, eye):
    return jnp.sum(jnp.where(eye, row, 0.0), axis=1, keepdims=True)


def _mlstm_chunk(q, k, logi_c, logf_c, m, n):
    tril, eye, _ = _masks(q.shape[0])
    logi_r, logf_r = _to_row(logi_c, eye), _to_row(logf_c, eye)
    b_c = jnp.sum(jnp.where(tril, logf_r, 0.0), axis=1, keepdims=True)
    b_r = _to_row(b_c, eye)
    g = jnp.sum(logf_c, axis=0, keepdims=True)
    dmat = jnp.where(tril, b_c - b_r + logi_r, NEG)
    mrow = jnp.maximum(b_c + m, jnp.max(dmat, axis=1, keepdims=True))
    dm = jnp.exp(dmat - mrow)
    s = _dot(q, k, 1, 1)
    w = dm * s
    a_in = jnp.exp(b_c + m - mrow)
    qn = jnp.sum(q * n, axis=1, keepdims=True)
    den = a_in * qn + jnp.sum(w, axis=1, keepdims=True)
    floor = jnp.exp(-mrow)
    nrm = jnp.maximum(jnp.abs(den), floor)
    wlog_c = g - b_c + logi_c
    m_new = jnp.maximum(g + m, jnp.max(wlog_c, axis=0, keepdims=True))
    a_st = jnp.exp(g + m - m_new)
    w_c = jnp.exp(wlog_c - m_new)
    return dict(b_c=b_c, g=g, dm=dm, s=s, w=w, a_in=a_in, qn=qn, den=den, floor=floor, nrm=nrm,
                m_new=m_new, a_st=a_st, w_c=w_c, tril=tril, eye=eye)


def _mlstm_head_fwd(h, glv, qk_ref, v_ref, mo_ref, hg_ref, hm_ref, ym_ref, cs_ref, nm_ref, c_s, nm_s):
    q = qk_ref[:, h * DQK:(h + 1) * DQK] * QSCALE
    k = qk_ref[:, 512 + h * DQK:512 + (h + 1) * DQK]
    v = v_ref[:, h * DV:(h + 1) * DV]
    C = c_s[h]
    n = nm_s[h, 0:1, :]
    m = nm_s[h, 1:2, 0:1]
    f = _mlstm_chunk(q, k, glv[:, h:h + 1], glv[:, 4 + h:5 + h], m, n)
    num = f["a_in"] * _dot(q, C, 1, 1) + _dot(f["w"], v, 1, 0)
    hh = num / f["nrm"]
    cs_ref[0, h] = C
    nm_ref[0, h] = nm_s[h]
    c_s[h] = f["a_st"] * C + _dot(f["w_c"] * v, k, 0, 0)
    n_new = f["a_st"] * n + jnp.sum(f["w_c"] * k, axis=0, keepdims=True)
    rowi = lax.broadcasted_iota(jnp.int32, (8, DQK), 0)
    nm_s[h] = jnp.where(rowi == 0, n_new, jnp.where(rowi == 1, f["m_new"], 0.0))
    rm = lax.rsqrt(jnp.mean(hh * hh, axis=-1, keepdims=True) + EPS)
    sl = slice(h * DV, (h + 1) * DV)
    hm_ref[:, sl] = hh
    ym_ref[:, sl] = (hh * rm * hg_ref[:, sl] * _sigmoid(mo_ref[:, sl])).astype(ym_ref.dtype)


def _mlstm_bwd_parts(dym_ref, hm_ref, qk_ref, cp_ref, v_ref, gl_ref, mo_ref, hg_ref, cs_ref, nm_ref,
                     dp_ref, dc_ref, dgl_ref, dhg_ref, dc_s, dn_s):
        def init():
            dc_s[...] = jnp.zeros_like(dc_s)
            dn_s[...] = jnp.zeros_like(dn_s)
            dhg_ref[...] = jnp.zeros_like(dhg_ref)

        def zero():
            dp_ref[...] = jnp.zeros_like(dp_ref)
            dc_ref[...] = jnp.zeros_like(dc_ref)
            dgl_ref[...] = jnp.zeros_like(dgl_ref)

        def compute():
            glv = gl_ref[...]
            lane = lax.broadcasted_iota(jnp.int32, (LM, 128), 1)
            dgl = jnp.zeros((LM, 128), F32)
            for h in range(NH):
                sl = slice(h * DV, (h + 1) * DV)
                sq = slice(h * DQK, (h + 1) * DQK)
                sk = slice(512 + h * DQK, 512 + (h + 1) * DQK)
                hh = hm_ref[:, sl]
                gain = hg_ref[:, sl]
                rm = lax.rsqrt(jnp.mean(hh * hh, axis=-1, keepdims=True) + EPS)
                sg = _sigmoid(mo_ref[:, sl])
                dyv = dym_ref[:, sl]
                dno = dyv * sg
                dp_ref[:, 1024 + h * DV:1024 + (h + 1) * DV] = (dyv * hh * rm * gain * sg * (1.0 - sg)).astype(dp_ref.dtype)
                dhg_ref[:, sl] += jnp.sum(dno * hh * rm, axis=0, keepdims=True)
                dnog = dno * gain
                dh = rm * dnog - hh * (rm * rm * rm * jnp.mean(dnog * hh, axis=-1, keepdims=True))
                q = qk_ref[:, sq] * QSCALE
                k = qk_ref[:, sk]
                v = v_ref[:, sl]
                C = cs_ref[0, h]
                n = nm_ref[0, h, 0:1, :]
                m = nm_ref[0, h, 1:2, 0:1]
                f = _mlstm_chunk(q, k, glv[:, h:h + 1], glv[:, 4 + h:5 + h], m, n)
                eye = f["eye"]
                a_in, nrm, den, w = f["a_in"], f["nrm"], f["den"], f["w"]
                dnum = dh / nrm
                dnrm = -jnp.sum(dh * hh, axis=1, keepdims=True) / nrm
                dden = jnp.where(jnp.abs(den) >= f["floor"], dnrm * jnp.sign(den), 0.0)
                dw = _dot(dnum, v, 1, 1) + dden
                dv = _dot(w, dnum, 0, 0)
                ds = dw * f["dm"]
                e = dw * w
                qc = _dot(q, C, 1, 1)
                dq = _dot(ds, k, 1, 0) + a_in * _dot(dnum, C, 1, 0) + (a_in * dden) * n
                dk = _dot(ds, q, 0, 0)
                dC_in = _dot(a_in * dnum, q, 0, 0)
                dn_in = jnp.sum((a_in * dden) * q, axis=0, keepdims=True)
                da_in = jnp.sum(dnum * qc, axis=1, keepdims=True) + dden * f["qn"]
                col_e = _to_col(jnp.sum(e, axis=0, keepdims=True), eye)
                db = jnp.sum(e, axis=1, keepdims=True) + da_in * a_in - col_e
                dlogi = col_e
                dCp = dc_s[h]
                dnp = dn_s[h, 0:1, :]
                a_st, w_c = f["a_st"], f["w_c"]
                da_st = (jnp.sum(jnp.sum(dCp * C, axis=1, keepdims=True), axis=0, keepdims=True)
                         + jnp.sum(dnp * n, axis=1, keepdims=True))
                vdc = _dot(v, dCp, 1, 0)
                dw_c = jnp.sum((vdc + dnp) * k, axis=1, keepdims=True)
                dv = dv + w_c * _dot(k, dCp, 1, 1)
                dk = dk + w_c * (vdc + dnp)
                fw = dw_c * w_c
                dg = jnp.sum(fw, axis=0, keepdims=True) + da_st * a_st
                db = db - fw
                dlogi = dlogi + fw
                rowc = lax.broadcasted_iota(jnp.int32, (LM, 1), 0)
                db = db + jnp.where(rowc == LM - 1, dg, 0.0)
                triu = lax.broadcasted_iota(jnp.int32, (LM, LM), 1) >= lax.broadcasted_iota(jnp.int32, (LM, LM), 0)
                dlogf = jnp.sum(jnp.where(triu, _to_row(db, eye), 0.0), axis=1, keepdims=True)
                dc_s[h] = a_st * dCp + dC_in
                dn_new = a_st * dnp + dn_in
                dn_s[h] = jnp.zeros((8, DQK), F32) + dn_new
                cq, ck = cp_ref[:, sq], cp_ref[:, sk]
                s_q, s_k = _sigmoid(cq), _sigmoid(ck)
                dc_ref[:, sq] = dq * QSCALE * s_q * (1.0 + cq * (1.0 - s_q))
                dc_ref[:, sk] = dk * s_k * (1.0 + ck * (1.0 - s_k))
                dp_ref[:, sl] = dv.astype(dp_ref.dtype)
                dgl = jnp.where(lane == h, dlogi, jnp.where(lane == 4 + h, dlogf, dgl))
            dgl_ref[...] = dgl

        return init, zero, compute


def _gla_logs(sm, a2p, b2, valid):
    za = _dot(sm, a2p, 1, 0) + b2
    return za, jnp.where(valid, _log_sigmoid(za) * (1.0 / TAU), 0.0)


def _valid_rows(c, width):
    row = lax.broadcasted_iota(jnp.int32, (L, width), 0) + c * L
    return row >= NPADROWS


def _gla_chunk(q, k, la):
    tril, _, _ = _masks()
    bc = _dot_exact(tril.astype(F32), la)
    btot = jnp.sum(la, axis=0, keepdims=True)
    ebc = jnp.exp(bc)
    qd = q * ebc
    ki = k * jnp.exp(-bc)
    ke = k * jnp.exp(btot - bc)
    att = jnp.where(tril, _dot(qd, ki, 1, 1), 0.0)
    return dict(tril=tril, bc=bc, btot=btot, ebc=ebc, qd=qd, ki=ki, ke=ke, att=att)


def _col128(row):
    r = lax.broadcasted_iota(jnp.int32, (DQK, DQK), 0)
    c = lax.broadcasted_iota(jnp.int32, (DQK, DQK), 1)
    return jnp.sum(jnp.where(r == c, row, 0.0), axis=1, keepdims=True)


def _gla_fwd_parts(c, q_ref, k_ref, v_ref, gr_ref, sm_ref, a2_ref, b2_ref, hg_ref, hgl_ref, yg_ref, ss_ref, s_s):
        def init():
            s_s[...] = jnp.zeros_like(s_s)

        def zero():
            hgl_ref[...] = jnp.zeros_like(hgl_ref)
            yg_ref[...] = jnp.zeros_like(yg_ref)
            ss_ref[...] = jnp.zeros_like(ss_ref)

        def compute():
            for s in range(LM // L):
                rows = pl.ds(s * L, L)
                chunk(LM // L * c + s, q_ref.at[rows], k_ref.at[rows], v_ref.at[rows], gr_ref.at[rows], sm_ref.at[rows],
                      hgl_ref.at[rows], yg_ref.at[rows], ss_ref.at[pl.ds(s, 1)])

        def chunk(c, q_ref, k_ref, v_ref, gr_ref, sm_ref, hgl_ref, yg_ref, ss_ref):
            _, loga = _gla_logs(sm_ref[...], a2_ref[...], b2_ref[...], _valid_rows(c, 512))
            for h in range(NH):
                sq = slice(h * DQK, (h + 1) * DQK)
                sl = slice(h * DV, (h + 1) * DV)
                q = q_ref[:, sq] * QSCALE
                k = k_ref[:, sq]
                v = v_ref[:, sl]
                S = s_s[h]
                f = _gla_chunk(q, k, loga[:, sq])
                o = _dot(f["att"], v, 1, 0) + _dot(f["qd"], S, 1, 0)
                ss_ref[0, h] = S
                s_s[h] = _col128(jnp.exp(f["btot"])) * S + _dot(f["ke"], v, 0, 0)
                rg = lax.rsqrt(jnp.mean(o * o, axis=-1, keepdims=True) + EPS)
                gr = gr_ref[:, sl]
                hgl_ref[:, sl] = o
                yg_ref[:, sl] = (o * rg * hg_ref[:, sl] * gr * _sigmoid(gr)).astype(yg_ref.dtype)

        return init, zero, compute


def _gla_bwd_parts(c, dy_ref, ho_ref, q_ref, k_ref, v_ref, gr_ref, sm_ref, a2_ref, b2_ref, hg_ref, ss_ref,
                   dp_ref, dga_ref, da2_ref, db2_ref, dhg_ref, ds_s):
        def init():
            ds_s[...] = jnp.zeros_like(ds_s)
            da2_ref[...] = jnp.zeros_like(da2_ref)
            db2_ref[...] = jnp.zeros_like(db2_ref)
            dhg_ref[...] = jnp.zeros_like(dhg_ref)

        def zero():
            dp_ref[...] = jnp.zeros_like(dp_ref)
            dga_ref[...] = jnp.zeros_like(dga_ref)

        def compute():
            for s in reversed(range(LM // L)):
                rows = pl.ds(s * L, L)
                chunk(LM // L * c + s, dy_ref.at[rows], ho_ref.at[rows], q_ref.at[rows], k_ref.at[rows], v_ref.at[rows],
                      gr_ref.at[rows], sm_ref.at[rows], ss_ref.at[pl.ds(s, 1)], dp_ref.at[rows], dga_ref.at[rows])

        def chunk(c, dy_ref, ho_ref, q_ref, k_ref, v_ref, gr_ref, sm_ref, ss_ref, dp_ref, dga_ref):
            valid = _valid_rows(c, 512)
            sm = sm_ref[...]
            za, loga = _gla_logs(sm, a2_ref[...], b2_ref[...], valid)
            dloga = []
            for h in range(NH):
                sq = slice(h * DQK, (h + 1) * DQK)
                sl = slice(h * DV, (h + 1) * DV)
                o = ho_ref[:, sl]
                gain = hg_ref[:, sl]
                rg = lax.rsqrt(jnp.mean(o * o, axis=-1, keepdims=True) + EPS)
                gr = gr_ref[:, sl]
                sg = _sigmoid(gr)
                dyv = dy_ref[:, sl]
                dno = dyv * gr * sg
                dp_ref[:, 2048 + h * DV:2048 + (h + 1) * DV] = (
                    dyv * o * rg * gain * sg * (1.0 + gr * (1.0 - sg))).astype(dp_ref.dtype)
                dhg_ref[:, sl] += jnp.sum(dno * o * rg, axis=0, keepdims=True)
                dnog = dno * gain
                do = rg * dnog - o * (rg * rg * rg * jnp.mean(dnog * o, axis=-1, keepdims=True))
                q = q_ref[:, sq] * QSCALE
                k = k_ref[:, sq]
                v = v_ref[:, sl]
                S = ss_ref[0, h]
                f = _gla_chunk(q, k, loga[:, sq])
                tril, qd, ki, ke = f["tril"], f["qd"], f["ki"], f["ke"]
                dSp = ds_s[h]
                datt = jnp.where(tril, _dot(do, v, 1, 1), 0.0)
                dqd = _dot(do, S, 1, 1) + _dot(datt, ki, 1, 0)
                dki = _dot(datt, qd, 0, 0)
                dv = _dot(f["att"], do, 0, 0) + _dot(ke, dSp, 1, 0)
                dke = _dot(v, dSp, 1, 1)
                ebt = jnp.exp(f["btot"])
                dbtot = jnp.sum(dke * ke, axis=0, keepdims=True) + ebt * _to_row128(jnp.sum(dSp * S, axis=1, keepdims=True))
                ds_s[h] = _dot(qd, do, 0, 0) + _col128(ebt) * dSp
                dq = dqd * f["ebc"]
                dk = dki * jnp.exp(-f["bc"]) + dke * jnp.exp(f["btot"] - f["bc"])
                dbc = dqd * qd - dki * ki - dke * ke
                rowc = lax.broadcasted_iota(jnp.int32, (L, DQK), 0)
                dbc = dbc + jnp.where(rowc == L - 1, dbtot, 0.0)
                triu = lax.broadcasted_iota(jnp.int32, (L, L), 1) >= lax.broadcasted_iota(jnp.int32, (L, L), 0)
                dloga.append(_dot_exact(triu.astype(F32), dbc))
                dp_ref[:, sq] = (dq * QSCALE).astype(dp_ref.dtype)
                dp_ref[:, 512 + h * DQK:512 + (h + 1) * DQK] = dk.astype(dp_ref.dtype)
                dp_ref[:, 1024 + h * DV:1024 + (h + 1) * DV] = dv.astype(dp_ref.dtype)
            dza = jnp.where(valid, jnp.concatenate(dloga, axis=1) * (1.0 / TAU) * _sigmoid(-za), 0.0)
            dga_ref[...] = _dot(dza, a2_ref[...], 1, 1)
            da2_ref[...] += _dot(sm, dza, 0, 0)
            db2_ref[...] += jnp.sum(dza, axis=0, keepdims=True)

        return init, zero, compute


def mix_fwd(qk, proj, gl, m_head_g, a2p, b2, g_head_g):
    R = qk.shape[0]
    NC = R // LM
    G = LM // L

    def body(qk_ref, mv_ref, gl_ref, mo_ref, mhg_ref, gq_ref, gk_ref, gv_ref, gr_ref, sm_ref, a2_ref, b2_ref, ghg_ref,
             hm_ref, ym_ref, cs_ref, nm_ref, hgl_ref, yg_ref, ss_ref, c_s, nm_s, s_s):
        c = pl.program_id(0)
        g_init, g_zero, g_compute = _gla_fwd_parts(c, gq_ref, gk_ref, gv_ref, gr_ref, sm_ref, a2_ref, b2_ref, ghg_ref,
                                                   hgl_ref, yg_ref, ss_ref, s_s)

        @pl.when(c <= CH0)
        def _():
            c_s[...] = jnp.zeros_like(c_s)
            nm_s[...] = jnp.zeros_like(nm_s)
            g_init()

        @pl.when(c < CH0)
        def _():
            hm_ref[...] = jnp.zeros_like(hm_ref)
            ym_ref[...] = jnp.zeros_like(ym_ref)
            cs_ref[...] = jnp.zeros_like(cs_ref)
            nm_ref[...] = jnp.zeros_like(nm_ref)
            g_zero()

        @pl.when(c >= CH0)
        def _():
            glv = gl_ref[...]
            for h in range(NH):
                _mlstm_head_fwd(h, glv, qk_ref, mv_ref, mo_ref, mhg_ref, hm_ref, ym_ref, cs_ref, nm_ref, c_s, nm_s)
            g_compute()

    st_m = pl.BlockSpec((1, NH, DV, DQK), lambda c: (c, 0, 0, 0))
    st_n = pl.BlockSpec((1, NH, 8, DQK), lambda c: (c, 0, 0, 0))
    st_g = pl.BlockSpec((G, NH, DQK, DV), lambda c: (c, 0, 0, 0))
    return pl.pallas_call(
        body, name="mix_fwd", grid=(NC,),
        in_specs=[_rb(LM, 1024, 0), _rb(LM, 1024, MV0 // 1024), _rb(LM, 128, 0), _rb(LM, 1024, MO0 // 1024),
                  _const((1, 1024)),
                  _rb(LM, 512, GQ0 // 512), _rb(LM, 512, GK0 // 512), _rb(LM, 1024, GV0 // 1024),
                  _rb(LM, 1024, GR0 // 1024), _rb(LM, 128, SM0 // 128), _const((128, 512)), _const((1, 512)),
                  _const((1, 1024))],
        out_specs=[_rb(LM, 1024, 0), _rb(LM, 1024, 0), st_m, st_n, _rb(LM, 1024, 0), _rb(LM, 1024, 0), st_g],
        out_shape=[jax.ShapeDtypeStruct((R, 1024), F32), jax.ShapeDtypeStruct((R, 1024), MXU),
                   jax.ShapeDtypeStruct((NC, NH, DV, DQK), F32), jax.ShapeDtypeStruct((NC, NH, 8, DQK), F32),
                   jax.ShapeDtypeStruct((R, 1024), F32), jax.ShapeDtypeStruct((R, 1024), MXU),
                   jax.ShapeDtypeStruct((G * NC, NH, DQK, DV), F32)],
        scratch_shapes=[pltpu.VMEM((NH, DV, DQK), F32), pltpu.VMEM((NH, 8, DQK), F32),
                        pltpu.VMEM((NH, DQK, DV), F32)],
        compiler_params=_cp(("arbitrary",)),
    )(qk, proj, gl, proj, m_head_g, proj, proj, proj, proj, proj, a2p, b2, g_head_g)


def mix_bwd(dym, hm, qk, cpre, proj, gl, m_head_g, cs, nm, dyg, hgl, a2p, b2, g_head_g, ss, dproj):
    R = qk.shape[0]
    NC = R // LM
    rev = lambda c: NC - 1 - c
    GW = GR0 + 1024 - GQ0

    def body(dym_ref, hm_ref, qk_ref, cp_ref, mv_ref, gl_ref, mo_ref, mhg_ref, cs_ref, nm_ref,
             dyg_ref, ho_ref, gq_ref, gk_ref, gv_ref, gr_ref, sm_ref, a2_ref, b2_ref, ghg_ref, ss_ref, dp_in,
             dp_ref, dc_ref, dgl_ref, dmhg_ref, dga_ref, da2_ref, db2_ref, dghg_ref, dc_s, dn_s, ds_s):
        del dp_in
        step = pl.program_id(0)
        c = NC - 1 - step
        m_init, m_zero, m_compute = _mlstm_bwd_parts(
            dym_ref, hm_ref, qk_ref, cp_ref, mv_ref, gl_ref, mo_ref, mhg_ref, cs_ref, nm_ref,
            dp_ref.at[:, 0:GQ0], dc_ref, dgl_ref, dmhg_ref, dc_s, dn_s)
        g_init, g_zero, g_compute = _gla_bwd_parts(
            c, dyg_ref, ho_ref, gq_ref, gk_ref, gv_ref, gr_ref, sm_ref, a2_ref, b2_ref, ghg_ref, ss_ref,
            dp_ref.at[:, GQ0:GQ0 + GW], dga_ref, da2_ref, db2_ref, dghg_ref, ds_s)

        @pl.when(step == 0)
        def _():
            m_init()
            g_init()

        @pl.when(c < CH0)
        def _():
            m_zero()
            g_zero()

        @pl.when(c >= CH0)
        def _():
            m_compute()
            g_compute()

    def rows(w, cb):
        return pl.BlockSpec((LM, w), lambda c: (rev(c), cb))

    return pl.pallas_call(
        body, name="mix_bwd", grid=(NC,),
        in_specs=[rows(1024, 0), rows(1024, 0), rows(1024, 0), rows(1024, 0), rows(1024, MV0 // 1024), rows(128, 0),
                  rows(1024, MO0 // 1024), _const((1, 1024)),
                  pl.BlockSpec((1, NH, DV, DQK), lambda c: (rev(c), 0, 0, 0)),
                  pl.BlockSpec((1, NH, 8, DQK), lambda c: (rev(c), 0, 0, 0)),
                  rows(1024, 0), rows(1024, 0), rows(512, GQ0 // 512), rows(512, GK0 // 512),
                  rows(1024, GV0 // 1024), rows(1024, GR0 // 1024), rows(128, SM0 // 128),
                  _const((128, 512)), _const((1, 512)), _const((1, 1024)),
                  pl.BlockSpec((LM // L, NH, DQK, DV), lambda c: (rev(c), 0, 0, 0)),
                  pl.BlockSpec(memory_space=pl.ANY)],
        out_specs=[rows(GQ0 + GW, 0), rows(1024, 0), rows(128, 0), _const((1, 1024)),
                   rows(128, 0), _const((128, 512)), _const((1, 512)), _const((1, 1024))],
        out_shape=[jax.ShapeDtypeStruct((R, NP), MXU), jax.ShapeDtypeStruct((R, 1024), F32),
                   jax.ShapeDtypeStruct((R, 128), F32), jax.ShapeDtypeStruct((1, 1024), F32),
                   jax.ShapeDtypeStruct((R, 128), F32), jax.ShapeDtypeStruct((128, 512), F32),
                   jax.ShapeDtypeStruct((1, 512), F32), jax.ShapeDtypeStruct((1, 1024), F32)],
        scratch_shapes=[pltpu.VMEM((NH, DV, DQK), F32), pltpu.VMEM((NH, 8, DQK), F32),
                        pltpu.VMEM((NH, DQK, DV), F32)],
        input_output_aliases={21: 0},
        compiler_params=_cp(("arbitrary",)),
    )(dym, hm, qk, cpre, proj, gl, proj, m_head_g, cs, nm, dyg, hgl, proj, proj, proj, proj, proj, a2p, b2,
      g_head_g, ss, dproj)


def _to_row128(col):
    r = lax.broadcasted_iota(jnp.int32, (DQK, DQK), 0)
    c = lax.broadcasted_iota(jnp.int32, (DQK, DQK), 1)
    return jnp.sum(jnp.where(r == c, col, 0.0), axis=0, keepdims=True)


def local_step(x, target, meta, norm1_g, wp, conv_w, conv_b, m_gate_b, g_a2, g_a2_b, m_head_g, g_head_g,
               norm2_g, final_g, late_weights, send_early, send_wp, first_order=None):
    seq = x.shape[0]
    assert seq % TM == 0
    h0 = jnp.concatenate([jnp.zeros((NPADROWS, D), F32), meta, x], axis=0)
    gb_row = jnp.zeros((1, 128), F32).at[0, 0:8].set(m_gate_b.reshape(8))
    a2p = jnp.zeros((128, 512), F32).at[8:8 + RANK].set(g_a2)
    mhg = m_head_g.reshape(1, 1024)
    ghg = g_head_g.reshape(1, 1024)

    xn = rms_fwd(h0, norm1_g, "rms1_fwd")
    proj = matmul(xn, wp, "nn", "proj_fwd", tm=1536, order=first_order)
    cpre, qk, gl = prep_fwd(proj, conv_w, conv_b, gb_row)
    hm, ym, cs, nm, hgl, yg, ss = mix_fwd(qk, proj, gl, mhg, a2p, g_a2_b, ghg)
    w_bm, w_bg, w_out, w_gu, w_down = late_weights(ym)
    bm = matmul(ym, w_bm, "nn", "branch_m_fwd")
    bg = matmul(yg, w_bg, "nn", "branch_g_fwd")
    merged = merge_fwd(bm, bg, proj)
    h1 = matmul(merged, w_out, "nn", "out_fwd", add=h0)
    hn = rms_fwd(h1, norm2_g, "rms2_fwd")
    au = matmul(hn, w_gu, "nn", "ff_in_fwd", out_dtype=MXU, tm=1536, tn=1408)
    ff = swiglu_fwd(au)
    h2 = matmul(ff, w_down, "nn", "ff_down_fwd", add=h1, tm=768, tk=2816)
    dh2, loss, d_final_g = final_loss(h2, final_g.reshape(1, D), target)

    d_w_down = matmul(ff, dh2, "tn", "ff_down_wgrad", tm=1408, tk=1536)
    dff = matmul(dh2, w_down, "nt", "ff_down_dgrad", tm=1536, tn=1408)
    dau = swiglu_bwd(au, dff)
    d_w_gu = matmul(hn, dau, "tn", "ff_in_wgrad", tm=1024, tn=1408, tk=1536)
    dhn = matmul(dau, w_gu, "nt", "ff_in_dgrad", tm=1536, tk=1408)
    dh1, d_norm2_g = rms_bwd(dhn, h1, norm2_g, dh2, "rms2_bwd")

    d_w_out = matmul(merged, dh1, "tn", "out_wgrad", tm=1024, tk=1536)
    dmerged = matmul(dh1, w_out, "nt", "out_dgrad")
    dbm, dbg, dproj = merge_bwd(dmerged, bm, bg, proj)
    d_w_bm = matmul(ym, dbm, "tn", "branch_m_wgrad", tm=1024, tk=1536)
    d_w_bg = matmul(yg, dbg, "tn", "branch_g_wgrad", tm=1024, tk=1536)
    token = send_early(dict(w_branch_m=d_w_bm, w_branch_g=d_w_bg, w_out=d_w_out, w_gu=d_w_gu, w_ff_down=d_w_down))
    dym = matmul(dbm, w_bm, "nt", "branch_m_dgrad", order=token)
    dyg = matmul(dbg, w_bg, "nt", "branch_g_dgrad")
    dproj, dc, dgl, d_mhg, dga, d_a2p, d_a2b, d_ghg = mix_bwd(
        dym, hm, qk, cpre, proj, gl, mhg, cs, nm, dyg, hgl, a2p, g_a2_b, ghg, ss, dproj)
    dproj, d_conv = conv_bwd(dc, proj, conv_w, dproj)
    dproj, d_gb = small_bwd(dgl, dga, proj, gb_row, dproj)
    d_wp = matmul(xn, dproj, "tn", "proj_wgrad", tm=1024, tn=1664, tk=1536)
    token = send_wp(d_wp)
    dxn = matmul(dproj, wp, "nt", "proj_dgrad", tm=1536, tk=1664, order=token)
    grad_x, d_meta, d_norm1_g = rms_bwd_input(dxn, h0, norm1_g, dh1, "rms1_bwd")

    grads = dict(
        meta_tokens=d_meta, norm1_g=d_norm1_g, conv_w=d_conv[0:4], conv_b=d_conv[4:5], m_gate_b=d_gb[0, 0:8].reshape(1, 2, 4),
        g_a2=d_a2p[8:8 + RANK], g_a2_b=d_a2b, m_head_g=d_mhg.reshape(NH, DV), g_head_g=d_ghg.reshape(NH, DV),
        norm2_g=d_norm2_g, final_g=d_final_g)
    return loss, grad_x, grads


_SEGS = [(0, 1024, QK0), (1024, 2048, MV0), (2048, 2056, SM0), (2056, 3080, MO0), (3080, 5128, GQ0),
         (5128, 5144, SM0 + 8), (5144, 6168, GR0), (6168, 8216, GM0)]
SHARD_W = NPROJ // NDEV


def regroup_cols(w8):
    parts = []
    for lo, hi, _ in sorted(_SEGS, key=lambda s: s[2]):
        while lo < hi:
            j = lo // SHARD_W
            end = min(hi, (j + 1) * SHARD_W)
            parts.append(w8[j, :, lo - j * SHARD_W:end - j * SHARD_W])
            lo = end
    parts.append(jnp.zeros((w8.shape[1], NP - NPROJ), w8.dtype))
    return jnp.concatenate(parts, axis=1)


def ungroup_cols(g):
    blocks = []
    for j in range(NDEV):
        lo, hi = j * SHARD_W, (j + 1) * SHARD_W
        parts = []
        for s_lo, s_hi, s_at in _SEGS:
            a, b = max(lo, s_lo), min(hi, s_hi)
            if a < b:
                parts.append(g[:, s_at + a - s_lo:s_at + b - s_lo])
        blocks.append(jnp.concatenate(parts, axis=1))
    return jnp.stack(blocks)


def col_blocks(g):
    r, c8 = g.shape
    return jnp.transpose(g.reshape(r, NDEV, c8 // NDEV), (1, 0, 2))


def from_col_blocks(g8):
    n, r, c = g8.shape
    return jnp.transpose(g8, (1, 0, 2)).reshape(r, n * c)


_MESHID = pl.DeviceIdType.MESH
_RELS = [(0, 0, 1), (1, 0, 0), (0, 1, 0), (1, 1, 0), (1, 0, 1), (0, 1, 1), (1, 1, 1)]


def _flip(v, bit):
    return 1 - v if bit else v


def all_gather(arrs, name):
    n = len(arrs)

    def body(*refs):
        ins, outs = refs[:n], refs[n:2 * n]
        send_sems, recv_sems, local_sems = refs[2 * n:]
        x, y, c = lax.axis_index("x"), lax.axis_index("y"), lax.axis_index("c")
        me, sibling = (x, y, c), (x, y, 1 - c)
        chips = [(1 - x, y), (x, 1 - y), (1 - x, 1 - y)]

        def slot(p):
            return 4 * p[0] + 2 * p[1] + p[2]

        def copy(a, k, block, to, src=None):
            dst = outs[a].at[slot(block)]
            return pltpu.make_async_remote_copy(
                src_ref=dst if src is None else src, dst_ref=dst,
                send_sem=send_sems.at[a, k], recv_sem=recv_sems.at[a, k],
                device_id=to, device_id_type=_MESHID)

        mine = [pltpu.make_async_copy(ins[a], outs[a].at[slot(me)], local_sems.at[a]) for a in range(n)]
        for cp in mine:
            cp.start()
        first = []
        for a in range(n):
            first.append(copy(a, 0, me, sibling, src=ins[a]))
            first += [copy(a, 1 + j, me, (*chip, c), src=ins[a]) for j, chip in enumerate(chips)]
        for cp in first:
            cp.start()
        passed = []
        for j, chip in enumerate(chips):
            for a in range(n):
                copy(a, 1 + j, (*chip, c), me).wait_recv()
                fwd = copy(a, 4 + j, (*chip, c), sibling)
                fwd.start()
                passed.append(fwd)
        for a in range(n):
            copy(a, 0, sibling, me).wait_recv()
            for j, chip in enumerate(chips):
                copy(a, 4 + j, (*chip, 1 - c), me).wait_recv()
        for cp in first + passed:
            cp.wait_send()
        for cp in mine:
            cp.wait()

    anyspec = pl.BlockSpec(memory_space=pl.ANY)
    return pl.pallas_call(
        body, name=name,
        in_specs=[anyspec] * n, out_specs=[anyspec] * n,
        out_shape=[jax.ShapeDtypeStruct((NDEV,) + a.shape, a.dtype) for a in arrs],
        scratch_shapes=[pltpu.SemaphoreType.DMA((n, 7)), pltpu.SemaphoreType.DMA((n, 7)),
                        pltpu.SemaphoreType.DMA((n,))],
    )(*arrs)


def exchange(blocks, rep, name):
    n = len(blocks)

    def body(*refs):
        b_refs, r_ref = refs[:n], refs[n]
        ob_refs, or_ref = refs[n + 1:2 * n + 1], refs[2 * n + 1]
        send_sems, recv_sems, local_sems = refs[2 * n + 2:]
        x, y, c = lax.axis_index("x"), lax.axis_index("y"), lax.axis_index("c")
        me = 4 * x + 2 * y + c

        def pairs(src_slot, dst_slot):
            return [(b_refs[a].at[src_slot], ob_refs[a].at[dst_slot]) for a in range(n)] + [(r_ref, or_ref.at[dst_slot])]

        loc = [pltpu.make_async_copy(s, d, local_sems.at[a]) for a, (s, d) in enumerate(pairs(me, me))]
        for cp in loc:
            cp.start()
        sends = []
        for k, (fx, fy, fc) in enumerate(_RELS):
            peer = (_flip(x, fx), _flip(y, fy), _flip(c, fc))
            pid = 4 * peer[0] + 2 * peer[1] + peer[2]
            for a, (s, d) in enumerate(pairs(pid, me)):
                sends.append(pltpu.make_async_remote_copy(
                    src_ref=s, dst_ref=d, send_sem=send_sems.at[a, k], recv_sem=recv_sems.at[a, k],
                    device_id=peer, device_id_type=_MESHID))
        for cp in sends:
            cp.start()
        for k, (fx, fy, fc) in enumerate(_RELS):
            peer = (_flip(x, fx), _flip(y, fy), _flip(c, fc))
            pid = 4 * peer[0] + 2 * peer[1] + peer[2]
            for a, (s, d) in enumerate(pairs(pid, pid)):
                pltpu.make_async_remote_copy(
                    src_ref=s, dst_ref=d, send_sem=send_sems.at[a, k], recv_sem=recv_sems.at[a, k],
                    device_id=peer, device_id_type=_MESHID).wait_recv()
        for cp in sends:
            cp.wait_send()
        for cp in loc:
            cp.wait()

    anyspec = pl.BlockSpec(memory_space=pl.ANY)
    return pl.pallas_call(
        body, name=name,
        in_specs=[anyspec] * (n + 1), out_specs=[anyspec] * (n + 1),
        out_shape=[jax.ShapeDtypeStruct(b.shape, b.dtype) for b in blocks]
        + [jax.ShapeDtypeStruct((NDEV,) + rep.shape, rep.dtype)],
        scratch_shapes=[pltpu.SemaphoreType.DMA((n + 1, 7)), pltpu.SemaphoreType.DMA((n + 1, 7)),
                        pltpu.SemaphoreType.DMA((n + 1,))],
    )(*blocks, rep)


_HBM = pl.BlockSpec(memory_space=pltpu.HBM)
_SEM = pl.BlockSpec(memory_space=pltpu.SEMAPHORE)
_EFFECT = pltpu.SideEffectType.DATAFLOW_SIDE_EFFECTING


def _peer_ids():
    x, y, c = lax.axis_index("x"), lax.axis_index("y"), lax.axis_index("c")
    peers = []
    for fx, fy, fc in _RELS:
        p = (_flip(x, fx), _flip(y, fy), _flip(c, fc))
        peers.append((p, 4 * p[0] + 2 * p[1] + p[2]))
    return 4 * x + 2 * y + c, peers


def _split_copy(src, land, a, k, peer, src_slot, dst_slot, send_sems, recv_sems):
    return pltpu.make_async_remote_copy(
        src_ref=src if src_slot is None else src.at[src_slot], dst_ref=land.at[dst_slot],
        send_sem=send_sems.at[7 * a + k], recv_sem=recv_sems.at[7 * a + k], device_id=peer, device_id_type=_MESHID)


def _own_copy(src, land, a, n, me, per_peer, send_sems):
    return pltpu.make_async_copy(src.at[me] if per_peer else src, land.at[me], send_sems.at[7 * n + a])


def send_start(srcs, per_peer, order, name):
    n = len(srcs)
    lands = [lax.empty((NDEV,) + (s.shape[1:] if per_peer else s.shape), s.dtype) for s in srcs]

    def body(*refs):
        src_refs, land_refs = refs[1:1 + n], refs[1 + n:1 + 2 * n]
        send_sems, recv_sems = refs[1 + 2 * n], refs[2 + 2 * n]
        token = refs[3 + 4 * n]
        me, peers = _peer_ids()
        for a in range(n):
            _own_copy(src_refs[a], land_refs[a], a, n, me, per_peer, send_sems).start()
        for a in range(n):
            for k, (peer, pid) in enumerate(peers):
                _split_copy(src_refs[a], land_refs[a], a, k, peer, pid if per_peer else None, me,
                            send_sems, recv_sems).start()
        token[...] = jnp.zeros_like(token)

    outs = pl.pallas_call(
        body, name=name,
        in_specs=[pl.BlockSpec(memory_space=pl.ANY)] + [_HBM] * (2 * n),
        out_shape=(pltpu.SemaphoreType.DMA((8 * n,)), pltpu.SemaphoreType.DMA((7 * n,)),
                   *[pltpu.HBM(s.shape, s.dtype) for s in srcs], *[pltpu.HBM(l.shape, l.dtype) for l in lands],
                   jax.ShapeDtypeStruct((8, 128), F32)),
        out_specs=(_SEM, _SEM, *[_HBM] * (2 * n), pl.BlockSpec(memory_space=pltpu.VMEM)),
        input_output_aliases={1 + i: 2 + i for i in range(2 * n)},
        compiler_params=pltpu.CompilerParams(has_side_effects=_EFFECT),
    )(order, *[pltpu.with_memory_space_constraint(s, pltpu.HBM) for s in srcs],
      *[pltpu.with_memory_space_constraint(l, pltpu.HBM) for l in lands])
    return (n, per_peer, outs[0], outs[1], outs[2:2 + n], outs[2 + n:2 + 2 * n]), outs[2 + 2 * n]


def send_wait(handle, after, name):
    n, per_peer, send_sems, recv_sems, src_thru, land_thru = handle

    def body(*refs):
        src_refs, land_refs = refs[:n], refs[n:2 * n]
        s_sems, r_sems = refs[2 * n], refs[2 * n + 1]
        me, peers = _peer_ids()
        for a in range(n):
            _own_copy(src_refs[a], land_refs[a], a, n, me, per_peer, s_sems).wait()
            for k, (peer, pid) in enumerate(peers):
                cp = _split_copy(src_refs[a], land_refs[a], a, k, peer, pid if per_peer else None, pid, s_sems, r_sems)
                cp.wait_send()
                cp.wait_recv()

    outs = pl.pallas_call(
        body, name=name,
        in_specs=[_HBM] * (2 * n) + [_SEM, _SEM, pl.BlockSpec(memory_space=pl.ANY)],
        out_shape=tuple(pltpu.HBM(t.shape, t.dtype) for t in (*src_thru, *land_thru)),
        out_specs=tuple([_HBM] * (2 * n)),
        input_output_aliases={i: i for i in range(2 * n)},
        compiler_params=pltpu.CompilerParams(has_side_effects=_EFFECT),
    )(*src_thru, *land_thru, send_sems, recv_sems, after)
    return list(outs[n:2 * n])


def adamw(parts, w, m, v, name, tr):
    npart, r, c = parts.shape
    c1 = 1.0 - ADAM_B1 ** ADAM_STEP
    c2 = 1.0 - ADAM_B2 ** ADAM_STEP

    def body(p_ref, w_ref, m_ref, v_ref, g_ref, d_ref, nm_ref, nv_ref):
        g = p_ref[0].astype(F32)
        for j in range(1, npart):
            g = g + p_ref[j].astype(F32)
        mn = ADAM_B1 * m_ref[...] + (1.0 - ADAM_B1) * g
        vn = ADAM_B2 * v_ref[...] + (1.0 - ADAM_B2) * (g * g)
        g_ref[...] = g
        nm_ref[...] = mn
        nv_ref[...] = vn
        d_ref[...] = -ADAM_LR * ((mn / c1) / (jnp.sqrt(vn / c2) + ADAM_EPS) + ADAM_WD * w_ref[...])

    spec = _rb(tr, c, 0)
    return pl.pallas_call(
        body, name=name, grid=(r // tr,),
        in_specs=[pl.BlockSpec((npart, tr, c), lambda i: (0, i, 0)), spec, spec, spec],
        out_specs=[spec] * 4, out_shape=[jax.ShapeDtypeStruct((r, c), F32)] * 4,
        compiler_params=_cp(("parallel",)),
    )(parts, w, m, v)


def sum_parts(parts, name, tc):
    npart, r, c = parts.shape

    def body(p_ref, o_ref):
        g = p_ref[0].astype(F32)
        for j in range(1, npart):
            g = g + p_ref[j].astype(F32)
        o_ref[...] = g

    return pl.pallas_call(
        body, name=name, grid=(c // tc,),
        in_specs=[pl.BlockSpec((npart, r, tc), lambda i: (0, 0, i))],
        out_specs=pl.BlockSpec((r, tc), lambda i: (0, i)),
        out_shape=jax.ShapeDtypeStruct((r, c), F32),
        compiler_params=_cp(("parallel",)),
    )(parts)


TINY = [("meta_tokens", (16, 1024)), ("conv_w", (4, 1024)), ("g_a2", (16, 512)), ("m_head_g", (4, 256)),
        ("g_head_g", (4, 256))]
REPL = [("norm1_g", (1, 1024)), ("conv_b", (1, 1024)), ("m_gate_b", (1, 2, 4)), ("g_a2_b", (1, 512)),
        ("norm2_g", (1, 1024)), ("final_g", (1024,))]
TINY_SIZE = 16 * 1024 + 4 * 1024 + 16 * 512 + 2 * 4 * 256
REPL_SIZE = 1024 + 1024 + 8 + 512 + 1024 + 1024
ROWS_GATHER = 8
ROWS_REP = 40
ROWS_OWN = 16


def pack_rows(vecs, rows):
    flat = jnp.concatenate([v.reshape(-1) for v in vecs])
    return jnp.pad(flat, (0, rows * 1024 - flat.shape[0])).reshape(rows, 1024)


def unpack_rows(packed, shapes):
    flat = packed.reshape(-1)
    out, off = [], 0
    for s in shapes:
        n = 1
        for d in s:
            n *= d
        out.append(flat[off:off + n].reshape(s))
        off += n
    return out


def kernel(x, meta_tokens, norm1_g, w_in, conv_w, conv_b, m_gate_b, g_a2, g_a2_b, m_head_g, g_head_g, w_branch_m, w_branch_g, w_out, norm2_g, w_ff_gate, w_ff_up, w_ff_down, final_g, loss_target, m_meta_tokens, m_norm1_g, m_w_in, m_conv_w, m_conv_b, m_m_gate_b, m_g_a2, m_g_a2_b, m_m_head_g, m_g_head_g, m_w_branch_m, m_w_branch_g, m_w_out, m_norm2_g, m_w_ff_gate, m_w_ff_up, m_w_ff_down, m_final_g, v_meta_tokens, v_norm1_g, v_w_in, v_conv_w, v_conv_b, v_m_gate_b, v_g_a2, v_g_a2_b, v_m_head_g, v_g_head_g, v_w_branch_m, v_w_branch_g, v_w_out, v_norm2_g, v_w_ff_gate, v_w_ff_up, v_w_ff_down, v_final_g):
    w_sh = dict(meta_tokens=meta_tokens, w_in=w_in[0], conv_w=conv_w[0], g_a2=g_a2[0], m_head_g=m_head_g[0],
                g_head_g=g_head_g[0], w_branch_m=w_branch_m[0], w_branch_g=w_branch_g[0], w_out=w_out[0],
                w_ff_gate=w_ff_gate[0], w_ff_up=w_ff_up[0], w_ff_down=w_ff_down[0])
    m_sh = dict(meta_tokens=m_meta_tokens, w_in=m_w_in[0], conv_w=m_conv_w[0], g_a2=m_g_a2[0],
                m_head_g=m_m_head_g[0], g_head_g=m_g_head_g[0], w_branch_m=m_w_branch_m[0],
                w_branch_g=m_w_branch_g[0], w_out=m_w_out[0], w_ff_gate=m_w_ff_gate[0], w_ff_up=m_w_ff_up[0],
                w_ff_down=m_w_ff_down[0])
    v_sh = dict(meta_tokens=v_meta_tokens, w_in=v_w_in[0], conv_w=v_conv_w[0], g_a2=v_g_a2[0],
                m_head_g=v_m_head_g[0], g_head_g=v_g_head_g[0], w_branch_m=v_w_branch_m[0],
                w_branch_g=v_w_branch_g[0], w_out=v_w_out[0], w_ff_gate=v_w_ff_gate[0], w_ff_up=v_w_ff_up[0],
                w_ff_down=v_w_ff_down[0])
    w_rep = dict(norm1_g=norm1_g, conv_b=conv_b, m_gate_b=m_gate_b, g_a2_b=g_a2_b, norm2_g=norm2_g, final_g=final_g)
    m_rep = dict(norm1_g=m_norm1_g, conv_b=m_conv_b, m_gate_b=m_m_gate_b, g_a2_b=m_g_a2_b, norm2_g=m_norm2_g,
                 final_g=m_final_g)
    v_rep = dict(norm1_g=v_norm1_g, conv_b=v_conv_b, m_gate_b=v_m_gate_b, g_a2_b=v_g_a2_b, norm2_g=v_norm2_g,
                 final_g=v_final_g)
    dev = 4 * lax.axis_index("x") + 2 * lax.axis_index("y") + lax.axis_index("c")
    tiny_names = [n for n, _ in TINY]
    repl_names = [n for n, _ in REPL]
    tiny_shard_shapes = [(s[0], s[1] // NDEV) for _, s in TINY]

    in8, tiny8 = all_gather([w_sh["w_in"].astype(MXU), pack_rows([w_sh[n] for n in tiny_names], ROWS_GATHER)],
                            "param_all_gather")
    late_names = ["w_branch_m", "w_branch_g", "w_out", "w_ff_gate", "w_ff_up", "w_ff_down"]
    late, first_order = send_start([w_sh[n].astype(MXU) for n in late_names], False, tiny8, "late_weights_start")
    wp = regroup_cols(in8)
    handles = {}

    def late_weights(after):
        bm8, bg8, out8, ffg8, ffu8, ffd8 = send_wait(late, after, "late_weights_wait")
        w_gu = jnp.concatenate([from_col_blocks(ffg8), from_col_blocks(ffu8)], axis=1)
        return bm8.reshape(D, D), bg8.reshape(D, D), out8.reshape(D, D), w_gu, ffd8.reshape(DFF, D)

    def send_early(g):
        blocks = [g["w_branch_m"].reshape(NDEV, D // NDEV, D).astype(WIRE),
                  g["w_branch_g"].reshape(NDEV, D // NDEV, D).astype(WIRE),
                  g["w_out"].reshape(NDEV, D // NDEV, D).astype(WIRE),
                  col_blocks(g["w_gu"][:, :DFF]).astype(WIRE), col_blocks(g["w_gu"][:, DFF:]).astype(WIRE),
                  g["w_ff_down"].reshape(NDEV, DFF // NDEV, D).astype(WIRE)]
        handles["early"], token = send_start(blocks, True, blocks[0], "early_grads_start")
        return token

    def send_wp(d_wp):
        blocks = [ungroup_cols(d_wp).astype(WIRE)]
        handles["wp"], token = send_start(blocks, True, blocks[0], "proj_grads_start")
        return token

    tiny_full = {}
    for j in range(NDEV):
        for name, blk in zip(tiny_names, unpack_rows(tiny8[j], tiny_shard_shapes)):
            tiny_full.setdefault(name, []).append(blk)
    tiny_full = {n: jnp.concatenate(v, axis=1) for n, v in tiny_full.items()}

    loss, grad_x, g = local_step(
        x[0], loss_target[0], tiny_full["meta_tokens"], norm1_g, wp, tiny_full["conv_w"], conv_b, m_gate_b[0],
        tiny_full["g_a2"], g_a2_b, tiny_full["m_head_g"], tiny_full["g_head_g"], norm2_g, final_g,
        late_weights, send_early, send_wp, first_order)

    rep = pack_rows([g[n] for n in tiny_names + repl_names] + [loss[0, 0:1]], ROWS_REP)
    (got_rep,) = exchange([], rep, "small_grad_exchange")
    got_early = send_wait(handles["early"], got_rep, "early_grads_wait")
    (got_wp,) = send_wait(handles["wp"], got_rep, "proj_grads_wait")

    result = {}

    def update(name, parts, tr):
        outs = adamw(parts, w_sh[name], m_sh[name], v_sh[name], "adamw_" + name, tr)
        for kind, arr in zip(("grad", "delta", "new_m", "new_v"), outs):
            result[kind, name] = arr[None]

    update("w_in", got_wp, 128)
    update("w_branch_m", got_early[0], 128)
    update("w_branch_g", got_early[1], 128)
    update("w_out", got_early[2], 128)
    update("w_ff_gate", got_early[3], 256)
    update("w_ff_up", got_early[4], 256)
    update("w_ff_down", got_early[5], DFF // NDEV)

    rep_sum = sum_parts(got_rep, "sum_small", 1024)
    rep_g = unpack_rows(rep_sum, [s for _, s in TINY] + [s for _, s in REPL] + [(1,)])
    own_g = [lax.dynamic_slice_in_dim(gf, dev * ss[1], ss[1], axis=1) for gf, ss in zip(rep_g, tiny_shard_shapes)]
    own_g += rep_g[len(TINY):len(TINY) + len(REPL)]
    w_all = {**w_sh, **w_rep}
    m_all = {**m_sh, **m_rep}
    v_all = {**v_sh, **v_rep}
    names = tiny_names + repl_names
    outs = adamw(pack_rows(own_g, ROWS_OWN)[None], pack_rows([w_all[n] for n in names], ROWS_OWN),
                 pack_rows([m_all[n] for n in names], ROWS_OWN), pack_rows([v_all[n] for n in names], ROWS_OWN),
                 "adamw_small", ROWS_OWN)
    shapes = tiny_shard_shapes + [s for _, s in REPL]
    for kind, packed in zip(("grad", "delta", "new_m", "new_v"), outs):
        for name, arr in zip(names, unpack_rows(packed, shapes)):
            result[kind, name] = arr[None] if name in tiny_names and name != "meta_tokens" else arr
    loss_total = rep_g[-1][0]
    order = ["meta_tokens", "norm1_g", "w_in", "conv_w", "conv_b", "m_gate_b", "g_a2", "g_a2_b", "m_head_g", "g_head_g",
             "w_branch_m", "w_branch_g", "w_out", "norm2_g", "w_ff_gate", "w_ff_up", "w_ff_down", "final_g"]
    return (loss_total, grad_x[None], *[result[kind, n] for kind in ("grad", "delta", "new_m", "new_v") for n in order])
```

```python
import functools

import jax
import jax.numpy as jnp
from jax import lax
from jax.experimental import pallas as pl
from jax.experimental.pallas import tpu as pltpu

F32 = jnp.float32
MXU = jnp.bfloat16
WIRE = jnp.bfloat16

D = 1024
NH = 4
DV = 256
DQK = 128
L = 64
NMETA = 16
PADR = 512
LM = 128
CH0 = PADR // LM - 1
NPADROWS = PADR - NMETA
RANK = 16
DFF = 2816
EPS = 1e-6
TAU = 16.0
QSCALE = DQK ** -0.5
NEG = -1e30
NDEV = 8

MV0, MO0, GQ0, GK0, GV0, GR0, QK0, GM0, GG0, SM0 = 0, 1024, 2048, 2560, 3072, 4096, 5120, 6144, 7168, 8192
NP = 8320
NPROJ = 8216

ADAM_LR, ADAM_B1, ADAM_B2, ADAM_EPS, ADAM_WD, ADAM_STEP = 0.001, 0.9, 0.999, 1e-08, 0.01, 10

VMEM_LIMIT = 56 * 1024 * 1024
TM = 512


def _cp(sem):
    return pltpu.CompilerParams(dimension_semantics=sem, vmem_limit_bytes=VMEM_LIMIT)


def _sigmoid(x):
    return 1.0 / (1.0 + jnp.exp(-x))


def _log_sigmoid(x):
    return jnp.minimum(x, 0.0) - jnp.log1p(jnp.exp(-jnp.abs(x)))


def _dot(a, b, ca, cb):
    return lax.dot_general(a.astype(MXU), b.astype(MXU), (((ca,), (cb,)), ((), ())), preferred_element_type=F32)


def _dot_exact(a, b):
    return lax.dot_general(a, b, (((1,), (0,)), ((), ())), precision=lax.Precision.HIGHEST,
                           preferred_element_type=F32)


def _rb(tm, w, cb):
    return pl.BlockSpec((tm, w), lambda i: (i, cb))


def _const(shape):
    nd = len(shape)
    return pl.BlockSpec(shape, lambda i: (0,) * nd)


def _pick(n, target):
    if n <= target:
        return n
    best = None
    for t in range(128, target + 1, 128):
        if n % t == 0:
            best = t
    assert best is not None, (n, target)
    return best


def matmul(a, b, mode, name, add=None, out_dtype=F32, tm=512, tn=1664, tk=1024, order=None):
    if mode == "nn":
        (M, K), (K2, N) = a.shape, b.shape
    elif mode == "nt":
        (M, K), (N, K2) = a.shape, b.shape
    else:
        (K, M), (K2, N) = a.shape, b.shape
    assert K == K2, (a.shape, b.shape, mode)
    tm, tn, tk = _pick(M, tm), _pick(N, tn), _pick(K, tk)
    nk = K // tk
    assert nk == 1 or out_dtype == F32
    ca, cb = {"nn": (1, 0), "nt": (1, 1), "tn": (0, 0)}[mode]
    a_spec = {"nn": pl.BlockSpec((tm, tk), lambda j, i, k: (i, k)),
              "nt": pl.BlockSpec((tm, tk), lambda j, i, k: (i, k)),
              "tn": pl.BlockSpec((tk, tm), lambda j, i, k: (k, i))}[mode]
    b_spec = {"nn": pl.BlockSpec((tk, tn), lambda j, i, k: (k, j)),
              "nt": pl.BlockSpec((tn, tk), lambda j, i, k: (j, k)),
              "tn": pl.BlockSpec((tk, tn), lambda j, i, k: (k, j))}[mode]
    o_spec = pl.BlockSpec((tm, tn), lambda j, i, k: (i, j))
    has_add = add is not None

    def body(*refs):
        if order is not None:
            refs = refs[:-2] + refs[-1:]
        if has_add:
            a_ref, b_ref, add_ref, o_ref = refs
        else:
            a_ref, b_ref, o_ref = refs
            add_ref = None
        part = _dot(a_ref[...], b_ref[...], ca, cb)
        if nk == 1:
            if has_add:
                part = part + add_ref[...]
            o_ref[...] = part.astype(o_ref.dtype)
            return
        k = pl.program_id(2)

        @pl.when(k == 0)
        def _():
            o_ref[...] = part + add_ref[...] if has_add else part

        @pl.when(k > 0)
        def _():
            o_ref[...] += part

    in_specs = [a_spec, b_spec] + ([o_spec] if has_add else [])
    args = (a, b) + ((add,) if has_add else ())
    if order is not None:
        in_specs.append(pl.BlockSpec(order.shape, lambda j, i, k: (0, 0)))
        args += (order,)
    return pl.pallas_call(
        body, name=name, grid=(N // tn, M // tm, nk),
        in_specs=in_specs, out_specs=o_spec,
        out_shape=jax.ShapeDtypeStruct((M, N), out_dtype),
        compiler_params=_cp(("parallel", "parallel", "arbitrary")),
    )(*args)


def rms_fwd(x, g, name):
    R = x.shape[0]

    def body(x_ref, g_ref, y_ref):
        xv = x_ref[...]
        r = lax.rsqrt(jnp.mean(xv * xv, axis=-1, keepdims=True) + EPS)
        y_ref[...] = (xv * r * g_ref[...]).astype(y_ref.dtype)

    return pl.pallas_call(
        body, name=name, grid=(R // TM,),
        in_specs=[_rb(TM, D, 0), _const((1, D))], out_specs=_rb(TM, D, 0),
        out_shape=jax.ShapeDtypeStruct((R, D), MXU), compiler_params=_cp(("parallel",)),
    )(x, g)


def rms_bwd(dy, x, g, dres, name):
    R = x.shape[0]

    def body(dy_ref, x_ref, g_ref, dres_ref, dx_ref, dg_ref):
        i = pl.program_id(0)
        xv, dyv = x_ref[...], dy_ref[...]
        r = lax.rsqrt(jnp.mean(xv * xv, axis=-1, keepdims=True) + EPS)
        dyg = dyv * g_ref[...]
        dx_ref[...] = dres_ref[...] + r * dyg - xv * (r * r * r * jnp.mean(dyg * xv, axis=-1, keepdims=True))
        part = jnp.sum(dyv * xv * r, axis=0, keepdims=True)

        @pl.when(i == 0)
        def _():
            dg_ref[...] = part

        @pl.when(i > 0)
        def _():
            dg_ref[...] += part

    return pl.pallas_call(
        body, name=name, grid=(R // TM,),
        in_specs=[_rb(TM, D, 0), _rb(TM, D, 0), _const((1, D)), _rb(TM, D, 0)],
        out_specs=[_rb(TM, D, 0), _const((1, D))],
        out_shape=[jax.ShapeDtypeStruct((R, D), F32), jax.ShapeDtypeStruct((1, D), F32)],
        compiler_params=_cp(("arbitrary",)),
    )(dy, x, g, dres)


def rms_bwd_input(dy, x, g, dres, name):
    R = x.shape[0]
    assert PADR == TM

    def body(dy_ref, x_ref, g_ref, dres_ref, dx_ref, dmeta_ref, dg_ref):
        i = pl.program_id(0)
        xv, dyv = x_ref[...], dy_ref[...]
        r = lax.rsqrt(jnp.mean(xv * xv, axis=-1, keepdims=True) + EPS)
        dyg = dyv * g_ref[...]
        dx = dres_ref[...] + r * dyg - xv * (r * r * r * jnp.mean(dyg * xv, axis=-1, keepdims=True))
        dx_ref[...] = dx
        part = jnp.sum(dyv * xv * r, axis=0, keepdims=True)

        @pl.when(i == 0)
        def _():
            dg_ref[...] = part
            dmeta_ref[...] = dx[NPADROWS:PADR]

        @pl.when(i > 0)
        def _():
            dg_ref[...] += part

    return pl.pallas_call(
        body, name=name, grid=(R // TM,),
        in_specs=[_rb(TM, D, 0), _rb(TM, D, 0), _const((1, D)), _rb(TM, D, 0)],
        out_specs=[pl.BlockSpec((TM, D), lambda i: (jnp.maximum(i - 1, 0), 0)), _const((NMETA, D)), _const((1, D))],
        out_shape=[jax.ShapeDtypeStruct((R - PADR, D), F32), jax.ShapeDtypeStruct((NMETA, D), F32),
                   jax.ShapeDtypeStruct((1, D), F32)],
        compiler_params=_cp(("arbitrary",)),
    )(dy, x, g, dres)


def _shift_down(cur, prev8, s):
    tm = cur.shape[0]
    rolled = pltpu.roll(cur, s, 0)
    rows8 = lax.broadcasted_iota(jnp.int32, (8, cur.shape[1]), 0)
    head = jnp.where(rows8 < s, pltpu.roll(prev8, s, 0), rolled[0:8])
    return jnp.concatenate([head, rolled[8:tm]], axis=0)


def _shift_up(cur, next8, s):
    tm = cur.shape[0]
    rolled = pltpu.roll(cur, tm - s, 0)
    rows8 = lax.broadcasted_iota(jnp.int32, (8, cur.shape[1]), 0)
    tail = jnp.where(rows8 >= 8 - s, pltpu.roll(next8, 8 - s, 0), rolled[tm - 8:tm])
    return jnp.concatenate([rolled[0:tm - 8], tail], axis=0)


def prep_fwd(proj, conv_w, conv_b, gb_row):
    R = proj.shape[0]
    t8 = TM // 8

    def body(x_ref, halo_ref, sm_ref, w_ref, b_ref, gb_ref, c_ref, qk_ref, gl_ref):
        i = pl.program_id(0)
        x = x_ref[...]
        halo = halo_ref[...]
        w = w_ref[...]
        c = x * w[3:4, :] + b_ref[...]
        for s in (1, 2, 3):
            c = c + _shift_down(x, halo, s) * w[3 - s:4 - s, :]
        c_ref[...] = c
        qk_ref[...] = c * _sigmoid(c)
        z = sm_ref[...] + gb_ref[...]
        lane = lax.broadcasted_iota(jnp.int32, z.shape, 1)
        row = lax.broadcasted_iota(jnp.int32, z.shape, 0) + i * TM
        valid = row >= NPADROWS
        logi = jnp.where(valid, z, NEG)
        logf = jnp.where(valid, _log_sigmoid(z), 0.0)
        gl_ref[...] = jnp.where(lane < 4, logi, jnp.where(lane < 8, logf, 0.0))

    return pl.pallas_call(
        body, name="prep_fwd", grid=(R // TM,),
        in_specs=[_rb(TM, 1024, QK0 // 1024),
                  pl.BlockSpec((8, 1024), lambda i: (jnp.maximum(i * t8 - 1, 0), QK0 // 1024)),
                  _rb(TM, 128, SM0 // 128), _const((4, 1024)), _const((1, 1024)), _const((1, 128))],
        out_specs=[_rb(TM, 1024, 0), _rb(TM, 1024, 0), _rb(TM, 128, 0)],
        out_shape=[jax.ShapeDtypeStruct((R, 1024), F32), jax.ShapeDtypeStruct((R, 1024), F32),
                   jax.ShapeDtypeStruct((R, 128), F32)],
        compiler_params=_cp(("parallel",)),
    )(proj, proj, proj, conv_w, conv_b, gb_row)


def merge_fwd(bm, bg, proj):
    R = bm.shape[0]

    def body(bm_ref, bg_ref, gm_ref, gg_ref, o_ref):
        o_ref[...] = (_sigmoid(gm_ref[...]) * bm_ref[...] + _sigmoid(gg_ref[...]) * bg_ref[...]).astype(o_ref.dtype)

    return pl.pallas_call(
        body, name="merge_fwd", grid=(R // TM,),
        in_specs=[_rb(TM, D, 0), _rb(TM, D, 0), _rb(TM, D, GM0 // D), _rb(TM, D, GG0 // D)],
        out_specs=_rb(TM, D, 0), out_shape=jax.ShapeDtypeStruct((R, D), MXU),
        compiler_params=_cp(("parallel",)),
    )(bm, bg, proj, proj)


def merge_bwd(dmerged, bm, bg, proj):
    R = bm.shape[0]

    def body(dm_ref, bm_ref, bg_ref, gm_ref, gg_ref, dbm_ref, dbg_ref, dp_ref):
        dm = dm_ref[...]
        sm, sg = _sigmoid(gm_ref[...]), _sigmoid(gg_ref[...])
        dbm_ref[...] = (dm * sm).astype(dbm_ref.dtype)
        dbg_ref[...] = (dm * sg).astype(dbg_ref.dtype)
        dp_ref[:, 0:D] = (dm * bm_ref[...] * sm * (1.0 - sm)).astype(dp_ref.dtype)
        dp_ref[:, D:2 * D] = (dm * bg_ref[...] * sg * (1.0 - sg)).astype(dp_ref.dtype)

    return pl.pallas_call(
        body, name="merge_bwd", grid=(R // TM,),
        in_specs=[_rb(TM, D, 0), _rb(TM, D, 0), _rb(TM, D, 0), _rb(TM, D, GM0 // D), _rb(TM, D, GG0 // D)],
        out_specs=[_rb(TM, D, 0), _rb(TM, D, 0), _rb(TM, 2 * D, GM0 // (2 * D))],
        out_shape=[jax.ShapeDtypeStruct((R, D), MXU), jax.ShapeDtypeStruct((R, D), MXU),
                   jax.ShapeDtypeStruct((R, NP), MXU)],
        compiler_params=_cp(("parallel",)),
    )(dmerged, bm, bg, proj, proj)


TF = DFF // 2
TMF = 768


def interleave_gu(gate, up):
    return jnp.concatenate([gate[:, :TF], up[:, :TF], gate[:, TF:], up[:, TF:]], axis=1)


def split_gu(gu):
    return (jnp.concatenate([gu[:, 0:TF], gu[:, 2 * TF:3 * TF]], axis=1),
            jnp.concatenate([gu[:, TF:2 * TF], gu[:, 3 * TF:]], axis=1))


def ff_in_fwd(hn, w_gu):
    R = hn.shape[0]

    def body(x_ref, w_ref, au_ref, ff_ref):
        au = _dot(x_ref[...], w_ref[...], 1, 0)
        au_ref[...] = au.astype(au_ref.dtype)
        a = au[:, :TF]
        ff_ref[...] = (a * _sigmoid(a) * au[:, TF:]).astype(ff_ref.dtype)

    tm = _pick(R, TMF)
    return pl.pallas_call(
        body, name="ff_in_fwd", grid=(DFF // TF, R // tm),
        in_specs=[pl.BlockSpec((tm, D), lambda j, i: (i, 0)), pl.BlockSpec((D, 2 * TF), lambda j, i: (0, j))],
        out_specs=[pl.BlockSpec((tm, 2 * TF), lambda j, i: (i, j)), pl.BlockSpec((tm, TF), lambda j, i: (i, j))],
        out_shape=[jax.ShapeDtypeStruct((R, 2 * DFF), MXU), jax.ShapeDtypeStruct((R, DFF), MXU)],
        compiler_params=_cp(("parallel", "parallel")),
    )(hn, w_gu)


def ff_down_dgrad(dh2, w_down, au):
    R = dh2.shape[0]

    def body(d_ref, w_ref, au_ref, o_ref):
        dff = _dot(d_ref[...], w_ref[...], 1, 1)
        a = au_ref[:, :TF].astype(F32)
        u = au_ref[:, TF:].astype(F32)
        s = _sigmoid(a)
        o_ref[:, :TF] = (dff * u * s * (1.0 + a * (1.0 - s))).astype(o_ref.dtype)
        o_ref[:, TF:] = (dff * a * s).astype(o_ref.dtype)

    tm = _pick(R, TMF)
    return pl.pallas_call(
        body, name="ff_down_dgrad", grid=(DFF // TF, R // tm),
        in_specs=[pl.BlockSpec((tm, D), lambda j, i: (i, 0)), pl.BlockSpec((TF, D), lambda j, i: (j, 0)),
                  pl.BlockSpec((tm, 2 * TF), lambda j, i: (i, j))],
        out_specs=pl.BlockSpec((tm, 2 * TF), lambda j, i: (i, j)),
        out_shape=jax.ShapeDtypeStruct((R, 2 * DFF), MXU),
        compiler_params=_cp(("parallel", "parallel")),
    )(dh2, w_down, au)


def out_proj_norm(x, w, res, g):
    R = x.shape[0]

    def body(x_ref, w_ref, res_ref, g_ref, h_ref, n_ref):
        hv = _dot(x_ref[...], w_ref[...], 1, 0) + res_ref[...]
        h_ref[...] = hv
        r = lax.rsqrt(jnp.mean(hv * hv, axis=-1, keepdims=True) + EPS)
        n_ref[...] = (hv * r * g_ref[...]).astype(n_ref.dtype)

    return pl.pallas_call(
        body, name="out_fwd", grid=(R // TM,),
        in_specs=[_rb(TM, D, 0), _const((D, D)), _rb(TM, D, 0), _const((1, D))],
        out_specs=[_rb(TM, D, 0), _rb(TM, D, 0)],
        out_shape=[jax.ShapeDtypeStruct((R, D), F32), jax.ShapeDtypeStruct((R, D), MXU)],
        compiler_params=_cp(("parallel",)),
    )(x, w, res, g)


def ff_down_loss(ff, w_down, h1, gf, target):
    R = ff.shape[0]
    assert PADR == TM

    def body(f_ref, w_ref, h1_ref, g_ref, t_ref, dh_ref, loss_ref, dg_ref):
        i = pl.program_id(0)
        hv = _dot(f_ref[...], w_ref[...], 1, 0) + h1_ref[...]
        r = lax.rsqrt(jnp.mean(hv * hv, axis=-1, keepdims=True) + EPS)
        g = g_ref[...]
        live = (i >= 1).astype(F32)
        e = (hv * r * g - t_ref[...]) * live
        dy = e * (1.0 / D)
        dyg = dy * g
        dh_ref[...] = r * dyg - hv * (r * r * r * jnp.mean(dyg * hv, axis=-1, keepdims=True))
        lpart = jnp.zeros((1, 128), F32) + 0.5 * jnp.sum(jnp.sum(e * e, axis=1, keepdims=True), axis=0, keepdims=True) * (1.0 / D)
        gpart = jnp.sum(dy * hv * r, axis=0, keepdims=True)

        @pl.when(i == 0)
        def _():
            loss_ref[...] = lpart
            dg_ref[...] = gpart

        @pl.when(i > 0)
        def _():
            loss_ref[...] += lpart
            dg_ref[...] += gpart

    return pl.pallas_call(
        body, name="ff_down_loss", grid=(R // TM,),
        in_specs=[_rb(TM, DFF, 0), _const((DFF, D)), _rb(TM, D, 0), _const((1, D)),
                  pl.BlockSpec((TM, D), lambda i: (jnp.maximum(i - 1, 0), 0))],
        out_specs=[_rb(TM, D, 0), _const((1, 128)), _const((1, D))],
        out_shape=[jax.ShapeDtypeStruct((R, D), F32), jax.ShapeDtypeStruct((1, 128), F32),
                   jax.ShapeDtypeStruct((1, D), F32)],
        compiler_params=_cp(("arbitrary",)),
    )(ff, w_down, h1, gf, target)


def conv_bwd(dc, proj, conv_w, dproj):
    R = dc.shape[0]
    t8 = TM // 8
    nt = R // TM

    def body(dc_ref, nxt_ref, x_ref, prv_ref, w_ref, dp_in, dp_ref, dw_ref):
        del dp_in
        i = pl.program_id(0)
        dcv = dc_ref[...]
        nxt = nxt_ref[...] * (i < nt - 1).astype(F32)
        x = x_ref[...]
        prv = prv_ref[...]
        w = w_ref[...]
        dx = dcv * w[3:4, :]
        rows = [None] * 4
        rows[3] = jnp.sum(dcv * x, axis=0, keepdims=True)
        for s in (1, 2, 3):
            dx = dx + _shift_up(dcv, nxt, s) * w[3 - s:4 - s, :]
            rows[3 - s] = jnp.sum(dcv * _shift_down(x, prv, s), axis=0, keepdims=True)
        dp_ref[...] = dx.astype(dp_ref.dtype)
        part = jnp.concatenate(rows + [jnp.sum(dcv, axis=0, keepdims=True), jnp.zeros((3, 1024), F32)], axis=0)

        @pl.when(i == 0)
        def _():
            dw_ref[...] = part

        @pl.when(i > 0)
        def _():
            dw_ref[...] += part

    return pl.pallas_call(
        body, name="conv_bwd", grid=(nt,),
        in_specs=[_rb(TM, 1024, 0),
                  pl.BlockSpec((8, 1024), lambda i: (jnp.minimum((i + 1) * t8, nt * t8 - 1), 0)),
                  _rb(TM, 1024, QK0 // 1024),
                  pl.BlockSpec((8, 1024), lambda i: (jnp.maximum(i * t8 - 1, 0), QK0 // 1024)),
                  _const((4, 1024)), pl.BlockSpec(memory_space=pl.ANY)],
        out_specs=[_rb(TM, 1024, QK0 // 1024), _const((8, 1024))],
        out_shape=[jax.ShapeDtypeStruct((R, NP), MXU), jax.ShapeDtypeStruct((8, 1024), F32)],
        input_output_aliases={5: 0},
        compiler_params=_cp(("arbitrary",)),
    )(dc, dc, proj, proj, conv_w, dproj)


def small_bwd(dgl, dga, proj, gb_row, dproj):
    R = dgl.shape[0]

    def body(dgl_ref, dga_ref, sm_ref, gb_ref, dp_in, dp_ref, dgb_ref):
        del dp_in
        i = pl.program_id(0)
        z = sm_ref[...] + gb_ref[...]
        lane = lax.broadcasted_iota(jnp.int32, z.shape, 1)
        row = lax.broadcasted_iota(jnp.int32, z.shape, 0) + i * TM
        valid = row >= NPADROWS
        dgl_v = dgl_ref[...]
        dgate = jnp.where(valid, jnp.where(lane < 4, dgl_v, dgl_v * _sigmoid(-z)), 0.0)
        ds = jnp.where(lane < 8, dgate, dga_ref[...])
        dp_ref[...] = ds.astype(dp_ref.dtype)
        part = jnp.sum(jnp.where(lane < 8, dgate, 0.0), axis=0, keepdims=True)

        @pl.when(i == 0)
        def _():
            dgb_ref[...] = part

        @pl.when(i > 0)
        def _():
            dgb_ref[...] += part

    return pl.pallas_call(
        body, name="small_bwd", grid=(R // TM,),
        in_specs=[_rb(TM, 128, 0), _rb(TM, 128, 0), _rb(TM, 128, SM0 // 128), _const((1, 128)),
                  pl.BlockSpec(memory_space=pl.ANY)],
        out_specs=[_rb(TM, 128, SM0 // 128), _const((1, 128))],
        out_shape=[jax.ShapeDtypeStruct((R, NP), MXU), jax.ShapeDtypeStruct((1, 128), F32)],
        input_output_aliases={4: 0},
        compiler_params=_cp(("arbitrary",)),
    )(dgl, dga, proj, gb_row, dproj)


def _masks(n=L):
    r = lax.broadcasted_iota(jnp.int32, (n, n), 0)
    c = lax.broadcasted_iota(jnp.int32, (n, n), 1)
    return r >= c, r == c, r


def _to_row(col, eye):
    return jnp.sum(jnp.where(eye, col, 0.0), axis=0, keepdims=True)


def _to_col(row, eye):
    return jnp.sum(jnp.where(eye, row, 0.0), axis=1, keepdims=True)


def _mlstm_chunk(q, k, logi_c, logf_c, m, n):
    tril, eye, _ = _masks(q.shape[0])
    logi_r, logf_r = _to_row(logi_c, eye), _to_row(logf_c, eye)
    b_c = jnp.sum(jnp.where(tril, logf_r, 0.0), axis=1, keepdims=True)
    b_r = _to_row(b_c, eye)
    g = jnp.sum(logf_c, axis=0, keepdims=True)
    dmat = jnp.where(tril, b_c - b_r + logi_r, NEG)
    mrow = jnp.maximum(b_c + m, jnp.max(dmat, axis=1, keepdims=True))
    dm = jnp.exp(dmat - mrow)
    s = _dot(q, k, 1, 1)
    w = dm * s
    a_in = jnp.exp(b_c + m - mrow)
    qn = jnp.sum(q * n, axis=1, keepdims=True)
    den = a_in * qn + jnp.sum(w, axis=1, keepdims=True)
    floor = jnp.exp(-mrow)
    nrm = jnp.maximum(jnp.abs(den), floor)
    wlog_c = g - b_c + logi_c
    m_new = jnp.maximum(g + m, jnp.max(wlog_c, axis=0, keepdims=True))
    a_st = jnp.exp(g + m - m_new)
    w_c = jnp.exp(wlog_c - m_new)
    return dict(b_c=b_c, g=g, dm=dm, s=s, w=w, a_in=a_in, qn=qn, den=den, floor=floor, nrm=nrm,
                m_new=m_new, a_st=a_st, w_c=w_c, tril=tril, eye=eye)


def _mlstm_head_fwd(h, glv, qk_ref, v_ref, mo_ref, hg_ref, hm_ref, ym_ref, cs_ref, nm_ref, c_s, nm_s):
    q = qk_ref[:, h * DQK:(h + 1) * DQK] * QSCALE
    k = qk_ref[:, 512 + h * DQK:512 + (h + 1) * DQK]
    v = v_ref[:, h * DV:(h + 1) * DV]
    C = c_s[h]
    n = nm_s[h, 0:1, :]
    m = nm_s[h, 1:2, 0:1]
    f = _mlstm_chunk(q, k, glv[:, h:h + 1], glv[:, 4 + h:5 + h], m, n)
    num = f["a_in"] * _dot(q, C, 1, 1) + _dot(f["w"], v, 1, 0)
    hh = num / f["nrm"]
    cs_ref[0, h] = C
    nm_ref[0, h] = nm_s[h]
    c_s[h] = f["a_st"] * C + _dot(f["w_c"] * v, k, 0, 0)
    n_new = f["a_st"] * n + jnp.sum(f["w_c"] * k, axis=0, keepdims=True)
    rowi = lax.broadcasted_iota(jnp.int32, (8, DQK), 0)
    nm_s[h] = jnp.where(rowi == 0, n_new, jnp.where(rowi == 1, f["m_new"], 0.0))
    rm = lax.rsqrt(jnp.mean(hh * hh, axis=-1, keepdims=True) + EPS)
    sl = slice(h * DV, (h + 1) * DV)
    hm_ref[:, sl] = hh
    ym_ref[:, sl] = (hh * rm * hg_ref[:, sl] * _sigmoid(mo_ref[:, sl])).astype(ym_ref.dtype)


def _mlstm_bwd_parts(dym_ref, hm_ref, qk_ref, cp_ref, v_ref, gl_ref, mo_ref, hg_ref, cs_ref, nm_ref,
                     dp_ref, dc_ref, dgl_ref, dhg_ref, dc_s, dn_s):
        def init():
            dc_s[...] = jnp.zeros_like(dc_s)
            dn_s[...] = jnp.zeros_like(dn_s)
            dhg_ref[...] = jnp.zeros_like(dhg_ref)

        def zero():
            dp_ref[...] = jnp.zeros_like(dp_ref)
            dc_ref[...] = jnp.zeros_like(dc_ref)
            dgl_ref[...] = jnp.zeros_like(dgl_ref)

        def compute():
            glv = gl_ref[...]
            lane = lax.broadcasted_iota(jnp.int32, (LM, 128), 1)
            dgl = jnp.zeros((LM, 128), F32)
            for h in range(NH):
                sl = slice(h * DV, (h + 1) * DV)
                sq = slice(h * DQK, (h + 1) * DQK)
                sk = slice(512 + h * DQK, 512 + (h + 1) * DQK)
                hh = hm_ref[:, sl]
                gain = hg_ref[:, sl]
                rm = lax.rsqrt(jnp.mean(hh * hh, axis=-1, keepdims=True) + EPS)
                sg = _sigmoid(mo_ref[:, sl])
                dyv = dym_ref[:, sl]
                dno = dyv * sg
                dp_ref[:, 1024 + h * DV:1024 + (h + 1) * DV] = (dyv * hh * rm * gain * sg * (1.0 - sg)).astype(dp_ref.dtype)
                dhg_ref[:, sl] += jnp.sum(dno * hh * rm, axis=0, keepdims=True)
                dnog = dno * gain
                dh = rm * dnog - hh * (rm * rm * rm * jnp.mean(dnog * hh, axis=-1, keepdims=True))
                q = qk_ref[:, sq] * QSCALE
                k = qk_ref[:, sk]
                v = v_ref[:, sl]
                C = cs_ref[0, h]
                n = nm_ref[0, h, 0:1, :]
                m = nm_ref[0, h, 1:2, 0:1]
                f = _mlstm_chunk(q, k, glv[:, h:h + 1], glv[:, 4 + h:5 + h], m, n)
                eye = f["eye"]
                a_in, nrm, den, w = f["a_in"], f["nrm"], f["den"], f["w"]
                dnum = dh / nrm
                dnrm = -jnp.sum(dh * hh, axis=1, keepdims=True) / nrm
                dden = jnp.where(jnp.abs(den) >= f["floor"], dnrm * jnp.sign(den), 0.0)
                dw = _dot(dnum, v, 1, 1) + dden
                dv = _dot(w, dnum, 0, 0)
                ds = dw * f["dm"]
                e = dw * w
                qc = _dot(q, C, 1, 1)
                dq = _dot(ds, k, 1, 0) + a_in * _dot(dnum, C, 1, 0) + (a_in * dden) * n
                dk = _dot(ds, q, 0, 0)
                dC_in = _dot(a_in * dnum, q, 0, 0)
                dn_in = jnp.sum((a_in * dden) * q, axis=0, keepdims=True)
                da_in = jnp.sum(dnum * qc, axis=1, keepdims=True) + dden * f["qn"]
                col_e = _to_col(jnp.sum(e, axis=0, keepdims=True), eye)
                db = jnp.sum(e, axis=1, keepdims=True) + da_in * a_in - col_e
                dlogi = col_e
                dCp = dc_s[h]
                dnp = dn_s[h, 0:1, :]
                a_st, w_c = f["a_st"], f["w_c"]
                da_st = (jnp.sum(jnp.sum(dCp * C, axis=1, keepdims=True), axis=0, keepdims=True)
                         + jnp.sum(dnp * n, axis=1, keepdims=True))
                vdc = _dot(v, dCp, 1, 0)
                dw_c = jnp.sum((vdc + dnp) * k, axis=1, keepdims=True)
                dv = dv + w_c * _dot(k, dCp, 1, 1)
                dk = dk + w_c * (vdc + dnp)
                fw = dw_c * w_c
                dg = jnp.sum(fw, axis=0, keepdims=True) + da_st * a_st
                db = db - fw
                dlogi = dlogi + fw
                rowc = lax.broadcasted_iota(jnp.int32, (LM, 1), 0)
                db = db + jnp.where(rowc == LM - 1, dg, 0.0)
                triu = lax.broadcasted_iota(jnp.int32, (LM, LM), 1) >= lax.broadcasted_iota(jnp.int32, (LM, LM), 0)
                dlogf = jnp.sum(jnp.where(triu, _to_row(db, eye), 0.0), axis=1, keepdims=True)
                dc_s[h] = a_st * dCp + dC_in
                dn_new = a_st * dnp + dn_in
                dn_s[h] = jnp.zeros((8, DQK), F32) + dn_new
                cq, ck = cp_ref[:, sq], cp_ref[:, sk]
                s_q, s_k = _sigmoid(cq), _sigmoid(ck)
                dc_ref[:, sq] = dq * QSCALE * s_q * (1.0 + cq * (1.0 - s_q))
                dc_ref[:, sk] = dk * s_k * (1.0 + ck * (1.0 - s_k))
                dp_ref[:, sl] = dv.astype(dp_ref.dtype)
                dgl = jnp.where(lane == h, dlogi, jnp.where(lane == 4 + h, dlogf, dgl))
            dgl_ref[...] = dgl

        return init, zero, compute


def _gla_logs(sm, a2p, b2, valid):
    za = _dot(sm, a2p, 1, 0) + b2
    return za, jnp.where(valid, _log_sigmoid(za) * (1.0 / TAU), 0.0)


def _valid_rows(c, width):
    row = lax.broadcasted_iota(jnp.int32, (L, width), 0) + c * L
    return row >= NPADROWS


def _gla_chunk(q, k, la):
    tril, _, _ = _masks()
    bc = _dot_exact(tril.astype(F32), la)
    btot = jnp.sum(la, axis=0, keepdims=True)
    ebc = jnp.exp(bc)
    qd = q * ebc
    ki = k * jnp.exp(-bc)
    ke = k * jnp.exp(btot - bc)
    att = jnp.where(tril, _dot(qd, ki, 1, 1), 0.0)
    return dict(tril=tril, bc=bc, btot=btot, ebc=ebc, qd=qd, ki=ki, ke=ke, att=att)


def _col128(row):
    r = lax.broadcasted_iota(jnp.int32, (DQK, DQK), 0)
    c = lax.broadcasted_iota(jnp.int32, (DQK, DQK), 1)
    return jnp.sum(jnp.where(r == c, row, 0.0), axis=1, keepdims=True)


def _gla_fwd_parts(c, q_ref, k_ref, v_ref, gr_ref, sm_ref, a2_ref, b2_ref, hg_ref, hgl_ref, yg_ref, ss_ref, s_s):
        def init():
            s_s[...] = jnp.zeros_like(s_s)

        def zero():
            hgl_ref[...] = jnp.zeros_like(hgl_ref)
            yg_ref[...] = jnp.zeros_like(yg_ref)
            ss_ref[...] = jnp.zeros_like(ss_ref)

        def compute():
            for s in range(LM // L):
                rows = pl.ds(s * L, L)
                chunk(LM // L * c + s, q_ref.at[rows], k_ref.at[rows], v_ref.at[rows], gr_ref.at[rows], sm_ref.at[rows],
                      hgl_ref.at[rows], yg_ref.at[rows], ss_ref.at[pl.ds(s, 1)])

        def chunk(c, q_ref, k_ref, v_ref, gr_ref, sm_ref, hgl_ref, yg_ref, ss_ref):
            _, loga = _gla_logs(sm_ref[...], a2_ref[...], b2_ref[...], _valid_rows(c, 512))
            for h in range(NH):
                sq = slice(h * DQK, (h + 1) * DQK)
                sl = slice(h * DV, (h + 1) * DV)
                q = q_ref[:, sq] * QSCALE
                k = k_ref[:, sq]
                v = v_ref[:, sl]
                S = s_s[h]
                f = _gla_chunk(q, k, loga[:, sq])
                o = _dot(f["att"], v, 1, 0) + _dot(f["qd"], S, 1, 0)
                ss_ref[0, h] = S
                s_s[h] = _col128(jnp.exp(f["btot"])) * S + _dot(f["ke"], v, 0, 0)
                rg = lax.rsqrt(jnp.mean(o * o, axis=-1, keepdims=True) + EPS)
                gr = gr_ref[:, sl]
                hgl_ref[:, sl] = o
                yg_ref[:, sl] = (o * rg * hg_ref[:, sl] * gr * _sigmoid(gr)).astype(yg_ref.dtype)

        return init, zero, compute


def _gla_bwd_parts(c, dy_ref, ho_ref, q_ref, k_ref, v_ref, gr_ref, sm_ref, a2_ref, b2_ref, hg_ref, ss_ref,
                   dp_ref, dga_ref, da2_ref, db2_ref, dhg_ref, ds_s):
        def init():
            ds_s[...] = jnp.zeros_like(ds_s)
            da2_ref[...] = jnp.zeros_like(da2_ref)
            db2_ref[...] = jnp.zeros_like(db2_ref)
            dhg_ref[...] = jnp.zeros_like(dhg_ref)

        def zero():
            dp_ref[...] = jnp.zeros_like(dp_ref)
            dga_ref[...] = jnp.zeros_like(dga_ref)

        def compute():
            for s in reversed(range(LM // L)):
                rows = pl.ds(s * L, L)
                chunk(LM // L * c + s, dy_ref.at[rows], ho_ref.at[rows], q_ref.at[rows], k_ref.at[rows], v_ref.at[rows],
                      gr_ref.at[rows], sm_ref.at[rows], ss_ref.at[pl.ds(s, 1)], dp_ref.at[rows], dga_ref.at[rows])

        def chunk(c, dy_ref, ho_ref, q_ref, k_ref, v_ref, gr_ref, sm_ref, ss_ref, dp_ref, dga_ref):
            valid = _valid_rows(c, 512)
            sm = sm_ref[...]
            za, loga = _gla_logs(sm, a2_ref[...], b2_ref[...], valid)
            dloga = []
            for h in range(NH):
                sq = slice(h * DQK, (h + 1) * DQK)
                sl = slice(h * DV, (h + 1) * DV)
                o = ho_ref[:, sl]
                gain = hg_ref[:, sl]
                rg = lax.rsqrt(jnp.mean(o * o, axis=-1, keepdims=True) + EPS)
                gr = gr_ref[:, sl]
                sg = _sigmoid(gr)
                dyv = dy_ref[:, sl]
                dno = dyv * gr * sg
                dp_ref[:, 2048 + h * DV:2048 + (h + 1) * DV] = (
                    dyv * o * rg * gain * sg * (1.0 + gr * (1.0 - sg))).astype(dp_ref.dtype)
                dhg_ref[:, sl] += jnp.sum(dno * o * rg, axis=0, keepdims=True)
                dnog = dno * gain
                do = rg * dnog - o * (rg * rg * rg * jnp.mean(dnog * o, axis=-1, keepdims=True))
                q = q_ref[:, sq] * QSCALE
                k = k_ref[:, sq]
                v = v_ref[:, sl]
                S = ss_ref[0, h]
                f = _gla_chunk(q, k, loga[:, sq])
                tril, qd, ki, ke = f["tril"], f["qd"], f["ki"], f["ke"]
                dSp = ds_s[h]
                datt = jnp.where(tril, _dot(do, v, 1, 1), 0.0)
                dqd = _dot(do, S, 1, 1) + _dot(datt, ki, 1, 0)
                dki = _dot(datt, qd, 0, 0)
                dv = _dot(f["att"], do, 0, 0) + _dot(ke, dSp, 1, 0)
                dke = _dot(v, dSp, 1, 1)
                ebt = jnp.exp(f["btot"])
                dbtot = jnp.sum(dke * ke, axis=0, keepdims=True) + ebt * _to_row128(jnp.sum(dSp * S, axis=1, keepdims=True))
                ds_s[h] = _dot(qd, do, 0, 0) + _col128(ebt) * dSp
                dq = dqd * f["ebc"]
                dk = dki * jnp.exp(-f["bc"]) + dke * jnp.exp(f["btot"] - f["bc"])
                dbc = dqd * qd - dki * ki - dke * ke
                rowc = lax.broadcasted_iota(jnp.int32, (L, DQK), 0)
                dbc = dbc + jnp.where(rowc == L - 1, dbtot, 0.0)
                triu = lax.broadcasted_iota(jnp.int32, (L, L), 1) >= lax.broadcasted_iota(jnp.int32, (L, L), 0)
                dloga.append(_dot_exact(triu.astype(F32), dbc))
                dp_ref[:, sq] = (dq * QSCALE).astype(dp_ref.dtype)
                dp_ref[:, 512 + h * DQK:512 + (h + 1) * DQK] = dk.astype(dp_ref.dtype)
                dp_ref[:, 1024 + h * DV:1024 + (h + 1) * DV] = dv.astype(dp_ref.dtype)
            dza = jnp.where(valid, jnp.concatenate(dloga, axis=1) * (1.0 / TAU) * _sigmoid(-za), 0.0)
            dga_ref[...] = _dot(dza, a2_ref[...], 1, 1)
            da2_ref[...] += _dot(sm, dza, 0, 0)
            db2_ref[...] += jnp.sum(dza, axis=0, keepdims=True)

        return init, zero, compute


def mix_fwd(qk, proj, gl, m_head_g, a2p, b2, g_head_g):
    R = qk.shape[0]
    NC = R // LM
    G = LM // L

    def body(qk_ref, mv_ref, gl_ref, mo_ref, mhg_ref, gq_ref, gk_ref, gv_ref, gr_ref, sm_ref, a2_ref, b2_ref, ghg_ref,
             hm_ref, ym_ref, cs_ref, nm_ref, hgl_ref, yg_ref, ss_ref, c_s, nm_s, s_s):
        c = pl.program_id(0)
        g_init, g_zero, g_compute = _gla_fwd_parts(c, gq_ref, gk_ref, gv_ref, gr_ref, sm_ref, a2_ref, b2_ref, ghg_ref,
                                                   hgl_ref, yg_ref, ss_ref, s_s)

        @pl.when(c <= CH0)
        def _():
            c_s[...] = jnp.zeros_like(c_s)
            nm_s[...] = jnp.zeros_like(nm_s)
            g_init()

        @pl.when(c < CH0)
        def _():
            hm_ref[...] = jnp.zeros_like(hm_ref)
            ym_ref[...] = jnp.zeros_like(ym_ref)
            cs_ref[...] = jnp.zeros_like(cs_ref)
            nm_ref[...] = jnp.zeros_like(nm_ref)
            g_zero()

        @pl.when(c >= CH0)
        def _():
            glv = gl_ref[...]
            for h in range(NH):
                _mlstm_head_fwd(h, glv, qk_ref, mv_ref, mo_ref, mhg_ref, hm_ref, ym_ref, cs_ref, nm_ref, c_s, nm_s)
            g_compute()

    st_m = pl.BlockSpec((1, NH, DV, DQK), lambda c: (c, 0, 0, 0))
    st_n = pl.BlockSpec((1, NH, 8, DQK), lambda c: (c, 0, 0, 0))
    st_g = pl.BlockSpec((G, NH, DQK, DV), lambda c: (c, 0, 0, 0))
    return pl.pallas_call(
        body, name="mix_fwd", grid=(NC,),
        in_specs=[_rb(LM, 1024, 0), _rb(LM, 1024, MV0 // 1024), _rb(LM, 128, 0), _rb(LM, 1024, MO0 // 1024),
                  _const((1, 1024)),
                  _rb(LM, 512, GQ0 // 512), _rb(LM, 512, GK0 // 512), _rb(LM, 1024, GV0 // 1024),
                  _rb(LM, 1024, GR0 // 1024), _rb(LM, 128, SM0 // 128), _const((128, 512)), _const((1, 512)),
                  _const((1, 1024))],
        out_specs=[_rb(LM, 1024, 0), _rb(LM, 1024, 0), st_m, st_n, _rb(LM, 1024, 0), _rb(LM, 1024, 0), st_g],
        out_shape=[jax.ShapeDtypeStruct((R, 1024), F32), jax.ShapeDtypeStruct((R, 1024), MXU),
                   jax.ShapeDtypeStruct((NC, NH, DV, DQK), F32), jax.ShapeDtypeStruct((NC, NH, 8, DQK), F32),
                   jax.ShapeDtypeStruct((R, 1024), F32), jax.ShapeDtypeStruct((R, 1024), MXU),
                   jax.ShapeDtypeStruct((G * NC, NH, DQK, DV), F32)],
        scratch_shapes=[pltpu.VMEM((NH, DV, DQK), F32), pltpu.VMEM((NH, 8, DQK), F32),
                        pltpu.VMEM((NH, DQK, DV), F32)],
        compiler_params=_cp(("arbitrary",)),
    )(qk, proj, gl, proj, m_head_g, proj, proj, proj, proj, proj, a2p, b2, g_head_g)


def mix_bwd(dym, hm, qk, cpre, proj, gl, m_head_g, cs, nm, dyg, hgl, a2p, b2, g_head_g, ss, dproj):
    R = qk.shape[0]
    NC = R // LM
    rev = lambda c: NC - 1 - c
    GW = GR0 + 1024 - GQ0

    def body(dym_ref, hm_ref, qk_ref, cp_ref, mv_ref, gl_ref, mo_ref, mhg_ref, cs_ref, nm_ref,
             dyg_ref, ho_ref, gq_ref, gk_ref, gv_ref, gr_ref, sm_ref, a2_ref, b2_ref, ghg_ref, ss_ref, dp_in,
             dp_ref, dc_ref, dgl_ref, dmhg_ref, dga_ref, da2_ref, db2_ref, dghg_ref, dc_s, dn_s, ds_s):
        del dp_in
        step = pl.program_id(0)
        c = NC - 1 - step
        m_init, m_zero, m_compute = _mlstm_bwd_parts(
            dym_ref, hm_ref, qk_ref, cp_ref, mv_ref, gl_ref, mo_ref, mhg_ref, cs_ref, nm_ref,
            dp_ref.at[:, 0:GQ0], dc_ref, dgl_ref, dmhg_ref, dc_s, dn_s)
        g_init, g_zero, g_compute = _gla_bwd_parts(
            c, dyg_ref, ho_ref, gq_ref, gk_ref, gv_ref, gr_ref, sm_ref, a2_ref, b2_ref, ghg_ref, ss_ref,
            dp_ref.at[:, GQ0:GQ0 + GW], dga_ref, da2_ref, db2_ref, dghg_ref, ds_s)

        @pl.when(step == 0)
        def _():
            m_init()
            g_init()

        @pl.when(c < CH0)
        def _():
            m_zero()
            g_zero()

        @pl.when(c >= CH0)
        def _():
            m_compute()
            g_compute()

    def rows(w, cb):
        return pl.BlockSpec((LM, w), lambda c: (rev(c), cb))

    return pl.pallas_call(
        body, name="mix_bwd", grid=(NC,),
        in_specs=[rows(1024, 0), rows(1024, 0), rows(1024, 0), rows(1024, 0), rows(1024, MV0 // 1024), rows(128, 0),
                  rows(1024, MO0 // 1024), _const((1, 1024)),
                  pl.BlockSpec((1, NH, DV, DQK), lambda c: (rev(c), 0, 0, 0)),
                  pl.BlockSpec((1, NH, 8, DQK), lambda c: (rev(c), 0, 0, 0)),
                  rows(1024, 0), rows(1024, 0), rows(512, GQ0 // 512), rows(512, GK0 // 512),
                  rows(1024, GV0 // 1024), rows(1024, GR0 // 1024), rows(128, SM0 // 128),
                  _const((128, 512)), _const((1, 512)), _const((1, 1024)),
                  pl.BlockSpec((LM // L, NH, DQK, DV), lambda c: (rev(c), 0, 0, 0)),
                  pl.BlockSpec(memory_space=pl.ANY)],
        out_specs=[rows(GQ0 + GW, 0), rows(1024, 0), rows(128, 0), _const((1, 1024)),
                   rows(128, 0), _const((128, 512)), _const((1, 512)), _const((1, 1024))],
        out_shape=[jax.ShapeDtypeStruct((R, NP), MXU), jax.ShapeDtypeStruct((R, 1024), F32),
                   jax.ShapeDtypeStruct((R, 128), F32), jax.ShapeDtypeStruct((1, 1024), F32),
                   jax.ShapeDtypeStruct((R, 128), F32), jax.ShapeDtypeStruct((128, 512), F32),
                   jax.ShapeDtypeStruct((1, 512), F32), jax.ShapeDtypeStruct((1, 1024), F32)],
        scratch_shapes=[pltpu.VMEM((NH, DV, DQK), F32), pltpu.VMEM((NH, 8, DQK), F32),
                        pltpu.VMEM((NH, DQK, DV), F32)],
        input_output_aliases={21: 0},
        compiler_params=_cp(("arbitrary",)),
    )(dym, hm, qk, cpre, proj, gl, proj, m_head_g, cs, nm, dyg, hgl, proj, proj, proj, proj, proj, a2p, b2,
      g_head_g, ss, dproj)


def _to_row128(col):
    r = lax.broadcasted_iota(jnp.int32, (DQK, DQK), 0)
    c = lax.broadcasted_iota(jnp.int32, (DQK, DQK), 1)
    return jnp.sum(jnp.where(r == c, col, 0.0), axis=0, keepdims=True)


def local_step(x, target, meta, norm1_g, wp, conv_w, conv_b, m_gate_b, g_a2, g_a2_b, m_head_g, g_head_g,
               norm2_g, final_g, late_weights, send_early, send_wp, first_order=None):
    seq = x.shape[0]
    assert seq % TM == 0
    h0 = jnp.concatenate([jnp.zeros((NPADROWS, D), F32), meta, x], axis=0)
    gb_row = jnp.zeros((1, 128), F32).at[0, 0:8].set(m_gate_b.reshape(8))
    a2p = jnp.zeros((128, 512), F32).at[8:8 + RANK].set(g_a2)
    mhg = m_head_g.reshape(1, 1024)
    ghg = g_head_g.reshape(1, 1024)

    xn = rms_fwd(h0, norm1_g, "rms1_fwd")
    proj = matmul(xn, wp, "nn", "proj_fwd", tm=1536, order=first_order)
    cpre, qk, gl = prep_fwd(proj, conv_w, conv_b, gb_row)
    hm, ym, cs, nm, hgl, yg, ss = mix_fwd(qk, proj, gl, mhg, a2p, g_a2_b, ghg)
    w_bm, w_bg, w_out, w_gu, w_down = late_weights(ym)
    bm = matmul(ym, w_bm, "nn", "branch_m_fwd")
    bg = matmul(yg, w_bg, "nn", "branch_g_fwd")
    merged = merge_fwd(bm, bg, proj)
    h1, hn = out_proj_norm(merged, w_out, h0, norm2_g)
    au, ff = ff_in_fwd(hn, w_gu)
    dh2, loss, d_final_g = ff_down_loss(ff, w_down, h1, final_g.reshape(1, D), target)

    d_w_down = matmul(ff, dh2, "tn", "ff_down_wgrad", tm=1408, tk=1536)
    dau = ff_down_dgrad(dh2, w_down, au)
    d_w_gu = matmul(hn, dau, "tn", "ff_in_wgrad", tm=1024, tn=1408, tk=1536)
    dhn = matmul(dau, w_gu, "nt", "ff_in_dgrad", tm=1536, tk=1408)
    dh1, d_norm2_g = rms_bwd(dhn, h1, norm2_g, dh2, "rms2_bwd")

    d_w_out = matmul(merged, dh1, "tn", "out_wgrad", tm=1024, tk=1536)
    dmerged = matmul(dh1, w_out, "nt", "out_dgrad")
    dbm, dbg, dproj = merge_bwd(dmerged, bm, bg, proj)
    d_w_bm = matmul(ym, dbm, "tn", "branch_m_wgrad", tm=1024, tk=1536)
    d_w_bg = matmul(yg, dbg, "tn", "branch_g_wgrad", tm=1024, tk=1536)
    token = send_early(dict(w_branch_m=d_w_bm, w_branch_g=d_w_bg, w_out=d_w_out, w_gu=d_w_gu, w_ff_down=d_w_down))
    dym = matmul(dbm, w_bm, "nt", "branch_m_dgrad", order=token)
    dyg = matmul(dbg, w_bg, "nt", "branch_g_dgrad")
    dproj, dc, dgl, d_mhg, dga, d_a2p, d_a2b, d_ghg = mix_bwd(
        dym, hm, qk, cpre, proj, gl, mhg, cs, nm, dyg, hgl, a2p, g_a2_b, ghg, ss, dproj)
    dproj, d_conv = conv_bwd(dc, proj, conv_w, dproj)
    dproj, d_gb = small_bwd(dgl, dga, proj, gb_row, dproj)
    d_wp = matmul(xn, dproj, "tn", "proj_wgrad", tm=1024, tn=1664, tk=1536)
    token = send_wp(d_wp)
    dxn = matmul(dproj, wp, "nt", "proj_dgrad", tm=1536, tk=1664, order=token)
    grad_x, d_meta, d_norm1_g = rms_bwd_input(dxn, h0, norm1_g, dh1, "rms1_bwd")

    grads = dict(
        meta_tokens=d_meta, norm1_g=d_norm1_g, conv_w=d_conv[0:4], conv_b=d_conv[4:5], m_gate_b=d_gb[0, 0:8].reshape(1, 2, 4),
        g_a2=d_a2p[8:8 + RANK], g_a2_b=d_a2b, m_head_g=d_mhg.reshape(NH, DV), g_head_g=d_ghg.reshape(NH, DV),
        norm2_g=d_norm2_g, final_g=d_final_g)
    return loss, grad_x, grads


_SEGS = [(0, 1024, QK0), (1024, 2048, MV0), (2048, 2056, SM0), (2056, 3080, MO0), (3080, 5128, GQ0),
         (5128, 5144, SM0 + 8), (5144, 6168, GR0), (6168, 8216, GM0)]
SHARD_W = NPROJ // NDEV


def regroup_cols(w8):
    parts = []
    for lo, hi, _ in sorted(_SEGS, key=lambda s: s[2]):
        while lo < hi:
            j = lo // SHARD_W
            end = min(hi, (j + 1) * SHARD_W)
            parts.append(w8[j, :, lo - j * SHARD_W:end - j * SHARD_W])
            lo = end
    parts.append(jnp.zeros((w8.shape[1], NP - NPROJ), w8.dtype))
    return jnp.concatenate(parts, axis=1)


def ungroup_cols(g):
    blocks = []
    for j in range(NDEV):
        lo, hi = j * SHARD_W, (j + 1) * SHARD_W
        parts = []
        for s_lo, s_hi, s_at in _SEGS:
            a, b = max(lo, s_lo), min(hi, s_hi)
            if a < b:
                parts.append(g[:, s_at + a - s_lo:s_at + b - s_lo])
        blocks.append(jnp.concatenate(parts, axis=1))
    return jnp.stack(blocks)


def col_blocks(g):
    r, c8 = g.shape
    return jnp.transpose(g.reshape(r, NDEV, c8 // NDEV), (1, 0, 2))


def from_col_blocks(g8):
    n, r, c = g8.shape
    return jnp.transpose(g8, (1, 0, 2)).reshape(r, n * c)


_MESHID = pl.DeviceIdType.MESH
_RELS = [(0, 0, 1), (1, 0, 0), (0, 1, 0), (1, 1, 0), (1, 0, 1), (0, 1, 1), (1, 1, 1)]


def _flip(v, bit):
    return 1 - v if bit else v


def all_gather(arrs, name):
    n = len(arrs)

    def body(*refs):
        ins, outs = refs[:n], refs[n:2 * n]
        send_sems, recv_sems, local_sems = refs[2 * n:]
        x, y, c = lax.axis_index("x"), lax.axis_index("y"), lax.axis_index("c")
        me, sibling = (x, y, c), (x, y, 1 - c)
        chips = [(1 - x, y), (x, 1 - y), (1 - x, 1 - y)]

        def slot(p):
            return 4 * p[0] + 2 * p[1] + p[2]

        def copy(a, k, block, to, src=None):
            dst = outs[a].at[slot(block)]
            return pltpu.make_async_remote_copy(
                src_ref=dst if src is None else src, dst_ref=dst,
                send_sem=send_sems.at[a, k], recv_sem=recv_sems.at[a, k],
                device_id=to, device_id_type=_MESHID)

        mine = [pltpu.make_async_copy(ins[a], outs[a].at[slot(me)], local_sems.at[a]) for a in range(n)]
        for cp in mine:
            cp.start()
        first = []
        for a in range(n):
            first.append(copy(a, 0, me, sibling, src=ins[a]))
            first += [copy(a, 1 + j, me, (*chip, c), src=ins[a]) for j, chip in enumerate(chips)]
        for cp in first:
            cp.start()
        passed = []
        for j, chip in enumerate(chips):
            for a in range(n):
                copy(a, 1 + j, (*chip, c), me).wait_recv()
                fwd = copy(a, 4 + j, (*chip, c), sibling)
                fwd.start()
                passed.append(fwd)
        for a in range(n):
            copy(a, 0, sibling, me).wait_recv()
            for j, chip in enumerate(chips):
                copy(a, 4 + j, (*chip, 1 - c), me).wait_recv()
        for cp in first + passed:
            cp.wait_send()
        for cp in mine:
            cp.wait()

    anyspec = pl.BlockSpec(memory_space=pl.ANY)
    return pl.pallas_call(
        body, name=name,
        in_specs=[anyspec] * n, out_specs=[anyspec] * n,
        out_shape=[jax.ShapeDtypeStruct((NDEV,) + a.shape, a.dtype) for a in arrs],
        scratch_shapes=[pltpu.SemaphoreType.DMA((n, 7)), pltpu.SemaphoreType.DMA((n, 7)),
                        pltpu.SemaphoreType.DMA((n,))],
    )(*arrs)


def exchange(blocks, rep, name):
    n = len(blocks)

    def body(*refs):
        b_refs, r_ref = refs[:n], refs[n]
        ob_refs, or_ref = refs[n + 1:2 * n + 1], refs[2 * n + 1]
        send_sems, recv_sems, local_sems = refs[2 * n + 2:]
        x, y, c = lax.axis_index("x"), lax.axis_index("y"), lax.axis_index("c")
        me = 4 * x + 2 * y + c

        def pairs(src_slot, dst_slot):
            return [(b_refs[a].at[src_slot], ob_refs[a].at[dst_slot]) for a in range(n)] + [(r_ref, or_ref.at[dst_slot])]

        loc = [pltpu.make_async_copy(s, d, local_sems.at[a]) for a, (s, d) in enumerate(pairs(me, me))]
        for cp in loc:
            cp.start()
        sends = []
        for k, (fx, fy, fc) in enumerate(_RELS):
            peer = (_flip(x, fx), _flip(y, fy), _flip(c, fc))
            pid = 4 * peer[0] + 2 * peer[1] + peer[2]
            for a, (s, d) in enumerate(pairs(pid, me)):
                sends.append(pltpu.make_async_remote_copy(
                    src_ref=s, dst_ref=d, send_sem=send_sems.at[a, k], recv_sem=recv_sems.at[a, k],
                    device_id=peer, device_id_type=_MESHID))
        for cp in sends:
            cp.start()
        for k, (fx, fy, fc) in enumerate(_RELS):
            peer = (_flip(x, fx), _flip(y, fy), _flip(c, fc))
            pid = 4 * peer[0] + 2 * peer[1] + peer[2]
            for a, (s, d) in enumerate(pairs(pid, pid)):
                pltpu.make_async_remote_copy(
                    src_ref=s, dst_ref=d, send_sem=send_sems.at[a, k], recv_sem=recv_sems.at[a, k],
                    device_id=peer, device_id_type=_MESHID).wait_recv()
        for cp in sends:
            cp.wait_send()
        for cp in loc:
            cp.wait()

    anyspec = pl.BlockSpec(memory_space=pl.ANY)
    return pl.pallas_call(
        body, name=name,
        in_specs=[anyspec] * (n + 1), out_specs=[anyspec] * (n + 1),
        out_shape=[jax.ShapeDtypeStruct(b.shape, b.dtype) for b in blocks]
        + [jax.ShapeDtypeStruct((NDEV,) + rep.shape, rep.dtype)],
        scratch_shapes=[pltpu.SemaphoreType.DMA((n + 1, 7)), pltpu.SemaphoreType.DMA((n + 1, 7)),
                        pltpu.SemaphoreType.DMA((n + 1,))],
    )(*blocks, rep)


_HBM = pl.BlockSpec(memory_space=pltpu.HBM)
_SEM = pl.BlockSpec(memory_space=pltpu.SEMAPHORE)
_EFFECT = pltpu.SideEffectType.DATAFLOW_SIDE_EFFECTING


def _peer_ids():
    x, y, c = lax.axis_index("x"), lax.axis_index("y"), lax.axis_index("c")
    peers = []
    for fx, fy, fc in _RELS:
        p = (_flip(x, fx), _flip(y, fy), _flip(c, fc))
        peers.append((p, 4 * p[0] + 2 * p[1] + p[2]))
    return 4 * x + 2 * y + c, peers


def _split_copy(src, land, a, k, peer, src_slot, dst_slot, send_sems, recv_sems):
    return pltpu.make_async_remote_copy(
        src_ref=src if src_slot is None else src.at[src_slot], dst_ref=land.at[dst_slot],
        send_sem=send_sems.at[7 * a + k], recv_sem=recv_sems.at[7 * a + k], device_id=peer, device_id_type=_MESHID)


def _own_copy(src, land, a, n, me, per_peer, send_sems):
    return pltpu.make_async_copy(src.at[me] if per_peer else src, land.at[me], send_sems.at[7 * n + a])


def send_start(srcs, per_peer, order, name):
    n = len(srcs)
    lands = [lax.empty((NDEV,) + (s.shape[1:] if per_peer else s.shape), s.dtype) for s in srcs]

    def body(*refs):
        src_refs, land_refs = refs[1:1 + n], refs[1 + n:1 + 2 * n]
        send_sems, recv_sems = refs[1 + 2 * n], refs[2 + 2 * n]
        token = refs[3 + 4 * n]
        me, peers = _peer_ids()
        for a in range(n):
            _own_copy(src_refs[a], land_refs[a], a, n, me, per_peer, send_sems).start()
        for a in range(n):
            for k, (peer, pid) in enumerate(peers):
                _split_copy(src_refs[a], land_refs[a], a, k, peer, pid if per_peer else None, me,
                            send_sems, recv_sems).start()
        token[...] = jnp.zeros_like(token)

    outs = pl.pallas_call(
        body, name=name,
        in_specs=[pl.BlockSpec(memory_space=pl.ANY)] + [_HBM] * (2 * n),
        out_shape=(pltpu.SemaphoreType.DMA((8 * n,)), pltpu.SemaphoreType.DMA((7 * n,)),
                   *[pltpu.HBM(s.shape, s.dtype) for s in srcs], *[pltpu.HBM(l.shape, l.dtype) for l in lands],
                   jax.ShapeDtypeStruct((8, 128), F32)),
        out_specs=(_SEM, _SEM, *[_HBM] * (2 * n), pl.BlockSpec(memory_space=pltpu.VMEM)),
        input_output_aliases={1 + i: 2 + i for i in range(2 * n)},
        compiler_params=pltpu.CompilerParams(has_side_effects=_EFFECT),
    )(order, *[pltpu.with_memory_space_constraint(s, pltpu.HBM) for s in srcs],
      *[pltpu.with_memory_space_constraint(l, pltpu.HBM) for l in lands])
    return (n, per_peer, outs[0], outs[1], outs[2:2 + n], outs[2 + n:2 + 2 * n]), outs[2 + 2 * n]


def send_wait(handle, after, name):
    n, per_peer, send_sems, recv_sems, src_thru, land_thru = handle

    def body(*refs):
        src_refs, land_refs = refs[:n], refs[n:2 * n]
        s_sems, r_sems = refs[2 * n], refs[2 * n + 1]
        me, peers = _peer_ids()
        for a in range(n):
            _own_copy(src_refs[a], land_refs[a], a, n, me, per_peer, s_sems).wait()
            for k, (peer, pid) in enumerate(peers):
                cp = _split_copy(src_refs[a], land_refs[a], a, k, peer, pid if per_peer else None, pid, s_sems, r_sems)
                cp.wait_send()
                cp.wait_recv()

    outs = pl.pallas_call(
        body, name=name,
        in_specs=[_HBM] * (2 * n) + [_SEM, _SEM, pl.BlockSpec(memory_space=pl.ANY)],
        out_shape=tuple(pltpu.HBM(t.shape, t.dtype) for t in (*src_thru, *land_thru)),
        out_specs=tuple([_HBM] * (2 * n)),
        input_output_aliases={i: i for i in range(2 * n)},
        compiler_params=pltpu.CompilerParams(has_side_effects=_EFFECT),
    )(*src_thru, *land_thru, send_sems, recv_sems, after)
    return list(outs[n:2 * n])


def adamw(parts, w, m, v, name, tr):
    npart, r, c = parts.shape
    c1 = 1.0 - ADAM_B1 ** ADAM_STEP
    c2 = 1.0 - ADAM_B2 ** ADAM_STEP

    def body(p_ref, w_ref, m_ref, v_ref, g_ref, d_ref, nm_ref, nv_ref):
        g = p_ref[0].astype(F32)
        for j in range(1, npart):
            g = g + p_ref[j].astype(F32)
        mn = ADAM_B1 * m_ref[...] + (1.0 - ADAM_B1) * g
        vn = ADAM_B2 * v_ref[...] + (1.0 - ADAM_B2) * (g * g)
        g_ref[...] = g
        nm_ref[...] = mn
        nv_ref[...] = vn
        d_ref[...] = -ADAM_LR * ((mn / c1) / (jnp.sqrt(vn / c2) + ADAM_EPS) + ADAM_WD * w_ref[...])

    spec = _rb(tr, c, 0)
    return pl.pallas_call(
        body, name=name, grid=(r // tr,),
        in_specs=[pl.BlockSpec((npart, tr, c), lambda i: (0, i, 0)), spec, spec, spec],
        out_specs=[spec] * 4, out_shape=[jax.ShapeDtypeStruct((r, c), F32)] * 4,
        compiler_params=_cp(("parallel",)),
    )(parts, w, m, v)


def sum_parts(parts, name, tc):
    npart, r, c = parts.shape

    def body(p_ref, o_ref):
        g = p_ref[0].astype(F32)
        for j in range(1, npart):
            g = g + p_ref[j].astype(F32)
        o_ref[...] = g

    return pl.pallas_call(
        body, name=name, grid=(c // tc,),
        in_specs=[pl.BlockSpec((npart, r, tc), lambda i: (0, 0, i))],
        out_specs=pl.BlockSpec((r, tc), lambda i: (0, i)),
        out_shape=jax.ShapeDtypeStruct((r, c), F32),
        compiler_params=_cp(("parallel",)),
    )(parts)


TINY = [("meta_tokens", (16, 1024)), ("conv_w", (4, 1024)), ("g_a2", (16, 512)), ("m_head_g", (4, 256)),
        ("g_head_g", (4, 256))]
REPL = [("norm1_g", (1, 1024)), ("conv_b", (1, 1024)), ("m_gate_b", (1, 2, 4)), ("g_a2_b", (1, 512)),
        ("norm2_g", (1, 1024)), ("final_g", (1024,))]
TINY_SIZE = 16 * 1024 + 4 * 1024 + 16 * 512 + 2 * 4 * 256
REPL_SIZE = 1024 + 1024 + 8 + 512 + 1024 + 1024
ROWS_GATHER = 8
ROWS_REP = 40
ROWS_OWN = 16


def pack_rows(vecs, rows):
    flat = jnp.concatenate([v.reshape(-1) for v in vecs])
    return jnp.pad(flat, (0, rows * 1024 - flat.shape[0])).reshape(rows, 1024)


def unpack_rows(packed, shapes):
    flat = packed.reshape(-1)
    out, off = [], 0
    for s in shapes:
        n = 1
        for d in s:
            n *= d
        out.append(flat[off:off + n].reshape(s))
        off += n
    return out


def kernel(x, meta_tokens, norm1_g, w_in, conv_w, conv_b, m_gate_b, g_a2, g_a2_b, m_head_g, g_head_g, w_branch_m, w_branch_g, w_out, norm2_g, w_ff_gate, w_ff_up, w_ff_down, final_g, loss_target, m_meta_tokens, m_norm1_g, m_w_in, m_conv_w, m_conv_b, m_m_gate_b, m_g_a2, m_g_a2_b, m_m_head_g, m_g_head_g, m_w_branch_m, m_w_branch_g, m_w_out, m_norm2_g, m_w_ff_gate, m_w_ff_up, m_w_ff_down, m_final_g, v_meta_tokens, v_norm1_g, v_w_in, v_conv_w, v_conv_b, v_m_gate_b, v_g_a2, v_g_a2_b, v_m_head_g, v_g_head_g, v_w_branch_m, v_w_branch_g, v_w_out, v_norm2_g, v_w_ff_gate, v_w_ff_up, v_w_ff_down, v_final_g):
    w_sh = dict(meta_tokens=meta_tokens, w_in=w_in[0], conv_w=conv_w[0], g_a2=g_a2[0], m_head_g=m_head_g[0],
                g_head_g=g_head_g[0], w_branch_m=w_branch_m[0], w_branch_g=w_branch_g[0], w_out=w_out[0],
                w_ff_gate=w_ff_gate[0], w_ff_up=w_ff_up[0], w_ff_down=w_ff_down[0])
    m_sh = dict(meta_tokens=m_meta_tokens, w_in=m_w_in[0], conv_w=m_conv_w[0], g_a2=m_g_a2[0],
                m_head_g=m_m_head_g[0], g_head_g=m_g_head_g[0], w_branch_m=m_w_branch_m[0],
                w_branch_g=m_w_branch_g[0], w_out=m_w_out[0], w_ff_gate=m_w_ff_gate[0], w_ff_up=m_w_ff_up[0],
                w_ff_down=m_w_ff_down[0])
    v_sh = dict(meta_tokens=v_meta_tokens, w_in=v_w_in[0], conv_w=v_conv_w[0], g_a2=v_g_a2[0],
                m_head_g=v_m_head_g[0], g_head_g=v_g_head_g[0], w_branch_m=v_w_branch_m[0],
                w_branch_g=v_w_branch_g[0], w_out=v_w_out[0], w_ff_gate=v_w_ff_gate[0], w_ff_up=v_w_ff_up[0],
                w_ff_down=v_w_ff_down[0])
    w_rep = dict(norm1_g=norm1_g, conv_b=conv_b, m_gate_b=m_gate_b, g_a2_b=g_a2_b, norm2_g=norm2_g, final_g=final_g)
    m_rep = dict(norm1_g=m_norm1_g, conv_b=m_conv_b, m_gate_b=m_m_gate_b, g_a2_b=m_g_a2_b, norm2_g=m_norm2_g,
                 final_g=m_final_g)
    v_rep = dict(norm1_g=v_norm1_g, conv_b=v_conv_b, m_gate_b=v_m_gate_b, g_a2_b=v_g_a2_b, norm2_g=v_norm2_g,
                 final_g=v_final_g)
    dev = 4 * lax.axis_index("x") + 2 * lax.axis_index("y") + lax.axis_index("c")
    tiny_names = [n for n, _ in TINY]
    repl_names = [n for n, _ in REPL]
    tiny_shard_shapes = [(s[0], s[1] // NDEV) for _, s in TINY]

    in8, tiny8 = all_gather([w_sh["w_in"].astype(MXU), pack_rows([w_sh[n] for n in tiny_names], ROWS_GATHER)],
                            "param_all_gather")
    late_names = ["w_branch_m", "w_branch_g", "w_out", "w_ff_gate", "w_ff_up", "w_ff_down"]
    late, first_order = send_start([w_sh[n].astype(MXU) for n in late_names], False, tiny8, "late_weights_start")
    wp = regroup_cols(in8)
    handles = {}

    def late_weights(after):
        bm8, bg8, out8, ffg8, ffu8, ffd8 = send_wait(late, after, "late_weights_wait")
        w_gu = interleave_gu(from_col_blocks(ffg8), from_col_blocks(ffu8))
        return bm8.reshape(D, D), bg8.reshape(D, D), out8.reshape(D, D), w_gu, ffd8.reshape(DFF, D)

    def send_early(g):
        d_gate, d_up = split_gu(g["w_gu"])
        blocks = [g["w_branch_m"].reshape(NDEV, D // NDEV, D).astype(WIRE),
                  g["w_branch_g"].reshape(NDEV, D // NDEV, D).astype(WIRE),
                  g["w_out"].reshape(NDEV, D // NDEV, D).astype(WIRE),
                  col_blocks(d_gate).astype(WIRE), col_blocks(d_up).astype(WIRE),
                  g["w_ff_down"].reshape(NDEV, DFF // NDEV, D).astype(WIRE)]
        handles["early"], token = send_start(blocks, True, blocks[0], "early_grads_start")
        return token

    def send_wp(d_wp):
        blocks = [ungroup_cols(d_wp).astype(WIRE)]
        handles["wp"], token = send_start(blocks, True, blocks[0], "proj_grads_start")
        return token

    tiny_full = {}
    for j in range(NDEV):
        for name, blk in zip(tiny_names, unpack_rows(tiny8[j], tiny_shard_shapes)):
            tiny_full.setdefault(name, []).append(blk)
    tiny_full = {n: jnp.concatenate(v, axis=1) for n, v in tiny_full.items()}

    loss, grad_x, g = local_step(
        x[0], loss_target[0], tiny_full["meta_tokens"], norm1_g, wp, tiny_full["conv_w"], conv_b, m_gate_b[0],
        tiny_full["g_a2"], g_a2_b, tiny_full["m_head_g"], tiny_full["g_head_g"], norm2_g, final_g,
        late_weights, send_early, send_wp, first_order)

    rep = pack_rows([g[n] for n in tiny_names + repl_names] + [loss[0, 0:1]], ROWS_REP)
    (got_rep,) = exchange([], rep, "small_grad_exchange")
    got_early = send_wait(handles["early"], got_rep, "early_grads_wait")
    (got_wp,) = send_wait(handles["wp"], got_rep, "proj_grads_wait")

    result = {}

    def update(name, parts, tr):
        outs = adamw(parts, w_sh[name], m_sh[name], v_sh[name], "adamw_" + name, tr)
        for kind, arr in zip(("grad", "delta", "new_m", "new_v"), outs):
            result[kind, name] = arr[None]

    update("w_in", got_wp, 128)
    update("w_branch_m", got_early[0], 128)
    update("w_branch_g", got_early[1], 128)
    update("w_out", got_early[2], 128)
    update("w_ff_gate", got_early[3], 256)
    update("w_ff_up", got_early[4], 256)
    update("w_ff_down", got_early[5], DFF // NDEV)

    rep_sum = sum_parts(got_rep, "sum_small", 1024)
    rep_g = unpack_rows(rep_sum, [s for _, s in TINY] + [s for _, s in REPL] + [(1,)])
    own_g = [lax.dynamic_slice_in_dim(gf, dev * ss[1], ss[1], axis=1) for gf, ss in zip(rep_g, tiny_shard_shapes)]
    own_g += rep_g[len(TINY):len(TINY) + len(REPL)]
    w_all = {**w_sh, **w_rep}
    m_all = {**m_sh, **m_rep}
    v_all = {**v_sh, **v_rep}
    names = tiny_names + repl_names
    outs = adamw(pack_rows(own_g, ROWS_OWN)[None], pack_rows([w_all[n] for n in names], ROWS_OWN),
                 pack_rows([m_all[n] for n in names], ROWS_OWN), pack_rows([v_all[n] for n in names], ROWS_OWN),
                 "adamw_small", ROWS_OWN)
    shapes = tiny_shard_shapes + [s for _, s in REPL]
    for kind, packed in zip(("grad", "delta", "new_m", "new_v"), outs):
        for name, arr in zip(names, unpack_rows(packed, shapes)):
            result[kind, name] = arr[None] if name in tiny_names and name != "meta_tokens" else arr
    loss_total = rep_g[-1][0]
    order = ["meta_tokens", "norm1_g", "w_in", "conv_w", "conv_b", "m_gate_b", "g_a2", "g_a2_b", "m_head_g", "g_head_g",
             "w_branch_m", "w_branch_g", "w_out", "norm2_g", "w_ff_gate", "w_ff_up", "w_ff_down", "final_g"]
    return (loss_total, grad_x[None], *[result[kind, n] for kind in ("grad", "delta", "new_m", "new_v") for n in order])
```

```python
import functools

import jax
import jax.numpy as jnp
from jax import lax
from jax.experimental import pallas as pl
from jax.experimental.pallas import tpu as pltpu

F32 = jnp.float32
MXU = jnp.bfloat16
WIRE = jnp.bfloat16

D = 1024
NH = 4
DV = 256
DQK = 128
L = 64
NMETA = 16
PADR = 512
LM = 256
CH0 = PADR // LM - 1
NPADROWS = PADR - NMETA
RANK = 16
DFF = 2816
EPS = 1e-6
TAU = 16.0
QSCALE = DQK ** -0.5
NEG = -1e30
NDEV = 8

MV0, MO0, GQ0, GK0, GV0, GR0, QK0, GM0, GG0, SM0 = 0, 1024, 2048, 2560, 3072, 4096, 5120, 6144, 7168, 8192
NP = 8320
NPROJ = 8216

ADAM_LR, ADAM_B1, ADAM_B2, ADAM_EPS, ADAM_WD, ADAM_STEP = 0.001, 0.9, 0.999, 1e-08, 0.01, 10

VMEM_LIMIT = 56 * 1024 * 1024
TM = 512


def _cp(sem):
    return pltpu.CompilerParams(dimension_semantics=sem, vmem_limit_bytes=VMEM_LIMIT)


def _sigmoid(x):
    return 1.0 / (1.0 + jnp.exp(-x))


def _log_sigmoid(x):
    return jnp.minimum(x, 0.0) - jnp.log1p(jnp.exp(-jnp.abs(x)))


def _dot(a, b, ca, cb):
    return lax.dot_general(a.astype(MXU), b.astype(MXU), (((ca,), (cb,)), ((), ())), preferred_element_type=F32)


def _dot_exact(a, b):
    return lax.dot_general(a, b, (((1,), (0,)), ((), ())), precision=lax.Precision.HIGHEST,
                           preferred_element_type=F32)


def _rb(tm, w, cb):
    return pl.BlockSpec((tm, w), lambda i: (i, cb))


def _const(shape):
    nd = len(shape)
    return pl.BlockSpec(shape, lambda i: (0,) * nd)


def _pick(n, target):
    if n <= target:
        return n
    best = None
    for t in range(128, target + 1, 128):
        if n % t == 0:
            best = t
    assert best is not None, (n, target)
    return best


def matmul(a, b, mode, name, add=None, out_dtype=F32, tm=512, tn=1664, tk=1024, order=None):
    if mode == "nn":
        (M, K), (K2, N) = a.shape, b.shape
    elif mode == "nt":
        (M, K), (N, K2) = a.shape, b.shape
    else:
        (K, M), (K2, N) = a.shape, b.shape
    assert K == K2, (a.shape, b.shape, mode)
    tm, tn, tk = _pick(M, tm), _pick(N, tn), _pick(K, tk)
    nk = K // tk
    assert nk == 1 or out_dtype == F32
    ca, cb = {"nn": (1, 0), "nt": (1, 1), "tn": (0, 0)}[mode]
    a_spec = {"nn": pl.BlockSpec((tm, tk), lambda j, i, k: (i, k)),
              "nt": pl.BlockSpec((tm, tk), lambda j, i, k: (i, k)),
              "tn": pl.BlockSpec((tk, tm), lambda j, i, k: (k, i))}[mode]
    b_spec = {"nn": pl.BlockSpec((tk, tn), lambda j, i, k: (k, j)),
              "nt": pl.BlockSpec((tn, tk), lambda j, i, k: (j, k)),
              "tn": pl.BlockSpec((tk, tn), lambda j, i, k: (k, j))}[mode]
    o_spec = pl.BlockSpec((tm, tn), lambda j, i, k: (i, j))
    has_add = add is not None

    def body(*refs):
        if order is not None:
            refs = refs[:-2] + refs[-1:]
        if has_add:
            a_ref, b_ref, add_ref, o_ref = refs
        else:
            a_ref, b_ref, o_ref = refs
            add_ref = None
        part = _dot(a_ref[...], b_ref[...], ca, cb)
        if nk == 1:
            if has_add:
                part = part + add_ref[...]
            o_ref[...] = part.astype(o_ref.dtype)
            return
        k = pl.program_id(2)

        @pl.when(k == 0)
        def _():
            o_ref[...] = part + add_ref[...] if has_add else part

        @pl.when(k > 0)
        def _():
            o_ref[...] += part

    in_specs = [a_spec, b_spec] + ([o_spec] if has_add else [])
    args = (a, b) + ((add,) if has_add else ())
    if order is not None:
        in_specs.append(pl.BlockSpec(order.shape, lambda j, i, k: (0, 0)))
        args += (order,)
    return pl.pallas_call(
        body, name=name, grid=(N // tn, M // tm, nk),
        in_specs=in_specs, out_specs=o_spec,
        out_shape=jax.ShapeDtypeStruct((M, N), out_dtype),
        compiler_params=_cp(("parallel", "parallel", "arbitrary")),
    )(*args)


def _h0_tile(i, x_ref, meta_ref):
    assert PADR == TM
    front = jnp.concatenate([jnp.zeros((NPADROWS, D), F32), meta_ref[...]], axis=0)
    return jnp.where(i == 0, front, x_ref[...])


_TOKENS = pl.BlockSpec((TM, D), lambda i: (jnp.maximum(i - 1, 0), 0))


def rms_fwd_input(x, meta, g, name):
    R = x.shape[0] + PADR

    def body(x_ref, meta_ref, g_ref, y_ref):
        xv = _h0_tile(pl.program_id(0), x_ref, meta_ref)
        r = lax.rsqrt(jnp.mean(xv * xv, axis=-1, keepdims=True) + EPS)
        y_ref[...] = (xv * r * g_ref[...]).astype(y_ref.dtype)

    return pl.pallas_call(
        body, name=name, grid=(R // TM,),
        in_specs=[_TOKENS, _const((NMETA, D)), _const((1, D))], out_specs=_rb(TM, D, 0),
        out_shape=jax.ShapeDtypeStruct((R, D), MXU), compiler_params=_cp(("parallel",)),
    )(x, meta, g)


def rms_bwd(dy, x, g, dres, name):
    R = x.shape[0]

    def body(dy_ref, x_ref, g_ref, dres_ref, dx_ref, dg_ref):
        i = pl.program_id(0)
        xv, dyv = x_ref[...], dy_ref[...]
        r = lax.rsqrt(jnp.mean(xv * xv, axis=-1, keepdims=True) + EPS)
        dyg = dyv * g_ref[...]
        dx_ref[...] = dres_ref[...] + r * dyg - xv * (r * r * r * jnp.mean(dyg * xv, axis=-1, keepdims=True))
        part = jnp.sum(dyv * xv * r, axis=0, keepdims=True)

        @pl.when(i == 0)
        def _():
            dg_ref[...] = part

        @pl.when(i > 0)
        def _():
            dg_ref[...] += part

    return pl.pallas_call(
        body, name=name, grid=(R // TM,),
        in_specs=[_rb(TM, D, 0), _rb(TM, D, 0), _const((1, D)), _rb(TM, D, 0)],
        out_specs=[_rb(TM, D, 0), _const((1, D))],
        out_shape=[jax.ShapeDtypeStruct((R, D), F32), jax.ShapeDtypeStruct((1, D), F32)],
        compiler_params=_cp(("arbitrary",)),
    )(dy, x, g, dres)


def dgrad_rms_bwd(a, w, x, g, dres, name, tm, tk):
    R, K = a.shape
    tm, tk = _pick(R, tm), _pick(K, tk)
    nk = K // tk

    def body(a_ref, w_ref, x_ref, g_ref, dres_ref, dx_ref, dg_ref):
        i, k = pl.program_id(0), pl.program_id(1)
        part = _dot(a_ref[...], w_ref[...], 1, 1)

        @pl.when(k == 0)
        def _():
            dx_ref[...] = part

        @pl.when(k > 0)
        def _():
            dx_ref[...] += part

        @pl.when(k == nk - 1)
        def _():
            xv, dyv = x_ref[...], dx_ref[...]
            r = lax.rsqrt(jnp.mean(xv * xv, axis=-1, keepdims=True) + EPS)
            dyg = dyv * g_ref[...]
            dx_ref[...] = dres_ref[...] + r * dyg - xv * (r * r * r * jnp.mean(dyg * xv, axis=-1, keepdims=True))
            gpart = jnp.sum(dyv * xv * r, axis=0, keepdims=True)

            @pl.when(i == 0)
            def _():
                dg_ref[...] = gpart

            @pl.when(i > 0)
            def _():
                dg_ref[...] += gpart

    row = pl.BlockSpec((tm, D), lambda i, k: (i, 0))
    return pl.pallas_call(
        body, name=name, grid=(R // tm, nk),
        in_specs=[pl.BlockSpec((tm, tk), lambda i, k: (i, k)), pl.BlockSpec((D, tk), lambda i, k: (0, k)), row,
                  pl.BlockSpec((1, D), lambda i, k: (0, 0)), row],
        out_specs=[row, pl.BlockSpec((1, D), lambda i, k: (0, 0))],
        out_shape=[jax.ShapeDtypeStruct((R, D), F32), jax.ShapeDtypeStruct((1, D), F32)],
        compiler_params=_cp(("arbitrary", "arbitrary")),
    )(a, w, x, g, dres)


def rms_bwd_input(dy, x, meta, g, dres, name):
    R = dy.shape[0]

    def body(dy_ref, x_ref, meta_ref, g_ref, dres_ref, dx_ref, dmeta_ref, dg_ref):
        i = pl.program_id(0)
        xv, dyv = _h0_tile(i, x_ref, meta_ref), dy_ref[...]
        r = lax.rsqrt(jnp.mean(xv * xv, axis=-1, keepdims=True) + EPS)
        dyg = dyv * g_ref[...]
        dx = dres_ref[...] + r * dyg - xv * (r * r * r * jnp.mean(dyg * xv, axis=-1, keepdims=True))
        dx_ref[...] = dx
        part = jnp.sum(dyv * xv * r, axis=0, keepdims=True)

        @pl.when(i == 0)
        def _():
            dg_ref[...] = part
            dmeta_ref[...] = dx[NPADROWS:PADR]

        @pl.when(i > 0)
        def _():
            dg_ref[...] += part

    return pl.pallas_call(
        body, name=name, grid=(R // TM,),
        in_specs=[_rb(TM, D, 0), _TOKENS, _const((NMETA, D)), _const((1, D)), _rb(TM, D, 0)],
        out_specs=[_TOKENS, _const((NMETA, D)), _const((1, D))],
        out_shape=[jax.ShapeDtypeStruct((R - PADR, D), F32), jax.ShapeDtypeStruct((NMETA, D), F32),
                   jax.ShapeDtypeStruct((1, D), F32)],
        compiler_params=_cp(("arbitrary",)),
    )(dy, x, meta, g, dres)


def _shift_down(cur, prev8, s):
    tm = cur.shape[0]
    rolled = pltpu.roll(cur, s, 0)
    rows8 = lax.broadcasted_iota(jnp.int32, (8, cur.shape[1]), 0)
    head = jnp.where(rows8 < s, pltpu.roll(prev8, s, 0), rolled[0:8])
    return jnp.concatenate([head, rolled[8:tm]], axis=0)


def _shift_up(cur, next8, s):
    tm = cur.shape[0]
    rolled = pltpu.roll(cur, tm - s, 0)
    rows8 = lax.broadcasted_iota(jnp.int32, (8, cur.shape[1]), 0)
    tail = jnp.where(rows8 >= 8 - s, pltpu.roll(next8, 8 - s, 0), rolled[tm - 8:tm])
    return jnp.concatenate([rolled[0:tm - 8], tail], axis=0)


def prep_fwd(proj, conv_w, conv_b, gb_row):
    R = proj.shape[0]
    t8 = TM // 8

    def body(x_ref, halo_ref, sm_ref, w_ref, b_ref, gb_ref, c_ref, qk_ref, gl_ref):
        i = pl.program_id(0)
        x = x_ref[...]
        halo = halo_ref[...]
        w = w_ref[...]
        c = x * w[3:4, :] + b_ref[...]
        for s in (1, 2, 3):
            c = c + _shift_down(x, halo, s) * w[3 - s:4 - s, :]
        c_ref[...] = c
        qk_ref[...] = c * _sigmoid(c)
        z = sm_ref[...] + gb_ref[...]
        lane = lax.broadcasted_iota(jnp.int32, z.shape, 1)
        row = lax.broadcasted_iota(jnp.int32, z.shape, 0) + i * TM
        valid = row >= NPADROWS
        logi = jnp.where(valid, z, NEG)
        logf = jnp.where(valid, _log_sigmoid(z), 0.0)
        gl_ref[...] = jnp.where(lane < 4, logi, jnp.where(lane < 8, logf, 0.0))

    return pl.pallas_call(
        body, name="prep_fwd", grid=(R // TM,),
        in_specs=[_rb(TM, 1024, QK0 // 1024),
                  pl.BlockSpec((8, 1024), lambda i: (jnp.maximum(i * t8 - 1, 0), QK0 // 1024)),
                  _rb(TM, 128, SM0 // 128), _const((4, 1024)), _const((1, 1024)), _const((1, 128))],
        out_specs=[_rb(TM, 1024, 0), _rb(TM, 1024, 0), _rb(TM, 128, 0)],
        out_shape=[jax.ShapeDtypeStruct((R, 1024), F32), jax.ShapeDtypeStruct((R, 1024), F32),
                   jax.ShapeDtypeStruct((R, 128), F32)],
        compiler_params=_cp(("parallel",)),
    )(proj, proj, proj, conv_w, conv_b, gb_row)


def merge_fwd(ym, yg, w_bm, w_bg, proj):
    R = ym.shape[0]

    def body(ym_ref, yg_ref, wm_ref, wg_ref, gm_ref, gg_ref, bm_ref, bg_ref, o_ref):
        bm = _dot(ym_ref[...], wm_ref[...], 1, 0)
        bg = _dot(yg_ref[...], wg_ref[...], 1, 0)
        bm_ref[...] = bm
        bg_ref[...] = bg
        o_ref[...] = (_sigmoid(gm_ref[...]) * bm + _sigmoid(gg_ref[...]) * bg).astype(o_ref.dtype)

    return pl.pallas_call(
        body, name="merge_fwd", grid=(R // TM,),
        in_specs=[_rb(TM, D, 0), _rb(TM, D, 0), _const((D, D)), _const((D, D)), _rb(TM, D, GM0 // D),
                  _rb(TM, D, GG0 // D)],
        out_specs=[_rb(TM, D, 0), _rb(TM, D, 0), _rb(TM, D, 0)],
        out_shape=[jax.ShapeDtypeStruct((R, D), F32), jax.ShapeDtypeStruct((R, D), F32),
                   jax.ShapeDtypeStruct((R, D), MXU)],
        compiler_params=_cp(("parallel",)),
    )(ym, yg, w_bm, w_bg, proj, proj)


def merge_bwd(dh1, w_out, bm, bg, proj):
    R = bm.shape[0]

    def body(dh_ref, w_ref, bm_ref, bg_ref, gm_ref, gg_ref, dbm_ref, dbg_ref, dp_ref):
        dm = _dot(dh_ref[...], w_ref[...], 1, 1)
        sm, sg = _sigmoid(gm_ref[...]), _sigmoid(gg_ref[...])
        dbm_ref[...] = (dm * sm).astype(dbm_ref.dtype)
        dbg_ref[...] = (dm * sg).astype(dbg_ref.dtype)
        dp_ref[:, 0:D] = (dm * bm_ref[...] * sm * (1.0 - sm)).astype(dp_ref.dtype)
        dp_ref[:, D:2 * D] = (dm * bg_ref[...] * sg * (1.0 - sg)).astype(dp_ref.dtype)

    return pl.pallas_call(
        body, name="merge_bwd", grid=(R // TM,),
        in_specs=[_rb(TM, D, 0), _const((D, D)), _rb(TM, D, 0), _rb(TM, D, 0), _rb(TM, D, GM0 // D),
                  _rb(TM, D, GG0 // D)],
        out_specs=[_rb(TM, D, 0), _rb(TM, D, 0), _rb(TM, 2 * D, GM0 // (2 * D))],
        out_shape=[jax.ShapeDtypeStruct((R, D), MXU), jax.ShapeDtypeStruct((R, D), MXU),
                   jax.ShapeDtypeStruct((R, NP), MXU)],
        compiler_params=_cp(("parallel",)),
    )(dh1, w_out, bm, bg, proj, proj)


TF = DFF // 2
TMF = 768


def interleave_gu(gate, up):
    return jnp.concatenate([gate[:, :TF], up[:, :TF], gate[:, TF:], up[:, TF:]], axis=1)


def split_gu(gu):
    return (jnp.concatenate([gu[:, 0:TF], gu[:, 2 * TF:3 * TF]], axis=1),
            jnp.concatenate([gu[:, TF:2 * TF], gu[:, 3 * TF:]], axis=1))


def ff_in_fwd(hn, w_gu):
    R = hn.shape[0]

    def body(x_ref, w_ref, au_ref, ff_ref):
        au = _dot(x_ref[...], w_ref[...], 1, 0)
        au_ref[...] = au.astype(au_ref.dtype)
        a = au[:, :TF]
        ff_ref[...] = (a * _sigmoid(a) * au[:, TF:]).astype(ff_ref.dtype)

    tm = _pick(R, TMF)
    return pl.pallas_call(
        body, name="ff_in_fwd", grid=(DFF // TF, R // tm),
        in_specs=[pl.BlockSpec((tm, D), lambda j, i: (i, 0)), pl.BlockSpec((D, 2 * TF), lambda j, i: (0, j))],
        out_specs=[pl.BlockSpec((tm, 2 * TF), lambda j, i: (i, j)), pl.BlockSpec((tm, TF), lambda j, i: (i, j))],
        out_shape=[jax.ShapeDtypeStruct((R, 2 * DFF), MXU), jax.ShapeDtypeStruct((R, DFF), MXU)],
        compiler_params=_cp(("parallel", "parallel")),
    )(hn, w_gu)


def ff_down_dgrad(dh2, w_down, au):
    R = dh2.shape[0]

    def body(d_ref, w_ref, au_ref, o_ref):
        dff = _dot(d_ref[...], w_ref[...], 1, 1)
        a = au_ref[:, :TF].astype(F32)
        u = au_ref[:, TF:].astype(F32)
        s = _sigmoid(a)
        o_ref[:, :TF] = (dff * u * s * (1.0 + a * (1.0 - s))).astype(o_ref.dtype)
        o_ref[:, TF:] = (dff * a * s).astype(o_ref.dtype)

    tm = _pick(R, TMF)
    return pl.pallas_call(
        body, name="ff_down_dgrad", grid=(DFF // TF, R // tm),
        in_specs=[pl.BlockSpec((tm, D), lambda j, i: (i, 0)), pl.BlockSpec((TF, D), lambda j, i: (j, 0)),
                  pl.BlockSpec((tm, 2 * TF), lambda j, i: (i, j))],
        out_specs=pl.BlockSpec((tm, 2 * TF), lambda j, i: (i, j)),
        out_shape=jax.ShapeDtypeStruct((R, 2 * DFF), MXU),
        compiler_params=_cp(("parallel", "parallel")),
    )(dh2, w_down, au)


def out_proj_norm(merged, w, x, meta, g):
    R = merged.shape[0]

    def body(m_ref, w_ref, x_ref, meta_ref, g_ref, h_ref, n_ref):
        hv = _dot(m_ref[...], w_ref[...], 1, 0) + _h0_tile(pl.program_id(0), x_ref, meta_ref)
        h_ref[...] = hv
        r = lax.rsqrt(jnp.mean(hv * hv, axis=-1, keepdims=True) + EPS)
        n_ref[...] = (hv * r * g_ref[...]).astype(n_ref.dtype)

    return pl.pallas_call(
        body, name="out_fwd", grid=(R // TM,),
        in_specs=[_rb(TM, D, 0), _const((D, D)), _TOKENS, _const((NMETA, D)), _const((1, D))],
        out_specs=[_rb(TM, D, 0), _rb(TM, D, 0)],
        out_shape=[jax.ShapeDtypeStruct((R, D), F32), jax.ShapeDtypeStruct((R, D), MXU)],
        compiler_params=_cp(("parallel",)),
    )(merged, w, x, meta, g)


def ff_down_loss(ff, w_down, h1, gf, target):
    R = ff.shape[0]
    assert PADR == TM

    def body(f_ref, w_ref, h1_ref, g_ref, t_ref, dh_ref, loss_ref, dg_ref):
        i = pl.program_id(0)
        hv = _dot(f_ref[...], w_ref[...], 1, 0) + h1_ref[...]
        r = lax.rsqrt(jnp.mean(hv * hv, axis=-1, keepdims=True) + EPS)
        g = g_ref[...]
        live = (i >= 1).astype(F32)
        e = (hv * r * g - t_ref[...]) * live
        dy = e * (1.0 / D)
        dyg = dy * g
        dh_ref[...] = r * dyg - hv * (r * r * r * jnp.mean(dyg * hv, axis=-1, keepdims=True))
        lpart = jnp.zeros((1, 128), F32) + 0.5 * jnp.sum(jnp.sum(e * e, axis=1, keepdims=True), axis=0, keepdims=True) * (1.0 / D)
        gpart = jnp.sum(dy * hv * r, axis=0, keepdims=True)

        @pl.when(i == 0)
        def _():
            loss_ref[...] = lpart
            dg_ref[...] = gpart

        @pl.when(i > 0)
        def _():
            loss_ref[...] += lpart
            dg_ref[...] += gpart

    return pl.pallas_call(
        body, name="ff_down_loss", grid=(R // TM,),
        in_specs=[_rb(TM, DFF, 0), _const((DFF, D)), _rb(TM, D, 0), _const((1, D)),
                  pl.BlockSpec((TM, D), lambda i: (jnp.maximum(i - 1, 0), 0))],
        out_specs=[_rb(TM, D, 0), _const((1, 128)), _const((1, D))],
        out_shape=[jax.ShapeDtypeStruct((R, D), F32), jax.ShapeDtypeStruct((1, 128), F32),
                   jax.ShapeDtypeStruct((1, D), F32)],
        compiler_params=_cp(("arbitrary",)),
    )(ff, w_down, h1, gf, target)


def conv_bwd(dc, proj, conv_w, dproj):
    R = dc.shape[0]
    t8 = TM // 8
    nt = R // TM

    def body(dc_ref, nxt_ref, x_ref, prv_ref, w_ref, dp_in, dp_ref, dw_ref):
        del dp_in
        i = pl.program_id(0)
        dcv = dc_ref[...]
        nxt = nxt_ref[...] * (i < nt - 1).astype(F32)
        x = x_ref[...]
        prv = prv_ref[...]
        w = w_ref[...]
        dx = dcv * w[3:4, :]
        rows = [None] * 4
        rows[3] = jnp.sum(dcv * x, axis=0, keepdims=True)
        for s in (1, 2, 3):
            dx = dx + _shift_up(dcv, nxt, s) * w[3 - s:4 - s, :]
            rows[3 - s] = jnp.sum(dcv * _shift_down(x, prv, s), axis=0, keepdims=True)
        dp_ref[...] = dx.astype(dp_ref.dtype)
        part = jnp.concatenate(rows + [jnp.sum(dcv, axis=0, keepdims=True), jnp.zeros((3, 1024), F32)], axis=0)

        @pl.when(i == 0)
        def _():
            dw_ref[...] = part

        @pl.when(i > 0)
        def _():
            dw_ref[...] += part

    return pl.pallas_call(
        body, name="conv_bwd", grid=(nt,),
        in_specs=[_rb(TM, 1024, 0),
                  pl.BlockSpec((8, 1024), lambda i: (jnp.minimum((i + 1) * t8, nt * t8 - 1), 0)),
                  _rb(TM, 1024, QK0 // 1024),
                  pl.BlockSpec((8, 1024), lambda i: (jnp.maximum(i * t8 - 1, 0), QK0 // 1024)),
                  _const((4, 1024)), pl.BlockSpec(memory_space=pl.ANY)],
        out_specs=[_rb(TM, 1024, QK0 // 1024), _const((8, 1024))],
        out_shape=[jax.ShapeDtypeStruct((R, NP), MXU), jax.ShapeDtypeStruct((8, 1024), F32)],
        input_output_aliases={5: 0},
        compiler_params=_cp(("arbitrary",)),
    )(dc, dc, proj, proj, conv_w, dproj)


def small_bwd(dgl, dga, proj, gb_row, dproj):
    R = dgl.shape[0]

    def body(dgl_ref, dga_ref, sm_ref, gb_ref, dp_in, dp_ref, dgb_ref):
        del dp_in
        i = pl.program_id(0)
        z = sm_ref[...] + gb_ref[...]
        lane = lax.broadcasted_iota(jnp.int32, z.shape, 1)
        row = lax.broadcasted_iota(jnp.int32, z.shape, 0) + i * TM
        valid = row >= NPADROWS
        dgl_v = dgl_ref[...]
        dgate = jnp.where(valid, jnp.where(lane < 4, dgl_v, dgl_v * _sigmoid(-z)), 0.0)
        ds = jnp.where(lane < 8, dgate, dga_ref[...])
        dp_ref[...] = ds.astype(dp_ref.dtype)
        part = jnp.sum(jnp.where(lane < 8, dgate, 0.0), axis=0, keepdims=True)

        @pl.when(i == 0)
        def _():
            dgb_ref[...] = part

        @pl.when(i > 0)
        def _():
            dgb_ref[...] += part

    return pl.pallas_call(
        body, name="small_bwd", grid=(R // TM,),
        in_specs=[_rb(TM, 128, 0), _rb(TM, 128, 0), _rb(TM, 128, SM0 // 128), _const((1, 128)),
                  pl.BlockSpec(memory_space=pl.ANY)],
        out_specs=[_rb(TM, 128, SM0 // 128), _const((1, 128))],
        out_shape=[jax.ShapeDtypeStruct((R, NP), MXU), jax.ShapeDtypeStruct((1, 128), F32)],
        input_output_aliases={4: 0},
        compiler_params=_cp(("arbitrary",)),
    )(dgl, dga, proj, gb_row, dproj)


def _masks(n=L):
    r = lax.broadcasted_iota(jnp.int32, (n, n), 0)
    c = lax.broadcasted_iota(jnp.int32, (n, n), 1)
    return r >= c, r == c, r


def _to_row(col, eye):
    return jnp.sum(jnp.where(eye, col, 0.0), axis=0, keepdims=True)


def _to_col(row, eye):
    return jnp.sum(jnp.where(eye, row, 0.0), axis=1, keepdims=True)


def _mlstm_chunk(q, k, logi_c, logf_c, m, n):
    tril, eye, _ = _masks(q.shape[0])
    logi_r, logf_r = _to_row(logi_c, eye), _to_row(logf_c, eye)
    b_c = jnp.sum(jnp.where(tril, logf_r, 0.0), axis=1, keepdims=True)
    b_r = _to_row(b_c, eye)
    g = jnp.sum(logf_c, axis=0, keepdims=True)
    dmat = jnp.where(tril, b_c - b_r + logi_r, NEG)
    mrow = jnp.maximum(b_c + m, jnp.max(dmat, axis=1, keepdims=True))
    dm = jnp.exp(dmat - mrow)
    s = _dot(q, k, 1, 1)
    w = dm * s
    a_in = jnp.exp(b_c + m - mrow)
    qn = jnp.sum(q * n, axis=1, keepdims=True)
    den = a_in * qn + jnp.sum(w, axis=1, keepdims=True)
    floor = jnp.exp(-mrow)
    nrm = jnp.maximum(jnp.abs(den), floor)
    wlog_c = g - b_c + logi_c
    m_new = jnp.maximum(g + m, jnp.max(wlog_c, axis=0, keepdims=True))
    a_st = jnp.exp(g + m - m_new)
    w_c = jnp.exp(wlog_c - m_new)
    return dict(b_c=b_c, g=g, dm=dm, s=s, w=w, a_in=a_in, qn=qn, den=den, floor=floor, nrm=nrm,
                m_new=m_new, a_st=a_st, w_c=w_c, tril=tril, eye=eye)


def _mlstm_head_fwd(h, glv, qk_ref, v_ref, mo_ref, hg_ref, hm_ref, ym_ref, cs_ref, nm_ref, c_s, nm_s):
    q = qk_ref[:, h * DQK:(h + 1) * DQK] * QSCALE
    k = qk_ref[:, 512 + h * DQK:512 + (h + 1) * DQK]
    v = v_ref[:, h * DV:(h + 1) * DV]
    C = c_s[h]
    n = nm_s[h, 0:1, :]
    m = nm_s[h, 1:2, 0:1]
    f = _mlstm_chunk(q, k, glv[:, h:h + 1], glv[:, 4 + h:5 + h], m, n)
    num = f["a_in"] * _dot(q, C, 1, 1) + _dot(f["w"], v, 1, 0)
    hh = num / f["nrm"]
    cs_ref[0, h] = C
    nm_ref[0, h] = nm_s[h]
    c_s[h] = f["a_st"] * C + _dot(f["w_c"] * v, k, 0, 0)
    n_new = f["a_st"] * n + jnp.sum(f["w_c"] * k, axis=0, keepdims=True)
    rowi = lax.broadcasted_iota(jnp.int32, (8, DQK), 0)
    nm_s[h] = jnp.where(rowi == 0, n_new, jnp.where(rowi == 1, f["m_new"], 0.0))
    rm = lax.rsqrt(jnp.mean(hh * hh, axis=-1, keepdims=True) + EPS)
    sl = slice(h * DV, (h + 1) * DV)
    hm_ref[:, sl] = hh
    ym_ref[:, sl] = (hh * rm * hg_ref[:, sl] * _sigmoid(mo_ref[:, sl])).astype(ym_ref.dtype)


def _mlstm_bwd_parts(dym_ref, hm_ref, qk_ref, cp_ref, v_ref, gl_ref, mo_ref, hg_ref, cs_ref, nm_ref,
                     dp_ref, dc_ref, dgl_ref, dhg_ref, dc_s, dn_s):
        def init():
            dc_s[...] = jnp.zeros_like(dc_s)
            dn_s[...] = jnp.zeros_like(dn_s)
            dhg_ref[...] = jnp.zeros_like(dhg_ref)

        def zero():
            dp_ref[...] = jnp.zeros_like(dp_ref)
            dc_ref[...] = jnp.zeros_like(dc_ref)
            dgl_ref[...] = jnp.zeros_like(dgl_ref)

        def compute():
            glv = gl_ref[...]
            lane = lax.broadcasted_iota(jnp.int32, (LM, 128), 1)
            dgl = jnp.zeros((LM, 128), F32)
            for h in range(NH):
                sl = slice(h * DV, (h + 1) * DV)
                sq = slice(h * DQK, (h + 1) * DQK)
                sk = slice(512 + h * DQK, 512 + (h + 1) * DQK)
                hh = hm_ref[:, sl]
                gain = hg_ref[:, sl]
                rm = lax.rsqrt(jnp.mean(hh * hh, axis=-1, keepdims=True) + EPS)
                sg = _sigmoid(mo_ref[:, sl])
                dyv = dym_ref[:, sl]
                dno = dyv * sg
                dp_ref[:, 1024 + h * DV:1024 + (h + 1) * DV] = (dyv * hh * rm * gain * sg * (1.0 - sg)).astype(dp_ref.dtype)
                dhg_ref[:, sl] += jnp.sum(dno * hh * rm, axis=0, keepdims=True)
                dnog = dno * gain
                dh = rm * dnog - hh * (rm * rm * rm * jnp.mean(dnog * hh, axis=-1, keepdims=True))
                q = qk_ref[:, sq] * QSCALE
                k = qk_ref[:, sk]
                v = v_ref[:, sl]
                C = cs_ref[0, h]
                n = nm_ref[0, h, 0:1, :]
                m = nm_ref[0, h, 1:2, 0:1]
                f = _mlstm_chunk(q, k, glv[:, h:h + 1], glv[:, 4 + h:5 + h], m, n)
                eye = f["eye"]
                a_in, nrm, den, w = f["a_in"], f["nrm"], f["den"], f["w"]
                dnum = dh / nrm
                dnrm = -jnp.sum(dh * hh, axis=1, keepdims=True) / nrm
                dden = jnp.where(jnp.abs(den) >= f["floor"], dnrm * jnp.sign(den), 0.0)
                dw = _dot(dnum, v, 1, 1) + dden
                dv = _dot(w, dnum, 0, 0)
                ds = dw * f["dm"]
                e = dw * w
                qc = _dot(q, C, 1, 1)
                dq = _dot(ds, k, 1, 0) + a_in * _dot(dnum, C, 1, 0) + (a_in * dden) * n
                dk = _dot(ds, q, 0, 0)
                dC_in = _dot(a_in * dnum, q, 0, 0)
                dn_in = jnp.sum((a_in * dden) * q, axis=0, keepdims=True)
                da_in = jnp.sum(dnum * qc, axis=1, keepdims=True) + dden * f["qn"]
                col_e = _to_col(jnp.sum(e, axis=0, keepdims=True), eye)
                db = jnp.sum(e, axis=1, keepdims=True) + da_in * a_in - col_e
                dlogi = col_e
                dCp = dc_s[h]
                dnp = dn_s[h, 0:1, :]
                a_st, w_c = f["a_st"], f["w_c"]
                da_st = (jnp.sum(jnp.sum(dCp * C, axis=1, keepdims=True), axis=0, keepdims=True)
                         + jnp.sum(dnp * n, axis=1, keepdims=True))
                vdc = _dot(v, dCp, 1, 0)
                dw_c = jnp.sum((vdc + dnp) * k, axis=1, keepdims=True)
                dv = dv + w_c * _dot(k, dCp, 1, 1)
                dk = dk + w_c * (vdc + dnp)
                fw = dw_c * w_c
                dg = jnp.sum(fw, axis=0, keepdims=True) + da_st * a_st
                db = db - fw
                dlogi = dlogi + fw
                rowc = lax.broadcasted_iota(jnp.int32, (LM, 1), 0)
                db = db + jnp.where(rowc == LM - 1, dg, 0.0)
                triu = lax.broadcasted_iota(jnp.int32, (LM, LM), 1) >= lax.broadcasted_iota(jnp.int32, (LM, LM), 0)
                dlogf = jnp.sum(jnp.where(triu, _to_row(db, eye), 0.0), axis=1, keepdims=True)
                dc_s[h] = a_st * dCp + dC_in
                dn_new = a_st * dnp + dn_in
                dn_s[h] = jnp.zeros((8, DQK), F32) + dn_new
                cq, ck = cp_ref[:, sq], cp_ref[:, sk]
                s_q, s_k = _sigmoid(cq), _sigmoid(ck)
                dc_ref[:, sq] = dq * QSCALE * s_q * (1.0 + cq * (1.0 - s_q))
                dc_ref[:, sk] = dk * s_k * (1.0 + ck * (1.0 - s_k))
                dp_ref[:, sl] = dv.astype(dp_ref.dtype)
                dgl = jnp.where(lane == h, dlogi, jnp.where(lane == 4 + h, dlogf, dgl))
            dgl_ref[...] = dgl

        return init, zero, compute


def _gla_logs(sm, a2p, b2, valid):
    za = _dot(sm, a2p, 1, 0) + b2
    return za, jnp.where(valid, _log_sigmoid(za) * (1.0 / TAU), 0.0)


def _valid_rows(c, width):
    row = lax.broadcasted_iota(jnp.int32, (L, width), 0) + c * L
    return row >= NPADROWS


def _gla_chunk(q, k, la):
    tril, _, _ = _masks()
    bc = _dot_exact(tril.astype(F32), la)
    btot = jnp.sum(la, axis=0, keepdims=True)
    ebc = jnp.exp(bc)
    qd = q * ebc
    ki = k * jnp.exp(-bc)
    ke = k * jnp.exp(btot - bc)
    att = jnp.where(tril, _dot(qd, ki, 1, 1), 0.0)
    return dict(tril=tril, bc=bc, btot=btot, ebc=ebc, qd=qd, ki=ki, ke=ke, att=att)


def _col128(row):
    r = lax.broadcasted_iota(jnp.int32, (DQK, DQK), 0)
    c = lax.broadcasted_iota(jnp.int32, (DQK, DQK), 1)
    return jnp.sum(jnp.where(r == c, row, 0.0), axis=1, keepdims=True)


def _gla_fwd_parts(c, q_ref, k_ref, v_ref, gr_ref, sm_ref, a2_ref, b2_ref, hg_ref, hgl_ref, yg_ref, ss_ref, s_s):
        def init():
            s_s[...] = jnp.zeros_like(s_s)

        def zero():
            hgl_ref[...] = jnp.zeros_like(hgl_ref)
            yg_ref[...] = jnp.zeros_like(yg_ref)
            ss_ref[...] = jnp.zeros_like(ss_ref)

        def compute():
            for s in range(LM // L):
                rows = pl.ds(s * L, L)
                chunk(LM // L * c + s, q_ref.at[rows], k_ref.at[rows], v_ref.at[rows], gr_ref.at[rows], sm_ref.at[rows],
                      hgl_ref.at[rows], yg_ref.at[rows], ss_ref.at[pl.ds(s, 1)])

        def chunk(c, q_ref, k_ref, v_ref, gr_ref, sm_ref, hgl_ref, yg_ref, ss_ref):
            _, loga = _gla_logs(sm_ref[...], a2_ref[...], b2_ref[...], _valid_rows(c, 512))
            for h in range(NH):
                sq = slice(h * DQK, (h + 1) * DQK)
                sl = slice(h * DV, (h + 1) * DV)
                q = q_ref[:, sq] * QSCALE
                k = k_ref[:, sq]
                v = v_ref[:, sl]
                S = s_s[h]
                f = _gla_chunk(q, k, loga[:, sq])
                o = _dot(f["att"], v, 1, 0) + _dot(f["qd"], S, 1, 0)
                ss_ref[0, h] = S
                s_s[h] = _col128(jnp.exp(f["btot"])) * S + _dot(f["ke"], v, 0, 0)
                rg = lax.rsqrt(jnp.mean(o * o, axis=-1, keepdims=True) + EPS)
                gr = gr_ref[:, sl]
                hgl_ref[:, sl] = o
                yg_ref[:, sl] = (o * rg * hg_ref[:, sl] * gr * _sigmoid(gr)).astype(yg_ref.dtype)

        return init, zero, compute


def _gla_bwd_parts(c, dy_ref, ho_ref, q_ref, k_ref, v_ref, gr_ref, sm_ref, a2_ref, b2_ref, hg_ref, ss_ref,
                   dp_ref, dga_ref, da2_ref, db2_ref, dhg_ref, ds_s):
        def init():
            ds_s[...] = jnp.zeros_like(ds_s)
            da2_ref[...] = jnp.zeros_like(da2_ref)
            db2_ref[...] = jnp.zeros_like(db2_ref)
            dhg_ref[...] = jnp.zeros_like(dhg_ref)

        def zero():
            dp_ref[...] = jnp.zeros_like(dp_ref)
            dga_ref[...] = jnp.zeros_like(dga_ref)

        def compute():
            for s in reversed(range(LM // L)):
                rows = pl.ds(s * L, L)
                chunk(LM // L * c + s, dy_ref.at[rows], ho_ref.at[rows], q_ref.at[rows], k_ref.at[rows], v_ref.at[rows],
                      gr_ref.at[rows], sm_ref.at[rows], ss_ref.at[pl.ds(s, 1)], dp_ref.at[rows], dga_ref.at[rows])

        def chunk(c, dy_ref, ho_ref, q_ref, k_ref, v_ref, gr_ref, sm_ref, ss_ref, dp_ref, dga_ref):
            valid = _valid_rows(c, 512)
            sm = sm_ref[...]
            za, loga = _gla_logs(sm, a2_ref[...], b2_ref[...], valid)
            dloga = []
            for h in range(NH):
                sq = slice(h * DQK, (h + 1) * DQK)
                sl = slice(h * DV, (h + 1) * DV)
                o = ho_ref[:, sl]
                gain = hg_ref[:, sl]
                rg = lax.rsqrt(jnp.mean(o * o, axis=-1, keepdims=True) + EPS)
                gr = gr_ref[:, sl]
                sg = _sigmoid(gr)
                dyv = dy_ref[:, sl]
                dno = dyv * gr * sg
                dp_ref[:, 2048 + h * DV:2048 + (h + 1) * DV] = (
                    dyv * o * rg * gain * sg * (1.0 + gr * (1.0 - sg))).astype(dp_ref.dtype)
                dhg_ref[:, sl] += jnp.sum(dno * o * rg, axis=0, keepdims=True)
                dnog = dno * gain
                do = rg * dnog - o * (rg * rg * rg * jnp.mean(dnog * o, axis=-1, keepdims=True))
                q = q_ref[:, sq] * QSCALE
                k = k_ref[:, sq]
                v = v_ref[:, sl]
                S = ss_ref[0, h]
                f = _gla_chunk(q, k, loga[:, sq])
                tril, qd, ki, ke = f["tril"], f["qd"], f["ki"], f["ke"]
                dSp = ds_s[h]
                datt = jnp.where(tril, _dot(do, v, 1, 1), 0.0)
                dqd = _dot(do, S, 1, 1) + _dot(datt, ki, 1, 0)
                dki = _dot(datt, qd, 0, 0)
                dv = _dot(f["att"], do, 0, 0) + _dot(ke, dSp, 1, 0)
                dke = _dot(v, dSp, 1, 1)
                ebt = jnp.exp(f["btot"])
                dbtot = jnp.sum(dke * ke, axis=0, keepdims=True) + ebt * _to_row128(jnp.sum(dSp * S, axis=1, keepdims=True))
                ds_s[h] = _dot(qd, do, 0, 0) + _col128(ebt) * dSp
                dq = dqd * f["ebc"]
                dk = dki * jnp.exp(-f["bc"]) + dke * jnp.exp(f["btot"] - f["bc"])
                dbc = dqd * qd - dki * ki - dke * ke
                rowc = lax.broadcasted_iota(jnp.int32, (L, DQK), 0)
                dbc = dbc + jnp.where(rowc == L - 1, dbtot, 0.0)
                triu = lax.broadcasted_iota(jnp.int32, (L, L), 1) >= lax.broadcasted_iota(jnp.int32, (L, L), 0)
                dloga.append(_dot_exact(triu.astype(F32), dbc))
                dp_ref[:, sq] = (dq * QSCALE).astype(dp_ref.dtype)
                dp_ref[:, 512 + h * DQK:512 + (h + 1) * DQK] = dk.astype(dp_ref.dtype)
                dp_ref[:, 1024 + h * DV:1024 + (h + 1) * DV] = dv.astype(dp_ref.dtype)
            dza = jnp.where(valid, jnp.concatenate(dloga, axis=1) * (1.0 / TAU) * _sigmoid(-za), 0.0)
            dga_ref[...] = _dot(dza, a2_ref[...], 1, 1)
            da2_ref[...] += _dot(sm, dza, 0, 0)
            db2_ref[...] += jnp.sum(dza, axis=0, keepdims=True)

        return init, zero, compute


def mix_fwd(qk, proj, gl, m_head_g, a2p, b2, g_head_g):
    R = qk.shape[0]
    NC = R // LM
    G = LM // L

    def body(qk_ref, mv_ref, gl_ref, mo_ref, mhg_ref, gq_ref, gk_ref, gv_ref, gr_ref, sm_ref, a2_ref, b2_ref, ghg_ref,
             hm_ref, ym_ref, cs_ref, nm_ref, hgl_ref, yg_ref, ss_ref, c_s, nm_s, s_s):
        c = pl.program_id(0)
        g_init, g_zero, g_compute = _gla_fwd_parts(c, gq_ref, gk_ref, gv_ref, gr_ref, sm_ref, a2_ref, b2_ref, ghg_ref,
                                                   hgl_ref, yg_ref, ss_ref, s_s)

        @pl.when(c <= CH0)
        def _():
            c_s[...] = jnp.zeros_like(c_s)
            nm_s[...] = jnp.zeros_like(nm_s)
            g_init()

        @pl.when(c < CH0)
        def _():
            hm_ref[...] = jnp.zeros_like(hm_ref)
            ym_ref[...] = jnp.zeros_like(ym_ref)
            cs_ref[...] = jnp.zeros_like(cs_ref)
            nm_ref[...] = jnp.zeros_like(nm_ref)
            g_zero()

        @pl.when(c >= CH0)
        def _():
            glv = gl_ref[...]
            for h in range(NH):
                _mlstm_head_fwd(h, glv, qk_ref, mv_ref, mo_ref, mhg_ref, hm_ref, ym_ref, cs_ref, nm_ref, c_s, nm_s)
            g_compute()

    st_m = pl.BlockSpec((1, NH, DV, DQK), lambda c: (c, 0, 0, 0))
    st_n = pl.BlockSpec((1, NH, 8, DQK), lambda c: (c, 0, 0, 0))
    st_g = pl.BlockSpec((G, NH, DQK, DV), lambda c: (c, 0, 0, 0))
    return pl.pallas_call(
        body, name="mix_fwd", grid=(NC,),
        in_specs=[_rb(LM, 1024, 0), _rb(LM, 1024, MV0 // 1024), _rb(LM, 128, 0), _rb(LM, 1024, MO0 // 1024),
                  _const((1, 1024)),
                  _rb(LM, 512, GQ0 // 512), _rb(LM, 512, GK0 // 512), _rb(LM, 1024, GV0 // 1024),
                  _rb(LM, 1024, GR0 // 1024), _rb(LM, 128, SM0 // 128), _const((128, 512)), _const((1, 512)),
                  _const((1, 1024))],
        out_specs=[_rb(LM, 1024, 0), _rb(LM, 1024, 0), st_m, st_n, _rb(LM, 1024, 0), _rb(LM, 1024, 0), st_g],
        out_shape=[jax.ShapeDtypeStruct((R, 1024), F32), jax.ShapeDtypeStruct((R, 1024), MXU),
                   jax.ShapeDtypeStruct((NC, NH, DV, DQK), F32), jax.ShapeDtypeStruct((NC, NH, 8, DQK), F32),
                   jax.ShapeDtypeStruct((R, 1024), F32), jax.ShapeDtypeStruct((R, 1024), MXU),
                   jax.ShapeDtypeStruct((G * NC, NH, DQK, DV), F32)],
        scratch_shapes=[pltpu.VMEM((NH, DV, DQK), F32), pltpu.VMEM((NH, 8, DQK), F32),
                        pltpu.VMEM((NH, DQK, DV), F32)],
        compiler_params=_cp(("arbitrary",)),
    )(qk, proj, gl, proj, m_head_g, proj, proj, proj, proj, proj, a2p, b2, g_head_g)


def mix_bwd(dym, hm, qk, cpre, proj, gl, m_head_g, cs, nm, dyg, hgl, a2p, b2, g_head_g, ss, dproj):
    R = qk.shape[0]
    NC = R // LM
    rev = lambda c: NC - 1 - c
    GW = GR0 + 1024 - GQ0

    def body(dym_ref, hm_ref, qk_ref, cp_ref, mv_ref, gl_ref, mo_ref, mhg_ref, cs_ref, nm_ref,
             dyg_ref, ho_ref, gq_ref, gk_ref, gv_ref, gr_ref, sm_ref, a2_ref, b2_ref, ghg_ref, ss_ref, dp_in,
             dp_ref, dc_ref, dgl_ref, dmhg_ref, dga_ref, da2_ref, db2_ref, dghg_ref, dc_s, dn_s, ds_s):
        del dp_in
        step = pl.program_id(0)
        c = NC - 1 - step
        m_init, m_zero, m_compute = _mlstm_bwd_parts(
            dym_ref, hm_ref, qk_ref, cp_ref, mv_ref, gl_ref, mo_ref, mhg_ref, cs_ref, nm_ref,
            dp_ref.at[:, 0:GQ0], dc_ref, dgl_ref, dmhg_ref, dc_s, dn_s)
        g_init, g_zero, g_compute = _gla_bwd_parts(
            c, dyg_ref, ho_ref, gq_ref, gk_ref, gv_ref, gr_ref, sm_ref, a2_ref, b2_ref, ghg_ref, ss_ref,
            dp_ref.at[:, GQ0:GQ0 + GW], dga_ref, da2_ref, db2_ref, dghg_ref, ds_s)

        @pl.when(step == 0)
        def _():
            m_init()
            g_init()

        @pl.when(c < CH0)
        def _():
            m_zero()
            g_zero()

        @pl.when(c >= CH0)
        def _():
            m_compute()
            g_compute()

    def rows(w, cb):
        return pl.BlockSpec((LM, w), lambda c: (rev(c), cb))

    return pl.pallas_call(
        body, name="mix_bwd", grid=(NC,),
        in_specs=[rows(1024, 0), rows(1024, 0), rows(1024, 0), rows(1024, 0), rows(1024, MV0 // 1024), rows(128, 0),
                  rows(1024, MO0 // 1024), _const((1, 1024)),
                  pl.BlockSpec((1, NH, DV, DQK), lambda c: (rev(c), 0, 0, 0)),
                  pl.BlockSpec((1, NH, 8, DQK), lambda c: (rev(c), 0, 0, 0)),
                  rows(1024, 0), rows(1024, 0), rows(512, GQ0 // 512), rows(512, GK0 // 512),
                  rows(1024, GV0 // 1024), rows(1024, GR0 // 1024), rows(128, SM0 // 128),
                  _const((128, 512)), _const((1, 512)), _const((1, 1024)),
                  pl.BlockSpec((LM // L, NH, DQK, DV), lambda c: (rev(c), 0, 0, 0)),
                  pl.BlockSpec(memory_space=pl.ANY)],
        out_specs=[rows(GQ0 + GW, 0), rows(1024, 0), rows(128, 0), _const((1, 1024)),
                   rows(128, 0), _const((128, 512)), _const((1, 512)), _const((1, 1024))],
        out_shape=[jax.ShapeDtypeStruct((R, NP), MXU), jax.ShapeDtypeStruct((R, 1024), F32),
                   jax.ShapeDtypeStruct((R, 128), F32), jax.ShapeDtypeStruct((1, 1024), F32),
                   jax.ShapeDtypeStruct((R, 128), F32), jax.ShapeDtypeStruct((128, 512), F32),
                   jax.ShapeDtypeStruct((1, 512), F32), jax.ShapeDtypeStruct((1, 1024), F32)],
        scratch_shapes=[pltpu.VMEM((NH, DV, DQK), F32), pltpu.VMEM((NH, 8, DQK), F32),
                        pltpu.VMEM((NH, DQK, DV), F32)],
        input_output_aliases={21: 0},
        compiler_params=_cp(("arbitrary",)),
    )(dym, hm, qk, cpre, proj, gl, proj, m_head_g, cs, nm, dyg, hgl, proj, proj, proj, proj, proj, a2p, b2,
      g_head_g, ss, dproj)


def _to_row128(col):
    r = lax.broadcasted_iota(jnp.int32, (DQK, DQK), 0)
    c = lax.broadcasted_iota(jnp.int32, (DQK, DQK), 1)
    return jnp.sum(jnp.where(r == c, col, 0.0), axis=0, keepdims=True)


def local_step(x, target, meta, norm1_g, wp, conv_w, conv_b, m_gate_b, g_a2, g_a2_b, m_head_g, g_head_g,
               norm2_g, final_g, late_weights, send_early, send_wp, first_order=None):
    seq = x.shape[0]
    assert seq % TM == 0
    gb_row = jnp.zeros((1, 128), F32).at[0, 0:8].set(m_gate_b.reshape(8))
    a2p = jnp.zeros((128, 512), F32).at[8:8 + RANK].set(g_a2)
    mhg = m_head_g.reshape(1, 1024)
    ghg = g_head_g.reshape(1, 1024)

    xn = rms_fwd_input(x, meta, norm1_g, "rms1_fwd")
    proj = matmul(xn, wp, "nn", "proj_fwd", tm=1536, order=first_order)
    cpre, qk, gl = prep_fwd(proj, conv_w, conv_b, gb_row)
    hm, ym, cs, nm, hgl, yg, ss = mix_fwd(qk, proj, gl, mhg, a2p, g_a2_b, ghg)
    w_bm, w_bg, w_out, w_gu, w_down = late_weights(ym)
    bm, bg, merged = merge_fwd(ym, yg, w_bm, w_bg, proj)
    h1, hn = out_proj_norm(merged, w_out, x, meta, norm2_g)
    au, ff = ff_in_fwd(hn, w_gu)
    dh2, loss, d_final_g = ff_down_loss(ff, w_down, h1, final_g.reshape(1, D), target)

    d_w_down = matmul(ff, dh2, "tn", "ff_down_wgrad", tm=1408, tk=1536)
    dau = ff_down_dgrad(dh2, w_down, au)
    d_w_gu = matmul(hn, dau, "tn", "ff_in_wgrad", tm=1024, tn=1408, tk=1536)
    dh1, d_norm2_g = dgrad_rms_bwd(dau, w_gu, h1, norm2_g, dh2, "ff_in_dgrad", 768, 1408)

    d_w_out = matmul(merged, dh1, "tn", "out_wgrad", tm=1024, tk=1536)
    dbm, dbg, dproj = merge_bwd(dh1, w_out, bm, bg, proj)
    d_w_bm = matmul(ym, dbm, "tn", "branch_m_wgrad", tm=1024, tk=1536)
    d_w_bg = matmul(yg, dbg, "tn", "branch_g_wgrad", tm=1024, tk=1536)
    token = send_early(dict(w_branch_m=d_w_bm, w_branch_g=d_w_bg, w_out=d_w_out, w_gu=d_w_gu, w_ff_down=d_w_down))
    dym = matmul(dbm, w_bm, "nt", "branch_m_dgrad", order=token)
    dyg = matmul(dbg, w_bg, "nt", "branch_g_dgrad")
    dproj, dc, dgl, d_mhg, dga, d_a2p, d_a2b, d_ghg = mix_bwd(
        dym, hm, qk, cpre, proj, gl, mhg, cs, nm, dyg, hgl, a2p, g_a2_b, ghg, ss, dproj)
    dproj, d_conv = conv_bwd(dc, proj, conv_w, dproj)
    dproj, d_gb = small_bwd(dgl, dga, proj, gb_row, dproj)
    d_wp = matmul(xn, dproj, "tn", "proj_wgrad", tm=1024, tn=1664, tk=1536)
    token = send_wp(d_wp)
    dxn = matmul(dproj, wp, "nt", "proj_dgrad", tm=1536, tk=1664, order=token)
    grad_x, d_meta, d_norm1_g = rms_bwd_input(dxn, x, meta, norm1_g, dh1, "rms1_bwd")

    grads = dict(
        meta_tokens=d_meta, norm1_g=d_norm1_g, conv_w=d_conv[0:4], conv_b=d_conv[4:5], m_gate_b=d_gb[0, 0:8].reshape(1, 2, 4),
        g_a2=d_a2p[8:8 + RANK], g_a2_b=d_a2b, m_head_g=d_mhg.reshape(NH, DV), g_head_g=d_ghg.reshape(NH, DV),
        norm2_g=d_norm2_g, final_g=d_final_g)
    return loss, grad_x, grads


_SEGS = [(0, 1024, QK0), (1024, 2048, MV0), (2048, 2056, SM0), (2056, 3080, MO0), (3080, 5128, GQ0),
         (5128, 5144, SM0 + 8), (5144, 6168, GR0), (6168, 8216, GM0)]
SHARD_W = NPROJ // NDEV


def regroup_cols(w8):
    parts = []
    for lo, hi, _ in sorted(_SEGS, key=lambda s: s[2]):
        while lo < hi:
            j = lo // SHARD_W
            end = min(hi, (j + 1) * SHARD_W)
            parts.append(w8[j, :, lo - j * SHARD_W:end - j * SHARD_W])
            lo = end
    parts.append(jnp.zeros((w8.shape[1], NP - NPROJ), w8.dtype))
    return jnp.concatenate(parts, axis=1)


def ungroup_cols(g):
    blocks = []
    for j in range(NDEV):
        lo, hi = j * SHARD_W, (j + 1) * SHARD_W
        parts = []
        for s_lo, s_hi, s_at in _SEGS:
            a, b = max(lo, s_lo), min(hi, s_hi)
            if a < b:
                parts.append(g[:, s_at + a - s_lo:s_at + b - s_lo])
        blocks.append(jnp.concatenate(parts, axis=1))
    return jnp.stack(blocks)


def col_blocks(g):
    r, c8 = g.shape
    return jnp.transpose(g.reshape(r, NDEV, c8 // NDEV), (1, 0, 2))


def from_col_blocks(g8):
    n, r, c = g8.shape
    return jnp.transpose(g8, (1, 0, 2)).reshape(r, n * c)


_MESHID = pl.DeviceIdType.MESH
_RELS = [(0, 0, 1), (1, 0, 0), (0, 1, 0), (1, 1, 0), (1, 0, 1), (0, 1, 1), (1, 1, 1)]


def _flip(v, bit):
    return 1 - v if bit else v


def all_gather(arrs, name):
    n = len(arrs)

    def body(*refs):
        ins, outs = refs[:n], refs[n:2 * n]
        send_sems, recv_sems, local_sems = refs[2 * n:]
        x, y, c = lax.axis_index("x"), lax.axis_index("y"), lax.axis_index("c")
        me, sibling = (x, y, c), (x, y, 1 - c)
        chips = [(1 - x, y), (x, 1 - y), (1 - x, 1 - y)]

        def slot(p):
            return 4 * p[0] + 2 * p[1] + p[2]

        def copy(a, k, block, to, src=None):
            dst = outs[a].at[slot(block)]
            return pltpu.make_async_remote_copy(
                src_ref=dst if src is None else src, dst_ref=dst,
                send_sem=send_sems.at[a, k], recv_sem=recv_sems.at[a, k],
                device_id=to, device_id_type=_MESHID)

        mine = [pltpu.make_async_copy(ins[a], outs[a].at[slot(me)], local_sems.at[a]) for a in range(n)]
        for cp in mine:
            cp.start()
        first = []
        for a in range(n):
            first.append(copy(a, 0, me, sibling, src=ins[a]))
            first += [copy(a, 1 + j, me, (*chip, c), src=ins[a]) for j, chip in enumerate(chips)]
        for cp in first:
            cp.start()
        passed = []
        for j, chip in enumerate(chips):
            for a in range(n):
                copy(a, 1 + j, (*chip, c), me).wait_recv()
                fwd = copy(a, 4 + j, (*chip, c), sibling)
                fwd.start()
                passed.append(fwd)
        for a in range(n):
            copy(a, 0, sibling, me).wait_recv()
            for j, chip in enumerate(chips):
                copy(a, 4 + j, (*chip, 1 - c), me).wait_recv()
        for cp in first + passed:
            cp.wait_send()
        for cp in mine:
            cp.wait()

    anyspec = pl.BlockSpec(memory_space=pl.ANY)
    return pl.pallas_call(
        body, name=name,
        in_specs=[anyspec] * n, out_specs=[anyspec] * n,
        out_shape=[jax.ShapeDtypeStruct((NDEV,) + a.shape, a.dtype) for a in arrs],
        scratch_shapes=[pltpu.SemaphoreType.DMA((n, 7)), pltpu.SemaphoreType.DMA((n, 7)),
                        pltpu.SemaphoreType.DMA((n,))],
    )(*arrs)


def exchange(blocks, rep, name):
    n = len(blocks)

    def body(*refs):
        b_refs, r_ref = refs[:n], refs[n]
        ob_refs, or_ref = refs[n + 1:2 * n + 1], refs[2 * n + 1]
        send_sems, recv_sems, local_sems = refs[2 * n + 2:]
        x, y, c = lax.axis_index("x"), lax.axis_index("y"), lax.axis_index("c")
        me = 4 * x + 2 * y + c

        def pairs(src_slot, dst_slot):
            return [(b_refs[a].at[src_slot], ob_refs[a].at[dst_slot]) for a in range(n)] + [(r_ref, or_ref.at[dst_slot])]

        loc = [pltpu.make_async_copy(s, d, local_sems.at[a]) for a, (s, d) in enumerate(pairs(me, me))]
        for cp in loc:
            cp.start()
        sends = []
        for k, (fx, fy, fc) in enumerate(_RELS):
            peer = (_flip(x, fx), _flip(y, fy), _flip(c, fc))
            pid = 4 * peer[0] + 2 * peer[1] + peer[2]
            for a, (s, d) in enumerate(pairs(pid, me)):
                sends.append(pltpu.make_async_remote_copy(
                    src_ref=s, dst_ref=d, send_sem=send_sems.at[a, k], recv_sem=recv_sems.at[a, k],
                    device_id=peer, device_id_type=_MESHID))
        for cp in sends:
            cp.start()
        for k, (fx, fy, fc) in enumerate(_RELS):
            peer = (_flip(x, fx), _flip(y, fy), _flip(c, fc))
            pid = 4 * peer[0] + 2 * peer[1] + peer[2]
            for a, (s, d) in enumerate(pairs(pid, pid)):
                pltpu.make_async_remote_copy(
                    src_ref=s, dst_ref=d, send_sem=send_sems.at[a, k], recv_sem=recv_sems.at[a, k],
                    device_id=peer, device_id_type=_MESHID).wait_recv()
        for cp in sends:
            cp.wait_send()
        for cp in loc:
            cp.wait()

    anyspec = pl.BlockSpec(memory_space=pl.ANY)
    return pl.pallas_call(
        body, name=name,
        in_specs=[anyspec] * (n + 1), out_specs=[anyspec] * (n + 1),
        out_shape=[jax.ShapeDtypeStruct(b.shape, b.dtype) for b in blocks]
        + [jax.ShapeDtypeStruct((NDEV,) + rep.shape, rep.dtype)],
        scratch_shapes=[pltpu.SemaphoreType.DMA((n + 1, 7)), pltpu.SemaphoreType.DMA((n + 1, 7)),
                        pltpu.SemaphoreType.DMA((n + 1,))],
    )(*blocks, rep)


_HBM = pl.BlockSpec(memory_space=pltpu.HBM)
_SEM = pl.BlockSpec(memory_space=pltpu.SEMAPHORE)
_EFFECT = pltpu.SideEffectType.DATAFLOW_SIDE_EFFECTING


def _peer_ids():
    x, y, c = lax.axis_index("x"), lax.axis_index("y"), lax.axis_index("c")
    peers = []
    for fx, fy, fc in _RELS:
        p = (_flip(x, fx), _flip(y, fy), _flip(c, fc))
        peers.append((p, 4 * p[0] + 2 * p[1] + p[2]))
    return 4 * x + 2 * y + c, peers


def _split_copy(src, land, a, k, peer, src_slot, dst_slot, send_sems, recv_sems):
    return pltpu.make_async_remote_copy(
        src_ref=src if src_slot is None else src.at[src_slot], dst_ref=land.at[dst_slot],
        send_sem=send_sems.at[7 * a + k], recv_sem=recv_sems.at[7 * a + k], device_id=peer, device_id_type=_MESHID)


def _own_copy(src, land, a, n, me, per_peer, send_sems):
    return pltpu.make_async_copy(src.at[me] if per_peer else src, land.at[me], send_sems.at[7 * n + a])


def send_start(srcs, per_peer, order, name):
    n = len(srcs)
    lands = [lax.empty((NDEV,) + (s.shape[1:] if per_peer else s.shape), s.dtype) for s in srcs]

    def body(*refs):
        src_refs, land_refs = refs[1:1 + n], refs[1 + n:1 + 2 * n]
        send_sems, recv_sems = refs[1 + 2 * n], refs[2 + 2 * n]
        token = refs[3 + 4 * n]
        me, peers = _peer_ids()
        for a in range(n):
            _own_copy(src_refs[a], land_refs[a], a, n, me, per_peer, send_sems).start()
        for a in range(n):
            for k, (peer, pid) in enumerate(peers):
                _split_copy(src_refs[a], land_refs[a], a, k, peer, pid if per_peer else None, me,
                            send_sems, recv_sems).start()
        token[...] = jnp.zeros_like(token)

    outs = pl.pallas_call(
        body, name=name,
        in_specs=[pl.BlockSpec(memory_space=pl.ANY)] + [_HBM] * (2 * n),
        out_shape=(pltpu.SemaphoreType.DMA((8 * n,)), pltpu.SemaphoreType.DMA((7 * n,)),
                   *[pltpu.HBM(s.shape, s.dtype) for s in srcs], *[pltpu.HBM(l.shape, l.dtype) for l in lands],
                   jax.ShapeDtypeStruct((8, 128), F32)),
        out_specs=(_SEM, _SEM, *[_HBM] * (2 * n), pl.BlockSpec(memory_space=pltpu.VMEM)),
        input_output_aliases={1 + i: 2 + i for i in range(2 * n)},
        compiler_params=pltpu.CompilerParams(has_side_effects=_EFFECT),
    )(order, *[pltpu.with_memory_space_constraint(s, pltpu.HBM) for s in srcs],
      *[pltpu.with_memory_space_constraint(l, pltpu.HBM) for l in lands])
    return (n, per_peer, outs[0], outs[1], outs[2:2 + n], outs[2 + n:2 + 2 * n]), outs[2 + 2 * n]


def send_wait(handle, after, name):
    n, per_peer, send_sems, recv_sems, src_thru, land_thru = handle

    def body(*refs):
        src_refs, land_refs = refs[:n], refs[n:2 * n]
        s_sems, r_sems = refs[2 * n], refs[2 * n + 1]
        me, peers = _peer_ids()
        for a in range(n):
            _own_copy(src_refs[a], land_refs[a], a, n, me, per_peer, s_sems).wait()
            for k, (peer, pid) in enumerate(peers):
                cp = _split_copy(src_refs[a], land_refs[a], a, k, peer, pid if per_peer else None, pid, s_sems, r_sems)
                cp.wait_send()
                cp.wait_recv()

    outs = pl.pallas_call(
        body, name=name,
        in_specs=[_HBM] * (2 * n) + [_SEM, _SEM, pl.BlockSpec(memory_space=pl.ANY)],
        out_shape=tuple(pltpu.HBM(t.shape, t.dtype) for t in (*src_thru, *land_thru)),
        out_specs=tuple([_HBM] * (2 * n)),
        input_output_aliases={i: i for i in range(2 * n)},
        compiler_params=pltpu.CompilerParams(has_side_effects=_EFFECT),
    )(*src_thru, *land_thru, send_sems, recv_sems, after)
    return list(outs[n:2 * n])


def adamw(parts, w, m, v, name, tr):
    npart, r, c = parts.shape
    c1 = 1.0 - ADAM_B1 ** ADAM_STEP
    c2 = 1.0 - ADAM_B2 ** ADAM_STEP

    def body(p_ref, w_ref, m_ref, v_ref, g_ref, d_ref, nm_ref, nv_ref):
        g = p_ref[0].astype(F32)
        for j in range(1, npart):
            g = g + p_ref[j].astype(F32)
        mn = ADAM_B1 * m_ref[...] + (1.0 - ADAM_B1) * g
        vn = ADAM_B2 * v_ref[...] + (1.0 - ADAM_B2) * (g * g)
        g_ref[...] = g
        nm_ref[...] = mn
        nv_ref[...] = vn
        d_ref[...] = -ADAM_LR * ((mn / c1) / (jnp.sqrt(vn / c2) + ADAM_EPS) + ADAM_WD * w_ref[...])

    spec = _rb(tr, c, 0)
    return pl.pallas_call(
        body, name=name, grid=(r // tr,),
        in_specs=[pl.BlockSpec((npart, tr, c), lambda i: (0, i, 0)), spec, spec, spec],
        out_specs=[spec] * 4, out_shape=[jax.ShapeDtypeStruct((r, c), F32)] * 4,
        compiler_params=_cp(("parallel",)),
    )(parts, w, m, v)


def sum_parts(parts, name, tc):
    npart, r, c = parts.shape

    def body(p_ref, o_ref):
        g = p_ref[0].astype(F32)
        for j in range(1, npart):
            g = g + p_ref[j].astype(F32)
        o_ref[...] = g

    return pl.pallas_call(
        body, name=name, grid=(c // tc,),
        in_specs=[pl.BlockSpec((npart, r, tc), lambda i: (0, 0, i))],
        out_specs=pl.BlockSpec((r, tc), lambda i: (0, i)),
        out_shape=jax.ShapeDtypeStruct((r, c), F32),
        compiler_params=_cp(("parallel",)),
    )(parts)


TINY = [("meta_tokens", (16, 1024)), ("conv_w", (4, 1024)), ("g_a2", (16, 512)), ("m_head_g", (4, 256)),
        ("g_head_g", (4, 256))]
REPL = [("norm1_g", (1, 1024)), ("conv_b", (1, 1024)), ("m_gate_b", (1, 2, 4)), ("g_a2_b", (1, 512)),
        ("norm2_g", (1, 1024)), ("final_g", (1024,))]
TINY_SIZE = 16 * 1024 + 4 * 1024 + 16 * 512 + 2 * 4 * 256
REPL_SIZE = 1024 + 1024 + 8 + 512 + 1024 + 1024
ROWS_GATHER = 8
ROWS_REP = 40
ROWS_OWN = 16


def pack_rows(vecs, rows):
    flat = jnp.concatenate([v.reshape(-1) for v in vecs])
    return jnp.pad(flat, (0, rows * 1024 - flat.shape[0])).reshape(rows, 1024)


def unpack_rows(packed, shapes):
    flat = packed.reshape(-1)
    out, off = [], 0
    for s in shapes:
        n = 1
        for d in s:
            n *= d
        out.append(flat[off:off + n].reshape(s))
        off += n
    return out


def kernel(x, meta_tokens, norm1_g, w_in, conv_w, conv_b, m_gate_b, g_a2, g_a2_b, m_head_g, g_head_g, w_branch_m, w_branch_g, w_out, norm2_g, w_ff_gate, w_ff_up, w_ff_down, final_g, loss_target, m_meta_tokens, m_norm1_g, m_w_in, m_conv_w, m_conv_b, m_m_gate_b, m_g_a2, m_g_a2_b, m_m_head_g, m_g_head_g, m_w_branch_m, m_w_branch_g, m_w_out, m_norm2_g, m_w_ff_gate, m_w_ff_up, m_w_ff_down, m_final_g, v_meta_tokens, v_norm1_g, v_w_in, v_conv_w, v_conv_b, v_m_gate_b, v_g_a2, v_g_a2_b, v_m_head_g, v_g_head_g, v_w_branch_m, v_w_branch_g, v_w_out, v_norm2_g, v_w_ff_gate, v_w_ff_up, v_w_ff_down, v_final_g):
    w_sh = dict(meta_tokens=meta_tokens, w_in=w_in[0], conv_w=conv_w[0], g_a2=g_a2[0], m_head_g=m_head_g[0],
                g_head_g=g_head_g[0], w_branch_m=w_branch_m[0], w_branch_g=w_branch_g[0], w_out=w_out[0],
                w_ff_gate=w_ff_gate[0], w_ff_up=w_ff_up[0], w_ff_down=w_ff_down[0])
    m_sh = dict(meta_tokens=m_meta_tokens, w_in=m_w_in[0], conv_w=m_conv_w[0], g_a2=m_g_a2[0],
                m_head_g=m_m_head_g[0], g_head_g=m_g_head_g[0], w_branch_m=m_w_branch_m[0],
                w_branch_g=m_w_branch_g[0], w_out=m_w_out[0], w_ff_gate=m_w_ff_gate[0], w_ff_up=m_w_ff_up[0],
                w_ff_down=m_w_ff_down[0])
    v_sh = dict(meta_tokens=v_meta_tokens, w_in=v_w_in[0], conv_w=v_conv_w[0], g_a2=v_g_a2[0],
                m_head_g=v_m_head_g[0], g_head_g=v_g_head_g[0], w_branch_m=v_w_branch_m[0],
                w_branch_g=v_w_branch_g[0], w_out=v_w_out[0], w_ff_gate=v_w_ff_gate[0], w_ff_up=v_w_ff_up[0],
                w_ff_down=v_w_ff_down[0])
    w_rep = dict(norm1_g=norm1_g, conv_b=conv_b, m_gate_b=m_gate_b, g_a2_b=g_a2_b, norm2_g=norm2_g, final_g=final_g)
    m_rep = dict(norm1_g=m_norm1_g, conv_b=m_conv_b, m_gate_b=m_m_gate_b, g_a2_b=m_g_a2_b, norm2_g=m_norm2_g,
                 final_g=m_final_g)
    v_rep = dict(norm1_g=v_norm1_g, conv_b=v_conv_b, m_gate_b=v_m_gate_b, g_a2_b=v_g_a2_b, norm2_g=v_norm2_g,
                 final_g=v_final_g)
    dev = 4 * lax.axis_index("x") + 2 * lax.axis_index("y") + lax.axis_index("c")
    tiny_names = [n for n, _ in TINY]
    repl_names = [n for n, _ in REPL]
    tiny_shard_shapes = [(s[0], s[1] // NDEV) for _, s in TINY]

    in8, tiny8 = all_gather([w_sh["w_in"].astype(MXU), pack_rows([w_sh[n] for n in tiny_names], ROWS_GATHER)],
                            "param_all_gather")
    late_names = ["w_branch_m", "w_branch_g", "w_out", "w_ff_gate", "w_ff_up", "w_ff_down"]
    late, first_order = send_start([w_sh[n].astype(MXU) for n in late_names], False, tiny8, "late_weights_start")
    wp = regroup_cols(in8)
    handles = {}

    def late_weights(after):
        bm8, bg8, out8, ffg8, ffu8, ffd8 = send_wait(late, after, "late_weights_wait")
        w_gu = interleave_gu(from_col_blocks(ffg8), from_col_blocks(ffu8))
        return bm8.reshape(D, D), bg8.reshape(D, D), out8.reshape(D, D), w_gu, ffd8.reshape(DFF, D)

    def send_early(g):
        d_gate, d_up = split_gu(g["w_gu"])
        blocks = [g["w_branch_m"].reshape(NDEV, D // NDEV, D).astype(WIRE),
                  g["w_branch_g"].reshape(NDEV, D // NDEV, D).astype(WIRE),
                  g["w_out"].reshape(NDEV, D // NDEV, D).astype(WIRE),
                  col_blocks(d_gate).astype(WIRE), col_blocks(d_up).astype(WIRE),
                  g["w_ff_down"].reshape(NDEV, DFF // NDEV, D).astype(WIRE)]
        handles["early"], token = send_start(blocks, True, blocks[0], "early_grads_start")
        return token

    def send_wp(d_wp):
        blocks = [ungroup_cols(d_wp).astype(WIRE)]
        handles["wp"], token = send_start(blocks, True, blocks[0], "proj_grads_start")
        return token

    tiny_full = {}
    for j in range(NDEV):
        for name, blk in zip(tiny_names, unpack_rows(tiny8[j], tiny_shard_shapes)):
            tiny_full.setdefault(name, []).append(blk)
    tiny_full = {n: jnp.concatenate(v, axis=1) for n, v in tiny_full.items()}

    loss, grad_x, g = local_step(
        x[0], loss_target[0], tiny_full["meta_tokens"], norm1_g, wp, tiny_full["conv_w"], conv_b, m_gate_b[0],
        tiny_full["g_a2"], g_a2_b, tiny_full["m_head_g"], tiny_full["g_head_g"], norm2_g, final_g,
        late_weights, send_early, send_wp, first_order)

    rep = pack_rows([g[n] for n in tiny_names + repl_names] + [loss[0, 0:1]], ROWS_REP)
    (got_rep,) = exchange([], rep, "small_grad_exchange")
    got_early = send_wait(handles["early"], got_rep, "early_grads_wait")
    (got_wp,) = send_wait(handles["wp"], got_rep, "proj_grads_wait")

    result = {}

    def update(name, parts, tr):
        outs = adamw(parts, w_sh[name], m_sh[name], v_sh[name], "adamw_" + name, tr)
        for kind, arr in zip(("grad", "delta", "new_m", "new_v"), outs):
            result[kind, name] = arr[None]

    update("w_in", got_wp, 128)
    update("w_branch_m", got_early[0], 128)
    update("w_branch_g", got_early[1], 128)
    update("w_out", got_early[2], 128)
    update("w_ff_gate", got_early[3], 256)
    update("w_ff_up", got_early[4], 256)
    update("w_ff_down", got_early[5], DFF // NDEV)

    rep_sum = sum_parts(got_rep, "sum_small", 1024)
    rep_g = unpack_rows(rep_sum, [s for _, s in TINY] + [s for _, s in REPL] + [(1,)])
    own_g = [lax.dynamic_slice_in_dim(gf, dev * ss[1], ss[1], axis=1) for gf, ss in zip(rep_g, tiny_shard_shapes)]
    own_g += rep_g[len(TINY):len(TINY) + len(REPL)]
    w_all = {**w_sh, **w_rep}
    m_all = {**m_sh, **m_rep}
    v_all = {**v_sh, **v_rep}
    names = tiny_names + repl_names
    outs = adamw(pack_rows(own_g, ROWS_OWN)[None], pack_rows([w_all[n] for n in names], ROWS_OWN),
                 pack_rows([m_all[n] for n in names], ROWS_OWN), pack_rows([v_all[n] for n in names], ROWS_OWN),
                 "adamw_small", ROWS_OWN)
    shapes = tiny_shard_shapes + [s for _, s in REPL]
    for kind, packed in zip(("grad", "delta", "new_m", "new_v"), outs):
        for name, arr in zip(names, unpack_rows(packed, shapes)):
            result[kind, name] = arr[None] if name in tiny_names and name != "meta_tokens" else arr
    loss_total = rep_g[-1][0]
    order = ["meta_tokens", "norm1_g", "w_in", "conv_w", "conv_b", "m_gate_b", "g_a2", "g_a2_b", "m_head_g", "g_head_g",
             "w_branch_m", "w_branch_g", "w_out", "norm2_g", "w_ff_gate", "w_ff_up", "w_ff_down", "final_g"]
    return (loss_total, grad_x[None], *[result[kind, n] for kind in ("grad", "delta", "new_m", "new_v") for n in order])
```

```python
import functools

import jax
import jax.numpy as jnp
from jax import lax
from jax.experimental import pallas as pl
from jax.experimental.pallas import tpu as pltpu

F32 = jnp.float32
MXU = jnp.bfloat16
WIRE = jnp.bfloat16

D = 1024
NH = 4
DV = 256
DQK = 128
L = 64
NMETA = 16
PADR = 512
LM = 256
CH0 = PADR // LM - 1
NPADROWS = PADR - NMETA
RANK = 16
DFF = 2816
EPS = 1e-6
TAU = 16.0
QSCALE = DQK ** -0.5
NEG = -1e30
NDEV = 8

MV0, GQ0, GK0, GV0, MO0, GR0, QK0, GM0, GG0, SM0 = 0, 1024, 1536, 2048, 3072, 4096, 5120, 6144, 7168, 8192
NP = 8320
NPROJ = 8216

ADAM_LR, ADAM_B1, ADAM_B2, ADAM_EPS, ADAM_WD, ADAM_STEP = 0.001, 0.9, 0.999, 1e-08, 0.01, 10

VMEM_LIMIT = 56 * 1024 * 1024
TM = 512


def _cp(sem):
    return pltpu.CompilerParams(dimension_semantics=sem, vmem_limit_bytes=VMEM_LIMIT)


def _sigmoid(x):
    return 1.0 / (1.0 + jnp.exp(-x))


def _log_sigmoid(x):
    return jnp.minimum(x, 0.0) - jnp.log1p(jnp.exp(-jnp.abs(x)))


def _dot(a, b, ca, cb):
    return lax.dot_general(a.astype(MXU), b.astype(MXU), (((ca,), (cb,)), ((), ())), preferred_element_type=F32)


def _dot_exact(a, b):
    return lax.dot_general(a, b, (((1,), (0,)), ((), ())), precision=lax.Precision.HIGHEST,
                           preferred_element_type=F32)


def _rb(tm, w, cb):
    return pl.BlockSpec((tm, w), lambda i: (i, cb))


def _const(shape):
    nd = len(shape)
    return pl.BlockSpec(shape, lambda i: (0,) * nd)


def _pick(n, target):
    if n <= target:
        return n
    best = None
    for t in range(128, target + 1, 128):
        if n % t == 0:
            best = t
    assert best is not None, (n, target)
    return best


def matmul(a, b, mode, name, add=None, out_dtype=F32, tm=512, tn=1664, tk=1024, order=None):
    if mode == "nn":
        (M, K), (K2, N) = a.shape, b.shape
    elif mode == "nt":
        (M, K), (N, K2) = a.shape, b.shape
    else:
        (K, M), (K2, N) = a.shape, b.shape
    assert K == K2, (a.shape, b.shape, mode)
    tm, tn, tk = _pick(M, tm), _pick(N, tn), _pick(K, tk)
    nk = K // tk
    assert nk == 1 or out_dtype == F32
    ca, cb = {"nn": (1, 0), "nt": (1, 1), "tn": (0, 0)}[mode]
    a_spec = {"nn": pl.BlockSpec((tm, tk), lambda j, i, k: (i, k)),
              "nt": pl.BlockSpec((tm, tk), lambda j, i, k: (i, k)),
              "tn": pl.BlockSpec((tk, tm), lambda j, i, k: (k, i))}[mode]
    b_spec = {"nn": pl.BlockSpec((tk, tn), lambda j, i, k: (k, j)),
              "nt": pl.BlockSpec((tn, tk), lambda j, i, k: (j, k)),
              "tn": pl.BlockSpec((tk, tn), lambda j, i, k: (k, j))}[mode]
    o_spec = pl.BlockSpec((tm, tn), lambda j, i, k: (i, j))
    has_add = add is not None

    def body(*refs):
        if order is not None:
            refs = refs[:-2] + refs[-1:]
        if has_add:
            a_ref, b_ref, add_ref, o_ref = refs
        else:
            a_ref, b_ref, o_ref = refs
            add_ref = None
        part = _dot(a_ref[...], b_ref[...], ca, cb)
        if nk == 1:
            if has_add:
                part = part + add_ref[...]
            o_ref[...] = part.astype(o_ref.dtype)
            return
        k = pl.program_id(2)

        @pl.when(k == 0)
        def _():
            o_ref[...] = part + add_ref[...] if has_add else part

        @pl.when(k > 0)
        def _():
            o_ref[...] += part

    in_specs = [a_spec, b_spec] + ([o_spec] if has_add else [])
    args = (a, b) + ((add,) if has_add else ())
    if order is not None:
        in_specs.append(pl.BlockSpec(order.shape, lambda j, i, k: (0, 0)))
        args += (order,)
    return pl.pallas_call(
        body, name=name, grid=(N // tn, M // tm, nk),
        in_specs=in_specs, out_specs=o_spec,
        out_shape=jax.ShapeDtypeStruct((M, N), out_dtype),
        compiler_params=_cp(("parallel", "parallel", "arbitrary")),
    )(*args)


def _h0_tile(i, x_ref, meta_ref):
    assert PADR == TM
    front = jnp.concatenate([jnp.zeros((NPADROWS, D), F32), meta_ref[...]], axis=0)
    return jnp.where(i == 0, front, x_ref[...])


_TOKENS = pl.BlockSpec((TM, D), lambda i: (jnp.maximum(i - 1, 0), 0))


def rms_fwd_input(x, meta, g, name):
    R = x.shape[0] + PADR

    def body(x_ref, meta_ref, g_ref, y_ref):
        xv = _h0_tile(pl.program_id(0), x_ref, meta_ref)
        r = lax.rsqrt(jnp.mean(xv * xv, axis=-1, keepdims=True) + EPS)
        y_ref[...] = (xv * r * g_ref[...]).astype(y_ref.dtype)

    return pl.pallas_call(
        body, name=name, grid=(R // TM,),
        in_specs=[_TOKENS, _const((NMETA, D)), _const((1, D))], out_specs=_rb(TM, D, 0),
        out_shape=jax.ShapeDtypeStruct((R, D), MXU), compiler_params=_cp(("parallel",)),
    )(x, meta, g)


def rms_bwd(dy, x, g, dres, name):
    R = x.shape[0]

    def body(dy_ref, x_ref, g_ref, dres_ref, dx_ref, dg_ref):
        i = pl.program_id(0)
        xv, dyv = x_ref[...], dy_ref[...]
        r = lax.rsqrt(jnp.mean(xv * xv, axis=-1, keepdims=True) + EPS)
        dyg = dyv * g_ref[...]
        dx_ref[...] = dres_ref[...] + r * dyg - xv * (r * r * r * jnp.mean(dyg * xv, axis=-1, keepdims=True))
        part = jnp.sum(dyv * xv * r, axis=0, keepdims=True)

        @pl.when(i == 0)
        def _():
            dg_ref[...] = part

        @pl.when(i > 0)
        def _():
            dg_ref[...] += part

    return pl.pallas_call(
        body, name=name, grid=(R // TM,),
        in_specs=[_rb(TM, D, 0), _rb(TM, D, 0), _const((1, D)), _rb(TM, D, 0)],
        out_specs=[_rb(TM, D, 0), _const((1, D))],
        out_shape=[jax.ShapeDtypeStruct((R, D), F32), jax.ShapeDtypeStruct((1, D), F32)],
        compiler_params=_cp(("arbitrary",)),
    )(dy, x, g, dres)


def dgrad_rms_bwd(a, w, x, g, dres, name, tm, tk):
    R, K = a.shape
    tm, tk = _pick(R, tm), _pick(K, tk)
    nk = K // tk

    def body(a_ref, w_ref, x_ref, g_ref, dres_ref, dx_ref, dg_ref):
        i, k = pl.program_id(0), pl.program_id(1)
        part = _dot(a_ref[...], w_ref[...], 1, 1)

        @pl.when(k == 0)
        def _():
            dx_ref[...] = part

        @pl.when(k > 0)
        def _():
            dx_ref[...] += part

        @pl.when(k == nk - 1)
        def _():
            xv, dyv = x_ref[...], dx_ref[...]
            r = lax.rsqrt(jnp.mean(xv * xv, axis=-1, keepdims=True) + EPS)
            dyg = dyv * g_ref[...]
            dx_ref[...] = dres_ref[...] + r * dyg - xv * (r * r * r * jnp.mean(dyg * xv, axis=-1, keepdims=True))
            gpart = jnp.sum(dyv * xv * r, axis=0, keepdims=True)

            @pl.when(i == 0)
            def _():
                dg_ref[...] = gpart

            @pl.when(i > 0)
            def _():
                dg_ref[...] += gpart

    row = pl.BlockSpec((tm, D), lambda i, k: (i, 0))
    return pl.pallas_call(
        body, name=name, grid=(R // tm, nk),
        in_specs=[pl.BlockSpec((tm, tk), lambda i, k: (i, k)), pl.BlockSpec((D, tk), lambda i, k: (0, k)), row,
                  pl.BlockSpec((1, D), lambda i, k: (0, 0)), row],
        out_specs=[row, pl.BlockSpec((1, D), lambda i, k: (0, 0))],
        out_shape=[jax.ShapeDtypeStruct((R, D), F32), jax.ShapeDtypeStruct((1, D), F32)],
        compiler_params=_cp(("arbitrary", "arbitrary")),
    )(a, w, x, g, dres)


def rms_bwd_input(dy, x, meta, g, dres, name):
    R = dy.shape[0]

    def body(dy_ref, x_ref, meta_ref, g_ref, dres_ref, dx_ref, dmeta_ref, dg_ref):
        i = pl.program_id(0)
        xv, dyv = _h0_tile(i, x_ref, meta_ref), dy_ref[...]
        r = lax.rsqrt(jnp.mean(xv * xv, axis=-1, keepdims=True) + EPS)
        dyg = dyv * g_ref[...]
        dx = dres_ref[...] + r * dyg - xv * (r * r * r * jnp.mean(dyg * xv, axis=-1, keepdims=True))
        dx_ref[...] = dx
        part = jnp.sum(dyv * xv * r, axis=0, keepdims=True)

        @pl.when(i == 0)
        def _():
            dg_ref[...] = part
            dmeta_ref[...] = dx[NPADROWS:PADR]

        @pl.when(i > 0)
        def _():
            dg_ref[...] += part

    return pl.pallas_call(
        body, name=name, grid=(R // TM,),
        in_specs=[_rb(TM, D, 0), _TOKENS, _const((NMETA, D)), _const((1, D)), _rb(TM, D, 0)],
        out_specs=[_TOKENS, _const((NMETA, D)), _const((1, D))],
        out_shape=[jax.ShapeDtypeStruct((R - PADR, D), F32), jax.ShapeDtypeStruct((NMETA, D), F32),
                   jax.ShapeDtypeStruct((1, D), F32)],
        compiler_params=_cp(("arbitrary",)),
    )(dy, x, meta, g, dres)


def _shift_down(cur, prev8, s):
    tm = cur.shape[0]
    rolled = pltpu.roll(cur, s, 0)
    rows8 = lax.broadcasted_iota(jnp.int32, (8, cur.shape[1]), 0)
    head = jnp.where(rows8 < s, pltpu.roll(prev8, s, 0), rolled[0:8])
    return jnp.concatenate([head, rolled[8:tm]], axis=0)


def _shift_up(cur, next8, s):
    tm = cur.shape[0]
    rolled = pltpu.roll(cur, tm - s, 0)
    rows8 = lax.broadcasted_iota(jnp.int32, (8, cur.shape[1]), 0)
    tail = jnp.where(rows8 >= 8 - s, pltpu.roll(next8, 8 - s, 0), rolled[tm - 8:tm])
    return jnp.concatenate([rolled[0:tm - 8], tail], axis=0)


def prep_fwd(proj, conv_w, conv_b, gb_row):
    R = proj.shape[0]
    t8 = TM // 8

    def body(x_ref, halo_ref, sm_ref, w_ref, b_ref, gb_ref, c_ref, qk_ref, gl_ref):
        i = pl.program_id(0)
        x = x_ref[...]
        halo = halo_ref[...]
        w = w_ref[...]
        c = x * w[3:4, :] + b_ref[...]
        for s in (1, 2, 3):
            c = c + _shift_down(x, halo, s) * w[3 - s:4 - s, :]
        c_ref[...] = c
        qk_ref[...] = c * _sigmoid(c)
        z = sm_ref[...] + gb_ref[...]
        lane = lax.broadcasted_iota(jnp.int32, z.shape, 1)
        row = lax.broadcasted_iota(jnp.int32, z.shape, 0) + i * TM
        valid = row >= NPADROWS
        logi = jnp.where(valid, z, NEG)
        logf = jnp.where(valid, _log_sigmoid(z), 0.0)
        gl_ref[...] = jnp.where(lane < 4, logi, jnp.where(lane < 8, logf, 0.0))

    return pl.pallas_call(
        body, name="prep_fwd", grid=(R // TM,),
        in_specs=[_rb(TM, 1024, QK0 // 1024),
                  pl.BlockSpec((8, 1024), lambda i: (jnp.maximum(i * t8 - 1, 0), QK0 // 1024)),
                  _rb(TM, 128, SM0 // 128), _const((4, 1024)), _const((1, 1024)), _const((1, 128))],
        out_specs=[_rb(TM, 1024, 0), _rb(TM, 1024, 0), _rb(TM, 128, 0)],
        out_shape=[jax.ShapeDtypeStruct((R, 1024), F32), jax.ShapeDtypeStruct((R, 1024), F32),
                   jax.ShapeDtypeStruct((R, 128), F32)],
        compiler_params=_cp(("parallel",)),
    )(proj, proj, proj, conv_w, conv_b, gb_row)


TMH = 256


def _head_norm(h):
    parts = []
    for j in range(NH):
        hj = h[:, j * DV:(j + 1) * DV]
        parts.append(jnp.broadcast_to(lax.rsqrt(jnp.mean(hj * hj, axis=-1, keepdims=True) + EPS), hj.shape))
    return jnp.concatenate(parts, axis=1)


def _head_mean(x):
    parts = []
    for j in range(NH):
        xj = x[:, j * DV:(j + 1) * DV]
        parts.append(jnp.broadcast_to(jnp.mean(xj, axis=-1, keepdims=True), xj.shape))
    return jnp.concatenate(parts, axis=1)


def merge_fwd(hm, hgl, w_bm, w_bg, proj, m_head_g, g_head_g):
    R = hm.shape[0]

    def body(hm_ref, hg_ref, wm_ref, wg_ref, mo_ref, gr_ref, gm_ref, gg_ref, mg_ref, gg2_ref,
             ym_ref, yg_ref, bm_ref, bg_ref, o_ref):
        hmv, hgv, gr = hm_ref[...], hg_ref[...], gr_ref[...]
        ym = (hmv * _head_norm(hmv) * mg_ref[...] * _sigmoid(mo_ref[...])).astype(ym_ref.dtype)
        yg = (hgv * _head_norm(hgv) * gg2_ref[...] * gr * _sigmoid(gr)).astype(yg_ref.dtype)
        ym_ref[...] = ym
        yg_ref[...] = yg
        bm = _dot(ym, wm_ref[...], 1, 0)
        bg = _dot(yg, wg_ref[...], 1, 0)
        bm_ref[...] = bm
        bg_ref[...] = bg
        o_ref[...] = (_sigmoid(gm_ref[...]) * bm + _sigmoid(gg_ref[...]) * bg).astype(o_ref.dtype)

    row = _rb(TMH, D, 0)
    return pl.pallas_call(
        body, name="merge_fwd", grid=(R // TMH,),
        in_specs=[row, row, _const((D, D)), _const((D, D)), _rb(TMH, D, MO0 // D), _rb(TMH, D, GR0 // D),
                  _rb(TMH, D, GM0 // D), _rb(TMH, D, GG0 // D), _const((1, D)), _const((1, D))],
        out_specs=[row, row, row, row, row],
        out_shape=[jax.ShapeDtypeStruct((R, D), MXU), jax.ShapeDtypeStruct((R, D), MXU),
                   jax.ShapeDtypeStruct((R, D), F32), jax.ShapeDtypeStruct((R, D), F32),
                   jax.ShapeDtypeStruct((R, D), MXU)],
        compiler_params=_cp(("parallel",)),
    )(hm, hgl, w_bm, w_bg, proj, proj, proj, proj, m_head_g, g_head_g)


def branch_dgrad(dyb, w, h, proj, gate_col, head_g, swish, dproj, name):
    R = h.shape[0]

    def body(d_ref, w_ref, h_ref, gt_ref, hg_ref, dp_in, dp_ref, dh_ref, dhg_ref):
        del dp_in
        i = pl.program_id(0)
        dy = _dot(d_ref[...], w_ref[...], 1, 1)
        hv, gt, gain = h_ref[...], gt_ref[...], hg_ref[...]
        rn = _head_norm(hv)
        sg = _sigmoid(gt)
        if swish:
            dno = dy * gt * sg
            dgate = dy * hv * rn * gain * sg * (1.0 + gt * (1.0 - sg))
        else:
            dno = dy * sg
            dgate = dy * hv * rn * gain * sg * (1.0 - sg)
        dp_ref[...] = dgate.astype(dp_ref.dtype)
        dnog = dno * gain
        dh_ref[...] = rn * dnog - hv * (rn * rn * rn * _head_mean(dnog * hv))
        part = jnp.sum(dno * hv * rn, axis=0, keepdims=True)

        @pl.when(i == 0)
        def _():
            dhg_ref[...] = part

        @pl.when(i > 0)
        def _():
            dhg_ref[...] += part

    row = _rb(TMH, D, 0)
    return pl.pallas_call(
        body, name=name, grid=(R // TMH,),
        in_specs=[row, _const((D, D)), row, _rb(TMH, D, gate_col // D), _const((1, D)),
                  pl.BlockSpec(memory_space=pl.ANY)],
        out_specs=[_rb(TMH, D, gate_col // D), row, _const((1, D))],
        out_shape=[jax.ShapeDtypeStruct((R, NP), MXU), jax.ShapeDtypeStruct((R, D), F32),
                   jax.ShapeDtypeStruct((1, D), F32)],
        input_output_aliases={5: 0},
        compiler_params=_cp(("arbitrary",)),
    )(dyb, w, h, proj, head_g, dproj)


def merge_bwd(dh1, w_out, bm, bg, proj):
    R = bm.shape[0]

    def body(dh_ref, w_ref, bm_ref, bg_ref, gm_ref, gg_ref, dbm_ref, dbg_ref, dp_ref):
        dm = _dot(dh_ref[...], w_ref[...], 1, 1)
        sm, sg = _sigmoid(gm_ref[...]), _sigmoid(gg_ref[...])
        dbm_ref[...] = (dm * sm).astype(dbm_ref.dtype)
        dbg_ref[...] = (dm * sg).astype(dbg_ref.dtype)
        dp_ref[:, 0:D] = (dm * bm_ref[...] * sm * (1.0 - sm)).astype(dp_ref.dtype)
        dp_ref[:, D:2 * D] = (dm * bg_ref[...] * sg * (1.0 - sg)).astype(dp_ref.dtype)

    return pl.pallas_call(
        body, name="merge_bwd", grid=(R // TM,),
        in_specs=[_rb(TM, D, 0), _const((D, D)), _rb(TM, D, 0), _rb(TM, D, 0), _rb(TM, D, GM0 // D),
                  _rb(TM, D, GG0 // D)],
        out_specs=[_rb(TM, D, 0), _rb(TM, D, 0), _rb(TM, 2 * D, GM0 // (2 * D))],
        out_shape=[jax.ShapeDtypeStruct((R, D), MXU), jax.ShapeDtypeStruct((R, D), MXU),
                   jax.ShapeDtypeStruct((R, NP), MXU)],
        compiler_params=_cp(("parallel",)),
    )(dh1, w_out, bm, bg, proj, proj)


TF = DFF // 2
TMF = 768


def interleave_gu(gate, up):
    return jnp.concatenate([gate[:, :TF], up[:, :TF], gate[:, TF:], up[:, TF:]], axis=1)


def split_gu(gu):
    return (jnp.concatenate([gu[:, 0:TF], gu[:, 2 * TF:3 * TF]], axis=1),
            jnp.concatenate([gu[:, TF:2 * TF], gu[:, 3 * TF:]], axis=1))


def ff_in_fwd(hn, w_gu):
    R = hn.shape[0]

    def body(x_ref, w_ref, au_ref, ff_ref):
        au = _dot(x_ref[...], w_ref[...], 1, 0)
        au_ref[...] = au.astype(au_ref.dtype)
        a = au[:, :TF]
        ff_ref[...] = (a * _sigmoid(a) * au[:, TF:]).astype(ff_ref.dtype)

    tm = _pick(R, TMF)
    return pl.pallas_call(
        body, name="ff_in_fwd", grid=(DFF // TF, R // tm),
        in_specs=[pl.BlockSpec((tm, D), lambda j, i: (i, 0)), pl.BlockSpec((D, 2 * TF), lambda j, i: (0, j))],
        out_specs=[pl.BlockSpec((tm, 2 * TF), lambda j, i: (i, j)), pl.BlockSpec((tm, TF), lambda j, i: (i, j))],
        out_shape=[jax.ShapeDtypeStruct((R, 2 * DFF), MXU), jax.ShapeDtypeStruct((R, DFF), MXU)],
        compiler_params=_cp(("parallel", "parallel")),
    )(hn, w_gu)


def ff_down_dgrad(dh2, w_down, au):
    R = dh2.shape[0]

    def body(d_ref, w_ref, au_ref, o_ref):
        dff = _dot(d_ref[...], w_ref[...], 1, 1)
        a = au_ref[:, :TF].astype(F32)
        u = au_ref[:, TF:].astype(F32)
        s = _sigmoid(a)
        o_ref[:, :TF] = (dff * u * s * (1.0 + a * (1.0 - s))).astype(o_ref.dtype)
        o_ref[:, TF:] = (dff * a * s).astype(o_ref.dtype)

    tm = _pick(R, TMF)
    return pl.pallas_call(
        body, name="ff_down_dgrad", grid=(DFF // TF, R // tm),
        in_specs=[pl.BlockSpec((tm, D), lambda j, i: (i, 0)), pl.BlockSpec((TF, D), lambda j, i: (j, 0)),
                  pl.BlockSpec((tm, 2 * TF), lambda j, i: (i, j))],
        out_specs=pl.BlockSpec((tm, 2 * TF), lambda j, i: (i, j)),
        out_shape=jax.ShapeDtypeStruct((R, 2 * DFF), MXU),
        compiler_params=_cp(("parallel", "parallel")),
    )(dh2, w_down, au)


def out_proj_norm(merged, w, x, meta, g):
    R = merged.shape[0]

    def body(m_ref, w_ref, x_ref, meta_ref, g_ref, h_ref, n_ref):
        hv = _dot(m_ref[...], w_ref[...], 1, 0) + _h0_tile(pl.program_id(0), x_ref, meta_ref)
        h_ref[...] = hv
        r = lax.rsqrt(jnp.mean(hv * hv, axis=-1, keepdims=True) + EPS)
        n_ref[...] = (hv * r * g_ref[...]).astype(n_ref.dtype)

    return pl.pallas_call(
        body, name="out_fwd", grid=(R // TM,),
        in_specs=[_rb(TM, D, 0), _const((D, D)), _TOKENS, _const((NMETA, D)), _const((1, D))],
        out_specs=[_rb(TM, D, 0), _rb(TM, D, 0)],
        out_shape=[jax.ShapeDtypeStruct((R, D), F32), jax.ShapeDtypeStruct((R, D), MXU)],
        compiler_params=_cp(("parallel",)),
    )(merged, w, x, meta, g)


def ff_down_loss(ff, w_down, h1, gf, target):
    R = ff.shape[0]
    assert PADR == TM

    def body(f_ref, w_ref, h1_ref, g_ref, t_ref, dh_ref, loss_ref, dg_ref):
        i = pl.program_id(0)
        hv = _dot(f_ref[...], w_ref[...], 1, 0) + h1_ref[...]
        r = lax.rsqrt(jnp.mean(hv * hv, axis=-1, keepdims=True) + EPS)
        g = g_ref[...]
        live = (i >= 1).astype(F32)
        e = (hv * r * g - t_ref[...]) * live
        dy = e * (1.0 / D)
        dyg = dy * g
        dh_ref[...] = r * dyg - hv * (r * r * r * jnp.mean(dyg * hv, axis=-1, keepdims=True))
        lpart = jnp.zeros((1, 128), F32) + 0.5 * jnp.sum(jnp.sum(e * e, axis=1, keepdims=True), axis=0, keepdims=True) * (1.0 / D)
        gpart = jnp.sum(dy * hv * r, axis=0, keepdims=True)

        @pl.when(i == 0)
        def _():
            loss_ref[...] = lpart
            dg_ref[...] = gpart

        @pl.when(i > 0)
        def _():
            loss_ref[...] += lpart
            dg_ref[...] += gpart

    return pl.pallas_call(
        body, name="ff_down_loss", grid=(R // TM,),
        in_specs=[_rb(TM, DFF, 0), _const((DFF, D)), _rb(TM, D, 0), _const((1, D)),
                  pl.BlockSpec((TM, D), lambda i: (jnp.maximum(i - 1, 0), 0))],
        out_specs=[_rb(TM, D, 0), _const((1, 128)), _const((1, D))],
        out_shape=[jax.ShapeDtypeStruct((R, D), F32), jax.ShapeDtypeStruct((1, 128), F32),
                   jax.ShapeDtypeStruct((1, D), F32)],
        compiler_params=_cp(("arbitrary",)),
    )(ff, w_down, h1, gf, target)


def conv_bwd(dc, proj, conv_w, dproj):
    R = dc.shape[0]
    t8 = TM // 8
    nt = R // TM

    def body(dc_ref, nxt_ref, x_ref, prv_ref, w_ref, dp_in, dp_ref, dw_ref):
        del dp_in
        i = pl.program_id(0)
        dcv = dc_ref[...]
        nxt = nxt_ref[...] * (i < nt - 1).astype(F32)
        x = x_ref[...]
        prv = prv_ref[...]
        w = w_ref[...]
        dx = dcv * w[3:4, :]
        rows = [None] * 4
        rows[3] = jnp.sum(dcv * x, axis=0, keepdims=True)
        for s in (1, 2, 3):
            dx = dx + _shift_up(dcv, nxt, s) * w[3 - s:4 - s, :]
            rows[3 - s] = jnp.sum(dcv * _shift_down(x, prv, s), axis=0, keepdims=True)
        dp_ref[...] = dx.astype(dp_ref.dtype)
        part = jnp.concatenate(rows + [jnp.sum(dcv, axis=0, keepdims=True), jnp.zeros((3, 1024), F32)], axis=0)

        @pl.when(i == 0)
        def _():
            dw_ref[...] = part

        @pl.when(i > 0)
        def _():
            dw_ref[...] += part

    return pl.pallas_call(
        body, name="conv_bwd", grid=(nt,),
        in_specs=[_rb(TM, 1024, 0),
                  pl.BlockSpec((8, 1024), lambda i: (jnp.minimum((i + 1) * t8, nt * t8 - 1), 0)),
                  _rb(TM, 1024, QK0 // 1024),
                  pl.BlockSpec((8, 1024), lambda i: (jnp.maximum(i * t8 - 1, 0), QK0 // 1024)),
                  _const((4, 1024)), pl.BlockSpec(memory_space=pl.ANY)],
        out_specs=[_rb(TM, 1024, QK0 // 1024), _const((8, 1024))],
        out_shape=[jax.ShapeDtypeStruct((R, NP), MXU), jax.ShapeDtypeStruct((8, 1024), F32)],
        input_output_aliases={5: 0},
        compiler_params=_cp(("arbitrary",)),
    )(dc, dc, proj, proj, conv_w, dproj)


def small_bwd(dgl, dga, proj, gb_row, dproj):
    R = dgl.shape[0]

    def body(dgl_ref, dga_ref, sm_ref, gb_ref, dp_in, dp_ref, dgb_ref):
        del dp_in
        i = pl.program_id(0)
        z = sm_ref[...] + gb_ref[...]
        lane = lax.broadcasted_iota(jnp.int32, z.shape, 1)
        row = lax.broadcasted_iota(jnp.int32, z.shape, 0) + i * TM
        valid = row >= NPADROWS
        dgl_v = dgl_ref[...]
        dgate = jnp.where(valid, jnp.where(lane < 4, dgl_v, dgl_v * _sigmoid(-z)), 0.0)
        ds = jnp.where(lane < 8, dgate, dga_ref[...])
        dp_ref[...] = ds.astype(dp_ref.dtype)
        part = jnp.sum(jnp.where(lane < 8, dgate, 0.0), axis=0, keepdims=True)

        @pl.when(i == 0)
        def _():
            dgb_ref[...] = part

        @pl.when(i > 0)
        def _():
            dgb_ref[...] += part

    return pl.pallas_call(
        body, name="small_bwd", grid=(R // TM,),
        in_specs=[_rb(TM, 128, 0), _rb(TM, 128, 0), _rb(TM, 128, SM0 // 128), _const((1, 128)),
                  pl.BlockSpec(memory_space=pl.ANY)],
        out_specs=[_rb(TM, 128, SM0 // 128), _const((1, 128))],
        out_shape=[jax.ShapeDtypeStruct((R, NP), MXU), jax.ShapeDtypeStruct((1, 128), F32)],
        input_output_aliases={4: 0},
        compiler_params=_cp(("arbitrary",)),
    )(dgl, dga, proj, gb_row, dproj)


def _masks(n=L):
    r = lax.broadcasted_iota(jnp.int32, (n, n), 0)
    c = lax.broadcasted_iota(jnp.int32, (n, n), 1)
    return r >= c, r == c, r


def _to_row(col, eye):
    return jnp.sum(jnp.where(eye, col, 0.0), axis=0, keepdims=True)


def _to_col(row, eye):
    return jnp.sum(jnp.where(eye, row, 0.0), axis=1, keepdims=True)


def _mlstm_chunk(q, k, logi_c, logf_c, m, n):
    tril, eye, _ = _masks(q.shape[0])
    logi_r, logf_r = _to_row(logi_c, eye), _to_row(logf_c, eye)
    b_c = jnp.sum(jnp.where(tril, logf_r, 0.0), axis=1, keepdims=True)
    b_r = _to_row(b_c, eye)
    g = jnp.sum(logf_c, axis=0, keepdims=True)
    dmat = jnp.where(tril, b_c - b_r + logi_r, NEG)
    mrow = jnp.maximum(b_c + m, jnp.max(dmat, axis=1, keepdims=True))
    dm = jnp.exp(dmat - mrow)
    s = _dot(q, k, 1, 1)
    w = dm * s
    a_in = jnp.exp(b_c + m - mrow)
    qn = jnp.sum(q * n, axis=1, keepdims=True)
    den = a_in * qn + jnp.sum(w, axis=1, keepdims=True)
    floor = jnp.exp(-mrow)
    nrm = jnp.maximum(jnp.abs(den), floor)
    wlog_c = g - b_c + logi_c
    m_new = jnp.maximum(g + m, jnp.max(wlog_c, axis=0, keepdims=True))
    a_st = jnp.exp(g + m - m_new)
    w_c = jnp.exp(wlog_c - m_new)
    return dict(b_c=b_c, g=g, dm=dm, s=s, w=w, a_in=a_in, qn=qn, den=den, floor=floor, nrm=nrm,
                m_new=m_new, a_st=a_st, w_c=w_c, tril=tril, eye=eye)


def _mlstm_head_fwd(h, glv, qk_ref, v_ref, hm_ref, cs_ref, nm_ref, c_s, nm_s):
    q = qk_ref[:, h * DQK:(h + 1) * DQK] * QSCALE
    k = qk_ref[:, 512 + h * DQK:512 + (h + 1) * DQK]
    v = v_ref[:, h * DV:(h + 1) * DV]
    C = c_s[h]
    n = nm_s[h, 0:1, :]
    m = nm_s[h, 1:2, 0:1]
    f = _mlstm_chunk(q, k, glv[:, h:h + 1], glv[:, 4 + h:5 + h], m, n)
    num = f["a_in"] * _dot(q, C, 1, 1) + _dot(f["w"], v, 1, 0)
    hh = num / f["nrm"]
    cs_ref[0, h] = C
    nm_ref[0, h] = nm_s[h]
    c_s[h] = f["a_st"] * C + _dot(f["w_c"] * v, k, 0, 0)
    n_new = f["a_st"] * n + jnp.sum(f["w_c"] * k, axis=0, keepdims=True)
    rowi = lax.broadcasted_iota(jnp.int32, (8, DQK), 0)
    nm_s[h] = jnp.where(rowi == 0, n_new, jnp.where(rowi == 1, f["m_new"], 0.0))
    hm_ref[:, h * DV:(h + 1) * DV] = hh


def _mlstm_bwd_parts(dh_ref, hm_ref, qk_ref, cp_ref, v_ref, gl_ref, cs_ref, nm_ref,
                     dp_ref, dc_ref, dgl_ref, dc_s, dn_s):
        def init():
            dc_s[...] = jnp.zeros_like(dc_s)
            dn_s[...] = jnp.zeros_like(dn_s)

        def zero():
            dp_ref[...] = jnp.zeros_like(dp_ref)
            dc_ref[...] = jnp.zeros_like(dc_ref)
            dgl_ref[...] = jnp.zeros_like(dgl_ref)

        def compute():
            glv = gl_ref[...]
            lane = lax.broadcasted_iota(jnp.int32, (LM, 128), 1)
            dgl = jnp.zeros((LM, 128), F32)
            for h in range(NH):
                sl = slice(h * DV, (h + 1) * DV)
                sq = slice(h * DQK, (h + 1) * DQK)
                sk = slice(512 + h * DQK, 512 + (h + 1) * DQK)
                hh = hm_ref[:, sl]
                dh = dh_ref[:, sl]
                q = qk_ref[:, sq] * QSCALE
                k = qk_ref[:, sk]
                v = v_ref[:, sl]
                C = cs_ref[0, h]
                n = nm_ref[0, h, 0:1, :]
                m = nm_ref[0, h, 1:2, 0:1]
                f = _mlstm_chunk(q, k, glv[:, h:h + 1], glv[:, 4 + h:5 + h], m, n)
                eye = f["eye"]
                a_in, nrm, den, w = f["a_in"], f["nrm"], f["den"], f["w"]
                dnum = dh / nrm
                dnrm = -jnp.sum(dh * hh, axis=1, keepdims=True) / nrm
                dden = jnp.where(jnp.abs(den) >= f["floor"], dnrm * jnp.sign(den), 0.0)
                dw = _dot(dnum, v, 1, 1) + dden
                dv = _dot(w, dnum, 0, 0)
                ds = dw * f["dm"]
                e = dw * w
                qc = _dot(q, C, 1, 1)
                dq = _dot(ds, k, 1, 0) + a_in * _dot(dnum, C, 1, 0) + (a_in * dden) * n
                dk = _dot(ds, q, 0, 0)
                dC_in = _dot(a_in * dnum, q, 0, 0)
                dn_in = jnp.sum((a_in * dden) * q, axis=0, keepdims=True)
                da_in = jnp.sum(dnum * qc, axis=1, keepdims=True) + dden * f["qn"]
                col_e = _to_col(jnp.sum(e, axis=0, keepdims=True), eye)
                db = jnp.sum(e, axis=1, keepdims=True) + da_in * a_in - col_e
                dlogi = col_e
                dCp = dc_s[h]
                dnp = dn_s[h, 0:1, :]
                a_st, w_c = f["a_st"], f["w_c"]
                da_st = (jnp.sum(jnp.sum(dCp * C, axis=1, keepdims=True), axis=0, keepdims=True)
                         + jnp.sum(dnp * n, axis=1, keepdims=True))
                vdc = _dot(v, dCp, 1, 0)
                dw_c = jnp.sum((vdc + dnp) * k, axis=1, keepdims=True)
                dv = dv + w_c * _dot(k, dCp, 1, 1)
                dk = dk + w_c * (vdc + dnp)
                fw = dw_c * w_c
                dg = jnp.sum(fw, axis=0, keepdims=True) + da_st * a_st
                db = db - fw
                dlogi = dlogi + fw
                rowc = lax.broadcasted_iota(jnp.int32, (LM, 1), 0)
                db = db + jnp.where(rowc == LM - 1, dg, 0.0)
                triu = lax.broadcasted_iota(jnp.int32, (LM, LM), 1) >= lax.broadcasted_iota(jnp.int32, (LM, LM), 0)
                dlogf = jnp.sum(jnp.where(triu, _to_row(db, eye), 0.0), axis=1, keepdims=True)
                dc_s[h] = a_st * dCp + dC_in
                dn_new = a_st * dnp + dn_in
                dn_s[h] = jnp.zeros((8, DQK), F32) + dn_new
                cq, ck = cp_ref[:, sq], cp_ref[:, sk]
                s_q, s_k = _sigmoid(cq), _sigmoid(ck)
                dc_ref[:, sq] = dq * QSCALE * s_q * (1.0 + cq * (1.0 - s_q))
                dc_ref[:, sk] = dk * s_k * (1.0 + ck * (1.0 - s_k))
                dp_ref[:, sl] = dv.astype(dp_ref.dtype)
                dgl = jnp.where(lane == h, dlogi, jnp.where(lane == 4 + h, dlogf, dgl))
            dgl_ref[...] = dgl

        return init, zero, compute


def _gla_logs(sm, a2p, b2, valid):
    za = _dot(sm, a2p, 1, 0) + b2
    return za, jnp.where(valid, _log_sigmoid(za) * (1.0 / TAU), 0.0)


def _valid_rows(c, width):
    row = lax.broadcasted_iota(jnp.int32, (L, width), 0) + c * L
    return row >= NPADROWS


def _gla_chunk(q, k, la):
    tril, _, _ = _masks()
    bc = _dot_exact(tril.astype(F32), la)
    btot = jnp.sum(la, axis=0, keepdims=True)
    ebc = jnp.exp(bc)
    qd = q * ebc
    ki = k * jnp.exp(-bc)
    ke = k * jnp.exp(btot - bc)
    att = jnp.where(tril, _dot(qd, ki, 1, 1), 0.0)
    return dict(tril=tril, bc=bc, btot=btot, ebc=ebc, qd=qd, ki=ki, ke=ke, att=att)


def _col128(row):
    r = lax.broadcasted_iota(jnp.int32, (DQK, DQK), 0)
    c = lax.broadcasted_iota(jnp.int32, (DQK, DQK), 1)
    return jnp.sum(jnp.where(r == c, row, 0.0), axis=1, keepdims=True)


def _gla_fwd_parts(c, q_ref, k_ref, v_ref, sm_ref, a2_ref, b2_ref, hgl_ref, ss_ref, s_s):
        def init():
            s_s[...] = jnp.zeros_like(s_s)

        def zero():
            hgl_ref[...] = jnp.zeros_like(hgl_ref)
            ss_ref[...] = jnp.zeros_like(ss_ref)

        def compute():
            for s in range(LM // L):
                rows = pl.ds(s * L, L)
                chunk(LM // L * c + s, q_ref.at[rows], k_ref.at[rows], v_ref.at[rows], sm_ref.at[rows],
                      hgl_ref.at[rows], ss_ref.at[pl.ds(s, 1)])

        def chunk(c, q_ref, k_ref, v_ref, sm_ref, hgl_ref, ss_ref):
            _, loga = _gla_logs(sm_ref[...], a2_ref[...], b2_ref[...], _valid_rows(c, 512))
            for h in range(NH):
                sq = slice(h * DQK, (h + 1) * DQK)
                sl = slice(h * DV, (h + 1) * DV)
                q = q_ref[:, sq] * QSCALE
                k = k_ref[:, sq]
                v = v_ref[:, sl]
                S = s_s[h]
                f = _gla_chunk(q, k, loga[:, sq])
                o = _dot(f["att"], v, 1, 0) + _dot(f["qd"], S, 1, 0)
                ss_ref[0, h] = S
                s_s[h] = _col128(jnp.exp(f["btot"])) * S + _dot(f["ke"], v, 0, 0)
                hgl_ref[:, sl] = o

        return init, zero, compute


def _gla_bwd_parts(c, do_ref, q_ref, k_ref, v_ref, sm_ref, a2_ref, b2_ref, ss_ref,
                   dp_ref, dga_ref, da2_ref, db2_ref, ds_s):
        def init():
            ds_s[...] = jnp.zeros_like(ds_s)
            da2_ref[...] = jnp.zeros_like(da2_ref)
            db2_ref[...] = jnp.zeros_like(db2_ref)

        def zero():
            dp_ref[...] = jnp.zeros_like(dp_ref)
            dga_ref[...] = jnp.zeros_like(dga_ref)

        def compute():
            for s in reversed(range(LM // L)):
                rows = pl.ds(s * L, L)
                chunk(LM // L * c + s, do_ref.at[rows], q_ref.at[rows], k_ref.at[rows], v_ref.at[rows],
                      sm_ref.at[rows], ss_ref.at[pl.ds(s, 1)], dp_ref.at[rows], dga_ref.at[rows])

        def chunk(c, do_ref, q_ref, k_ref, v_ref, sm_ref, ss_ref, dp_ref, dga_ref):
            valid = _valid_rows(c, 512)
            sm = sm_ref[...]
            za, loga = _gla_logs(sm, a2_ref[...], b2_ref[...], valid)
            dloga = []
            for h in range(NH):
                sq = slice(h * DQK, (h + 1) * DQK)
                sl = slice(h * DV, (h + 1) * DV)
                do = do_ref[:, sl]
                q = q_ref[:, sq] * QSCALE
                k = k_ref[:, sq]
                v = v_ref[:, sl]
                S = ss_ref[0, h]
                f = _gla_chunk(q, k, loga[:, sq])
                tril, qd, ki, ke = f["tril"], f["qd"], f["ki"], f["ke"]
                dSp = ds_s[h]
                datt = jnp.where(tril, _dot(do, v, 1, 1), 0.0)
                dqd = _dot(do, S, 1, 1) + _dot(datt, ki, 1, 0)
                dki = _dot(datt, qd, 0, 0)
                dv = _dot(f["att"], do, 0, 0) + _dot(ke, dSp, 1, 0)
                dke = _dot(v, dSp, 1, 1)
                ebt = jnp.exp(f["btot"])
                dbtot = jnp.sum(dke * ke, axis=0, keepdims=True) + ebt * _to_row128(jnp.sum(dSp * S, axis=1, keepdims=True))
                ds_s[h] = _dot(qd, do, 0, 0) + _col128(ebt) * dSp
                dq = dqd * f["ebc"]
                dk = dki * jnp.exp(-f["bc"]) + dke * jnp.exp(f["btot"] - f["bc"])
                dbc = dqd * qd - dki * ki - dke * ke
                rowc = lax.broadcasted_iota(jnp.int32, (L, DQK), 0)
                dbc = dbc + jnp.where(rowc == L - 1, dbtot, 0.0)
                triu = lax.broadcasted_iota(jnp.int32, (L, L), 1) >= lax.broadcasted_iota(jnp.int32, (L, L), 0)
                dloga.append(_dot_exact(triu.astype(F32), dbc))
                dp_ref[:, sq] = (dq * QSCALE).astype(dp_ref.dtype)
                dp_ref[:, 512 + h * DQK:512 + (h + 1) * DQK] = dk.astype(dp_ref.dtype)
                dp_ref[:, 1024 + h * DV:1024 + (h + 1) * DV] = dv.astype(dp_ref.dtype)
            dza = jnp.where(valid, jnp.concatenate(dloga, axis=1) * (1.0 / TAU) * _sigmoid(-za), 0.0)
            dga_ref[...] = _dot(dza, a2_ref[...], 1, 1)
            da2_ref[...] += _dot(sm, dza, 0, 0)
            db2_ref[...] += jnp.sum(dza, axis=0, keepdims=True)

        return init, zero, compute


def mix_fwd(qk, proj, gl, a2p, b2):
    R = qk.shape[0]
    NC = R // LM
    G = LM // L

    def body(qk_ref, mv_ref, gl_ref, gq_ref, gk_ref, gv_ref, sm_ref, a2_ref, b2_ref,
             hm_ref, cs_ref, nm_ref, hgl_ref, ss_ref, c_s, nm_s, s_s):
        c = pl.program_id(0)
        g_init, g_zero, g_compute = _gla_fwd_parts(c, gq_ref, gk_ref, gv_ref, sm_ref, a2_ref, b2_ref,
                                                   hgl_ref, ss_ref, s_s)

        @pl.when(c <= CH0)
        def _():
            c_s[...] = jnp.zeros_like(c_s)
            nm_s[...] = jnp.zeros_like(nm_s)
            g_init()

        @pl.when(c < CH0)
        def _():
            hm_ref[...] = jnp.zeros_like(hm_ref)
            cs_ref[...] = jnp.zeros_like(cs_ref)
            nm_ref[...] = jnp.zeros_like(nm_ref)
            g_zero()

        @pl.when(c >= CH0)
        def _():
            glv = gl_ref[...]
            for h in range(NH):
                _mlstm_head_fwd(h, glv, qk_ref, mv_ref, hm_ref, cs_ref, nm_ref, c_s, nm_s)
            g_compute()

    st_m = pl.BlockSpec((1, NH, DV, DQK), lambda c: (c, 0, 0, 0))
    st_n = pl.BlockSpec((1, NH, 8, DQK), lambda c: (c, 0, 0, 0))
    st_g = pl.BlockSpec((G, NH, DQK, DV), lambda c: (c, 0, 0, 0))
    return pl.pallas_call(
        body, name="mix_fwd", grid=(NC,),
        in_specs=[_rb(LM, 1024, 0), _rb(LM, 1024, MV0 // 1024), _rb(LM, 128, 0),
                  _rb(LM, 512, GQ0 // 512), _rb(LM, 512, GK0 // 512), _rb(LM, 1024, GV0 // 1024),
                  _rb(LM, 128, SM0 // 128), _const((128, 512)), _const((1, 512))],
        out_specs=[_rb(LM, 1024, 0), st_m, st_n, _rb(LM, 1024, 0), st_g],
        out_shape=[jax.ShapeDtypeStruct((R, 1024), F32),
                   jax.ShapeDtypeStruct((NC, NH, DV, DQK), F32), jax.ShapeDtypeStruct((NC, NH, 8, DQK), F32),
                   jax.ShapeDtypeStruct((R, 1024), F32),
                   jax.ShapeDtypeStruct((G * NC, NH, DQK, DV), F32)],
        scratch_shapes=[pltpu.VMEM((NH, DV, DQK), F32), pltpu.VMEM((NH, 8, DQK), F32),
                        pltpu.VMEM((NH, DQK, DV), F32)],
        compiler_params=_cp(("arbitrary",)),
    )(qk, proj, gl, proj, proj, proj, proj, a2p, b2)


def mix_bwd(dhm, hm, qk, cpre, proj, gl, cs, nm, dhg, a2p, b2, ss, dproj):
    R = qk.shape[0]
    NC = R // LM
    rev = lambda c: NC - 1 - c
    GW = GV0 + 1024 - GQ0

    def body(dhm_ref, hm_ref, qk_ref, cp_ref, mv_ref, gl_ref, cs_ref, nm_ref,
             dhg_ref, gq_ref, gk_ref, gv_ref, sm_ref, a2_ref, b2_ref, ss_ref, dp_in,
             dp_ref, dc_ref, dgl_ref, dga_ref, da2_ref, db2_ref, dc_s, dn_s, ds_s):
        del dp_in
        step = pl.program_id(0)
        c = NC - 1 - step
        m_init, m_zero, m_compute = _mlstm_bwd_parts(
            dhm_ref, hm_ref, qk_ref, cp_ref, mv_ref, gl_ref, cs_ref, nm_ref,
            dp_ref.at[:, 0:GQ0], dc_ref, dgl_ref, dc_s, dn_s)
        g_init, g_zero, g_compute = _gla_bwd_parts(
            c, dhg_ref, gq_ref, gk_ref, gv_ref, sm_ref, a2_ref, b2_ref, ss_ref,
            dp_ref.at[:, GQ0:GQ0 + GW], dga_ref, da2_ref, db2_ref, ds_s)

        @pl.when(step == 0)
        def _():
            m_init()
            g_init()

        @pl.when(c < CH0)
        def _():
            m_zero()
            g_zero()

        @pl.when(c >= CH0)
        def _():
            m_compute()
            g_compute()

    def rows(w, cb):
        return pl.BlockSpec((LM, w), lambda c: (rev(c), cb))

    return pl.pallas_call(
        body, name="mix_bwd", grid=(NC,),
        in_specs=[rows(1024, 0), rows(1024, 0), rows(1024, 0), rows(1024, 0), rows(1024, MV0 // 1024), rows(128, 0),
                  pl.BlockSpec((1, NH, DV, DQK), lambda c: (rev(c), 0, 0, 0)),
                  pl.BlockSpec((1, NH, 8, DQK), lambda c: (rev(c), 0, 0, 0)),
                  rows(1024, 0), rows(512, GQ0 // 512), rows(512, GK0 // 512),
                  rows(1024, GV0 // 1024), rows(128, SM0 // 128),
                  _const((128, 512)), _const((1, 512)),
                  pl.BlockSpec((LM // L, NH, DQK, DV), lambda c: (rev(c), 0, 0, 0)),
                  pl.BlockSpec(memory_space=pl.ANY)],
        out_specs=[rows(GQ0 + GW, 0), rows(1024, 0), rows(128, 0),
                   rows(128, 0), _const((128, 512)), _const((1, 512))],
        out_shape=[jax.ShapeDtypeStruct((R, NP), MXU), jax.ShapeDtypeStruct((R, 1024), F32),
                   jax.ShapeDtypeStruct((R, 128), F32),
                   jax.ShapeDtypeStruct((R, 128), F32), jax.ShapeDtypeStruct((128, 512), F32),
                   jax.ShapeDtypeStruct((1, 512), F32)],
        scratch_shapes=[pltpu.VMEM((NH, DV, DQK), F32), pltpu.VMEM((NH, 8, DQK), F32),
                        pltpu.VMEM((NH, DQK, DV), F32)],
        input_output_aliases={16: 0},
        compiler_params=_cp(("arbitrary",)),
    )(dhm, hm, qk, cpre, proj, gl, cs, nm, dhg, proj, proj, proj, proj, a2p, b2, ss, dproj)


def _to_row128(col):
    r = lax.broadcasted_iota(jnp.int32, (DQK, DQK), 0)
    c = lax.broadcasted_iota(jnp.int32, (DQK, DQK), 1)
    return jnp.sum(jnp.where(r == c, col, 0.0), axis=0, keepdims=True)


def local_step(x, target, meta, norm1_g, wp, conv_w, conv_b, m_gate_b, g_a2, g_a2_b, m_head_g, g_head_g,
               norm2_g, final_g, late_weights, send_early, send_wp, first_order=None):
    seq = x.shape[0]
    assert seq % TM == 0
    gb_row = jnp.zeros((1, 128), F32).at[0, 0:8].set(m_gate_b.reshape(8))
    a2p = jnp.zeros((128, 512), F32).at[8:8 + RANK].set(g_a2)
    mhg = m_head_g.reshape(1, 1024)
    ghg = g_head_g.reshape(1, 1024)

    xn = rms_fwd_input(x, meta, norm1_g, "rms1_fwd")
    proj = matmul(xn, wp, "nn", "proj_fwd", tm=1536, order=first_order)
    cpre, qk, gl = prep_fwd(proj, conv_w, conv_b, gb_row)
    hm, cs, nm, hgl, ss = mix_fwd(qk, proj, gl, a2p, g_a2_b)
    w_bm, w_bg, w_out, w_gu, w_down = late_weights(hgl)
    ym, yg, bm, bg, merged = merge_fwd(hm, hgl, w_bm, w_bg, proj, mhg, ghg)
    h1, hn = out_proj_norm(merged, w_out, x, meta, norm2_g)
    au, ff = ff_in_fwd(hn, w_gu)
    dh2, loss, d_final_g = ff_down_loss(ff, w_down, h1, final_g.reshape(1, D), target)

    d_w_down = matmul(ff, dh2, "tn", "ff_down_wgrad", tm=1408, tk=1536)
    dau = ff_down_dgrad(dh2, w_down, au)
    d_w_gu = matmul(hn, dau, "tn", "ff_in_wgrad", tm=1024, tn=1408, tk=1536)
    dh1, d_norm2_g = dgrad_rms_bwd(dau, w_gu, h1, norm2_g, dh2, "ff_in_dgrad", 768, 1408)

    d_w_out = matmul(merged, dh1, "tn", "out_wgrad", tm=1024, tk=1536)
    dbm, dbg, dproj = merge_bwd(dh1, w_out, bm, bg, proj)
    d_w_bm = matmul(ym, dbm, "tn", "branch_m_wgrad", tm=1024, tk=1536)
    d_w_bg = matmul(yg, dbg, "tn", "branch_g_wgrad", tm=1024, tk=1536)
    token = send_early(dict(w_branch_m=d_w_bm, w_branch_g=d_w_bg, w_out=d_w_out, w_gu=d_w_gu, w_ff_down=d_w_down))
    mhg_after = mhg if token is None else mhg + token[0:1, 0:1]
    dproj, dhm, d_mhg = branch_dgrad(dbm, w_bm, hm, proj, MO0, mhg_after, False, dproj, "branch_m_dgrad")
    dproj, dhg, d_ghg = branch_dgrad(dbg, w_bg, hgl, proj, GR0, ghg, True, dproj, "branch_g_dgrad")
    dproj, dc, dgl, dga, d_a2p, d_a2b = mix_bwd(dhm, hm, qk, cpre, proj, gl, cs, nm, dhg, a2p, g_a2_b, ss, dproj)
    dproj, d_conv = conv_bwd(dc, proj, conv_w, dproj)
    dproj, d_gb = small_bwd(dgl, dga, proj, gb_row, dproj)
    d_wp = matmul(xn, dproj, "tn", "proj_wgrad", tm=1024, tn=1664, tk=1536)
    token = send_wp(d_wp)
    dxn = matmul(dproj, wp, "nt", "proj_dgrad", tm=1536, tk=1664, order=token)
    grad_x, d_meta, d_norm1_g = rms_bwd_input(dxn, x, meta, norm1_g, dh1, "rms1_bwd")

    grads = dict(
        meta_tokens=d_meta, norm1_g=d_norm1_g, conv_w=d_conv[0:4], conv_b=d_conv[4:5], m_gate_b=d_gb[0, 0:8].reshape(1, 2, 4),
        g_a2=d_a2p[8:8 + RANK], g_a2_b=d_a2b, m_head_g=d_mhg.reshape(NH, DV), g_head_g=d_ghg.reshape(NH, DV),
        norm2_g=d_norm2_g, final_g=d_final_g)
    return loss, grad_x, grads


_SEGS = [(0, 1024, QK0), (1024, 2048, MV0), (2048, 2056, SM0), (2056, 3080, MO0), (3080, 5128, GQ0),
         (5128, 5144, SM0 + 8), (5144, 6168, GR0), (6168, 8216, GM0)]
SHARD_W = NPROJ // NDEV


def regroup_cols(w8):
    parts = []
    for lo, hi, _ in sorted(_SEGS, key=lambda s: s[2]):
        while lo < hi:
            j = lo // SHARD_W
            end = min(hi, (j + 1) * SHARD_W)
            parts.append(w8[j, :, lo - j * SHARD_W:end - j * SHARD_W])
            lo = end
    parts.append(jnp.zeros((w8.shape[1], NP - NPROJ), w8.dtype))
    return jnp.concatenate(parts, axis=1)


def ungroup_cols(g):
    blocks = []
    for j in range(NDEV):
        lo, hi = j * SHARD_W, (j + 1) * SHARD_W
        parts = []
        for s_lo, s_hi, s_at in _SEGS:
            a, b = max(lo, s_lo), min(hi, s_hi)
            if a < b:
                parts.append(g[:, s_at + a - s_lo:s_at + b - s_lo])
        blocks.append(jnp.concatenate(parts, axis=1))
    return jnp.stack(blocks)


def col_blocks(g):
    r, c8 = g.shape
    return jnp.transpose(g.reshape(r, NDEV, c8 // NDEV), (1, 0, 2))


def from_col_blocks(g8):
    n, r, c = g8.shape
    return jnp.transpose(g8, (1, 0, 2)).reshape(r, n * c)


_MESHID = pl.DeviceIdType.MESH
_RELS = [(0, 0, 1), (1, 0, 0), (0, 1, 0), (1, 1, 0), (1, 0, 1), (0, 1, 1), (1, 1, 1)]


def _flip(v, bit):
    return 1 - v if bit else v


def all_gather(arrs, name):
    n = len(arrs)

    def body(*refs):
        ins, outs = refs[:n], refs[n:2 * n]
        send_sems, recv_sems, local_sems = refs[2 * n:]
        x, y, c = lax.axis_index("x"), lax.axis_index("y"), lax.axis_index("c")
        me, sibling = (x, y, c), (x, y, 1 - c)
        chips = [(1 - x, y), (x, 1 - y), (1 - x, 1 - y)]

        def slot(p):
            return 4 * p[0] + 2 * p[1] + p[2]

        def copy(a, k, block, to, src=None):
            dst = outs[a].at[slot(block)]
            return pltpu.make_async_remote_copy(
                src_ref=dst if src is None else src, dst_ref=dst,
                send_sem=send_sems.at[a, k], recv_sem=recv_sems.at[a, k],
                device_id=to, device_id_type=_MESHID)

        mine = [pltpu.make_async_copy(ins[a], outs[a].at[slot(me)], local_sems.at[a]) for a in range(n)]
        for cp in mine:
            cp.start()
        first = []
        for a in range(n):
            first.append(copy(a, 0, me, sibling, src=ins[a]))
            first += [copy(a, 1 + j, me, (*chip, c), src=ins[a]) for j, chip in enumerate(chips)]
        for cp in first:
            cp.start()
        passed = []
        for j, chip in enumerate(chips):
            for a in range(n):
                copy(a, 1 + j, (*chip, c), me).wait_recv()
                fwd = copy(a, 4 + j, (*chip, c), sibling)
                fwd.start()
                passed.append(fwd)
        for a in range(n):
            copy(a, 0, sibling, me).wait_recv()
            for j, chip in enumerate(chips):
                copy(a, 4 + j, (*chip, 1 - c), me).wait_recv()
        for cp in first + passed:
            cp.wait_send()
        for cp in mine:
            cp.wait()

    anyspec = pl.BlockSpec(memory_space=pl.ANY)
    return pl.pallas_call(
        body, name=name,
        in_specs=[anyspec] * n, out_specs=[anyspec] * n,
        out_shape=[jax.ShapeDtypeStruct((NDEV,) + a.shape, a.dtype) for a in arrs],
        scratch_shapes=[pltpu.SemaphoreType.DMA((n, 7)), pltpu.SemaphoreType.DMA((n, 7)),
                        pltpu.SemaphoreType.DMA((n,))],
    )(*arrs)


def exchange(blocks, rep, name):
    n = len(blocks)

    def body(*refs):
        b_refs, r_ref = refs[:n], refs[n]
        ob_refs, or_ref = refs[n + 1:2 * n + 1], refs[2 * n + 1]
        send_sems, recv_sems, local_sems = refs[2 * n + 2:]
        x, y, c = lax.axis_index("x"), lax.axis_index("y"), lax.axis_index("c")
        me = 4 * x + 2 * y + c

        def pairs(src_slot, dst_slot):
            return [(b_refs[a].at[src_slot], ob_refs[a].at[dst_slot]) for a in range(n)] + [(r_ref, or_ref.at[dst_slot])]

        loc = [pltpu.make_async_copy(s, d, local_sems.at[a]) for a, (s, d) in enumerate(pairs(me, me))]
        for cp in loc:
            cp.start()
        sends = []
        for k, (fx, fy, fc) in enumerate(_RELS):
            peer = (_flip(x, fx), _flip(y, fy), _flip(c, fc))
            pid = 4 * peer[0] + 2 * peer[1] + peer[2]
            for a, (s, d) in enumerate(pairs(pid, me)):
                sends.append(pltpu.make_async_remote_copy(
                    src_ref=s, dst_ref=d, send_sem=send_sems.at[a, k], recv_sem=recv_sems.at[a, k],
                    device_id=peer, device_id_type=_MESHID))
        for cp in sends:
            cp.start()
        for k, (fx, fy, fc) in enumerate(_RELS):
            peer = (_flip(x, fx), _flip(y, fy), _flip(c, fc))
            pid = 4 * peer[0] + 2 * peer[1] + peer[2]
            for a, (s, d) in enumerate(pairs(pid, pid)):
                pltpu.make_async_remote_copy(
                    src_ref=s, dst_ref=d, send_sem=send_sems.at[a, k], recv_sem=recv_sems.at[a, k],
                    device_id=peer, device_id_type=_MESHID).wait_recv()
        for cp in sends:
            cp.wait_send()
        for cp in loc:
            cp.wait()

    anyspec = pl.BlockSpec(memory_space=pl.ANY)
    return pl.pallas_call(
        body, name=name,
        in_specs=[anyspec] * (n + 1), out_specs=[anyspec] * (n + 1),
        out_shape=[jax.ShapeDtypeStruct(b.shape, b.dtype) for b in blocks]
        + [jax.ShapeDtypeStruct((NDEV,) + rep.shape, rep.dtype)],
        scratch_shapes=[pltpu.SemaphoreType.DMA((n + 1, 7)), pltpu.SemaphoreType.DMA((n + 1, 7)),
                        pltpu.SemaphoreType.DMA((n + 1,))],
    )(*blocks, rep)


_HBM = pl.BlockSpec(memory_space=pltpu.HBM)
_SEM = pl.BlockSpec(memory_space=pltpu.SEMAPHORE)
_EFFECT = pltpu.SideEffectType.DATAFLOW_SIDE_EFFECTING


def _peer_ids():
    x, y, c = lax.axis_index("x"), lax.axis_index("y"), lax.axis_index("c")
    peers = []
    for fx, fy, fc in _RELS:
        p = (_flip(x, fx), _flip(y, fy), _flip(c, fc))
        peers.append((p, 4 * p[0] + 2 * p[1] + p[2]))
    return 4 * x + 2 * y + c, peers


def _split_copy(src, land, a, k, peer, src_slot, dst_slot, send_sems, recv_sems):
    return pltpu.make_async_remote_copy(
        src_ref=src if src_slot is None else src.at[src_slot], dst_ref=land.at[dst_slot],
        send_sem=send_sems.at[7 * a + k], recv_sem=recv_sems.at[7 * a + k], device_id=peer, device_id_type=_MESHID)


def _own_copy(src, land, a, n, me, per_peer, send_sems):
    return pltpu.make_async_copy(src.at[me] if per_peer else src, land.at[me], send_sems.at[7 * n + a])


def send_start(srcs, per_peer, order, name):
    n = len(srcs)
    lands = [lax.empty((NDEV,) + (s.shape[1:] if per_peer else s.shape), s.dtype) for s in srcs]

    def body(*refs):
        src_refs, land_refs = refs[1:1 + n], refs[1 + n:1 + 2 * n]
        send_sems, recv_sems = refs[1 + 2 * n], refs[2 + 2 * n]
        token = refs[3 + 4 * n]
        me, peers = _peer_ids()
        for a in range(n):
            _own_copy(src_refs[a], land_refs[a], a, n, me, per_peer, send_sems).start()
        for a in range(n):
            for k, (peer, pid) in enumerate(peers):
                _split_copy(src_refs[a], land_refs[a], a, k, peer, pid if per_peer else None, me,
                            send_sems, recv_sems).start()
        token[...] = jnp.zeros_like(token)

    outs = pl.pallas_call(
        body, name=name,
        in_specs=[pl.BlockSpec(memory_space=pl.ANY)] + [_HBM] * (2 * n),
        out_shape=(pltpu.SemaphoreType.DMA((8 * n,)), pltpu.SemaphoreType.DMA((7 * n,)),
                   *[pltpu.HBM(s.shape, s.dtype) for s in srcs], *[pltpu.HBM(l.shape, l.dtype) for l in lands],
                   jax.ShapeDtypeStruct((8, 128), F32)),
        out_specs=(_SEM, _SEM, *[_HBM] * (2 * n), pl.BlockSpec(memory_space=pltpu.VMEM)),
        input_output_aliases={1 + i: 2 + i for i in range(2 * n)},
        compiler_params=pltpu.CompilerParams(has_side_effects=_EFFECT),
    )(order, *[pltpu.with_memory_space_constraint(s, pltpu.HBM) for s in srcs],
      *[pltpu.with_memory_space_constraint(l, pltpu.HBM) for l in lands])
    return (n, per_peer, outs[0], outs[1], outs[2:2 + n], outs[2 + n:2 + 2 * n]), outs[2 + 2 * n]


def send_wait(handle, after, name):
    n, per_peer, send_sems, recv_sems, src_thru, land_thru = handle

    def body(*refs):
        src_refs, land_refs = refs[:n], refs[n:2 * n]
        s_sems, r_sems = refs[2 * n], refs[2 * n + 1]
        me, peers = _peer_ids()
        for a in range(n):
            _own_copy(src_refs[a], land_refs[a], a, n, me, per_peer, s_sems).wait()
            for k, (peer, pid) in enumerate(peers):
                cp = _split_copy(src_refs[a], land_refs[a], a, k, peer, pid if per_peer else None, pid, s_sems, r_sems)
                cp.wait_send()
                cp.wait_recv()

    outs = pl.pallas_call(
        body, name=name,
        in_specs=[_HBM] * (2 * n) + [_SEM, _SEM, pl.BlockSpec(memory_space=pl.ANY)],
        out_shape=tuple(pltpu.HBM(t.shape, t.dtype) for t in (*src_thru, *land_thru)),
        out_specs=tuple([_HBM] * (2 * n)),
        input_output_aliases={i: i for i in range(2 * n)},
        compiler_params=pltpu.CompilerParams(has_side_effects=_EFFECT),
    )(*src_thru, *land_thru, send_sems, recv_sems, after)
    return list(outs[n:2 * n])


def adamw(parts, w, m, v, name, tr):
    npart, r, c = parts.shape
    c1 = 1.0 - ADAM_B1 ** ADAM_STEP
    c2 = 1.0 - ADAM_B2 ** ADAM_STEP

    def body(p_ref, w_ref, m_ref, v_ref, g_ref, d_ref, nm_ref, nv_ref):
        g = p_ref[0].astype(F32)
        for j in range(1, npart):
            g = g + p_ref[j].astype(F32)
        mn = ADAM_B1 * m_ref[...] + (1.0 - ADAM_B1) * g
        vn = ADAM_B2 * v_ref[...] + (1.0 - ADAM_B2) * (g * g)
        g_ref[...] = g
        nm_ref[...] = mn
        nv_ref[...] = vn
        d_ref[...] = -ADAM_LR * ((mn / c1) / (jnp.sqrt(vn / c2) + ADAM_EPS) + ADAM_WD * w_ref[...])

    spec = _rb(tr, c, 0)
    return pl.pallas_call(
        body, name=name, grid=(r // tr,),
        in_specs=[pl.BlockSpec((npart, tr, c), lambda i: (0, i, 0)), spec, spec, spec],
        out_specs=[spec] * 4, out_shape=[jax.ShapeDtypeStruct((r, c), F32)] * 4,
        compiler_params=_cp(("parallel",)),
    )(parts, w, m, v)


def sum_parts(parts, name, tc):
    npart, r, c = parts.shape

    def body(p_ref, o_ref):
        g = p_ref[0].astype(F32)
        for j in range(1, npart):
            g = g + p_ref[j].astype(F32)
        o_ref[...] = g

    return pl.pallas_call(
        body, name=name, grid=(c // tc,),
        in_specs=[pl.BlockSpec((npart, r, tc), lambda i: (0, 0, i))],
        out_specs=pl.BlockSpec((r, tc), lambda i: (0, i)),
        out_shape=jax.ShapeDtypeStruct((r, c), F32),
        compiler_params=_cp(("parallel",)),
    )(parts)


TINY = [("meta_tokens", (16, 1024)), ("conv_w", (4, 1024)), ("g_a2", (16, 512)), ("m_head_g", (4, 256)),
        ("g_head_g", (4, 256))]
REPL = [("norm1_g", (1, 1024)), ("conv_b", (1, 1024)), ("m_gate_b", (1, 2, 4)), ("g_a2_b", (1, 512)),
        ("norm2_g", (1, 1024)), ("final_g", (1024,))]
TINY_SIZE = 16 * 1024 + 4 * 1024 + 16 * 512 + 2 * 4 * 256
REPL_SIZE = 1024 + 1024 + 8 + 512 + 1024 + 1024
ROWS_GATHER = 8
ROWS_REP = 40
ROWS_OWN = 16


def pack_rows(vecs, rows):
    flat = jnp.concatenate([v.reshape(-1) for v in vecs])
    return jnp.pad(flat, (0, rows * 1024 - flat.shape[0])).reshape(rows, 1024)


def unpack_rows(packed, shapes):
    flat = packed.reshape(-1)
    out, off = [], 0
    for s in shapes:
        n = 1
        for d in s:
            n *= d
        out.append(flat[off:off + n].reshape(s))
        off += n
    return out


def kernel(x, meta_tokens, norm1_g, w_in, conv_w, conv_b, m_gate_b, g_a2, g_a2_b, m_head_g, g_head_g, w_branch_m, w_branch_g, w_out, norm2_g, w_ff_gate, w_ff_up, w_ff_down, final_g, loss_target, m_meta_tokens, m_norm1_g, m_w_in, m_conv_w, m_conv_b, m_m_gate_b, m_g_a2, m_g_a2_b, m_m_head_g, m_g_head_g, m_w_branch_m, m_w_branch_g, m_w_out, m_norm2_g, m_w_ff_gate, m_w_ff_up, m_w_ff_down, m_final_g, v_meta_tokens, v_norm1_g, v_w_in, v_conv_w, v_conv_b, v_m_gate_b, v_g_a2, v_g_a2_b, v_m_head_g, v_g_head_g, v_w_branch_m, v_w_branch_g, v_w_out, v_norm2_g, v_w_ff_gate, v_w_ff_up, v_w_ff_down, v_final_g):
    w_sh = dict(meta_tokens=meta_tokens, w_in=w_in[0], conv_w=conv_w[0], g_a2=g_a2[0], m_head_g=m_head_g[0],
                g_head_g=g_head_g[0], w_branch_m=w_branch_m[0], w_branch_g=w_branch_g[0], w_out=w_out[0],
                w_ff_gate=w_ff_gate[0], w_ff_up=w_ff_up[0], w_ff_down=w_ff_down[0])
    m_sh = dict(meta_tokens=m_meta_tokens, w_in=m_w_in[0], conv_w=m_conv_w[0], g_a2=m_g_a2[0],
                m_head_g=m_m_head_g[0], g_head_g=m_g_head_g[0], w_branch_m=m_w_branch_m[0],
                w_branch_g=m_w_branch_g[0], w_out=m_w_out[0], w_ff_gate=m_w_ff_gate[0], w_ff_up=m_w_ff_up[0],
                w_ff_down=m_w_ff_down[0])
    v_sh = dict(meta_tokens=v_meta_tokens, w_in=v_w_in[0], conv_w=v_conv_w[0], g_a2=v_g_a2[0],
                m_head_g=v_m_head_g[0], g_head_g=v_g_head_g[0], w_branch_m=v_w_branch_m[0],
                w_branch_g=v_w_branch_g[0], w_out=v_w_out[0], w_ff_gate=v_w_ff_gate[0], w_ff_up=v_w_ff_up[0],
                w_ff_down=v_w_ff_down[0])
    w_rep = dict(norm1_g=norm1_g, conv_b=conv_b, m_gate_b=m_gate_b, g_a2_b=g_a2_b, norm2_g=norm2_g, final_g=final_g)
    m_rep = dict(norm1_g=m_norm1_g, conv_b=m_conv_b, m_gate_b=m_m_gate_b, g_a2_b=m_g_a2_b, norm2_g=m_norm2_g,
                 final_g=m_final_g)
    v_rep = dict(norm1_g=v_norm1_g, conv_b=v_conv_b, m_gate_b=v_m_gate_b, g_a2_b=v_g_a2_b, norm2_g=v_norm2_g,
                 final_g=v_final_g)
    dev = 4 * lax.axis_index("x") + 2 * lax.axis_index("y") + lax.axis_index("c")
    tiny_names = [n for n, _ in TINY]
    repl_names = [n for n, _ in REPL]
    tiny_shard_shapes = [(s[0], s[1] // NDEV) for _, s in TINY]

    in8, tiny8 = all_gather([w_sh["w_in"].astype(MXU), pack_rows([w_sh[n] for n in tiny_names], ROWS_GATHER)],
                            "param_all_gather")
    late_names = ["w_branch_m", "w_branch_g", "w_out", "w_ff_gate", "w_ff_up", "w_ff_down"]
    late, first_order = send_start([w_sh[n].astype(MXU) for n in late_names], False, tiny8, "late_weights_start")
    wp = regroup_cols(in8)
    handles = {}

    def late_weights(after):
        bm8, bg8, out8, ffg8, ffu8, ffd8 = send_wait(late, after, "late_weights_wait")
        w_gu = interleave_gu(from_col_blocks(ffg8), from_col_blocks(ffu8))
        return bm8.reshape(D, D), bg8.reshape(D, D), out8.reshape(D, D), w_gu, ffd8.reshape(DFF, D)

    def send_early(g):
        d_gate, d_up = split_gu(g["w_gu"])
        blocks = [g["w_branch_m"].reshape(NDEV, D // NDEV, D).astype(WIRE),
                  g["w_branch_g"].reshape(NDEV, D // NDEV, D).astype(WIRE),
                  g["w_out"].reshape(NDEV, D // NDEV, D).astype(WIRE),
                  col_blocks(d_gate).astype(WIRE), col_blocks(d_up).astype(WIRE),
                  g["w_ff_down"].reshape(NDEV, DFF // NDEV, D).astype(WIRE)]
        handles["early"], token = send_start(blocks, True, blocks[0], "early_grads_start")
        return token

    def send_wp(d_wp):
        blocks = [ungroup_cols(d_wp).astype(WIRE)]
        handles["wp"], token = send_start(blocks, True, blocks[0], "proj_grads_start")
        return token

    tiny_full = {}
    for j in range(NDEV):
        for name, blk in zip(tiny_names, unpack_rows(tiny8[j], tiny_shard_shapes)):
            tiny_full.setdefault(name, []).append(blk)
    tiny_full = {n: jnp.concatenate(v, axis=1) for n, v in tiny_full.items()}

    loss, grad_x, g = local_step(
        x[0], loss_target[0], tiny_full["meta_tokens"], norm1_g, wp, tiny_full["conv_w"], conv_b, m_gate_b[0],
        tiny_full["g_a2"], g_a2_b, tiny_full["m_head_g"], tiny_full["g_head_g"], norm2_g, final_g,
        late_weights, send_early, send_wp, first_order)

    rep = pack_rows([g[n] for n in tiny_names + repl_names] + [loss[0, 0:1]], ROWS_REP)
    (got_rep,) = exchange([], rep, "small_grad_exchange")
    got_early = send_wait(handles["early"], got_rep, "early_grads_wait")
    (got_wp,) = send_wait(handles["wp"], got_rep, "proj_grads_wait")

    result = {}

    def update(name, parts, tr):
        outs = adamw(parts, w_sh[name], m_sh[name], v_sh[name], "adamw_" + name, tr)
        for kind, arr in zip(("grad", "delta", "new_m", "new_v"), outs):
            result[kind, name] = arr[None]

    update("w_in", got_wp, 128)
    update("w_branch_m", got_early[0], 128)
    update("w_branch_g", got_early[1], 128)
    update("w_out", got_early[2], 128)
    update("w_ff_gate", got_early[3], 256)
    update("w_ff_up", got_early[4], 256)
    update("w_ff_down", got_early[5], DFF // NDEV)

    rep_sum = sum_parts(got_rep, "sum_small", 1024)
    rep_g = unpack_rows(rep_sum, [s for _, s in TINY] + [s for _, s in REPL] + [(1,)])
    own_g = [lax.dynamic_slice_in_dim(gf, dev * ss[1], ss[1], axis=1) for gf, ss in zip(rep_g, tiny_shard_shapes)]
    own_g += rep_g[len(TINY):len(TINY) + len(REPL)]
    w_all = {**w_sh, **w_rep}
    m_all = {**m_sh, **m_rep}
    v_all = {**v_sh, **v_rep}
    names = tiny_names + repl_names
    outs = adamw(pack_rows(own_g, ROWS_OWN)[None], pack_rows([w_all[n] for n in names], ROWS_OWN),
                 pack_rows([m_all[n] for n in names], ROWS_OWN), pack_rows([v_all[n] for n in names], ROWS_OWN),
                 "adamw_small", ROWS_OWN)
    shapes = tiny_shard_shapes + [s for _, s in REPL]
    for kind, packed in zip(("grad", "delta", "new_m", "new_v"), outs):
        for name, arr in zip(names, unpack_rows(packed, shapes)):
            result[kind, name] = arr[None] if name in tiny_names and name != "meta_tokens" else arr
    loss_total = rep_g[-1][0]
    order = ["meta_tokens", "norm1_g", "w_in", "conv_w", "conv_b", "m_gate_b", "g_a2", "g_a2_b", "m_head_g", "g_head_g",
             "w_branch_m", "w_branch_g", "w_out", "norm2_g", "w_ff_gate", "w_ff_up", "w_ff_down", "final_g"]
    return (loss_total, grad_x[None], *[result[kind, n] for kind in ("grad", "delta", "new_m", "new_v") for n in order])
```

```python
import functools

import jax
import jax.numpy as jnp
from jax import lax
from jax.experimental import pallas as pl
from jax.experimental.pallas import tpu as pltpu

F32 = jnp.float32
MXU = jnp.bfloat16
WIRE = jnp.bfloat16

D = 1024
NH = 4
DV = 256
DQK = 128
L = 64
NMETA = 16
PADR = 512
LM = 256
CH0 = PADR // LM - 1
NPADROWS = PADR - NMETA
RANK = 16
DFF = 2816
EPS = 1e-6
TAU = 16.0
QSCALE = DQK ** -0.5
NEG = -1e30
NDEV = 8

MV0, MO0, GQ0, GK0, GV0, GR0, QK0, GM0, GG0, SM0 = 0, 1024, 2048, 2560, 3072, 4096, 5120, 6144, 7168, 8192
NP = 8320
NPROJ = 8216

ADAM_LR, ADAM_B1, ADAM_B2, ADAM_EPS, ADAM_WD, ADAM_STEP = 0.001, 0.9, 0.999, 1e-08, 0.01, 10

VMEM_LIMIT = 56 * 1024 * 1024
TM = 512


def _cp(sem):
    return pltpu.CompilerParams(dimension_semantics=sem, vmem_limit_bytes=VMEM_LIMIT)


def _sigmoid(x):
    return 1.0 / (1.0 + jnp.exp(-x))


def _log_sigmoid(x):
    return jnp.minimum(x, 0.0) - jnp.log1p(jnp.exp(-jnp.abs(x)))


def _dot(a, b, ca, cb):
    return lax.dot_general(a.astype(MXU), b.astype(MXU), (((ca,), (cb,)), ((), ())), preferred_element_type=F32)


def _dot_exact(a, b):
    return lax.dot_general(a, b, (((1,), (0,)), ((), ())), precision=lax.Precision.HIGHEST,
                           preferred_element_type=F32)


def _rb(tm, w, cb):
    return pl.BlockSpec((tm, w), lambda i: (i, cb))


def _const(shape):
    nd = len(shape)
    return pl.BlockSpec(shape, lambda i: (0,) * nd)


def _pick(n, target):
    if n <= target:
        return n
    best = None
    for t in range(128, target + 1, 128):
        if n % t == 0:
            best = t
    assert best is not None, (n, target)
    return best


def matmul(a, b, mode, name, add=None, out_dtype=F32, tm=512, tn=1664, tk=1024, order=None):
    if mode == "nn":
        (M, K), (K2, N) = a.shape, b.shape
    elif mode == "nt":
        (M, K), (N, K2) = a.shape, b.shape
    else:
        (K, M), (K2, N) = a.shape, b.shape
    assert K == K2, (a.shape, b.shape, mode)
    tm, tn, tk = _pick(M, tm), _pick(N, tn), _pick(K, tk)
    nk = K // tk
    assert nk == 1 or out_dtype == F32
    ca, cb = {"nn": (1, 0), "nt": (1, 1), "tn": (0, 0)}[mode]
    a_spec = {"nn": pl.BlockSpec((tm, tk), lambda j, i, k: (i, k)),
              "nt": pl.BlockSpec((tm, tk), lambda j, i, k: (i, k)),
              "tn": pl.BlockSpec((tk, tm), lambda j, i, k: (k, i))}[mode]
    b_spec = {"nn": pl.BlockSpec((tk, tn), lambda j, i, k: (k, j)),
              "nt": pl.BlockSpec((tn, tk), lambda j, i, k: (j, k)),
              "tn": pl.BlockSpec((tk, tn), lambda j, i, k: (k, j))}[mode]
    o_spec = pl.BlockSpec((tm, tn), lambda j, i, k: (i, j))
    has_add = add is not None

    def body(*refs):
        if order is not None:
            refs = refs[:-2] + refs[-1:]
        if has_add:
            a_ref, b_ref, add_ref, o_ref = refs
        else:
            a_ref, b_ref, o_ref = refs
            add_ref = None
        part = _dot(a_ref[...], b_ref[...], ca, cb)
        if nk == 1:
            if has_add:
                part = part + add_ref[...]
            o_ref[...] = part.astype(o_ref.dtype)
            return
        k = pl.program_id(2)

        @pl.when(k == 0)
        def _():
            o_ref[...] = part + add_ref[...] if has_add else part

        @pl.when(k > 0)
        def _():
            o_ref[...] += part

    in_specs = [a_spec, b_spec] + ([o_spec] if has_add else [])
    args = (a, b) + ((add,) if has_add else ())
    if order is not None:
        in_specs.append(pl.BlockSpec(order.shape, lambda j, i, k: (0, 0)))
        args += (order,)
    return pl.pallas_call(
        body, name=name, grid=(N // tn, M // tm, nk),
        in_specs=in_specs, out_specs=o_spec,
        out_shape=jax.ShapeDtypeStruct((M, N), out_dtype),
        compiler_params=_cp(("parallel", "parallel", "arbitrary")),
    )(*args)


def _h0_tile(i, x_ref, meta_ref):
    assert PADR == TM
    front = jnp.concatenate([jnp.zeros((NPADROWS, D), F32), meta_ref[...]], axis=0)
    return jnp.where(i == 0, front, x_ref[...])


_TOKENS = pl.BlockSpec((TM, D), lambda i: (jnp.maximum(i - 1, 0), 0))


def rms_fwd_input(x, meta, g, name):
    R = x.shape[0] + PADR

    def body(x_ref, meta_ref, g_ref, y_ref):
        xv = _h0_tile(pl.program_id(0), x_ref, meta_ref)
        r = lax.rsqrt(jnp.mean(xv * xv, axis=-1, keepdims=True) + EPS)
        y_ref[...] = (xv * r * g_ref[...]).astype(y_ref.dtype)

    return pl.pallas_call(
        body, name=name, grid=(R // TM,),
        in_specs=[_TOKENS, _const((NMETA, D)), _const((1, D))], out_specs=_rb(TM, D, 0),
        out_shape=jax.ShapeDtypeStruct((R, D), MXU), compiler_params=_cp(("parallel",)),
    )(x, meta, g)


def dgrad_rms_bwd(a, w, x, g, dres, name, tm, tk):
    R, K = a.shape
    tm, tk = _pick(R, tm), _pick(K, tk)
    nk = K // tk

    def body(a_ref, w_ref, x_ref, g_ref, dres_ref, dx_ref, dg_ref):
        i, k = pl.program_id(0), pl.program_id(1)
        part = _dot(a_ref[...], w_ref[...], 1, 1)

        @pl.when(k == 0)
        def _():
            dx_ref[...] = part

        @pl.when(k > 0)
        def _():
            dx_ref[...] += part

        @pl.when(k == nk - 1)
        def _():
            xv, dyv = x_ref[...], dx_ref[...]
            r = lax.rsqrt(jnp.mean(xv * xv, axis=-1, keepdims=True) + EPS)
            dyg = dyv * g_ref[...]
            dx_ref[...] = dres_ref[...] + r * dyg - xv * (r * r * r * jnp.mean(dyg * xv, axis=-1, keepdims=True))
            gpart = jnp.sum(dyv * xv * r, axis=0, keepdims=True)

            @pl.when(i == 0)
            def _():
                dg_ref[...] = gpart

            @pl.when(i > 0)
            def _():
                dg_ref[...] += gpart

    row = pl.BlockSpec((tm, D), lambda i, k: (i, 0))
    return pl.pallas_call(
        body, name=name, grid=(R // tm, nk),
        in_specs=[pl.BlockSpec((tm, tk), lambda i, k: (i, k)), pl.BlockSpec((D, tk), lambda i, k: (0, k)), row,
                  pl.BlockSpec((1, D), lambda i, k: (0, 0)), row],
        out_specs=[row, pl.BlockSpec((1, D), lambda i, k: (0, 0))],
        out_shape=[jax.ShapeDtypeStruct((R, D), F32), jax.ShapeDtypeStruct((1, D), F32)],
        compiler_params=_cp(("arbitrary", "arbitrary")),
    )(a, w, x, g, dres)


def rms_bwd_input(dy, x, meta, g, dres, name):
    R = dy.shape[0]

    def body(dy_ref, x_ref, meta_ref, g_ref, dres_ref, dx_ref, dmeta_ref, dg_ref):
        i = pl.program_id(0)
        xv, dyv = _h0_tile(i, x_ref, meta_ref), dy_ref[...]
        r = lax.rsqrt(jnp.mean(xv * xv, axis=-1, keepdims=True) + EPS)
        dyg = dyv * g_ref[...]
        dx = dres_ref[...] + r * dyg - xv * (r * r * r * jnp.mean(dyg * xv, axis=-1, keepdims=True))
        dx_ref[...] = dx
        part = jnp.sum(dyv * xv * r, axis=0, keepdims=True)

        @pl.when(i == 0)
        def _():
            dg_ref[...] = part
            dmeta_ref[...] = dx[NPADROWS:PADR]

        @pl.when(i > 0)
        def _():
            dg_ref[...] += part

    return pl.pallas_call(
        body, name=name, grid=(R // TM,),
        in_specs=[_rb(TM, D, 0), _TOKENS, _const((NMETA, D)), _const((1, D)), _rb(TM, D, 0)],
        out_specs=[_TOKENS, _const((NMETA, D)), _const((1, D))],
        out_shape=[jax.ShapeDtypeStruct((R - PADR, D), F32), jax.ShapeDtypeStruct((NMETA, D), F32),
                   jax.ShapeDtypeStruct((1, D), F32)],
        compiler_params=_cp(("arbitrary",)),
    )(dy, x, meta, g, dres)


def _shift_down(cur, prev8, s):
    tm = cur.shape[0]
    rolled = pltpu.roll(cur, s, 0)
    rows8 = lax.broadcasted_iota(jnp.int32, (8, cur.shape[1]), 0)
    head = jnp.where(rows8 < s, pltpu.roll(prev8, s, 0), rolled[0:8])
    return jnp.concatenate([head, rolled[8:tm]], axis=0)


def _shift_up(cur, next8, s):
    tm = cur.shape[0]
    rolled = pltpu.roll(cur, tm - s, 0)
    rows8 = lax.broadcasted_iota(jnp.int32, (8, cur.shape[1]), 0)
    tail = jnp.where(rows8 >= 8 - s, pltpu.roll(next8, 8 - s, 0), rolled[tm - 8:tm])
    return jnp.concatenate([rolled[0:tm - 8], tail], axis=0)


def prep_fwd(proj, conv_w, conv_b, gb_row):
    R = proj.shape[0]
    t8 = TM // 8

    def body(x_ref, halo_ref, sm_ref, w_ref, b_ref, gb_ref, c_ref, qk_ref, gl_ref):
        i = pl.program_id(0)
        x = x_ref[...]
        halo = halo_ref[...]
        w = w_ref[...]
        c = x * w[3:4, :] + b_ref[...]
        for s in (1, 2, 3):
            c = c + _shift_down(x, halo, s) * w[3 - s:4 - s, :]
        c_ref[...] = c
        qk_ref[...] = c * _sigmoid(c)
        z = sm_ref[...] + gb_ref[...]
        lane = lax.broadcasted_iota(jnp.int32, z.shape, 1)
        row = lax.broadcasted_iota(jnp.int32, z.shape, 0) + i * TM
        valid = row >= NPADROWS
        logi = jnp.where(valid, z, NEG)
        logf = jnp.where(valid, _log_sigmoid(z), 0.0)
        gl_ref[...] = jnp.where(lane < 4, logi, jnp.where(lane < 8, logf, 0.0))

    return pl.pallas_call(
        body, name="prep_fwd", grid=(R // TM,),
        in_specs=[_rb(TM, 1024, QK0 // 1024),
                  pl.BlockSpec((8, 1024), lambda i: (jnp.maximum(i * t8 - 1, 0), QK0 // 1024)),
                  _rb(TM, 128, SM0 // 128), _const((4, 1024)), _const((1, 1024)), _const((1, 128))],
        out_specs=[_rb(TM, 1024, 0), _rb(TM, 1024, 0), _rb(TM, 128, 0)],
        out_shape=[jax.ShapeDtypeStruct((R, 1024), F32), jax.ShapeDtypeStruct((R, 1024), F32),
                   jax.ShapeDtypeStruct((R, 128), F32)],
        compiler_params=_cp(("parallel",)),
    )(proj, proj, proj, conv_w, conv_b, gb_row)


def merge_fwd(ym, yg, w_bm, w_bg, proj):
    R = ym.shape[0]

    def body(ym_ref, yg_ref, wm_ref, wg_ref, gm_ref, gg_ref, bm_ref, bg_ref, o_ref):
        bm = _dot(ym_ref[...], wm_ref[...], 1, 0)
        bg = _dot(yg_ref[...], wg_ref[...], 1, 0)
        bm_ref[...] = bm
        bg_ref[...] = bg
        o_ref[...] = (_sigmoid(gm_ref[...]) * bm + _sigmoid(gg_ref[...]) * bg).astype(o_ref.dtype)

    return pl.pallas_call(
        body, name="merge_fwd", grid=(R // TM,),
        in_specs=[_rb(TM, D, 0), _rb(TM, D, 0), _const((D, D)), _const((D, D)), _rb(TM, D, GM0 // D),
                  _rb(TM, D, GG0 // D)],
        out_specs=[_rb(TM, D, 0), _rb(TM, D, 0), _rb(TM, D, 0)],
        out_shape=[jax.ShapeDtypeStruct((R, D), F32), jax.ShapeDtypeStruct((R, D), F32),
                   jax.ShapeDtypeStruct((R, D), MXU)],
        compiler_params=_cp(("parallel",)),
    )(ym, yg, w_bm, w_bg, proj, proj)


def merge_bwd(dh1, w_out, bm, bg, proj):
    R = bm.shape[0]

    def body(dh_ref, w_ref, bm_ref, bg_ref, gm_ref, gg_ref, dbm_ref, dbg_ref, dp_ref):
        dm = _dot(dh_ref[...], w_ref[...], 1, 1)
        sm, sg = _sigmoid(gm_ref[...]), _sigmoid(gg_ref[...])
        dbm_ref[...] = (dm * sm).astype(dbm_ref.dtype)
        dbg_ref[...] = (dm * sg).astype(dbg_ref.dtype)
        dp_ref[:, 0:D] = (dm * bm_ref[...] * sm * (1.0 - sm)).astype(dp_ref.dtype)
        dp_ref[:, D:2 * D] = (dm * bg_ref[...] * sg * (1.0 - sg)).astype(dp_ref.dtype)

    return pl.pallas_call(
        body, name="merge_bwd", grid=(R // TM,),
        in_specs=[_rb(TM, D, 0), _const((D, D)), _rb(TM, D, 0), _rb(TM, D, 0), _rb(TM, D, GM0 // D),
                  _rb(TM, D, GG0 // D)],
        out_specs=[_rb(TM, D, 0), _rb(TM, D, 0), _rb(TM, 2 * D, GM0 // (2 * D))],
        out_shape=[jax.ShapeDtypeStruct((R, D), MXU), jax.ShapeDtypeStruct((R, D), MXU),
                   jax.ShapeDtypeStruct((R, NP), MXU)],
        compiler_params=_cp(("parallel",)),
    )(dh1, w_out, bm, bg, proj, proj)


TF = DFF // 2
TMF = 768


def interleave_gu(gate, up):
    return jnp.concatenate([gate[:, :TF], up[:, :TF], gate[:, TF:], up[:, TF:]], axis=1)


def split_gu(gu):
    return (jnp.concatenate([gu[:, 0:TF], gu[:, 2 * TF:3 * TF]], axis=1),
            jnp.concatenate([gu[:, TF:2 * TF], gu[:, 3 * TF:]], axis=1))


def ff_in_fwd(hn, w_gu):
    R = hn.shape[0]

    def body(x_ref, w_ref, au_ref, ff_ref):
        au = _dot(x_ref[...], w_ref[...], 1, 0)
        au_ref[...] = au.astype(au_ref.dtype)
        a = au[:, :TF]
        ff_ref[...] = (a * _sigmoid(a) * au[:, TF:]).astype(ff_ref.dtype)

    tm = _pick(R, TMF)
    return pl.pallas_call(
        body, name="ff_in_fwd", grid=(DFF // TF, R // tm),
        in_specs=[pl.BlockSpec((tm, D), lambda j, i: (i, 0)), pl.BlockSpec((D, 2 * TF), lambda j, i: (0, j))],
        out_specs=[pl.BlockSpec((tm, 2 * TF), lambda j, i: (i, j)), pl.BlockSpec((tm, TF), lambda j, i: (i, j))],
        out_shape=[jax.ShapeDtypeStruct((R, 2 * DFF), MXU), jax.ShapeDtypeStruct((R, DFF), MXU)],
        compiler_params=_cp(("parallel", "parallel")),
    )(hn, w_gu)


def ff_down_dgrad(dh2, w_down, au):
    R = dh2.shape[0]

    def body(d_ref, w_ref, au_ref, o_ref):
        dff = _dot(d_ref[...], w_ref[...], 1, 1)
        a = au_ref[:, :TF].astype(F32)
        u = au_ref[:, TF:].astype(F32)
        s = _sigmoid(a)
        o_ref[:, :TF] = (dff * u * s * (1.0 + a * (1.0 - s))).astype(o_ref.dtype)
        o_ref[:, TF:] = (dff * a * s).astype(o_ref.dtype)

    tm = _pick(R, TMF)
    return pl.pallas_call(
        body, name="ff_down_dgrad", grid=(DFF // TF, R // tm),
        in_specs=[pl.BlockSpec((tm, D), lambda j, i: (i, 0)), pl.BlockSpec((TF, D), lambda j, i: (j, 0)),
                  pl.BlockSpec((tm, 2 * TF), lambda j, i: (i, j))],
        out_specs=pl.BlockSpec((tm, 2 * TF), lambda j, i: (i, j)),
        out_shape=jax.ShapeDtypeStruct((R, 2 * DFF), MXU),
        compiler_params=_cp(("parallel", "parallel")),
    )(dh2, w_down, au)


def out_proj_norm(merged, w, x, meta, g):
    R = merged.shape[0]

    def body(m_ref, w_ref, x_ref, meta_ref, g_ref, h_ref, n_ref):
        hv = _dot(m_ref[...], w_ref[...], 1, 0) + _h0_tile(pl.program_id(0), x_ref, meta_ref)
        h_ref[...] = hv
        r = lax.rsqrt(jnp.mean(hv * hv, axis=-1, keepdims=True) + EPS)
        n_ref[...] = (hv * r * g_ref[...]).astype(n_ref.dtype)

    return pl.pallas_call(
        body, name="out_fwd", grid=(R // TM,),
        in_specs=[_rb(TM, D, 0), _const((D, D)), _TOKENS, _const((NMETA, D)), _const((1, D))],
        out_specs=[_rb(TM, D, 0), _rb(TM, D, 0)],
        out_shape=[jax.ShapeDtypeStruct((R, D), F32), jax.ShapeDtypeStruct((R, D), MXU)],
        compiler_params=_cp(("parallel",)),
    )(merged, w, x, meta, g)


def ff_down_loss(ff, w_down, h1, gf, target):
    R = ff.shape[0]
    assert PADR == TM

    def body(f_ref, w_ref, h1_ref, g_ref, t_ref, dh_ref, loss_ref, dg_ref):
        i = pl.program_id(0)
        hv = _dot(f_ref[...], w_ref[...], 1, 0) + h1_ref[...]
        r = lax.rsqrt(jnp.mean(hv * hv, axis=-1, keepdims=True) + EPS)
        g = g_ref[...]
        live = (i >= 1).astype(F32)
        e = (hv * r * g - t_ref[...]) * live
        dy = e * (1.0 / D)
        dyg = dy * g
        dh_ref[...] = r * dyg - hv * (r * r * r * jnp.mean(dyg * hv, axis=-1, keepdims=True))
        lpart = jnp.zeros((1, 128), F32) + 0.5 * jnp.sum(jnp.sum(e * e, axis=1, keepdims=True), axis=0, keepdims=True) * (1.0 / D)
        gpart = jnp.sum(dy * hv * r, axis=0, keepdims=True)

        @pl.when(i == 0)
        def _():
            loss_ref[...] = lpart
            dg_ref[...] = gpart

        @pl.when(i > 0)
        def _():
            loss_ref[...] += lpart
            dg_ref[...] += gpart

    return pl.pallas_call(
        body, name="ff_down_loss", grid=(R // TM,),
        in_specs=[_rb(TM, DFF, 0), _const((DFF, D)), _rb(TM, D, 0), _const((1, D)),
                  pl.BlockSpec((TM, D), lambda i: (jnp.maximum(i - 1, 0), 0))],
        out_specs=[_rb(TM, D, 0), _const((1, 128)), _const((1, D))],
        out_shape=[jax.ShapeDtypeStruct((R, D), F32), jax.ShapeDtypeStruct((1, 128), F32),
                   jax.ShapeDtypeStruct((1, D), F32)],
        compiler_params=_cp(("arbitrary",)),
    )(ff, w_down, h1, gf, target)


def conv_bwd(dc, proj, conv_w, dproj):
    R = dc.shape[0]
    t8 = TM // 8
    nt = R // TM

    def body(dc_ref, nxt_ref, x_ref, prv_ref, w_ref, dp_in, dp_ref, dw_ref):
        del dp_in
        i = pl.program_id(0)
        dcv = dc_ref[...]
        nxt = nxt_ref[...] * (i < nt - 1).astype(F32)
        x = x_ref[...]
        prv = prv_ref[...]
        w = w_ref[...]
        dx = dcv * w[3:4, :]
        rows = [None] * 4
        rows[3] = jnp.sum(dcv * x, axis=0, keepdims=True)
        for s in (1, 2, 3):
            dx = dx + _shift_up(dcv, nxt, s) * w[3 - s:4 - s, :]
            rows[3 - s] = jnp.sum(dcv * _shift_down(x, prv, s), axis=0, keepdims=True)
        dp_ref[...] = dx.astype(dp_ref.dtype)
        part = jnp.concatenate(rows + [jnp.sum(dcv, axis=0, keepdims=True), jnp.zeros((3, 1024), F32)], axis=0)

        @pl.when(i == 0)
        def _():
            dw_ref[...] = part

        @pl.when(i > 0)
        def _():
            dw_ref[...] += part

    return pl.pallas_call(
        body, name="conv_bwd", grid=(nt,),
        in_specs=[_rb(TM, 1024, 0),
                  pl.BlockSpec((8, 1024), lambda i: (jnp.minimum((i + 1) * t8, nt * t8 - 1), 0)),
                  _rb(TM, 1024, QK0 // 1024),
                  pl.BlockSpec((8, 1024), lambda i: (jnp.maximum(i * t8 - 1, 0), QK0 // 1024)),
                  _const((4, 1024)), pl.BlockSpec(memory_space=pl.ANY)],
        out_specs=[_rb(TM, 1024, QK0 // 1024), _const((8, 1024))],
        out_shape=[jax.ShapeDtypeStruct((R, NP), MXU), jax.ShapeDtypeStruct((8, 1024), F32)],
        input_output_aliases={5: 0},
        compiler_params=_cp(("arbitrary",)),
    )(dc, dc, proj, proj, conv_w, dproj)


def small_bwd(dgl, dga, proj, gb_row, dproj):
    R = dgl.shape[0]

    def body(dgl_ref, dga_ref, sm_ref, gb_ref, dp_in, dp_ref, dgb_ref):
        del dp_in
        i = pl.program_id(0)
        z = sm_ref[...] + gb_ref[...]
        lane = lax.broadcasted_iota(jnp.int32, z.shape, 1)
        row = lax.broadcasted_iota(jnp.int32, z.shape, 0) + i * TM
        valid = row >= NPADROWS
        dgl_v = dgl_ref[...]
        dgate = jnp.where(valid, jnp.where(lane < 4, dgl_v, dgl_v * _sigmoid(-z)), 0.0)
        ds = jnp.where(lane < 8, dgate, dga_ref[...])
        dp_ref[...] = ds.astype(dp_ref.dtype)
        part = jnp.sum(jnp.where(lane < 8, dgate, 0.0), axis=0, keepdims=True)

        @pl.when(i == 0)
        def _():
            dgb_ref[...] = part

        @pl.when(i > 0)
        def _():
            dgb_ref[...] += part

    return pl.pallas_call(
        body, name="small_bwd", grid=(R // TM,),
        in_specs=[_rb(TM, 128, 0), _rb(TM, 128, 0), _rb(TM, 128, SM0 // 128), _const((1, 128)),
                  pl.BlockSpec(memory_space=pl.ANY)],
        out_specs=[_rb(TM, 128, SM0 // 128), _const((1, 128))],
        out_shape=[jax.ShapeDtypeStruct((R, NP), MXU), jax.ShapeDtypeStruct((1, 128), F32)],
        input_output_aliases={4: 0},
        compiler_params=_cp(("arbitrary",)),
    )(dgl, dga, proj, gb_row, dproj)


def _masks(n=L):
    r = lax.broadcasted_iota(jnp.int32, (n, n), 0)
    c = lax.broadcasted_iota(jnp.int32, (n, n), 1)
    return r >= c, r == c, r


def _to_row(col, eye):
    return jnp.sum(jnp.where(eye, col, 0.0), axis=0, keepdims=True)


def _to_col(row, eye):
    return jnp.sum(jnp.where(eye, row, 0.0), axis=1, keepdims=True)


def _mlstm_chunk(q, k, logi_c, logf_c, m, n):
    tril, eye, _ = _masks(q.shape[0])
    logi_r, logf_r = _to_row(logi_c, eye), _to_row(logf_c, eye)
    b_c = jnp.sum(jnp.where(tril, logf_r, 0.0), axis=1, keepdims=True)
    b_r = _to_row(b_c, eye)
    g = jnp.sum(logf_c, axis=0, keepdims=True)
    dmat = jnp.where(tril, b_c - b_r + logi_r, NEG)
    mrow = jnp.maximum(b_c + m, jnp.max(dmat, axis=1, keepdims=True))
    dm = jnp.exp(dmat - mrow)
    s = _dot(q, k, 1, 1)
    w = dm * s
    a_in = jnp.exp(b_c + m - mrow)
    qn = jnp.sum(q * n, axis=1, keepdims=True)
    den = a_in * qn + jnp.sum(w, axis=1, keepdims=True)
    floor = jnp.exp(-mrow)
    nrm = jnp.maximum(jnp.abs(den), floor)
    wlog_c = g - b_c + logi_c
    m_new = jnp.maximum(g + m, jnp.max(wlog_c, axis=0, keepdims=True))
    a_st = jnp.exp(g + m - m_new)
    w_c = jnp.exp(wlog_c - m_new)
    return dict(b_c=b_c, g=g, dm=dm, s=s, w=w, a_in=a_in, qn=qn, den=den, floor=floor, nrm=nrm,
                m_new=m_new, a_st=a_st, w_c=w_c, tril=tril, eye=eye)


def _mlstm_head_fwd(h, glv, qk_ref, v_ref, mo_ref, hg_ref, hm_ref, ym_ref, cs_ref, nm_ref, c_s, nm_s):
    q = qk_ref[:, h * DQK:(h + 1) * DQK] * QSCALE
    k = qk_ref[:, 512 + h * DQK:512 + (h + 1) * DQK]
    v = v_ref[:, h * DV:(h + 1) * DV]
    C = c_s[h]
    n = nm_s[h, 0:1, :]
    m = nm_s[h, 1:2, 0:1]
    f = _mlstm_chunk(q, k, glv[:, h:h + 1], glv[:, 4 + h:5 + h], m, n)
    num = f["a_in"] * _dot(q, C, 1, 1) + _dot(f["w"], v, 1, 0)
    hh = num / f["nrm"]
    cs_ref[0, h] = C
    nm_ref[0, h] = nm_s[h]
    c_s[h] = f["a_st"] * C + _dot(f["w_c"] * v, k, 0, 0)
    n_new = f["a_st"] * n + jnp.sum(f["w_c"] * k, axis=0, keepdims=True)
    rowi = lax.broadcasted_iota(jnp.int32, (8, DQK), 0)
    nm_s[h] = jnp.where(rowi == 0, n_new, jnp.where(rowi == 1, f["m_new"], 0.0))
    rm = lax.rsqrt(jnp.mean(hh * hh, axis=-1, keepdims=True) + EPS)
    sl = slice(h * DV, (h + 1) * DV)
    hm_ref[:, sl] = hh
    ym_ref[:, sl] = (hh * rm * hg_ref[:, sl] * _sigmoid(mo_ref[:, sl])).astype(ym_ref.dtype)


def _mlstm_bwd_parts(dym_ref, hm_ref, qk_ref, cp_ref, v_ref, gl_ref, mo_ref, hg_ref, cs_ref, nm_ref,
                     dp_ref, dc_ref, dgl_ref, dhg_ref, dc_s, dn_s):
        def init():
            dc_s[...] = jnp.zeros_like(dc_s)
            dn_s[...] = jnp.zeros_like(dn_s)
            dhg_ref[...] = jnp.zeros_like(dhg_ref)

        def zero():
            dp_ref[...] = jnp.zeros_like(dp_ref)
            dc_ref[...] = jnp.zeros_like(dc_ref)
            dgl_ref[...] = jnp.zeros_like(dgl_ref)

        def compute():
            glv = gl_ref[...]
            lane = lax.broadcasted_iota(jnp.int32, (LM, 128), 1)
            dgl = jnp.zeros((LM, 128), F32)
            for h in range(NH):
                sl = slice(h * DV, (h + 1) * DV)
                sq = slice(h * DQK, (h + 1) * DQK)
                sk = slice(512 + h * DQK, 512 + (h + 1) * DQK)
                hh = hm_ref[:, sl]
                gain = hg_ref[:, sl]
                rm = lax.rsqrt(jnp.mean(hh * hh, axis=-1, keepdims=True) + EPS)
                sg = _sigmoid(mo_ref[:, sl])
                dyv = dym_ref[:, sl]
                dno = dyv * sg
                dp_ref[:, 1024 + h * DV:1024 + (h + 1) * DV] = (dyv * hh * rm * gain * sg * (1.0 - sg)).astype(dp_ref.dtype)
                dhg_ref[:, sl] += jnp.sum(dno * hh * rm, axis=0, keepdims=True)
                dnog = dno * gain
                dh = rm * dnog - hh * (rm * rm * rm * jnp.mean(dnog * hh, axis=-1, keepdims=True))
                q = qk_ref[:, sq] * QSCALE
                k = qk_ref[:, sk]
                v = v_ref[:, sl]
                C = cs_ref[0, h]
                n = nm_ref[0, h, 0:1, :]
                m = nm_ref[0, h, 1:2, 0:1]
                f = _mlstm_chunk(q, k, glv[:, h:h + 1], glv[:, 4 + h:5 + h], m, n)
                eye = f["eye"]
                a_in, nrm, den, w = f["a_in"], f["nrm"], f["den"], f["w"]
                dnum = dh / nrm
                dnrm = -jnp.sum(dh * hh, axis=1, keepdims=True) / nrm
                dden = jnp.where(jnp.abs(den) >= f["floor"], dnrm * jnp.sign(den), 0.0)
                dw = _dot(dnum, v, 1, 1) + dden
                dv = _dot(w, dnum, 0, 0)
                ds = dw * f["dm"]
                e = dw * w
                qc = _dot(q, C, 1, 1)
                dq = _dot(ds, k, 1, 0) + a_in * _dot(dnum, C, 1, 0) + (a_in * dden) * n
                dk = _dot(ds, q, 0, 0)
                dC_in = _dot(a_in * dnum, q, 0, 0)
                dn_in = jnp.sum((a_in * dden) * q, axis=0, keepdims=True)
                da_in = jnp.sum(dnum * qc, axis=1, keepdims=True) + dden * f["qn"]
                col_e = _to_col(jnp.sum(e, axis=0, keepdims=True), eye)
                db = jnp.sum(e, axis=1, keepdims=True) + da_in * a_in - col_e
                dlogi = col_e
                dCp = dc_s[h]
                dnp = dn_s[h, 0:1, :]
                a_st, w_c = f["a_st"], f["w_c"]
                da_st = (jnp.sum(jnp.sum(dCp * C, axis=1, keepdims=True), axis=0, keepdims=True)
                         + jnp.sum(dnp * n, axis=1, keepdims=True))
                vdc = _dot(v, dCp, 1, 0)
                dw_c = jnp.sum((vdc + dnp) * k, axis=1, keepdims=True)
                dv = dv + w_c * _dot(k, dCp, 1, 1)
                dk = dk + w_c * (vdc + dnp)
                fw = dw_c * w_c
                dg = jnp.sum(fw, axis=0, keepdims=True) + da_st * a_st
                db = db - fw
                dlogi = dlogi + fw
                rowc = lax.broadcasted_iota(jnp.int32, (LM, 1), 0)
                db = db + jnp.where(rowc == LM - 1, dg, 0.0)
                triu = lax.broadcasted_iota(jnp.int32, (LM, LM), 1) >= lax.broadcasted_iota(jnp.int32, (LM, LM), 0)
                dlogf = jnp.sum(jnp.where(triu, _to_row(db, eye), 0.0), axis=1, keepdims=True)
                dc_s[h] = a_st * dCp + dC_in
                dn_new = a_st * dnp + dn_in
                dn_s[h] = jnp.zeros((8, DQK), F32) + dn_new
                cq, ck = cp_ref[:, sq], cp_ref[:, sk]
                s_q, s_k = _sigmoid(cq), _sigmoid(ck)
                dc_ref[:, sq] = dq * QSCALE * s_q * (1.0 + cq * (1.0 - s_q))
                dc_ref[:, sk] = dk * s_k * (1.0 + ck * (1.0 - s_k))
                dp_ref[:, sl] = dv.astype(dp_ref.dtype)
                dgl = jnp.where(lane == h, dlogi, jnp.where(lane == 4 + h, dlogf, dgl))
            dgl_ref[...] = dgl

        return init, zero, compute


def _gla_logs(sm, a2p, b2, valid):
    za = _dot(sm, a2p, 1, 0) + b2
    return za, jnp.where(valid, _log_sigmoid(za) * (1.0 / TAU), 0.0)


def _valid_rows(c, width):
    row = lax.broadcasted_iota(jnp.int32, (L, width), 0) + c * L
    return row >= NPADROWS


def _gla_cumsum(loga):
    tril, _, _ = _masks()
    return _dot_exact(tril.astype(F32), loga)


def _gla_chunk(q, k, la, bc):
    tril, _, _ = _masks()
    btot = jnp.sum(la, axis=0, keepdims=True)
    ebc = jnp.exp(bc)
    qd = q * ebc
    ki = k * jnp.exp(-bc)
    ke = k * jnp.exp(btot - bc)
    att = jnp.where(tril, _dot(qd, ki, 1, 1), 0.0)
    return dict(tril=tril, bc=bc, btot=btot, ebc=ebc, qd=qd, ki=ki, ke=ke, att=att)


def _col128(row):
    r = lax.broadcasted_iota(jnp.int32, (DQK, DQK), 0)
    c = lax.broadcasted_iota(jnp.int32, (DQK, DQK), 1)
    return jnp.sum(jnp.where(r == c, row, 0.0), axis=1, keepdims=True)


def _gla_fwd_parts(c, q_ref, k_ref, v_ref, gr_ref, sm_ref, a2_ref, b2_ref, hg_ref, hgl_ref, yg_ref, ss_ref, s_s):
        def init():
            s_s[...] = jnp.zeros_like(s_s)

        def zero():
            hgl_ref[...] = jnp.zeros_like(hgl_ref)
            yg_ref[...] = jnp.zeros_like(yg_ref)
            ss_ref[...] = jnp.zeros_like(ss_ref)

        def compute():
            for s in range(LM // L):
                rows = pl.ds(s * L, L)
                chunk(LM // L * c + s, q_ref.at[rows], k_ref.at[rows], v_ref.at[rows], gr_ref.at[rows], sm_ref.at[rows],
                      hgl_ref.at[rows], yg_ref.at[rows], ss_ref.at[pl.ds(s, 1)])

        def chunk(c, q_ref, k_ref, v_ref, gr_ref, sm_ref, hgl_ref, yg_ref, ss_ref):
            _, loga = _gla_logs(sm_ref[...], a2_ref[...], b2_ref[...], _valid_rows(c, 512))
            bc_all = _gla_cumsum(loga)
            for h in range(NH):
                sq = slice(h * DQK, (h + 1) * DQK)
                sl = slice(h * DV, (h + 1) * DV)
                q = q_ref[:, sq] * QSCALE
                k = k_ref[:, sq]
                v = v_ref[:, sl]
                S = s_s[h]
                f = _gla_chunk(q, k, loga[:, sq], bc_all[:, sq])
                o = _dot(f["att"], v, 1, 0) + _dot(f["qd"], S, 1, 0)
                ss_ref[0, h] = S
                s_s[h] = _col128(jnp.exp(f["btot"])) * S + _dot(f["ke"], v, 0, 0)
                rg = lax.rsqrt(jnp.mean(o * o, axis=-1, keepdims=True) + EPS)
                gr = gr_ref[:, sl]
                hgl_ref[:, sl] = o
                yg_ref[:, sl] = (o * rg * hg_ref[:, sl] * gr * _sigmoid(gr)).astype(yg_ref.dtype)

        return init, zero, compute


def _gla_bwd_parts(c, dy_ref, ho_ref, q_ref, k_ref, v_ref, gr_ref, sm_ref, a2_ref, b2_ref, hg_ref, ss_ref,
                   dp_ref, dga_ref, da2_ref, db2_ref, dhg_ref, ds_s):
        def init():
            ds_s[...] = jnp.zeros_like(ds_s)
            da2_ref[...] = jnp.zeros_like(da2_ref)
            db2_ref[...] = jnp.zeros_like(db2_ref)
            dhg_ref[...] = jnp.zeros_like(dhg_ref)

        def zero():
            dp_ref[...] = jnp.zeros_like(dp_ref)
            dga_ref[...] = jnp.zeros_like(dga_ref)

        def compute():
            for s in reversed(range(LM // L)):
                rows = pl.ds(s * L, L)
                chunk(LM // L * c + s, dy_ref.at[rows], ho_ref.at[rows], q_ref.at[rows], k_ref.at[rows], v_ref.at[rows],
                      gr_ref.at[rows], sm_ref.at[rows], ss_ref.at[pl.ds(s, 1)], dp_ref.at[rows], dga_ref.at[rows])

        def chunk(c, dy_ref, ho_ref, q_ref, k_ref, v_ref, gr_ref, sm_ref, ss_ref, dp_ref, dga_ref):
            valid = _valid_rows(c, 512)
            sm = sm_ref[...]
            za, loga = _gla_logs(sm, a2_ref[...], b2_ref[...], valid)
            dbcs = []
            for h in range(NH):
                sq = slice(h * DQK, (h + 1) * DQK)
                sl = slice(h * DV, (h + 1) * DV)
                o = ho_ref[:, sl]
                gain = hg_ref[:, sl]
                rg = lax.rsqrt(jnp.mean(o * o, axis=-1, keepdims=True) + EPS)
                gr = gr_ref[:, sl]
                sg = _sigmoid(gr)
                dyv = dy_ref[:, sl]
                dno = dyv * gr * sg
                dp_ref[:, 2048 + h * DV:2048 + (h + 1) * DV] = (
                    dyv * o * rg * gain * sg * (1.0 + gr * (1.0 - sg))).astype(dp_ref.dtype)
                dhg_ref[:, sl] += jnp.sum(dno * o * rg, axis=0, keepdims=True)
                dnog = dno * gain
                do = rg * dnog - o * (rg * rg * rg * jnp.mean(dnog * o, axis=-1, keepdims=True))
                q = q_ref[:, sq] * QSCALE
                k = k_ref[:, sq]
                v = v_ref[:, sl]
                S = ss_ref[0, h]
                f = _gla_chunk(q, k, loga[:, sq], _gla_cumsum(loga[:, sq]))
                tril, qd, ki, ke = f["tril"], f["qd"], f["ki"], f["ke"]
                dSp = ds_s[h]
                datt = jnp.where(tril, _dot(do, v, 1, 1), 0.0)
                dqd = _dot(do, S, 1, 1) + _dot(datt, ki, 1, 0)
                dki = _dot(datt, qd, 0, 0)
                dv = _dot(f["att"], do, 0, 0) + _dot(ke, dSp, 1, 0)
                dke = _dot(v, dSp, 1, 1)
                ebt = jnp.exp(f["btot"])
                dbtot = jnp.sum(dke * ke, axis=0, keepdims=True) + ebt * _to_row128(jnp.sum(dSp * S, axis=1, keepdims=True))
                ds_s[h] = _dot(qd, do, 0, 0) + _col128(ebt) * dSp
                dq = dqd * f["ebc"]
                dk = dki * jnp.exp(-f["bc"]) + dke * jnp.exp(f["btot"] - f["bc"])
                dbc = dqd * qd - dki * ki - dke * ke
                rowc = lax.broadcasted_iota(jnp.int32, (L, DQK), 0)
                dbc = dbc + jnp.where(rowc == L - 1, dbtot, 0.0)
                triu = lax.broadcasted_iota(jnp.int32, (L, L), 1) >= lax.broadcasted_iota(jnp.int32, (L, L), 0)
                dbcs.append(_dot_exact(triu.astype(F32), dbc))
                dp_ref[:, sq] = (dq * QSCALE).astype(dp_ref.dtype)
                dp_ref[:, 512 + h * DQK:512 + (h + 1) * DQK] = dk.astype(dp_ref.dtype)
                dp_ref[:, 1024 + h * DV:1024 + (h + 1) * DV] = dv.astype(dp_ref.dtype)
            dza = jnp.where(valid, jnp.concatenate(dbcs, axis=1) * (1.0 / TAU) * _sigmoid(-za), 0.0)
            dga_ref[...] = _dot(dza, a2_ref[...], 1, 1)
            da2_ref[...] += _dot(sm, dza, 0, 0)
            db2_ref[...] += jnp.sum(dza, axis=0, keepdims=True)

        return init, zero, compute


def mix_fwd(qk, proj, gl, m_head_g, a2p, b2, g_head_g):
    R = qk.shape[0]
    NC = R // LM
    G = LM // L

    def body(qk_ref, mv_ref, gl_ref, mo_ref, mhg_ref, gq_ref, gk_ref, gv_ref, gr_ref, sm_ref, a2_ref, b2_ref, ghg_ref,
             hm_ref, ym_ref, cs_ref, nm_ref, hgl_ref, yg_ref, ss_ref, c_s, nm_s, s_s):
        c = pl.program_id(0)
        g_init, g_zero, g_compute = _gla_fwd_parts(c, gq_ref, gk_ref, gv_ref, gr_ref, sm_ref, a2_ref, b2_ref, ghg_ref,
                                                   hgl_ref, yg_ref, ss_ref, s_s)

        @pl.when(c <= CH0)
        def _():
            c_s[...] = jnp.zeros_like(c_s)
            nm_s[...] = jnp.zeros_like(nm_s)
            g_init()

        @pl.when(c < CH0)
        def _():
            hm_ref[...] = jnp.zeros_like(hm_ref)
            ym_ref[...] = jnp.zeros_like(ym_ref)
            cs_ref[...] = jnp.zeros_like(cs_ref)
            nm_ref[...] = jnp.zeros_like(nm_ref)
            g_zero()

        @pl.when(c >= CH0)
        def _():
            glv = gl_ref[...]
            for h in range(NH):
                _mlstm_head_fwd(h, glv, qk_ref, mv_ref, mo_ref, mhg_ref, hm_ref, ym_ref, cs_ref, nm_ref, c_s, nm_s)
            g_compute()

    st_m = pl.BlockSpec((1, NH, DV, DQK), lambda c: (c, 0, 0, 0))
    st_n = pl.BlockSpec((1, NH, 8, DQK), lambda c: (c, 0, 0, 0))
    st_g = pl.BlockSpec((G, NH, DQK, DV), lambda c: (c, 0, 0, 0))
    return pl.pallas_call(
        body, name="mix_fwd", grid=(NC,),
        in_specs=[_rb(LM, 1024, 0), _rb(LM, 1024, MV0 // 1024), _rb(LM, 128, 0), _rb(LM, 1024, MO0 // 1024),
                  _const((1, 1024)),
                  _rb(LM, 512, GQ0 // 512), _rb(LM, 512, GK0 // 512), _rb(LM, 1024, GV0 // 1024),
                  _rb(LM, 1024, GR0 // 1024), _rb(LM, 128, SM0 // 128), _const((128, 512)), _const((1, 512)),
                  _const((1, 1024))],
        out_specs=[_rb(LM, 1024, 0), _rb(LM, 1024, 0), st_m, st_n, _rb(LM, 1024, 0), _rb(LM, 1024, 0), st_g],
        out_shape=[jax.ShapeDtypeStruct((R, 1024), F32), jax.ShapeDtypeStruct((R, 1024), MXU),
                   jax.ShapeDtypeStruct((NC, NH, DV, DQK), F32), jax.ShapeDtypeStruct((NC, NH, 8, DQK), F32),
                   jax.ShapeDtypeStruct((R, 1024), F32), jax.ShapeDtypeStruct((R, 1024), MXU),
                   jax.ShapeDtypeStruct((G * NC, NH, DQK, DV), F32)],
        scratch_shapes=[pltpu.VMEM((NH, DV, DQK), F32), pltpu.VMEM((NH, 8, DQK), F32),
                        pltpu.VMEM((NH, DQK, DV), F32)],
        compiler_params=_cp(("arbitrary",)),
    )(qk, proj, gl, proj, m_head_g, proj, proj, proj, proj, proj, a2p, b2, g_head_g)


def mix_bwd(dym, hm, qk, cpre, proj, gl, m_head_g, cs, nm, dyg, hgl, a2p, b2, g_head_g, ss, dproj):
    R = qk.shape[0]
    NC = R // LM
    rev = lambda c: NC - 1 - c
    GW = GR0 + 1024 - GQ0

    def body(dym_ref, hm_ref, qk_ref, cp_ref, mv_ref, gl_ref, mo_ref, mhg_ref, cs_ref, nm_ref,
             dyg_ref, ho_ref, gq_ref, gk_ref, gv_ref, gr_ref, sm_ref, a2_ref, b2_ref, ghg_ref, ss_ref, dp_in,
             dp_ref, dc_ref, dgl_ref, dmhg_ref, dga_ref, da2_ref, db2_ref, dghg_ref, dc_s, dn_s, ds_s):
        del dp_in
        step = pl.program_id(0)
        c = NC - 1 - step
        m_init, m_zero, m_compute = _mlstm_bwd_parts(
            dym_ref, hm_ref, qk_ref, cp_ref, mv_ref, gl_ref, mo_ref, mhg_ref, cs_ref, nm_ref,
            dp_ref.at[:, 0:GQ0], dc_ref, dgl_ref, dmhg_ref, dc_s, dn_s)
        g_init, g_zero, g_compute = _gla_bwd_parts(
            c, dyg_ref, ho_ref, gq_ref, gk_ref, gv_ref, gr_ref, sm_ref, a2_ref, b2_ref, ghg_ref, ss_ref,
            dp_ref.at[:, GQ0:GQ0 + GW], dga_ref, da2_ref, db2_ref, dghg_ref, ds_s)

        @pl.when(step == 0)
        def _():
            m_init()
            g_init()

        @pl.when(c < CH0)
        def _():
            m_zero()
            g_zero()

        @pl.when(c >= CH0)
        def _():
            m_compute()
            g_compute()

    def rows(w, cb):
        return pl.BlockSpec((LM, w), lambda c: (rev(c), cb))

    return pl.pallas_call(
        body, name="mix_bwd", grid=(NC,),
        in_specs=[rows(1024, 0), rows(1024, 0), rows(1024, 0), rows(1024, 0), rows(1024, MV0 // 1024), rows(128, 0),
                  rows(1024, MO0 // 1024), _const((1, 1024)),
                  pl.BlockSpec((1, NH, DV, DQK), lambda c: (rev(c), 0, 0, 0)),
                  pl.BlockSpec((1, NH, 8, DQK), lambda c: (rev(c), 0, 0, 0)),
                  rows(1024, 0), rows(1024, 0), rows(512, GQ0 // 512), rows(512, GK0 // 512),
                  rows(1024, GV0 // 1024), rows(1024, GR0 // 1024), rows(128, SM0 // 128),
                  _const((128, 512)), _const((1, 512)), _const((1, 1024)),
                  pl.BlockSpec((LM // L, NH, DQK, DV), lambda c: (rev(c), 0, 0, 0)),
                  pl.BlockSpec(memory_space=pl.ANY)],
        out_specs=[rows(GQ0 + GW, 0), rows(1024, 0), rows(128, 0), _const((1, 1024)),
                   rows(128, 0), _const((128, 512)), _const((1, 512)), _const((1, 1024))],
        out_shape=[jax.ShapeDtypeStruct((R, NP), MXU), jax.ShapeDtypeStruct((R, 1024), F32),
                   jax.ShapeDtypeStruct((R, 128), F32), jax.ShapeDtypeStruct((1, 1024), F32),
                   jax.ShapeDtypeStruct((R, 128), F32), jax.ShapeDtypeStruct((128, 512), F32),
                   jax.ShapeDtypeStruct((1, 512), F32), jax.ShapeDtypeStruct((1, 1024), F32)],
        scratch_shapes=[pltpu.VMEM((NH, DV, DQK), F32), pltpu.VMEM((NH, 8, DQK), F32),
                        pltpu.VMEM((NH, DQK, DV), F32)],
        input_output_aliases={21: 0},
        compiler_params=_cp(("arbitrary",)),
    )(dym, hm, qk, cpre, proj, gl, proj, m_head_g, cs, nm, dyg, hgl, proj, proj, proj, proj, proj, a2p, b2,
      g_head_g, ss, dproj)


def _to_row128(col):
    r = lax.broadcasted_iota(jnp.int32, (DQK, DQK), 0)
    c = lax.broadcasted_iota(jnp.int32, (DQK, DQK), 1)
    return jnp.sum(jnp.where(r == c, col, 0.0), axis=0, keepdims=True)


def local_step(x, target, meta, norm1_g, wp, conv_w, conv_b, m_gate_b, g_a2, g_a2_b, m_head_g, g_head_g,
               norm2_g, final_g, late_weights, send_early, send_wp, first_order=None):
    seq = x.shape[0]
    assert seq % TM == 0
    gb_row = jnp.zeros((1, 128), F32).at[0, 0:8].set(m_gate_b.reshape(8))
    a2p = jnp.zeros((128, 512), F32).at[8:8 + RANK].set(g_a2)
    mhg = m_head_g.reshape(1, 1024)
    ghg = g_head_g.reshape(1, 1024)

    xn = rms_fwd_input(x, meta, norm1_g, "rms1_fwd")
    proj = matmul(xn, wp, "nn", "proj_fwd", tm=1536, order=first_order)
    cpre, qk, gl = prep_fwd(proj, conv_w, conv_b, gb_row)
    hm, ym, cs, nm, hgl, yg, ss = mix_fwd(qk, proj, gl, mhg, a2p, g_a2_b, ghg)
    w_bm, w_bg, w_out, w_gu, w_down = late_weights(ym)
    bm, bg, merged = merge_fwd(ym, yg, w_bm, w_bg, proj)
    h1, hn = out_proj_norm(merged, w_out, x, meta, norm2_g)
    au, ff = ff_in_fwd(hn, w_gu)
    dh2, loss, d_final_g = ff_down_loss(ff, w_down, h1, final_g.reshape(1, D), target)

    d_w_down = matmul(ff, dh2, "tn", "ff_down_wgrad", tm=1408, tk=1536)
    dau = ff_down_dgrad(dh2, w_down, au)
    d_w_gu = matmul(hn, dau, "tn", "ff_in_wgrad", tm=1024, tn=1408, tk=1536)
    dh1, d_norm2_g = dgrad_rms_bwd(dau, w_gu, h1, norm2_g, dh2, "ff_in_dgrad", 768, 1408)

    d_w_out = matmul(merged, dh1, "tn", "out_wgrad", tm=1024, tk=1536)
    dbm, dbg, dproj = merge_bwd(dh1, w_out, bm, bg, proj)
    d_w_bm = matmul(ym, dbm, "tn", "branch_m_wgrad", tm=1024, tk=1536)
    d_w_bg = matmul(yg, dbg, "tn", "branch_g_wgrad", tm=1024, tk=1536)
    token = send_early(dict(w_branch_m=d_w_bm, w_branch_g=d_w_bg, w_out=d_w_out, w_gu=d_w_gu, w_ff_down=d_w_down))
    dym = matmul(dbm, w_bm, "nt", "branch_m_dgrad", order=token)
    dyg = matmul(dbg, w_bg, "nt", "branch_g_dgrad")
    dproj, dc, dgl, d_mhg, dga, d_a2p, d_a2b, d_ghg = mix_bwd(
        dym, hm, qk, cpre, proj, gl, mhg, cs, nm, dyg, hgl, a2p, g_a2_b, ghg, ss, dproj)
    dproj, d_conv = conv_bwd(dc, proj, conv_w, dproj)
    dproj, d_gb = small_bwd(dgl, dga, proj, gb_row, dproj)
    d_wp = matmul(xn, dproj, "tn", "proj_wgrad", tm=1024, tn=1664, tk=1536)
    token = send_wp(d_wp)
    dxn = matmul(dproj, wp, "nt", "proj_dgrad", tm=1536, tk=1664, order=token)
    grad_x, d_meta, d_norm1_g = rms_bwd_input(dxn, x, meta, norm1_g, dh1, "rms1_bwd")

    grads = dict(
        meta_tokens=d_meta, norm1_g=d_norm1_g, conv_w=d_conv[0:4], conv_b=d_conv[4:5], m_gate_b=d_gb[0, 0:8].reshape(1, 2, 4),
        g_a2=d_a2p[8:8 + RANK], g_a2_b=d_a2b, m_head_g=d_mhg.reshape(NH, DV), g_head_g=d_ghg.reshape(NH, DV),
        norm2_g=d_norm2_g, final_g=d_final_g)
    return loss, grad_x, grads


_SEGS = [(0, 1024, QK0), (1024, 2048, MV0), (2048, 2056, SM0), (2056, 3080, MO0), (3080, 5128, GQ0),
         (5128, 5144, SM0 + 8), (5144, 6168, GR0), (6168, 8216, GM0)]
SHARD_W = NPROJ // NDEV


def regroup_cols(w8):
    parts = []
    for lo, hi, _ in sorted(_SEGS, key=lambda s: s[2]):
        while lo < hi:
            j = lo // SHARD_W
            end = min(hi, (j + 1) * SHARD_W)
            parts.append(w8[j, :, lo - j * SHARD_W:end - j * SHARD_W])
            lo = end
    parts.append(jnp.zeros((w8.shape[1], NP - NPROJ), w8.dtype))
    return jnp.concatenate(parts, axis=1)


def ungroup_cols(g):
    blocks = []
    for j in range(NDEV):
        lo, hi = j * SHARD_W, (j + 1) * SHARD_W
        parts = []
        for s_lo, s_hi, s_at in _SEGS:
            a, b = max(lo, s_lo), min(hi, s_hi)
            if a < b:
                parts.append(g[:, s_at + a - s_lo:s_at + b - s_lo])
        blocks.append(jnp.concatenate(parts, axis=1))
    return jnp.stack(blocks)


def col_blocks(g):
    r, c8 = g.shape
    return jnp.transpose(g.reshape(r, NDEV, c8 // NDEV), (1, 0, 2))


def from_col_blocks(g8):
    n, r, c = g8.shape
    return jnp.transpose(g8, (1, 0, 2)).reshape(r, n * c)


_MESHID = pl.DeviceIdType.MESH
_RELS = [(0, 0, 1), (1, 0, 0), (0, 1, 0), (1, 1, 0), (1, 0, 1), (0, 1, 1), (1, 1, 1)]


def _flip(v, bit):
    return 1 - v if bit else v


def all_gather(arrs, name):
    n = len(arrs)

    def body(*refs):
        ins, outs = refs[:n], refs[n:2 * n]
        send_sems, recv_sems, local_sems = refs[2 * n:]
        x, y, c = lax.axis_index("x"), lax.axis_index("y"), lax.axis_index("c")
        me, sibling = (x, y, c), (x, y, 1 - c)
        chips = [(1 - x, y), (x, 1 - y), (1 - x, 1 - y)]

        def slot(p):
            return 4 * p[0] + 2 * p[1] + p[2]

        def copy(a, k, block, to, src=None):
            dst = outs[a].at[slot(block)]
            return pltpu.make_async_remote_copy(
                src_ref=dst if src is None else src, dst_ref=dst,
                send_sem=send_sems.at[a, k], recv_sem=recv_sems.at[a, k],
                device_id=to, device_id_type=_MESHID)

        mine = [pltpu.make_async_copy(ins[a], outs[a].at[slot(me)], local_sems.at[a]) for a in range(n)]
        for cp in mine:
            cp.start()
        first = []
        for a in range(n):
            first.append(copy(a, 0, me, sibling, src=ins[a]))
            first += [copy(a, 1 + j, me, (*chip, c), src=ins[a]) for j, chip in enumerate(chips)]
        for cp in first:
            cp.start()
        passed = []
        for j, chip in enumerate(chips):
            for a in range(n):
                copy(a, 1 + j, (*chip, c), me).wait_recv()
                fwd = copy(a, 4 + j, (*chip, c), sibling)
                fwd.start()
                passed.append(fwd)
        for a in range(n):
            copy(a, 0, sibling, me).wait_recv()
            for j, chip in enumerate(chips):
                copy(a, 4 + j, (*chip, 1 - c), me).wait_recv()
        for cp in first + passed:
            cp.wait_send()
        for cp in mine:
            cp.wait()

    anyspec = pl.BlockSpec(memory_space=pl.ANY)
    return pl.pallas_call(
        body, name=name,
        in_specs=[anyspec] * n, out_specs=[anyspec] * n,
        out_shape=[jax.ShapeDtypeStruct((NDEV,) + a.shape, a.dtype) for a in arrs],
        scratch_shapes=[pltpu.SemaphoreType.DMA((n, 7)), pltpu.SemaphoreType.DMA((n, 7)),
                        pltpu.SemaphoreType.DMA((n,))],
    )(*arrs)


def exchange(blocks, rep, name):
    n = len(blocks)

    def body(*refs):
        b_refs, r_ref = refs[:n], refs[n]
        ob_refs, or_ref = refs[n + 1:2 * n + 1], refs[2 * n + 1]
        send_sems, recv_sems, local_sems = refs[2 * n + 2:]
        x, y, c = lax.axis_index("x"), lax.axis_index("y"), lax.axis_index("c")
        me = 4 * x + 2 * y + c

        def pairs(src_slot, dst_slot):
            return [(b_refs[a].at[src_slot], ob_refs[a].at[dst_slot]) for a in range(n)] + [(r_ref, or_ref.at[dst_slot])]

        loc = [pltpu.make_async_copy(s, d, local_sems.at[a]) for a, (s, d) in enumerate(pairs(me, me))]
        for cp in loc:
            cp.start()
        sends = []
        for k, (fx, fy, fc) in enumerate(_RELS):
            peer = (_flip(x, fx), _flip(y, fy), _flip(c, fc))
            pid = 4 * peer[0] + 2 * peer[1] + peer[2]
            for a, (s, d) in enumerate(pairs(pid, me)):
                sends.append(pltpu.make_async_remote_copy(
                    src_ref=s, dst_ref=d, send_sem=send_sems.at[a, k], recv_sem=recv_sems.at[a, k],
                    device_id=peer, device_id_type=_MESHID))
        for cp in sends:
            cp.start()
        for k, (fx, fy, fc) in enumerate(_RELS):
            peer = (_flip(x, fx), _flip(y, fy), _flip(c, fc))
            pid = 4 * peer[0] + 2 * peer[1] + peer[2]
            for a, (s, d) in enumerate(pairs(pid, pid)):
                pltpu.make_async_remote_copy(
                    src_ref=s, dst_ref=d, send_sem=send_sems.at[a, k], recv_sem=recv_sems.at[a, k],
                    device_id=peer, device_id_type=_MESHID).wait_recv()
        for cp in sends:
            cp.wait_send()
        for cp in loc:
            cp.wait()

    anyspec = pl.BlockSpec(memory_space=pl.ANY)
    return pl.pallas_call(
        body, name=name,
        in_specs=[anyspec] * (n + 1), out_specs=[anyspec] * (n + 1),
        out_shape=[jax.ShapeDtypeStruct(b.shape, b.dtype) for b in blocks]
        + [jax.ShapeDtypeStruct((NDEV,) + rep.shape, rep.dtype)],
        scratch_shapes=[pltpu.SemaphoreType.DMA((n + 1, 7)), pltpu.SemaphoreType.DMA((n + 1, 7)),
                        pltpu.SemaphoreType.DMA((n + 1,))],
    )(*blocks, rep)


_HBM = pl.BlockSpec(memory_space=pltpu.HBM)
_SEM = pl.BlockSpec(memory_space=pltpu.SEMAPHORE)
_EFFECT = pltpu.SideEffectType.DATAFLOW_SIDE_EFFECTING


def _peer_ids():
    x, y, c = lax.axis_index("x"), lax.axis_index("y"), lax.axis_index("c")
    peers = []
    for fx, fy, fc in _RELS:
        p = (_flip(x, fx), _flip(y, fy), _flip(c, fc))
        peers.append((p, 4 * p[0] + 2 * p[1] + p[2]))
    return 4 * x + 2 * y + c, peers


def _split_copy(src, land, a, k, peer, src_slot, dst_slot, send_sems, recv_sems):
    return pltpu.make_async_remote_copy(
        src_ref=src if src_slot is None else src.at[src_slot], dst_ref=land.at[dst_slot],
        send_sem=send_sems.at[7 * a + k], recv_sem=recv_sems.at[7 * a + k], device_id=peer, device_id_type=_MESHID)


def _own_copy(src, land, a, n, me, per_peer, send_sems):
    return pltpu.make_async_copy(src.at[me] if per_peer else src, land.at[me], send_sems.at[7 * n + a])


def send_start(srcs, per_peer, order, name):
    n = len(srcs)
    lands = [lax.empty((NDEV,) + (s.shape[1:] if per_peer else s.shape), s.dtype) for s in srcs]

    def body(*refs):
        src_refs, land_refs = refs[1:1 + n], refs[1 + n:1 + 2 * n]
        send_sems, recv_sems = refs[1 + 2 * n], refs[2 + 2 * n]
        token = refs[3 + 4 * n]
        me, peers = _peer_ids()
        for a in range(n):
            _own_copy(src_refs[a], land_refs[a], a, n, me, per_peer, send_sems).start()
        for a in range(n):
            for k, (peer, pid) in enumerate(peers):
                _split_copy(src_refs[a], land_refs[a], a, k, peer, pid if per_peer else None, me,
                            send_sems, recv_sems).start()
        token[...] = jnp.zeros_like(token)

    outs = pl.pallas_call(
        body, name=name,
        in_specs=[pl.BlockSpec(memory_space=pl.ANY)] + [_HBM] * (2 * n),
        out_shape=(pltpu.SemaphoreType.DMA((8 * n,)), pltpu.SemaphoreType.DMA((7 * n,)),
                   *[pltpu.HBM(s.shape, s.dtype) for s in srcs], *[pltpu.HBM(l.shape, l.dtype) for l in lands],
                   jax.ShapeDtypeStruct((8, 128), F32)),
        out_specs=(_SEM, _SEM, *[_HBM] * (2 * n), pl.BlockSpec(memory_space=pltpu.VMEM)),
        input_output_aliases={1 + i: 2 + i for i in range(2 * n)},
        compiler_params=pltpu.CompilerParams(has_side_effects=_EFFECT),
    )(order, *[pltpu.with_memory_space_constraint(s, pltpu.HBM) for s in srcs],
      *[pltpu.with_memory_space_constraint(l, pltpu.HBM) for l in lands])
    return (n, per_peer, outs[0], outs[1], outs[2:2 + n], outs[2 + n:2 + 2 * n]), outs[2 + 2 * n]


def send_wait(handle, after, name):
    n, per_peer, send_sems, recv_sems, src_thru, land_thru = handle

    def body(*refs):
        src_refs, land_refs = refs[:n], refs[n:2 * n]
        s_sems, r_sems = refs[2 * n], refs[2 * n + 1]
        me, peers = _peer_ids()
        for a in range(n):
            _own_copy(src_refs[a], land_refs[a], a, n, me, per_peer, s_sems).wait()
            for k, (peer, pid) in enumerate(peers):
                cp = _split_copy(src_refs[a], land_refs[a], a, k, peer, pid if per_peer else None, pid, s_sems, r_sems)
                cp.wait_send()
                cp.wait_recv()

    outs = pl.pallas_call(
        body, name=name,
        in_specs=[_HBM] * (2 * n) + [_SEM, _SEM, pl.BlockSpec(memory_space=pl.ANY)],
        out_shape=tuple(pltpu.HBM(t.shape, t.dtype) for t in (*src_thru, *land_thru)),
        out_specs=tuple([_HBM] * (2 * n)),
        input_output_aliases={i: i for i in range(2 * n)},
        compiler_params=pltpu.CompilerParams(has_side_effects=_EFFECT),
    )(*src_thru, *land_thru, send_sems, recv_sems, after)
    return list(outs[n:2 * n])


def adamw(parts, w, m, v, name, tr):
    npart, r, c = parts.shape
    c1 = 1.0 - ADAM_B1 ** ADAM_STEP
    c2 = 1.0 - ADAM_B2 ** ADAM_STEP

    def body(p_ref, w_ref, m_ref, v_ref, g_ref, d_ref, nm_ref, nv_ref):
        g = p_ref[0].astype(F32)
        for j in range(1, npart):
            g = g + p_ref[j].astype(F32)
        mn = ADAM_B1 * m_ref[...] + (1.0 - ADAM_B1) * g
        vn = ADAM_B2 * v_ref[...] + (1.0 - ADAM_B2) * (g * g)
        g_ref[...] = g
        nm_ref[...] = mn
        nv_ref[...] = vn
        d_ref[...] = -ADAM_LR * ((mn / c1) / (jnp.sqrt(vn / c2) + ADAM_EPS) + ADAM_WD * w_ref[...])

    spec = _rb(tr, c, 0)
    return pl.pallas_call(
        body, name=name, grid=(r // tr,),
        in_specs=[pl.BlockSpec((npart, tr, c), lambda i: (0, i, 0)), spec, spec, spec],
        out_specs=[spec] * 4, out_shape=[jax.ShapeDtypeStruct((r, c), F32)] * 4,
        compiler_params=_cp(("parallel",)),
    )(parts, w, m, v)


def sum_parts(parts, name, tc):
    npart, r, c = parts.shape

    def body(p_ref, o_ref):
        g = p_ref[0].astype(F32)
        for j in range(1, npart):
            g = g + p_ref[j].astype(F32)
        o_ref[...] = g

    return pl.pallas_call(
        body, name=name, grid=(c // tc,),
        in_specs=[pl.BlockSpec((npart, r, tc), lambda i: (0, 0, i))],
        out_specs=pl.BlockSpec((r, tc), lambda i: (0, i)),
        out_shape=jax.ShapeDtypeStruct((r, c), F32),
        compiler_params=_cp(("parallel",)),
    )(parts)


TINY = [("meta_tokens", (16, 1024)), ("conv_w", (4, 1024)), ("g_a2", (16, 512)), ("m_head_g", (4, 256)),
        ("g_head_g", (4, 256))]
REPL = [("norm1_g", (1, 1024)), ("conv_b", (1, 1024)), ("m_gate_b", (1, 2, 4)), ("g_a2_b", (1, 512)),
        ("norm2_g", (1, 1024)), ("final_g", (1024,))]
TINY_SIZE = 16 * 1024 + 4 * 1024 + 16 * 512 + 2 * 4 * 256
REPL_SIZE = 1024 + 1024 + 8 + 512 + 1024 + 1024
ROWS_GATHER = 8
ROWS_REP = 40
ROWS_OWN = 16


def pack_rows(vecs, rows):
    flat = jnp.concatenate([v.reshape(-1) for v in vecs])
    return jnp.pad(flat, (0, rows * 1024 - flat.shape[0])).reshape(rows, 1024)


def unpack_rows(packed, shapes):
    flat = packed.reshape(-1)
    out, off = [], 0
    for s in shapes:
        n = 1
        for d in s:
            n *= d
        out.append(flat[off:off + n].reshape(s))
        off += n
    return out


def kernel(x, meta_tokens, norm1_g, w_in, conv_w, conv_b, m_gate_b, g_a2, g_a2_b, m_head_g, g_head_g, w_branch_m, w_branch_g, w_out, norm2_g, w_ff_gate, w_ff_up, w_ff_down, final_g, loss_target, m_meta_tokens, m_norm1_g, m_w_in, m_conv_w, m_conv_b, m_m_gate_b, m_g_a2, m_g_a2_b, m_m_head_g, m_g_head_g, m_w_branch_m, m_w_branch_g, m_w_out, m_norm2_g, m_w_ff_gate, m_w_ff_up, m_w_ff_down, m_final_g, v_meta_tokens, v_norm1_g, v_w_in, v_conv_w, v_conv_b, v_m_gate_b, v_g_a2, v_g_a2_b, v_m_head_g, v_g_head_g, v_w_branch_m, v_w_branch_g, v_w_out, v_norm2_g, v_w_ff_gate, v_w_ff_up, v_w_ff_down, v_final_g):
    w_sh = dict(meta_tokens=meta_tokens, w_in=w_in[0], conv_w=conv_w[0], g_a2=g_a2[0], m_head_g=m_head_g[0],
                g_head_g=g_head_g[0], w_branch_m=w_branch_m[0], w_branch_g=w_branch_g[0], w_out=w_out[0],
                w_ff_gate=w_ff_gate[0], w_ff_up=w_ff_up[0], w_ff_down=w_ff_down[0])
    m_sh = dict(meta_tokens=m_meta_tokens, w_in=m_w_in[0], conv_w=m_conv_w[0], g_a2=m_g_a2[0],
                m_head_g=m_m_head_g[0], g_head_g=m_g_head_g[0], w_branch_m=m_w_branch_m[0],
                w_branch_g=m_w_branch_g[0], w_out=m_w_out[0], w_ff_gate=m_w_ff_gate[0], w_ff_up=m_w_ff_up[0],
                w_ff_down=m_w_ff_down[0])
    v_sh = dict(meta_tokens=v_meta_tokens, w_in=v_w_in[0], conv_w=v_conv_w[0], g_a2=v_g_a2[0],
                m_head_g=v_m_head_g[0], g_head_g=v_g_head_g[0], w_branch_m=v_w_branch_m[0],
                w_branch_g=v_w_branch_g[0], w_out=v_w_out[0], w_ff_gate=v_w_ff_gate[0], w_ff_up=v_w_ff_up[0],
                w_ff_down=v_w_ff_down[0])
    w_rep = dict(norm1_g=norm1_g, conv_b=conv_b, m_gate_b=m_gate_b, g_a2_b=g_a2_b, norm2_g=norm2_g, final_g=final_g)
    m_rep = dict(norm1_g=m_norm1_g, conv_b=m_conv_b, m_gate_b=m_m_gate_b, g_a2_b=m_g_a2_b, norm2_g=m_norm2_g,
                 final_g=m_final_g)
    v_rep = dict(norm1_g=v_norm1_g, conv_b=v_conv_b, m_gate_b=v_m_gate_b, g_a2_b=v_g_a2_b, norm2_g=v_norm2_g,
                 final_g=v_final_g)
    dev = 4 * lax.axis_index("x") + 2 * lax.axis_index("y") + lax.axis_index("c")
    tiny_names = [n for n, _ in TINY]
    repl_names = [n for n, _ in REPL]
    tiny_shard_shapes = [(s[0], s[1] // NDEV) for _, s in TINY]

    in8, tiny8 = all_gather([w_sh["w_in"].astype(MXU), pack_rows([w_sh[n] for n in tiny_names], ROWS_GATHER)],
                            "param_all_gather")
    late_names = ["w_branch_m", "w_branch_g", "w_out", "w_ff_gate", "w_ff_up", "w_ff_down"]
    late, first_order = send_start([w_sh[n].astype(MXU) for n in late_names], False, tiny8, "late_weights_start")
    wp = regroup_cols(in8)
    handles = {}

    def late_weights(after):
        bm8, bg8, out8, ffg8, ffu8, ffd8 = send_wait(late, after, "late_weights_wait")
        w_gu = interleave_gu(from_col_blocks(ffg8), from_col_blocks(ffu8))
        return bm8.reshape(D, D), bg8.reshape(D, D), out8.reshape(D, D), w_gu, ffd8.reshape(DFF, D)

    def send_early(g):
        d_gate, d_up = split_gu(g["w_gu"])
        blocks = [g["w_branch_m"].reshape(NDEV, D // NDEV, D).astype(WIRE),
                  g["w_branch_g"].reshape(NDEV, D // NDEV, D).astype(WIRE),
                  g["w_out"].reshape(NDEV, D // NDEV, D).astype(WIRE),
                  col_blocks(d_gate).astype(WIRE), col_blocks(d_up).astype(WIRE),
                  g["w_ff_down"].reshape(NDEV, DFF // NDEV, D).astype(WIRE)]
        handles["early"], token = send_start(blocks, True, blocks[0], "early_grads_start")
        return token

    def send_wp(d_wp):
        blocks = [ungroup_cols(d_wp).astype(WIRE)]
        handles["wp"], token = send_start(blocks, True, blocks[0], "proj_grads_start")
        return token

    tiny_full = {}
    for j in range(NDEV):
        for name, blk in zip(tiny_names, unpack_rows(tiny8[j], tiny_shard_shapes)):
            tiny_full.setdefault(name, []).append(blk)
    tiny_full = {n: jnp.concatenate(v, axis=1) for n, v in tiny_full.items()}

    loss, grad_x, g = local_step(
        x[0], loss_target[0], tiny_full["meta_tokens"], norm1_g, wp, tiny_full["conv_w"], conv_b, m_gate_b[0],
        tiny_full["g_a2"], g_a2_b, tiny_full["m_head_g"], tiny_full["g_head_g"], norm2_g, final_g,
        late_weights, send_early, send_wp, first_order)

    rep = pack_rows([g[n] for n in tiny_names + repl_names] + [loss[0, 0:1]], ROWS_REP)
    (got_rep,) = exchange([], rep, "small_grad_exchange")
    got_early = send_wait(handles["early"], got_rep, "early_grads_wait")
    (got_wp,) = send_wait(handles["wp"], got_rep, "proj_grads_wait")

    result = {}

    def update(name, parts, tr):
        outs = adamw(parts, w_sh[name], m_sh[name], v_sh[name], "adamw_" + name, tr)
        for kind, arr in zip(("grad", "delta", "new_m", "new_v"), outs):
            result[kind, name] = arr[None]

    update("w_in", got_wp, 128)
    update("w_branch_m", got_early[0], 128)
    update("w_branch_g", got_early[1], 128)
    update("w_out", got_early[2], 128)
    update("w_ff_gate", got_early[3], 256)
    update("w_ff_up", got_early[4], 256)
    update("w_ff_down", got_early[5], DFF // NDEV)

    rep_sum = sum_parts(got_rep, "sum_small", 1024)
    rep_g = unpack_rows(rep_sum, [s for _, s in TINY] + [s for _, s in REPL] + [(1,)])
    own_g = [lax.dynamic_slice_in_dim(gf, dev * ss[1], ss[1], axis=1) for gf, ss in zip(rep_g, tiny_shard_shapes)]
    own_g += rep_g[len(TINY):len(TINY) + len(REPL)]
    w_all = {**w_sh, **w_rep}
    m_all = {**m_sh, **m_rep}
    v_all = {**v_sh, **v_rep}
    names = tiny_names + repl_names
    outs = adamw(pack_rows(own_g, ROWS_OWN)[None], pack_rows([w_all[n] for n in names], ROWS_OWN),
                 pack_rows([m_all[n] for n in names], ROWS_OWN), pack_rows([v_all[n] for n in names], ROWS_OWN),
                 "adamw_small", ROWS_OWN)
    shapes = tiny_shard_shapes + [s for _, s in REPL]
    for kind, packed in zip(("grad", "delta", "new_m", "new_v"), outs):
        for name, arr in zip(names, unpack_rows(packed, shapes)):
            result[kind, name] = arr[None] if name in tiny_names and name != "meta_tokens" else arr
    loss_total = rep_g[-1][0]
    order = ["meta_tokens", "norm1_g", "w_in", "conv_w", "conv_b", "m_gate_b", "g_a2", "g_a2_b", "m_head_g", "g_head_g",
             "w_branch_m", "w_branch_g", "w_out", "norm2_g", "w_ff_gate", "w_ff_up", "w_ff_down", "final_g"]
    return (loss_total, grad_x[None], *[result[kind, n] for kind in ("grad", "delta", "new_m", "new_v") for n in order])
```

```python
import functools

import jax
import jax.numpy as jnp
from jax import lax
from jax.experimental import pallas as pl
from jax.experimental.pallas import tpu as pltpu

F32 = jnp.float32
MXU = jnp.bfloat16
WIRE = jnp.bfloat16

D = 1024
NH = 4
DV = 256
DQK = 128
L = 64
NMETA = 16
PADR = 512
LM = 256
CH0 = PADR // LM - 1
NPADROWS = PADR - NMETA
RANK = 16
DFF = 2816
EPS = 1e-6
TAU = 16.0
QSCALE = DQK ** -0.5
NEG = -1e30
NDEV = 8

MV0, MO0, GQ0, GK0, GV0, GR0, QK0, GM0, GG0, SM0 = 0, 1024, 2048, 2560, 3072, 4096, 5120, 6144, 7168, 8192
NP = 8320
NPROJ = 8216

ADAM_LR, ADAM_B1, ADAM_B2, ADAM_EPS, ADAM_WD, ADAM_STEP = 0.001, 0.9, 0.999, 1e-08, 0.01, 10

VMEM_LIMIT = 56 * 1024 * 1024
TM = 512


def _cp(sem):
    return pltpu.CompilerParams(dimension_semantics=sem, vmem_limit_bytes=VMEM_LIMIT)


def _sigmoid(x):
    return 1.0 / (1.0 + jnp.exp(-x))


def _log_sigmoid(x):
    return jnp.minimum(x, 0.0) - jnp.log1p(jnp.exp(-jnp.abs(x)))


def _dot(a, b, ca, cb):
    return lax.dot_general(a.astype(MXU), b.astype(MXU), (((ca,), (cb,)), ((), ())), preferred_element_type=F32)


def _dot_exact(a, b):
    return lax.dot_general(a, b, (((1,), (0,)), ((), ())), precision=lax.Precision.HIGHEST,
                           preferred_element_type=F32)


def _rb(tm, w, cb):
    return pl.BlockSpec((tm, w), lambda i: (i, cb))


def _const(shape):
    nd = len(shape)
    return pl.BlockSpec(shape, lambda i: (0,) * nd)


def _pick(n, target):
    if n <= target:
        return n
    best = None
    for t in range(128, target + 1, 128):
        if n % t == 0:
            best = t
    assert best is not None, (n, target)
    return best


def matmul(a, b, mode, name, add=None, out_dtype=F32, tm=512, tn=1664, tk=1024, order=None):
    if mode == "nn":
        (M, K), (K2, N) = a.shape, b.shape
    elif mode == "nt":
        (M, K), (N, K2) = a.shape, b.shape
    else:
        (K, M), (K2, N) = a.shape, b.shape
    assert K == K2, (a.shape, b.shape, mode)
    tm, tn, tk = _pick(M, tm), _pick(N, tn), _pick(K, tk)
    nk = K // tk
    assert nk == 1 or out_dtype == F32
    ca, cb = {"nn": (1, 0), "nt": (1, 1), "tn": (0, 0)}[mode]
    a_spec = {"nn": pl.BlockSpec((tm, tk), lambda j, i, k: (i, k)),
              "nt": pl.BlockSpec((tm, tk), lambda j, i, k: (i, k)),
              "tn": pl.BlockSpec((tk, tm), lambda j, i, k: (k, i))}[mode]
    b_spec = {"nn": pl.BlockSpec((tk, tn), lambda j, i, k: (k, j)),
              "nt": pl.BlockSpec((tn, tk), lambda j, i, k: (j, k)),
              "tn": pl.BlockSpec((tk, tn), lambda j, i, k: (k, j))}[mode]
    o_spec = pl.BlockSpec((tm, tn), lambda j, i, k: (i, j))
    has_add = add is not None

    def body(*refs):
        if order is not None:
            refs = refs[:-2] + refs[-1:]
        if has_add:
            a_ref, b_ref, add_ref, o_ref = refs
        else:
            a_ref, b_ref, o_ref = refs
            add_ref = None
        part = _dot(a_ref[...], b_ref[...], ca, cb)
        if nk == 1:
            if has_add:
                part = part + add_ref[...]
            o_ref[...] = part.astype(o_ref.dtype)
            return
        k = pl.program_id(2)

        @pl.when(k == 0)
        def _():
            o_ref[...] = part + add_ref[...] if has_add else part

        @pl.when(k > 0)
        def _():
            o_ref[...] += part

    in_specs = [a_spec, b_spec] + ([o_spec] if has_add else [])
    args = (a, b) + ((add,) if has_add else ())
    if order is not None:
        in_specs.append(pl.BlockSpec(order.shape, lambda j, i, k: (0, 0)))
        args += (order,)
    return pl.pallas_call(
        body, name=name, grid=(N // tn, M // tm, nk),
        in_specs=in_specs, out_specs=o_spec,
        out_shape=jax.ShapeDtypeStruct((M, N), out_dtype),
        compiler_params=_cp(("parallel", "parallel", "arbitrary")),
    )(*args)


def _h0_tile(i, x_ref, meta_ref):
    assert PADR == TM
    front = jnp.concatenate([jnp.zeros((NPADROWS, D), F32), meta_ref[...]], axis=0)
    return jnp.where(i == 0, front, x_ref[...])


_TOKENS = pl.BlockSpec((TM, D), lambda i: (jnp.maximum(i - 1, 0), 0))


def rms_fwd_input(x, meta, g, name):
    R = x.shape[0] + PADR

    def body(x_ref, meta_ref, g_ref, y_ref):
        xv = _h0_tile(pl.program_id(0), x_ref, meta_ref)
        r = lax.rsqrt(jnp.mean(xv * xv, axis=-1, keepdims=True) + EPS)
        y_ref[...] = (xv * r * g_ref[...]).astype(y_ref.dtype)

    return pl.pallas_call(
        body, name=name, grid=(R // TM,),
        in_specs=[_TOKENS, _const((NMETA, D)), _const((1, D))], out_specs=_rb(TM, D, 0),
        out_shape=jax.ShapeDtypeStruct((R, D), MXU), compiler_params=_cp(("parallel",)),
    )(x, meta, g)


def dgrad_rms_bwd(a, w, x, g, dres, name, tm, tk):
    R, K = a.shape
    tm, tk = _pick(R, tm), _pick(K, tk)
    nk = K // tk

    def body(a_ref, w_ref, x_ref, g_ref, dres_ref, dx_ref, dg_ref):
        i, k = pl.program_id(0), pl.program_id(1)
        part = _dot(a_ref[...], w_ref[...], 1, 1)

        @pl.when(k == 0)
        def _():
            dx_ref[...] = part

        @pl.when(k > 0)
        def _():
            dx_ref[...] += part

        @pl.when(k == nk - 1)
        def _():
            xv, dyv = x_ref[...], dx_ref[...]
            r = lax.rsqrt(jnp.mean(xv * xv, axis=-1, keepdims=True) + EPS)
            dyg = dyv * g_ref[...]
            dx_ref[...] = dres_ref[...] + r * dyg - xv * (r * r * r * jnp.mean(dyg * xv, axis=-1, keepdims=True))
            gpart = jnp.sum(dyv * xv * r, axis=0, keepdims=True)

            @pl.when(i == 0)
            def _():
                dg_ref[...] = gpart

            @pl.when(i > 0)
            def _():
                dg_ref[...] += gpart

    row = pl.BlockSpec((tm, D), lambda i, k: (i, 0))
    return pl.pallas_call(
        body, name=name, grid=(R // tm, nk),
        in_specs=[pl.BlockSpec((tm, tk), lambda i, k: (i, k)), pl.BlockSpec((D, tk), lambda i, k: (0, k)), row,
                  pl.BlockSpec((1, D), lambda i, k: (0, 0)), row],
        out_specs=[row, pl.BlockSpec((1, D), lambda i, k: (0, 0))],
        out_shape=[jax.ShapeDtypeStruct((R, D), F32), jax.ShapeDtypeStruct((1, D), F32)],
        compiler_params=_cp(("arbitrary", "arbitrary")),
    )(a, w, x, g, dres)


def rms_bwd_input(dy, x, meta, g, dres, name):
    R = dy.shape[0]

    def body(dy_ref, x_ref, meta_ref, g_ref, dres_ref, dx_ref, dmeta_ref, dg_ref):
        i = pl.program_id(0)
        xv, dyv = _h0_tile(i, x_ref, meta_ref), dy_ref[...]
        r = lax.rsqrt(jnp.mean(xv * xv, axis=-1, keepdims=True) + EPS)
        dyg = dyv * g_ref[...]
        dx = dres_ref[...] + r * dyg - xv * (r * r * r * jnp.mean(dyg * xv, axis=-1, keepdims=True))
        dx_ref[...] = dx
        part = jnp.sum(dyv * xv * r, axis=0, keepdims=True)

        @pl.when(i == 0)
        def _():
            dg_ref[...] = part
            dmeta_ref[...] = dx[NPADROWS:PADR]

        @pl.when(i > 0)
        def _():
            dg_ref[...] += part

    return pl.pallas_call(
        body, name=name, grid=(R // TM,),
        in_specs=[_rb(TM, D, 0), _TOKENS, _const((NMETA, D)), _const((1, D)), _rb(TM, D, 0)],
        out_specs=[_TOKENS, _const((NMETA, D)), _const((1, D))],
        out_shape=[jax.ShapeDtypeStruct((R - PADR, D), F32), jax.ShapeDtypeStruct((NMETA, D), F32),
                   jax.ShapeDtypeStruct((1, D), F32)],
        compiler_params=_cp(("arbitrary",)),
    )(dy, x, meta, g, dres)


def _shift_down(cur, prev8, s):
    tm = cur.shape[0]
    rolled = pltpu.roll(cur, s, 0)
    rows8 = lax.broadcasted_iota(jnp.int32, (8, cur.shape[1]), 0)
    head = jnp.where(rows8 < s, pltpu.roll(prev8, s, 0), rolled[0:8])
    return jnp.concatenate([head, rolled[8:tm]], axis=0)


def _shift_up(cur, next8, s):
    tm = cur.shape[0]
    rolled = pltpu.roll(cur, tm - s, 0)
    rows8 = lax.broadcasted_iota(jnp.int32, (8, cur.shape[1]), 0)
    tail = jnp.where(rows8 >= 8 - s, pltpu.roll(next8, 8 - s, 0), rolled[tm - 8:tm])
    return jnp.concatenate([rolled[0:tm - 8], tail], axis=0)


def prep_fwd(proj, conv_w, conv_b, gb_row):
    R = proj.shape[0]
    t8 = TM // 8

    def body(x_ref, halo_ref, sm_ref, w_ref, b_ref, gb_ref, c_ref, qk_ref, gl_ref):
        i = pl.program_id(0)
        x = x_ref[...]
        halo = halo_ref[...]
        w = w_ref[...]
        c = x * w[3:4, :] + b_ref[...]
        for s in (1, 2, 3):
            c = c + _shift_down(x, halo, s) * w[3 - s:4 - s, :]
        c_ref[...] = c
        qk_ref[...] = c * _sigmoid(c)
        z = sm_ref[...] + gb_ref[...]
        lane = lax.broadcasted_iota(jnp.int32, z.shape, 1)
        row = lax.broadcasted_iota(jnp.int32, z.shape, 0) + i * TM
        valid = row >= NPADROWS
        logi = jnp.where(valid, z, NEG)
        logf = jnp.where(valid, _log_sigmoid(z), 0.0)
        gl_ref[...] = jnp.where(lane < 4, logi, jnp.where(lane < 8, logf, 0.0))

    return pl.pallas_call(
        body, name="prep_fwd", grid=(R // TM,),
        in_specs=[_rb(TM, 1024, QK0 // 1024),
                  pl.BlockSpec((8, 1024), lambda i: (jnp.maximum(i * t8 - 1, 0), QK0 // 1024)),
                  _rb(TM, 128, SM0 // 128), _const((4, 1024)), _const((1, 1024)), _const((1, 128))],
        out_specs=[_rb(TM, 1024, 0), _rb(TM, 1024, 0), _rb(TM, 128, 0)],
        out_shape=[jax.ShapeDtypeStruct((R, 1024), F32), jax.ShapeDtypeStruct((R, 1024), F32),
                   jax.ShapeDtypeStruct((R, 128), F32)],
        compiler_params=_cp(("parallel",)),
    )(proj, proj, proj, conv_w, conv_b, gb_row)


def merge_fwd(ym, yg, w_bm, w_bg, proj):
    R = ym.shape[0]

    def body(ym_ref, yg_ref, wm_ref, wg_ref, gm_ref, gg_ref, bm_ref, bg_ref, o_ref):
        bm = _dot(ym_ref[...], wm_ref[...], 1, 0)
        bg = _dot(yg_ref[...], wg_ref[...], 1, 0)
        bm_ref[...] = bm
        bg_ref[...] = bg
        o_ref[...] = (_sigmoid(gm_ref[...]) * bm + _sigmoid(gg_ref[...]) * bg).astype(o_ref.dtype)

    return pl.pallas_call(
        body, name="merge_fwd", grid=(R // TM,),
        in_specs=[_rb(TM, D, 0), _rb(TM, D, 0), _const((D, D)), _const((D, D)), _rb(TM, D, GM0 // D),
                  _rb(TM, D, GG0 // D)],
        out_specs=[_rb(TM, D, 0), _rb(TM, D, 0), _rb(TM, D, 0)],
        out_shape=[jax.ShapeDtypeStruct((R, D), F32), jax.ShapeDtypeStruct((R, D), F32),
                   jax.ShapeDtypeStruct((R, D), MXU)],
        compiler_params=_cp(("parallel",)),
    )(ym, yg, w_bm, w_bg, proj, proj)


def merge_bwd(dh1, w_out, bm, bg, proj):
    R = bm.shape[0]

    def body(dh_ref, w_ref, bm_ref, bg_ref, gm_ref, gg_ref, dbm_ref, dbg_ref, dp_ref):
        dm = _dot(dh_ref[...], w_ref[...], 1, 1)
        sm, sg = _sigmoid(gm_ref[...]), _sigmoid(gg_ref[...])
        dbm_ref[...] = (dm * sm).astype(dbm_ref.dtype)
        dbg_ref[...] = (dm * sg).astype(dbg_ref.dtype)
        dp_ref[:, 0:D] = (dm * bm_ref[...] * sm * (1.0 - sm)).astype(dp_ref.dtype)
        dp_ref[:, D:2 * D] = (dm * bg_ref[...] * sg * (1.0 - sg)).astype(dp_ref.dtype)

    return pl.pallas_call(
        body, name="merge_bwd", grid=(R // TM,),
        in_specs=[_rb(TM, D, 0), _const((D, D)), _rb(TM, D, 0), _rb(TM, D, 0), _rb(TM, D, GM0 // D),
                  _rb(TM, D, GG0 // D)],
        out_specs=[_rb(TM, D, 0), _rb(TM, D, 0), _rb(TM, 2 * D, GM0 // (2 * D))],
        out_shape=[jax.ShapeDtypeStruct((R, D), MXU), jax.ShapeDtypeStruct((R, D), MXU),
                   jax.ShapeDtypeStruct((R, NP), MXU)],
        compiler_params=_cp(("parallel",)),
    )(dh1, w_out, bm, bg, proj, proj)


TF = DFF // 2
TMF = 768


def interleave_gu(gate, up):
    return jnp.concatenate([gate[:, :TF], up[:, :TF], gate[:, TF:], up[:, TF:]], axis=1)


def split_gu(gu):
    return (jnp.concatenate([gu[:, 0:TF], gu[:, 2 * TF:3 * TF]], axis=1),
            jnp.concatenate([gu[:, TF:2 * TF], gu[:, 3 * TF:]], axis=1))


def ff_in_fwd(hn, w_gu):
    R = hn.shape[0]

    def body(x_ref, w_ref, au_ref, ff_ref):
        au = _dot(x_ref[...], w_ref[...], 1, 0)
        au_ref[...] = au.astype(au_ref.dtype)
        a = au[:, :TF]
        ff_ref[...] = (a * _sigmoid(a) * au[:, TF:]).astype(ff_ref.dtype)

    tm = _pick(R, TMF)
    return pl.pallas_call(
        body, name="ff_in_fwd", grid=(DFF // TF, R // tm),
        in_specs=[pl.BlockSpec((tm, D), lambda j, i: (i, 0)), pl.BlockSpec((D, 2 * TF), lambda j, i: (0, j))],
        out_specs=[pl.BlockSpec((tm, 2 * TF), lambda j, i: (i, j)), pl.BlockSpec((tm, TF), lambda j, i: (i, j))],
        out_shape=[jax.ShapeDtypeStruct((R, 2 * DFF), MXU), jax.ShapeDtypeStruct((R, DFF), MXU)],
        compiler_params=_cp(("parallel", "parallel")),
    )(hn, w_gu)


def ff_down_dgrad(dh2, w_down, au):
    R = dh2.shape[0]

    def body(d_ref, w_ref, au_ref, o_ref):
        dff = _dot(d_ref[...], w_ref[...], 1, 1)
        a = au_ref[:, :TF].astype(F32)
        u = au_ref[:, TF:].astype(F32)
        s = _sigmoid(a)
        o_ref[:, :TF] = (dff * u * s * (1.0 + a * (1.0 - s))).astype(o_ref.dtype)
        o_ref[:, TF:] = (dff * a * s).astype(o_ref.dtype)

    tm = _pick(R, TMF)
    return pl.pallas_call(
        body, name="ff_down_dgrad", grid=(DFF // TF, R // tm),
        in_specs=[pl.BlockSpec((tm, D), lambda j, i: (i, 0)), pl.BlockSpec((TF, D), lambda j, i: (j, 0)),
                  pl.BlockSpec((tm, 2 * TF), lambda j, i: (i, j))],
        out_specs=pl.BlockSpec((tm, 2 * TF), lambda j, i: (i, j)),
        out_shape=jax.ShapeDtypeStruct((R, 2 * DFF), MXU),
        compiler_params=_cp(("parallel", "parallel")),
    )(dh2, w_down, au)


def out_proj_norm(merged, w, x, meta, g):
    R = merged.shape[0]

    def body(m_ref, w_ref, x_ref, meta_ref, g_ref, h_ref, n_ref):
        hv = _dot(m_ref[...], w_ref[...], 1, 0) + _h0_tile(pl.program_id(0), x_ref, meta_ref)
        h_ref[...] = hv
        r = lax.rsqrt(jnp.mean(hv * hv, axis=-1, keepdims=True) + EPS)
        n_ref[...] = (hv * r * g_ref[...]).astype(n_ref.dtype)

    return pl.pallas_call(
        body, name="out_fwd", grid=(R // TM,),
        in_specs=[_rb(TM, D, 0), _const((D, D)), _TOKENS, _const((NMETA, D)), _const((1, D))],
        out_specs=[_rb(TM, D, 0), _rb(TM, D, 0)],
        out_shape=[jax.ShapeDtypeStruct((R, D), F32), jax.ShapeDtypeStruct((R, D), MXU)],
        compiler_params=_cp(("parallel",)),
    )(merged, w, x, meta, g)


def ff_down_loss(ff, w_down, h1, gf, target):
    R = ff.shape[0]
    assert PADR == TM

    def body(f_ref, w_ref, h1_ref, g_ref, t_ref, dh_ref, loss_ref, dg_ref):
        i = pl.program_id(0)
        hv = _dot(f_ref[...], w_ref[...], 1, 0) + h1_ref[...]
        r = lax.rsqrt(jnp.mean(hv * hv, axis=-1, keepdims=True) + EPS)
        g = g_ref[...]
        live = (i >= 1).astype(F32)
        e = (hv * r * g - t_ref[...]) * live
        dy = e * (1.0 / D)
        dyg = dy * g
        dh_ref[...] = r * dyg - hv * (r * r * r * jnp.mean(dyg * hv, axis=-1, keepdims=True))
        lpart = jnp.zeros((1, 128), F32) + 0.5 * jnp.sum(jnp.sum(e * e, axis=1, keepdims=True), axis=0, keepdims=True) * (1.0 / D)
        gpart = jnp.sum(dy * hv * r, axis=0, keepdims=True)

        @pl.when(i == 0)
        def _():
            loss_ref[...] = lpart
            dg_ref[...] = gpart

        @pl.when(i > 0)
        def _():
            loss_ref[...] += lpart
            dg_ref[...] += gpart

    return pl.pallas_call(
        body, name="ff_down_loss", grid=(R // TM,),
        in_specs=[_rb(TM, DFF, 0), _const((DFF, D)), _rb(TM, D, 0), _const((1, D)),
                  pl.BlockSpec((TM, D), lambda i: (jnp.maximum(i - 1, 0), 0))],
        out_specs=[_rb(TM, D, 0), _const((1, 128)), _const((1, D))],
        out_shape=[jax.ShapeDtypeStruct((R, D), F32), jax.ShapeDtypeStruct((1, 128), F32),
                   jax.ShapeDtypeStruct((1, D), F32)],
        compiler_params=_cp(("arbitrary",)),
    )(ff, w_down, h1, gf, target)


def conv_bwd(dc, proj, conv_w, dproj):
    R = dc.shape[0]
    t8 = TM // 8
    nt = R // TM

    def body(dc_ref, nxt_ref, x_ref, prv_ref, w_ref, dp_in, dp_ref, dw_ref):
        del dp_in
        i = pl.program_id(0)
        dcv = dc_ref[...]
        nxt = nxt_ref[...] * (i < nt - 1).astype(F32)
        x = x_ref[...]
        prv = prv_ref[...]
        w = w_ref[...]
        dx = dcv * w[3:4, :]
        rows = [None] * 4
        rows[3] = jnp.sum(dcv * x, axis=0, keepdims=True)
        for s in (1, 2, 3):
            dx = dx + _shift_up(dcv, nxt, s) * w[3 - s:4 - s, :]
            rows[3 - s] = jnp.sum(dcv * _shift_down(x, prv, s), axis=0, keepdims=True)
        dp_ref[...] = dx.astype(dp_ref.dtype)
        part = jnp.concatenate(rows + [jnp.sum(dcv, axis=0, keepdims=True), jnp.zeros((3, 1024), F32)], axis=0)

        @pl.when(i == 0)
        def _():
            dw_ref[...] = part

        @pl.when(i > 0)
        def _():
            dw_ref[...] += part

    return pl.pallas_call(
        body, name="conv_bwd", grid=(nt,),
        in_specs=[_rb(TM, 1024, 0),
                  pl.BlockSpec((8, 1024), lambda i: (jnp.minimum((i + 1) * t8, nt * t8 - 1), 0)),
                  _rb(TM, 1024, QK0 // 1024),
                  pl.BlockSpec((8, 1024), lambda i: (jnp.maximum(i * t8 - 1, 0), QK0 // 1024)),
                  _const((4, 1024)), pl.BlockSpec(memory_space=pl.ANY)],
        out_specs=[_rb(TM, 1024, QK0 // 1024), _const((8, 1024))],
        out_shape=[jax.ShapeDtypeStruct((R, NP), MXU), jax.ShapeDtypeStruct((8, 1024), F32)],
        input_output_aliases={5: 0},
        compiler_params=_cp(("arbitrary",)),
    )(dc, dc, proj, proj, conv_w, dproj)


def small_bwd(dgl, dga, proj, gb_row, dproj):
    R = dgl.shape[0]

    def body(dgl_ref, dga_ref, sm_ref, gb_ref, dp_in, dp_ref, dgb_ref):
        del dp_in
        i = pl.program_id(0)
        z = sm_ref[...] + gb_ref[...]
        lane = lax.broadcasted_iota(jnp.int32, z.shape, 1)
        row = lax.broadcasted_iota(jnp.int32, z.shape, 0) + i * TM
        valid = row >= NPADROWS
        dgl_v = dgl_ref[...]
        dgate = jnp.where(valid, jnp.where(lane < 4, dgl_v, dgl_v * _sigmoid(-z)), 0.0)
        ds = jnp.where(lane < 8, dgate, dga_ref[...])
        dp_ref[...] = ds.astype(dp_ref.dtype)
        part = jnp.sum(jnp.where(lane < 8, dgate, 0.0), axis=0, keepdims=True)

        @pl.when(i == 0)
        def _():
            dgb_ref[...] = part

        @pl.when(i > 0)
        def _():
            dgb_ref[...] += part

    return pl.pallas_call(
        body, name="small_bwd", grid=(R // TM,),
        in_specs=[_rb(TM, 128, 0), _rb(TM, 128, 0), _rb(TM, 128, SM0 // 128), _const((1, 128)),
                  pl.BlockSpec(memory_space=pl.ANY)],
        out_specs=[_rb(TM, 128, SM0 // 128), _const((1, 128))],
        out_shape=[jax.ShapeDtypeStruct((R, NP), MXU), jax.ShapeDtypeStruct((1, 128), F32)],
        input_output_aliases={4: 0},
        compiler_params=_cp(("arbitrary",)),
    )(dgl, dga, proj, gb_row, dproj)


def _masks(n=L):
    r = lax.broadcasted_iota(jnp.int32, (n, n), 0)
    c = lax.broadcasted_iota(jnp.int32, (n, n), 1)
    return r >= c, r == c, r


def _to_row(col, eye):
    return jnp.sum(jnp.where(eye, col, 0.0), axis=0, keepdims=True)


def _to_col(row, eye):
    return jnp.sum(jnp.where(eye, row, 0.0), axis=1, keepdims=True)


def _mlstm_chunk(q, k, logi_c, logf_c, m, n):
    tril, eye, _ = _masks(q.shape[0])
    logi_r, logf_r = _to_row(logi_c, eye), _to_row(logf_c, eye)
    b_c = jnp.sum(jnp.where(tril, logf_r, 0.0), axis=1, keepdims=True)
    b_r = _to_row(b_c, eye)
    g = jnp.sum(logf_c, axis=0, keepdims=True)
    dmat = jnp.where(tril, b_c - b_r + logi_r, NEG)
    mrow = jnp.maximum(b_c + m, jnp.max(dmat, axis=1, keepdims=True))
    dm = jnp.exp(dmat - mrow)
    s = _dot(q, k, 1, 1)
    w = dm * s
    a_in = jnp.exp(b_c + m - mrow)
    qn = jnp.sum(q * n, axis=1, keepdims=True)
    den = a_in * qn + jnp.sum(w, axis=1, keepdims=True)
    floor = jnp.exp(-mrow)
    nrm = jnp.maximum(jnp.abs(den), floor)
    wlog_c = g - b_c + logi_c
    m_new = jnp.maximum(g + m, jnp.max(wlog_c, axis=0, keepdims=True))
    a_st = jnp.exp(g + m - m_new)
    w_c = jnp.exp(wlog_c - m_new)
    return dict(b_c=b_c, g=g, dm=dm, s=s, w=w, a_in=a_in, qn=qn, den=den, floor=floor, nrm=nrm,
                m_new=m_new, a_st=a_st, w_c=w_c, tril=tril, eye=eye)


def _mlstm_head_fwd(h, glv, qk_ref, v_ref, mo_ref, hg_ref, hm_ref, ym_ref, cs_ref, nm_ref, c_s, nm_s):
    q = qk_ref[:, h * DQK:(h + 1) * DQK] * QSCALE
    k = qk_ref[:, 512 + h * DQK:512 + (h + 1) * DQK]
    v = v_ref[:, h * DV:(h + 1) * DV]
    C = c_s[h]
    n = nm_s[h, 0:1, :]
    m = nm_s[h, 1:2, 0:1]
    f = _mlstm_chunk(q, k, glv[:, h:h + 1], glv[:, 4 + h:5 + h], m, n)
    num = f["a_in"] * _dot(q, C, 1, 1) + _dot(f["w"], v, 1, 0)
    hh = num / f["nrm"]
    cs_ref[0, h] = C
    nm_ref[0, h] = nm_s[h]
    c_s[h] = f["a_st"] * C + _dot(f["w_c"] * v, k, 0, 0)
    n_new = f["a_st"] * n + jnp.sum(f["w_c"] * k, axis=0, keepdims=True)
    rowi = lax.broadcasted_iota(jnp.int32, (8, DQK), 0)
    nm_s[h] = jnp.where(rowi == 0, n_new, jnp.where(rowi == 1, f["m_new"], 0.0))
    rm = lax.rsqrt(jnp.mean(hh * hh, axis=-1, keepdims=True) + EPS)
    sl = slice(h * DV, (h + 1) * DV)
    hm_ref[:, sl] = hh
    ym_ref[:, sl] = (hh * rm * hg_ref[:, sl] * _sigmoid(mo_ref[:, sl])).astype(ym_ref.dtype)


def _mlstm_bwd_parts(dym_ref, hm_ref, qk_ref, cp_ref, v_ref, gl_ref, mo_ref, hg_ref, cs_ref, nm_ref,
                     dp_ref, dc_ref, dgl_ref, dhg_ref, dc_s, dn_s):
        def init():
            dc_s[...] = jnp.zeros_like(dc_s)
            dn_s[...] = jnp.zeros_like(dn_s)
            dhg_ref[...] = jnp.zeros_like(dhg_ref)

        def zero():
            dp_ref[...] = jnp.zeros_like(dp_ref)
            dc_ref[...] = jnp.zeros_like(dc_ref)
            dgl_ref[...] = jnp.zeros_like(dgl_ref)

        def compute(before_head):
            glv = gl_ref[...]
            lane = lax.broadcasted_iota(jnp.int32, (LM, 128), 1)
            dgl = jnp.zeros((LM, 128), F32)
            for h in range(NH):
                before_head(h)
                sl = slice(h * DV, (h + 1) * DV)
                sq = slice(h * DQK, (h + 1) * DQK)
                sk = slice(512 + h * DQK, 512 + (h + 1) * DQK)
                hh = hm_ref[:, sl]
                gain = hg_ref[:, sl]
                rm = lax.rsqrt(jnp.mean(hh * hh, axis=-1, keepdims=True) + EPS)
                sg = _sigmoid(mo_ref[:, sl])
                dyv = dym_ref[:, sl]
                dno = dyv * sg
                dp_ref[:, 1024 + h * DV:1024 + (h + 1) * DV] = (dyv * hh * rm * gain * sg * (1.0 - sg)).astype(dp_ref.dtype)
                dhg_ref[:, sl] += jnp.sum(dno * hh * rm, axis=0, keepdims=True)
                dnog = dno * gain
                dh = rm * dnog - hh * (rm * rm * rm * jnp.mean(dnog * hh, axis=-1, keepdims=True))
                q = qk_ref[:, sq] * QSCALE
                k = qk_ref[:, sk]
                v = v_ref[:, sl]
                C = cs_ref[0, h]
                n = nm_ref[0, h, 0:1, :]
                m = nm_ref[0, h, 1:2, 0:1]
                f = _mlstm_chunk(q, k, glv[:, h:h + 1], glv[:, 4 + h:5 + h], m, n)
                eye = f["eye"]
                a_in, nrm, den, w = f["a_in"], f["nrm"], f["den"], f["w"]
                dnum = dh / nrm
                dnrm = -jnp.sum(dh * hh, axis=1, keepdims=True) / nrm
                dden = jnp.where(jnp.abs(den) >= f["floor"], dnrm * jnp.sign(den), 0.0)
                dw = _dot(dnum, v, 1, 1) + dden
                dv = _dot(w, dnum, 0, 0)
                ds = dw * f["dm"]
                e = dw * w
                qc = _dot(q, C, 1, 1)
                dq = _dot(ds, k, 1, 0) + a_in * _dot(dnum, C, 1, 0) + (a_in * dden) * n
                dk = _dot(ds, q, 0, 0)
                dC_in = _dot(a_in * dnum, q, 0, 0)
                dn_in = jnp.sum((a_in * dden) * q, axis=0, keepdims=True)
                da_in = jnp.sum(dnum * qc, axis=1, keepdims=True) + dden * f["qn"]
                col_e = _to_col(jnp.sum(e, axis=0, keepdims=True), eye)
                db = jnp.sum(e, axis=1, keepdims=True) + da_in * a_in - col_e
                dlogi = col_e
                dCp = dc_s[h]
                dnp = dn_s[h, 0:1, :]
                a_st, w_c = f["a_st"], f["w_c"]
                da_st = (jnp.sum(jnp.sum(dCp * C, axis=1, keepdims=True), axis=0, keepdims=True)
                         + jnp.sum(dnp * n, axis=1, keepdims=True))
                vdc = _dot(v, dCp, 1, 0)
                dw_c = jnp.sum((vdc + dnp) * k, axis=1, keepdims=True)
                dv = dv + w_c * _dot(k, dCp, 1, 1)
                dk = dk + w_c * (vdc + dnp)
                fw = dw_c * w_c
                dg = jnp.sum(fw, axis=0, keepdims=True) + da_st * a_st
                db = db - fw
                dlogi = dlogi + fw
                rowc = lax.broadcasted_iota(jnp.int32, (LM, 1), 0)
                db = db + jnp.where(rowc == LM - 1, dg, 0.0)
                triu = lax.broadcasted_iota(jnp.int32, (LM, LM), 1) >= lax.broadcasted_iota(jnp.int32, (LM, LM), 0)
                dlogf = jnp.sum(jnp.where(triu, _to_row(db, eye), 0.0), axis=1, keepdims=True)
                dc_s[h] = a_st * dCp + dC_in
                dn_new = a_st * dnp + dn_in
                dn_s[h] = jnp.zeros((8, DQK), F32) + dn_new
                cq, ck = cp_ref[:, sq], cp_ref[:, sk]
                s_q, s_k = _sigmoid(cq), _sigmoid(ck)
                dc_ref[:, sq] = dq * QSCALE * s_q * (1.0 + cq * (1.0 - s_q))
                dc_ref[:, sk] = dk * s_k * (1.0 + ck * (1.0 - s_k))
                dp_ref[:, sl] = dv.astype(dp_ref.dtype)
                dgl = jnp.where(lane == h, dlogi, jnp.where(lane == 4 + h, dlogf, dgl))
            dgl_ref[...] = dgl

        return init, zero, compute


def _gla_logs(sm, a2p, b2, valid):
    za = _dot(sm, a2p, 1, 0) + b2
    return za, jnp.where(valid, _log_sigmoid(za) * (1.0 / TAU), 0.0)


def _valid_rows(c, width):
    row = lax.broadcasted_iota(jnp.int32, (L, width), 0) + c * L
    return row >= NPADROWS


def _gla_cumsum(loga):
    tril, _, _ = _masks()
    return _dot_exact(tril.astype(F32), loga)


def _gla_chunk(q, k, la, bc):
    tril, _, _ = _masks()
    btot = jnp.sum(la, axis=0, keepdims=True)
    ebc = jnp.exp(bc)
    qd = q * ebc
    ki = k * jnp.exp(-bc)
    ke = k * jnp.exp(btot - bc)
    att = jnp.where(tril, _dot(qd, ki, 1, 1), 0.0)
    return dict(tril=tril, bc=bc, btot=btot, ebc=ebc, qd=qd, ki=ki, ke=ke, att=att)


def _col128(row):
    r = lax.broadcasted_iota(jnp.int32, (DQK, DQK), 0)
    c = lax.broadcasted_iota(jnp.int32, (DQK, DQK), 1)
    return jnp.sum(jnp.where(r == c, row, 0.0), axis=1, keepdims=True)


def _gla_fwd_parts(c, q_ref, k_ref, v_ref, gr_ref, sm_ref, a2_ref, b2_ref, hg_ref, hgl_ref, yg_ref, ss_ref, s_s):
        def init():
            s_s[...] = jnp.zeros_like(s_s)

        def zero():
            hgl_ref[...] = jnp.zeros_like(hgl_ref)
            yg_ref[...] = jnp.zeros_like(yg_ref)
            ss_ref[...] = jnp.zeros_like(ss_ref)

        def compute(s):
            rows = pl.ds(s * L, L)
            chunk(LM // L * c + s, q_ref.at[rows], k_ref.at[rows], v_ref.at[rows], gr_ref.at[rows], sm_ref.at[rows],
                  hgl_ref.at[rows], yg_ref.at[rows], ss_ref.at[pl.ds(s, 1)])

        def chunk(c, q_ref, k_ref, v_ref, gr_ref, sm_ref, hgl_ref, yg_ref, ss_ref):
            _, loga = _gla_logs(sm_ref[...], a2_ref[...], b2_ref[...], _valid_rows(c, 512))
            bc_all = _gla_cumsum(loga)
            for h in range(NH):
                sq = slice(h * DQK, (h + 1) * DQK)
                sl = slice(h * DV, (h + 1) * DV)
                q = q_ref[:, sq] * QSCALE
                k = k_ref[:, sq]
                v = v_ref[:, sl]
                S = s_s[h]
                f = _gla_chunk(q, k, loga[:, sq], bc_all[:, sq])
                o = _dot(f["att"], v, 1, 0) + _dot(f["qd"], S, 1, 0)
                ss_ref[0, h] = S
                s_s[h] = _col128(jnp.exp(f["btot"])) * S + _dot(f["ke"], v, 0, 0)
                rg = lax.rsqrt(jnp.mean(o * o, axis=-1, keepdims=True) + EPS)
                gr = gr_ref[:, sl]
                hgl_ref[:, sl] = o
                yg_ref[:, sl] = (o * rg * hg_ref[:, sl] * gr * _sigmoid(gr)).astype(yg_ref.dtype)

        return init, zero, compute


def _gla_bwd_parts(c, dy_ref, ho_ref, q_ref, k_ref, v_ref, gr_ref, sm_ref, a2_ref, b2_ref, hg_ref, ss_ref,
                   dp_ref, dga_ref, da2_ref, db2_ref, dhg_ref, ds_s):
        def init():
            ds_s[...] = jnp.zeros_like(ds_s)
            da2_ref[...] = jnp.zeros_like(da2_ref)
            db2_ref[...] = jnp.zeros_like(db2_ref)
            dhg_ref[...] = jnp.zeros_like(dhg_ref)

        def zero():
            dp_ref[...] = jnp.zeros_like(dp_ref)
            dga_ref[...] = jnp.zeros_like(dga_ref)

        def compute(s):
            rows = pl.ds(s * L, L)
            chunk(LM // L * c + s, dy_ref.at[rows], ho_ref.at[rows], q_ref.at[rows], k_ref.at[rows], v_ref.at[rows],
                  gr_ref.at[rows], sm_ref.at[rows], ss_ref.at[pl.ds(s, 1)], dp_ref.at[rows], dga_ref.at[rows])

        def chunk(c, dy_ref, ho_ref, q_ref, k_ref, v_ref, gr_ref, sm_ref, ss_ref, dp_ref, dga_ref):
            valid = _valid_rows(c, 512)
            sm = sm_ref[...]
            za, loga = _gla_logs(sm, a2_ref[...], b2_ref[...], valid)
            dbcs = []
            for h in range(NH):
                sq = slice(h * DQK, (h + 1) * DQK)
                sl = slice(h * DV, (h + 1) * DV)
                o = ho_ref[:, sl]
                gain = hg_ref[:, sl]
                rg = lax.rsqrt(jnp.mean(o * o, axis=-1, keepdims=True) + EPS)
                gr = gr_ref[:, sl]
                sg = _sigmoid(gr)
                dyv = dy_ref[:, sl]
                dno = dyv * gr * sg
                dp_ref[:, 2048 + h * DV:2048 + (h + 1) * DV] = (
                    dyv * o * rg * gain * sg * (1.0 + gr * (1.0 - sg))).astype(dp_ref.dtype)
                dhg_ref[:, sl] += jnp.sum(dno * o * rg, axis=0, keepdims=True)
                dnog = dno * gain
                do = rg * dnog - o * (rg * rg * rg * jnp.mean(dnog * o, axis=-1, keepdims=True))
                q = q_ref[:, sq] * QSCALE
                k = k_ref[:, sq]
                v = v_ref[:, sl]
                S = ss_ref[0, h]
                f = _gla_chunk(q, k, loga[:, sq], _gla_cumsum(loga[:, sq]))
                tril, qd, ki, ke = f["tril"], f["qd"], f["ki"], f["ke"]
                dSp = ds_s[h]
                datt = jnp.where(tril, _dot(do, v, 1, 1), 0.0)
                dqd = _dot(do, S, 1, 1) + _dot(datt, ki, 1, 0)
                dki = _dot(datt, qd, 0, 0)
                dv = _dot(f["att"], do, 0, 0) + _dot(ke, dSp, 1, 0)
                dke = _dot(v, dSp, 1, 1)
                ebt = jnp.exp(f["btot"])
                dbtot = jnp.sum(dke * ke, axis=0, keepdims=True) + ebt * _to_row128(jnp.sum(dSp * S, axis=1, keepdims=True))
                ds_s[h] = _dot(qd, do, 0, 0) + _col128(ebt) * dSp
                dq = dqd * f["ebc"]
                dk = dki * jnp.exp(-f["bc"]) + dke * jnp.exp(f["btot"] - f["bc"])
                dbc = dqd * qd - dki * ki - dke * ke
                rowc = lax.broadcasted_iota(jnp.int32, (L, DQK), 0)
                dbc = dbc + jnp.where(rowc == L - 1, dbtot, 0.0)
                triu = lax.broadcasted_iota(jnp.int32, (L, L), 1) >= lax.broadcasted_iota(jnp.int32, (L, L), 0)
                dbcs.append(_dot_exact(triu.astype(F32), dbc))
                dp_ref[:, sq] = (dq * QSCALE).astype(dp_ref.dtype)
                dp_ref[:, 512 + h * DQK:512 + (h + 1) * DQK] = dk.astype(dp_ref.dtype)
                dp_ref[:, 1024 + h * DV:1024 + (h + 1) * DV] = dv.astype(dp_ref.dtype)
            dza = jnp.where(valid, jnp.concatenate(dbcs, axis=1) * (1.0 / TAU) * _sigmoid(-za), 0.0)
            dga_ref[...] = _dot(dza, a2_ref[...], 1, 1)
            da2_ref[...] += _dot(sm, dza, 0, 0)
            db2_ref[...] += jnp.sum(dza, axis=0, keepdims=True)

        return init, zero, compute


def mix_fwd(qk, proj, gl, m_head_g, a2p, b2, g_head_g):
    R = qk.shape[0]
    NC = R // LM
    G = LM // L

    def body(qk_ref, mv_ref, gl_ref, mo_ref, mhg_ref, gq_ref, gk_ref, gv_ref, gr_ref, sm_ref, a2_ref, b2_ref, ghg_ref,
             hm_ref, ym_ref, cs_ref, nm_ref, hgl_ref, yg_ref, ss_ref, c_s, nm_s, s_s):
        c = pl.program_id(0)
        g_init, g_zero, g_compute = _gla_fwd_parts(c, gq_ref, gk_ref, gv_ref, gr_ref, sm_ref, a2_ref, b2_ref, ghg_ref,
                                                   hgl_ref, yg_ref, ss_ref, s_s)

        @pl.when(c <= CH0)
        def _():
            c_s[...] = jnp.zeros_like(c_s)
            nm_s[...] = jnp.zeros_like(nm_s)
            g_init()

        @pl.when(c < CH0)
        def _():
            hm_ref[...] = jnp.zeros_like(hm_ref)
            ym_ref[...] = jnp.zeros_like(ym_ref)
            cs_ref[...] = jnp.zeros_like(cs_ref)
            nm_ref[...] = jnp.zeros_like(nm_ref)
            g_zero()

        @pl.when(c >= CH0)
        def _():
            glv = gl_ref[...]
            assert LM // L == NH
            for h in range(NH):
                g_compute(h)
                _mlstm_head_fwd(h, glv, qk_ref, mv_ref, mo_ref, mhg_ref, hm_ref, ym_ref, cs_ref, nm_ref, c_s, nm_s)

    st_m = pl.BlockSpec((1, NH, DV, DQK), lambda c: (c, 0, 0, 0))
    st_n = pl.BlockSpec((1, NH, 8, DQK), lambda c: (c, 0, 0, 0))
    st_g = pl.BlockSpec((G, NH, DQK, DV), lambda c: (c, 0, 0, 0))
    return pl.pallas_call(
        body, name="mix_fwd", grid=(NC,),
        in_specs=[_rb(LM, 1024, 0), _rb(LM, 1024, MV0 // 1024), _rb(LM, 128, 0), _rb(LM, 1024, MO0 // 1024),
                  _const((1, 1024)),
                  _rb(LM, 512, GQ0 // 512), _rb(LM, 512, GK0 // 512), _rb(LM, 1024, GV0 // 1024),
                  _rb(LM, 1024, GR0 // 1024), _rb(LM, 128, SM0 // 128), _const((128, 512)), _const((1, 512)),
                  _const((1, 1024))],
        out_specs=[_rb(LM, 1024, 0), _rb(LM, 1024, 0), st_m, st_n, _rb(LM, 1024, 0), _rb(LM, 1024, 0), st_g],
        out_shape=[jax.ShapeDtypeStruct((R, 1024), F32), jax.ShapeDtypeStruct((R, 1024), MXU),
                   jax.ShapeDtypeStruct((NC, NH, DV, DQK), F32), jax.ShapeDtypeStruct((NC, NH, 8, DQK), F32),
                   jax.ShapeDtypeStruct((R, 1024), F32), jax.ShapeDtypeStruct((R, 1024), MXU),
                   jax.ShapeDtypeStruct((G * NC, NH, DQK, DV), F32)],
        scratch_shapes=[pltpu.VMEM((NH, DV, DQK), F32), pltpu.VMEM((NH, 8, DQK), F32),
                        pltpu.VMEM((NH, DQK, DV), F32)],
        compiler_params=_cp(("arbitrary",)),
    )(qk, proj, gl, proj, m_head_g, proj, proj, proj, proj, proj, a2p, b2, g_head_g)


def mix_bwd(dym, hm, qk, cpre, proj, gl, m_head_g, cs, nm, dyg, hgl, a2p, b2, g_head_g, ss, dproj):
    R = qk.shape[0]
    NC = R // LM
    rev = lambda c: NC - 1 - c
    GW = GR0 + 1024 - GQ0

    def body(dym_ref, hm_ref, qk_ref, cp_ref, mv_ref, gl_ref, mo_ref, mhg_ref, cs_ref, nm_ref,
             dyg_ref, ho_ref, gq_ref, gk_ref, gv_ref, gr_ref, sm_ref, a2_ref, b2_ref, ghg_ref, ss_ref, dp_in,
             dp_ref, dc_ref, dgl_ref, dmhg_ref, dga_ref, da2_ref, db2_ref, dghg_ref, dc_s, dn_s, ds_s):
        del dp_in
        step = pl.program_id(0)
        c = NC - 1 - step
        m_init, m_zero, m_compute = _mlstm_bwd_parts(
            dym_ref, hm_ref, qk_ref, cp_ref, mv_ref, gl_ref, mo_ref, mhg_ref, cs_ref, nm_ref,
            dp_ref.at[:, 0:GQ0], dc_ref, dgl_ref, dmhg_ref, dc_s, dn_s)
        g_init, g_zero, g_compute = _gla_bwd_parts(
            c, dyg_ref, ho_ref, gq_ref, gk_ref, gv_ref, gr_ref, sm_ref, a2_ref, b2_ref, ghg_ref, ss_ref,
            dp_ref.at[:, GQ0:GQ0 + GW], dga_ref, da2_ref, db2_ref, dghg_ref, ds_s)

        @pl.when(step == 0)
        def _():
            m_init()
            g_init()

        @pl.when(c < CH0)
        def _():
            m_zero()
            g_zero()

        @pl.when(c >= CH0)
        def _():
            assert LM // L == NH
            m_compute(lambda h: g_compute(NH - 1 - h))

    def rows(w, cb):
        return pl.BlockSpec((LM, w), lambda c: (rev(c), cb))

    return pl.pallas_call(
        body, name="mix_bwd", grid=(NC,),
        in_specs=[rows(1024, 0), rows(1024, 0), rows(1024, 0), rows(1024, 0), rows(1024, MV0 // 1024), rows(128, 0),
                  rows(1024, MO0 // 1024), _const((1, 1024)),
                  pl.BlockSpec((1, NH, DV, DQK), lambda c: (rev(c), 0, 0, 0)),
                  pl.BlockSpec((1, NH, 8, DQK), lambda c: (rev(c), 0, 0, 0)),
                  rows(1024, 0), rows(1024, 0), rows(512, GQ0 // 512), rows(512, GK0 // 512),
                  rows(1024, GV0 // 1024), rows(1024, GR0 // 1024), rows(128, SM0 // 128),
                  _const((128, 512)), _const((1, 512)), _const((1, 1024)),
                  pl.BlockSpec((LM // L, NH, DQK, DV), lambda c: (rev(c), 0, 0, 0)),
                  pl.BlockSpec(memory_space=pl.ANY)],
        out_specs=[rows(GQ0 + GW, 0), rows(1024, 0), rows(128, 0), _const((1, 1024)),
                   rows(128, 0), _const((128, 512)), _const((1, 512)), _const((1, 1024))],
        out_shape=[jax.ShapeDtypeStruct((R, NP), MXU), jax.ShapeDtypeStruct((R, 1024), F32),
                   jax.ShapeDtypeStruct((R, 128), F32), jax.ShapeDtypeStruct((1, 1024), F32),
                   jax.ShapeDtypeStruct((R, 128), F32), jax.ShapeDtypeStruct((128, 512), F32),
                   jax.ShapeDtypeStruct((1, 512), F32), jax.ShapeDtypeStruct((1, 1024), F32)],
        scratch_shapes=[pltpu.VMEM((NH, DV, DQK), F32), pltpu.VMEM((NH, 8, DQK), F32),
                        pltpu.VMEM((NH, DQK, DV), F32)],
        input_output_aliases={21: 0},
        compiler_params=_cp(("arbitrary",)),
    )(dym, hm, qk, cpre, proj, gl, proj, m_head_g, cs, nm, dyg, hgl, proj, proj, proj, proj, proj, a2p, b2,
      g_head_g, ss, dproj)


def _to_row128(col):
    r = lax.broadcasted_iota(jnp.int32, (DQK, DQK), 0)
    c = lax.broadcasted_iota(jnp.int32, (DQK, DQK), 1)
    return jnp.sum(jnp.where(r == c, col, 0.0), axis=0, keepdims=True)


def local_step(x, target, meta, norm1_g, wp, conv_w, conv_b, m_gate_b, g_a2, g_a2_b, m_head_g, g_head_g,
               norm2_g, final_g, late_weights, send_early, send_wp, first_order=None):
    seq = x.shape[0]
    assert seq % TM == 0
    gb_row = jnp.zeros((1, 128), F32).at[0, 0:8].set(m_gate_b.reshape(8))
    a2p = jnp.zeros((128, 512), F32).at[8:8 + RANK].set(g_a2)
    mhg = m_head_g.reshape(1, 1024)
    ghg = g_head_g.reshape(1, 1024)

    xn = rms_fwd_input(x, meta, norm1_g, "rms1_fwd")
    proj = matmul(xn, wp, "nn", "proj_fwd", tm=1536, order=first_order)
    cpre, qk, gl = prep_fwd(proj, conv_w, conv_b, gb_row)
    hm, ym, cs, nm, hgl, yg, ss = mix_fwd(qk, proj, gl, mhg, a2p, g_a2_b, ghg)
    w_bm, w_bg, w_out, w_gu, w_down = late_weights(ym)
    bm, bg, merged = merge_fwd(ym, yg, w_bm, w_bg, proj)
    h1, hn = out_proj_norm(merged, w_out, x, meta, norm2_g)
    au, ff = ff_in_fwd(hn, w_gu)
    dh2, loss, d_final_g = ff_down_loss(ff, w_down, h1, final_g.reshape(1, D), target)

    d_w_down = matmul(ff, dh2, "tn", "ff_down_wgrad", tm=1408, tk=1536)
    dau = ff_down_dgrad(dh2, w_down, au)
    d_w_gu = matmul(hn, dau, "tn", "ff_in_wgrad", tm=1024, tn=1408, tk=1536)
    dh1, d_norm2_g = dgrad_rms_bwd(dau, w_gu, h1, norm2_g, dh2, "ff_in_dgrad", 768, 1408)

    d_w_out = matmul(merged, dh1, "tn", "out_wgrad", tm=1024, tk=1536)
    dbm, dbg, dproj = merge_bwd(dh1, w_out, bm, bg, proj)
    d_w_bm = matmul(ym, dbm, "tn", "branch_m_wgrad", tm=1024, tk=1536)
    d_w_bg = matmul(yg, dbg, "tn", "branch_g_wgrad", tm=1024, tk=1536)
    token = send_early(dict(w_branch_m=d_w_bm, w_branch_g=d_w_bg, w_out=d_w_out, w_gu=d_w_gu, w_ff_down=d_w_down))
    dym = matmul(dbm, w_bm, "nt", "branch_m_dgrad", order=token)
    dyg = matmul(dbg, w_bg, "nt", "branch_g_dgrad")
    dproj, dc, dgl, d_mhg, dga, d_a2p, d_a2b, d_ghg = mix_bwd(
        dym, hm, qk, cpre, proj, gl, mhg, cs, nm, dyg, hgl, a2p, g_a2_b, ghg, ss, dproj)
    dproj, d_conv = conv_bwd(dc, proj, conv_w, dproj)
    dproj, d_gb = small_bwd(dgl, dga, proj, gb_row, dproj)
    d_wp = matmul(xn, dproj, "tn", "proj_wgrad", tm=1024, tn=1664, tk=1536)
    token = send_wp(d_wp)
    dxn = matmul(dproj, wp, "nt", "proj_dgrad", tm=1536, tk=1664, order=token)
    grad_x, d_meta, d_norm1_g = rms_bwd_input(dxn, x, meta, norm1_g, dh1, "rms1_bwd")

    grads = dict(
        meta_tokens=d_meta, norm1_g=d_norm1_g, conv_w=d_conv[0:4], conv_b=d_conv[4:5], m_gate_b=d_gb[0, 0:8].reshape(1, 2, 4),
        g_a2=d_a2p[8:8 + RANK], g_a2_b=d_a2b, m_head_g=d_mhg.reshape(NH, DV), g_head_g=d_ghg.reshape(NH, DV),
        norm2_g=d_norm2_g, final_g=d_final_g)
    return loss, grad_x, grads


_SEGS = [(0, 1024, QK0), (1024, 2048, MV0), (2048, 2056, SM0), (2056, 3080, MO0), (3080, 5128, GQ0),
         (5128, 5144, SM0 + 8), (5144, 6168, GR0), (6168, 8216, GM0)]
SHARD_W = NPROJ // NDEV


def regroup_cols(w8):
    parts = []
    for lo, hi, _ in sorted(_SEGS, key=lambda s: s[2]):
        while lo < hi:
            j = lo // SHARD_W
            end = min(hi, (j + 1) * SHARD_W)
            parts.append(w8[j, :, lo - j * SHARD_W:end - j * SHARD_W])
            lo = end
    parts.append(jnp.zeros((w8.shape[1], NP - NPROJ), w8.dtype))
    return jnp.concatenate(parts, axis=1)


def ungroup_cols(g):
    blocks = []
    for j in range(NDEV):
        lo, hi = j * SHARD_W, (j + 1) * SHARD_W
        parts = []
        for s_lo, s_hi, s_at in _SEGS:
            a, b = max(lo, s_lo), min(hi, s_hi)
            if a < b:
                parts.append(g[:, s_at + a - s_lo:s_at + b - s_lo])
        blocks.append(jnp.concatenate(parts, axis=1))
    return jnp.stack(blocks)


def col_blocks(g):
    r, c8 = g.shape
    return jnp.transpose(g.reshape(r, NDEV, c8 // NDEV), (1, 0, 2))


def from_col_blocks(g8):
    n, r, c = g8.shape
    return jnp.transpose(g8, (1, 0, 2)).reshape(r, n * c)


_MESHID = pl.DeviceIdType.MESH
_RELS = [(0, 0, 1), (1, 0, 0), (0, 1, 0), (1, 1, 0), (1, 0, 1), (0, 1, 1), (1, 1, 1)]


def _flip(v, bit):
    return 1 - v if bit else v


def all_gather(arrs, name):
    n = len(arrs)

    def body(*refs):
        ins, outs = refs[:n], refs[n:2 * n]
        send_sems, recv_sems, local_sems = refs[2 * n:]
        x, y, c = lax.axis_index("x"), lax.axis_index("y"), lax.axis_index("c")
        me, sibling = (x, y, c), (x, y, 1 - c)
        chips = [(1 - x, y), (x, 1 - y), (1 - x, 1 - y)]

        def slot(p):
            return 4 * p[0] + 2 * p[1] + p[2]

        def copy(a, k, block, to, src=None):
            dst = outs[a].at[slot(block)]
            return pltpu.make_async_remote_copy(
                src_ref=dst if src is None else src, dst_ref=dst,
                send_sem=send_sems.at[a, k], recv_sem=recv_sems.at[a, k],
                device_id=to, device_id_type=_MESHID)

        mine = [pltpu.make_async_copy(ins[a], outs[a].at[slot(me)], local_sems.at[a]) for a in range(n)]
        for cp in mine:
            cp.start()
        first = []
        for a in range(n):
            first.append(copy(a, 0, me, sibling, src=ins[a]))
            first += [copy(a, 1 + j, me, (*chip, c), src=ins[a]) for j, chip in enumerate(chips)]
        for cp in first:
            cp.start()
        passed = []
        for j, chip in enumerate(chips):
            for a in range(n):
                copy(a, 1 + j, (*chip, c), me).wait_recv()
                fwd = copy(a, 4 + j, (*chip, c), sibling)
                fwd.start()
                passed.append(fwd)
        for a in range(n):
            copy(a, 0, sibling, me).wait_recv()
            for j, chip in enumerate(chips):
                copy(a, 4 + j, (*chip, 1 - c), me).wait_recv()
        for cp in first + passed:
            cp.wait_send()
        for cp in mine:
            cp.wait()

    anyspec = pl.BlockSpec(memory_space=pl.ANY)
    return pl.pallas_call(
        body, name=name,
        in_specs=[anyspec] * n, out_specs=[anyspec] * n,
        out_shape=[jax.ShapeDtypeStruct((NDEV,) + a.shape, a.dtype) for a in arrs],
        scratch_shapes=[pltpu.SemaphoreType.DMA((n, 7)), pltpu.SemaphoreType.DMA((n, 7)),
                        pltpu.SemaphoreType.DMA((n,))],
    )(*arrs)


def exchange(blocks, rep, name):
    n = len(blocks)

    def body(*refs):
        b_refs, r_ref = refs[:n], refs[n]
        ob_refs, or_ref = refs[n + 1:2 * n + 1], refs[2 * n + 1]
        send_sems, recv_sems, local_sems = refs[2 * n + 2:]
        x, y, c = lax.axis_index("x"), lax.axis_index("y"), lax.axis_index("c")
        me = 4 * x + 2 * y + c

        def pairs(src_slot, dst_slot):
            return [(b_refs[a].at[src_slot], ob_refs[a].at[dst_slot]) for a in range(n)] + [(r_ref, or_ref.at[dst_slot])]

        loc = [pltpu.make_async_copy(s, d, local_sems.at[a]) for a, (s, d) in enumerate(pairs(me, me))]
        for cp in loc:
            cp.start()
        sends = []
        for k, (fx, fy, fc) in enumerate(_RELS):
            peer = (_flip(x, fx), _flip(y, fy), _flip(c, fc))
            pid = 4 * peer[0] + 2 * peer[1] + peer[2]
            for a, (s, d) in enumerate(pairs(pid, me)):
                sends.append(pltpu.make_async_remote_copy(
                    src_ref=s, dst_ref=d, send_sem=send_sems.at[a, k], recv_sem=recv_sems.at[a, k],
                    device_id=peer, device_id_type=_MESHID))
        for cp in sends:
            cp.start()
        for k, (fx, fy, fc) in enumerate(_RELS):
            peer = (_flip(x, fx), _flip(y, fy), _flip(c, fc))
            pid = 4 * peer[0] + 2 * peer[1] + peer[2]
            for a, (s, d) in enumerate(pairs(pid, pid)):
                pltpu.make_async_remote_copy(
                    src_ref=s, dst_ref=d, send_sem=send_sems.at[a, k], recv_sem=recv_sems.at[a, k],
                    device_id=peer, device_id_type=_MESHID).wait_recv()
        for cp in sends:
            cp.wait_send()
        for cp in loc:
            cp.wait()

    anyspec = pl.BlockSpec(memory_space=pl.ANY)
    return pl.pallas_call(
        body, name=name,
        in_specs=[anyspec] * (n + 1), out_specs=[anyspec] * (n + 1),
        out_shape=[jax.ShapeDtypeStruct(b.shape, b.dtype) for b in blocks]
        + [jax.ShapeDtypeStruct((NDEV,) + rep.shape, rep.dtype)],
        scratch_shapes=[pltpu.SemaphoreType.DMA((n + 1, 7)), pltpu.SemaphoreType.DMA((n + 1, 7)),
                        pltpu.SemaphoreType.DMA((n + 1,))],
    )(*blocks, rep)


_HBM = pl.BlockSpec(memory_space=pltpu.HBM)
_SEM = pl.BlockSpec(memory_space=pltpu.SEMAPHORE)
_EFFECT = pltpu.SideEffectType.DATAFLOW_SIDE_EFFECTING


def _peer_ids():
    x, y, c = lax.axis_index("x"), lax.axis_index("y"), lax.axis_index("c")
    peers = []
    for fx, fy, fc in _RELS:
        p = (_flip(x, fx), _flip(y, fy), _flip(c, fc))
        peers.append((p, 4 * p[0] + 2 * p[1] + p[2]))
    return 4 * x + 2 * y + c, peers


def _split_copy(src, land, a, k, peer, src_slot, dst_slot, send_sems, recv_sems):
    return pltpu.make_async_remote_copy(
        src_ref=src if src_slot is None else src.at[src_slot], dst_ref=land.at[dst_slot],
        send_sem=send_sems.at[7 * a + k], recv_sem=recv_sems.at[7 * a + k], device_id=peer, device_id_type=_MESHID)


def _own_copy(src, land, a, n, me, per_peer, send_sems):
    return pltpu.make_async_copy(src.at[me] if per_peer else src, land.at[me], send_sems.at[7 * n + a])


def send_start(srcs, per_peer, order, name):
    n = len(srcs)
    lands = [lax.empty((NDEV,) + (s.shape[1:] if per_peer else s.shape), s.dtype) for s in srcs]

    def body(*refs):
        src_refs, land_refs = refs[1:1 + n], refs[1 + n:1 + 2 * n]
        send_sems, recv_sems = refs[1 + 2 * n], refs[2 + 2 * n]
        token = refs[3 + 4 * n]
        me, peers = _peer_ids()
        for a in range(n):
            _own_copy(src_refs[a], land_refs[a], a, n, me, per_peer, send_sems).start()
        for a in range(n):
            for k, (peer, pid) in enumerate(peers):
                _split_copy(src_refs[a], land_refs[a], a, k, peer, pid if per_peer else None, me,
                            send_sems, recv_sems).start()
        token[...] = jnp.zeros_like(token)

    outs = pl.pallas_call(
        body, name=name,
        in_specs=[pl.BlockSpec(memory_space=pl.ANY)] + [_HBM] * (2 * n),
        out_shape=(pltpu.SemaphoreType.DMA((8 * n,)), pltpu.SemaphoreType.DMA((7 * n,)),
                   *[pltpu.HBM(s.shape, s.dtype) for s in srcs], *[pltpu.HBM(l.shape, l.dtype) for l in lands],
                   jax.ShapeDtypeStruct((8, 128), F32)),
        out_specs=(_SEM, _SEM, *[_HBM] * (2 * n), pl.BlockSpec(memory_space=pltpu.VMEM)),
        input_output_aliases={1 + i: 2 + i for i in range(2 * n)},
        compiler_params=pltpu.CompilerParams(has_side_effects=_EFFECT),
    )(order, *[pltpu.with_memory_space_constraint(s, pltpu.HBM) for s in srcs],
      *[pltpu.with_memory_space_constraint(l, pltpu.HBM) for l in lands])
    return (n, per_peer, outs[0], outs[1], outs[2:2 + n], outs[2 + n:2 + 2 * n]), outs[2 + 2 * n]


def send_wait(handle, after, name):
    n, per_peer, send_sems, recv_sems, src_thru, land_thru = handle

    def body(*refs):
        src_refs, land_refs = refs[:n], refs[n:2 * n]
        s_sems, r_sems = refs[2 * n], refs[2 * n + 1]
        me, peers = _peer_ids()
        for a in range(n):
            _own_copy(src_refs[a], land_refs[a], a, n, me, per_peer, s_sems).wait()
            for k, (peer, pid) in enumerate(peers):
                cp = _split_copy(src_refs[a], land_refs[a], a, k, peer, pid if per_peer else None, pid, s_sems, r_sems)
                cp.wait_send()
                cp.wait_recv()

    outs = pl.pallas_call(
        body, name=name,
        in_specs=[_HBM] * (2 * n) + [_SEM, _SEM, pl.BlockSpec(memory_space=pl.ANY)],
        out_shape=tuple(pltpu.HBM(t.shape, t.dtype) for t in (*src_thru, *land_thru)),
        out_specs=tuple([_HBM] * (2 * n)),
        input_output_aliases={i: i for i in range(2 * n)},
        compiler_params=pltpu.CompilerParams(has_side_effects=_EFFECT),
    )(*src_thru, *land_thru, send_sems, recv_sems, after)
    return list(outs[n:2 * n])


def adamw(parts, w, m, v, name, tr):
    npart, r, c = parts.shape
    c1 = 1.0 - ADAM_B1 ** ADAM_STEP
    c2 = 1.0 - ADAM_B2 ** ADAM_STEP

    def body(p_ref, w_ref, m_ref, v_ref, g_ref, d_ref, nm_ref, nv_ref):
        g = p_ref[0].astype(F32)
        for j in range(1, npart):
            g = g + p_ref[j].astype(F32)
        mn = ADAM_B1 * m_ref[...] + (1.0 - ADAM_B1) * g
        vn = ADAM_B2 * v_ref[...] + (1.0 - ADAM_B2) * (g * g)
        g_ref[...] = g
        nm_ref[...] = mn
        nv_ref[...] = vn
        d_ref[...] = -ADAM_LR * ((mn / c1) / (jnp.sqrt(vn / c2) + ADAM_EPS) + ADAM_WD * w_ref[...])

    spec = _rb(tr, c, 0)
    return pl.pallas_call(
        body, name=name, grid=(r // tr,),
        in_specs=[pl.BlockSpec((npart, tr, c), lambda i: (0, i, 0)), spec, spec, spec],
        out_specs=[spec] * 4, out_shape=[jax.ShapeDtypeStruct((r, c), F32)] * 4,
        compiler_params=_cp(("parallel",)),
    )(parts, w, m, v)


def sum_parts(parts, name, tc):
    npart, r, c = parts.shape

    def body(p_ref, o_ref):
        g = p_ref[0].astype(F32)
        for j in range(1, npart):
            g = g + p_ref[j].astype(F32)
        o_ref[...] = g

    return pl.pallas_call(
        body, name=name, grid=(c // tc,),
        in_specs=[pl.BlockSpec((npart, r, tc), lambda i: (0, 0, i))],
        out_specs=pl.BlockSpec((r, tc), lambda i: (0, i)),
        out_shape=jax.ShapeDtypeStruct((r, c), F32),
        compiler_params=_cp(("parallel",)),
    )(parts)


TINY = [("meta_tokens", (16, 1024)), ("conv_w", (4, 1024)), ("g_a2", (16, 512)), ("m_head_g", (4, 256)),
        ("g_head_g", (4, 256))]
REPL = [("norm1_g", (1, 1024)), ("conv_b", (1, 1024)), ("m_gate_b", (1, 2, 4)), ("g_a2_b", (1, 512)),
        ("norm2_g", (1, 1024)), ("final_g", (1024,))]
TINY_SIZE = 16 * 1024 + 4 * 1024 + 16 * 512 + 2 * 4 * 256
REPL_SIZE = 1024 + 1024 + 8 + 512 + 1024 + 1024
ROWS_GATHER = 8
ROWS_REP = 40
ROWS_OWN = 16


def pack_rows(vecs, rows):
    flat = jnp.concatenate([v.reshape(-1) for v in vecs])
    return jnp.pad(flat, (0, rows * 1024 - flat.shape[0])).reshape(rows, 1024)


def unpack_rows(packed, shapes):
    flat = packed.reshape(-1)
    out, off = [], 0
    for s in shapes:
        n = 1
        for d in s:
            n *= d
        out.append(flat[off:off + n].reshape(s))
        off += n
    return out


def kernel(x, meta_tokens, norm1_g, w_in, conv_w, conv_b, m_gate_b, g_a2, g_a2_b, m_head_g, g_head_g, w_branch_m, w_branch_g, w_out, norm2_g, w_ff_gate, w_ff_up, w_ff_down, final_g, loss_target, m_meta_tokens, m_norm1_g, m_w_in, m_conv_w, m_conv_b, m_m_gate_b, m_g_a2, m_g_a2_b, m_m_head_g, m_g_head_g, m_w_branch_m, m_w_branch_g, m_w_out, m_norm2_g, m_w_ff_gate, m_w_ff_up, m_w_ff_down, m_final_g, v_meta_tokens, v_norm1_g, v_w_in, v_conv_w, v_conv_b, v_m_gate_b, v_g_a2, v_g_a2_b, v_m_head_g, v_g_head_g, v_w_branch_m, v_w_branch_g, v_w_out, v_norm2_g, v_w_ff_gate, v_w_ff_up, v_w_ff_down, v_final_g):
    w_sh = dict(meta_tokens=meta_tokens, w_in=w_in[0], conv_w=conv_w[0], g_a2=g_a2[0], m_head_g=m_head_g[0],
                g_head_g=g_head_g[0], w_branch_m=w_branch_m[0], w_branch_g=w_branch_g[0], w_out=w_out[0],
                w_ff_gate=w_ff_gate[0], w_ff_up=w_ff_up[0], w_ff_down=w_ff_down[0])
    m_sh = dict(meta_tokens=m_meta_tokens, w_in=m_w_in[0], conv_w=m_conv_w[0], g_a2=m_g_a2[0],
                m_head_g=m_m_head_g[0], g_head_g=m_g_head_g[0], w_branch_m=m_w_branch_m[0],
                w_branch_g=m_w_branch_g[0], w_out=m_w_out[0], w_ff_gate=m_w_ff_gate[0], w_ff_up=m_w_ff_up[0],
                w_ff_down=m_w_ff_down[0])
    v_sh = dict(meta_tokens=v_meta_tokens, w_in=v_w_in[0], conv_w=v_conv_w[0], g_a2=v_g_a2[0],
                m_head_g=v_m_head_g[0], g_head_g=v_g_head_g[0], w_branch_m=v_w_branch_m[0],
                w_branch_g=v_w_branch_g[0], w_out=v_w_out[0], w_ff_gate=v_w_ff_gate[0], w_ff_up=v_w_ff_up[0],
                w_ff_down=v_w_ff_down[0])
    w_rep = dict(norm1_g=norm1_g, conv_b=conv_b, m_gate_b=m_gate_b, g_a2_b=g_a2_b, norm2_g=norm2_g, final_g=final_g)
    m_rep = dict(norm1_g=m_norm1_g, conv_b=m_conv_b, m_gate_b=m_m_gate_b, g_a2_b=m_g_a2_b, norm2_g=m_norm2_g,
                 final_g=m_final_g)
    v_rep = dict(norm1_g=v_norm1_g, conv_b=v_conv_b, m_gate_b=v_m_gate_b, g_a2_b=v_g_a2_b, norm2_g=v_norm2_g,
                 final_g=v_final_g)
    dev = 4 * lax.axis_index("x") + 2 * lax.axis_index("y") + lax.axis_index("c")
    tiny_names = [n for n, _ in TINY]
    repl_names = [n for n, _ in REPL]
    tiny_shard_shapes = [(s[0], s[1] // NDEV) for _, s in TINY]

    in8, tiny8 = all_gather([w_sh["w_in"].astype(MXU), pack_rows([w_sh[n] for n in tiny_names], ROWS_GATHER)],
                            "param_all_gather")
    late_names = ["w_branch_m", "w_branch_g", "w_out", "w_ff_gate", "w_ff_up", "w_ff_down"]
    late, first_order = send_start([w_sh[n].astype(MXU) for n in late_names], False, tiny8, "late_weights_start")
    wp = regroup_cols(in8)
    handles = {}

    def late_weights(after):
        bm8, bg8, out8, ffg8, ffu8, ffd8 = send_wait(late, after, "late_weights_wait")
        w_gu = interleave_gu(from_col_blocks(ffg8), from_col_blocks(ffu8))
        return bm8.reshape(D, D), bg8.reshape(D, D), out8.reshape(D, D), w_gu, ffd8.reshape(DFF, D)

    def send_early(g):
        d_gate, d_up = split_gu(g["w_gu"])
        blocks = [g["w_branch_m"].reshape(NDEV, D // NDEV, D).astype(WIRE),
                  g["w_branch_g"].reshape(NDEV, D // NDEV, D).astype(WIRE),
                  g["w_out"].reshape(NDEV, D // NDEV, D).astype(WIRE),
                  col_blocks(d_gate).astype(WIRE), col_blocks(d_up).astype(WIRE),
                  g["w_ff_down"].reshape(NDEV, DFF // NDEV, D).astype(WIRE)]
        handles["early"], token = send_start(blocks, True, blocks[0], "early_grads_start")
        return token

    def send_wp(d_wp):
        blocks = [ungroup_cols(d_wp).astype(WIRE)]
        handles["wp"], token = send_start(blocks, True, blocks[0], "proj_grads_start")
        return token

    tiny_full = {}
    for j in range(NDEV):
        for name, blk in zip(tiny_names, unpack_rows(tiny8[j], tiny_shard_shapes)):
            tiny_full.setdefault(name, []).append(blk)
    tiny_full = {n: jnp.concatenate(v, axis=1) for n, v in tiny_full.items()}

    loss, grad_x, g = local_step(
        x[0], loss_target[0], tiny_full["meta_tokens"], norm1_g, wp, tiny_full["conv_w"], conv_b, m_gate_b[0],
        tiny_full["g_a2"], g_a2_b, tiny_full["m_head_g"], tiny_full["g_head_g"], norm2_g, final_g,
        late_weights, send_early, send_wp, first_order)

    rep = pack_rows([g[n] for n in tiny_names + repl_names] + [loss[0, 0:1]], ROWS_REP)
    (got_rep,) = exchange([], rep, "small_grad_exchange")
    got_early = send_wait(handles["early"], got_rep, "early_grads_wait")
    (got_wp,) = send_wait(handles["wp"], got_rep, "proj_grads_wait")

    result = {}

    def update(name, parts, tr):
        outs = adamw(parts, w_sh[name], m_sh[name], v_sh[name], "adamw_" + name, tr)
        for kind, arr in zip(("grad", "delta", "new_m", "new_v"), outs):
            result[kind, name] = arr[None]

    update("w_in", got_wp, 128)
    update("w_branch_m", got_early[0], 128)
    update("w_branch_g", got_early[1], 128)
    update("w_out", got_early[2], 128)
    update("w_ff_gate", got_early[3], 256)
    update("w_ff_up", got_early[4], 256)
    update("w_ff_down", got_early[5], DFF // NDEV)

    rep_sum = sum_parts(got_rep, "sum_small", 1024)
    rep_g = unpack_rows(rep_sum, [s for _, s in TINY] + [s for _, s in REPL] + [(1,)])
    own_g = [lax.dynamic_slice_in_dim(gf, dev * ss[1], ss[1], axis=1) for gf, ss in zip(rep_g, tiny_shard_shapes)]
    own_g += rep_g[len(TINY):len(TINY) + len(REPL)]
    w_all = {**w_sh, **w_rep}
    m_all = {**m_sh, **m_rep}
    v_all = {**v_sh, **v_rep}
    names = tiny_names + repl_names
    outs = adamw(pack_rows(own_g, ROWS_OWN)[None], pack_rows([w_all[n] for n in names], ROWS_OWN),
                 pack_rows([m_all[n] for n in names], ROWS_OWN), pack_rows([v_all[n] for n in names], ROWS_OWN),
                 "adamw_small", ROWS_OWN)
    shapes = tiny_shard_shapes + [s for _, s in REPL]
    for kind, packed in zip(("grad", "delta", "new_m", "new_v"), outs):
        for name, arr in zip(names, unpack_rows(packed, shapes)):
            result[kind, name] = arr[None] if name in tiny_names and name != "meta_tokens" else arr
    loss_total = rep_g[-1][0]
    order = ["meta_tokens", "norm1_g", "w_in", "conv_w", "conv_b", "m_gate_b", "g_a2", "g_a2_b", "m_head_g", "g_head_g",
             "w_branch_m", "w_branch_g", "w_out", "norm2_g", "w_ff_gate", "w_ff_up", "w_ff_down", "final_g"]
    return (loss_total, grad_x[None], *[result[kind, n] for kind in ("grad", "delta", "new_m", "new_v") for n in order])
```

```python
import functools

import jax
import jax.numpy as jnp
from jax import lax
from jax.experimental import pallas as pl
from jax.experimental.pallas import tpu as pltpu

F32 = jnp.float32
MXU = jnp.bfloat16
WIRE = jnp.bfloat16

D = 1024
NH = 4
DV = 256
DQK = 128
L = 64
NMETA = 16
PADR = 512
LM = 256
CH0 = PADR // LM - 1
NPADROWS = PADR - NMETA
RANK = 16
DFF = 2816
EPS = 1e-6
TAU = 16.0
QSCALE = DQK ** -0.5
NEG = -1e30
NDEV = 8

MV0, MO0, GQ0, GK0, GV0, GR0, QK0, GM0, GG0, SM0 = 0, 1024, 2048, 2560, 3072, 4096, 5120, 6144, 7168, 8192
NP = 8320
NPROJ = 8216

ADAM_LR, ADAM_B1, ADAM_B2, ADAM_EPS, ADAM_WD, ADAM_STEP = 0.001, 0.9, 0.999, 1e-08, 0.01, 10

VMEM_LIMIT = 56 * 1024 * 1024
TM = 512


def _cp(sem):
    return pltpu.CompilerParams(dimension_semantics=sem, vmem_limit_bytes=VMEM_LIMIT)


def _sigmoid(x):
    return 1.0 / (1.0 + jnp.exp(-x))


def _log_sigmoid(x):
    return jnp.minimum(x, 0.0) - jnp.log1p(jnp.exp(-jnp.abs(x)))


def _dot(a, b, ca, cb):
    return lax.dot_general(a.astype(MXU), b.astype(MXU), (((ca,), (cb,)), ((), ())), preferred_element_type=F32)


def _dot_exact(a, b):
    return lax.dot_general(a, b, (((1,), (0,)), ((), ())), precision=lax.Precision.HIGHEST,
                           preferred_element_type=F32)


def _rb(tm, w, cb):
    return pl.BlockSpec((tm, w), lambda i: (i, cb))


def _const(shape):
    nd = len(shape)
    return pl.BlockSpec(shape, lambda i: (0,) * nd)


def _pick(n, target):
    if n <= target:
        return n
    best = None
    for t in range(128, target + 1, 128):
        if n % t == 0:
            best = t
    assert best is not None, (n, target)
    return best


def matmul(a, b, mode, name, add=None, out_dtype=F32, tm=512, tn=1664, tk=1024, order=None):
    if mode == "nn":
        (M, K), (K2, N) = a.shape, b.shape
    elif mode == "nt":
        (M, K), (N, K2) = a.shape, b.shape
    else:
        (K, M), (K2, N) = a.shape, b.shape
    assert K == K2, (a.shape, b.shape, mode)
    tm, tn, tk = _pick(M, tm), _pick(N, tn), _pick(K, tk)
    nk = K // tk
    assert nk == 1 or out_dtype == F32
    ca, cb = {"nn": (1, 0), "nt": (1, 1), "tn": (0, 0)}[mode]
    a_spec = {"nn": pl.BlockSpec((tm, tk), lambda j, i, k: (i, k)),
              "nt": pl.BlockSpec((tm, tk), lambda j, i, k: (i, k)),
              "tn": pl.BlockSpec((tk, tm), lambda j, i, k: (k, i))}[mode]
    b_spec = {"nn": pl.BlockSpec((tk, tn), lambda j, i, k: (k, j)),
              "nt": pl.BlockSpec((tn, tk), lambda j, i, k: (j, k)),
              "tn": pl.BlockSpec((tk, tn), lambda j, i, k: (k, j))}[mode]
    o_spec = pl.BlockSpec((tm, tn), lambda j, i, k: (i, j))
    has_add = add is not None

    def body(*refs):
        if order is not None:
            refs = refs[:-2] + refs[-1:]
        if has_add:
            a_ref, b_ref, add_ref, o_ref = refs
        else:
            a_ref, b_ref, o_ref = refs
            add_ref = None
        part = _dot(a_ref[...], b_ref[...], ca, cb)
        if nk == 1:
            if has_add:
                part = part + add_ref[...]
            o_ref[...] = part.astype(o_ref.dtype)
            return
        k = pl.program_id(2)

        @pl.when(k == 0)
        def _():
            o_ref[...] = part + add_ref[...] if has_add else part

        @pl.when(k > 0)
        def _():
            o_ref[...] += part

    in_specs = [a_spec, b_spec] + ([o_spec] if has_add else [])
    args = (a, b) + ((add,) if has_add else ())
    if order is not None:
        in_specs.append(pl.BlockSpec(order.shape, lambda j, i, k: (0, 0)))
        args += (order,)
    return pl.pallas_call(
        body, name=name, grid=(N // tn, M // tm, nk),
        in_specs=in_specs, out_specs=o_spec,
        out_shape=jax.ShapeDtypeStruct((M, N), out_dtype),
        compiler_params=_cp(("parallel", "parallel", "arbitrary")),
    )(*args)


def _h0_tile(i, x_ref, meta_ref):
    assert PADR == TM
    front = jnp.concatenate([jnp.zeros((NPADROWS, D), F32), meta_ref[...]], axis=0)
    return jnp.where(i == 0, front, x_ref[...])


_TOKENS = pl.BlockSpec((TM, D), lambda i: (jnp.maximum(i - 1, 0), 0))


def rms_fwd_input(x, meta, g, name):
    R = x.shape[0] + PADR

    def body(x_ref, meta_ref, g_ref, y_ref):
        xv = _h0_tile(pl.program_id(0), x_ref, meta_ref)
        r = lax.rsqrt(jnp.mean(xv * xv, axis=-1, keepdims=True) + EPS)
        y_ref[...] = (xv * r * g_ref[...]).astype(y_ref.dtype)

    return pl.pallas_call(
        body, name=name, grid=(R // TM,),
        in_specs=[_TOKENS, _const((NMETA, D)), _const((1, D))], out_specs=_rb(TM, D, 0),
        out_shape=jax.ShapeDtypeStruct((R, D), MXU), compiler_params=_cp(("parallel",)),
    )(x, meta, g)


def dgrad_rms_bwd(a, w, x, g, dres, name, tm, tk):
    R, K = a.shape
    tm, tk = _pick(R, tm), _pick(K, tk)
    nk = K // tk

    def body(a_ref, w_ref, x_ref, g_ref, dres_ref, dx_ref, dg_ref):
        i, k = pl.program_id(0), pl.program_id(1)
        part = _dot(a_ref[...], w_ref[...], 1, 1)

        @pl.when(k == 0)
        def _():
            dx_ref[...] = part

        @pl.when(k > 0)
        def _():
            dx_ref[...] += part

        @pl.when(k == nk - 1)
        def _():
            xv, dyv = x_ref[...], dx_ref[...]
            r = lax.rsqrt(jnp.mean(xv * xv, axis=-1, keepdims=True) + EPS)
            dyg = dyv * g_ref[...]
            dx_ref[...] = dres_ref[...] + r * dyg - xv * (r * r * r * jnp.mean(dyg * xv, axis=-1, keepdims=True))
            gpart = jnp.sum(dyv * xv * r, axis=0, keepdims=True)

            @pl.when(i == 0)
            def _():
                dg_ref[...] = gpart

            @pl.when(i > 0)
            def _():
                dg_ref[...] += gpart

    row = pl.BlockSpec((tm, D), lambda i, k: (i, 0))
    return pl.pallas_call(
        body, name=name, grid=(R // tm, nk),
        in_specs=[pl.BlockSpec((tm, tk), lambda i, k: (i, k)), pl.BlockSpec((D, tk), lambda i, k: (0, k)), row,
                  pl.BlockSpec((1, D), lambda i, k: (0, 0)), row],
        out_specs=[row, pl.BlockSpec((1, D), lambda i, k: (0, 0))],
        out_shape=[jax.ShapeDtypeStruct((R, D), F32), jax.ShapeDtypeStruct((1, D), F32)],
        compiler_params=_cp(("arbitrary", "arbitrary")),
    )(a, w, x, g, dres)


def rms_bwd_input(dy, x, meta, g, dres, name):
    R = dy.shape[0]

    def body(dy_ref, x_ref, meta_ref, g_ref, dres_ref, dx_ref, dmeta_ref, dg_ref):
        i = pl.program_id(0)
        xv, dyv = _h0_tile(i, x_ref, meta_ref), dy_ref[...]
        r = lax.rsqrt(jnp.mean(xv * xv, axis=-1, keepdims=True) + EPS)
        dyg = dyv * g_ref[...]
        dx = dres_ref[...] + r * dyg - xv * (r * r * r * jnp.mean(dyg * xv, axis=-1, keepdims=True))
        dx_ref[...] = dx
        part = jnp.sum(dyv * xv * r, axis=0, keepdims=True)

        @pl.when(i == 0)
        def _():
            dg_ref[...] = part
            dmeta_ref[...] = dx[NPADROWS:PADR]

        @pl.when(i > 0)
        def _():
            dg_ref[...] += part

    return pl.pallas_call(
        body, name=name, grid=(R // TM,),
        in_specs=[_rb(TM, D, 0), _TOKENS, _const((NMETA, D)), _const((1, D)), _rb(TM, D, 0)],
        out_specs=[_TOKENS, _const((NMETA, D)), _const((1, D))],
        out_shape=[jax.ShapeDtypeStruct((R - PADR, D), F32), jax.ShapeDtypeStruct((NMETA, D), F32),
                   jax.ShapeDtypeStruct((1, D), F32)],
        compiler_params=_cp(("arbitrary",)),
    )(dy, x, meta, g, dres)


def _shift_down(cur, prev8, s):
    tm = cur.shape[0]
    rolled = pltpu.roll(cur, s, 0)
    rows8 = lax.broadcasted_iota(jnp.int32, (8, cur.shape[1]), 0)
    head = jnp.where(rows8 < s, pltpu.roll(prev8, s, 0), rolled[0:8])
    return jnp.concatenate([head, rolled[8:tm]], axis=0)


def _shift_up(cur, next8, s):
    tm = cur.shape[0]
    rolled = pltpu.roll(cur, tm - s, 0)
    rows8 = lax.broadcasted_iota(jnp.int32, (8, cur.shape[1]), 0)
    tail = jnp.where(rows8 >= 8 - s, pltpu.roll(next8, 8 - s, 0), rolled[tm - 8:tm])
    return jnp.concatenate([rolled[0:tm - 8], tail], axis=0)


def prep_fwd(proj, conv_w, conv_b, gb_row):
    R = proj.shape[0]
    t8 = TM // 8

    def body(x_ref, halo_ref, sm_ref, w_ref, b_ref, gb_ref, c_ref, qk_ref, gl_ref):
        i = pl.program_id(0)
        x = x_ref[...]
        halo = halo_ref[...]
        w = w_ref[...]
        c = x * w[3:4, :] + b_ref[...]
        for s in (1, 2, 3):
            c = c + _shift_down(x, halo, s) * w[3 - s:4 - s, :]
        c_ref[...] = c
        qk_ref[...] = c * _sigmoid(c)
        z = sm_ref[...] + gb_ref[...]
        lane = lax.broadcasted_iota(jnp.int32, z.shape, 1)
        row = lax.broadcasted_iota(jnp.int32, z.shape, 0) + i * TM
        valid = row >= NPADROWS
        logi = jnp.where(valid, z, NEG)
        logf = jnp.where(valid, _log_sigmoid(z), 0.0)
        gl_ref[...] = jnp.where(lane < 4, logi, jnp.where(lane < 8, logf, 0.0))

    return pl.pallas_call(
        body, name="prep_fwd", grid=(R // TM,),
        in_specs=[_rb(TM, 1024, QK0 // 1024),
                  pl.BlockSpec((8, 1024), lambda i: (jnp.maximum(i * t8 - 1, 0), QK0 // 1024)),
                  _rb(TM, 128, SM0 // 128), _const((4, 1024)), _const((1, 1024)), _const((1, 128))],
        out_specs=[_rb(TM, 1024, 0), _rb(TM, 1024, 0), _rb(TM, 128, 0)],
        out_shape=[jax.ShapeDtypeStruct((R, 1024), F32), jax.ShapeDtypeStruct((R, 1024), F32),
                   jax.ShapeDtypeStruct((R, 128), F32)],
        compiler_params=_cp(("parallel",)),
    )(proj, proj, proj, conv_w, conv_b, gb_row)


def merge_fwd(ym, yg, w_bm, w_bg, proj):
    R = ym.shape[0]

    def body(ym_ref, yg_ref, wm_ref, wg_ref, gm_ref, gg_ref, bm_ref, bg_ref, o_ref):
        bm = _dot(ym_ref[...], wm_ref[...], 1, 0)
        bg = _dot(yg_ref[...], wg_ref[...], 1, 0)
        bm_ref[...] = bm
        bg_ref[...] = bg
        o_ref[...] = (_sigmoid(gm_ref[...]) * bm + _sigmoid(gg_ref[...]) * bg).astype(o_ref.dtype)

    return pl.pallas_call(
        body, name="merge_fwd", grid=(R // TM,),
        in_specs=[_rb(TM, D, 0), _rb(TM, D, 0), _const((D, D)), _const((D, D)), _rb(TM, D, GM0 // D),
                  _rb(TM, D, GG0 // D)],
        out_specs=[_rb(TM, D, 0), _rb(TM, D, 0), _rb(TM, D, 0)],
        out_shape=[jax.ShapeDtypeStruct((R, D), F32), jax.ShapeDtypeStruct((R, D), F32),
                   jax.ShapeDtypeStruct((R, D), MXU)],
        compiler_params=_cp(("parallel",)),
    )(ym, yg, w_bm, w_bg, proj, proj)


def merge_bwd(dh1, w_out, bm, bg, proj):
    R = bm.shape[0]

    def body(dh_ref, w_ref, bm_ref, bg_ref, gm_ref, gg_ref, dbm_ref, dbg_ref, dp_ref):
        dm = _dot(dh_ref[...], w_ref[...], 1, 1)
        sm, sg = _sigmoid(gm_ref[...]), _sigmoid(gg_ref[...])
        dbm_ref[...] = (dm * sm).astype(dbm_ref.dtype)
        dbg_ref[...] = (dm * sg).astype(dbg_ref.dtype)
        dp_ref[:, 0:D] = (dm * bm_ref[...] * sm * (1.0 - sm)).astype(dp_ref.dtype)
        dp_ref[:, D:2 * D] = (dm * bg_ref[...] * sg * (1.0 - sg)).astype(dp_ref.dtype)

    return pl.pallas_call(
        body, name="merge_bwd", grid=(R // TM,),
        in_specs=[_rb(TM, D, 0), _const((D, D)), _rb(TM, D, 0), _rb(TM, D, 0), _rb(TM, D, GM0 // D),
                  _rb(TM, D, GG0 // D)],
        out_specs=[_rb(TM, D, 0), _rb(TM, D, 0), _rb(TM, 2 * D, GM0 // (2 * D))],
        out_shape=[jax.ShapeDtypeStruct((R, D), MXU), jax.ShapeDtypeStruct((R, D), MXU),
                   jax.ShapeDtypeStruct((R, NP), MXU)],
        compiler_params=_cp(("parallel",)),
    )(dh1, w_out, bm, bg, proj, proj)


TF = DFF // 2
TMF = 768


def interleave_gu(gate, up):
    return jnp.concatenate([gate[:, :TF], up[:, :TF], gate[:, TF:], up[:, TF:]], axis=1)


def split_gu(gu):
    return (jnp.concatenate([gu[:, 0:TF], gu[:, 2 * TF:3 * TF]], axis=1),
            jnp.concatenate([gu[:, TF:2 * TF], gu[:, 3 * TF:]], axis=1))


def ff_in_fwd(hn, w_gu):
    R = hn.shape[0]

    def body(x_ref, w_ref, au_ref, ff_ref):
        au = _dot(x_ref[...], w_ref[...], 1, 0)
        au_ref[...] = au.astype(au_ref.dtype)
        a = au[:, :TF]
        ff_ref[...] = (a * _sigmoid(a) * au[:, TF:]).astype(ff_ref.dtype)

    tm = _pick(R, TMF)
    return pl.pallas_call(
        body, name="ff_in_fwd", grid=(DFF // TF, R // tm),
        in_specs=[pl.BlockSpec((tm, D), lambda j, i: (i, 0)), pl.BlockSpec((D, 2 * TF), lambda j, i: (0, j))],
        out_specs=[pl.BlockSpec((tm, 2 * TF), lambda j, i: (i, j)), pl.BlockSpec((tm, TF), lambda j, i: (i, j))],
        out_shape=[jax.ShapeDtypeStruct((R, 2 * DFF), MXU), jax.ShapeDtypeStruct((R, DFF), MXU)],
        compiler_params=_cp(("parallel", "parallel")),
    )(hn, w_gu)


def ff_down_dgrad(dh2, w_down, au):
    R = dh2.shape[0]

    def body(d_ref, w_ref, au_ref, o_ref):
        dff = _dot(d_ref[...], w_ref[...], 1, 1)
        a = au_ref[:, :TF].astype(F32)
        u = au_ref[:, TF:].astype(F32)
        s = _sigmoid(a)
        o_ref[:, :TF] = (dff * u * s * (1.0 + a * (1.0 - s))).astype(o_ref.dtype)
        o_ref[:, TF:] = (dff * a * s).astype(o_ref.dtype)

    tm = _pick(R, TMF)
    return pl.pallas_call(
        body, name="ff_down_dgrad", grid=(DFF // TF, R // tm),
        in_specs=[pl.BlockSpec((tm, D), lambda j, i: (i, 0)), pl.BlockSpec((TF, D), lambda j, i: (j, 0)),
                  pl.BlockSpec((tm, 2 * TF), lambda j, i: (i, j))],
        out_specs=pl.BlockSpec((tm, 2 * TF), lambda j, i: (i, j)),
        out_shape=jax.ShapeDtypeStruct((R, 2 * DFF), MXU),
        compiler_params=_cp(("parallel", "parallel")),
    )(dh2, w_down, au)


def out_proj_norm(merged, w, x, meta, g):
    R = merged.shape[0]

    def body(m_ref, w_ref, x_ref, meta_ref, g_ref, h_ref, n_ref):
        hv = _dot(m_ref[...], w_ref[...], 1, 0) + _h0_tile(pl.program_id(0), x_ref, meta_ref)
        h_ref[...] = hv
        r = lax.rsqrt(jnp.mean(hv * hv, axis=-1, keepdims=True) + EPS)
        n_ref[...] = (hv * r * g_ref[...]).astype(n_ref.dtype)

    return pl.pallas_call(
        body, name="out_fwd", grid=(R // TM,),
        in_specs=[_rb(TM, D, 0), _const((D, D)), _TOKENS, _const((NMETA, D)), _const((1, D))],
        out_specs=[_rb(TM, D, 0), _rb(TM, D, 0)],
        out_shape=[jax.ShapeDtypeStruct((R, D), F32), jax.ShapeDtypeStruct((R, D), MXU)],
        compiler_params=_cp(("parallel",)),
    )(merged, w, x, meta, g)


def ff_down_loss(ff, w_down, h1, gf, target):
    R = ff.shape[0]
    assert PADR == TM

    def body(f_ref, w_ref, h1_ref, g_ref, t_ref, dh_ref, loss_ref, dg_ref):
        i = pl.program_id(0)
        hv = _dot(f_ref[...], w_ref[...], 1, 0) + h1_ref[...]
        r = lax.rsqrt(jnp.mean(hv * hv, axis=-1, keepdims=True) + EPS)
        g = g_ref[...]
        live = (i >= 1).astype(F32)
        e = (hv * r * g - t_ref[...]) * live
        dy = e * (1.0 / D)
        dyg = dy * g
        dh_ref[...] = r * dyg - hv * (r * r * r * jnp.mean(dyg * hv, axis=-1, keepdims=True))
        lpart = jnp.zeros((1, 128), F32) + 0.5 * jnp.sum(jnp.sum(e * e, axis=1, keepdims=True), axis=0, keepdims=True) * (1.0 / D)
        gpart = jnp.sum(dy * hv * r, axis=0, keepdims=True)

        @pl.when(i == 0)
        def _():
            loss_ref[...] = lpart
            dg_ref[...] = gpart

        @pl.when(i > 0)
        def _():
            loss_ref[...] += lpart
            dg_ref[...] += gpart

    return pl.pallas_call(
        body, name="ff_down_loss", grid=(R // TM,),
        in_specs=[_rb(TM, DFF, 0), _const((DFF, D)), _rb(TM, D, 0), _const((1, D)),
                  pl.BlockSpec((TM, D), lambda i: (jnp.maximum(i - 1, 0), 0))],
        out_specs=[_rb(TM, D, 0), _const((1, 128)), _const((1, D))],
        out_shape=[jax.ShapeDtypeStruct((R, D), F32), jax.ShapeDtypeStruct((1, 128), F32),
                   jax.ShapeDtypeStruct((1, D), F32)],
        compiler_params=_cp(("arbitrary",)),
    )(ff, w_down, h1, gf, target)


def conv_bwd(dc, proj, conv_w, dproj):
    R = dc.shape[0]
    t8 = TM // 8
    nt = R // TM

    def body(dc_ref, nxt_ref, x_ref, prv_ref, w_ref, dp_in, dp_ref, dw_ref):
        del dp_in
        i = pl.program_id(0)
        dcv = dc_ref[...]
        nxt = nxt_ref[...] * (i < nt - 1).astype(F32)
        x = x_ref[...]
        prv = prv_ref[...]
        w = w_ref[...]
        dx = dcv * w[3:4, :]
        rows = [None] * 4
        rows[3] = jnp.sum(dcv * x, axis=0, keepdims=True)
        for s in (1, 2, 3):
            dx = dx + _shift_up(dcv, nxt, s) * w[3 - s:4 - s, :]
            rows[3 - s] = jnp.sum(dcv * _shift_down(x, prv, s), axis=0, keepdims=True)
        dp_ref[...] = dx.astype(dp_ref.dtype)
        part = jnp.concatenate(rows + [jnp.sum(dcv, axis=0, keepdims=True), jnp.zeros((3, 1024), F32)], axis=0)

        @pl.when(i == 0)
        def _():
            dw_ref[...] = part

        @pl.when(i > 0)
        def _():
            dw_ref[...] += part

    return pl.pallas_call(
        body, name="conv_bwd", grid=(nt,),
        in_specs=[_rb(TM, 1024, 0),
                  pl.BlockSpec((8, 1024), lambda i: (jnp.minimum((i + 1) * t8, nt * t8 - 1), 0)),
                  _rb(TM, 1024, QK0 // 1024),
                  pl.BlockSpec((8, 1024), lambda i: (jnp.maximum(i * t8 - 1, 0), QK0 // 1024)),
                  _const((4, 1024)), pl.BlockSpec(memory_space=pl.ANY)],
        out_specs=[_rb(TM, 1024, QK0 // 1024), _const((8, 1024))],
        out_shape=[jax.ShapeDtypeStruct((R, NP), MXU), jax.ShapeDtypeStruct((8, 1024), F32)],
        input_output_aliases={5: 0},
        compiler_params=_cp(("arbitrary",)),
    )(dc, dc, proj, proj, conv_w, dproj)


def small_bwd(dgl, dga, proj, gb_row, dproj):
    R = dgl.shape[0]

    def body(dgl_ref, dga_ref, sm_ref, gb_ref, dp_in, dp_ref, dgb_ref):
        del dp_in
        i = pl.program_id(0)
        z = sm_ref[...] + gb_ref[...]
        lane = lax.broadcasted_iota(jnp.int32, z.shape, 1)
        row = lax.broadcasted_iota(jnp.int32, z.shape, 0) + i * TM
        valid = row >= NPADROWS
        dgl_v = dgl_ref[...]
        dgate = jnp.where(valid, jnp.where(lane < 4, dgl_v, dgl_v * _sigmoid(-z)), 0.0)
        ds = jnp.where(lane < 8, dgate, dga_ref[...])
        dp_ref[...] = ds.astype(dp_ref.dtype)
        part = jnp.sum(jnp.where(lane < 8, dgate, 0.0), axis=0, keepdims=True)

        @pl.when(i == 0)
        def _():
            dgb_ref[...] = part

        @pl.when(i > 0)
        def _():
            dgb_ref[...] += part

    return pl.pallas_call(
        body, name="small_bwd", grid=(R // TM,),
        in_specs=[_rb(TM, 128, 0), _rb(TM, 128, 0), _rb(TM, 128, SM0 // 128), _const((1, 128)),
                  pl.BlockSpec(memory_space=pl.ANY)],
        out_specs=[_rb(TM, 128, SM0 // 128), _const((1, 128))],
        out_shape=[jax.ShapeDtypeStruct((R, NP), MXU), jax.ShapeDtypeStruct((1, 128), F32)],
        input_output_aliases={4: 0},
        compiler_params=_cp(("arbitrary",)),
    )(dgl, dga, proj, gb_row, dproj)


def _masks(n=L):
    r = lax.broadcasted_iota(jnp.int32, (n, n), 0)
    c = lax.broadcasted_iota(jnp.int32, (n, n), 1)
    return r >= c, r == c, r


def _to_row(col, eye):
    return jnp.sum(jnp.where(eye, col, 0.0), axis=0, keepdims=True)


def _to_col(row, eye):
    return jnp.sum(jnp.where(eye, row, 0.0), axis=1, keepdims=True)


def _mlstm_chunk(q, k, logi_c, logf_c, m, n):
    tril, eye, _ = _masks(q.shape[0])
    logi_r, logf_r = _to_row(logi_c, eye), _to_row(logf_c, eye)
    b_c = jnp.sum(jnp.where(tril, logf_r, 0.0), axis=1, keepdims=True)
    b_r = _to_row(b_c, eye)
    g = jnp.sum(logf_c, axis=0, keepdims=True)
    dmat = jnp.where(tril, b_c - b_r + logi_r, NEG)
    mrow = jnp.maximum(b_c + m, jnp.max(dmat, axis=1, keepdims=True))
    dm = jnp.exp(dmat - mrow)
    s = _dot(q, k, 1, 1)
    w = dm * s
    a_in = jnp.exp(b_c + m - mrow)
    qn = jnp.sum(q * n, axis=1, keepdims=True)
    den = a_in * qn + jnp.sum(w, axis=1, keepdims=True)
    floor = jnp.exp(-mrow)
    nrm = jnp.maximum(jnp.abs(den), floor)
    wlog_c = g - b_c + logi_c
    m_new = jnp.maximum(g + m, jnp.max(wlog_c, axis=0, keepdims=True))
    a_st = jnp.exp(g + m - m_new)
    w_c = jnp.exp(wlog_c - m_new)
    return dict(b_c=b_c, g=g, dm=dm, s=s, w=w, a_in=a_in, qn=qn, den=den, floor=floor, nrm=nrm,
                m_new=m_new, a_st=a_st, w_c=w_c, tril=tril, eye=eye)


def _mlstm_head_fwd(h, glv, qk_ref, v_ref, mo_ref, hg_ref, hm_ref, ym_ref, cs_ref, nm_ref, c_s, nm_s):
    q = qk_ref[:, h * DQK:(h + 1) * DQK] * QSCALE
    k = qk_ref[:, 512 + h * DQK:512 + (h + 1) * DQK]
    v = v_ref[:, h * DV:(h + 1) * DV]
    C = c_s[h]
    n = nm_s[h, 0:1, :]
    m = nm_s[h, 1:2, 0:1]
    f = _mlstm_chunk(q, k, glv[:, h:h + 1], glv[:, 4 + h:5 + h], m, n)
    num = f["a_in"] * _dot(q, C, 1, 1) + _dot(f["w"], v, 1, 0)
    hh = num / f["nrm"]
    cs_ref[0, h] = C
    nm_ref[0, h] = nm_s[h]
    c_s[h] = f["a_st"] * C + _dot(f["w_c"] * v, k, 0, 0)
    n_new = f["a_st"] * n + jnp.sum(f["w_c"] * k, axis=0, keepdims=True)
    rowi = lax.broadcasted_iota(jnp.int32, (8, DQK), 0)
    nm_s[h] = jnp.where(rowi == 0, n_new, jnp.where(rowi == 1, f["m_new"], 0.0))
    rm = lax.rsqrt(jnp.mean(hh * hh, axis=-1, keepdims=True) + EPS)
    sl = slice(h * DV, (h + 1) * DV)
    hm_ref[:, sl] = hh
    ym_ref[:, sl] = (hh * rm * hg_ref[:, sl] * _sigmoid(mo_ref[:, sl])).astype(ym_ref.dtype)


def _mlstm_bwd_parts(dym_ref, hm_ref, qk_ref, cp_ref, v_ref, gl_ref, mo_ref, hg_ref, cs_ref, nm_ref,
                     dp_ref, dc_ref, dgl_ref, dhg_ref, dc_s, dn_s):
        def init():
            dc_s[...] = jnp.zeros_like(dc_s)
            dn_s[...] = jnp.zeros_like(dn_s)
            dhg_ref[...] = jnp.zeros_like(dhg_ref)

        def zero():
            dp_ref[...] = jnp.zeros_like(dp_ref)
            dc_ref[...] = jnp.zeros_like(dc_ref)
            dgl_ref[...] = jnp.zeros_like(dgl_ref)

        def compute(after_head):
            glv = gl_ref[...]
            lane = lax.broadcasted_iota(jnp.int32, (LM, 128), 1)
            dgl = jnp.zeros((LM, 128), F32)
            for h in range(NH):
                sl = slice(h * DV, (h + 1) * DV)
                sq = slice(h * DQK, (h + 1) * DQK)
                sk = slice(512 + h * DQK, 512 + (h + 1) * DQK)
                hh = hm_ref[:, sl]
                gain = hg_ref[:, sl]
                rm = lax.rsqrt(jnp.mean(hh * hh, axis=-1, keepdims=True) + EPS)
                sg = _sigmoid(mo_ref[:, sl])
                dyv = dym_ref[:, sl]
                dno = dyv * sg
                dp_ref[:, 1024 + h * DV:1024 + (h + 1) * DV] = (dyv * hh * rm * gain * sg * (1.0 - sg)).astype(dp_ref.dtype)
                dhg_ref[:, sl] += jnp.sum(dno * hh * rm, axis=0, keepdims=True)
                dnog = dno * gain
                dh = rm * dnog - hh * (rm * rm * rm * jnp.mean(dnog * hh, axis=-1, keepdims=True))
                q = qk_ref[:, sq] * QSCALE
                k = qk_ref[:, sk]
                v = v_ref[:, sl]
                C = cs_ref[0, h]
                n = nm_ref[0, h, 0:1, :]
                m = nm_ref[0, h, 1:2, 0:1]
                f = _mlstm_chunk(q, k, glv[:, h:h + 1], glv[:, 4 + h:5 + h], m, n)
                eye = f["eye"]
                a_in, nrm, den, w = f["a_in"], f["nrm"], f["den"], f["w"]
                dnum = dh / nrm
                dnrm = -jnp.sum(dh * hh, axis=1, keepdims=True) / nrm
                dden = jnp.where(jnp.abs(den) >= f["floor"], dnrm * jnp.sign(den), 0.0)
                dw = _dot(dnum, v, 1, 1) + dden
                dv = _dot(w, dnum, 0, 0)
                ds = dw * f["dm"]
                e = dw * w
                qc = _dot(q, C, 1, 1)
                dq = _dot(ds, k, 1, 0) + a_in * _dot(dnum, C, 1, 0) + (a_in * dden) * n
                dk = _dot(ds, q, 0, 0)
                dC_in = _dot(a_in * dnum, q, 0, 0)
                dn_in = jnp.sum((a_in * dden) * q, axis=0, keepdims=True)
                da_in = jnp.sum(dnum * qc, axis=1, keepdims=True) + dden * f["qn"]
                col_e = _to_col(jnp.sum(e, axis=0, keepdims=True), eye)
                db = jnp.sum(e, axis=1, keepdims=True) + da_in * a_in - col_e
                dlogi = col_e
                dCp = dc_s[h]
                dnp = dn_s[h, 0:1, :]
                a_st, w_c = f["a_st"], f["w_c"]
                da_st = (jnp.sum(jnp.sum(dCp * C, axis=1, keepdims=True), axis=0, keepdims=True)
                         + jnp.sum(dnp * n, axis=1, keepdims=True))
                vdc = _dot(v, dCp, 1, 0)
                dw_c = jnp.sum((vdc + dnp) * k, axis=1, keepdims=True)
                dv = dv + w_c * _dot(k, dCp, 1, 1)
                dk = dk + w_c * (vdc + dnp)
                fw = dw_c * w_c
                dg = jnp.sum(fw, axis=0, keepdims=True) + da_st * a_st
                db = db - fw
                dlogi = dlogi + fw
                rowc = lax.broadcasted_iota(jnp.int32, (LM, 1), 0)
                db = db + jnp.where(rowc == LM - 1, dg, 0.0)
                triu = lax.broadcasted_iota(jnp.int32, (LM, LM), 1) >= lax.broadcasted_iota(jnp.int32, (LM, LM), 0)
                dlogf = jnp.sum(jnp.where(triu, _to_row(db, eye), 0.0), axis=1, keepdims=True)
                dc_s[h] = a_st * dCp + dC_in
                dn_new = a_st * dnp + dn_in
                dn_s[h] = jnp.zeros((8, DQK), F32) + dn_new
                cq, ck = cp_ref[:, sq], cp_ref[:, sk]
                s_q, s_k = _sigmoid(cq), _sigmoid(ck)
                dc_ref[:, sq] = dq * QSCALE * s_q * (1.0 + cq * (1.0 - s_q))
                dc_ref[:, sk] = dk * s_k * (1.0 + ck * (1.0 - s_k))
                dp_ref[:, sl] = dv.astype(dp_ref.dtype)
                dgl = jnp.where(lane == h, dlogi, jnp.where(lane == 4 + h, dlogf, dgl))
                after_head(h)
            dgl_ref[...] = dgl

        return init, zero, compute


def _gla_logs(sm, a2p, b2, valid):
    za = _dot(sm, a2p, 1, 0) + b2
    return za, jnp.where(valid, _log_sigmoid(za) * (1.0 / TAU), 0.0)


def _valid_rows(c, width):
    row = lax.broadcasted_iota(jnp.int32, (L, width), 0) + c * L
    return row >= NPADROWS


def _gla_cumsum(loga):
    tril, _, _ = _masks()
    return _dot_exact(tril.astype(F32), loga)


def _gla_chunk(q, k, la, bc):
    tril, _, _ = _masks()
    btot = jnp.sum(la, axis=0, keepdims=True)
    ebc = jnp.exp(bc)
    qd = q * ebc
    ki = k * jnp.exp(-bc)
    ke = k * jnp.exp(btot - bc)
    att = jnp.where(tril, _dot(qd, ki, 1, 1), 0.0)
    return dict(tril=tril, bc=bc, btot=btot, ebc=ebc, qd=qd, ki=ki, ke=ke, att=att)


def _col128(row):
    r = lax.broadcasted_iota(jnp.int32, (DQK, DQK), 0)
    c = lax.broadcasted_iota(jnp.int32, (DQK, DQK), 1)
    return jnp.sum(jnp.where(r == c, row, 0.0), axis=1, keepdims=True)


def _gla_fwd_parts(c, q_ref, k_ref, v_ref, gr_ref, sm_ref, a2_ref, b2_ref, hg_ref, hgl_ref, yg_ref, ss_ref, s_s):
        def init():
            s_s[...] = jnp.zeros_like(s_s)

        def zero():
            hgl_ref[...] = jnp.zeros_like(hgl_ref)
            yg_ref[...] = jnp.zeros_like(yg_ref)
            ss_ref[...] = jnp.zeros_like(ss_ref)

        def compute(s):
            rows = pl.ds(s * L, L)
            chunk(LM // L * c + s, q_ref.at[rows], k_ref.at[rows], v_ref.at[rows], gr_ref.at[rows], sm_ref.at[rows],
                  hgl_ref.at[rows], yg_ref.at[rows], ss_ref.at[pl.ds(s, 1)])

        def chunk(c, q_ref, k_ref, v_ref, gr_ref, sm_ref, hgl_ref, yg_ref, ss_ref):
            _, loga = _gla_logs(sm_ref[...], a2_ref[...], b2_ref[...], _valid_rows(c, 512))
            bc_all = _gla_cumsum(loga)
            for h in range(NH):
                sq = slice(h * DQK, (h + 1) * DQK)
                sl = slice(h * DV, (h + 1) * DV)
                q = q_ref[:, sq] * QSCALE
                k = k_ref[:, sq]
                v = v_ref[:, sl]
                S = s_s[h]
                f = _gla_chunk(q, k, loga[:, sq], bc_all[:, sq])
                o = _dot(f["att"], v, 1, 0) + _dot(f["qd"], S, 1, 0)
                ss_ref[0, h] = S
                s_s[h] = _col128(jnp.exp(f["btot"])) * S + _dot(f["ke"], v, 0, 0)
                rg = lax.rsqrt(jnp.mean(o * o, axis=-1, keepdims=True) + EPS)
                gr = gr_ref[:, sl]
                hgl_ref[:, sl] = o
                yg_ref[:, sl] = (o * rg * hg_ref[:, sl] * gr * _sigmoid(gr)).astype(yg_ref.dtype)

        return init, zero, compute


def _gla_bwd_parts(c, dy_ref, ho_ref, q_ref, k_ref, v_ref, gr_ref, sm_ref, a2_ref, b2_ref, hg_ref, ss_ref,
                   dp_ref, dga_ref, da2_ref, db2_ref, dhg_ref, ds_s):
        def init():
            ds_s[...] = jnp.zeros_like(ds_s)
            da2_ref[...] = jnp.zeros_like(da2_ref)
            db2_ref[...] = jnp.zeros_like(db2_ref)
            dhg_ref[...] = jnp.zeros_like(dhg_ref)

        def zero():
            dp_ref[...] = jnp.zeros_like(dp_ref)
            dga_ref[...] = jnp.zeros_like(dga_ref)

        def compute(s):
            rows = pl.ds(s * L, L)
            chunk(LM // L * c + s, dy_ref.at[rows], ho_ref.at[rows], q_ref.at[rows], k_ref.at[rows], v_ref.at[rows],
                  gr_ref.at[rows], sm_ref.at[rows], ss_ref.at[pl.ds(s, 1)], dp_ref.at[rows], dga_ref.at[rows])

        def chunk(c, dy_ref, ho_ref, q_ref, k_ref, v_ref, gr_ref, sm_ref, ss_ref, dp_ref, dga_ref):
            valid = _valid_rows(c, 512)
            sm = sm_ref[...]
            za, loga = _gla_logs(sm, a2_ref[...], b2_ref[...], valid)
            dbcs = []
            for h in range(NH):
                sq = slice(h * DQK, (h + 1) * DQK)
                sl = slice(h * DV, (h + 1) * DV)
                o = ho_ref[:, sl]
                gain = hg_ref[:, sl]
                rg = lax.rsqrt(jnp.mean(o * o, axis=-1, keepdims=True) + EPS)
                gr = gr_ref[:, sl]
                sg = _sigmoid(gr)
                dyv = dy_ref[:, sl]
                dno = dyv * gr * sg
                dp_ref[:, 2048 + h * DV:2048 + (h + 1) * DV] = (
                    dyv * o * rg * gain * sg * (1.0 + gr * (1.0 - sg))).astype(dp_ref.dtype)
                dhg_ref[:, sl] += jnp.sum(dno * o * rg, axis=0, keepdims=True)
                dnog = dno * gain
                do = rg * dnog - o * (rg * rg * rg * jnp.mean(dnog * o, axis=-1, keepdims=True))
                q = q_ref[:, sq] * QSCALE
                k = k_ref[:, sq]
                v = v_ref[:, sl]
                S = ss_ref[0, h]
                f = _gla_chunk(q, k, loga[:, sq], _gla_cumsum(loga[:, sq]))
                tril, qd, ki, ke = f["tril"], f["qd"], f["ki"], f["ke"]
                dSp = ds_s[h]
                datt = jnp.where(tril, _dot(do, v, 1, 1), 0.0)
                dqd = _dot(do, S, 1, 1) + _dot(datt, ki, 1, 0)
                dki = _dot(datt, qd, 0, 0)
                dv = _dot(f["att"], do, 0, 0) + _dot(ke, dSp, 1, 0)
                dke = _dot(v, dSp, 1, 1)
                ebt = jnp.exp(f["btot"])
                dbtot = jnp.sum(dke * ke, axis=0, keepdims=True) + ebt * _to_row128(jnp.sum(dSp * S, axis=1, keepdims=True))
                ds_s[h] = _dot(qd, do, 0, 0) + _col128(ebt) * dSp
                dq = dqd * f["ebc"]
                dk = dki * jnp.exp(-f["bc"]) + dke * jnp.exp(f["btot"] - f["bc"])
                dbc = dqd * qd - dki * ki - dke * ke
                rowc = lax.broadcasted_iota(jnp.int32, (L, DQK), 0)
                dbc = dbc + jnp.where(rowc == L - 1, dbtot, 0.0)
                triu = lax.broadcasted_iota(jnp.int32, (L, L), 1) >= lax.broadcasted_iota(jnp.int32, (L, L), 0)
                dbcs.append(_dot_exact(triu.astype(F32), dbc))
                dp_ref[:, sq] = (dq * QSCALE).astype(dp_ref.dtype)
                dp_ref[:, 512 + h * DQK:512 + (h + 1) * DQK] = dk.astype(dp_ref.dtype)
                dp_ref[:, 1024 + h * DV:1024 + (h + 1) * DV] = dv.astype(dp_ref.dtype)
            dza = jnp.where(valid, jnp.concatenate(dbcs, axis=1) * (1.0 / TAU) * _sigmoid(-za), 0.0)
            dga_ref[...] = _dot(dza, a2_ref[...], 1, 1)
            da2_ref[...] += _dot(sm, dza, 0, 0)
            db2_ref[...] += jnp.sum(dza, axis=0, keepdims=True)

        return init, zero, compute


def mix_fwd(qk, proj, gl, m_head_g, a2p, b2, g_head_g):
    R = qk.shape[0]
    NC = R // LM
    G = LM // L

    def body(qk_ref, mv_ref, gl_ref, mo_ref, mhg_ref, gq_ref, gk_ref, gv_ref, gr_ref, sm_ref, a2_ref, b2_ref, ghg_ref,
             hm_ref, ym_ref, cs_ref, nm_ref, hgl_ref, yg_ref, ss_ref, c_s, nm_s, s_s):
        c = pl.program_id(0)
        g_init, g_zero, g_compute = _gla_fwd_parts(c, gq_ref, gk_ref, gv_ref, gr_ref, sm_ref, a2_ref, b2_ref, ghg_ref,
                                                   hgl_ref, yg_ref, ss_ref, s_s)

        @pl.when(c <= CH0)
        def _():
            c_s[...] = jnp.zeros_like(c_s)
            nm_s[...] = jnp.zeros_like(nm_s)
            g_init()

        @pl.when(c < CH0)
        def _():
            hm_ref[...] = jnp.zeros_like(hm_ref)
            ym_ref[...] = jnp.zeros_like(ym_ref)
            cs_ref[...] = jnp.zeros_like(cs_ref)
            nm_ref[...] = jnp.zeros_like(nm_ref)
            g_zero()

        @pl.when(c >= CH0)
        def _():
            glv = gl_ref[...]
            assert LM // L == NH
            for h in range(NH):
                g_compute(h)
                _mlstm_head_fwd(h, glv, qk_ref, mv_ref, mo_ref, mhg_ref, hm_ref, ym_ref, cs_ref, nm_ref, c_s, nm_s)

    st_m = pl.BlockSpec((1, NH, DV, DQK), lambda c: (c, 0, 0, 0))
    st_n = pl.BlockSpec((1, NH, 8, DQK), lambda c: (c, 0, 0, 0))
    st_g = pl.BlockSpec((G, NH, DQK, DV), lambda c: (c, 0, 0, 0))
    return pl.pallas_call(
        body, name="mix_fwd", grid=(NC,),
        in_specs=[_rb(LM, 1024, 0), _rb(LM, 1024, MV0 // 1024), _rb(LM, 128, 0), _rb(LM, 1024, MO0 // 1024),
                  _const((1, 1024)),
                  _rb(LM, 512, GQ0 // 512), _rb(LM, 512, GK0 // 512), _rb(LM, 1024, GV0 // 1024),
                  _rb(LM, 1024, GR0 // 1024), _rb(LM, 128, SM0 // 128), _const((128, 512)), _const((1, 512)),
                  _const((1, 1024))],
        out_specs=[_rb(LM, 1024, 0), _rb(LM, 1024, 0), st_m, st_n, _rb(LM, 1024, 0), _rb(LM, 1024, 0), st_g],
        out_shape=[jax.ShapeDtypeStruct((R, 1024), F32), jax.ShapeDtypeStruct((R, 1024), MXU),
                   jax.ShapeDtypeStruct((NC, NH, DV, DQK), F32), jax.ShapeDtypeStruct((NC, NH, 8, DQK), F32),
                   jax.ShapeDtypeStruct((R, 1024), F32), jax.ShapeDtypeStruct((R, 1024), MXU),
                   jax.ShapeDtypeStruct((G * NC, NH, DQK, DV), F32)],
        scratch_shapes=[pltpu.VMEM((NH, DV, DQK), F32), pltpu.VMEM((NH, 8, DQK), F32),
                        pltpu.VMEM((NH, DQK, DV), F32)],
        compiler_params=_cp(("arbitrary",)),
    )(qk, proj, gl, proj, m_head_g, proj, proj, proj, proj, proj, a2p, b2, g_head_g)


def mix_bwd(dym, hm, qk, cpre, proj, gl, m_head_g, cs, nm, dyg, hgl, a2p, b2, g_head_g, ss, dproj):
    R = qk.shape[0]
    NC = R // LM
    rev = lambda c: NC - 1 - c
    GW = GR0 + 1024 - GQ0

    def body(dym_ref, hm_ref, qk_ref, cp_ref, mv_ref, gl_ref, mo_ref, mhg_ref, cs_ref, nm_ref,
             dyg_ref, ho_ref, gq_ref, gk_ref, gv_ref, gr_ref, sm_ref, a2_ref, b2_ref, ghg_ref, ss_ref, dp_in,
             dp_ref, dc_ref, dgl_ref, dmhg_ref, dga_ref, da2_ref, db2_ref, dghg_ref, dc_s, dn_s, ds_s):
        del dp_in
        step = pl.program_id(0)
        c = NC - 1 - step
        m_init, m_zero, m_compute = _mlstm_bwd_parts(
            dym_ref, hm_ref, qk_ref, cp_ref, mv_ref, gl_ref, mo_ref, mhg_ref, cs_ref, nm_ref,
            dp_ref.at[:, 0:GQ0], dc_ref, dgl_ref, dmhg_ref, dc_s, dn_s)
        g_init, g_zero, g_compute = _gla_bwd_parts(
            c, dyg_ref, ho_ref, gq_ref, gk_ref, gv_ref, gr_ref, sm_ref, a2_ref, b2_ref, ghg_ref, ss_ref,
            dp_ref.at[:, GQ0:GQ0 + GW], dga_ref, da2_ref, db2_ref, dghg_ref, ds_s)

        @pl.when(step == 0)
        def _():
            m_init()
            g_init()

        @pl.when(c < CH0)
        def _():
            m_zero()
            g_zero()

        @pl.when(c >= CH0)
        def _():
            assert LM // L == NH
            m_compute(lambda h: g_compute(NH - 1 - h))

    def rows(w, cb):
        return pl.BlockSpec((LM, w), lambda c: (rev(c), cb))

    return pl.pallas_call(
        body, name="mix_bwd", grid=(NC,),
        in_specs=[rows(1024, 0), rows(1024, 0), rows(1024, 0), rows(1024, 0), rows(1024, MV0 // 1024), rows(128, 0),
                  rows(1024, MO0 // 1024), _const((1, 1024)),
                  pl.BlockSpec((1, NH, DV, DQK), lambda c: (rev(c), 0, 0, 0)),
                  pl.BlockSpec((1, NH, 8, DQK), lambda c: (rev(c), 0, 0, 0)),
                  rows(1024, 0), rows(1024, 0), rows(512, GQ0 // 512), rows(512, GK0 // 512),
                  rows(1024, GV0 // 1024), rows(1024, GR0 // 1024), rows(128, SM0 // 128),
                  _const((128, 512)), _const((1, 512)), _const((1, 1024)),
                  pl.BlockSpec((LM // L, NH, DQK, DV), lambda c: (rev(c), 0, 0, 0)),
                  pl.BlockSpec(memory_space=pl.ANY)],
        out_specs=[rows(GQ0 + GW, 0), rows(1024, 0), rows(128, 0), _const((1, 1024)),
                   rows(128, 0), _const((128, 512)), _const((1, 512)), _const((1, 1024))],
        out_shape=[jax.ShapeDtypeStruct((R, NP), MXU), jax.ShapeDtypeStruct((R, 1024), F32),
                   jax.ShapeDtypeStruct((R, 128), F32), jax.ShapeDtypeStruct((1, 1024), F32),
                   jax.ShapeDtypeStruct((R, 128), F32), jax.ShapeDtypeStruct((128, 512), F32),
                   jax.ShapeDtypeStruct((1, 512), F32), jax.ShapeDtypeStruct((1, 1024), F32)],
        scratch_shapes=[pltpu.VMEM((NH, DV, DQK), F32), pltpu.VMEM((NH, 8, DQK), F32),
                        pltpu.VMEM((NH, DQK, DV), F32)],
        input_output_aliases={21: 0},
        compiler_params=_cp(("arbitrary",)),
    )(dym, hm, qk, cpre, proj, gl, proj, m_head_g, cs, nm, dyg, hgl, proj, proj, proj, proj, proj, a2p, b2,
      g_head_g, ss, dproj)


def _to_row128(col):
    r = lax.broadcasted_iota(jnp.int32, (DQK, DQK), 0)
    c = lax.broadcasted_iota(jnp.int32, (DQK, DQK), 1)
    return jnp.sum(jnp.where(r == c, col, 0.0), axis=0, keepdims=True)


def local_step(x, target, meta, norm1_g, wp, conv_w, conv_b, m_gate_b, g_a2, g_a2_b, m_head_g, g_head_g,
               norm2_g, final_g, late_weights, send_early, send_wp, first_order=None):
    seq = x.shape[0]
    assert seq % TM == 0
    gb_row = jnp.zeros((1, 128), F32).at[0, 0:8].set(m_gate_b.reshape(8))
    a2p = jnp.zeros((128, 512), F32).at[8:8 + RANK].set(g_a2)
    mhg = m_head_g.reshape(1, 1024)
    ghg = g_head_g.reshape(1, 1024)

    xn = rms_fwd_input(x, meta, norm1_g, "rms1_fwd")
    proj = matmul(xn, wp, "nn", "proj_fwd", tm=1536, order=first_order)
    cpre, qk, gl = prep_fwd(proj, conv_w, conv_b, gb_row)
    hm, ym, cs, nm, hgl, yg, ss = mix_fwd(qk, proj, gl, mhg, a2p, g_a2_b, ghg)
    w_bm, w_bg, w_out, w_gu, w_down = late_weights(ym)
    bm, bg, merged = merge_fwd(ym, yg, w_bm, w_bg, proj)
    h1, hn = out_proj_norm(merged, w_out, x, meta, norm2_g)
    au, ff = ff_in_fwd(hn, w_gu)
    dh2, loss, d_final_g = ff_down_loss(ff, w_down, h1, final_g.reshape(1, D), target)

    d_w_down = matmul(ff, dh2, "tn", "ff_down_wgrad", tm=1408, tk=1536)
    dau = ff_down_dgrad(dh2, w_down, au)
    d_w_gu = matmul(hn, dau, "tn", "ff_in_wgrad", tm=1024, tn=1408, tk=1536)
    dh1, d_norm2_g = dgrad_rms_bwd(dau, w_gu, h1, norm2_g, dh2, "ff_in_dgrad", 768, 2816)

    d_w_out = matmul(merged, dh1, "tn", "out_wgrad", tm=1024, tk=1536)
    dbm, dbg, dproj = merge_bwd(dh1, w_out, bm, bg, proj)
    d_w_bm = matmul(ym, dbm, "tn", "branch_m_wgrad", tm=1024, tk=1536)
    d_w_bg = matmul(yg, dbg, "tn", "branch_g_wgrad", tm=1024, tk=1536)
    token = send_early(dict(w_branch_m=d_w_bm, w_branch_g=d_w_bg, w_out=d_w_out, w_gu=d_w_gu, w_ff_down=d_w_down))
    dym = matmul(dbm, w_bm, "nt", "branch_m_dgrad", order=token)
    dyg = matmul(dbg, w_bg, "nt", "branch_g_dgrad")
    dproj, dc, dgl, d_mhg, dga, d_a2p, d_a2b, d_ghg = mix_bwd(
        dym, hm, qk, cpre, proj, gl, mhg, cs, nm, dyg, hgl, a2p, g_a2_b, ghg, ss, dproj)
    dproj, d_conv = conv_bwd(dc, proj, conv_w, dproj)
    dproj, d_gb = small_bwd(dgl, dga, proj, gb_row, dproj)
    d_wp = matmul(xn, dproj, "tn", "proj_wgrad", tm=1024, tn=1664, tk=1536)
    token = send_wp(d_wp)
    dxn = matmul(dproj, wp, "nt", "proj_dgrad", tm=1536, tk=1664, order=token)
    grad_x, d_meta, d_norm1_g = rms_bwd_input(dxn, x, meta, norm1_g, dh1, "rms1_bwd")

    grads = dict(
        meta_tokens=d_meta, norm1_g=d_norm1_g, conv_w=d_conv[0:4], conv_b=d_conv[4:5], m_gate_b=d_gb[0, 0:8].reshape(1, 2, 4),
        g_a2=d_a2p[8:8 + RANK], g_a2_b=d_a2b, m_head_g=d_mhg.reshape(NH, DV), g_head_g=d_ghg.reshape(NH, DV),
        norm2_g=d_norm2_g, final_g=d_final_g)
    return loss, grad_x, grads


_SEGS = [(0, 1024, QK0), (1024, 2048, MV0), (2048, 2056, SM0), (2056, 3080, MO0), (3080, 5128, GQ0),
         (5128, 5144, SM0 + 8), (5144, 6168, GR0), (6168, 8216, GM0)]
SHARD_W = NPROJ // NDEV


def regroup_cols(w8):
    parts = []
    for lo, hi, _ in sorted(_SEGS, key=lambda s: s[2]):
        while lo < hi:
            j = lo // SHARD_W
            end = min(hi, (j + 1) * SHARD_W)
            parts.append(w8[j, :, lo - j * SHARD_W:end - j * SHARD_W])
            lo = end
    parts.append(jnp.zeros((w8.shape[1], NP - NPROJ), w8.dtype))
    return jnp.concatenate(parts, axis=1)


def ungroup_cols(g):
    blocks = []
    for j in range(NDEV):
        lo, hi = j * SHARD_W, (j + 1) * SHARD_W
        parts = []
        for s_lo, s_hi, s_at in _SEGS:
            a, b = max(lo, s_lo), min(hi, s_hi)
            if a < b:
                parts.append(g[:, s_at + a - s_lo:s_at + b - s_lo])
        blocks.append(jnp.concatenate(parts, axis=1))
    return jnp.stack(blocks)


def col_blocks(g):
    r, c8 = g.shape
    return jnp.transpose(g.reshape(r, NDEV, c8 // NDEV), (1, 0, 2))


def from_col_blocks(g8):
    n, r, c = g8.shape
    return jnp.transpose(g8, (1, 0, 2)).reshape(r, n * c)


_MESHID = pl.DeviceIdType.MESH
_RELS = [(0, 0, 1), (1, 0, 0), (0, 1, 0), (1, 1, 0), (1, 0, 1), (0, 1, 1), (1, 1, 1)]


def _flip(v, bit):
    return 1 - v if bit else v


def all_gather(arrs, name):
    n = len(arrs)

    def body(*refs):
        ins, outs = refs[:n], refs[n:2 * n]
        send_sems, recv_sems, local_sems = refs[2 * n:]
        x, y, c = lax.axis_index("x"), lax.axis_index("y"), lax.axis_index("c")
        me, sibling = (x, y, c), (x, y, 1 - c)
        chips = [(1 - x, y), (x, 1 - y), (1 - x, 1 - y)]

        def slot(p):
            return 4 * p[0] + 2 * p[1] + p[2]

        def copy(a, k, block, to, src=None):
            dst = outs[a].at[slot(block)]
            return pltpu.make_async_remote_copy(
                src_ref=dst if src is None else src, dst_ref=dst,
                send_sem=send_sems.at[a, k], recv_sem=recv_sems.at[a, k],
                device_id=to, device_id_type=_MESHID)

        mine = [pltpu.make_async_copy(ins[a], outs[a].at[slot(me)], local_sems.at[a]) for a in range(n)]
        for cp in mine:
            cp.start()
        first = []
        for a in range(n):
            first.append(copy(a, 0, me, sibling, src=ins[a]))
            first += [copy(a, 1 + j, me, (*chip, c), src=ins[a]) for j, chip in enumerate(chips)]
        for cp in first:
            cp.start()
        passed = []
        for j, chip in enumerate(chips):
            for a in range(n):
                copy(a, 1 + j, (*chip, c), me).wait_recv()
                fwd = copy(a, 4 + j, (*chip, c), sibling)
                fwd.start()
                passed.append(fwd)
        for a in range(n):
            copy(a, 0, sibling, me).wait_recv()
            for j, chip in enumerate(chips):
                copy(a, 4 + j, (*chip, 1 - c), me).wait_recv()
        for cp in first + passed:
            cp.wait_send()
        for cp in mine:
            cp.wait()

    anyspec = pl.BlockSpec(memory_space=pl.ANY)
    return pl.pallas_call(
        body, name=name,
        in_specs=[anyspec] * n, out_specs=[anyspec] * n,
        out_shape=[jax.ShapeDtypeStruct((NDEV,) + a.shape, a.dtype) for a in arrs],
        scratch_shapes=[pltpu.SemaphoreType.DMA((n, 7)), pltpu.SemaphoreType.DMA((n, 7)),
                        pltpu.SemaphoreType.DMA((n,))],
    )(*arrs)


def exchange(blocks, rep, name):
    n = len(blocks)

    def body(*refs):
        b_refs, r_ref = refs[:n], refs[n]
        ob_refs, or_ref = refs[n + 1:2 * n + 1], refs[2 * n + 1]
        send_sems, recv_sems, local_sems = refs[2 * n + 2:]
        x, y, c = lax.axis_index("x"), lax.axis_index("y"), lax.axis_index("c")
        me = 4 * x + 2 * y + c

        def pairs(src_slot, dst_slot):
            return [(b_refs[a].at[src_slot], ob_refs[a].at[dst_slot]) for a in range(n)] + [(r_ref, or_ref.at[dst_slot])]

        loc = [pltpu.make_async_copy(s, d, local_sems.at[a]) for a, (s, d) in enumerate(pairs(me, me))]
        for cp in loc:
            cp.start()
        sends = []
        for k, (fx, fy, fc) in enumerate(_RELS):
            peer = (_flip(x, fx), _flip(y, fy), _flip(c, fc))
            pid = 4 * peer[0] + 2 * peer[1] + peer[2]
            for a, (s, d) in enumerate(pairs(pid, me)):
                sends.append(pltpu.make_async_remote_copy(
                    src_ref=s, dst_ref=d, send_sem=send_sems.at[a, k], recv_sem=recv_sems.at[a, k],
                    device_id=peer, device_id_type=_MESHID))
        for cp in sends:
            cp.start()
        for k, (fx, fy, fc) in enumerate(_RELS):
            peer = (_flip(x, fx), _flip(y, fy), _flip(c, fc))
            pid = 4 * peer[0] + 2 * peer[1] + peer[2]
            for a, (s, d) in enumerate(pairs(pid, pid)):
                pltpu.make_async_remote_copy(
                    src_ref=s, dst_ref=d, send_sem=send_sems.at[a, k], recv_sem=recv_sems.at[a, k],
                    device_id=peer, device_id_type=_MESHID).wait_recv()
        for cp in sends:
            cp.wait_send()
        for cp in loc:
            cp.wait()

    anyspec = pl.BlockSpec(memory_space=pl.ANY)
    return pl.pallas_call(
        body, name=name,
        in_specs=[anyspec] * (n + 1), out_specs=[anyspec] * (n + 1),
        out_shape=[jax.ShapeDtypeStruct(b.shape, b.dtype) for b in blocks]
        + [jax.ShapeDtypeStruct((NDEV,) + rep.shape, rep.dtype)],
        scratch_shapes=[pltpu.SemaphoreType.DMA((n + 1, 7)), pltpu.SemaphoreType.DMA((n + 1, 7)),
                        pltpu.SemaphoreType.DMA((n + 1,))],
    )(*blocks, rep)


_HBM = pl.BlockSpec(memory_space=pltpu.HBM)
_SEM = pl.BlockSpec(memory_space=pltpu.SEMAPHORE)
_EFFECT = pltpu.SideEffectType.DATAFLOW_SIDE_EFFECTING


def _peer_ids():
    x, y, c = lax.axis_index("x"), lax.axis_index("y"), lax.axis_index("c")
    peers = []
    for fx, fy, fc in _RELS:
        p = (_flip(x, fx), _flip(y, fy), _flip(c, fc))
        peers.append((p, 4 * p[0] + 2 * p[1] + p[2]))
    return 4 * x + 2 * y + c, peers


def _split_copy(src, land, a, k, peer, src_slot, dst_slot, send_sems, recv_sems):
    return pltpu.make_async_remote_copy(
        src_ref=src if src_slot is None else src.at[src_slot], dst_ref=land.at[dst_slot],
        send_sem=send_sems.at[7 * a + k], recv_sem=recv_sems.at[7 * a + k], device_id=peer, device_id_type=_MESHID)


def _own_copy(src, land, a, n, me, per_peer, send_sems):
    return pltpu.make_async_copy(src.at[me] if per_peer else src, land.at[me], send_sems.at[7 * n + a])


def send_start(srcs, per_peer, order, name):
    n = len(srcs)
    lands = [lax.empty((NDEV,) + (s.shape[1:] if per_peer else s.shape), s.dtype) for s in srcs]

    def body(*refs):
        src_refs, land_refs = refs[1:1 + n], refs[1 + n:1 + 2 * n]
        send_sems, recv_sems = refs[1 + 2 * n], refs[2 + 2 * n]
        token = refs[3 + 4 * n]
        me, peers = _peer_ids()
        for a in range(n):
            _own_copy(src_refs[a], land_refs[a], a, n, me, per_peer, send_sems).start()
        for a in range(n):
            for k, (peer, pid) in enumerate(peers):
                _split_copy(src_refs[a], land_refs[a], a, k, peer, pid if per_peer else None, me,
                            send_sems, recv_sems).start()
        token[...] = jnp.zeros_like(token)

    outs = pl.pallas_call(
        body, name=name,
        in_specs=[pl.BlockSpec(memory_space=pl.ANY)] + [_HBM] * (2 * n),
        out_shape=(pltpu.SemaphoreType.DMA((8 * n,)), pltpu.SemaphoreType.DMA((7 * n,)),
                   *[pltpu.HBM(s.shape, s.dtype) for s in srcs], *[pltpu.HBM(l.shape, l.dtype) for l in lands],
                   jax.ShapeDtypeStruct((8, 128), F32)),
        out_specs=(_SEM, _SEM, *[_HBM] * (2 * n), pl.BlockSpec(memory_space=pltpu.VMEM)),
        input_output_aliases={1 + i: 2 + i for i in range(2 * n)},
        compiler_params=pltpu.CompilerParams(has_side_effects=_EFFECT),
    )(order, *[pltpu.with_memory_space_constraint(s, pltpu.HBM) for s in srcs],
      *[pltpu.with_memory_space_constraint(l, pltpu.HBM) for l in lands])
    return (n, per_peer, outs[0], outs[1], outs[2:2 + n], outs[2 + n:2 + 2 * n]), outs[2 + 2 * n]


def send_wait(handle, after, name):
    n, per_peer, send_sems, recv_sems, src_thru, land_thru = handle

    def body(*refs):
        src_refs, land_refs = refs[:n], refs[n:2 * n]
        s_sems, r_sems = refs[2 * n], refs[2 * n + 1]
        me, peers = _peer_ids()
        for a in range(n):
            _own_copy(src_refs[a], land_refs[a], a, n, me, per_peer, s_sems).wait()
            for k, (peer, pid) in enumerate(peers):
                cp = _split_copy(src_refs[a], land_refs[a], a, k, peer, pid if per_peer else None, pid, s_sems, r_sems)
                cp.wait_send()
                cp.wait_recv()

    outs = pl.pallas_call(
        body, name=name,
        in_specs=[_HBM] * (2 * n) + [_SEM, _SEM, pl.BlockSpec(memory_space=pl.ANY)],
        out_shape=tuple(pltpu.HBM(t.shape, t.dtype) for t in (*src_thru, *land_thru)),
        out_specs=tuple([_HBM] * (2 * n)),
        input_output_aliases={i: i for i in range(2 * n)},
        compiler_params=pltpu.CompilerParams(has_side_effects=_EFFECT),
    )(*src_thru, *land_thru, send_sems, recv_sems, after)
    return list(outs[n:2 * n])


def adamw(parts, w, m, v, name, tr):
    npart, r, c = parts.shape
    c1 = 1.0 - ADAM_B1 ** ADAM_STEP
    c2 = 1.0 - ADAM_B2 ** ADAM_STEP

    def body(p_ref, w_ref, m_ref, v_ref, g_ref, d_ref, nm_ref, nv_ref):
        g = p_ref[0].astype(F32)
        for j in range(1, npart):
            g = g + p_ref[j].astype(F32)
        mn = ADAM_B1 * m_ref[...] + (1.0 - ADAM_B1) * g
        vn = ADAM_B2 * v_ref[...] + (1.0 - ADAM_B2) * (g * g)
        g_ref[...] = g
        nm_ref[...] = mn
        nv_ref[...] = vn
        d_ref[...] = -ADAM_LR * ((mn / c1) / (jnp.sqrt(vn / c2) + ADAM_EPS) + ADAM_WD * w_ref[...])

    spec = _rb(tr, c, 0)
    return pl.pallas_call(
        body, name=name, grid=(r // tr,),
        in_specs=[pl.BlockSpec((npart, tr, c), lambda i: (0, i, 0)), spec, spec, spec],
        out_specs=[spec] * 4, out_shape=[jax.ShapeDtypeStruct((r, c), F32)] * 4,
        compiler_params=_cp(("parallel",)),
    )(parts, w, m, v)


def sum_parts(parts, name, tc):
    npart, r, c = parts.shape

    def body(p_ref, o_ref):
        g = p_ref[0].astype(F32)
        for j in range(1, npart):
            g = g + p_ref[j].astype(F32)
        o_ref[...] = g

    return pl.pallas_call(
        body, name=name, grid=(c // tc,),
        in_specs=[pl.BlockSpec((npart, r, tc), lambda i: (0, 0, i))],
        out_specs=pl.BlockSpec((r, tc), lambda i: (0, i)),
        out_shape=jax.ShapeDtypeStruct((r, c), F32),
        compiler_params=_cp(("parallel",)),
    )(parts)


TINY = [("meta_tokens", (16, 1024)), ("conv_w", (4, 1024)), ("g_a2", (16, 512)), ("m_head_g", (4, 256)),
        ("g_head_g", (4, 256))]
REPL = [("norm1_g", (1, 1024)), ("conv_b", (1, 1024)), ("m_gate_b", (1, 2, 4)), ("g_a2_b", (1, 512)),
        ("norm2_g", (1, 1024)), ("final_g", (1024,))]
TINY_SIZE = 16 * 1024 + 4 * 1024 + 16 * 512 + 2 * 4 * 256
REPL_SIZE = 1024 + 1024 + 8 + 512 + 1024 + 1024
ROWS_GATHER = 8
ROWS_REP = 40
ROWS_OWN = 16


def pack_rows(vecs, rows):
    flat = jnp.concatenate([v.reshape(-1) for v in vecs])
    return jnp.pad(flat, (0, rows * 1024 - flat.shape[0])).reshape(rows, 1024)


def unpack_rows(packed, shapes):
    flat = packed.reshape(-1)
    out, off = [], 0
    for s in shapes:
        n = 1
        for d in s:
            n *= d
        out.append(flat[off:off + n].reshape(s))
        off += n
    return out


def kernel(x, meta_tokens, norm1_g, w_in, conv_w, conv_b, m_gate_b, g_a2, g_a2_b, m_head_g, g_head_g, w_branch_m, w_branch_g, w_out, norm2_g, w_ff_gate, w_ff_up, w_ff_down, final_g, loss_target, m_meta_tokens, m_norm1_g, m_w_in, m_conv_w, m_conv_b, m_m_gate_b, m_g_a2, m_g_a2_b, m_m_head_g, m_g_head_g, m_w_branch_m, m_w_branch_g, m_w_out, m_norm2_g, m_w_ff_gate, m_w_ff_up, m_w_ff_down, m_final_g, v_meta_tokens, v_norm1_g, v_w_in, v_conv_w, v_conv_b, v_m_gate_b, v_g_a2, v_g_a2_b, v_m_head_g, v_g_head_g, v_w_branch_m, v_w_branch_g, v_w_out, v_norm2_g, v_w_ff_gate, v_w_ff_up, v_w_ff_down, v_final_g):
    w_sh = dict(meta_tokens=meta_tokens, w_in=w_in[0], conv_w=conv_w[0], g_a2=g_a2[0], m_head_g=m_head_g[0],
                g_head_g=g_head_g[0], w_branch_m=w_branch_m[0], w_branch_g=w_branch_g[0], w_out=w_out[0],
                w_ff_gate=w_ff_gate[0], w_ff_up=w_ff_up[0], w_ff_down=w_ff_down[0])
    m_sh = dict(meta_tokens=m_meta_tokens, w_in=m_w_in[0], conv_w=m_conv_w[0], g_a2=m_g_a2[0],
                m_head_g=m_m_head_g[0], g_head_g=m_g_head_g[0], w_branch_m=m_w_branch_m[0],
                w_branch_g=m_w_branch_g[0], w_out=m_w_out[0], w_ff_gate=m_w_ff_gate[0], w_ff_up=m_w_ff_up[0],
                w_ff_down=m_w_ff_down[0])
    v_sh = dict(meta_tokens=v_meta_tokens, w_in=v_w_in[0], conv_w=v_conv_w[0], g_a2=v_g_a2[0],
                m_head_g=v_m_head_g[0], g_head_g=v_g_head_g[0], w_branch_m=v_w_branch_m[0],
                w_branch_g=v_w_branch_g[0], w_out=v_w_out[0], w_ff_gate=v_w_ff_gate[0], w_ff_up=v_w_ff_up[0],
                w_ff_down=v_w_ff_down[0])
    w_rep = dict(norm1_g=norm1_g, conv_b=conv_b, m_gate_b=m_gate_b, g_a2_b=g_a2_b, norm2_g=norm2_g, final_g=final_g)
    m_rep = dict(norm1_g=m_norm1_g, conv_b=m_conv_b, m_gate_b=m_m_gate_b, g_a2_b=m_g_a2_b, norm2_g=m_norm2_g,
                 final_g=m_final_g)
    v_rep = dict(norm1_g=v_norm1_g, conv_b=v_conv_b, m_gate_b=v_m_gate_b, g_a2_b=v_g_a2_b, norm2_g=v_norm2_g,
                 final_g=v_final_g)
    dev = 4 * lax.axis_index("x") + 2 * lax.axis_index("y") + lax.axis_index("c")
    tiny_names = [n for n, _ in TINY]
    repl_names = [n for n, _ in REPL]
    tiny_shard_shapes = [(s[0], s[1] // NDEV) for _, s in TINY]

    in8, tiny8 = all_gather([w_sh["w_in"].astype(MXU), pack_rows([w_sh[n] for n in tiny_names], ROWS_GATHER)],
                            "param_all_gather")
    late_names = ["w_branch_m", "w_branch_g", "w_out", "w_ff_gate", "w_ff_up", "w_ff_down"]
    late, first_order = send_start([w_sh[n].astype(MXU) for n in late_names], False, tiny8, "late_weights_start")
    wp = regroup_cols(in8)
    handles = {}

    def late_weights(after):
        bm8, bg8, out8, ffg8, ffu8, ffd8 = send_wait(late, after, "late_weights_wait")
        w_gu = interleave_gu(from_col_blocks(ffg8), from_col_blocks(ffu8))
        return bm8.reshape(D, D), bg8.reshape(D, D), out8.reshape(D, D), w_gu, ffd8.reshape(DFF, D)

    def send_early(g):
        d_gate, d_up = split_gu(g["w_gu"])
        blocks = [g["w_branch_m"].reshape(NDEV, D // NDEV, D).astype(WIRE),
                  g["w_branch_g"].reshape(NDEV, D // NDEV, D).astype(WIRE),
                  g["w_out"].reshape(NDEV, D // NDEV, D).astype(WIRE),
                  col_blocks(d_gate).astype(WIRE), col_blocks(d_up).astype(WIRE),
                  g["w_ff_down"].reshape(NDEV, DFF // NDEV, D).astype(WIRE)]
        handles["early"], token = send_start(blocks, True, blocks[0], "early_grads_start")
        return token

    def send_wp(d_wp):
        blocks = [ungroup_cols(d_wp).astype(WIRE)]
        handles["wp"], token = send_start(blocks, True, blocks[0], "proj_grads_start")
        return token

    tiny_full = {}
    for j in range(NDEV):
        for name, blk in zip(tiny_names, unpack_rows(tiny8[j], tiny_shard_shapes)):
            tiny_full.setdefault(name, []).append(blk)
    tiny_full = {n: jnp.concatenate(v, axis=1) for n, v in tiny_full.items()}

    loss, grad_x, g = local_step(
        x[0], loss_target[0], tiny_full["meta_tokens"], norm1_g, wp, tiny_full["conv_w"], conv_b, m_gate_b[0],
        tiny_full["g_a2"], g_a2_b, tiny_full["m_head_g"], tiny_full["g_head_g"], norm2_g, final_g,
        late_weights, send_early, send_wp, first_order)

    rep = pack_rows([g[n] for n in tiny_names + repl_names] + [loss[0, 0:1]], ROWS_REP)
    (got_rep,) = exchange([], rep, "small_grad_exchange")
    got_early = send_wait(handles["early"], got_rep, "early_grads_wait")
    (got_wp,) = send_wait(handles["wp"], got_rep, "proj_grads_wait")

    result = {}

    def update(name, parts, tr):
        outs = adamw(parts, w_sh[name], m_sh[name], v_sh[name], "adamw_" + name, tr)
        for kind, arr in zip(("grad", "delta", "new_m", "new_v"), outs):
            result[kind, name] = arr[None]

    update("w_in", got_wp, 128)
    update("w_branch_m", got_early[0], 128)
    update("w_branch_g", got_early[1], 128)
    update("w_out", got_early[2], 128)
    update("w_ff_gate", got_early[3], 256)
    update("w_ff_up", got_early[4], 256)
    update("w_ff_down", got_early[5], DFF // NDEV)

    rep_sum = sum_parts(got_rep, "sum_small", 1024)
    rep_g = unpack_rows(rep_sum, [s for _, s in TINY] + [s for _, s in REPL] + [(1,)])
    own_g = [lax.dynamic_slice_in_dim(gf, dev * ss[1], ss[1], axis=1) for gf, ss in zip(rep_g, tiny_shard_shapes)]
    own_g += rep_g[len(TINY):len(TINY) + len(REPL)]
    w_all = {**w_sh, **w_rep}
    m_all = {**m_sh, **m_rep}
    v_all = {**v_sh, **v_rep}
    names = tiny_names + repl_names
    outs = adamw(pack_rows(own_g, ROWS_OWN)[None], pack_rows([w_all[n] for n in names], ROWS_OWN),
                 pack_rows([m_all[n] for n in names], ROWS_OWN), pack_rows([v_all[n] for n in names], ROWS_OWN),
                 "adamw_small", ROWS_OWN)
    shapes = tiny_shard_shapes + [s for _, s in REPL]
    for kind, packed in zip(("grad", "delta", "new_m", "new_v"), outs):
        for name, arr in zip(names, unpack_rows(packed, shapes)):
            result[kind, name] = arr[None] if name in tiny_names and name != "meta_tokens" else arr
    loss_total = rep_g[-1][0]
    order = ["meta_tokens", "norm1_g", "w_in", "conv_w", "conv_b", "m_gate_b", "g_a2", "g_a2_b", "m_head_g", "g_head_g",
             "w_branch_m", "w_branch_g", "w_out", "norm2_g", "w_ff_gate", "w_ff_up", "w_ff_down", "final_g"]
    return (loss_total, grad_x[None], *[result[kind, n] for kind in ("grad", "delta", "new_m", "new_v") for n in order])
```

```python
import functools

import jax
import jax.numpy as jnp
from jax import lax
from jax.experimental import pallas as pl
from jax.experimental.pallas import tpu as pltpu

F32 = jnp.float32
MXU = jnp.bfloat16
WIRE = jnp.bfloat16

D = 1024
NH = 4
DV = 256
DQK = 128
L = 64
NMETA = 16
PADR = 512
LM = 256
CH0 = PADR // LM - 1
NPADROWS = PADR - NMETA
RANK = 16
DFF = 2816
EPS = 1e-6
TAU = 16.0
QSCALE = DQK ** -0.5
NEG = -1e30
NDEV = 8

MV0, MO0, GQ0, GK0, GV0, GR0, QK0, GM0, GG0, SM0 = 0, 1024, 2048, 2560, 3072, 4096, 5120, 6144, 7168, 8192
NP = 8320
NPROJ = 8216

ADAM_LR, ADAM_B1, ADAM_B2, ADAM_EPS, ADAM_WD, ADAM_STEP = 0.001, 0.9, 0.999, 1e-08, 0.01, 10

VMEM_LIMIT = 56 * 1024 * 1024
TM = 512


def _cp(sem):
    return pltpu.CompilerParams(dimension_semantics=sem, vmem_limit_bytes=VMEM_LIMIT)


def _sigmoid(x):
    return 1.0 / (1.0 + jnp.exp(-x))


def _log_sigmoid(x):
    return jnp.minimum(x, 0.0) - jnp.log1p(jnp.exp(-jnp.abs(x)))


def _dot(a, b, ca, cb):
    return lax.dot_general(a.astype(MXU), b.astype(MXU), (((ca,), (cb,)), ((), ())), preferred_element_type=F32)


def _dot_exact(a, b):
    return lax.dot_general(a, b, (((1,), (0,)), ((), ())), precision=lax.Precision.HIGHEST,
                           preferred_element_type=F32)


def _rb(tm, w, cb):
    return pl.BlockSpec((tm, w), lambda i: (i, cb))


def _const(shape):
    nd = len(shape)
    return pl.BlockSpec(shape, lambda i: (0,) * nd)


def _pick(n, target):
    if n <= target:
        return n
    best = None
    for t in range(128, target + 1, 128):
        if n % t == 0:
            best = t
    assert best is not None, (n, target)
    return best


def matmul(a, b, mode, name, add=None, out_dtype=F32, tm=512, tn=1664, tk=1024, order=None):
    if mode == "nn":
        (M, K), (K2, N) = a.shape, b.shape
    elif mode == "nt":
        (M, K), (N, K2) = a.shape, b.shape
    else:
        (K, M), (K2, N) = a.shape, b.shape
    assert K == K2, (a.shape, b.shape, mode)
    tm, tn, tk = _pick(M, tm), _pick(N, tn), _pick(K, tk)
    nk = K // tk
    assert nk == 1 or out_dtype == F32
    ca, cb = {"nn": (1, 0), "nt": (1, 1), "tn": (0, 0)}[mode]
    a_spec = {"nn": pl.BlockSpec((tm, tk), lambda j, i, k: (i, k)),
              "nt": pl.BlockSpec((tm, tk), lambda j, i, k: (i, k)),
              "tn": pl.BlockSpec((tk, tm), lambda j, i, k: (k, i))}[mode]
    b_spec = {"nn": pl.BlockSpec((tk, tn), lambda j, i, k: (k, j)),
              "nt": pl.BlockSpec((tn, tk), lambda j, i, k: (j, k)),
              "tn": pl.BlockSpec((tk, tn), lambda j, i, k: (k, j))}[mode]
    o_spec = pl.BlockSpec((tm, tn), lambda j, i, k: (i, j))
    has_add = add is not None

    def body(*refs):
        if order is not None:
            refs = refs[:-2] + refs[-1:]
        if has_add:
            a_ref, b_ref, add_ref, o_ref = refs
        else:
            a_ref, b_ref, o_ref = refs
            add_ref = None
        part = _dot(a_ref[...], b_ref[...], ca, cb)
        if nk == 1:
            if has_add:
                part = part + add_ref[...]
            o_ref[...] = part.astype(o_ref.dtype)
            return
        k = pl.program_id(2)

        @pl.when(k == 0)
        def _():
            o_ref[...] = part + add_ref[...] if has_add else part

        @pl.when(k > 0)
        def _():
            o_ref[...] += part

    in_specs = [a_spec, b_spec] + ([o_spec] if has_add else [])
    args = (a, b) + ((add,) if has_add else ())
    if order is not None:
        in_specs.append(pl.BlockSpec(order.shape, lambda j, i, k: (0, 0)))
        args += (order,)
    return pl.pallas_call(
        body, name=name, grid=(N // tn, M // tm, nk),
        in_specs=in_specs, out_specs=o_spec,
        out_shape=jax.ShapeDtypeStruct((M, N), out_dtype),
        compiler_params=_cp(("parallel", "parallel", "arbitrary")),
    )(*args)


def _h0_tile(i, x_ref, meta_ref):
    assert PADR == TM
    front = jnp.concatenate([jnp.zeros((NPADROWS, D), F32), meta_ref[...]], axis=0)
    return jnp.where(i == 0, front, x_ref[...])


_TOKENS = pl.BlockSpec((TM, D), lambda i: (jnp.maximum(i - 1, 0), 0))


def rms_fwd_input(x, meta, g, name):
    R = x.shape[0] + PADR

    def body(x_ref, meta_ref, g_ref, y_ref):
        xv = _h0_tile(pl.program_id(0), x_ref, meta_ref)
        r = lax.rsqrt(jnp.mean(xv * xv, axis=-1, keepdims=True) + EPS)
        y_ref[...] = (xv * r * g_ref[...]).astype(y_ref.dtype)

    return pl.pallas_call(
        body, name=name, grid=(R // TM,),
        in_specs=[_TOKENS, _const((NMETA, D)), _const((1, D))], out_specs=_rb(TM, D, 0),
        out_shape=jax.ShapeDtypeStruct((R, D), MXU), compiler_params=_cp(("parallel",)),
    )(x, meta, g)


def dgrad_rms_bwd(a, w, x, g, dres, name, tm, tk):
    R, K = a.shape
    tm, tk = _pick(R, tm), _pick(K, tk)
    nk = K // tk

    def body(a_ref, w_ref, x_ref, g_ref, dres_ref, dx_ref, dg_ref):
        i, k = pl.program_id(0), pl.program_id(1)
        part = _dot(a_ref[...], w_ref[...], 1, 1)

        @pl.when(k == 0)
        def _():
            dx_ref[...] = part

        @pl.when(k > 0)
        def _():
            dx_ref[...] += part

        @pl.when(k == nk - 1)
        def _():
            xv, dyv = x_ref[...], dx_ref[...]
            r = lax.rsqrt(jnp.mean(xv * xv, axis=-1, keepdims=True) + EPS)
            dyg = dyv * g_ref[...]
            dx_ref[...] = dres_ref[...] + r * dyg - xv * (r * r * r * jnp.mean(dyg * xv, axis=-1, keepdims=True))
            gpart = jnp.sum(dyv * xv * r, axis=0, keepdims=True)

            @pl.when(i == 0)
            def _():
                dg_ref[...] = gpart

            @pl.when(i > 0)
            def _():
                dg_ref[...] += gpart

    row = pl.BlockSpec((tm, D), lambda i, k: (i, 0))
    return pl.pallas_call(
        body, name=name, grid=(R // tm, nk),
        in_specs=[pl.BlockSpec((tm, tk), lambda i, k: (i, k)), pl.BlockSpec((D, tk), lambda i, k: (0, k)), row,
                  pl.BlockSpec((1, D), lambda i, k: (0, 0)), row],
        out_specs=[row, pl.BlockSpec((1, D), lambda i, k: (0, 0))],
        out_shape=[jax.ShapeDtypeStruct((R, D), F32), jax.ShapeDtypeStruct((1, D), F32)],
        compiler_params=_cp(("arbitrary", "arbitrary")),
    )(a, w, x, g, dres)


def rms_bwd_input(dy, x, meta, g, dres, name):
    R = dy.shape[0]

    def body(dy_ref, x_ref, meta_ref, g_ref, dres_ref, dx_ref, dmeta_ref, dg_ref):
        i = pl.program_id(0)
        xv, dyv = _h0_tile(i, x_ref, meta_ref), dy_ref[...]
        r = lax.rsqrt(jnp.mean(xv * xv, axis=-1, keepdims=True) + EPS)
        dyg = dyv * g_ref[...]
        dx = dres_ref[...] + r * dyg - xv * (r * r * r * jnp.mean(dyg * xv, axis=-1, keepdims=True))
        dx_ref[...] = dx
        part = jnp.sum(dyv * xv * r, axis=0, keepdims=True)

        @pl.when(i == 0)
        def _():
            dg_ref[...] = part
            dmeta_ref[...] = dx[NPADROWS:PADR]

        @pl.when(i > 0)
        def _():
            dg_ref[...] += part

    return pl.pallas_call(
        body, name=name, grid=(R // TM,),
        in_specs=[_rb(TM, D, 0), _TOKENS, _const((NMETA, D)), _const((1, D)), _rb(TM, D, 0)],
        out_specs=[_TOKENS, _const((NMETA, D)), _const((1, D))],
        out_shape=[jax.ShapeDtypeStruct((R - PADR, D), F32), jax.ShapeDtypeStruct((NMETA, D), F32),
                   jax.ShapeDtypeStruct((1, D), F32)],
        compiler_params=_cp(("arbitrary",)),
    )(dy, x, meta, g, dres)


def _shift_down(cur, prev8, s):
    tm = cur.shape[0]
    rolled = pltpu.roll(cur, s, 0)
    rows8 = lax.broadcasted_iota(jnp.int32, (8, cur.shape[1]), 0)
    head = jnp.where(rows8 < s, pltpu.roll(prev8, s, 0), rolled[0:8])
    return jnp.concatenate([head, rolled[8:tm]], axis=0)


def _shift_up(cur, next8, s):
    tm = cur.shape[0]
    rolled = pltpu.roll(cur, tm - s, 0)
    rows8 = lax.broadcasted_iota(jnp.int32, (8, cur.shape[1]), 0)
    tail = jnp.where(rows8 >= 8 - s, pltpu.roll(next8, 8 - s, 0), rolled[tm - 8:tm])
    return jnp.concatenate([rolled[0:tm - 8], tail], axis=0)


def prep_fwd(proj, conv_w, conv_b, gb_row):
    R = proj.shape[0]
    t8 = TM // 8

    def body(x_ref, halo_ref, sm_ref, w_ref, b_ref, gb_ref, c_ref, qk_ref, gl_ref):
        i = pl.program_id(0)
        x = x_ref[...]
        halo = halo_ref[...]
        w = w_ref[...]
        c = x * w[3:4, :] + b_ref[...]
        for s in (1, 2, 3):
            c = c + _shift_down(x, halo, s) * w[3 - s:4 - s, :]
        c_ref[...] = c
        qk_ref[...] = c * _sigmoid(c)
        z = sm_ref[...] + gb_ref[...]
        lane = lax.broadcasted_iota(jnp.int32, z.shape, 1)
        row = lax.broadcasted_iota(jnp.int32, z.shape, 0) + i * TM
        valid = row >= NPADROWS
        logi = jnp.where(valid, z, NEG)
        logf = jnp.where(valid, _log_sigmoid(z), 0.0)
        gl_ref[...] = jnp.where(lane < 4, logi, jnp.where(lane < 8, logf, 0.0))

    return pl.pallas_call(
        body, name="prep_fwd", grid=(R // TM,),
        in_specs=[_rb(TM, 1024, QK0 // 1024),
                  pl.BlockSpec((8, 1024), lambda i: (jnp.maximum(i * t8 - 1, 0), QK0 // 1024)),
                  _rb(TM, 128, SM0 // 128), _const((4, 1024)), _const((1, 1024)), _const((1, 128))],
        out_specs=[_rb(TM, 1024, 0), _rb(TM, 1024, 0), _rb(TM, 128, 0)],
        out_shape=[jax.ShapeDtypeStruct((R, 1024), F32), jax.ShapeDtypeStruct((R, 1024), F32),
                   jax.ShapeDtypeStruct((R, 128), F32)],
        compiler_params=_cp(("parallel",)),
    )(proj, proj, proj, conv_w, conv_b, gb_row)


def merge_fwd(ym, yg, w_bm, w_bg, proj):
    R = ym.shape[0]

    def body(ym_ref, yg_ref, wm_ref, wg_ref, gm_ref, gg_ref, bm_ref, bg_ref, o_ref):
        bm = _dot(ym_ref[...], wm_ref[...], 1, 0)
        bg = _dot(yg_ref[...], wg_ref[...], 1, 0)
        bm_ref[...] = bm
        bg_ref[...] = bg
        o_ref[...] = (_sigmoid(gm_ref[...]) * bm + _sigmoid(gg_ref[...]) * bg).astype(o_ref.dtype)

    return pl.pallas_call(
        body, name="merge_fwd", grid=(R // TM,),
        in_specs=[_rb(TM, D, 0), _rb(TM, D, 0), _const((D, D)), _const((D, D)), _rb(TM, D, GM0 // D),
                  _rb(TM, D, GG0 // D)],
        out_specs=[_rb(TM, D, 0), _rb(TM, D, 0), _rb(TM, D, 0)],
        out_shape=[jax.ShapeDtypeStruct((R, D), F32), jax.ShapeDtypeStruct((R, D), F32),
                   jax.ShapeDtypeStruct((R, D), MXU)],
        compiler_params=_cp(("parallel",)),
    )(ym, yg, w_bm, w_bg, proj, proj)


def merge_bwd(dh1, w_out, bm, bg, proj):
    R = bm.shape[0]

    def body(dh_ref, w_ref, bm_ref, bg_ref, gm_ref, gg_ref, dbm_ref, dbg_ref, dp_ref):
        dm = _dot(dh_ref[...], w_ref[...], 1, 1)
        sm, sg = _sigmoid(gm_ref[...]), _sigmoid(gg_ref[...])
        dbm_ref[...] = (dm * sm).astype(dbm_ref.dtype)
        dbg_ref[...] = (dm * sg).astype(dbg_ref.dtype)
        dp_ref[:, 0:D] = (dm * bm_ref[...] * sm * (1.0 - sm)).astype(dp_ref.dtype)
        dp_ref[:, D:2 * D] = (dm * bg_ref[...] * sg * (1.0 - sg)).astype(dp_ref.dtype)

    return pl.pallas_call(
        body, name="merge_bwd", grid=(R // TM,),
        in_specs=[_rb(TM, D, 0), _const((D, D)), _rb(TM, D, 0), _rb(TM, D, 0), _rb(TM, D, GM0 // D),
                  _rb(TM, D, GG0 // D)],
        out_specs=[_rb(TM, D, 0), _rb(TM, D, 0), _rb(TM, 2 * D, GM0 // (2 * D))],
        out_shape=[jax.ShapeDtypeStruct((R, D), MXU), jax.ShapeDtypeStruct((R, D), MXU),
                   jax.ShapeDtypeStruct((R, NP), MXU)],
        compiler_params=_cp(("parallel",)),
    )(dh1, w_out, bm, bg, proj, proj)


TF = DFF // 2
TMF = 768


def interleave_gu(gate, up):
    return jnp.concatenate([gate[:, :TF], up[:, :TF], gate[:, TF:], up[:, TF:]], axis=1)


def split_gu(gu):
    return (jnp.concatenate([gu[:, 0:TF], gu[:, 2 * TF:3 * TF]], axis=1),
            jnp.concatenate([gu[:, TF:2 * TF], gu[:, 3 * TF:]], axis=1))


def ff_in_fwd(hn, w_gu):
    R = hn.shape[0]

    def body(x_ref, w_ref, au_ref, ff_ref):
        au = _dot(x_ref[...], w_ref[...], 1, 0)
        au_ref[...] = au.astype(au_ref.dtype)
        a = au[:, :TF]
        ff_ref[...] = (a * _sigmoid(a) * au[:, TF:]).astype(ff_ref.dtype)

    tm = _pick(R, TMF)
    return pl.pallas_call(
        body, name="ff_in_fwd", grid=(DFF // TF, R // tm),
        in_specs=[pl.BlockSpec((tm, D), lambda j, i: (i, 0)), pl.BlockSpec((D, 2 * TF), lambda j, i: (0, j))],
        out_specs=[pl.BlockSpec((tm, 2 * TF), lambda j, i: (i, j)), pl.BlockSpec((tm, TF), lambda j, i: (i, j))],
        out_shape=[jax.ShapeDtypeStruct((R, 2 * DFF), MXU), jax.ShapeDtypeStruct((R, DFF), MXU)],
        compiler_params=_cp(("parallel", "parallel")),
    )(hn, w_gu)


def ff_down_dgrad(dh2, w_down, au):
    R = dh2.shape[0]

    def body(d_ref, w_ref, au_ref, o_ref):
        dff = _dot(d_ref[...], w_ref[...], 1, 1)
        a = au_ref[:, :TF].astype(F32)
        u = au_ref[:, TF:].astype(F32)
        s = _sigmoid(a)
        o_ref[:, :TF] = (dff * u * s * (1.0 + a * (1.0 - s))).astype(o_ref.dtype)
        o_ref[:, TF:] = (dff * a * s).astype(o_ref.dtype)

    tm = _pick(R, TMF)
    return pl.pallas_call(
        body, name="ff_down_dgrad", grid=(DFF // TF, R // tm),
        in_specs=[pl.BlockSpec((tm, D), lambda j, i: (i, 0)), pl.BlockSpec((TF, D), lambda j, i: (j, 0)),
                  pl.BlockSpec((tm, 2 * TF), lambda j, i: (i, j))],
        out_specs=pl.BlockSpec((tm, 2 * TF), lambda j, i: (i, j)),
        out_shape=jax.ShapeDtypeStruct((R, 2 * DFF), MXU),
        compiler_params=_cp(("parallel", "parallel")),
    )(dh2, w_down, au)


def out_proj_norm(merged, w, x, meta, g):
    R = merged.shape[0]

    def body(m_ref, w_ref, x_ref, meta_ref, g_ref, h_ref, n_ref):
        hv = _dot(m_ref[...], w_ref[...], 1, 0) + _h0_tile(pl.program_id(0), x_ref, meta_ref)
        h_ref[...] = hv
        r = lax.rsqrt(jnp.mean(hv * hv, axis=-1, keepdims=True) + EPS)
        n_ref[...] = (hv * r * g_ref[...]).astype(n_ref.dtype)

    return pl.pallas_call(
        body, name="out_fwd", grid=(R // TM,),
        in_specs=[_rb(TM, D, 0), _const((D, D)), _TOKENS, _const((NMETA, D)), _const((1, D))],
        out_specs=[_rb(TM, D, 0), _rb(TM, D, 0)],
        out_shape=[jax.ShapeDtypeStruct((R, D), F32), jax.ShapeDtypeStruct((R, D), MXU)],
        compiler_params=_cp(("parallel",)),
    )(merged, w, x, meta, g)


def ff_down_loss(ff, w_down, h1, gf, target):
    R = ff.shape[0]
    assert PADR == TM

    def body(f_ref, w_ref, h1_ref, g_ref, t_ref, dh_ref, loss_ref, dg_ref):
        i = pl.program_id(0)
        hv = _dot(f_ref[...], w_ref[...], 1, 0) + h1_ref[...]
        r = lax.rsqrt(jnp.mean(hv * hv, axis=-1, keepdims=True) + EPS)
        g = g_ref[...]
        live = (i >= 1).astype(F32)
        e = (hv * r * g - t_ref[...]) * live
        dy = e * (1.0 / D)
        dyg = dy * g
        dh_ref[...] = r * dyg - hv * (r * r * r * jnp.mean(dyg * hv, axis=-1, keepdims=True))
        lpart = jnp.zeros((1, 128), F32) + 0.5 * jnp.sum(jnp.sum(e * e, axis=1, keepdims=True), axis=0, keepdims=True) * (1.0 / D)
        gpart = jnp.sum(dy * hv * r, axis=0, keepdims=True)

        @pl.when(i == 0)
        def _():
            loss_ref[...] = lpart
            dg_ref[...] = gpart

        @pl.when(i > 0)
        def _():
            loss_ref[...] += lpart
            dg_ref[...] += gpart

    return pl.pallas_call(
        body, name="ff_down_loss", grid=(R // TM,),
        in_specs=[_rb(TM, DFF, 0), _const((DFF, D)), _rb(TM, D, 0), _const((1, D)),
                  pl.BlockSpec((TM, D), lambda i: (jnp.maximum(i - 1, 0), 0))],
        out_specs=[_rb(TM, D, 0), _const((1, 128)), _const((1, D))],
        out_shape=[jax.ShapeDtypeStruct((R, D), F32), jax.ShapeDtypeStruct((1, 128), F32),
                   jax.ShapeDtypeStruct((1, D), F32)],
        compiler_params=_cp(("arbitrary",)),
    )(ff, w_down, h1, gf, target)


def conv_bwd(dc, proj, conv_w, dproj):
    R = dc.shape[0]
    t8 = TM // 8
    nt = R // TM

    def body(dc_ref, nxt_ref, x_ref, prv_ref, w_ref, dp_in, dp_ref, dw_ref):
        del dp_in
        i = pl.program_id(0)
        dcv = dc_ref[...]
        nxt = nxt_ref[...] * (i < nt - 1).astype(F32)
        x = x_ref[...]
        prv = prv_ref[...]
        w = w_ref[...]
        dx = dcv * w[3:4, :]
        rows = [None] * 4
        rows[3] = jnp.sum(dcv * x, axis=0, keepdims=True)
        for s in (1, 2, 3):
            dx = dx + _shift_up(dcv, nxt, s) * w[3 - s:4 - s, :]
            rows[3 - s] = jnp.sum(dcv * _shift_down(x, prv, s), axis=0, keepdims=True)
        dp_ref[...] = dx.astype(dp_ref.dtype)
        part = jnp.concatenate(rows + [jnp.sum(dcv, axis=0, keepdims=True), jnp.zeros((3, 1024), F32)], axis=0)

        @pl.when(i == 0)
        def _():
            dw_ref[...] = part

        @pl.when(i > 0)
        def _():
            dw_ref[...] += part

    return pl.pallas_call(
        body, name="conv_bwd", grid=(nt,),
        in_specs=[_rb(TM, 1024, 0),
                  pl.BlockSpec((8, 1024), lambda i: (jnp.minimum((i + 1) * t8, nt * t8 - 1), 0)),
                  _rb(TM, 1024, QK0 // 1024),
                  pl.BlockSpec((8, 1024), lambda i: (jnp.maximum(i * t8 - 1, 0), QK0 // 1024)),
                  _const((4, 1024)), pl.BlockSpec(memory_space=pl.ANY)],
        out_specs=[_rb(TM, 1024, QK0 // 1024), _const((8, 1024))],
        out_shape=[jax.ShapeDtypeStruct((R, NP), MXU), jax.ShapeDtypeStruct((8, 1024), F32)],
        input_output_aliases={5: 0},
        compiler_params=_cp(("arbitrary",)),
    )(dc, dc, proj, proj, conv_w, dproj)


def small_bwd(dgl, dga, proj, gb_row, dproj):
    R = dgl.shape[0]

    def body(dgl_ref, dga_ref, sm_ref, gb_ref, dp_in, dp_ref, dgb_ref):
        del dp_in
        i = pl.program_id(0)
        z = sm_ref[...] + gb_ref[...]
        lane = lax.broadcasted_iota(jnp.int32, z.shape, 1)
        row = lax.broadcasted_iota(jnp.int32, z.shape, 0) + i * TM
        valid = row >= NPADROWS
        dgl_v = dgl_ref[...]
        dgate = jnp.where(valid, jnp.where(lane < 4, dgl_v, dgl_v * _sigmoid(-z)), 0.0)
        ds = jnp.where(lane < 8, dgate, dga_ref[...])
        dp_ref[...] = ds.astype(dp_ref.dtype)
        part = jnp.sum(jnp.where(lane < 8, dgate, 0.0), axis=0, keepdims=True)

        @pl.when(i == 0)
        def _():
            dgb_ref[...] = part

        @pl.when(i > 0)
        def _():
            dgb_ref[...] += part

    return pl.pallas_call(
        body, name="small_bwd", grid=(R // TM,),
        in_specs=[_rb(TM, 128, 0), _rb(TM, 128, 0), _rb(TM, 128, SM0 // 128), _const((1, 128)),
                  pl.BlockSpec(memory_space=pl.ANY)],
        out_specs=[_rb(TM, 128, SM0 // 128), _const((1, 128))],
        out_shape=[jax.ShapeDtypeStruct((R, NP), MXU), jax.ShapeDtypeStruct((1, 128), F32)],
        input_output_aliases={4: 0},
        compiler_params=_cp(("arbitrary",)),
    )(dgl, dga, proj, gb_row, dproj)


def _masks(n=L):
    r = lax.broadcasted_iota(jnp.int32, (n, n), 0)
    c = lax.broadcasted_iota(jnp.int32, (n, n), 1)
    return r >= c, r == c, r


def _to_row(col, eye):
    return jnp.sum(jnp.where(eye, col, 0.0), axis=0, keepdims=True)


def _to_col(row, eye):
    return jnp.sum(jnp.where(eye, row, 0.0), axis=1, keepdims=True)


def _mlstm_chunk(q, k, logi_c, logf_c, m, n):
    tril, eye, _ = _masks(q.shape[0])
    logi_r, logf_r = _to_row(logi_c, eye), _to_row(logf_c, eye)
    b_c = jnp.sum(jnp.where(tril, logf_r, 0.0), axis=1, keepdims=True)
    b_r = _to_row(b_c, eye)
    g = jnp.sum(logf_c, axis=0, keepdims=True)
    dmat = jnp.where(tril, b_c - b_r + logi_r, NEG)
    mrow = jnp.maximum(b_c + m, jnp.max(dmat, axis=1, keepdims=True))
    dm = jnp.exp(dmat - mrow)
    s = _dot(q, k, 1, 1)
    w = dm * s
    a_in = jnp.exp(b_c + m - mrow)
    qn = jnp.sum(q * n, axis=1, keepdims=True)
    den = a_in * qn + jnp.sum(w, axis=1, keepdims=True)
    floor = jnp.exp(-mrow)
    nrm = jnp.maximum(jnp.abs(den), floor)
    wlog_c = g - b_c + logi_c
    m_new = jnp.maximum(g + m, jnp.max(wlog_c, axis=0, keepdims=True))
    a_st = jnp.exp(g + m - m_new)
    w_c = jnp.exp(wlog_c - m_new)
    return dict(b_c=b_c, g=g, dm=dm, s=s, w=w, a_in=a_in, qn=qn, den=den, floor=floor, nrm=nrm,
                m_new=m_new, a_st=a_st, w_c=w_c, tril=tril, eye=eye)


def _mlstm_head_fwd(h, glv, qk_ref, v_ref, mo_ref, hg_ref, hm_ref, ym_ref, cs_ref, nm_ref, c_s, nm_s):
    q = qk_ref[:, h * DQK:(h + 1) * DQK] * QSCALE
    k = qk_ref[:, 512 + h * DQK:512 + (h + 1) * DQK]
    v = v_ref[:, h * DV:(h + 1) * DV]
    C = c_s[h]
    n = nm_s[h, 0:1, :]
    m = nm_s[h, 1:2, 0:1]
    f = _mlstm_chunk(q, k, glv[:, h:h + 1], glv[:, 4 + h:5 + h], m, n)
    num = f["a_in"] * _dot(q, C, 1, 1) + _dot(f["w"], v, 1, 0)
    hh = num / f["nrm"]
    cs_ref[0, h] = C
    nm_ref[0, h] = nm_s[h]
    c_s[h] = f["a_st"] * C + _dot(f["w_c"] * v, k, 0, 0)
    n_new = f["a_st"] * n + jnp.sum(f["w_c"] * k, axis=0, keepdims=True)
    rowi = lax.broadcasted_iota(jnp.int32, (8, DQK), 0)
    nm_s[h] = jnp.where(rowi == 0, n_new, jnp.where(rowi == 1, f["m_new"], 0.0))
    rm = lax.rsqrt(jnp.mean(hh * hh, axis=-1, keepdims=True) + EPS)
    sl = slice(h * DV, (h + 1) * DV)
    hm_ref[:, sl] = hh
    ym_ref[:, sl] = (hh * rm * hg_ref[:, sl] * _sigmoid(mo_ref[:, sl])).astype(ym_ref.dtype)


def _mlstm_bwd_parts(dym_ref, hm_ref, qk_ref, cp_ref, v_ref, gl_ref, mo_ref, hg_ref, cs_ref, nm_ref,
                     dp_ref, dc_ref, dgl_ref, dhg_ref, dc_s, dn_s):
        def init():
            dc_s[...] = jnp.zeros_like(dc_s)
            dn_s[...] = jnp.zeros_like(dn_s)
            dhg_ref[...] = jnp.zeros_like(dhg_ref)

        def zero():
            dp_ref[...] = jnp.zeros_like(dp_ref)
            dc_ref[...] = jnp.zeros_like(dc_ref)
            dgl_ref[...] = jnp.zeros_like(dgl_ref)

        def compute(after_head):
            glv = gl_ref[...]
            lane = lax.broadcasted_iota(jnp.int32, (LM, 128), 1)
            dgl = jnp.zeros((LM, 128), F32)
            for h in range(NH):
                sl = slice(h * DV, (h + 1) * DV)
                sq = slice(h * DQK, (h + 1) * DQK)
                sk = slice(512 + h * DQK, 512 + (h + 1) * DQK)
                hh = hm_ref[:, sl]
                gain = hg_ref[:, sl]
                rm = lax.rsqrt(jnp.mean(hh * hh, axis=-1, keepdims=True) + EPS)
                sg = _sigmoid(mo_ref[:, sl])
                dyv = dym_ref[:, sl]
                dno = dyv * sg
                dp_ref[:, 1024 + h * DV:1024 + (h + 1) * DV] = (dyv * hh * rm * gain * sg * (1.0 - sg)).astype(dp_ref.dtype)
                dhg_ref[:, sl] += jnp.sum(dno * hh * rm, axis=0, keepdims=True)
                dnog = dno * gain
                dh = rm * dnog - hh * (rm * rm * rm * jnp.mean(dnog * hh, axis=-1, keepdims=True))
                q = qk_ref[:, sq] * QSCALE
                k = qk_ref[:, sk]
                v = v_ref[:, sl]
                C = cs_ref[0, h]
                n = nm_ref[0, h, 0:1, :]
                m = nm_ref[0, h, 1:2, 0:1]
                f = _mlstm_chunk(q, k, glv[:, h:h + 1], glv[:, 4 + h:5 + h], m, n)
                eye = f["eye"]
                a_in, nrm, den, w = f["a_in"], f["nrm"], f["den"], f["w"]
                dnum = dh / nrm
                dnrm = -jnp.sum(dh * hh, axis=1, keepdims=True) / nrm
                dden = jnp.where(jnp.abs(den) >= f["floor"], dnrm * jnp.sign(den), 0.0)
                dw = _dot(dnum, v, 1, 1) + dden
                dv = _dot(w, dnum, 0, 0)
                ds = dw * f["dm"]
                e = dw * w
                qc = _dot(q, C, 1, 1)
                dq = _dot(ds, k, 1, 0) + a_in * _dot(dnum, C, 1, 0) + (a_in * dden) * n
                dk = _dot(ds, q, 0, 0)
                dC_in = _dot(a_in * dnum, q, 0, 0)
                dn_in = jnp.sum((a_in * dden) * q, axis=0, keepdims=True)
                da_in = jnp.sum(dnum * qc, axis=1, keepdims=True) + dden * f["qn"]
                col_e = _to_col(jnp.sum(e, axis=0, keepdims=True), eye)
                db = jnp.sum(e, axis=1, keepdims=True) + da_in * a_in - col_e
                dlogi = col_e
                dCp = dc_s[h]
                dnp = dn_s[h, 0:1, :]
                a_st, w_c = f["a_st"], f["w_c"]
                da_st = (jnp.sum(jnp.sum(dCp * C, axis=1, keepdims=True), axis=0, keepdims=True)
                         + jnp.sum(dnp * n, axis=1, keepdims=True))
                vdc = _dot(v, dCp, 1, 0)
                dw_c = jnp.sum((vdc + dnp) * k, axis=1, keepdims=True)
                dv = dv + w_c * _dot(k, dCp, 1, 1)
                dk = dk + w_c * (vdc + dnp)
                fw = dw_c * w_c
                dg = jnp.sum(fw, axis=0, keepdims=True) + da_st * a_st
                db = db - fw
                dlogi = dlogi + fw
                rowc = lax.broadcasted_iota(jnp.int32, (LM, 1), 0)
                db = db + jnp.where(rowc == LM - 1, dg, 0.0)
                triu = lax.broadcasted_iota(jnp.int32, (LM, LM), 1) >= lax.broadcasted_iota(jnp.int32, (LM, LM), 0)
                dlogf = jnp.sum(jnp.where(triu, _to_row(db, eye), 0.0), axis=1, keepdims=True)
                dc_s[h] = a_st * dCp + dC_in
                dn_new = a_st * dnp + dn_in
                dn_s[h] = jnp.zeros((8, DQK), F32) + dn_new
                cq, ck = cp_ref[:, sq], cp_ref[:, sk]
                s_q, s_k = _sigmoid(cq), _sigmoid(ck)
                dc_ref[:, sq] = dq * QSCALE * s_q * (1.0 + cq * (1.0 - s_q))
                dc_ref[:, sk] = dk * s_k * (1.0 + ck * (1.0 - s_k))
                dp_ref[:, sl] = dv.astype(dp_ref.dtype)
                dgl = jnp.where(lane == h, dlogi, jnp.where(lane == 4 + h, dlogf, dgl))
                after_head(h)
            dgl_ref[...] = dgl

        return init, zero, compute


def _gla_logs(sm, a2p, b2, valid):
    za = _dot(sm, a2p, 1, 0) + b2
    return za, jnp.where(valid, _log_sigmoid(za) * (1.0 / TAU), 0.0)


def _valid_rows(c, width):
    row = lax.broadcasted_iota(jnp.int32, (L, width), 0) + c * L
    return row >= NPADROWS


def _gla_cumsum(loga):
    tril, _, _ = _masks()
    return _dot_exact(tril.astype(F32), loga)


def _gla_chunk(q, k, la, bc):
    tril, _, _ = _masks()
    btot = jnp.sum(la, axis=0, keepdims=True)
    ebc = jnp.exp(bc)
    qd = q * ebc
    ki = k * jnp.exp(-bc)
    ke = k * jnp.exp(btot - bc)
    att = jnp.where(tril, _dot(qd, ki, 1, 1), 0.0)
    return dict(tril=tril, bc=bc, btot=btot, ebc=ebc, qd=qd, ki=ki, ke=ke, att=att)


def _col128(row):
    r = lax.broadcasted_iota(jnp.int32, (DQK, DQK), 0)
    c = lax.broadcasted_iota(jnp.int32, (DQK, DQK), 1)
    return jnp.sum(jnp.where(r == c, row, 0.0), axis=1, keepdims=True)


def _gla_fwd_parts(c, q_ref, k_ref, v_ref, gr_ref, sm_ref, a2_ref, b2_ref, hg_ref, hgl_ref, yg_ref, ss_ref, s_s):
        def init():
            s_s[...] = jnp.zeros_like(s_s)

        def zero():
            hgl_ref[...] = jnp.zeros_like(hgl_ref)
            yg_ref[...] = jnp.zeros_like(yg_ref)
            ss_ref[...] = jnp.zeros_like(ss_ref)

        def compute(s):
            rows = pl.ds(s * L, L)
            chunk(LM // L * c + s, q_ref.at[rows], k_ref.at[rows], v_ref.at[rows], gr_ref.at[rows], sm_ref.at[rows],
                  hgl_ref.at[rows], yg_ref.at[rows], ss_ref.at[pl.ds(s, 1)])

        def chunk(c, q_ref, k_ref, v_ref, gr_ref, sm_ref, hgl_ref, yg_ref, ss_ref):
            _, loga = _gla_logs(sm_ref[...], a2_ref[...], b2_ref[...], _valid_rows(c, 512))
            bc_all = _gla_cumsum(loga)
            for h in range(NH):
                sq = slice(h * DQK, (h + 1) * DQK)
                sl = slice(h * DV, (h + 1) * DV)
                q = q_ref[:, sq] * QSCALE
                k = k_ref[:, sq]
                v = v_ref[:, sl]
                S = s_s[h]
                f = _gla_chunk(q, k, loga[:, sq], bc_all[:, sq])
                o = _dot(f["att"], v, 1, 0) + _dot(f["qd"], S, 1, 0)
                ss_ref[0, h] = S
                s_s[h] = _col128(jnp.exp(f["btot"])) * S + _dot(f["ke"], v, 0, 0)
                rg = lax.rsqrt(jnp.mean(o * o, axis=-1, keepdims=True) + EPS)
                gr = gr_ref[:, sl]
                hgl_ref[:, sl] = o
                yg_ref[:, sl] = (o * rg * hg_ref[:, sl] * gr * _sigmoid(gr)).astype(yg_ref.dtype)

        return init, zero, compute


def _gla_bwd_parts(c, dy_ref, ho_ref, q_ref, k_ref, v_ref, gr_ref, sm_ref, a2_ref, b2_ref, hg_ref, ss_ref,
                   dp_ref, dga_ref, da2_ref, db2_ref, dhg_ref, ds_s):
        def init():
            ds_s[...] = jnp.zeros_like(ds_s)
            da2_ref[...] = jnp.zeros_like(da2_ref)
            db2_ref[...] = jnp.zeros_like(db2_ref)
            dhg_ref[...] = jnp.zeros_like(dhg_ref)

        def zero():
            dp_ref[...] = jnp.zeros_like(dp_ref)
            dga_ref[...] = jnp.zeros_like(dga_ref)

        def compute(s):
            rows = pl.ds(s * L, L)
            chunk(LM // L * c + s, dy_ref.at[rows], ho_ref.at[rows], q_ref.at[rows], k_ref.at[rows], v_ref.at[rows],
                  gr_ref.at[rows], sm_ref.at[rows], ss_ref.at[pl.ds(s, 1)], dp_ref.at[rows], dga_ref.at[rows])

        def chunk(c, dy_ref, ho_ref, q_ref, k_ref, v_ref, gr_ref, sm_ref, ss_ref, dp_ref, dga_ref):
            valid = _valid_rows(c, 512)
            sm = sm_ref[...]
            za, loga = _gla_logs(sm, a2_ref[...], b2_ref[...], valid)
            dbcs = []
            for h in range(NH):
                sq = slice(h * DQK, (h + 1) * DQK)
                sl = slice(h * DV, (h + 1) * DV)
                o = ho_ref[:, sl]
                gain = hg_ref[:, sl]
                rg = lax.rsqrt(jnp.mean(o * o, axis=-1, keepdims=True) + EPS)
                gr = gr_ref[:, sl]
                sg = _sigmoid(gr)
                dyv = dy_ref[:, sl]
                dno = dyv * gr * sg
                dp_ref[:, 2048 + h * DV:2048 + (h + 1) * DV] = (
                    dyv * o * rg * gain * sg * (1.0 + gr * (1.0 - sg))).astype(dp_ref.dtype)
                dhg_ref[:, sl] += jnp.sum(dno * o * rg, axis=0, keepdims=True)
                dnog = dno * gain
                do = rg * dnog - o * (rg * rg * rg * jnp.mean(dnog * o, axis=-1, keepdims=True))
                q = q_ref[:, sq] * QSCALE
                k = k_ref[:, sq]
                v = v_ref[:, sl]
                S = ss_ref[0, h]
                f = _gla_chunk(q, k, loga[:, sq], _gla_cumsum(loga[:, sq]))
                tril, qd, ki, ke = f["tril"], f["qd"], f["ki"], f["ke"]
                dSp = ds_s[h]
                datt = jnp.where(tril, _dot(do, v, 1, 1), 0.0)
                dqd = _dot(do, S, 1, 1) + _dot(datt, ki, 1, 0)
                dki = _dot(datt, qd, 0, 0)
                dv = _dot(f["att"], do, 0, 0) + _dot(ke, dSp, 1, 0)
                dke = _dot(v, dSp, 1, 1)
                ebt = jnp.exp(f["btot"])
                dbtot = jnp.sum(dke * ke, axis=0, keepdims=True) + ebt * _to_row128(jnp.sum(dSp * S, axis=1, keepdims=True))
                ds_s[h] = _dot(qd, do, 0, 0) + _col128(ebt) * dSp
                dq = dqd * f["ebc"]
                dk = dki * jnp.exp(-f["bc"]) + dke * jnp.exp(f["btot"] - f["bc"])
                dbc = dqd * qd - dki * ki - dke * ke
                rowc = lax.broadcasted_iota(jnp.int32, (L, DQK), 0)
                dbc = dbc + jnp.where(rowc == L - 1, dbtot, 0.0)
                triu = lax.broadcasted_iota(jnp.int32, (L, L), 1) >= lax.broadcasted_iota(jnp.int32, (L, L), 0)
                dbcs.append(_dot_exact(triu.astype(F32), dbc))
                dp_ref[:, sq] = (dq * QSCALE).astype(dp_ref.dtype)
                dp_ref[:, 512 + h * DQK:512 + (h + 1) * DQK] = dk.astype(dp_ref.dtype)
                dp_ref[:, 1024 + h * DV:1024 + (h + 1) * DV] = dv.astype(dp_ref.dtype)
            dza = jnp.where(valid, jnp.concatenate(dbcs, axis=1) * (1.0 / TAU) * _sigmoid(-za), 0.0)
            dga_ref[...] = _dot(dza, a2_ref[...], 1, 1)
            da2_ref[...] += _dot(sm, dza, 0, 0)
            db2_ref[...] += jnp.sum(dza, axis=0, keepdims=True)

        return init, zero, compute


def mix_fwd(qk, proj, gl, m_head_g, a2p, b2, g_head_g):
    R = qk.shape[0]
    NC = R // LM
    G = LM // L

    def body(qk_ref, mv_ref, gl_ref, mo_ref, mhg_ref, gq_ref, gk_ref, gv_ref, gr_ref, sm_ref, a2_ref, b2_ref, ghg_ref,
             hm_ref, ym_ref, cs_ref, nm_ref, hgl_ref, yg_ref, ss_ref, c_s, nm_s, s_s):
        c = pl.program_id(0)
        g_init, g_zero, g_compute = _gla_fwd_parts(c, gq_ref, gk_ref, gv_ref, gr_ref, sm_ref, a2_ref, b2_ref, ghg_ref,
                                                   hgl_ref, yg_ref, ss_ref, s_s)

        @pl.when(c <= CH0)
        def _():
            c_s[...] = jnp.zeros_like(c_s)
            nm_s[...] = jnp.zeros_like(nm_s)
            g_init()

        @pl.when(c < CH0)
        def _():
            hm_ref[...] = jnp.zeros_like(hm_ref)
            ym_ref[...] = jnp.zeros_like(ym_ref)
            cs_ref[...] = jnp.zeros_like(cs_ref)
            nm_ref[...] = jnp.zeros_like(nm_ref)
            g_zero()

        @pl.when(c >= CH0)
        def _():
            glv = gl_ref[...]
            assert LM // L == NH
            for h in range(NH):
                g_compute(h)
                _mlstm_head_fwd(h, glv, qk_ref, mv_ref, mo_ref, mhg_ref, hm_ref, ym_ref, cs_ref, nm_ref, c_s, nm_s)

    st_m = pl.BlockSpec((1, NH, DV, DQK), lambda c: (c, 0, 0, 0))
    st_n = pl.BlockSpec((1, NH, 8, DQK), lambda c: (c, 0, 0, 0))
    st_g = pl.BlockSpec((G, NH, DQK, DV), lambda c: (c, 0, 0, 0))
    return pl.pallas_call(
        body, name="mix_fwd", grid=(NC,),
        in_specs=[_rb(LM, 1024, 0), _rb(LM, 1024, MV0 // 1024), _rb(LM, 128, 0), _rb(LM, 1024, MO0 // 1024),
                  _const((1, 1024)),
                  _rb(LM, 512, GQ0 // 512), _rb(LM, 512, GK0 // 512), _rb(LM, 1024, GV0 // 1024),
                  _rb(LM, 1024, GR0 // 1024), _rb(LM, 128, SM0 // 128), _const((128, 512)), _const((1, 512)),
                  _const((1, 1024))],
        out_specs=[_rb(LM, 1024, 0), _rb(LM, 1024, 0), st_m, st_n, _rb(LM, 1024, 0), _rb(LM, 1024, 0), st_g],
        out_shape=[jax.ShapeDtypeStruct((R, 1024), F32), jax.ShapeDtypeStruct((R, 1024), MXU),
                   jax.ShapeDtypeStruct((NC, NH, DV, DQK), F32), jax.ShapeDtypeStruct((NC, NH, 8, DQK), F32),
                   jax.ShapeDtypeStruct((R, 1024), F32), jax.ShapeDtypeStruct((R, 1024), MXU),
                   jax.ShapeDtypeStruct((G * NC, NH, DQK, DV), F32)],
        scratch_shapes=[pltpu.VMEM((NH, DV, DQK), F32), pltpu.VMEM((NH, 8, DQK), F32),
                        pltpu.VMEM((NH, DQK, DV), F32)],
        compiler_params=_cp(("arbitrary",)),
    )(qk, proj, gl, proj, m_head_g, proj, proj, proj, proj, proj, a2p, b2, g_head_g)


def mix_bwd(dym, hm, qk, cpre, proj, gl, m_head_g, cs, nm, dyg, hgl, a2p, b2, g_head_g, ss, dproj):
    R = qk.shape[0]
    NC = R // LM
    rev = lambda c: NC - 1 - c
    GW = GR0 + 1024 - GQ0

    def body(dym_ref, hm_ref, qk_ref, cp_ref, mv_ref, gl_ref, mo_ref, mhg_ref, cs_ref, nm_ref,
             dyg_ref, ho_ref, gq_ref, gk_ref, gv_ref, gr_ref, sm_ref, a2_ref, b2_ref, ghg_ref, ss_ref, dp_in,
             dp_ref, dc_ref, dgl_ref, dmhg_ref, dga_ref, da2_ref, db2_ref, dghg_ref, dc_s, dn_s, ds_s):
        del dp_in
        step = pl.program_id(0)
        c = NC - 1 - step
        m_init, m_zero, m_compute = _mlstm_bwd_parts(
            dym_ref, hm_ref, qk_ref, cp_ref, mv_ref, gl_ref, mo_ref, mhg_ref, cs_ref, nm_ref,
            dp_ref.at[:, 0:GQ0], dc_ref, dgl_ref, dmhg_ref, dc_s, dn_s)
        g_init, g_zero, g_compute = _gla_bwd_parts(
            c, dyg_ref, ho_ref, gq_ref, gk_ref, gv_ref, gr_ref, sm_ref, a2_ref, b2_ref, ghg_ref, ss_ref,
            dp_ref.at[:, GQ0:GQ0 + GW], dga_ref, da2_ref, db2_ref, dghg_ref, ds_s)

        @pl.when(step == 0)
        def _():
            m_init()
            g_init()

        @pl.when(c < CH0)
        def _():
            m_zero()
            g_zero()

        @pl.when(c >= CH0)
        def _():
            assert LM // L == NH
            m_compute(lambda h: g_compute(NH - 1 - h))

    def rows(w, cb):
        return pl.BlockSpec((LM, w), lambda c: (rev(c), cb))

    return pl.pallas_call(
        body, name="mix_bwd", grid=(NC,),
        in_specs=[rows(1024, 0), rows(1024, 0), rows(1024, 0), rows(1024, 0), rows(1024, MV0 // 1024), rows(128, 0),
                  rows(1024, MO0 // 1024), _const((1, 1024)),
                  pl.BlockSpec((1, NH, DV, DQK), lambda c: (rev(c), 0, 0, 0)),
                  pl.BlockSpec((1, NH, 8, DQK), lambda c: (rev(c), 0, 0, 0)),
                  rows(1024, 0), rows(1024, 0), rows(512, GQ0 // 512), rows(512, GK0 // 512),
                  rows(1024, GV0 // 1024), rows(1024, GR0 // 1024), rows(128, SM0 // 128),
                  _const((128, 512)), _const((1, 512)), _const((1, 1024)),
                  pl.BlockSpec((LM // L, NH, DQK, DV), lambda c: (rev(c), 0, 0, 0)),
                  pl.BlockSpec(memory_space=pl.ANY)],
        out_specs=[rows(GQ0 + GW, 0), rows(1024, 0), rows(128, 0), _const((1, 1024)),
                   rows(128, 0), _const((128, 512)), _const((1, 512)), _const((1, 1024))],
        out_shape=[jax.ShapeDtypeStruct((R, NP), MXU), jax.ShapeDtypeStruct((R, 1024), F32),
                   jax.ShapeDtypeStruct((R, 128), F32), jax.ShapeDtypeStruct((1, 1024), F32),
                   jax.ShapeDtypeStruct((R, 128), F32), jax.ShapeDtypeStruct((128, 512), F32),
                   jax.ShapeDtypeStruct((1, 512), F32), jax.ShapeDtypeStruct((1, 1024), F32)],
        scratch_shapes=[pltpu.VMEM((NH, DV, DQK), F32), pltpu.VMEM((NH, 8, DQK), F32),
                        pltpu.VMEM((NH, DQK, DV), F32)],
        input_output_aliases={21: 0},
        compiler_params=_cp(("arbitrary",)),
    )(dym, hm, qk, cpre, proj, gl, proj, m_head_g, cs, nm, dyg, hgl, proj, proj, proj, proj, proj, a2p, b2,
      g_head_g, ss, dproj)


def _to_row128(col):
    r = lax.broadcasted_iota(jnp.int32, (DQK, DQK), 0)
    c = lax.broadcasted_iota(jnp.int32, (DQK, DQK), 1)
    return jnp.sum(jnp.where(r == c, col, 0.0), axis=0, keepdims=True)


def local_step(x, target, meta, norm1_g, wp, conv_w, conv_b, m_gate_b, g_a2, g_a2_b, m_head_g, g_head_g,
               norm2_g, final_g, late_weights, send_early, send_wp, first_order=None):
    seq = x.shape[0]
    assert seq % TM == 0
    gb_row = jnp.zeros((1, 128), F32).at[0, 0:8].set(m_gate_b.reshape(8))
    a2p = jnp.zeros((128, 512), F32).at[8:8 + RANK].set(g_a2)
    mhg = m_head_g.reshape(1, 1024)
    ghg = g_head_g.reshape(1, 1024)

    xn = rms_fwd_input(x, meta, norm1_g, "rms1_fwd")
    proj = matmul(xn, wp, "nn", "proj_fwd", tm=1536, order=first_order)
    cpre, qk, gl = prep_fwd(proj, conv_w, conv_b, gb_row)
    hm, ym, cs, nm, hgl, yg, ss = mix_fwd(qk, proj, gl, mhg, a2p, g_a2_b, ghg)
    w_bm, w_bg, w_out, w_gu, w_down = late_weights(ym)
    bm, bg, merged = merge_fwd(ym, yg, w_bm, w_bg, proj)
    h1, hn = out_proj_norm(merged, w_out, x, meta, norm2_g)
    au, ff = ff_in_fwd(hn, w_gu)
    dh2, loss, d_final_g = ff_down_loss(ff, w_down, h1, final_g.reshape(1, D), target)

    d_w_down = matmul(ff, dh2, "tn", "ff_down_wgrad", tm=1408, tk=1536)
    dau = ff_down_dgrad(dh2, w_down, au)
    d_w_gu = matmul(hn, dau, "tn", "ff_in_wgrad", tm=1024, tn=1408, tk=2816)
    dh1, d_norm2_g = dgrad_rms_bwd(dau, w_gu, h1, norm2_g, dh2, "ff_in_dgrad", 768, 2816)

    d_w_out = matmul(merged, dh1, "tn", "out_wgrad", tm=1024, tk=1536)
    dbm, dbg, dproj = merge_bwd(dh1, w_out, bm, bg, proj)
    d_w_bm = matmul(ym, dbm, "tn", "branch_m_wgrad", tm=1024, tk=1536)
    d_w_bg = matmul(yg, dbg, "tn", "branch_g_wgrad", tm=1024, tk=1536)
    token = send_early(dict(w_branch_m=d_w_bm, w_branch_g=d_w_bg, w_out=d_w_out, w_gu=d_w_gu, w_ff_down=d_w_down))
    dym = matmul(dbm, w_bm, "nt", "branch_m_dgrad", order=token)
    dyg = matmul(dbg, w_bg, "nt", "branch_g_dgrad")
    dproj, dc, dgl, d_mhg, dga, d_a2p, d_a2b, d_ghg = mix_bwd(
        dym, hm, qk, cpre, proj, gl, mhg, cs, nm, dyg, hgl, a2p, g_a2_b, ghg, ss, dproj)
    dproj, d_conv = conv_bwd(dc, proj, conv_w, dproj)
    dproj, d_gb = small_bwd(dgl, dga, proj, gb_row, dproj)
    d_wp = matmul(xn, dproj, "tn", "proj_wgrad", tm=1024, tn=1664, tk=2816)
    token = send_wp(d_wp)
    dxn = matmul(dproj, wp, "nt", "proj_dgrad", tm=256, tk=NP, order=token)
    grad_x, d_meta, d_norm1_g = rms_bwd_input(dxn, x, meta, norm1_g, dh1, "rms1_bwd")

    grads = dict(
        meta_tokens=d_meta, norm1_g=d_norm1_g, conv_w=d_conv[0:4], conv_b=d_conv[4:5], m_gate_b=d_gb[0, 0:8].reshape(1, 2, 4),
        g_a2=d_a2p[8:8 + RANK], g_a2_b=d_a2b, m_head_g=d_mhg.reshape(NH, DV), g_head_g=d_ghg.reshape(NH, DV),
        norm2_g=d_norm2_g, final_g=d_final_g)
    return loss, grad_x, grads


_SEGS = [(0, 1024, QK0), (1024, 2048, MV0), (2048, 2056, SM0), (2056, 3080, MO0), (3080, 5128, GQ0),
         (5128, 5144, SM0 + 8), (5144, 6168, GR0), (6168, 8216, GM0)]
SHARD_W = NPROJ // NDEV


def regroup_cols(w8):
    parts = []
    for lo, hi, _ in sorted(_SEGS, key=lambda s: s[2]):
        while lo < hi:
            j = lo // SHARD_W
            end = min(hi, (j + 1) * SHARD_W)
            parts.append(w8[j, :, lo - j * SHARD_W:end - j * SHARD_W])
            lo = end
    parts.append(jnp.zeros((w8.shape[1], NP - NPROJ), w8.dtype))
    return jnp.concatenate(parts, axis=1)


def ungroup_cols(g):
    blocks = []
    for j in range(NDEV):
        lo, hi = j * SHARD_W, (j + 1) * SHARD_W
        parts = []
        for s_lo, s_hi, s_at in _SEGS:
            a, b = max(lo, s_lo), min(hi, s_hi)
            if a < b:
                parts.append(g[:, s_at + a - s_lo:s_at + b - s_lo])
        blocks.append(jnp.concatenate(parts, axis=1))
    return jnp.stack(blocks)


def col_blocks(g):
    r, c8 = g.shape
    return jnp.transpose(g.reshape(r, NDEV, c8 // NDEV), (1, 0, 2))


def from_col_blocks(g8):
    n, r, c = g8.shape
    return jnp.transpose(g8, (1, 0, 2)).reshape(r, n * c)


_MESHID = pl.DeviceIdType.MESH
_RELS = [(0, 0, 1), (1, 0, 0), (0, 1, 0), (1, 1, 0), (1, 0, 1), (0, 1, 1), (1, 1, 1)]


def _flip(v, bit):
    return 1 - v if bit else v


def all_gather(arrs, name):
    n = len(arrs)

    def body(*refs):
        ins, outs = refs[:n], refs[n:2 * n]
        send_sems, recv_sems, local_sems = refs[2 * n:]
        x, y, c = lax.axis_index("x"), lax.axis_index("y"), lax.axis_index("c")
        me, sibling = (x, y, c), (x, y, 1 - c)
        chips = [(1 - x, y), (x, 1 - y), (1 - x, 1 - y)]

        def slot(p):
            return 4 * p[0] + 2 * p[1] + p[2]

        def copy(a, k, block, to, src=None):
            dst = outs[a].at[slot(block)]
            return pltpu.make_async_remote_copy(
                src_ref=dst if src is None else src, dst_ref=dst,
                send_sem=send_sems.at[a, k], recv_sem=recv_sems.at[a, k],
                device_id=to, device_id_type=_MESHID)

        mine = [pltpu.make_async_copy(ins[a], outs[a].at[slot(me)], local_sems.at[a]) for a in range(n)]
        for cp in mine:
            cp.start()
        first = []
        for a in range(n):
            first.append(copy(a, 0, me, sibling, src=ins[a]))
            first += [copy(a, 1 + j, me, (*chip, c), src=ins[a]) for j, chip in enumerate(chips)]
        for cp in first:
            cp.start()
        passed = []
        for j, chip in enumerate(chips):
            for a in range(n):
                copy(a, 1 + j, (*chip, c), me).wait_recv()
                fwd = copy(a, 4 + j, (*chip, c), sibling)
                fwd.start()
                passed.append(fwd)
        for a in range(n):
            copy(a, 0, sibling, me).wait_recv()
            for j, chip in enumerate(chips):
                copy(a, 4 + j, (*chip, 1 - c), me).wait_recv()
        for cp in first + passed:
            cp.wait_send()
        for cp in mine:
            cp.wait()

    anyspec = pl.BlockSpec(memory_space=pl.ANY)
    return pl.pallas_call(
        body, name=name,
        in_specs=[anyspec] * n, out_specs=[anyspec] * n,
        out_shape=[jax.ShapeDtypeStruct((NDEV,) + a.shape, a.dtype) for a in arrs],
        scratch_shapes=[pltpu.SemaphoreType.DMA((n, 7)), pltpu.SemaphoreType.DMA((n, 7)),
                        pltpu.SemaphoreType.DMA((n,))],
    )(*arrs)


def exchange(blocks, rep, name):
    n = len(blocks)

    def body(*refs):
        b_refs, r_ref = refs[:n], refs[n]
        ob_refs, or_ref = refs[n + 1:2 * n + 1], refs[2 * n + 1]
        send_sems, recv_sems, local_sems = refs[2 * n + 2:]
        x, y, c = lax.axis_index("x"), lax.axis_index("y"), lax.axis_index("c")
        me = 4 * x + 2 * y + c

        def pairs(src_slot, dst_slot):
            return [(b_refs[a].at[src_slot], ob_refs[a].at[dst_slot]) for a in range(n)] + [(r_ref, or_ref.at[dst_slot])]

        loc = [pltpu.make_async_copy(s, d, local_sems.at[a]) for a, (s, d) in enumerate(pairs(me, me))]
        for cp in loc:
            cp.start()
        sends = []
        for k, (fx, fy, fc) in enumerate(_RELS):
            peer = (_flip(x, fx), _flip(y, fy), _flip(c, fc))
            pid = 4 * peer[0] + 2 * peer[1] + peer[2]
            for a, (s, d) in enumerate(pairs(pid, me)):
                sends.append(pltpu.make_async_remote_copy(
                    src_ref=s, dst_ref=d, send_sem=send_sems.at[a, k], recv_sem=recv_sems.at[a, k],
                    device_id=peer, device_id_type=_MESHID))
        for cp in sends:
            cp.start()
        for k, (fx, fy, fc) in enumerate(_RELS):
            peer = (_flip(x, fx), _flip(y, fy), _flip(c, fc))
            pid = 4 * peer[0] + 2 * peer[1] + peer[2]
            for a, (s, d) in enumerate(pairs(pid, pid)):
                pltpu.make_async_remote_copy(
                    src_ref=s, dst_ref=d, send_sem=send_sems.at[a, k], recv_sem=recv_sems.at[a, k],
                    device_id=peer, device_id_type=_MESHID).wait_recv()
        for cp in sends:
            cp.wait_send()
        for cp in loc:
            cp.wait()

    anyspec = pl.BlockSpec(memory_space=pl.ANY)
    return pl.pallas_call(
        body, name=name,
        in_specs=[anyspec] * (n + 1), out_specs=[anyspec] * (n + 1),
        out_shape=[jax.ShapeDtypeStruct(b.shape, b.dtype) for b in blocks]
        + [jax.ShapeDtypeStruct((NDEV,) + rep.shape, rep.dtype)],
        scratch_shapes=[pltpu.SemaphoreType.DMA((n + 1, 7)), pltpu.SemaphoreType.DMA((n + 1, 7)),
                        pltpu.SemaphoreType.DMA((n + 1,))],
    )(*blocks, rep)


_HBM = pl.BlockSpec(memory_space=pltpu.HBM)
_SEM = pl.BlockSpec(memory_space=pltpu.SEMAPHORE)
_EFFECT = pltpu.SideEffectType.DATAFLOW_SIDE_EFFECTING


def _peer_ids():
    x, y, c = lax.axis_index("x"), lax.axis_index("y"), lax.axis_index("c")
    peers = []
    for fx, fy, fc in _RELS:
        p = (_flip(x, fx), _flip(y, fy), _flip(c, fc))
        peers.append((p, 4 * p[0] + 2 * p[1] + p[2]))
    return 4 * x + 2 * y + c, peers


def _split_copy(src, land, a, k, peer, src_slot, dst_slot, send_sems, recv_sems):
    return pltpu.make_async_remote_copy(
        src_ref=src if src_slot is None else src.at[src_slot], dst_ref=land.at[dst_slot],
        send_sem=send_sems.at[7 * a + k], recv_sem=recv_sems.at[7 * a + k], device_id=peer, device_id_type=_MESHID)


def _own_copy(src, land, a, n, me, per_peer, send_sems):
    return pltpu.make_async_copy(src.at[me] if per_peer else src, land.at[me], send_sems.at[7 * n + a])


def send_start(srcs, per_peer, order, name):
    n = len(srcs)
    lands = [lax.empty((NDEV,) + (s.shape[1:] if per_peer else s.shape), s.dtype) for s in srcs]

    def body(*refs):
        src_refs, land_refs = refs[1:1 + n], refs[1 + n:1 + 2 * n]
        send_sems, recv_sems = refs[1 + 2 * n], refs[2 + 2 * n]
        token = refs[3 + 4 * n]
        me, peers = _peer_ids()
        for a in range(n):
            _own_copy(src_refs[a], land_refs[a], a, n, me, per_peer, send_sems).start()
        for a in range(n):
            for k, (peer, pid) in enumerate(peers):
                _split_copy(src_refs[a], land_refs[a], a, k, peer, pid if per_peer else None, me,
                            send_sems, recv_sems).start()
        token[...] = jnp.zeros_like(token)

    outs = pl.pallas_call(
        body, name=name,
        in_specs=[pl.BlockSpec(memory_space=pl.ANY)] + [_HBM] * (2 * n),
        out_shape=(pltpu.SemaphoreType.DMA((8 * n,)), pltpu.SemaphoreType.DMA((7 * n,)),
                   *[pltpu.HBM(s.shape, s.dtype) for s in srcs], *[pltpu.HBM(l.shape, l.dtype) for l in lands],
                   jax.ShapeDtypeStruct((8, 128), F32)),
        out_specs=(_SEM, _SEM, *[_HBM] * (2 * n), pl.BlockSpec(memory_space=pltpu.VMEM)),
        input_output_aliases={1 + i: 2 + i for i in range(2 * n)},
        compiler_params=pltpu.CompilerParams(has_side_effects=_EFFECT),
    )(order, *[pltpu.with_memory_space_constraint(s, pltpu.HBM) for s in srcs],
      *[pltpu.with_memory_space_constraint(l, pltpu.HBM) for l in lands])
    return (n, per_peer, outs[0], outs[1], outs[2:2 + n], outs[2 + n:2 + 2 * n]), outs[2 + 2 * n]


def send_wait(handle, after, name):
    n, per_peer, send_sems, recv_sems, src_thru, land_thru = handle

    def body(*refs):
        src_refs, land_refs = refs[:n], refs[n:2 * n]
        s_sems, r_sems = refs[2 * n], refs[2 * n + 1]
        me, peers = _peer_ids()
        for a in range(n):
            _own_copy(src_refs[a], land_refs[a], a, n, me, per_peer, s_sems).wait()
            for k, (peer, pid) in enumerate(peers):
                cp = _split_copy(src_refs[a], land_refs[a], a, k, peer, pid if per_peer else None, pid, s_sems, r_sems)
                cp.wait_send()
                cp.wait_recv()

    outs = pl.pallas_call(
        body, name=name,
        in_specs=[_HBM] * (2 * n) + [_SEM, _SEM, pl.BlockSpec(memory_space=pl.ANY)],
        out_shape=tuple(pltpu.HBM(t.shape, t.dtype) for t in (*src_thru, *land_thru)),
        out_specs=tuple([_HBM] * (2 * n)),
        input_output_aliases={i: i for i in range(2 * n)},
        compiler_params=pltpu.CompilerParams(has_side_effects=_EFFECT),
    )(*src_thru, *land_thru, send_sems, recv_sems, after)
    return list(outs[n:2 * n])


def adamw(parts, w, m, v, name, tr):
    npart, r, c = parts.shape
    c1 = 1.0 - ADAM_B1 ** ADAM_STEP
    c2 = 1.0 - ADAM_B2 ** ADAM_STEP

    def body(p_ref, w_ref, m_ref, v_ref, g_ref, d_ref, nm_ref, nv_ref):
        g = p_ref[0].astype(F32)
        for j in range(1, npart):
            g = g + p_ref[j].astype(F32)
        mn = ADAM_B1 * m_ref[...] + (1.0 - ADAM_B1) * g
        vn = ADAM_B2 * v_ref[...] + (1.0 - ADAM_B2) * (g * g)
        g_ref[...] = g
        nm_ref[...] = mn
        nv_ref[...] = vn
        d_ref[...] = -ADAM_LR * ((mn / c1) / (jnp.sqrt(vn / c2) + ADAM_EPS) + ADAM_WD * w_ref[...])

    spec = _rb(tr, c, 0)
    return pl.pallas_call(
        body, name=name, grid=(r // tr,),
        in_specs=[pl.BlockSpec((npart, tr, c), lambda i: (0, i, 0)), spec, spec, spec],
        out_specs=[spec] * 4, out_shape=[jax.ShapeDtypeStruct((r, c), F32)] * 4,
        compiler_params=_cp(("parallel",)),
    )(parts, w, m, v)


def sum_parts(parts, name, tc):
    npart, r, c = parts.shape

    def body(p_ref, o_ref):
        g = p_ref[0].astype(F32)
        for j in range(1, npart):
            g = g + p_ref[j].astype(F32)
        o_ref[...] = g

    return pl.pallas_call(
        body, name=name, grid=(c // tc,),
        in_specs=[pl.BlockSpec((npart, r, tc), lambda i: (0, 0, i))],
        out_specs=pl.BlockSpec((r, tc), lambda i: (0, i)),
        out_shape=jax.ShapeDtypeStruct((r, c), F32),
        compiler_params=_cp(("parallel",)),
    )(parts)


TINY = [("meta_tokens", (16, 1024)), ("conv_w", (4, 1024)), ("g_a2", (16, 512)), ("m_head_g", (4, 256)),
        ("g_head_g", (4, 256))]
REPL = [("norm1_g", (1, 1024)), ("conv_b", (1, 1024)), ("m_gate_b", (1, 2, 4)), ("g_a2_b", (1, 512)),
        ("norm2_g", (1, 1024)), ("final_g", (1024,))]
TINY_SIZE = 16 * 1024 + 4 * 1024 + 16 * 512 + 2 * 4 * 256
REPL_SIZE = 1024 + 1024 + 8 + 512 + 1024 + 1024
ROWS_GATHER = 8
ROWS_REP = 40
ROWS_OWN = 16


def pack_rows(vecs, rows):
    flat = jnp.concatenate([v.reshape(-1) for v in vecs])
    return jnp.pad(flat, (0, rows * 1024 - flat.shape[0])).reshape(rows, 1024)


def unpack_rows(packed, shapes):
    flat = packed.reshape(-1)
    out, off = [], 0
    for s in shapes:
        n = 1
        for d in s:
            n *= d
        out.append(flat[off:off + n].reshape(s))
        off += n
    return out


def kernel(x, meta_tokens, norm1_g, w_in, conv_w, conv_b, m_gate_b, g_a2, g_a2_b, m_head_g, g_head_g, w_branch_m, w_branch_g, w_out, norm2_g, w_ff_gate, w_ff_up, w_ff_down, final_g, loss_target, m_meta_tokens, m_norm1_g, m_w_in, m_conv_w, m_conv_b, m_m_gate_b, m_g_a2, m_g_a2_b, m_m_head_g, m_g_head_g, m_w_branch_m, m_w_branch_g, m_w_out, m_norm2_g, m_w_ff_gate, m_w_ff_up, m_w_ff_down, m_final_g, v_meta_tokens, v_norm1_g, v_w_in, v_conv_w, v_conv_b, v_m_gate_b, v_g_a2, v_g_a2_b, v_m_head_g, v_g_head_g, v_w_branch_m, v_w_branch_g, v_w_out, v_norm2_g, v_w_ff_gate, v_w_ff_up, v_w_ff_down, v_final_g):
    w_sh = dict(meta_tokens=meta_tokens, w_in=w_in[0], conv_w=conv_w[0], g_a2=g_a2[0], m_head_g=m_head_g[0],
                g_head_g=g_head_g[0], w_branch_m=w_branch_m[0], w_branch_g=w_branch_g[0], w_out=w_out[0],
                w_ff_gate=w_ff_gate[0], w_ff_up=w_ff_up[0], w_ff_down=w_ff_down[0])
    m_sh = dict(meta_tokens=m_meta_tokens, w_in=m_w_in[0], conv_w=m_conv_w[0], g_a2=m_g_a2[0],
                m_head_g=m_m_head_g[0], g_head_g=m_g_head_g[0], w_branch_m=m_w_branch_m[0],
                w_branch_g=m_w_branch_g[0], w_out=m_w_out[0], w_ff_gate=m_w_ff_gate[0], w_ff_up=m_w_ff_up[0],
                w_ff_down=m_w_ff_down[0])
    v_sh = dict(meta_tokens=v_meta_tokens, w_in=v_w_in[0], conv_w=v_conv_w[0], g_a2=v_g_a2[0],
                m_head_g=v_m_head_g[0], g_head_g=v_g_head_g[0], w_branch_m=v_w_branch_m[0],
                w_branch_g=v_w_branch_g[0], w_out=v_w_out[0], w_ff_gate=v_w_ff_gate[0], w_ff_up=v_w_ff_up[0],
                w_ff_down=v_w_ff_down[0])
    w_rep = dict(norm1_g=norm1_g, conv_b=conv_b, m_gate_b=m_gate_b, g_a2_b=g_a2_b, norm2_g=norm2_g, final_g=final_g)
    m_rep = dict(norm1_g=m_norm1_g, conv_b=m_conv_b, m_gate_b=m_m_gate_b, g_a2_b=m_g_a2_b, norm2_g=m_norm2_g,
                 final_g=m_final_g)
    v_rep = dict(norm1_g=v_norm1_g, conv_b=v_conv_b, m_gate_b=v_m_gate_b, g_a2_b=v_g_a2_b, norm2_g=v_norm2_g,
                 final_g=v_final_g)
    dev = 4 * lax.axis_index("x") + 2 * lax.axis_index("y") + lax.axis_index("c")
    tiny_names = [n for n, _ in TINY]
    repl_names = [n for n, _ in REPL]
    tiny_shard_shapes = [(s[0], s[1] // NDEV) for _, s in TINY]

    in8, tiny8 = all_gather([w_sh["w_in"].astype(MXU), pack_rows([w_sh[n] for n in tiny_names], ROWS_GATHER)],
                            "param_all_gather")
    late_names = ["w_branch_m", "w_branch_g", "w_out", "w_ff_gate", "w_ff_up", "w_ff_down"]
    late, first_order = send_start([w_sh[n].astype(MXU) for n in late_names], False, tiny8, "late_weights_start")
    wp = regroup_cols(in8)
    handles = {}

    def late_weights(after):
        bm8, bg8, out8, ffg8, ffu8, ffd8 = send_wait(late, after, "late_weights_wait")
        w_gu = interleave_gu(from_col_blocks(ffg8), from_col_blocks(ffu8))
        return bm8.reshape(D, D), bg8.reshape(D, D), out8.reshape(D, D), w_gu, ffd8.reshape(DFF, D)

    def send_early(g):
        d_gate, d_up = split_gu(g["w_gu"])
        blocks = [g["w_branch_m"].reshape(NDEV, D // NDEV, D).astype(WIRE),
                  g["w_branch_g"].reshape(NDEV, D // NDEV, D).astype(WIRE),
                  g["w_out"].reshape(NDEV, D // NDEV, D).astype(WIRE),
                  col_blocks(d_gate).astype(WIRE), col_blocks(d_up).astype(WIRE),
                  g["w_ff_down"].reshape(NDEV, DFF // NDEV, D).astype(WIRE)]
        handles["early"], token = send_start(blocks, True, blocks[0], "early_grads_start")
        return token

    def send_wp(d_wp):
        blocks = [ungroup_cols(d_wp).astype(WIRE)]
        handles["wp"], token = send_start(blocks, True, blocks[0], "proj_grads_start")
        return token

    tiny_full = {}
    for j in range(NDEV):
        for name, blk in zip(tiny_names, unpack_rows(tiny8[j], tiny_shard_shapes)):
            tiny_full.setdefault(name, []).append(blk)
    tiny_full = {n: jnp.concatenate(v, axis=1) for n, v in tiny_full.items()}

    loss, grad_x, g = local_step(
        x[0], loss_target[0], tiny_full["meta_tokens"], norm1_g, wp, tiny_full["conv_w"], conv_b, m_gate_b[0],
        tiny_full["g_a2"], g_a2_b, tiny_full["m_head_g"], tiny_full["g_head_g"], norm2_g, final_g,
        late_weights, send_early, send_wp, first_order)

    rep = pack_rows([g[n] for n in tiny_names + repl_names] + [loss[0, 0:1]], ROWS_REP)
    (got_rep,) = exchange([], rep, "small_grad_exchange")
    got_early = send_wait(handles["early"], got_rep, "early_grads_wait")
    (got_wp,) = send_wait(handles["wp"], got_rep, "proj_grads_wait")

    result = {}

    def update(name, parts, tr):
        outs = adamw(parts, w_sh[name], m_sh[name], v_sh[name], "adamw_" + name, tr)
        for kind, arr in zip(("grad", "delta", "new_m", "new_v"), outs):
            result[kind, name] = arr[None]

    update("w_in", got_wp, 128)
    update("w_branch_m", got_early[0], 128)
    update("w_branch_g", got_early[1], 128)
    update("w_out", got_early[2], 128)
    update("w_ff_gate", got_early[3], 256)
    update("w_ff_up", got_early[4], 256)
    update("w_ff_down", got_early[5], DFF // NDEV)

    rep_sum = sum_parts(got_rep, "sum_small", 1024)
    rep_g = unpack_rows(rep_sum, [s for _, s in TINY] + [s for _, s in REPL] + [(1,)])
    own_g = [lax.dynamic_slice_in_dim(gf, dev * ss[1], ss[1], axis=1) for gf, ss in zip(rep_g, tiny_shard_shapes)]
    own_g += rep_g[len(TINY):len(TINY) + len(REPL)]
    w_all = {**w_sh, **w_rep}
    m_all = {**m_sh, **m_rep}
    v_all = {**v_sh, **v_rep}
    names = tiny_names + repl_names
    outs = adamw(pack_rows(own_g, ROWS_OWN)[None], pack_rows([w_all[n] for n in names], ROWS_OWN),
                 pack_rows([m_all[n] for n in names], ROWS_OWN), pack_rows([v_all[n] for n in names], ROWS_OWN),
                 "adamw_small", ROWS_OWN)
    shapes = tiny_shard_shapes + [s for _, s in REPL]
    for kind, packed in zip(("grad", "delta", "new_m", "new_v"), outs):
        for name, arr in zip(names, unpack_rows(packed, shapes)):
            result[kind, name] = arr[None] if name in tiny_names and name != "meta_tokens" else arr
    loss_total = rep_g[-1][0]
    order = ["meta_tokens", "norm1_g", "w_in", "conv_w", "conv_b", "m_gate_b", "g_a2", "g_a2_b", "m_head_g", "g_head_g",
             "w_branch_m", "w_branch_g", "w_out", "norm2_g", "w_ff_gate", "w_ff_up", "w_ff_down", "final_g"]
    return (loss_total, grad_x[None], *[result[kind, n] for kind in ("grad", "delta", "new_m", "new_v") for n in order])
```

```python
import functools

import jax
import jax.numpy as jnp
from jax import lax
from jax.experimental import pallas as pl
from jax.experimental.pallas import tpu as pltpu

F32 = jnp.float32
MXU = jnp.bfloat16
WIRE = jnp.bfloat16

D = 1024
NH = 4
DV = 256
DQK = 128
L = 64
NMETA = 16
PADR = 512
LM = 256
CH0 = PADR // LM - 1
NPADROWS = PADR - NMETA
RANK = 16
DFF = 2816
EPS = 1e-6
TAU = 16.0
QSCALE = DQK ** -0.5
NEG = -1e30
NDEV = 8

MV0, MO0, GQ0, GK0, GV0, GR0, QK0, GM0, GG0, SM0 = 0, 1024, 2048, 2560, 3072, 4096, 5120, 6144, 7168, 8192
NP = 8320
NPROJ = 8216

ADAM_LR, ADAM_B1, ADAM_B2, ADAM_EPS, ADAM_WD, ADAM_STEP = 0.001, 0.9, 0.999, 1e-08, 0.01, 10

VMEM_LIMIT = 56 * 1024 * 1024
TM = 512


def _cp(sem):
    return pltpu.CompilerParams(dimension_semantics=sem, vmem_limit_bytes=VMEM_LIMIT)


def _sigmoid(x):
    return 1.0 / (1.0 + jnp.exp(-x))


def _log_sigmoid(x):
    return jnp.minimum(x, 0.0) - jnp.log1p(jnp.exp(-jnp.abs(x)))


def _dot(a, b, ca, cb):
    return lax.dot_general(a.astype(MXU), b.astype(MXU), (((ca,), (cb,)), ((), ())), preferred_element_type=F32)


def _dot_exact(a, b):
    return lax.dot_general(a, b, (((1,), (0,)), ((), ())), precision=lax.Precision.HIGHEST,
                           preferred_element_type=F32)


def _rb(tm, w, cb):
    return pl.BlockSpec((tm, w), lambda i: (i, cb))


def _const(shape):
    nd = len(shape)
    return pl.BlockSpec(shape, lambda i: (0,) * nd)


def _pick(n, target):
    if n <= target:
        return n
    best = None
    for t in range(128, target + 1, 128):
        if n % t == 0:
            best = t
    assert best is not None, (n, target)
    return best


def matmul(a, b, mode, name, add=None, out_dtype=F32, tm=512, tn=1664, tk=1024, order=None):
    if mode == "nn":
        (M, K), (K2, N) = a.shape, b.shape
    elif mode == "nt":
        (M, K), (N, K2) = a.shape, b.shape
    else:
        (K, M), (K2, N) = a.shape, b.shape
    assert K == K2, (a.shape, b.shape, mode)
    tm, tn, tk = _pick(M, tm), _pick(N, tn), _pick(K, tk)
    nk = K // tk
    assert nk == 1 or out_dtype == F32
    ca, cb = {"nn": (1, 0), "nt": (1, 1), "tn": (0, 0)}[mode]
    a_spec = {"nn": pl.BlockSpec((tm, tk), lambda j, i, k: (i, k)),
              "nt": pl.BlockSpec((tm, tk), lambda j, i, k: (i, k)),
              "tn": pl.BlockSpec((tk, tm), lambda j, i, k: (k, i))}[mode]
    b_spec = {"nn": pl.BlockSpec((tk, tn), lambda j, i, k: (k, j)),
              "nt": pl.BlockSpec((tn, tk), lambda j, i, k: (j, k)),
              "tn": pl.BlockSpec((tk, tn), lambda j, i, k: (k, j))}[mode]
    o_spec = pl.BlockSpec((tm, tn), lambda j, i, k: (i, j))
    has_add = add is not None

    def body(*refs):
        if order is not None:
            refs = refs[:-2] + refs[-1:]
        if has_add:
            a_ref, b_ref, add_ref, o_ref = refs
        else:
            a_ref, b_ref, o_ref = refs
            add_ref = None
        part = _dot(a_ref[...], b_ref[...], ca, cb)
        if nk == 1:
            if has_add:
                part = part + add_ref[...]
            o_ref[...] = part.astype(o_ref.dtype)
            return
        k = pl.program_id(2)

        @pl.when(k == 0)
        def _():
            o_ref[...] = part + add_ref[...] if has_add else part

        @pl.when(k > 0)
        def _():
            o_ref[...] += part

    in_specs = [a_spec, b_spec] + ([o_spec] if has_add else [])
    args = (a, b) + ((add,) if has_add else ())
    if order is not None:
        in_specs.append(pl.BlockSpec(order.shape, lambda j, i, k: (0, 0)))
        args += (order,)
    return pl.pallas_call(
        body, name=name, grid=(N // tn, M // tm, nk),
        in_specs=in_specs, out_specs=o_spec,
        out_shape=jax.ShapeDtypeStruct((M, N), out_dtype),
        compiler_params=_cp(("parallel", "parallel", "arbitrary")),
    )(*args)


def _h0_tile(i, x_ref, meta_ref):
    assert PADR == TM
    front = jnp.concatenate([jnp.zeros((NPADROWS, D), F32), meta_ref[...]], axis=0)
    return jnp.where(i == 0, front, x_ref[...])


_TOKENS = pl.BlockSpec((TM, D), lambda i: (jnp.maximum(i - 1, 0), 0))


def rms_fwd_input(x, meta, g, name):
    R = x.shape[0] + PADR

    def body(x_ref, meta_ref, g_ref, y_ref):
        xv = _h0_tile(pl.program_id(0), x_ref, meta_ref)
        r = lax.rsqrt(jnp.mean(xv * xv, axis=-1, keepdims=True) + EPS)
        y_ref[...] = (xv * r * g_ref[...]).astype(y_ref.dtype)

    return pl.pallas_call(
        body, name=name, grid=(R // TM,),
        in_specs=[_TOKENS, _const((NMETA, D)), _const((1, D))], out_specs=_rb(TM, D, 0),
        out_shape=jax.ShapeDtypeStruct((R, D), MXU), compiler_params=_cp(("parallel",)),
    )(x, meta, g)


def dgrad_rms_bwd(a, w, x, g, dres, name, tm, tk):
    R, K = a.shape
    tm, tk = _pick(R, tm), _pick(K, tk)
    nk = K // tk

    def body(a_ref, w_ref, x_ref, g_ref, dres_ref, dx_ref, dg_ref):
        i, k = pl.program_id(0), pl.program_id(1)
        part = _dot(a_ref[...], w_ref[...], 1, 1)

        @pl.when(k == 0)
        def _():
            dx_ref[...] = part

        @pl.when(k > 0)
        def _():
            dx_ref[...] += part

        @pl.when(k == nk - 1)
        def _():
            xv, dyv = x_ref[...], dx_ref[...]
            r = lax.rsqrt(jnp.mean(xv * xv, axis=-1, keepdims=True) + EPS)
            dyg = dyv * g_ref[...]
            dx_ref[...] = dres_ref[...] + r * dyg - xv * (r * r * r * jnp.mean(dyg * xv, axis=-1, keepdims=True))
            gpart = jnp.sum(dyv * xv * r, axis=0, keepdims=True)

            @pl.when(i == 0)
            def _():
                dg_ref[...] = gpart

            @pl.when(i > 0)
            def _():
                dg_ref[...] += gpart

    row = pl.BlockSpec((tm, D), lambda i, k: (i, 0))
    return pl.pallas_call(
        body, name=name, grid=(R // tm, nk),
        in_specs=[pl.BlockSpec((tm, tk), lambda i, k: (i, k)), pl.BlockSpec((D, tk), lambda i, k: (0, k)), row,
                  pl.BlockSpec((1, D), lambda i, k: (0, 0)), row],
        out_specs=[row, pl.BlockSpec((1, D), lambda i, k: (0, 0))],
        out_shape=[jax.ShapeDtypeStruct((R, D), F32), jax.ShapeDtypeStruct((1, D), F32)],
        compiler_params=_cp(("arbitrary", "arbitrary")),
    )(a, w, x, g, dres)


def rms_bwd_input(dy, x, meta, g, dres, name):
    R = dy.shape[0]

    def body(dy_ref, x_ref, meta_ref, g_ref, dres_ref, dx_ref, dmeta_ref, dg_ref):
        i = pl.program_id(0)
        xv, dyv = _h0_tile(i, x_ref, meta_ref), dy_ref[...]
        r = lax.rsqrt(jnp.mean(xv * xv, axis=-1, keepdims=True) + EPS)
        dyg = dyv * g_ref[...]
        dx = dres_ref[...] + r * dyg - xv * (r * r * r * jnp.mean(dyg * xv, axis=-1, keepdims=True))
        dx_ref[...] = dx
        part = jnp.sum(dyv * xv * r, axis=0, keepdims=True)

        @pl.when(i == 0)
        def _():
            dg_ref[...] = part
            dmeta_ref[...] = dx[NPADROWS:PADR]

        @pl.when(i > 0)
        def _():
            dg_ref[...] += part

    return pl.pallas_call(
        body, name=name, grid=(R // TM,),
        in_specs=[_rb(TM, D, 0), _TOKENS, _const((NMETA, D)), _const((1, D)), _rb(TM, D, 0)],
        out_specs=[_TOKENS, _const((NMETA, D)), _const((1, D))],
        out_shape=[jax.ShapeDtypeStruct((R - PADR, D), F32), jax.ShapeDtypeStruct((NMETA, D), F32),
                   jax.ShapeDtypeStruct((1, D), F32)],
        compiler_params=_cp(("arbitrary",)),
    )(dy, x, meta, g, dres)


def _shift_down(cur, prev8, s):
    tm = cur.shape[0]
    rolled = pltpu.roll(cur, s, 0)
    rows8 = lax.broadcasted_iota(jnp.int32, (8, cur.shape[1]), 0)
    head = jnp.where(rows8 < s, pltpu.roll(prev8, s, 0), rolled[0:8])
    return jnp.concatenate([head, rolled[8:tm]], axis=0)


def _shift_up(cur, next8, s):
    tm = cur.shape[0]
    rolled = pltpu.roll(cur, tm - s, 0)
    rows8 = lax.broadcasted_iota(jnp.int32, (8, cur.shape[1]), 0)
    tail = jnp.where(rows8 >= 8 - s, pltpu.roll(next8, 8 - s, 0), rolled[tm - 8:tm])
    return jnp.concatenate([rolled[0:tm - 8], tail], axis=0)


def prep_fwd(proj, conv_w, conv_b, gb_row):
    R = proj.shape[0]
    t8 = TM // 8

    def body(x_ref, halo_ref, sm_ref, w_ref, b_ref, gb_ref, c_ref, qk_ref, gl_ref):
        i = pl.program_id(0)
        x = x_ref[...]
        halo = halo_ref[...]
        w = w_ref[...]
        c = x * w[3:4, :] + b_ref[...]
        for s in (1, 2, 3):
            c = c + _shift_down(x, halo, s) * w[3 - s:4 - s, :]
        c_ref[...] = c
        qk_ref[...] = c * _sigmoid(c)
        z = sm_ref[...] + gb_ref[...]
        lane = lax.broadcasted_iota(jnp.int32, z.shape, 1)
        row = lax.broadcasted_iota(jnp.int32, z.shape, 0) + i * TM
        valid = row >= NPADROWS
        logi = jnp.where(valid, z, NEG)
        logf = jnp.where(valid, _log_sigmoid(z), 0.0)
        gl_ref[...] = jnp.where(lane < 4, logi, jnp.where(lane < 8, logf, 0.0))

    return pl.pallas_call(
        body, name="prep_fwd", grid=(R // TM,),
        in_specs=[_rb(TM, 1024, QK0 // 1024),
                  pl.BlockSpec((8, 1024), lambda i: (jnp.maximum(i * t8 - 1, 0), QK0 // 1024)),
                  _rb(TM, 128, SM0 // 128), _const((4, 1024)), _const((1, 1024)), _const((1, 128))],
        out_specs=[_rb(TM, 1024, 0), _rb(TM, 1024, 0), _rb(TM, 128, 0)],
        out_shape=[jax.ShapeDtypeStruct((R, 1024), F32), jax.ShapeDtypeStruct((R, 1024), F32),
                   jax.ShapeDtypeStruct((R, 128), F32)],
        compiler_params=_cp(("parallel",)),
    )(proj, proj, proj, conv_w, conv_b, gb_row)


def merge_fwd(ym, yg, w_bm, w_bg, proj):
    R = ym.shape[0]

    def body(ym_ref, yg_ref, wm_ref, wg_ref, gm_ref, gg_ref, bm_ref, bg_ref, o_ref):
        bm = _dot(ym_ref[...], wm_ref[...], 1, 0)
        bg = _dot(yg_ref[...], wg_ref[...], 1, 0)
        bm_ref[...] = bm
        bg_ref[...] = bg
        o_ref[...] = (_sigmoid(gm_ref[...]) * bm + _sigmoid(gg_ref[...]) * bg).astype(o_ref.dtype)

    return pl.pallas_call(
        body, name="merge_fwd", grid=(R // TM,),
        in_specs=[_rb(TM, D, 0), _rb(TM, D, 0), _const((D, D)), _const((D, D)), _rb(TM, D, GM0 // D),
                  _rb(TM, D, GG0 // D)],
        out_specs=[_rb(TM, D, 0), _rb(TM, D, 0), _rb(TM, D, 0)],
        out_shape=[jax.ShapeDtypeStruct((R, D), F32), jax.ShapeDtypeStruct((R, D), F32),
                   jax.ShapeDtypeStruct((R, D), MXU)],
        compiler_params=_cp(("parallel",)),
    )(ym, yg, w_bm, w_bg, proj, proj)


def merge_bwd(dh1, w_out, bm, bg, proj):
    R = bm.shape[0]

    def body(dh_ref, w_ref, bm_ref, bg_ref, gm_ref, gg_ref, dbm_ref, dbg_ref, dp_ref):
        dm = _dot(dh_ref[...], w_ref[...], 1, 1)
        sm, sg = _sigmoid(gm_ref[...]), _sigmoid(gg_ref[...])
        dbm_ref[...] = (dm * sm).astype(dbm_ref.dtype)
        dbg_ref[...] = (dm * sg).astype(dbg_ref.dtype)
        dp_ref[:, 0:D] = (dm * bm_ref[...] * sm * (1.0 - sm)).astype(dp_ref.dtype)
        dp_ref[:, D:2 * D] = (dm * bg_ref[...] * sg * (1.0 - sg)).astype(dp_ref.dtype)

    return pl.pallas_call(
        body, name="merge_bwd", grid=(R // TM,),
        in_specs=[_rb(TM, D, 0), _const((D, D)), _rb(TM, D, 0), _rb(TM, D, 0), _rb(TM, D, GM0 // D),
                  _rb(TM, D, GG0 // D)],
        out_specs=[_rb(TM, D, 0), _rb(TM, D, 0), _rb(TM, 2 * D, GM0 // (2 * D))],
        out_shape=[jax.ShapeDtypeStruct((R, D), MXU), jax.ShapeDtypeStruct((R, D), MXU),
                   jax.ShapeDtypeStruct((R, NP), MXU)],
        compiler_params=_cp(("parallel",)),
    )(dh1, w_out, bm, bg, proj, proj)


TF = DFF // 2
TMF = 768


def interleave_gu(gate, up):
    return jnp.concatenate([gate[:, :TF], up[:, :TF], gate[:, TF:], up[:, TF:]], axis=1)


def split_gu(gu):
    return (jnp.concatenate([gu[:, 0:TF], gu[:, 2 * TF:3 * TF]], axis=1),
            jnp.concatenate([gu[:, TF:2 * TF], gu[:, 3 * TF:]], axis=1))


def ff_in_fwd(hn, w_gu):
    R = hn.shape[0]

    def body(x_ref, w_ref, au_ref, ff_ref):
        au = _dot(x_ref[...], w_ref[...], 1, 0)
        au_ref[...] = au.astype(au_ref.dtype)
        a = au[:, :TF]
        ff_ref[...] = (a * _sigmoid(a) * au[:, TF:]).astype(ff_ref.dtype)

    tm = _pick(R, TMF)
    return pl.pallas_call(
        body, name="ff_in_fwd", grid=(DFF // TF, R // tm),
        in_specs=[pl.BlockSpec((tm, D), lambda j, i: (i, 0)), pl.BlockSpec((D, 2 * TF), lambda j, i: (0, j))],
        out_specs=[pl.BlockSpec((tm, 2 * TF), lambda j, i: (i, j)), pl.BlockSpec((tm, TF), lambda j, i: (i, j))],
        out_shape=[jax.ShapeDtypeStruct((R, 2 * DFF), MXU), jax.ShapeDtypeStruct((R, DFF), MXU)],
        compiler_params=_cp(("parallel", "parallel")),
    )(hn, w_gu)


def ff_down_dgrad(dh2, w_down, au):
    R = dh2.shape[0]

    def body(d_ref, w_ref, au_ref, o_ref):
        dff = _dot(d_ref[...], w_ref[...], 1, 1)
        a = au_ref[:, :TF].astype(F32)
        u = au_ref[:, TF:].astype(F32)
        s = _sigmoid(a)
        o_ref[:, :TF] = (dff * u * s * (1.0 + a * (1.0 - s))).astype(o_ref.dtype)
        o_ref[:, TF:] = (dff * a * s).astype(o_ref.dtype)

    tm = _pick(R, TMF)
    return pl.pallas_call(
        body, name="ff_down_dgrad", grid=(DFF // TF, R // tm),
        in_specs=[pl.BlockSpec((tm, D), lambda j, i: (i, 0)), pl.BlockSpec((TF, D), lambda j, i: (j, 0)),
                  pl.BlockSpec((tm, 2 * TF), lambda j, i: (i, j))],
        out_specs=pl.BlockSpec((tm, 2 * TF), lambda j, i: (i, j)),
        out_shape=jax.ShapeDtypeStruct((R, 2 * DFF), MXU),
        compiler_params=_cp(("parallel", "parallel")),
    )(dh2, w_down, au)


def out_proj_norm(merged, w, x, meta, g):
    R = merged.shape[0]

    def body(m_ref, w_ref, x_ref, meta_ref, g_ref, h_ref, n_ref):
        hv = _dot(m_ref[...], w_ref[...], 1, 0) + _h0_tile(pl.program_id(0), x_ref, meta_ref)
        h_ref[...] = hv
        r = lax.rsqrt(jnp.mean(hv * hv, axis=-1, keepdims=True) + EPS)
        n_ref[...] = (hv * r * g_ref[...]).astype(n_ref.dtype)

    return pl.pallas_call(
        body, name="out_fwd", grid=(R // TM,),
        in_specs=[_rb(TM, D, 0), _const((D, D)), _TOKENS, _const((NMETA, D)), _const((1, D))],
        out_specs=[_rb(TM, D, 0), _rb(TM, D, 0)],
        out_shape=[jax.ShapeDtypeStruct((R, D), F32), jax.ShapeDtypeStruct((R, D), MXU)],
        compiler_params=_cp(("parallel",)),
    )(merged, w, x, meta, g)


def ff_down_loss(ff, w_down, h1, gf, target):
    R = ff.shape[0]
    assert PADR == TM

    def body(f_ref, w_ref, h1_ref, g_ref, t_ref, dh_ref, loss_ref, dg_ref):
        i = pl.program_id(0)
        hv = _dot(f_ref[...], w_ref[...], 1, 0) + h1_ref[...]
        r = lax.rsqrt(jnp.mean(hv * hv, axis=-1, keepdims=True) + EPS)
        g = g_ref[...]
        live = (i >= 1).astype(F32)
        e = (hv * r * g - t_ref[...]) * live
        dy = e * (1.0 / D)
        dyg = dy * g
        dh_ref[...] = r * dyg - hv * (r * r * r * jnp.mean(dyg * hv, axis=-1, keepdims=True))
        lpart = jnp.zeros((1, 128), F32) + 0.5 * jnp.sum(jnp.sum(e * e, axis=1, keepdims=True), axis=0, keepdims=True) * (1.0 / D)
        gpart = jnp.sum(dy * hv * r, axis=0, keepdims=True)

        @pl.when(i == 0)
        def _():
            loss_ref[...] = lpart
            dg_ref[...] = gpart

        @pl.when(i > 0)
        def _():
            loss_ref[...] += lpart
            dg_ref[...] += gpart

    return pl.pallas_call(
        body, name="ff_down_loss", grid=(R // TM,),
        in_specs=[_rb(TM, DFF, 0), _const((DFF, D)), _rb(TM, D, 0), _const((1, D)),
                  pl.BlockSpec((TM, D), lambda i: (jnp.maximum(i - 1, 0), 0))],
        out_specs=[_rb(TM, D, 0), _const((1, 128)), _const((1, D))],
        out_shape=[jax.ShapeDtypeStruct((R, D), F32), jax.ShapeDtypeStruct((1, 128), F32),
                   jax.ShapeDtypeStruct((1, D), F32)],
        compiler_params=_cp(("arbitrary",)),
    )(ff, w_down, h1, gf, target)


def conv_bwd(dc, proj, conv_w, dproj):
    R = dc.shape[0]
    t8 = TM // 8
    nt = R // TM

    def body(dc_ref, nxt_ref, x_ref, prv_ref, w_ref, dp_in, dp_ref, dw_ref):
        del dp_in
        i = pl.program_id(0)
        dcv = dc_ref[...]
        nxt = nxt_ref[...] * (i < nt - 1).astype(F32)
        x = x_ref[...]
        prv = prv_ref[...]
        w = w_ref[...]
        dx = dcv * w[3:4, :]
        rows = [None] * 4
        rows[3] = jnp.sum(dcv * x, axis=0, keepdims=True)
        for s in (1, 2, 3):
            dx = dx + _shift_up(dcv, nxt, s) * w[3 - s:4 - s, :]
            rows[3 - s] = jnp.sum(dcv * _shift_down(x, prv, s), axis=0, keepdims=True)
        dp_ref[...] = dx.astype(dp_ref.dtype)
        part = jnp.concatenate(rows + [jnp.sum(dcv, axis=0, keepdims=True), jnp.zeros((3, 1024), F32)], axis=0)

        @pl.when(i == 0)
        def _():
            dw_ref[...] = part

        @pl.when(i > 0)
        def _():
            dw_ref[...] += part

    return pl.pallas_call(
        body, name="conv_bwd", grid=(nt,),
        in_specs=[_rb(TM, 1024, 0),
                  pl.BlockSpec((8, 1024), lambda i: (jnp.minimum((i + 1) * t8, nt * t8 - 1), 0)),
                  _rb(TM, 1024, QK0 // 1024),
                  pl.BlockSpec((8, 1024), lambda i: (jnp.maximum(i * t8 - 1, 0), QK0 // 1024)),
                  _const((4, 1024)), pl.BlockSpec(memory_space=pl.ANY)],
        out_specs=[_rb(TM, 1024, QK0 // 1024), _const((8, 1024))],
        out_shape=[jax.ShapeDtypeStruct((R, NP), MXU), jax.ShapeDtypeStruct((8, 1024), F32)],
        input_output_aliases={5: 0},
        compiler_params=_cp(("arbitrary",)),
    )(dc, dc, proj, proj, conv_w, dproj)


def small_bwd(dgl, dga, proj, gb_row, dproj):
    R = dgl.shape[0]

    def body(dgl_ref, dga_ref, sm_ref, gb_ref, dp_in, dp_ref, dgb_ref):
        del dp_in
        i = pl.program_id(0)
        z = sm_ref[...] + gb_ref[...]
        lane = lax.broadcasted_iota(jnp.int32, z.shape, 1)
        row = lax.broadcasted_iota(jnp.int32, z.shape, 0) + i * TM
        valid = row >= NPADROWS
        dgl_v = dgl_ref[...]
        dgate = jnp.where(valid, jnp.where(lane < 4, dgl_v, dgl_v * _sigmoid(-z)), 0.0)
        ds = jnp.where(lane < 8, dgate, dga_ref[...])
        dp_ref[...] = ds.astype(dp_ref.dtype)
        part = jnp.sum(jnp.where(lane < 8, dgate, 0.0), axis=0, keepdims=True)

        @pl.when(i == 0)
        def _():
            dgb_ref[...] = part

        @pl.when(i > 0)
        def _():
            dgb_ref[...] += part

    return pl.pallas_call(
        body, name="small_bwd", grid=(R // TM,),
        in_specs=[_rb(TM, 128, 0), _rb(TM, 128, 0), _rb(TM, 128, SM0 // 128), _const((1, 128)),
                  pl.BlockSpec(memory_space=pl.ANY)],
        out_specs=[_rb(TM, 128, SM0 // 128), _const((1, 128))],
        out_shape=[jax.ShapeDtypeStruct((R, NP), MXU), jax.ShapeDtypeStruct((1, 128), F32)],
        input_output_aliases={4: 0},
        compiler_params=_cp(("arbitrary",)),
    )(dgl, dga, proj, gb_row, dproj)


def _masks(n=L):
    r = lax.broadcasted_iota(jnp.int32, (n, n), 0)
    c = lax.broadcasted_iota(jnp.int32, (n, n), 1)
    return r >= c, r == c, r


def _to_row(col, eye):
    return jnp.sum(jnp.where(eye, col, 0.0), axis=0, keepdims=True)


def _to_col(row, eye):
    return jnp.sum(jnp.where(eye, row, 0.0), axis=1, keepdims=True)


def _mlstm_chunk(q, k, logi_c, logf_c, m, n):
    tril, eye, _ = _masks(q.shape[0])
    logi_r, logf_r = _to_row(logi_c, eye), _to_row(logf_c, eye)
    b_c = jnp.sum(jnp.where(tril, logf_r, 0.0), axis=1, keepdims=True)
    b_r = _to_row(b_c, eye)
    g = jnp.sum(logf_c, axis=0, keepdims=True)
    dmat = jnp.where(tril, b_c - b_r + logi_r, NEG)
    mrow = jnp.maximum(b_c + m, jnp.max(dmat, axis=1, keepdims=True))
    dm = jnp.exp(dmat - mrow)
    s = _dot(q, k, 1, 1)
    w = dm * s
    a_in = jnp.exp(b_c + m - mrow)
    qn = jnp.sum(q * n, axis=1, keepdims=True)
    den = a_in * qn + jnp.sum(w, axis=1, keepdims=True)
    floor = jnp.exp(-mrow)
    nrm = jnp.maximum(jnp.abs(den), floor)
    wlog_c = g - b_c + logi_c
    m_new = jnp.maximum(g + m, jnp.max(wlog_c, axis=0, keepdims=True))
    a_st = jnp.exp(g + m - m_new)
    w_c = jnp.exp(wlog_c - m_new)
    return dict(b_c=b_c, g=g, dm=dm, s=s, w=w, a_in=a_in, qn=qn, den=den, floor=floor, nrm=nrm,
                m_new=m_new, a_st=a_st, w_c=w_c, tril=tril, eye=eye)


def _mlstm_head_fwd(h, glv, qk_ref, v_ref, mo_ref, hg_ref, hm_ref, ym_ref, cs_ref, nm_ref, c_s, nm_s):
    q = qk_ref[:, h * DQK:(h + 1) * DQK] * QSCALE
    k = qk_ref[:, 512 + h * DQK:512 + (h + 1) * DQK]
    v = v_ref[:, h * DV:(h + 1) * DV]
    C = c_s[h]
    n = nm_s[h, 0:1, :]
    m = nm_s[h, 1:2, 0:1]
    f = _mlstm_chunk(q, k, glv[:, h:h + 1], glv[:, 4 + h:5 + h], m, n)
    num = f["a_in"] * _dot(q, C, 1, 1) + _dot(f["w"], v, 1, 0)
    hh = num / f["nrm"]
    cs_ref[0, h] = C
    nm_ref[0, h] = nm_s[h]
    c_s[h] = f["a_st"] * C + _dot(f["w_c"] * v, k, 0, 0)
    n_new = f["a_st"] * n + jnp.sum(f["w_c"] * k, axis=0, keepdims=True)
    rowi = lax.broadcasted_iota(jnp.int32, (8, DQK), 0)
    nm_s[h] = jnp.where(rowi == 0, n_new, jnp.where(rowi == 1, f["m_new"], 0.0))
    rm = lax.rsqrt(jnp.mean(hh * hh, axis=-1, keepdims=True) + EPS)
    sl = slice(h * DV, (h + 1) * DV)
    hm_ref[:, sl] = hh
    ym_ref[:, sl] = (hh * rm * hg_ref[:, sl] * _sigmoid(mo_ref[:, sl])).astype(ym_ref.dtype)


def _mlstm_bwd_parts(dym_ref, hm_ref, qk_ref, cp_ref, v_ref, gl_ref, mo_ref, hg_ref, cs_ref, nm_ref,
                     dp_ref, dc_ref, dgl_ref, dhg_ref, dc_s, dn_s):
        def init():
            dc_s[...] = jnp.zeros_like(dc_s)
            dn_s[...] = jnp.zeros_like(dn_s)
            dhg_ref[...] = jnp.zeros_like(dhg_ref)

        def zero():
            dp_ref[...] = jnp.zeros_like(dp_ref)
            dc_ref[...] = jnp.zeros_like(dc_ref)
            dgl_ref[...] = jnp.zeros_like(dgl_ref)

        def compute(after_head):
            glv = gl_ref[...]
            lane = lax.broadcasted_iota(jnp.int32, (LM, 128), 1)
            dgl = jnp.zeros((LM, 128), F32)
            for h in range(NH):
                sl = slice(h * DV, (h + 1) * DV)
                sq = slice(h * DQK, (h + 1) * DQK)
                sk = slice(512 + h * DQK, 512 + (h + 1) * DQK)
                hh = hm_ref[:, sl]
                gain = hg_ref[:, sl]
                rm = lax.rsqrt(jnp.mean(hh * hh, axis=-1, keepdims=True) + EPS)
                sg = _sigmoid(mo_ref[:, sl])
                dyv = dym_ref[:, sl]
                dno = dyv * sg
                dp_ref[:, 1024 + h * DV:1024 + (h + 1) * DV] = (dyv * hh * rm * gain * sg * (1.0 - sg)).astype(dp_ref.dtype)
                dhg_ref[:, sl] += jnp.sum(dno * hh * rm, axis=0, keepdims=True)
                dnog = dno * gain
                dh = rm * dnog - hh * (rm * rm * rm * jnp.mean(dnog * hh, axis=-1, keepdims=True))
                q = qk_ref[:, sq] * QSCALE
                k = qk_ref[:, sk]
                v = v_ref[:, sl]
                C = cs_ref[0, h]
                n = nm_ref[0, h, 0:1, :]
                m = nm_ref[0, h, 1:2, 0:1]
                f = _mlstm_chunk(q, k, glv[:, h:h + 1], glv[:, 4 + h:5 + h], m, n)
                eye = f["eye"]
                a_in, nrm, den, w = f["a_in"], f["nrm"], f["den"], f["w"]
                dnum = dh / nrm
                dnrm = -jnp.sum(dh * hh, axis=1, keepdims=True) / nrm
                dden = jnp.where(jnp.abs(den) >= f["floor"], dnrm * jnp.sign(den), 0.0)
                dw = _dot(dnum, v, 1, 1) + dden
                dv = _dot(w, dnum, 0, 0)
                ds = dw * f["dm"]
                e = dw * w
                qc = _dot(q, C, 1, 1)
                dq = _dot(ds, k, 1, 0) + a_in * _dot(dnum, C, 1, 0) + (a_in * dden) * n
                dk = _dot(ds, q, 0, 0)
                dC_in = _dot(a_in * dnum, q, 0, 0)
                dn_in = jnp.sum((a_in * dden) * q, axis=0, keepdims=True)
                da_in = jnp.sum(dnum * qc, axis=1, keepdims=True) + dden * f["qn"]
                col_e = _to_col(jnp.sum(e, axis=0, keepdims=True), eye)
                db = jnp.sum(e, axis=1, keepdims=True) + da_in * a_in - col_e
                dlogi = col_e
                dCp = dc_s[h]
                dnp = dn_s[h, 0:1, :]
                a_st, w_c = f["a_st"], f["w_c"]
                da_st = (jnp.sum(jnp.sum(dCp * C, axis=1, keepdims=True), axis=0, keepdims=True)
                         + jnp.sum(dnp * n, axis=1, keepdims=True))
                vdc = _dot(v, dCp, 1, 0)
                dw_c = jnp.sum((vdc + dnp) * k, axis=1, keepdims=True)
                dv = dv + w_c * _dot(k, dCp, 1, 1)
                dk = dk + w_c * (vdc + dnp)
                fw = dw_c * w_c
                dg = jnp.sum(fw, axis=0, keepdims=True) + da_st * a_st
                db = db - fw
                dlogi = dlogi + fw
                rowc = lax.broadcasted_iota(jnp.int32, (LM, 1), 0)
                db = db + jnp.where(rowc == LM - 1, dg, 0.0)
                triu = lax.broadcasted_iota(jnp.int32, (LM, LM), 1) >= lax.broadcasted_iota(jnp.int32, (LM, LM), 0)
                dlogf = jnp.sum(jnp.where(triu, _to_row(db, eye), 0.0), axis=1, keepdims=True)
                dc_s[h] = a_st * dCp + dC_in
                dn_new = a_st * dnp + dn_in
                dn_s[h] = jnp.zeros((8, DQK), F32) + dn_new
                cq, ck = cp_ref[:, sq], cp_ref[:, sk]
                s_q, s_k = _sigmoid(cq), _sigmoid(ck)
                dc_ref[:, sq] = dq * QSCALE * s_q * (1.0 + cq * (1.0 - s_q))
                dc_ref[:, sk] = dk * s_k * (1.0 + ck * (1.0 - s_k))
                dp_ref[:, sl] = dv.astype(dp_ref.dtype)
                dgl = jnp.where(lane == h, dlogi, jnp.where(lane == 4 + h, dlogf, dgl))
                after_head(h)
            dgl_ref[...] = dgl

        return init, zero, compute


def _gla_logs(sm, a2p, b2, valid):
    za = _dot(sm, a2p, 1, 0) + b2
    return za, jnp.where(valid, _log_sigmoid(za) * (1.0 / TAU), 0.0)


def _valid_rows(c, width):
    row = lax.broadcasted_iota(jnp.int32, (L, width), 0) + c * L
    return row >= NPADROWS


def _gla_cumsum(loga):
    tril, _, _ = _masks()
    return _dot_exact(tril.astype(F32), loga)


def _gla_chunk(q, k, la, bc):
    tril, _, _ = _masks()
    btot = jnp.sum(la, axis=0, keepdims=True)
    ebc = jnp.exp(bc)
    qd = q * ebc
    ki = k * jnp.exp(-bc)
    ke = k * jnp.exp(btot - bc)
    att = jnp.where(tril, _dot(qd, ki, 1, 1), 0.0)
    return dict(tril=tril, bc=bc, btot=btot, ebc=ebc, qd=qd, ki=ki, ke=ke, att=att)


def _col128(row):
    r = lax.broadcasted_iota(jnp.int32, (DQK, DQK), 0)
    c = lax.broadcasted_iota(jnp.int32, (DQK, DQK), 1)
    return jnp.sum(jnp.where(r == c, row, 0.0), axis=1, keepdims=True)


def _gla_fwd_parts(c, q_ref, k_ref, v_ref, gr_ref, sm_ref, a2_ref, b2_ref, hg_ref, hgl_ref, yg_ref, ss_ref, s_s):
        def init():
            s_s[...] = jnp.zeros_like(s_s)

        def zero():
            hgl_ref[...] = jnp.zeros_like(hgl_ref)
            yg_ref[...] = jnp.zeros_like(yg_ref)
            ss_ref[...] = jnp.zeros_like(ss_ref)

        def compute(s):
            rows = pl.ds(s * L, L)
            chunk(LM // L * c + s, q_ref.at[rows], k_ref.at[rows], v_ref.at[rows], gr_ref.at[rows], sm_ref.at[rows],
                  hgl_ref.at[rows], yg_ref.at[rows], ss_ref.at[pl.ds(s, 1)])

        def chunk(c, q_ref, k_ref, v_ref, gr_ref, sm_ref, hgl_ref, yg_ref, ss_ref):
            _, loga = _gla_logs(sm_ref[...], a2_ref[...], b2_ref[...], _valid_rows(c, 512))
            bc_all = _gla_cumsum(loga)
            for h in range(NH):
                sq = slice(h * DQK, (h + 1) * DQK)
                sl = slice(h * DV, (h + 1) * DV)
                q = q_ref[:, sq] * QSCALE
                k = k_ref[:, sq]
                v = v_ref[:, sl]
                S = s_s[h]
                f = _gla_chunk(q, k, loga[:, sq], bc_all[:, sq])
                o = _dot(f["att"], v, 1, 0) + _dot(f["qd"], S, 1, 0)
                ss_ref[0, h] = S
                s_s[h] = _col128(jnp.exp(f["btot"])) * S + _dot(f["ke"], v, 0, 0)
                rg = lax.rsqrt(jnp.mean(o * o, axis=-1, keepdims=True) + EPS)
                gr = gr_ref[:, sl]
                hgl_ref[:, sl] = o
                yg_ref[:, sl] = (o * rg * hg_ref[:, sl] * gr * _sigmoid(gr)).astype(yg_ref.dtype)

        return init, zero, compute


def _gla_bwd_parts(c, dy_ref, ho_ref, q_ref, k_ref, v_ref, gr_ref, sm_ref, a2_ref, b2_ref, hg_ref, ss_ref,
                   dp_ref, dga_ref, da2_ref, db2_ref, dhg_ref, ds_s):
        def init():
            ds_s[...] = jnp.zeros_like(ds_s)
            da2_ref[...] = jnp.zeros_like(da2_ref)
            db2_ref[...] = jnp.zeros_like(db2_ref)
            dhg_ref[...] = jnp.zeros_like(dhg_ref)

        def zero():
            dp_ref[...] = jnp.zeros_like(dp_ref)
            dga_ref[...] = jnp.zeros_like(dga_ref)

        def compute(s):
            rows = pl.ds(s * L, L)
            chunk(LM // L * c + s, dy_ref.at[rows], ho_ref.at[rows], q_ref.at[rows], k_ref.at[rows], v_ref.at[rows],
                  gr_ref.at[rows], sm_ref.at[rows], ss_ref.at[pl.ds(s, 1)], dp_ref.at[rows], dga_ref.at[rows])

        def chunk(c, dy_ref, ho_ref, q_ref, k_ref, v_ref, gr_ref, sm_ref, ss_ref, dp_ref, dga_ref):
            valid = _valid_rows(c, 512)
            sm = sm_ref[...]
            za, loga = _gla_logs(sm, a2_ref[...], b2_ref[...], valid)
            dbcs = []
            for h in range(NH):
                sq = slice(h * DQK, (h + 1) * DQK)
                sl = slice(h * DV, (h + 1) * DV)
                o = ho_ref[:, sl]
                gain = hg_ref[:, sl]
                rg = lax.rsqrt(jnp.mean(o * o, axis=-1, keepdims=True) + EPS)
                gr = gr_ref[:, sl]
                sg = _sigmoid(gr)
                dyv = dy_ref[:, sl]
                dno = dyv * gr * sg
                dp_ref[:, 2048 + h * DV:2048 + (h + 1) * DV] = (
                    dyv * o * rg * gain * sg * (1.0 + gr * (1.0 - sg))).astype(dp_ref.dtype)
                dhg_ref[:, sl] += jnp.sum(dno * o * rg, axis=0, keepdims=True)
                dnog = dno * gain
                do = rg * dnog - o * (rg * rg * rg * jnp.mean(dnog * o, axis=-1, keepdims=True))
                q = q_ref[:, sq] * QSCALE
                k = k_ref[:, sq]
                v = v_ref[:, sl]
                S = ss_ref[0, h]
                f = _gla_chunk(q, k, loga[:, sq], _gla_cumsum(loga[:, sq]))
                tril, qd, ki, ke = f["tril"], f["qd"], f["ki"], f["ke"]
                dSp = ds_s[h]
                datt = jnp.where(tril, _dot(do, v, 1, 1), 0.0)
                dqd = _dot(do, S, 1, 1) + _dot(datt, ki, 1, 0)
                dki = _dot(datt, qd, 0, 0)
                dv = _dot(f["att"], do, 0, 0) + _dot(ke, dSp, 1, 0)
                dke = _dot(v, dSp, 1, 1)
                ebt = jnp.exp(f["btot"])
                dbtot = jnp.sum(dke * ke, axis=0, keepdims=True) + ebt * _to_row128(jnp.sum(dSp * S, axis=1, keepdims=True))
                ds_s[h] = _dot(qd, do, 0, 0) + _col128(ebt) * dSp
                dq = dqd * f["ebc"]
                dk = dki * jnp.exp(-f["bc"]) + dke * jnp.exp(f["btot"] - f["bc"])
                dbc = dqd * qd - dki * ki - dke * ke
                rowc = lax.broadcasted_iota(jnp.int32, (L, DQK), 0)
                dbc = dbc + jnp.where(rowc == L - 1, dbtot, 0.0)
                triu = lax.broadcasted_iota(jnp.int32, (L, L), 1) >= lax.broadcasted_iota(jnp.int32, (L, L), 0)
                dbcs.append(_dot_exact(triu.astype(F32), dbc))
                dp_ref[:, sq] = (dq * QSCALE).astype(dp_ref.dtype)
                dp_ref[:, 512 + h * DQK:512 + (h + 1) * DQK] = dk.astype(dp_ref.dtype)
                dp_ref[:, 1024 + h * DV:1024 + (h + 1) * DV] = dv.astype(dp_ref.dtype)
            dza = jnp.where(valid, jnp.concatenate(dbcs, axis=1) * (1.0 / TAU) * _sigmoid(-za), 0.0)
            dga_ref[...] = _dot(dza, a2_ref[...], 1, 1)
            da2_ref[...] += _dot(sm, dza, 0, 0)
            db2_ref[...] += jnp.sum(dza, axis=0, keepdims=True)

        return init, zero, compute


def mix_fwd(qk, proj, gl, m_head_g, a2p, b2, g_head_g):
    R = qk.shape[0]
    NC = R // LM
    G = LM // L

    def body(qk_ref, mv_ref, gl_ref, mo_ref, mhg_ref, gq_ref, gk_ref, gv_ref, gr_ref, sm_ref, a2_ref, b2_ref, ghg_ref,
             hm_ref, ym_ref, cs_ref, nm_ref, hgl_ref, yg_ref, ss_ref, c_s, nm_s, s_s):
        c = pl.program_id(0)
        g_init, g_zero, g_compute = _gla_fwd_parts(c, gq_ref, gk_ref, gv_ref, gr_ref, sm_ref, a2_ref, b2_ref, ghg_ref,
                                                   hgl_ref, yg_ref, ss_ref, s_s)

        @pl.when(c <= CH0)
        def _():
            c_s[...] = jnp.zeros_like(c_s)
            nm_s[...] = jnp.zeros_like(nm_s)
            g_init()

        @pl.when(c < CH0)
        def _():
            hm_ref[...] = jnp.zeros_like(hm_ref)
            ym_ref[...] = jnp.zeros_like(ym_ref)
            cs_ref[...] = jnp.zeros_like(cs_ref)
            nm_ref[...] = jnp.zeros_like(nm_ref)
            g_zero()

        @pl.when(c >= CH0)
        def _():
            glv = gl_ref[...]
            assert LM // L == NH
            for h in range(NH):
                g_compute(h)
                _mlstm_head_fwd(h, glv, qk_ref, mv_ref, mo_ref, mhg_ref, hm_ref, ym_ref, cs_ref, nm_ref, c_s, nm_s)

    st_m = pl.BlockSpec((1, NH, DV, DQK), lambda c: (c, 0, 0, 0))
    st_n = pl.BlockSpec((1, NH, 8, DQK), lambda c: (c, 0, 0, 0))
    st_g = pl.BlockSpec((G, NH, DQK, DV), lambda c: (c, 0, 0, 0))
    return pl.pallas_call(
        body, name="mix_fwd", grid=(NC,),
        in_specs=[_rb(LM, 1024, 0), _rb(LM, 1024, MV0 // 1024), _rb(LM, 128, 0), _rb(LM, 1024, MO0 // 1024),
                  _const((1, 1024)),
                  _rb(LM, 512, GQ0 // 512), _rb(LM, 512, GK0 // 512), _rb(LM, 1024, GV0 // 1024),
                  _rb(LM, 1024, GR0 // 1024), _rb(LM, 128, SM0 // 128), _const((128, 512)), _const((1, 512)),
                  _const((1, 1024))],
        out_specs=[_rb(LM, 1024, 0), _rb(LM, 1024, 0), st_m, st_n, _rb(LM, 1024, 0), _rb(LM, 1024, 0), st_g],
        out_shape=[jax.ShapeDtypeStruct((R, 1024), F32), jax.ShapeDtypeStruct((R, 1024), MXU),
                   jax.ShapeDtypeStruct((NC, NH, DV, DQK), F32), jax.ShapeDtypeStruct((NC, NH, 8, DQK), F32),
                   jax.ShapeDtypeStruct((R, 1024), F32), jax.ShapeDtypeStruct((R, 1024), MXU),
                   jax.ShapeDtypeStruct((G * NC, NH, DQK, DV), F32)],
        scratch_shapes=[pltpu.VMEM((NH, DV, DQK), F32), pltpu.VMEM((NH, 8, DQK), F32),
                        pltpu.VMEM((NH, DQK, DV), F32)],
        compiler_params=_cp(("arbitrary",)),
    )(qk, proj, gl, proj, m_head_g, proj, proj, proj, proj, proj, a2p, b2, g_head_g)


def mix_bwd(dym, hm, qk, cpre, proj, gl, m_head_g, cs, nm, dyg, hgl, a2p, b2, g_head_g, ss, dproj):
    R = qk.shape[0]
    NC = R // LM
    rev = lambda c: NC - 1 - c
    GW = GR0 + 1024 - GQ0

    def body(dym_ref, hm_ref, qk_ref, cp_ref, mv_ref, gl_ref, mo_ref, mhg_ref, cs_ref, nm_ref,
             dyg_ref, ho_ref, gq_ref, gk_ref, gv_ref, gr_ref, sm_ref, a2_ref, b2_ref, ghg_ref, ss_ref, dp_in,
             dp_ref, dc_ref, dgl_ref, dmhg_ref, dga_ref, da2_ref, db2_ref, dghg_ref, dc_s, dn_s, ds_s):
        del dp_in
        step = pl.program_id(0)
        c = NC - 1 - step
        m_init, m_zero, m_compute = _mlstm_bwd_parts(
            dym_ref, hm_ref, qk_ref, cp_ref, mv_ref, gl_ref, mo_ref, mhg_ref, cs_ref, nm_ref,
            dp_ref.at[:, 0:GQ0], dc_ref, dgl_ref, dmhg_ref, dc_s, dn_s)
        g_init, g_zero, g_compute = _gla_bwd_parts(
            c, dyg_ref, ho_ref, gq_ref, gk_ref, gv_ref, gr_ref, sm_ref, a2_ref, b2_ref, ghg_ref, ss_ref,
            dp_ref.at[:, GQ0:GQ0 + GW], dga_ref, da2_ref, db2_ref, dghg_ref, ds_s)

        @pl.when(step == 0)
        def _():
            m_init()
            g_init()

        @pl.when(c < CH0)
        def _():
            m_zero()
            g_zero()

        @pl.when(c >= CH0)
        def _():
            assert LM // L == NH
            m_compute(lambda h: g_compute(NH - 1 - h))

    def rows(w, cb):
        return pl.BlockSpec((LM, w), lambda c: (rev(c), cb))

    return pl.pallas_call(
        body, name="mix_bwd", grid=(NC,),
        in_specs=[rows(1024, 0), rows(1024, 0), rows(1024, 0), rows(1024, 0), rows(1024, MV0 // 1024), rows(128, 0),
                  rows(1024, MO0 // 1024), _const((1, 1024)),
                  pl.BlockSpec((1, NH, DV, DQK), lambda c: (rev(c), 0, 0, 0)),
                  pl.BlockSpec((1, NH, 8, DQK), lambda c: (rev(c), 0, 0, 0)),
                  rows(1024, 0), rows(1024, 0), rows(512, GQ0 // 512), rows(512, GK0 // 512),
                  rows(1024, GV0 // 1024), rows(1024, GR0 // 1024), rows(128, SM0 // 128),
                  _const((128, 512)), _const((1, 512)), _const((1, 1024)),
                  pl.BlockSpec((LM // L, NH, DQK, DV), lambda c: (rev(c), 0, 0, 0)),
                  pl.BlockSpec(memory_space=pl.ANY)],
        out_specs=[rows(GQ0 + GW, 0), rows(1024, 0), rows(128, 0), _const((1, 1024)),
                   rows(128, 0), _const((128, 512)), _const((1, 512)), _const((1, 1024))],
        out_shape=[jax.ShapeDtypeStruct((R, NP), MXU), jax.ShapeDtypeStruct((R, 1024), F32),
                   jax.ShapeDtypeStruct((R, 128), F32), jax.ShapeDtypeStruct((1, 1024), F32),
                   jax.ShapeDtypeStruct((R, 128), F32), jax.ShapeDtypeStruct((128, 512), F32),
                   jax.ShapeDtypeStruct((1, 512), F32), jax.ShapeDtypeStruct((1, 1024), F32)],
        scratch_shapes=[pltpu.VMEM((NH, DV, DQK), F32), pltpu.VMEM((NH, 8, DQK), F32),
                        pltpu.VMEM((NH, DQK, DV), F32)],
        input_output_aliases={21: 0},
        compiler_params=_cp(("arbitrary",)),
    )(dym, hm, qk, cpre, proj, gl, proj, m_head_g, cs, nm, dyg, hgl, proj, proj, proj, proj, proj, a2p, b2,
      g_head_g, ss, dproj)


def _to_row128(col):
    r = lax.broadcasted_iota(jnp.int32, (DQK, DQK), 0)
    c = lax.broadcasted_iota(jnp.int32, (DQK, DQK), 1)
    return jnp.sum(jnp.where(r == c, col, 0.0), axis=0, keepdims=True)


def local_step(x, target, meta, norm1_g, wp, conv_w, conv_b, m_gate_b, g_a2, g_a2_b, m_head_g, g_head_g,
               norm2_g, final_g, late_weights, send_early, send_wp, first_order=None):
    seq = x.shape[0]
    assert seq % TM == 0
    gb_row = jnp.zeros((1, 128), F32).at[0, 0:8].set(m_gate_b.reshape(8))
    a2p = jnp.zeros((128, 512), F32).at[8:8 + RANK].set(g_a2)
    mhg = m_head_g.reshape(1, 1024)
    ghg = g_head_g.reshape(1, 1024)

    xn = rms_fwd_input(x, meta, norm1_g, "rms1_fwd")
    proj = matmul(xn, wp, "nn", "proj_fwd", tm=1536, order=first_order)
    cpre, qk, gl = prep_fwd(proj, conv_w, conv_b, gb_row)
    hm, ym, cs, nm, hgl, yg, ss = mix_fwd(qk, proj, gl, mhg, a2p, g_a2_b, ghg)
    w_bm, w_bg, w_out, w_gu, w_down = late_weights(ym)
    bm, bg, merged = merge_fwd(ym, yg, w_bm, w_bg, proj)
    h1, hn = out_proj_norm(merged, w_out, x, meta, norm2_g)
    au, ff = ff_in_fwd(hn, w_gu)
    dh2, loss, d_final_g = ff_down_loss(ff, w_down, h1, final_g.reshape(1, D), target)

    d_w_down = matmul(ff, dh2, "tn", "ff_down_wgrad", tm=1408, tk=1536)
    dau = ff_down_dgrad(dh2, w_down, au)
    d_w_gu = matmul(hn, dau, "tn", "ff_in_wgrad", tm=1024, tn=1408, tk=2816)
    dh1, d_norm2_g = dgrad_rms_bwd(dau, w_gu, h1, norm2_g, dh2, "ff_in_dgrad", 512, 2 * DFF)

    d_w_out = matmul(merged, dh1, "tn", "out_wgrad", tm=1024, tk=1536)
    dbm, dbg, dproj = merge_bwd(dh1, w_out, bm, bg, proj)
    d_w_bm = matmul(ym, dbm, "tn", "branch_m_wgrad", tm=1024, tk=2816)
    d_w_bg = matmul(yg, dbg, "tn", "branch_g_wgrad", tm=1024, tk=2816)
    token = send_early(dict(w_branch_m=d_w_bm, w_branch_g=d_w_bg, w_out=d_w_out, w_gu=d_w_gu, w_ff_down=d_w_down))
    dym = matmul(dbm, w_bm, "nt", "branch_m_dgrad", order=token)
    dyg = matmul(dbg, w_bg, "nt", "branch_g_dgrad")
    dproj, dc, dgl, d_mhg, dga, d_a2p, d_a2b, d_ghg = mix_bwd(
        dym, hm, qk, cpre, proj, gl, mhg, cs, nm, dyg, hgl, a2p, g_a2_b, ghg, ss, dproj)
    dproj, d_conv = conv_bwd(dc, proj, conv_w, dproj)
    dproj, d_gb = small_bwd(dgl, dga, proj, gb_row, dproj)
    d_wp = matmul(xn, dproj, "tn", "proj_wgrad", tm=1024, tn=1664, tk=2816)
    token = send_wp(d_wp)
    dxn = matmul(dproj, wp, "nt", "proj_dgrad", tm=256, tk=NP, order=token)
    grad_x, d_meta, d_norm1_g = rms_bwd_input(dxn, x, meta, norm1_g, dh1, "rms1_bwd")

    grads = dict(
        meta_tokens=d_meta, norm1_g=d_norm1_g, conv_w=d_conv[0:4], conv_b=d_conv[4:5], m_gate_b=d_gb[0, 0:8].reshape(1, 2, 4),
        g_a2=d_a2p[8:8 + RANK], g_a2_b=d_a2b, m_head_g=d_mhg.reshape(NH, DV), g_head_g=d_ghg.reshape(NH, DV),
        norm2_g=d_norm2_g, final_g=d_final_g)
    return loss, grad_x, grads


_SEGS = [(0, 1024, QK0), (1024, 2048, MV0), (2048, 2056, SM0), (2056, 3080, MO0), (3080, 5128, GQ0),
         (5128, 5144, SM0 + 8), (5144, 6168, GR0), (6168, 8216, GM0)]
SHARD_W = NPROJ // NDEV


def regroup_cols(w8):
    parts = []
    for lo, hi, _ in sorted(_SEGS, key=lambda s: s[2]):
        while lo < hi:
            j = lo // SHARD_W
            end = min(hi, (j + 1) * SHARD_W)
            parts.append(w8[j, :, lo - j * SHARD_W:end - j * SHARD_W])
            lo = end
    parts.append(jnp.zeros((w8.shape[1], NP - NPROJ), w8.dtype))
    return jnp.concatenate(parts, axis=1)


def ungroup_cols(g):
    blocks = []
    for j in range(NDEV):
        lo, hi = j * SHARD_W, (j + 1) * SHARD_W
        parts = []
        for s_lo, s_hi, s_at in _SEGS:
            a, b = max(lo, s_lo), min(hi, s_hi)
            if a < b:
                parts.append(g[:, s_at + a - s_lo:s_at + b - s_lo])
        blocks.append(jnp.concatenate(parts, axis=1))
    return jnp.stack(blocks)


def col_blocks(g):
    r, c8 = g.shape
    return jnp.transpose(g.reshape(r, NDEV, c8 // NDEV), (1, 0, 2))


def from_col_blocks(g8):
    n, r, c = g8.shape
    return jnp.transpose(g8, (1, 0, 2)).reshape(r, n * c)


_MESHID = pl.DeviceIdType.MESH
_RELS = [(0, 0, 1), (1, 0, 0), (0, 1, 0), (1, 1, 0), (1, 0, 1), (0, 1, 1), (1, 1, 1)]


def _flip(v, bit):
    return 1 - v if bit else v


def all_gather(arrs, name):
    n = len(arrs)

    def body(*refs):
        ins, outs = refs[:n], refs[n:2 * n]
        send_sems, recv_sems, local_sems = refs[2 * n:]
        x, y, c = lax.axis_index("x"), lax.axis_index("y"), lax.axis_index("c")
        me, sibling = (x, y, c), (x, y, 1 - c)
        chips = [(1 - x, y), (x, 1 - y), (1 - x, 1 - y)]

        def slot(p):
            return 4 * p[0] + 2 * p[1] + p[2]

        def copy(a, k, block, to, src=None):
            dst = outs[a].at[slot(block)]
            return pltpu.make_async_remote_copy(
                src_ref=dst if src is None else src, dst_ref=dst,
                send_sem=send_sems.at[a, k], recv_sem=recv_sems.at[a, k],
                device_id=to, device_id_type=_MESHID)

        mine = [pltpu.make_async_copy(ins[a], outs[a].at[slot(me)], local_sems.at[a]) for a in range(n)]
        for cp in mine:
            cp.start()
        first = []
        for a in range(n):
            first.append(copy(a, 0, me, sibling, src=ins[a]))
            first += [copy(a, 1 + j, me, (*chip, c), src=ins[a]) for j, chip in enumerate(chips)]
        for cp in first:
            cp.start()
        passed = []
        for j, chip in enumerate(chips):
            for a in range(n):
                copy(a, 1 + j, (*chip, c), me).wait_recv()
                fwd = copy(a, 4 + j, (*chip, c), sibling)
                fwd.start()
                passed.append(fwd)
        for a in range(n):
            copy(a, 0, sibling, me).wait_recv()
            for j, chip in enumerate(chips):
                copy(a, 4 + j, (*chip, 1 - c), me).wait_recv()
        for cp in first + passed:
            cp.wait_send()
        for cp in mine:
            cp.wait()

    anyspec = pl.BlockSpec(memory_space=pl.ANY)
    return pl.pallas_call(
        body, name=name,
        in_specs=[anyspec] * n, out_specs=[anyspec] * n,
        out_shape=[jax.ShapeDtypeStruct((NDEV,) + a.shape, a.dtype) for a in arrs],
        scratch_shapes=[pltpu.SemaphoreType.DMA((n, 7)), pltpu.SemaphoreType.DMA((n, 7)),
                        pltpu.SemaphoreType.DMA((n,))],
    )(*arrs)


def exchange(blocks, rep, name):
    n = len(blocks)

    def body(*refs):
        b_refs, r_ref = refs[:n], refs[n]
        ob_refs, or_ref = refs[n + 1:2 * n + 1], refs[2 * n + 1]
        send_sems, recv_sems, local_sems = refs[2 * n + 2:]
        x, y, c = lax.axis_index("x"), lax.axis_index("y"), lax.axis_index("c")
        me = 4 * x + 2 * y + c

        def pairs(src_slot, dst_slot):
            return [(b_refs[a].at[src_slot], ob_refs[a].at[dst_slot]) for a in range(n)] + [(r_ref, or_ref.at[dst_slot])]

        loc = [pltpu.make_async_copy(s, d, local_sems.at[a]) for a, (s, d) in enumerate(pairs(me, me))]
        for cp in loc:
            cp.start()
        sends = []
        for k, (fx, fy, fc) in enumerate(_RELS):
            peer = (_flip(x, fx), _flip(y, fy), _flip(c, fc))
            pid = 4 * peer[0] + 2 * peer[1] + peer[2]
            for a, (s, d) in enumerate(pairs(pid, me)):
                sends.append(pltpu.make_async_remote_copy(
                    src_ref=s, dst_ref=d, send_sem=send_sems.at[a, k], recv_sem=recv_sems.at[a, k],
                    device_id=peer, device_id_type=_MESHID))
        for cp in sends:
            cp.start()
        for k, (fx, fy, fc) in enumerate(_RELS):
            peer = (_flip(x, fx), _flip(y, fy), _flip(c, fc))
            pid = 4 * peer[0] + 2 * peer[1] + peer[2]
            for a, (s, d) in enumerate(pairs(pid, pid)):
                pltpu.make_async_remote_copy(
                    src_ref=s, dst_ref=d, send_sem=send_sems.at[a, k], recv_sem=recv_sems.at[a, k],
                    device_id=peer, device_id_type=_MESHID).wait_recv()
        for cp in sends:
            cp.wait_send()
        for cp in loc:
            cp.wait()

    anyspec = pl.BlockSpec(memory_space=pl.ANY)
    return pl.pallas_call(
        body, name=name,
        in_specs=[anyspec] * (n + 1), out_specs=[anyspec] * (n + 1),
        out_shape=[jax.ShapeDtypeStruct(b.shape, b.dtype) for b in blocks]
        + [jax.ShapeDtypeStruct((NDEV,) + rep.shape, rep.dtype)],
        scratch_shapes=[pltpu.SemaphoreType.DMA((n + 1, 7)), pltpu.SemaphoreType.DMA((n + 1, 7)),
                        pltpu.SemaphoreType.DMA((n + 1,))],
    )(*blocks, rep)


_HBM = pl.BlockSpec(memory_space=pltpu.HBM)
_SEM = pl.BlockSpec(memory_space=pltpu.SEMAPHORE)
_EFFECT = pltpu.SideEffectType.DATAFLOW_SIDE_EFFECTING


def _peer_ids():
    x, y, c = lax.axis_index("x"), lax.axis_index("y"), lax.axis_index("c")
    peers = []
    for fx, fy, fc in _RELS:
        p = (_flip(x, fx), _flip(y, fy), _flip(c, fc))
        peers.append((p, 4 * p[0] + 2 * p[1] + p[2]))
    return 4 * x + 2 * y + c, peers


def _split_copy(src, land, a, k, peer, src_slot, dst_slot, send_sems, recv_sems):
    return pltpu.make_async_remote_copy(
        src_ref=src if src_slot is None else src.at[src_slot], dst_ref=land.at[dst_slot],
        send_sem=send_sems.at[7 * a + k], recv_sem=recv_sems.at[7 * a + k], device_id=peer, device_id_type=_MESHID)


def _own_copy(src, land, a, n, me, per_peer, send_sems):
    return pltpu.make_async_copy(src.at[me] if per_peer else src, land.at[me], send_sems.at[7 * n + a])


def send_start(srcs, per_peer, order, name):
    n = len(srcs)
    lands = [lax.empty((NDEV,) + (s.shape[1:] if per_peer else s.shape), s.dtype) for s in srcs]

    def body(*refs):
        src_refs, land_refs = refs[1:1 + n], refs[1 + n:1 + 2 * n]
        send_sems, recv_sems = refs[1 + 2 * n], refs[2 + 2 * n]
        token = refs[3 + 4 * n]
        me, peers = _peer_ids()
        for a in range(n):
            _own_copy(src_refs[a], land_refs[a], a, n, me, per_peer, send_sems).start()
        for a in range(n):
            for k, (peer, pid) in enumerate(peers):
                _split_copy(src_refs[a], land_refs[a], a, k, peer, pid if per_peer else None, me,
                            send_sems, recv_sems).start()
        token[...] = jnp.zeros_like(token)

    outs = pl.pallas_call(
        body, name=name,
        in_specs=[pl.BlockSpec(memory_space=pl.ANY)] + [_HBM] * (2 * n),
        out_shape=(pltpu.SemaphoreType.DMA((8 * n,)), pltpu.SemaphoreType.DMA((7 * n,)),
                   *[pltpu.HBM(s.shape, s.dtype) for s in srcs], *[pltpu.HBM(l.shape, l.dtype) for l in lands],
                   jax.ShapeDtypeStruct((8, 128), F32)),
        out_specs=(_SEM, _SEM, *[_HBM] * (2 * n), pl.BlockSpec(memory_space=pltpu.VMEM)),
        input_output_aliases={1 + i: 2 + i for i in range(2 * n)},
        compiler_params=pltpu.CompilerParams(has_side_effects=_EFFECT),
    )(order, *[pltpu.with_memory_space_constraint(s, pltpu.HBM) for s in srcs],
      *[pltpu.with_memory_space_constraint(l, pltpu.HBM) for l in lands])
    return (n, per_peer, outs[0], outs[1], outs[2:2 + n], outs[2 + n:2 + 2 * n]), outs[2 + 2 * n]


def send_wait(handle, after, name):
    n, per_peer, send_sems, recv_sems, src_thru, land_thru = handle

    def body(*refs):
        src_refs, land_refs = refs[:n], refs[n:2 * n]
        s_sems, r_sems = refs[2 * n], refs[2 * n + 1]
        me, peers = _peer_ids()
        for a in range(n):
            _own_copy(src_refs[a], land_refs[a], a, n, me, per_peer, s_sems).wait()
            for k, (peer, pid) in enumerate(peers):
                cp = _split_copy(src_refs[a], land_refs[a], a, k, peer, pid if per_peer else None, pid, s_sems, r_sems)
                cp.wait_send()
                cp.wait_recv()

    outs = pl.pallas_call(
        body, name=name,
        in_specs=[_HBM] * (2 * n) + [_SEM, _SEM, pl.BlockSpec(memory_space=pl.ANY)],
        out_shape=tuple(pltpu.HBM(t.shape, t.dtype) for t in (*src_thru, *land_thru)),
        out_specs=tuple([_HBM] * (2 * n)),
        input_output_aliases={i: i for i in range(2 * n)},
        compiler_params=pltpu.CompilerParams(has_side_effects=_EFFECT),
    )(*src_thru, *land_thru, send_sems, recv_sems, after)
    return list(outs[n:2 * n])


def adamw(parts, w, m, v, name, tr):
    npart, r, c = parts.shape
    c1 = 1.0 - ADAM_B1 ** ADAM_STEP
    c2 = 1.0 - ADAM_B2 ** ADAM_STEP

    def body(p_ref, w_ref, m_ref, v_ref, g_ref, d_ref, nm_ref, nv_ref):
        g = p_ref[0].astype(F32)
        for j in range(1, npart):
            g = g + p_ref[j].astype(F32)
        mn = ADAM_B1 * m_ref[...] + (1.0 - ADAM_B1) * g
        vn = ADAM_B2 * v_ref[...] + (1.0 - ADAM_B2) * (g * g)
        g_ref[...] = g
        nm_ref[...] = mn
        nv_ref[...] = vn
        d_ref[...] = -ADAM_LR * ((mn / c1) / (jnp.sqrt(vn / c2) + ADAM_EPS) + ADAM_WD * w_ref[...])

    spec = _rb(tr, c, 0)
    return pl.pallas_call(
        body, name=name, grid=(r // tr,),
        in_specs=[pl.BlockSpec((npart, tr, c), lambda i: (0, i, 0)), spec, spec, spec],
        out_specs=[spec] * 4, out_shape=[jax.ShapeDtypeStruct((r, c), F32)] * 4,
        compiler_params=_cp(("parallel",)),
    )(parts, w, m, v)


def sum_parts(parts, name, tc):
    npart, r, c = parts.shape

    def body(p_ref, o_ref):
        g = p_ref[0].astype(F32)
        for j in range(1, npart):
            g = g + p_ref[j].astype(F32)
        o_ref[...] = g

    return pl.pallas_call(
        body, name=name, grid=(c // tc,),
        in_specs=[pl.BlockSpec((npart, r, tc), lambda i: (0, 0, i))],
        out_specs=pl.BlockSpec((r, tc), lambda i: (0, i)),
        out_shape=jax.ShapeDtypeStruct((r, c), F32),
        compiler_params=_cp(("parallel",)),
    )(parts)


TINY = [("meta_tokens", (16, 1024)), ("conv_w", (4, 1024)), ("g_a2", (16, 512)), ("m_head_g", (4, 256)),
        ("g_head_g", (4, 256))]
REPL = [("norm1_g", (1, 1024)), ("conv_b", (1, 1024)), ("m_gate_b", (1, 2, 4)), ("g_a2_b", (1, 512)),
        ("norm2_g", (1, 1024)), ("final_g", (1024,))]
TINY_SIZE = 16 * 1024 + 4 * 1024 + 16 * 512 + 2 * 4 * 256
REPL_SIZE = 1024 + 1024 + 8 + 512 + 1024 + 1024
ROWS_GATHER = 8
ROWS_REP = 40
ROWS_OWN = 16


def pack_rows(vecs, rows):
    flat = jnp.concatenate([v.reshape(-1) for v in vecs])
    return jnp.pad(flat, (0, rows * 1024 - flat.shape[0])).reshape(rows, 1024)


def unpack_rows(packed, shapes):
    flat = packed.reshape(-1)
    out, off = [], 0
    for s in shapes:
        n = 1
        for d in s:
            n *= d
        out.append(flat[off:off + n].reshape(s))
        off += n
    return out


def kernel(x, meta_tokens, norm1_g, w_in, conv_w, conv_b, m_gate_b, g_a2, g_a2_b, m_head_g, g_head_g, w_branch_m, w_branch_g, w_out, norm2_g, w_ff_gate, w_ff_up, w_ff_down, final_g, loss_target, m_meta_tokens, m_norm1_g, m_w_in, m_conv_w, m_conv_b, m_m_gate_b, m_g_a2, m_g_a2_b, m_m_head_g, m_g_head_g, m_w_branch_m, m_w_branch_g, m_w_out, m_norm2_g, m_w_ff_gate, m_w_ff_up, m_w_ff_down, m_final_g, v_meta_tokens, v_norm1_g, v_w_in, v_conv_w, v_conv_b, v_m_gate_b, v_g_a2, v_g_a2_b, v_m_head_g, v_g_head_g, v_w_branch_m, v_w_branch_g, v_w_out, v_norm2_g, v_w_ff_gate, v_w_ff_up, v_w_ff_down, v_final_g):
    w_sh = dict(meta_tokens=meta_tokens, w_in=w_in[0], conv_w=conv_w[0], g_a2=g_a2[0], m_head_g=m_head_g[0],
                g_head_g=g_head_g[0], w_branch_m=w_branch_m[0], w_branch_g=w_branch_g[0], w_out=w_out[0],
                w_ff_gate=w_ff_gate[0], w_ff_up=w_ff_up[0], w_ff_down=w_ff_down[0])
    m_sh = dict(meta_tokens=m_meta_tokens, w_in=m_w_in[0], conv_w=m_conv_w[0], g_a2=m_g_a2[0],
                m_head_g=m_m_head_g[0], g_head_g=m_g_head_g[0], w_branch_m=m_w_branch_m[0],
                w_branch_g=m_w_branch_g[0], w_out=m_w_out[0], w_ff_gate=m_w_ff_gate[0], w_ff_up=m_w_ff_up[0],
                w_ff_down=m_w_ff_down[0])
    v_sh = dict(meta_tokens=v_meta_tokens, w_in=v_w_in[0], conv_w=v_conv_w[0], g_a2=v_g_a2[0],
                m_head_g=v_m_head_g[0], g_head_g=v_g_head_g[0], w_branch_m=v_w_branch_m[0],
                w_branch_g=v_w_branch_g[0], w_out=v_w_out[0], w_ff_gate=v_w_ff_gate[0], w_ff_up=v_w_ff_up[0],
                w_ff_down=v_w_ff_down[0])
    w_rep = dict(norm1_g=norm1_g, conv_b=conv_b, m_gate_b=m_gate_b, g_a2_b=g_a2_b, norm2_g=norm2_g, final_g=final_g)
    m_rep = dict(norm1_g=m_norm1_g, conv_b=m_conv_b, m_gate_b=m_m_gate_b, g_a2_b=m_g_a2_b, norm2_g=m_norm2_g,
                 final_g=m_final_g)
    v_rep = dict(norm1_g=v_norm1_g, conv_b=v_conv_b, m_gate_b=v_m_gate_b, g_a2_b=v_g_a2_b, norm2_g=v_norm2_g,
                 final_g=v_final_g)
    dev = 4 * lax.axis_index("x") + 2 * lax.axis_index("y") + lax.axis_index("c")
    tiny_names = [n for n, _ in TINY]
    repl_names = [n for n, _ in REPL]
    tiny_shard_shapes = [(s[0], s[1] // NDEV) for _, s in TINY]

    in8, tiny8 = all_gather([w_sh["w_in"].astype(MXU), pack_rows([w_sh[n] for n in tiny_names], ROWS_GATHER)],
                            "param_all_gather")
    late_names = ["w_branch_m", "w_branch_g", "w_out", "w_ff_gate", "w_ff_up", "w_ff_down"]
    late, first_order = send_start([w_sh[n].astype(MXU) for n in late_names], False, tiny8, "late_weights_start")
    wp = regroup_cols(in8)
    handles = {}

    def late_weights(after):
        bm8, bg8, out8, ffg8, ffu8, ffd8 = send_wait(late, after, "late_weights_wait")
        w_gu = interleave_gu(from_col_blocks(ffg8), from_col_blocks(ffu8))
        return bm8.reshape(D, D), bg8.reshape(D, D), out8.reshape(D, D), w_gu, ffd8.reshape(DFF, D)

    def send_early(g):
        d_gate, d_up = split_gu(g["w_gu"])
        blocks = [g["w_branch_m"].reshape(NDEV, D // NDEV, D).astype(WIRE),
                  g["w_branch_g"].reshape(NDEV, D // NDEV, D).astype(WIRE),
                  g["w_out"].reshape(NDEV, D // NDEV, D).astype(WIRE),
                  col_blocks(d_gate).astype(WIRE), col_blocks(d_up).astype(WIRE),
                  g["w_ff_down"].reshape(NDEV, DFF // NDEV, D).astype(WIRE)]
        handles["early"], token = send_start(blocks, True, blocks[0], "early_grads_start")
        return token

    def send_wp(d_wp):
        blocks = [ungroup_cols(d_wp).astype(WIRE)]
        handles["wp"], token = send_start(blocks, True, blocks[0], "proj_grads_start")
        return token

    tiny_full = {}
    for j in range(NDEV):
        for name, blk in zip(tiny_names, unpack_rows(tiny8[j], tiny_shard_shapes)):
            tiny_full.setdefault(name, []).append(blk)
    tiny_full = {n: jnp.concatenate(v, axis=1) for n, v in tiny_full.items()}

    loss, grad_x, g = local_step(
        x[0], loss_target[0], tiny_full["meta_tokens"], norm1_g, wp, tiny_full["conv_w"], conv_b, m_gate_b[0],
        tiny_full["g_a2"], g_a2_b, tiny_full["m_head_g"], tiny_full["g_head_g"], norm2_g, final_g,
        late_weights, send_early, send_wp, first_order)

    rep = pack_rows([g[n] for n in tiny_names + repl_names] + [loss[0, 0:1]], ROWS_REP)
    (got_rep,) = exchange([], rep, "small_grad_exchange")
    got_early = send_wait(handles["early"], got_rep, "early_grads_wait")
    (got_wp,) = send_wait(handles["wp"], got_rep, "proj_grads_wait")

    result = {}

    def update(name, parts, tr):
        outs = adamw(parts, w_sh[name], m_sh[name], v_sh[name], "adamw_" + name, tr)
        for kind, arr in zip(("grad", "delta", "new_m", "new_v"), outs):
            result[kind, name] = arr[None]

    update("w_in", got_wp, 128)
    update("w_branch_m", got_early[0], 128)
    update("w_branch_g", got_early[1], 128)
    update("w_out", got_early[2], 128)
    update("w_ff_gate", got_early[3], 256)
    update("w_ff_up", got_early[4], 256)
    update("w_ff_down", got_early[5], DFF // NDEV)

    rep_sum = sum_parts(got_rep, "sum_small", 1024)
    rep_g = unpack_rows(rep_sum, [s for _, s in TINY] + [s for _, s in REPL] + [(1,)])
    own_g = [lax.dynamic_slice_in_dim(gf, dev * ss[1], ss[1], axis=1) for gf, ss in zip(rep_g, tiny_shard_shapes)]
    own_g += rep_g[len(TINY):len(TINY) + len(REPL)]
    w_all = {**w_sh, **w_rep}
    m_all = {**m_sh, **m_rep}
    v_all = {**v_sh, **v_rep}
    names = tiny_names + repl_names
    outs = adamw(pack_rows(own_g, ROWS_OWN)[None], pack_rows([w_all[n] for n in names], ROWS_OWN),
                 pack_rows([m_all[n] for n in names], ROWS_OWN), pack_rows([v_all[n] for n in names], ROWS_OWN),
                 "adamw_small", ROWS_OWN)
    shapes = tiny_shard_shapes + [s for _, s in REPL]
    for kind, packed in zip(("grad", "delta", "new_m", "new_v"), outs):
        for name, arr in zip(names, unpack_rows(packed, shapes)):
            result[kind, name] = arr[None] if name in tiny_names and name != "meta_tokens" else arr
    loss_total = rep_g[-1][0]
    order = ["meta_tokens", "norm1_g", "w_in", "conv_w", "conv_b", "m_gate_b", "g_a2", "g_a2_b", "m_head_g", "g_head_g",
             "w_branch_m", "w_branch_g", "w_out", "norm2_g", "w_ff_gate", "w_ff_up", "w_ff_down", "final_g"]
    return (loss_total, grad_x[None], *[result[kind, n] for kind in ("grad", "delta", "new_m", "new_v") for n in order])
```

```python
import functools

import jax
import jax.numpy as jnp
from jax import lax
from jax.experimental import pallas as pl
from jax.experimental.pallas import tpu as pltpu

F32 = jnp.float32
MXU = jnp.bfloat16
WIRE = jnp.bfloat16

D = 1024
NH = 4
DV = 256
DQK = 128
L = 64
NMETA = 16
PADR = 512
LM = 256
CH0 = PADR // LM - 1
NPADROWS = PADR - NMETA
RANK = 16
DFF = 2816
EPS = 1e-6
TAU = 16.0
QSCALE = DQK ** -0.5
NEG = -1e30
NDEV = 8

MV0, MO0, GQ0, GK0, GV0, GR0, QK0, GM0, GG0, SM0 = 0, 1024, 2048, 2560, 3072, 4096, 5120, 6144, 7168, 8192
NP = 8320
NPROJ = 8216

ADAM_LR, ADAM_B1, ADAM_B2, ADAM_EPS, ADAM_WD, ADAM_STEP = 0.001, 0.9, 0.999, 1e-08, 0.01, 10

VMEM_LIMIT = 56 * 1024 * 1024
TM = 512


def _cp(sem):
    return pltpu.CompilerParams(dimension_semantics=sem, vmem_limit_bytes=VMEM_LIMIT)


def _sigmoid(x):
    return 1.0 / (1.0 + jnp.exp(-x))


def _log_sigmoid(x):
    return jnp.minimum(x, 0.0) - jnp.log1p(jnp.exp(-jnp.abs(x)))


def _dot(a, b, ca, cb):
    return lax.dot_general(a.astype(MXU), b.astype(MXU), (((ca,), (cb,)), ((), ())), preferred_element_type=F32)


def _dot_exact(a, b):
    return lax.dot_general(a, b, (((1,), (0,)), ((), ())), precision=lax.Precision.HIGHEST,
                           preferred_element_type=F32)


def _rb(tm, w, cb):
    return pl.BlockSpec((tm, w), lambda i: (i, cb))


def _const(shape):
    nd = len(shape)
    return pl.BlockSpec(shape, lambda i: (0,) * nd)


def _pick(n, target):
    if n <= target:
        return n
    best = None
    for t in range(128, target + 1, 128):
        if n % t == 0:
            best = t
    assert best is not None, (n, target)
    return best


def matmul(a, b, mode, name, add=None, out_dtype=F32, tm=512, tn=1664, tk=1024, order=None):
    if mode == "nn":
        (M, K), (K2, N) = a.shape, b.shape
    elif mode == "nt":
        (M, K), (N, K2) = a.shape, b.shape
    else:
        (K, M), (K2, N) = a.shape, b.shape
    assert K == K2, (a.shape, b.shape, mode)
    tm, tn, tk = _pick(M, tm), _pick(N, tn), _pick(K, tk)
    nk = K // tk
    assert nk == 1 or out_dtype == F32
    ca, cb = {"nn": (1, 0), "nt": (1, 1), "tn": (0, 0)}[mode]
    a_spec = {"nn": pl.BlockSpec((tm, tk), lambda j, i, k: (i, k)),
              "nt": pl.BlockSpec((tm, tk), lambda j, i, k: (i, k)),
              "tn": pl.BlockSpec((tk, tm), lambda j, i, k: (k, i))}[mode]
    b_spec = {"nn": pl.BlockSpec((tk, tn), lambda j, i, k: (k, j)),
              "nt": pl.BlockSpec((tn, tk), lambda j, i, k: (j, k)),
              "tn": pl.BlockSpec((tk, tn), lambda j, i, k: (k, j))}[mode]
    o_spec = pl.BlockSpec((tm, tn), lambda j, i, k: (i, j))
    has_add = add is not None

    def body(*refs):
        if order is not None:
            refs = refs[:-2] + refs[-1:]
        if has_add:
            a_ref, b_ref, add_ref, o_ref = refs
        else:
            a_ref, b_ref, o_ref = refs
            add_ref = None
        part = _dot(a_ref[...], b_ref[...], ca, cb)
        if nk == 1:
            if has_add:
                part = part + add_ref[...]
            o_ref[...] = part.astype(o_ref.dtype)
            return
        k = pl.program_id(2)

        @pl.when(k == 0)
        def _():
            o_ref[...] = part + add_ref[...] if has_add else part

        @pl.when(k > 0)
        def _():
            o_ref[...] += part

    in_specs = [a_spec, b_spec] + ([o_spec] if has_add else [])
    args = (a, b) + ((add,) if has_add else ())
    if order is not None:
        in_specs.append(pl.BlockSpec(order.shape, lambda j, i, k: (0, 0)))
        args += (order,)
    return pl.pallas_call(
        body, name=name, grid=(N // tn, M // tm, nk),
        in_specs=in_specs, out_specs=o_spec,
        out_shape=jax.ShapeDtypeStruct((M, N), out_dtype),
        compiler_params=_cp(("parallel", "parallel", "arbitrary")),
    )(*args)


def _h0_tile(i, x_ref, meta_ref):
    assert PADR == TM
    front = jnp.concatenate([jnp.zeros((NPADROWS, D), F32), meta_ref[...]], axis=0)
    return jnp.where(i == 0, front, x_ref[...])


_TOKENS = pl.BlockSpec((TM, D), lambda i: (jnp.maximum(i - 1, 0), 0))


def rms_fwd_input(x, meta, g, name):
    R = x.shape[0] + PADR

    def body(x_ref, meta_ref, g_ref, y_ref):
        xv = _h0_tile(pl.program_id(0), x_ref, meta_ref)
        r = lax.rsqrt(jnp.mean(xv * xv, axis=-1, keepdims=True) + EPS)
        y_ref[...] = (xv * r * g_ref[...]).astype(y_ref.dtype)

    return pl.pallas_call(
        body, name=name, grid=(R // TM,),
        in_specs=[_TOKENS, _const((NMETA, D)), _const((1, D))], out_specs=_rb(TM, D, 0),
        out_shape=jax.ShapeDtypeStruct((R, D), MXU), compiler_params=_cp(("parallel",)),
    )(x, meta, g)


def dgrad_rms_bwd(a, w, x, g, dres, name, tm, tk):
    R, K = a.shape
    tm, tk = _pick(R, tm), _pick(K, tk)
    nk = K // tk

    def body(a_ref, w_ref, x_ref, g_ref, dres_ref, dx_ref, dxb_ref, dg_ref):
        i, k = pl.program_id(0), pl.program_id(1)
        part = _dot(a_ref[...], w_ref[...], 1, 1)

        @pl.when(k == 0)
        def _():
            dx_ref[...] = part

        @pl.when(k > 0)
        def _():
            dx_ref[...] += part

        @pl.when(k == nk - 1)
        def _():
            xv, dyv = x_ref[...], dx_ref[...]
            r = lax.rsqrt(jnp.mean(xv * xv, axis=-1, keepdims=True) + EPS)
            dyg = dyv * g_ref[...]
            dx = dres_ref[...] + r * dyg - xv * (r * r * r * jnp.mean(dyg * xv, axis=-1, keepdims=True))
            dx_ref[...] = dx
            dxb_ref[...] = dx.astype(dxb_ref.dtype)
            gpart = jnp.sum(dyv * xv * r, axis=0, keepdims=True)

            @pl.when(i == 0)
            def _():
                dg_ref[...] = gpart

            @pl.when(i > 0)
            def _():
                dg_ref[...] += gpart

    row = pl.BlockSpec((tm, D), lambda i, k: (i, 0))
    return pl.pallas_call(
        body, name=name, grid=(R // tm, nk),
        in_specs=[pl.BlockSpec((tm, tk), lambda i, k: (i, k)), pl.BlockSpec((D, tk), lambda i, k: (0, k)), row,
                  pl.BlockSpec((1, D), lambda i, k: (0, 0)), row],
        out_specs=[row, row, pl.BlockSpec((1, D), lambda i, k: (0, 0))],
        out_shape=[jax.ShapeDtypeStruct((R, D), F32), jax.ShapeDtypeStruct((R, D), MXU),
                   jax.ShapeDtypeStruct((1, D), F32)],
        compiler_params=_cp(("arbitrary", "arbitrary")),
    )(a, w, x, g, dres)


def rms_bwd_input(dy, x, meta, g, dres, name):
    R = dy.shape[0]

    def body(dy_ref, x_ref, meta_ref, g_ref, dres_ref, dx_ref, dmeta_ref, dg_ref):
        i = pl.program_id(0)
        xv, dyv = _h0_tile(i, x_ref, meta_ref), dy_ref[...]
        r = lax.rsqrt(jnp.mean(xv * xv, axis=-1, keepdims=True) + EPS)
        dyg = dyv * g_ref[...]
        dx = dres_ref[...] + r * dyg - xv * (r * r * r * jnp.mean(dyg * xv, axis=-1, keepdims=True))
        dx_ref[...] = dx
        part = jnp.sum(dyv * xv * r, axis=0, keepdims=True)

        @pl.when(i == 0)
        def _():
            dg_ref[...] = part
            dmeta_ref[...] = dx[NPADROWS:PADR]

        @pl.when(i > 0)
        def _():
            dg_ref[...] += part

    return pl.pallas_call(
        body, name=name, grid=(R // TM,),
        in_specs=[_rb(TM, D, 0), _TOKENS, _const((NMETA, D)), _const((1, D)), _rb(TM, D, 0)],
        out_specs=[_TOKENS, _const((NMETA, D)), _const((1, D))],
        out_shape=[jax.ShapeDtypeStruct((R - PADR, D), F32), jax.ShapeDtypeStruct((NMETA, D), F32),
                   jax.ShapeDtypeStruct((1, D), F32)],
        compiler_params=_cp(("arbitrary",)),
    )(dy, x, meta, g, dres)


def _shift_down(cur, prev8, s):
    tm = cur.shape[0]
    rolled = pltpu.roll(cur, s, 0)
    rows8 = lax.broadcasted_iota(jnp.int32, (8, cur.shape[1]), 0)
    head = jnp.where(rows8 < s, pltpu.roll(prev8, s, 0), rolled[0:8])
    return jnp.concatenate([head, rolled[8:tm]], axis=0)


def _shift_up(cur, next8, s):
    tm = cur.shape[0]
    rolled = pltpu.roll(cur, tm - s, 0)
    rows8 = lax.broadcasted_iota(jnp.int32, (8, cur.shape[1]), 0)
    tail = jnp.where(rows8 >= 8 - s, pltpu.roll(next8, 8 - s, 0), rolled[tm - 8:tm])
    return jnp.concatenate([rolled[0:tm - 8], tail], axis=0)


def prep_fwd(proj, conv_w, conv_b, gb_row):
    R = proj.shape[0]
    t8 = TM // 8

    def body(x_ref, halo_ref, sm_ref, w_ref, b_ref, gb_ref, c_ref, qk_ref, gl_ref):
        i = pl.program_id(0)
        x = x_ref[...]
        halo = halo_ref[...]
        w = w_ref[...]
        c = x * w[3:4, :] + b_ref[...]
        for s in (1, 2, 3):
            c = c + _shift_down(x, halo, s) * w[3 - s:4 - s, :]
        c_ref[...] = c
        qk_ref[...] = c * _sigmoid(c)
        z = sm_ref[...] + gb_ref[...]
        lane = lax.broadcasted_iota(jnp.int32, z.shape, 1)
        row = lax.broadcasted_iota(jnp.int32, z.shape, 0) + i * TM
        valid = row >= NPADROWS
        logi = jnp.where(valid, z, NEG)
        logf = jnp.where(valid, _log_sigmoid(z), 0.0)
        gl_ref[...] = jnp.where(lane < 4, logi, jnp.where(lane < 8, logf, 0.0))

    return pl.pallas_call(
        body, name="prep_fwd", grid=(R // TM,),
        in_specs=[_rb(TM, 1024, QK0 // 1024),
                  pl.BlockSpec((8, 1024), lambda i: (jnp.maximum(i * t8 - 1, 0), QK0 // 1024)),
                  _rb(TM, 128, SM0 // 128), _const((4, 1024)), _const((1, 1024)), _const((1, 128))],
        out_specs=[_rb(TM, 1024, 0), _rb(TM, 1024, 0), _rb(TM, 128, 0)],
        out_shape=[jax.ShapeDtypeStruct((R, 1024), F32), jax.ShapeDtypeStruct((R, 1024), F32),
                   jax.ShapeDtypeStruct((R, 128), F32)],
        compiler_params=_cp(("parallel",)),
    )(proj, proj, proj, conv_w, conv_b, gb_row)


def merge_fwd(ym, yg, w_bm, w_bg, proj):
    R = ym.shape[0]

    def body(ym_ref, yg_ref, wm_ref, wg_ref, gm_ref, gg_ref, bm_ref, bg_ref, o_ref):
        bm = _dot(ym_ref[...], wm_ref[...], 1, 0)
        bg = _dot(yg_ref[...], wg_ref[...], 1, 0)
        bm_ref[...] = bm
        bg_ref[...] = bg
        o_ref[...] = (_sigmoid(gm_ref[...]) * bm + _sigmoid(gg_ref[...]) * bg).astype(o_ref.dtype)

    return pl.pallas_call(
        body, name="merge_fwd", grid=(R // TM,),
        in_specs=[_rb(TM, D, 0), _rb(TM, D, 0), _const((D, D)), _const((D, D)), _rb(TM, D, GM0 // D),
                  _rb(TM, D, GG0 // D)],
        out_specs=[_rb(TM, D, 0), _rb(TM, D, 0), _rb(TM, D, 0)],
        out_shape=[jax.ShapeDtypeStruct((R, D), F32), jax.ShapeDtypeStruct((R, D), F32),
                   jax.ShapeDtypeStruct((R, D), MXU)],
        compiler_params=_cp(("parallel",)),
    )(ym, yg, w_bm, w_bg, proj, proj)


def merge_bwd(dh1, w_out, bm, bg, proj):
    R = bm.shape[0]

    def body(dh_ref, w_ref, bm_ref, bg_ref, gm_ref, gg_ref, dbm_ref, dbg_ref, dp_ref):
        dm = _dot(dh_ref[...], w_ref[...], 1, 1)
        sm, sg = _sigmoid(gm_ref[...]), _sigmoid(gg_ref[...])
        dbm_ref[...] = (dm * sm).astype(dbm_ref.dtype)
        dbg_ref[...] = (dm * sg).astype(dbg_ref.dtype)
        dp_ref[:, 0:D] = (dm * bm_ref[...] * sm * (1.0 - sm)).astype(dp_ref.dtype)
        dp_ref[:, D:2 * D] = (dm * bg_ref[...] * sg * (1.0 - sg)).astype(dp_ref.dtype)

    return pl.pallas_call(
        body, name="merge_bwd", grid=(R // TM,),
        in_specs=[_rb(TM, D, 0), _const((D, D)), _rb(TM, D, 0), _rb(TM, D, 0), _rb(TM, D, GM0 // D),
                  _rb(TM, D, GG0 // D)],
        out_specs=[_rb(TM, D, 0), _rb(TM, D, 0), _rb(TM, 2 * D, GM0 // (2 * D))],
        out_shape=[jax.ShapeDtypeStruct((R, D), MXU), jax.ShapeDtypeStruct((R, D), MXU),
                   jax.ShapeDtypeStruct((R, NP), MXU)],
        compiler_params=_cp(("parallel",)),
    )(dh1, w_out, bm, bg, proj, proj)


TF = DFF // 2
TMF = 768


def interleave_gu(gate, up):
    return jnp.concatenate([gate[:, :TF], up[:, :TF], gate[:, TF:], up[:, TF:]], axis=1)


def split_gu(gu):
    return (jnp.concatenate([gu[:, 0:TF], gu[:, 2 * TF:3 * TF]], axis=1),
            jnp.concatenate([gu[:, TF:2 * TF], gu[:, 3 * TF:]], axis=1))


def ff_in_fwd(hn, w_gu):
    R = hn.shape[0]

    def body(x_ref, w_ref, au_ref, ff_ref):
        au = _dot(x_ref[...], w_ref[...], 1, 0)
        au_ref[...] = au.astype(au_ref.dtype)
        a = au[:, :TF]
        ff_ref[...] = (a * _sigmoid(a) * au[:, TF:]).astype(ff_ref.dtype)

    tm = _pick(R, TMF)
    return pl.pallas_call(
        body, name="ff_in_fwd", grid=(DFF // TF, R // tm),
        in_specs=[pl.BlockSpec((tm, D), lambda j, i: (i, 0)), pl.BlockSpec((D, 2 * TF), lambda j, i: (0, j))],
        out_specs=[pl.BlockSpec((tm, 2 * TF), lambda j, i: (i, j)), pl.BlockSpec((tm, TF), lambda j, i: (i, j))],
        out_shape=[jax.ShapeDtypeStruct((R, 2 * DFF), MXU), jax.ShapeDtypeStruct((R, DFF), MXU)],
        compiler_params=_cp(("parallel", "parallel")),
    )(hn, w_gu)


def ff_down_dgrad(dh2, w_down, au):
    R = dh2.shape[0]

    def body(d_ref, w_ref, au_ref, o_ref):
        dff = _dot(d_ref[...], w_ref[...], 1, 1)
        a = au_ref[:, :TF].astype(F32)
        u = au_ref[:, TF:].astype(F32)
        s = _sigmoid(a)
        o_ref[:, :TF] = (dff * u * s * (1.0 + a * (1.0 - s))).astype(o_ref.dtype)
        o_ref[:, TF:] = (dff * a * s).astype(o_ref.dtype)

    tm = _pick(R, TMF)
    return pl.pallas_call(
        body, name="ff_down_dgrad", grid=(DFF // TF, R // tm),
        in_specs=[pl.BlockSpec((tm, D), lambda j, i: (i, 0)), pl.BlockSpec((TF, D), lambda j, i: (j, 0)),
                  pl.BlockSpec((tm, 2 * TF), lambda j, i: (i, j))],
        out_specs=pl.BlockSpec((tm, 2 * TF), lambda j, i: (i, j)),
        out_shape=jax.ShapeDtypeStruct((R, 2 * DFF), MXU),
        compiler_params=_cp(("parallel", "parallel")),
    )(dh2, w_down, au)


def out_proj_norm(merged, w, x, meta, g):
    R = merged.shape[0]

    def body(m_ref, w_ref, x_ref, meta_ref, g_ref, h_ref, n_ref):
        hv = _dot(m_ref[...], w_ref[...], 1, 0) + _h0_tile(pl.program_id(0), x_ref, meta_ref)
        h_ref[...] = hv
        r = lax.rsqrt(jnp.mean(hv * hv, axis=-1, keepdims=True) + EPS)
        n_ref[...] = (hv * r * g_ref[...]).astype(n_ref.dtype)

    return pl.pallas_call(
        body, name="out_fwd", grid=(R // TM,),
        in_specs=[_rb(TM, D, 0), _const((D, D)), _TOKENS, _const((NMETA, D)), _const((1, D))],
        out_specs=[_rb(TM, D, 0), _rb(TM, D, 0)],
        out_shape=[jax.ShapeDtypeStruct((R, D), F32), jax.ShapeDtypeStruct((R, D), MXU)],
        compiler_params=_cp(("parallel",)),
    )(merged, w, x, meta, g)


def ff_down_loss(ff, w_down, h1, gf, target):
    R = ff.shape[0]
    assert PADR == TM

    def body(f_ref, w_ref, h1_ref, g_ref, t_ref, dh_ref, dhb_ref, loss_ref, dg_ref):
        i = pl.program_id(0)
        hv = _dot(f_ref[...], w_ref[...], 1, 0) + h1_ref[...]
        r = lax.rsqrt(jnp.mean(hv * hv, axis=-1, keepdims=True) + EPS)
        g = g_ref[...]
        live = (i >= 1).astype(F32)
        e = (hv * r * g - t_ref[...]) * live
        dy = e * (1.0 / D)
        dyg = dy * g
        dh = r * dyg - hv * (r * r * r * jnp.mean(dyg * hv, axis=-1, keepdims=True))
        dh_ref[...] = dh
        dhb_ref[...] = dh.astype(dhb_ref.dtype)
        lpart = jnp.zeros((1, 128), F32) + 0.5 * jnp.sum(jnp.sum(e * e, axis=1, keepdims=True), axis=0, keepdims=True) * (1.0 / D)
        gpart = jnp.sum(dy * hv * r, axis=0, keepdims=True)

        @pl.when(i == 0)
        def _():
            loss_ref[...] = lpart
            dg_ref[...] = gpart

        @pl.when(i > 0)
        def _():
            loss_ref[...] += lpart
            dg_ref[...] += gpart

    return pl.pallas_call(
        body, name="ff_down_loss", grid=(R // TM,),
        in_specs=[_rb(TM, DFF, 0), _const((DFF, D)), _rb(TM, D, 0), _const((1, D)),
                  pl.BlockSpec((TM, D), lambda i: (jnp.maximum(i - 1, 0), 0))],
        out_specs=[_rb(TM, D, 0), _rb(TM, D, 0), _const((1, 128)), _const((1, D))],
        out_shape=[jax.ShapeDtypeStruct((R, D), F32), jax.ShapeDtypeStruct((R, D), MXU),
                   jax.ShapeDtypeStruct((1, 128), F32),
                   jax.ShapeDtypeStruct((1, D), F32)],
        compiler_params=_cp(("arbitrary",)),
    )(ff, w_down, h1, gf, target)


def conv_bwd(dc, proj, conv_w, dproj):
    R = dc.shape[0]
    t8 = TM // 8
    nt = R // TM

    def body(dc_ref, nxt_ref, x_ref, prv_ref, w_ref, dp_in, dp_ref, dw_ref):
        del dp_in
        i = pl.program_id(0)
        dcv = dc_ref[...]
        nxt = nxt_ref[...] * (i < nt - 1).astype(F32)
        x = x_ref[...]
        prv = prv_ref[...]
        w = w_ref[...]
        dx = dcv * w[3:4, :]
        rows = [None] * 4
        rows[3] = jnp.sum(dcv * x, axis=0, keepdims=True)
        for s in (1, 2, 3):
            dx = dx + _shift_up(dcv, nxt, s) * w[3 - s:4 - s, :]
            rows[3 - s] = jnp.sum(dcv * _shift_down(x, prv, s), axis=0, keepdims=True)
        dp_ref[...] = dx.astype(dp_ref.dtype)
        part = jnp.concatenate(rows + [jnp.sum(dcv, axis=0, keepdims=True), jnp.zeros((3, 1024), F32)], axis=0)

        @pl.when(i == 0)
        def _():
            dw_ref[...] = part

        @pl.when(i > 0)
        def _():
            dw_ref[...] += part

    return pl.pallas_call(
        body, name="conv_bwd", grid=(nt,),
        in_specs=[_rb(TM, 1024, 0),
                  pl.BlockSpec((8, 1024), lambda i: (jnp.minimum((i + 1) * t8, nt * t8 - 1), 0)),
                  _rb(TM, 1024, QK0 // 1024),
                  pl.BlockSpec((8, 1024), lambda i: (jnp.maximum(i * t8 - 1, 0), QK0 // 1024)),
                  _const((4, 1024)), pl.BlockSpec(memory_space=pl.ANY)],
        out_specs=[_rb(TM, 1024, QK0 // 1024), _const((8, 1024))],
        out_shape=[jax.ShapeDtypeStruct((R, NP), MXU), jax.ShapeDtypeStruct((8, 1024), F32)],
        input_output_aliases={5: 0},
        compiler_params=_cp(("arbitrary",)),
    )(dc, dc, proj, proj, conv_w, dproj)


def small_bwd(dgl, dga, proj, gb_row, dproj):
    R = dgl.shape[0]

    def body(dgl_ref, dga_ref, sm_ref, gb_ref, dp_in, dp_ref, dgb_ref):
        del dp_in
        i = pl.program_id(0)
        z = sm_ref[...] + gb_ref[...]
        lane = lax.broadcasted_iota(jnp.int32, z.shape, 1)
        row = lax.broadcasted_iota(jnp.int32, z.shape, 0) + i * TM
        valid = row >= NPADROWS
        dgl_v = dgl_ref[...]
        dgate = jnp.where(valid, jnp.where(lane < 4, dgl_v, dgl_v * _sigmoid(-z)), 0.0)
        ds = jnp.where(lane < 8, dgate, dga_ref[...])
        dp_ref[...] = ds.astype(dp_ref.dtype)
        part = jnp.sum(jnp.where(lane < 8, dgate, 0.0), axis=0, keepdims=True)

        @pl.when(i == 0)
        def _():
            dgb_ref[...] = part

        @pl.when(i > 0)
        def _():
            dgb_ref[...] += part

    return pl.pallas_call(
        body, name="small_bwd", grid=(R // TM,),
        in_specs=[_rb(TM, 128, 0), _rb(TM, 128, 0), _rb(TM, 128, SM0 // 128), _const((1, 128)),
                  pl.BlockSpec(memory_space=pl.ANY)],
        out_specs=[_rb(TM, 128, SM0 // 128), _const((1, 128))],
        out_shape=[jax.ShapeDtypeStruct((R, NP), MXU), jax.ShapeDtypeStruct((1, 128), F32)],
        input_output_aliases={4: 0},
        compiler_params=_cp(("arbitrary",)),
    )(dgl, dga, proj, gb_row, dproj)


def _masks(n=L):
    r = lax.broadcasted_iota(jnp.int32, (n, n), 0)
    c = lax.broadcasted_iota(jnp.int32, (n, n), 1)
    return r >= c, r == c, r


def _to_row(col, eye):
    return jnp.sum(jnp.where(eye, col, 0.0), axis=0, keepdims=True)


def _to_col(row, eye):
    return jnp.sum(jnp.where(eye, row, 0.0), axis=1, keepdims=True)


def _mlstm_chunk(q, k, logi_c, logf_c, m, n):
    tril, eye, _ = _masks(q.shape[0])
    logi_r, logf_r = _to_row(logi_c, eye), _to_row(logf_c, eye)
    b_c = jnp.sum(jnp.where(tril, logf_r, 0.0), axis=1, keepdims=True)
    b_r = _to_row(b_c, eye)
    g = jnp.sum(logf_c, axis=0, keepdims=True)
    dmat = jnp.where(tril, b_c - b_r + logi_r, NEG)
    mrow = jnp.maximum(b_c + m, jnp.max(dmat, axis=1, keepdims=True))
    dm = jnp.exp(dmat - mrow)
    s = _dot(q, k, 1, 1)
    w = dm * s
    a_in = jnp.exp(b_c + m - mrow)
    qn = jnp.sum(q * n, axis=1, keepdims=True)
    den = a_in * qn + jnp.sum(w, axis=1, keepdims=True)
    floor = jnp.exp(-mrow)
    nrm = jnp.maximum(jnp.abs(den), floor)
    wlog_c = g - b_c + logi_c
    m_new = jnp.maximum(g + m, jnp.max(wlog_c, axis=0, keepdims=True))
    a_st = jnp.exp(g + m - m_new)
    w_c = jnp.exp(wlog_c - m_new)
    return dict(b_c=b_c, g=g, dm=dm, s=s, w=w, a_in=a_in, qn=qn, den=den, floor=floor, nrm=nrm,
                m_new=m_new, a_st=a_st, w_c=w_c, tril=tril, eye=eye)


def _mlstm_head_fwd(h, glv, qk_ref, v_ref, mo_ref, hg_ref, hm_ref, ym_ref, cs_ref, nm_ref, c_s, nm_s):
    q = qk_ref[:, h * DQK:(h + 1) * DQK] * QSCALE
    k = qk_ref[:, 512 + h * DQK:512 + (h + 1) * DQK]
    v = v_ref[:, h * DV:(h + 1) * DV]
    C = c_s[h]
    n = nm_s[h, 0:1, :]
    m = nm_s[h, 1:2, 0:1]
    f = _mlstm_chunk(q, k, glv[:, h:h + 1], glv[:, 4 + h:5 + h], m, n)
    num = f["a_in"] * _dot(q, C, 1, 1) + _dot(f["w"], v, 1, 0)
    hh = num / f["nrm"]
    cs_ref[0, h] = C
    nm_ref[0, h] = nm_s[h]
    c_s[h] = f["a_st"] * C + _dot(f["w_c"] * v, k, 0, 0)
    n_new = f["a_st"] * n + jnp.sum(f["w_c"] * k, axis=0, keepdims=True)
    rowi = lax.broadcasted_iota(jnp.int32, (8, DQK), 0)
    nm_s[h] = jnp.where(rowi == 0, n_new, jnp.where(rowi == 1, f["m_new"], 0.0))
    rm = lax.rsqrt(jnp.mean(hh * hh, axis=-1, keepdims=True) + EPS)
    sl = slice(h * DV, (h + 1) * DV)
    hm_ref[:, sl] = hh
    ym_ref[:, sl] = (hh * rm * hg_ref[:, sl] * _sigmoid(mo_ref[:, sl])).astype(ym_ref.dtype)


def _mlstm_bwd_parts(dym_ref, hm_ref, qk_ref, cp_ref, v_ref, gl_ref, mo_ref, hg_ref, cs_ref, nm_ref,
                     dp_ref, dc_ref, dgl_ref, dhg_ref, dc_s, dn_s):
        def init():
            dc_s[...] = jnp.zeros_like(dc_s)
            dn_s[...] = jnp.zeros_like(dn_s)
            dhg_ref[...] = jnp.zeros_like(dhg_ref)

        def zero():
            dp_ref[...] = jnp.zeros_like(dp_ref)
            dc_ref[...] = jnp.zeros_like(dc_ref)
            dgl_ref[...] = jnp.zeros_like(dgl_ref)

        def compute(after_head):
            glv = gl_ref[...]
            lane = lax.broadcasted_iota(jnp.int32, (LM, 128), 1)
            dgl = jnp.zeros((LM, 128), F32)
            for h in range(NH):
                sl = slice(h * DV, (h + 1) * DV)
                sq = slice(h * DQK, (h + 1) * DQK)
                sk = slice(512 + h * DQK, 512 + (h + 1) * DQK)
                hh = hm_ref[:, sl]
                gain = hg_ref[:, sl]
                rm = lax.rsqrt(jnp.mean(hh * hh, axis=-1, keepdims=True) + EPS)
                sg = _sigmoid(mo_ref[:, sl])
                dyv = dym_ref[:, sl]
                dno = dyv * sg
                dp_ref[:, 1024 + h * DV:1024 + (h + 1) * DV] = (dyv * hh * rm * gain * sg * (1.0 - sg)).astype(dp_ref.dtype)
                dhg_ref[:, sl] += jnp.sum(dno * hh * rm, axis=0, keepdims=True)
                dnog = dno * gain
                dh = rm * dnog - hh * (rm * rm * rm * jnp.mean(dnog * hh, axis=-1, keepdims=True))
                q = qk_ref[:, sq] * QSCALE
                k = qk_ref[:, sk]
                v = v_ref[:, sl]
                C = cs_ref[0, h]
                n = nm_ref[0, h, 0:1, :]
                m = nm_ref[0, h, 1:2, 0:1]
                f = _mlstm_chunk(q, k, glv[:, h:h + 1], glv[:, 4 + h:5 + h], m, n)
                eye = f["eye"]
                a_in, nrm, den, w = f["a_in"], f["nrm"], f["den"], f["w"]
                dnum = dh / nrm
                dnrm = -jnp.sum(dh * hh, axis=1, keepdims=True) / nrm
                dden = jnp.where(jnp.abs(den) >= f["floor"], dnrm * jnp.sign(den), 0.0)
                dw = _dot(dnum, v, 1, 1) + dden
                dv = _dot(w, dnum, 0, 0)
                ds = dw * f["dm"]
                e = dw * w
                qc = _dot(q, C, 1, 1)
                dq = _dot(ds, k, 1, 0) + a_in * _dot(dnum, C, 1, 0) + (a_in * dden) * n
                dk = _dot(ds, q, 0, 0)
                dC_in = _dot(a_in * dnum, q, 0, 0)
                dn_in = jnp.sum((a_in * dden) * q, axis=0, keepdims=True)
                da_in = jnp.sum(dnum * qc, axis=1, keepdims=True) + dden * f["qn"]
                col_e = _to_col(jnp.sum(e, axis=0, keepdims=True), eye)
                db = jnp.sum(e, axis=1, keepdims=True) + da_in * a_in - col_e
                dlogi = col_e
                dCp = dc_s[h]
                dnp = dn_s[h, 0:1, :]
                a_st, w_c = f["a_st"], f["w_c"]
                da_st = (jnp.sum(jnp.sum(dCp * C, axis=1, keepdims=True), axis=0, keepdims=True)
                         + jnp.sum(dnp * n, axis=1, keepdims=True))
                vdc = _dot(v, dCp, 1, 0)
                dw_c = jnp.sum((vdc + dnp) * k, axis=1, keepdims=True)
                dv = dv + w_c * _dot(k, dCp, 1, 1)
                dk = dk + w_c * (vdc + dnp)
                fw = dw_c * w_c
                dg = jnp.sum(fw, axis=0, keepdims=True) + da_st * a_st
                db = db - fw
                dlogi = dlogi + fw
                rowc = lax.broadcasted_iota(jnp.int32, (LM, 1), 0)
                db = db + jnp.where(rowc == LM - 1, dg, 0.0)
                triu = lax.broadcasted_iota(jnp.int32, (LM, LM), 1) >= lax.broadcasted_iota(jnp.int32, (LM, LM), 0)
                dlogf = jnp.sum(jnp.where(triu, _to_row(db, eye), 0.0), axis=1, keepdims=True)
                dc_s[h] = a_st * dCp + dC_in
                dn_new = a_st * dnp + dn_in
                dn_s[h] = jnp.zeros((8, DQK), F32) + dn_new
                cq, ck = cp_ref[:, sq], cp_ref[:, sk]
                s_q, s_k = _sigmoid(cq), _sigmoid(ck)
                dc_ref[:, sq] = dq * QSCALE * s_q * (1.0 + cq * (1.0 - s_q))
                dc_ref[:, sk] = dk * s_k * (1.0 + ck * (1.0 - s_k))
                dp_ref[:, sl] = dv.astype(dp_ref.dtype)
                dgl = jnp.where(lane == h, dlogi, jnp.where(lane == 4 + h, dlogf, dgl))
                after_head(h)
            dgl_ref[...] = dgl

        return init, zero, compute


def _gla_logs(sm, a2p, b2, valid):
    za = _dot(sm, a2p, 1, 0) + b2
    return za, jnp.where(valid, _log_sigmoid(za) * (1.0 / TAU), 0.0)


def _valid_rows(c, width):
    row = lax.broadcasted_iota(jnp.int32, (L, width), 0) + c * L
    return row >= NPADROWS


def _gla_cumsum(loga):
    tril, _, _ = _masks()
    return _dot_exact(tril.astype(F32), loga)


def _gla_chunk(q, k, la, bc):
    tril, _, _ = _masks()
    btot = jnp.sum(la, axis=0, keepdims=True)
    ebc = jnp.exp(bc)
    qd = q * ebc
    ki = k * jnp.exp(-bc)
    ke = k * jnp.exp(btot - bc)
    att = jnp.where(tril, _dot(qd, ki, 1, 1), 0.0)
    return dict(tril=tril, bc=bc, btot=btot, ebc=ebc, qd=qd, ki=ki, ke=ke, att=att)


def _col128(row):
    r = lax.broadcasted_iota(jnp.int32, (DQK, DQK), 0)
    c = lax.broadcasted_iota(jnp.int32, (DQK, DQK), 1)
    return jnp.sum(jnp.where(r == c, row, 0.0), axis=1, keepdims=True)


def _gla_fwd_parts(c, q_ref, k_ref, v_ref, gr_ref, sm_ref, a2_ref, b2_ref, hg_ref, hgl_ref, yg_ref, ss_ref, s_s):
        def init():
            s_s[...] = jnp.zeros_like(s_s)

        def zero():
            hgl_ref[...] = jnp.zeros_like(hgl_ref)
            yg_ref[...] = jnp.zeros_like(yg_ref)
            ss_ref[...] = jnp.zeros_like(ss_ref)

        def compute(s):
            rows = pl.ds(s * L, L)
            chunk(LM // L * c + s, q_ref.at[rows], k_ref.at[rows], v_ref.at[rows], gr_ref.at[rows], sm_ref.at[rows],
                  hgl_ref.at[rows], yg_ref.at[rows], ss_ref.at[pl.ds(s, 1)])

        def chunk(c, q_ref, k_ref, v_ref, gr_ref, sm_ref, hgl_ref, yg_ref, ss_ref):
            _, loga = _gla_logs(sm_ref[...], a2_ref[...], b2_ref[...], _valid_rows(c, 512))
            bc_all = _gla_cumsum(loga)
            for h in range(NH):
                sq = slice(h * DQK, (h + 1) * DQK)
                sl = slice(h * DV, (h + 1) * DV)
                q = q_ref[:, sq] * QSCALE
                k = k_ref[:, sq]
                v = v_ref[:, sl]
                S = s_s[h]
                f = _gla_chunk(q, k, loga[:, sq], bc_all[:, sq])
                o = _dot(f["att"], v, 1, 0) + _dot(f["qd"], S, 1, 0)
                ss_ref[0, h] = S
                s_s[h] = _col128(jnp.exp(f["btot"])) * S + _dot(f["ke"], v, 0, 0)
                rg = lax.rsqrt(jnp.mean(o * o, axis=-1, keepdims=True) + EPS)
                gr = gr_ref[:, sl]
                hgl_ref[:, sl] = o
                yg_ref[:, sl] = (o * rg * hg_ref[:, sl] * gr * _sigmoid(gr)).astype(yg_ref.dtype)

        return init, zero, compute


def _gla_bwd_parts(c, dy_ref, ho_ref, q_ref, k_ref, v_ref, gr_ref, sm_ref, a2_ref, b2_ref, hg_ref, ss_ref,
                   dp_ref, dga_ref, da2_ref, db2_ref, dhg_ref, ds_s):
        def init():
            ds_s[...] = jnp.zeros_like(ds_s)
            da2_ref[...] = jnp.zeros_like(da2_ref)
            db2_ref[...] = jnp.zeros_like(db2_ref)
            dhg_ref[...] = jnp.zeros_like(dhg_ref)

        def zero():
            dp_ref[...] = jnp.zeros_like(dp_ref)
            dga_ref[...] = jnp.zeros_like(dga_ref)

        def compute(s):
            rows = pl.ds(s * L, L)
            chunk(LM // L * c + s, dy_ref.at[rows], ho_ref.at[rows], q_ref.at[rows], k_ref.at[rows], v_ref.at[rows],
                  gr_ref.at[rows], sm_ref.at[rows], ss_ref.at[pl.ds(s, 1)], dp_ref.at[rows], dga_ref.at[rows])

        def chunk(c, dy_ref, ho_ref, q_ref, k_ref, v_ref, gr_ref, sm_ref, ss_ref, dp_ref, dga_ref):
            valid = _valid_rows(c, 512)
            sm = sm_ref[...]
            za, loga = _gla_logs(sm, a2_ref[...], b2_ref[...], valid)
            dbcs = []
            for h in range(NH):
                sq = slice(h * DQK, (h + 1) * DQK)
                sl = slice(h * DV, (h + 1) * DV)
                o = ho_ref[:, sl]
                gain = hg_ref[:, sl]
                rg = lax.rsqrt(jnp.mean(o * o, axis=-1, keepdims=True) + EPS)
                gr = gr_ref[:, sl]
                sg = _sigmoid(gr)
                dyv = dy_ref[:, sl]
                dno = dyv * gr * sg
                dp_ref[:, 2048 + h * DV:2048 + (h + 1) * DV] = (
                    dyv * o * rg * gain * sg * (1.0 + gr * (1.0 - sg))).astype(dp_ref.dtype)
                dhg_ref[:, sl] += jnp.sum(dno * o * rg, axis=0, keepdims=True)
                dnog = dno * gain
                do = rg * dnog - o * (rg * rg * rg * jnp.mean(dnog * o, axis=-1, keepdims=True))
                q = q_ref[:, sq] * QSCALE
                k = k_ref[:, sq]
                v = v_ref[:, sl]
                S = ss_ref[0, h]
                f = _gla_chunk(q, k, loga[:, sq], _gla_cumsum(loga[:, sq]))
                tril, qd, ki, ke = f["tril"], f["qd"], f["ki"], f["ke"]
                dSp = ds_s[h]
                datt = jnp.where(tril, _dot(do, v, 1, 1), 0.0)
                dqd = _dot(do, S, 1, 1) + _dot(datt, ki, 1, 0)
                dki = _dot(datt, qd, 0, 0)
                dv = _dot(f["att"], do, 0, 0) + _dot(ke, dSp, 1, 0)
                dke = _dot(v, dSp, 1, 1)
                ebt = jnp.exp(f["btot"])
                dbtot = jnp.sum(dke * ke, axis=0, keepdims=True) + ebt * _to_row128(jnp.sum(dSp * S, axis=1, keepdims=True))
                ds_s[h] = _dot(qd, do, 0, 0) + _col128(ebt) * dSp
                dq = dqd * f["ebc"]
                dk = dki * jnp.exp(-f["bc"]) + dke * jnp.exp(f["btot"] - f["bc"])
                dbc = dqd * qd - dki * ki - dke * ke
                rowc = lax.broadcasted_iota(jnp.int32, (L, DQK), 0)
                dbc = dbc + jnp.where(rowc == L - 1, dbtot, 0.0)
                triu = lax.broadcasted_iota(jnp.int32, (L, L), 1) >= lax.broadcasted_iota(jnp.int32, (L, L), 0)
                dbcs.append(_dot_exact(triu.astype(F32), dbc))
                dp_ref[:, sq] = (dq * QSCALE).astype(dp_ref.dtype)
                dp_ref[:, 512 + h * DQK:512 + (h + 1) * DQK] = dk.astype(dp_ref.dtype)
                dp_ref[:, 1024 + h * DV:1024 + (h + 1) * DV] = dv.astype(dp_ref.dtype)
            dza = jnp.where(valid, jnp.concatenate(dbcs, axis=1) * (1.0 / TAU) * _sigmoid(-za), 0.0)
            dga_ref[...] = _dot(dza, a2_ref[...], 1, 1)
            da2_ref[...] += _dot(sm, dza, 0, 0)
            db2_ref[...] += jnp.sum(dza, axis=0, keepdims=True)

        return init, zero, compute


def mix_fwd(qk, proj, gl, m_head_g, a2p, b2, g_head_g):
    R = qk.shape[0]
    NC = R // LM
    G = LM // L

    def body(qk_ref, mv_ref, gl_ref, mo_ref, mhg_ref, gq_ref, gk_ref, gv_ref, gr_ref, sm_ref, a2_ref, b2_ref, ghg_ref,
             hm_ref, ym_ref, cs_ref, nm_ref, hgl_ref, yg_ref, ss_ref, c_s, nm_s, s_s):
        c = pl.program_id(0)
        g_init, g_zero, g_compute = _gla_fwd_parts(c, gq_ref, gk_ref, gv_ref, gr_ref, sm_ref, a2_ref, b2_ref, ghg_ref,
                                                   hgl_ref, yg_ref, ss_ref, s_s)

        @pl.when(c <= CH0)
        def _():
            c_s[...] = jnp.zeros_like(c_s)
            nm_s[...] = jnp.zeros_like(nm_s)
            g_init()

        @pl.when(c < CH0)
        def _():
            hm_ref[...] = jnp.zeros_like(hm_ref)
            ym_ref[...] = jnp.zeros_like(ym_ref)
            cs_ref[...] = jnp.zeros_like(cs_ref)
            nm_ref[...] = jnp.zeros_like(nm_ref)
            g_zero()

        @pl.when(c >= CH0)
        def _():
            glv = gl_ref[...]
            assert LM // L == NH
            for h in range(NH):
                g_compute(h)
                _mlstm_head_fwd(h, glv, qk_ref, mv_ref, mo_ref, mhg_ref, hm_ref, ym_ref, cs_ref, nm_ref, c_s, nm_s)

    st_m = pl.BlockSpec((1, NH, DV, DQK), lambda c: (c, 0, 0, 0))
    st_n = pl.BlockSpec((1, NH, 8, DQK), lambda c: (c, 0, 0, 0))
    st_g = pl.BlockSpec((G, NH, DQK, DV), lambda c: (c, 0, 0, 0))
    return pl.pallas_call(
        body, name="mix_fwd", grid=(NC,),
        in_specs=[_rb(LM, 1024, 0), _rb(LM, 1024, MV0 // 1024), _rb(LM, 128, 0), _rb(LM, 1024, MO0 // 1024),
                  _const((1, 1024)),
                  _rb(LM, 512, GQ0 // 512), _rb(LM, 512, GK0 // 512), _rb(LM, 1024, GV0 // 1024),
                  _rb(LM, 1024, GR0 // 1024), _rb(LM, 128, SM0 // 128), _const((128, 512)), _const((1, 512)),
                  _const((1, 1024))],
        out_specs=[_rb(LM, 1024, 0), _rb(LM, 1024, 0), st_m, st_n, _rb(LM, 1024, 0), _rb(LM, 1024, 0), st_g],
        out_shape=[jax.ShapeDtypeStruct((R, 1024), F32), jax.ShapeDtypeStruct((R, 1024), MXU),
                   jax.ShapeDtypeStruct((NC, NH, DV, DQK), F32), jax.ShapeDtypeStruct((NC, NH, 8, DQK), F32),
                   jax.ShapeDtypeStruct((R, 1024), F32), jax.ShapeDtypeStruct((R, 1024), MXU),
                   jax.ShapeDtypeStruct((G * NC, NH, DQK, DV), F32)],
        scratch_shapes=[pltpu.VMEM((NH, DV, DQK), F32), pltpu.VMEM((NH, 8, DQK), F32),
                        pltpu.VMEM((NH, DQK, DV), F32)],
        compiler_params=_cp(("arbitrary",)),
    )(qk, proj, gl, proj, m_head_g, proj, proj, proj, proj, proj, a2p, b2, g_head_g)


def mix_bwd(dym, hm, qk, cpre, proj, gl, m_head_g, cs, nm, dyg, hgl, a2p, b2, g_head_g, ss, dproj):
    R = qk.shape[0]
    NC = R // LM
    rev = lambda c: NC - 1 - c
    GW = GR0 + 1024 - GQ0

    def body(dym_ref, hm_ref, qk_ref, cp_ref, mv_ref, gl_ref, mo_ref, mhg_ref, cs_ref, nm_ref,
             dyg_ref, ho_ref, gq_ref, gk_ref, gv_ref, gr_ref, sm_ref, a2_ref, b2_ref, ghg_ref, ss_ref, dp_in,
             dp_ref, dc_ref, dgl_ref, dmhg_ref, dga_ref, da2_ref, db2_ref, dghg_ref, dc_s, dn_s, ds_s):
        del dp_in
        step = pl.program_id(0)
        c = NC - 1 - step
        m_init, m_zero, m_compute = _mlstm_bwd_parts(
            dym_ref, hm_ref, qk_ref, cp_ref, mv_ref, gl_ref, mo_ref, mhg_ref, cs_ref, nm_ref,
            dp_ref.at[:, 0:GQ0], dc_ref, dgl_ref, dmhg_ref, dc_s, dn_s)
        g_init, g_zero, g_compute = _gla_bwd_parts(
            c, dyg_ref, ho_ref, gq_ref, gk_ref, gv_ref, gr_ref, sm_ref, a2_ref, b2_ref, ghg_ref, ss_ref,
            dp_ref.at[:, GQ0:GQ0 + GW], dga_ref, da2_ref, db2_ref, dghg_ref, ds_s)

        @pl.when(step == 0)
        def _():
            m_init()
            g_init()

        @pl.when(c < CH0)
        def _():
            m_zero()
            g_zero()

        @pl.when(c >= CH0)
        def _():
            assert LM // L == NH
            m_compute(lambda h: g_compute(NH - 1 - h))

    def rows(w, cb):
        return pl.BlockSpec((LM, w), lambda c: (rev(c), cb))

    return pl.pallas_call(
        body, name="mix_bwd", grid=(NC,),
        in_specs=[rows(1024, 0), rows(1024, 0), rows(1024, 0), rows(1024, 0), rows(1024, MV0 // 1024), rows(128, 0),
                  rows(1024, MO0 // 1024), _const((1, 1024)),
                  pl.BlockSpec((1, NH, DV, DQK), lambda c: (rev(c), 0, 0, 0)),
                  pl.BlockSpec((1, NH, 8, DQK), lambda c: (rev(c), 0, 0, 0)),
                  rows(1024, 0), rows(1024, 0), rows(512, GQ0 // 512), rows(512, GK0 // 512),
                  rows(1024, GV0 // 1024), rows(1024, GR0 // 1024), rows(128, SM0 // 128),
                  _const((128, 512)), _const((1, 512)), _const((1, 1024)),
                  pl.BlockSpec((LM // L, NH, DQK, DV), lambda c: (rev(c), 0, 0, 0)),
                  pl.BlockSpec(memory_space=pl.ANY)],
        out_specs=[rows(GQ0 + GW, 0), rows(1024, 0), rows(128, 0), _const((1, 1024)),
                   rows(128, 0), _const((128, 512)), _const((1, 512)), _const((1, 1024))],
        out_shape=[jax.ShapeDtypeStruct((R, NP), MXU), jax.ShapeDtypeStruct((R, 1024), F32),
                   jax.ShapeDtypeStruct((R, 128), F32), jax.ShapeDtypeStruct((1, 1024), F32),
                   jax.ShapeDtypeStruct((R, 128), F32), jax.ShapeDtypeStruct((128, 512), F32),
                   jax.ShapeDtypeStruct((1, 512), F32), jax.ShapeDtypeStruct((1, 1024), F32)],
        scratch_shapes=[pltpu.VMEM((NH, DV, DQK), F32), pltpu.VMEM((NH, 8, DQK), F32),
                        pltpu.VMEM((NH, DQK, DV), F32)],
        input_output_aliases={21: 0},
        compiler_params=_cp(("arbitrary",)),
    )(dym, hm, qk, cpre, proj, gl, proj, m_head_g, cs, nm, dyg, hgl, proj, proj, proj, proj, proj, a2p, b2,
      g_head_g, ss, dproj)


def _to_row128(col):
    r = lax.broadcasted_iota(jnp.int32, (DQK, DQK), 0)
    c = lax.broadcasted_iota(jnp.int32, (DQK, DQK), 1)
    return jnp.sum(jnp.where(r == c, col, 0.0), axis=0, keepdims=True)


def local_step(x, target, meta, norm1_g, wp, conv_w, conv_b, m_gate_b, g_a2, g_a2_b, m_head_g, g_head_g,
               norm2_g, final_g, late_weights, send_early, send_wp, first_order=None):
    seq = x.shape[0]
    assert seq % TM == 0
    gb_row = jnp.zeros((1, 128), F32).at[0, 0:8].set(m_gate_b.reshape(8))
    a2p = jnp.zeros((128, 512), F32).at[8:8 + RANK].set(g_a2)
    mhg = m_head_g.reshape(1, 1024)
    ghg = g_head_g.reshape(1, 1024)

    xn = rms_fwd_input(x, meta, norm1_g, "rms1_fwd")
    proj = matmul(xn, wp, "nn", "proj_fwd", tm=1536, order=first_order)
    cpre, qk, gl = prep_fwd(proj, conv_w, conv_b, gb_row)
    hm, ym, cs, nm, hgl, yg, ss = mix_fwd(qk, proj, gl, mhg, a2p, g_a2_b, ghg)
    w_bm, w_bg, w_out, w_gu, w_down = late_weights(ym)
    bm, bg, merged = merge_fwd(ym, yg, w_bm, w_bg, proj)
    h1, hn = out_proj_norm(merged, w_out, x, meta, norm2_g)
    au, ff = ff_in_fwd(hn, w_gu)
    dh2, dh2b, loss, d_final_g = ff_down_loss(ff, w_down, h1, final_g.reshape(1, D), target)

    d_w_down = matmul(ff, dh2b, "tn", "ff_down_wgrad", tm=1408, tk=1536)
    dau = ff_down_dgrad(dh2b, w_down, au)
    d_w_gu = matmul(hn, dau, "tn", "ff_in_wgrad", tm=1024, tn=1408, tk=2816)
    dh1, dh1b, d_norm2_g = dgrad_rms_bwd(dau, w_gu, h1, norm2_g, dh2, "ff_in_dgrad", 512, 2 * DFF)

    d_w_out = matmul(merged, dh1b, "tn", "out_wgrad", tm=1024, tk=1536)
    dbm, dbg, dproj = merge_bwd(dh1b, w_out, bm, bg, proj)
    d_w_bm = matmul(ym, dbm, "tn", "branch_m_wgrad", tm=1024, tk=2816)
    d_w_bg = matmul(yg, dbg, "tn", "branch_g_wgrad", tm=1024, tk=2816)
    token = send_early(dict(w_branch_m=d_w_bm, w_branch_g=d_w_bg, w_out=d_w_out, w_gu=d_w_gu, w_ff_down=d_w_down))
    dym = matmul(dbm, w_bm, "nt", "branch_m_dgrad", order=token)
    dyg = matmul(dbg, w_bg, "nt", "branch_g_dgrad")
    dproj, dc, dgl, d_mhg, dga, d_a2p, d_a2b, d_ghg = mix_bwd(
        dym, hm, qk, cpre, proj, gl, mhg, cs, nm, dyg, hgl, a2p, g_a2_b, ghg, ss, dproj)
    dproj, d_conv = conv_bwd(dc, proj, conv_w, dproj)
    dproj, d_gb = small_bwd(dgl, dga, proj, gb_row, dproj)
    d_wp = matmul(xn, dproj, "tn", "proj_wgrad", tm=1024, tn=1664, tk=2816)
    token = send_wp(d_wp)
    dxn = matmul(dproj, wp, "nt", "proj_dgrad", tm=256, tk=NP, order=token)
    grad_x, d_meta, d_norm1_g = rms_bwd_input(dxn, x, meta, norm1_g, dh1, "rms1_bwd")

    grads = dict(
        meta_tokens=d_meta, norm1_g=d_norm1_g, conv_w=d_conv[0:4], conv_b=d_conv[4:5], m_gate_b=d_gb[0, 0:8].reshape(1, 2, 4),
        g_a2=d_a2p[8:8 + RANK], g_a2_b=d_a2b, m_head_g=d_mhg.reshape(NH, DV), g_head_g=d_ghg.reshape(NH, DV),
        norm2_g=d_norm2_g, final_g=d_final_g)
    return loss, grad_x, grads


_SEGS = [(0, 1024, QK0), (1024, 2048, MV0), (2048, 2056, SM0), (2056, 3080, MO0), (3080, 5128, GQ0),
         (5128, 5144, SM0 + 8), (5144, 6168, GR0), (6168, 8216, GM0)]
SHARD_W = NPROJ // NDEV


def regroup_cols(w8):
    parts = []
    for lo, hi, _ in sorted(_SEGS, key=lambda s: s[2]):
        while lo < hi:
            j = lo // SHARD_W
            end = min(hi, (j + 1) * SHARD_W)
            parts.append(w8[j, :, lo - j * SHARD_W:end - j * SHARD_W])
            lo = end
    parts.append(jnp.zeros((w8.shape[1], NP - NPROJ), w8.dtype))
    return jnp.concatenate(parts, axis=1)


def ungroup_cols(g):
    blocks = []
    for j in range(NDEV):
        lo, hi = j * SHARD_W, (j + 1) * SHARD_W
        parts = []
        for s_lo, s_hi, s_at in _SEGS:
            a, b = max(lo, s_lo), min(hi, s_hi)
            if a < b:
                parts.append(g[:, s_at + a - s_lo:s_at + b - s_lo])
        blocks.append(jnp.concatenate(parts, axis=1))
    return jnp.stack(blocks)


def col_blocks(g):
    r, c8 = g.shape
    return jnp.transpose(g.reshape(r, NDEV, c8 // NDEV), (1, 0, 2))


def from_col_blocks(g8):
    n, r, c = g8.shape
    return jnp.transpose(g8, (1, 0, 2)).reshape(r, n * c)


_MESHID = pl.DeviceIdType.MESH
_RELS = [(0, 0, 1), (1, 0, 0), (0, 1, 0), (1, 1, 0), (1, 0, 1), (0, 1, 1), (1, 1, 1)]


def _flip(v, bit):
    return 1 - v if bit else v


def all_gather(arrs, name):
    n = len(arrs)

    def body(*refs):
        ins, outs = refs[:n], refs[n:2 * n]
        send_sems, recv_sems, local_sems = refs[2 * n:]
        x, y, c = lax.axis_index("x"), lax.axis_index("y"), lax.axis_index("c")
        me, sibling = (x, y, c), (x, y, 1 - c)
        chips = [(1 - x, y), (x, 1 - y), (1 - x, 1 - y)]

        def slot(p):
            return 4 * p[0] + 2 * p[1] + p[2]

        def copy(a, k, block, to, src=None):
            dst = outs[a].at[slot(block)]
            return pltpu.make_async_remote_copy(
                src_ref=dst if src is None else src, dst_ref=dst,
                send_sem=send_sems.at[a, k], recv_sem=recv_sems.at[a, k],
                device_id=to, device_id_type=_MESHID)

        mine = [pltpu.make_async_copy(ins[a], outs[a].at[slot(me)], local_sems.at[a]) for a in range(n)]
        for cp in mine:
            cp.start()
        first = []
        for a in range(n):
            first.append(copy(a, 0, me, sibling, src=ins[a]))
            first += [copy(a, 1 + j, me, (*chip, c), src=ins[a]) for j, chip in enumerate(chips)]
        for cp in first:
            cp.start()
        passed = []
        for j, chip in enumerate(chips):
            for a in range(n):
                copy(a, 1 + j, (*chip, c), me).wait_recv()
                fwd = copy(a, 4 + j, (*chip, c), sibling)
                fwd.start()
                passed.append(fwd)
        for a in range(n):
            copy(a, 0, sibling, me).wait_recv()
            for j, chip in enumerate(chips):
                copy(a, 4 + j, (*chip, 1 - c), me).wait_recv()
        for cp in first + passed:
            cp.wait_send()
        for cp in mine:
            cp.wait()

    anyspec = pl.BlockSpec(memory_space=pl.ANY)
    return pl.pallas_call(
        body, name=name,
        in_specs=[anyspec] * n, out_specs=[anyspec] * n,
        out_shape=[jax.ShapeDtypeStruct((NDEV,) + a.shape, a.dtype) for a in arrs],
        scratch_shapes=[pltpu.SemaphoreType.DMA((n, 7)), pltpu.SemaphoreType.DMA((n, 7)),
                        pltpu.SemaphoreType.DMA((n,))],
    )(*arrs)


def exchange(blocks, rep, name):
    n = len(blocks)

    def body(*refs):
        b_refs, r_ref = refs[:n], refs[n]
        ob_refs, or_ref = refs[n + 1:2 * n + 1], refs[2 * n + 1]
        send_sems, recv_sems, local_sems = refs[2 * n + 2:]
        x, y, c = lax.axis_index("x"), lax.axis_index("y"), lax.axis_index("c")
        me = 4 * x + 2 * y + c

        def pairs(src_slot, dst_slot):
            return [(b_refs[a].at[src_slot], ob_refs[a].at[dst_slot]) for a in range(n)] + [(r_ref, or_ref.at[dst_slot])]

        loc = [pltpu.make_async_copy(s, d, local_sems.at[a]) for a, (s, d) in enumerate(pairs(me, me))]
        for cp in loc:
            cp.start()
        sends = []
        for k, (fx, fy, fc) in enumerate(_RELS):
            peer = (_flip(x, fx), _flip(y, fy), _flip(c, fc))
            pid = 4 * peer[0] + 2 * peer[1] + peer[2]
            for a, (s, d) in enumerate(pairs(pid, me)):
                sends.append(pltpu.make_async_remote_copy(
                    src_ref=s, dst_ref=d, send_sem=send_sems.at[a, k], recv_sem=recv_sems.at[a, k],
                    device_id=peer, device_id_type=_MESHID))
        for cp in sends:
            cp.start()
        for k, (fx, fy, fc) in enumerate(_RELS):
            peer = (_flip(x, fx), _flip(y, fy), _flip(c, fc))
            pid = 4 * peer[0] + 2 * peer[1] + peer[2]
            for a, (s, d) in enumerate(pairs(pid, pid)):
                pltpu.make_async_remote_copy(
                    src_ref=s, dst_ref=d, send_sem=send_sems.at[a, k], recv_sem=recv_sems.at[a, k],
                    device_id=peer, device_id_type=_MESHID).wait_recv()
        for cp in sends:
            cp.wait_send()
        for cp in loc:
            cp.wait()

    anyspec = pl.BlockSpec(memory_space=pl.ANY)
    return pl.pallas_call(
        body, name=name,
        in_specs=[anyspec] * (n + 1), out_specs=[anyspec] * (n + 1),
        out_shape=[jax.ShapeDtypeStruct(b.shape, b.dtype) for b in blocks]
        + [jax.ShapeDtypeStruct((NDEV,) + rep.shape, rep.dtype)],
        scratch_shapes=[pltpu.SemaphoreType.DMA((n + 1, 7)), pltpu.SemaphoreType.DMA((n + 1, 7)),
                        pltpu.SemaphoreType.DMA((n + 1,))],
    )(*blocks, rep)


_HBM = pl.BlockSpec(memory_space=pltpu.HBM)
_SEM = pl.BlockSpec(memory_space=pltpu.SEMAPHORE)
_EFFECT = pltpu.SideEffectType.DATAFLOW_SIDE_EFFECTING


def _peer_ids():
    x, y, c = lax.axis_index("x"), lax.axis_index("y"), lax.axis_index("c")
    peers = []
    for fx, fy, fc in _RELS:
        p = (_flip(x, fx), _flip(y, fy), _flip(c, fc))
        peers.append((p, 4 * p[0] + 2 * p[1] + p[2]))
    return 4 * x + 2 * y + c, peers


def _split_copy(src, land, a, k, peer, src_slot, dst_slot, send_sems, recv_sems):
    return pltpu.make_async_remote_copy(
        src_ref=src if src_slot is None else src.at[src_slot], dst_ref=land.at[dst_slot],
        send_sem=send_sems.at[7 * a + k], recv_sem=recv_sems.at[7 * a + k], device_id=peer, device_id_type=_MESHID)


def _own_copy(src, land, a, n, me, per_peer, send_sems):
    return pltpu.make_async_copy(src.at[me] if per_peer else src, land.at[me], send_sems.at[7 * n + a])


def send_start(srcs, per_peer, order, name):
    n = len(srcs)
    lands = [lax.empty((NDEV,) + (s.shape[1:] if per_peer else s.shape), s.dtype) for s in srcs]

    def body(*refs):
        src_refs, land_refs = refs[1:1 + n], refs[1 + n:1 + 2 * n]
        send_sems, recv_sems = refs[1 + 2 * n], refs[2 + 2 * n]
        token = refs[3 + 4 * n]
        me, peers = _peer_ids()
        for a in range(n):
            _own_copy(src_refs[a], land_refs[a], a, n, me, per_peer, send_sems).start()
        for a in range(n):
            for k, (peer, pid) in enumerate(peers):
                _split_copy(src_refs[a], land_refs[a], a, k, peer, pid if per_peer else None, me,
                            send_sems, recv_sems).start()
        token[...] = jnp.zeros_like(token)

    outs = pl.pallas_call(
        body, name=name,
        in_specs=[pl.BlockSpec(memory_space=pl.ANY)] + [_HBM] * (2 * n),
        out_shape=(pltpu.SemaphoreType.DMA((8 * n,)), pltpu.SemaphoreType.DMA((7 * n,)),
                   *[pltpu.HBM(s.shape, s.dtype) for s in srcs], *[pltpu.HBM(l.shape, l.dtype) for l in lands],
                   jax.ShapeDtypeStruct((8, 128), F32)),
        out_specs=(_SEM, _SEM, *[_HBM] * (2 * n), pl.BlockSpec(memory_space=pltpu.VMEM)),
        input_output_aliases={1 + i: 2 + i for i in range(2 * n)},
        compiler_params=pltpu.CompilerParams(has_side_effects=_EFFECT),
    )(order, *[pltpu.with_memory_space_constraint(s, pltpu.HBM) for s in srcs],
      *[pltpu.with_memory_space_constraint(l, pltpu.HBM) for l in lands])
    return (n, per_peer, outs[0], outs[1], outs[2:2 + n], outs[2 + n:2 + 2 * n]), outs[2 + 2 * n]


def send_wait(handle, after, name):
    n, per_peer, send_sems, recv_sems, src_thru, land_thru = handle

    def body(*refs):
        src_refs, land_refs = refs[:n], refs[n:2 * n]
        s_sems, r_sems = refs[2 * n], refs[2 * n + 1]
        me, peers = _peer_ids()
        for a in range(n):
            _own_copy(src_refs[a], land_refs[a], a, n, me, per_peer, s_sems).wait()
            for k, (peer, pid) in enumerate(peers):
                cp = _split_copy(src_refs[a], land_refs[a], a, k, peer, pid if per_peer else None, pid, s_sems, r_sems)
                cp.wait_send()
                cp.wait_recv()

    outs = pl.pallas_call(
        body, name=name,
        in_specs=[_HBM] * (2 * n) + [_SEM, _SEM, pl.BlockSpec(memory_space=pl.ANY)],
        out_shape=tuple(pltpu.HBM(t.shape, t.dtype) for t in (*src_thru, *land_thru)),
        out_specs=tuple([_HBM] * (2 * n)),
        input_output_aliases={i: i for i in range(2 * n)},
        compiler_params=pltpu.CompilerParams(has_side_effects=_EFFECT),
    )(*src_thru, *land_thru, send_sems, recv_sems, after)
    return list(outs[n:2 * n])


def adamw(parts, w, m, v, name, tr):
    npart, r, c = parts.shape
    c1 = 1.0 - ADAM_B1 ** ADAM_STEP
    c2 = 1.0 - ADAM_B2 ** ADAM_STEP

    def body(p_ref, w_ref, m_ref, v_ref, g_ref, d_ref, nm_ref, nv_ref):
        g = p_ref[0].astype(F32)
        for j in range(1, npart):
            g = g + p_ref[j].astype(F32)
        mn = ADAM_B1 * m_ref[...] + (1.0 - ADAM_B1) * g
        vn = ADAM_B2 * v_ref[...] + (1.0 - ADAM_B2) * (g * g)
        g_ref[...] = g
        nm_ref[...] = mn
        nv_ref[...] = vn
        d_ref[...] = -ADAM_LR * ((mn / c1) / (jnp.sqrt(vn / c2) + ADAM_EPS) + ADAM_WD * w_ref[...])

    spec = _rb(tr, c, 0)
    return pl.pallas_call(
        body, name=name, grid=(r // tr,),
        in_specs=[pl.BlockSpec((npart, tr, c), lambda i: (0, i, 0)), spec, spec, spec],
        out_specs=[spec] * 4, out_shape=[jax.ShapeDtypeStruct((r, c), F32)] * 4,
        compiler_params=_cp(("parallel",)),
    )(parts, w, m, v)


def sum_parts(parts, name, tc):
    npart, r, c = parts.shape

    def body(p_ref, o_ref):
        g = p_ref[0].astype(F32)
        for j in range(1, npart):
            g = g + p_ref[j].astype(F32)
        o_ref[...] = g

    return pl.pallas_call(
        body, name=name, grid=(c // tc,),
        in_specs=[pl.BlockSpec((npart, r, tc), lambda i: (0, 0, i))],
        out_specs=pl.BlockSpec((r, tc), lambda i: (0, i)),
        out_shape=jax.ShapeDtypeStruct((r, c), F32),
        compiler_params=_cp(("parallel",)),
    )(parts)


TINY = [("meta_tokens", (16, 1024)), ("conv_w", (4, 1024)), ("g_a2", (16, 512)), ("m_head_g", (4, 256)),
        ("g_head_g", (4, 256))]
REPL = [("norm1_g", (1, 1024)), ("conv_b", (1, 1024)), ("m_gate_b", (1, 2, 4)), ("g_a2_b", (1, 512)),
        ("norm2_g", (1, 1024)), ("final_g", (1024,))]
TINY_SIZE = 16 * 1024 + 4 * 1024 + 16 * 512 + 2 * 4 * 256
REPL_SIZE = 1024 + 1024 + 8 + 512 + 1024 + 1024
ROWS_GATHER = 8
ROWS_REP = 40
ROWS_OWN = 16


def pack_rows(vecs, rows):
    flat = jnp.concatenate([v.reshape(-1) for v in vecs])
    return jnp.pad(flat, (0, rows * 1024 - flat.shape[0])).reshape(rows, 1024)


def unpack_rows(packed, shapes):
    flat = packed.reshape(-1)
    out, off = [], 0
    for s in shapes:
        n = 1
        for d in s:
            n *= d
        out.append(flat[off:off + n].reshape(s))
        off += n
    return out


def kernel(x, meta_tokens, norm1_g, w_in, conv_w, conv_b, m_gate_b, g_a2, g_a2_b, m_head_g, g_head_g, w_branch_m, w_branch_g, w_out, norm2_g, w_ff_gate, w_ff_up, w_ff_down, final_g, loss_target, m_meta_tokens, m_norm1_g, m_w_in, m_conv_w, m_conv_b, m_m_gate_b, m_g_a2, m_g_a2_b, m_m_head_g, m_g_head_g, m_w_branch_m, m_w_branch_g, m_w_out, m_norm2_g, m_w_ff_gate, m_w_ff_up, m_w_ff_down, m_final_g, v_meta_tokens, v_norm1_g, v_w_in, v_conv_w, v_conv_b, v_m_gate_b, v_g_a2, v_g_a2_b, v_m_head_g, v_g_head_g, v_w_branch_m, v_w_branch_g, v_w_out, v_norm2_g, v_w_ff_gate, v_w_ff_up, v_w_ff_down, v_final_g):
    w_sh = dict(meta_tokens=meta_tokens, w_in=w_in[0], conv_w=conv_w[0], g_a2=g_a2[0], m_head_g=m_head_g[0],
                g_head_g=g_head_g[0], w_branch_m=w_branch_m[0], w_branch_g=w_branch_g[0], w_out=w_out[0],
                w_ff_gate=w_ff_gate[0], w_ff_up=w_ff_up[0], w_ff_down=w_ff_down[0])
    m_sh = dict(meta_tokens=m_meta_tokens, w_in=m_w_in[0], conv_w=m_conv_w[0], g_a2=m_g_a2[0],
                m_head_g=m_m_head_g[0], g_head_g=m_g_head_g[0], w_branch_m=m_w_branch_m[0],
                w_branch_g=m_w_branch_g[0], w_out=m_w_out[0], w_ff_gate=m_w_ff_gate[0], w_ff_up=m_w_ff_up[0],
                w_ff_down=m_w_ff_down[0])
    v_sh = dict(meta_tokens=v_meta_tokens, w_in=v_w_in[0], conv_w=v_conv_w[0], g_a2=v_g_a2[0],
                m_head_g=v_m_head_g[0], g_head_g=v_g_head_g[0], w_branch_m=v_w_branch_m[0],
                w_branch_g=v_w_branch_g[0], w_out=v_w_out[0], w_ff_gate=v_w_ff_gate[0], w_ff_up=v_w_ff_up[0],
                w_ff_down=v_w_ff_down[0])
    w_rep = dict(norm1_g=norm1_g, conv_b=conv_b, m_gate_b=m_gate_b, g_a2_b=g_a2_b, norm2_g=norm2_g, final_g=final_g)
    m_rep = dict(norm1_g=m_norm1_g, conv_b=m_conv_b, m_gate_b=m_m_gate_b, g_a2_b=m_g_a2_b, norm2_g=m_norm2_g,
                 final_g=m_final_g)
    v_rep = dict(norm1_g=v_norm1_g, conv_b=v_conv_b, m_gate_b=v_m_gate_b, g_a2_b=v_g_a2_b, norm2_g=v_norm2_g,
                 final_g=v_final_g)
    dev = 4 * lax.axis_index("x") + 2 * lax.axis_index("y") + lax.axis_index("c")
    tiny_names = [n for n, _ in TINY]
    repl_names = [n for n, _ in REPL]
    tiny_shard_shapes = [(s[0], s[1] // NDEV) for _, s in TINY]

    in8, tiny8 = all_gather([w_sh["w_in"].astype(MXU), pack_rows([w_sh[n] for n in tiny_names], ROWS_GATHER)],
                            "param_all_gather")
    late_names = ["w_branch_m", "w_branch_g", "w_out", "w_ff_gate", "w_ff_up", "w_ff_down"]
    late, first_order = send_start([w_sh[n].astype(MXU) for n in late_names], False, tiny8, "late_weights_start")
    wp = regroup_cols(in8)
    handles = {}

    def late_weights(after):
        bm8, bg8, out8, ffg8, ffu8, ffd8 = send_wait(late, after, "late_weights_wait")
        w_gu = interleave_gu(from_col_blocks(ffg8), from_col_blocks(ffu8))
        return bm8.reshape(D, D), bg8.reshape(D, D), out8.reshape(D, D), w_gu, ffd8.reshape(DFF, D)

    def send_early(g):
        d_gate, d_up = split_gu(g["w_gu"])
        blocks = [g["w_branch_m"].reshape(NDEV, D // NDEV, D).astype(WIRE),
                  g["w_branch_g"].reshape(NDEV, D // NDEV, D).astype(WIRE),
                  g["w_out"].reshape(NDEV, D // NDEV, D).astype(WIRE),
                  col_blocks(d_gate).astype(WIRE), col_blocks(d_up).astype(WIRE),
                  g["w_ff_down"].reshape(NDEV, DFF // NDEV, D).astype(WIRE)]
        handles["early"], token = send_start(blocks, True, blocks[0], "early_grads_start")
        return token

    def send_wp(d_wp):
        blocks = [ungroup_cols(d_wp).astype(WIRE)]
        handles["wp"], token = send_start(blocks, True, blocks[0], "proj_grads_start")
        return token

    tiny_full = {}
    for j in range(NDEV):
        for name, blk in zip(tiny_names, unpack_rows(tiny8[j], tiny_shard_shapes)):
            tiny_full.setdefault(name, []).append(blk)
    tiny_full = {n: jnp.concatenate(v, axis=1) for n, v in tiny_full.items()}

    loss, grad_x, g = local_step(
        x[0], loss_target[0], tiny_full["meta_tokens"], norm1_g, wp, tiny_full["conv_w"], conv_b, m_gate_b[0],
        tiny_full["g_a2"], g_a2_b, tiny_full["m_head_g"], tiny_full["g_head_g"], norm2_g, final_g,
        late_weights, send_early, send_wp, first_order)

    rep = pack_rows([g[n] for n in tiny_names + repl_names] + [loss[0, 0:1]], ROWS_REP)
    (got_rep,) = exchange([], rep, "small_grad_exchange")
    got_early = send_wait(handles["early"], got_rep, "early_grads_wait")
    (got_wp,) = send_wait(handles["wp"], got_rep, "proj_grads_wait")

    result = {}

    def update(name, parts, tr):
        outs = adamw(parts, w_sh[name], m_sh[name], v_sh[name], "adamw_" + name, tr)
        for kind, arr in zip(("grad", "delta", "new_m", "new_v"), outs):
            result[kind, name] = arr[None]

    update("w_in", got_wp, 128)
    update("w_branch_m", got_early[0], 128)
    update("w_branch_g", got_early[1], 128)
    update("w_out", got_early[2], 128)
    update("w_ff_gate", got_early[3], 256)
    update("w_ff_up", got_early[4], 256)
    update("w_ff_down", got_early[5], DFF // NDEV)

    rep_sum = sum_parts(got_rep, "sum_small", 1024)
    rep_g = unpack_rows(rep_sum, [s for _, s in TINY] + [s for _, s in REPL] + [(1,)])
    own_g = [lax.dynamic_slice_in_dim(gf, dev * ss[1], ss[1], axis=1) for gf, ss in zip(rep_g, tiny_shard_shapes)]
    own_g += rep_g[len(TINY):len(TINY) + len(REPL)]
    w_all = {**w_sh, **w_rep}
    m_all = {**m_sh, **m_rep}
    v_all = {**v_sh, **v_rep}
    names = tiny_names + repl_names
    outs = adamw(pack_rows(own_g, ROWS_OWN)[None], pack_rows([w_all[n] for n in names], ROWS_OWN),
                 pack_rows([m_all[n] for n in names], ROWS_OWN), pack_rows([v_all[n] for n in names], ROWS_OWN),
                 "adamw_small", ROWS_OWN)
    shapes = tiny_shard_shapes + [s for _, s in REPL]
    for kind, packed in zip(("grad", "delta", "new_m", "new_v"), outs):
        for name, arr in zip(names, unpack_rows(packed, shapes)):
            result[kind, name] = arr[None] if name in tiny_names and name != "meta_tokens" else arr
    loss_total = rep_g[-1][0]
    order = ["meta_tokens", "norm1_g", "w_in", "conv_w", "conv_b", "m_gate_b", "g_a2", "g_a2_b", "m_head_g", "g_head_g",
             "w_branch_m", "w_branch_g", "w_out", "norm2_g", "w_ff_gate", "w_ff_up", "w_ff_down", "final_g"]
    return (loss_total, grad_x[None], *[result[kind, n] for kind in ("grad", "delta", "new_m", "new_v") for n in order])
```
